```python
import math
import jax, jax.numpy as jnp
from jax import lax
import numpy as np

D_MODEL = 1024
BATCH = 8
SEQ = 2048
DEPTH = 1
DEC_BATCH = 128
DEC_SEQ = 8
PAST_LEN = 16384
PAGE_SIZE = 128

D_MIX = D_MODEL
D_RG = D_MIX // 2
RG_BLOCKS = 8
RG_BW = D_RG // RG_BLOCKS
RG_C = 8.0
D_M = D_MIX - D_RG
M_HEADS = 4
M_DH = D_M // M_HEADS
M_CHUNK = 64
CONV_W = 4
N_IN = 2 * D_RG + 2 * D_M + 2 * M_HEADS
N_GROUPS = 4
EXPERTS_PER_GROUP = 8
N_EXPERTS = N_GROUPS * EXPERTS_PER_GROUP
TOP_K = 2
D_EXPERT = D_MODEL // 4
ALPHA = (2.0 * DEPTH) ** 0.25
BETA = (8.0 * DEPTH) ** -0.25
M_INIT = -1.0e4
LN_EPS = 1e-5

kernel_name = "hawk_mlstm_parallel_heads_hmoe_deepnorm_step"


def layer_norm(x, g, b):
    xf = x.astype(jnp.float32)
    mu = jnp.mean(xf, -1, keepdims=True)
    var = jnp.mean(jnp.square(xf - mu), -1, keepdims=True)
    return ((xf - mu) * lax.rsqrt(var + LN_EPS) * g + b).astype(x.dtype)


def causal_conv(x, buf, w, b):
    T = x.shape[1]
    xp = jnp.concatenate([buf.astype(x.dtype), x], axis=1)
    y = b + xp[:, 0:T] * w[0]
    for j in range(1, CONV_W):
        y = y + xp[:, j:j + T] * w[j]
    return y, xp[:, xp.shape[1] - (CONV_W - 1):]


def rg_lru(xc, h0, w_a, b_a, w_x, b_x, lam):
    f32 = jnp.float32
    B, T, _ = xc.shape
    xb = xc.reshape(B, T, RG_BLOCKS, RG_BW)
    r = jax.nn.sigmoid((jnp.einsum('btnd,nde->btne', xb, w_a).reshape(B, T, D_RG) + b_a).astype(f32))
    i = jax.nn.sigmoid((jnp.einsum('btnd,nde->btne', xb, w_x).reshape(B, T, D_RG) + b_x).astype(f32))
    log_a = -RG_C * r * jax.nn.softplus(-lam.astype(f32))
    a = jnp.exp(log_a)
    u = jnp.sqrt(-jnp.expm1(2.0 * log_a)) * i * xc.astype(f32)
    u = u.at[:, 0].add(a[:, 0] * h0.astype(f32))

    def combine(left, right):
        a_l, u_l = left
        a_r, u_r = right
        return a_l * a_r, a_r * u_l + u_r

    _, h = lax.associative_scan(combine, (a, u), axis=1)
    return h, h[:, -1]


def mlstm_chunked(q, k, v, il, fl, C0, n0, m0):
    f32 = jnp.float32
    B, T, H, Dk = q.shape
    L = T if T <= M_CHUNK else math.gcd(T, M_CHUNK)
    NC = T // L

    def to_chunks(a):
        a = a.astype(f32).reshape((B, NC, L, H) + a.shape[3:])
        return jnp.moveaxis(a, (1, 3), (0, 2))

    mask = jnp.tril(jnp.ones((L, L), dtype=bool))

    def step(carry, inp):
        C, n, m_prev = carry
        qc, kc, vc, ic, fc = inp
        bcum = jnp.cumsum(fc, axis=-1)
        dmat = jnp.where(mask, bcum[..., :, None] - bcum[..., None, :] + ic[..., None, :], -jnp.inf)
        m_inter = bcum + m_prev[..., None]
        m_t = jnp.maximum(m_inter, jnp.max(dmat, axis=-1))
        s = jnp.einsum('bhtd,bhsd->bhts', qc, kc) * jnp.exp(dmat - m_t[..., None])
        g = jnp.exp(m_inter - m_t)
        num = jnp.einsum('bhts,bhsv->bhtv', s, vc) + g[..., None] * jnp.einsum('bhtk,bhkv->bhtv', qc, C)
        den = jnp.sum(s, -1) + g * jnp.einsum('bhtk,bhk->bht', qc, n)
        h = num / jnp.maximum(jnp.abs(den), jnp.exp(-m_t))[..., None]
        m_end = m_t[..., -1]
        w_end = jnp.exp(bcum[..., -1:] - bcum + ic - m_end[..., None])
        g_end = jnp.exp(bcum[..., -1] + m_prev - m_end)
        C_new = g_end[..., None, None] * C + jnp.einsum('bhs,bhsk,bhsv->bhkv', w_end, kc, vc)
        n_new = g_end[..., None] * n + jnp.einsum('bhs,bhsk->bhk', w_end, kc)
        return (C_new, n_new, m_end), h

    carry0 = (C0.astype(f32), n0.astype(f32), m0.astype(f32))
    (C_f, n_f, m_f), hs = lax.scan(step, carry0, (to_chunks(q), to_chunks(k), to_chunks(v), to_chunks(il), to_chunks(fl)))
    h = jnp.moveaxis(hs, (0, 2), (1, 3)).reshape(B, T, H, Dk)
    return h, C_f, n_f, m_f


def hier_moe(x, w_group, b_group, w_expert, b_expert, w_gate, w_up, w_down):
    f32 = jnp.float32
    B, T, D = x.shape
    xf = x.reshape(B * T, D)
    g_logits = (xf @ w_group + b_group).astype(f32)
    g_prob = jax.nn.softmax(g_logits, axis=-1)
    g_idx = jnp.argmax(g_logits, axis=-1)
    p_g = jnp.take_along_axis(g_prob, g_idx[:, None], axis=-1)[:, 0]
    e_logits = (xf @ w_expert + b_expert).astype(f32).reshape(-1, N_GROUPS, EXPERTS_PER_GROUP)
    e_sel = jnp.take_along_axis(e_logits, g_idx[:, None, None], axis=1)[:, 0]
    top_v, top_i = lax.top_k(e_sel, TOP_K)
    top_p = jax.nn.softmax(top_v, axis=-1)
    ids = g_idx[:, None] * EXPERTS_PER_GROUP + top_i
    wts = p_g[:, None] * top_p
    gates = jnp.sum(jax.nn.one_hot(ids, N_EXPERTS, dtype=f32) * wts[..., None], axis=1)
    out = jnp.zeros(xf.shape, f32)
    for e in range(N_EXPERTS):
        h = jax.nn.silu(xf @ w_gate[e]) * (xf @ w_up[e])
        out = out + gates[:, e:e + 1] * (h @ w_down[e]).astype(f32)
    return out.reshape(B, T, D).astype(x.dtype)


def decoder_layer(x, rg_h, rg_buf, m_C, m_n, m_m, m_buf,
                  w_in, rg_conv_w, rg_conv_b, rg_w_a, rg_b_a, rg_w_x, rg_b_x, rg_lambda,
                  m_conv_w, m_conv_b, m_w_q, m_w_k, m_w_v, m_b_i, m_b_f, m_norm_g, m_skip,
                  w_out, ln1_g, ln1_b, ln2_g, ln2_b,
                  moe_w_group, moe_b_group, moe_w_expert, moe_b_expert, moe_w_gate, moe_w_up, moe_w_down):
    f32 = jnp.float32
    B, T, _ = x.shape
    proj = x @ w_in
    rg_x, rg_g, m_x, m_z, m_if = jnp.split(proj, [D_RG, 2 * D_RG, 2 * D_RG + D_M, 2 * D_RG + 2 * D_M], axis=-1)
    rg_xc, rg_buf_new = causal_conv(rg_x, rg_buf, rg_conv_w, rg_conv_b)
    h_rg, rg_h_new = rg_lru(rg_xc, rg_h, rg_w_a, rg_b_a, rg_w_x, rg_b_x, rg_lambda)
    y_rg = (h_rg * jax.nn.gelu(rg_g.astype(f32), approximate=True)).astype(x.dtype)
    m_xc, m_buf_new = causal_conv(m_x, m_buf, m_conv_w, m_conv_b)
    m_xa = jax.nn.silu(m_xc)
    q = jnp.einsum('bthd,hde->bthe', m_xa.reshape(B, T, M_HEADS, M_DH), m_w_q)
    k = jnp.einsum('bthd,hde->bthe', m_xa.reshape(B, T, M_HEADS, M_DH), m_w_k) * (M_DH ** -0.5)
    v = jnp.einsum('bthd,hde->bthe', m_x.reshape(B, T, M_HEADS, M_DH), m_w_v)
    il = (m_if[..., :M_HEADS] + m_b_i).astype(f32)
    fl = jax.nn.log_sigmoid((m_if[..., M_HEADS:] + m_b_f).astype(f32))
    h_m, C_new, n_new, m_new = mlstm_chunked(q, k, v, il, fl, m_C, m_n, m_m)
    mu = jnp.mean(h_m, -1, keepdims=True)
    var = jnp.mean(jnp.square(h_m - mu), -1, keepdims=True)
    hn = ((h_m - mu) * lax.rsqrt(var + LN_EPS)).reshape(B, T, D_M) * m_norm_g
    y_m = (jax.nn.sigmoid(m_z.astype(f32)) * (hn + m_skip * m_xa.astype(f32))).astype(x.dtype)
    mix = jnp.concatenate([y_rg, y_m], axis=-1) @ w_out
    x = layer_norm(ALPHA * x + mix, ln1_g, ln1_b)
    ffn = hier_moe(x, moe_w_group, moe_b_group, moe_w_expert, moe_b_expert, moe_w_gate, moe_w_up, moe_w_down)
    x = layer_norm(ALPHA * x + ffn, ln2_g, ln2_b)
    return x, (rg_h_new, rg_buf_new, C_new, n_new, m_new, m_buf_new)


def setup_inputs(seed: int = 0) -> dict:
    key = jax.random.key(seed)
    ks = iter(jax.random.split(key, 48))
    f32 = jnp.float32

    def nrm(shape, scale):
        return jax.random.normal(next(ks), shape, f32) * scale

    a0 = jax.random.uniform(next(ks), (DEPTH, D_RG), f32, minval=0.9, maxval=0.999)
    return {
        "x_prompt": nrm((BATCH, SEQ, D_MODEL), 1.0),
        "x_sample": nrm((DEC_BATCH, DEC_SEQ, D_MODEL), 1.0),
        "state_rg_h": nrm((DEPTH, DEC_BATCH, D_RG), 1.0),
        "state_rg_conv": nrm((DEPTH, DEC_BATCH, CONV_W - 1, D_RG), 1.0),
        "state_m_C": nrm((DEPTH, DEC_BATCH, M_HEADS, M_DH, M_DH), 1.0),
        "state_m_n": nrm((DEPTH, DEC_BATCH, M_HEADS, M_DH), 1.0),
        "state_m_m": nrm((DEPTH, DEC_BATCH, M_HEADS), 1.0),
        "state_m_conv": nrm((DEPTH, DEC_BATCH, CONV_W - 1, D_M), 1.0),
        "w_in": nrm((DEPTH, D_MODEL, N_IN), D_MODEL ** -0.5),
        "rg_conv_w": nrm((DEPTH, CONV_W, D_RG), CONV_W ** -0.5),
        "rg_conv_b": nrm((DEPTH, D_RG), 0.02),
        "rg_w_a": nrm((DEPTH, RG_BLOCKS, RG_BW, RG_BW), RG_BW ** -0.5),
        "rg_b_a": nrm((DEPTH, D_RG), 0.02),
        "rg_w_x": nrm((DEPTH, RG_BLOCKS, RG_BW, RG_BW), RG_BW ** -0.5),
        "rg_b_x": nrm((DEPTH, D_RG), 0.02),
        "rg_lambda": jnp.log(a0) - jnp.log1p(-a0),
        "m_conv_w": nrm((DEPTH, CONV_W, D_M), CONV_W ** -0.5),
        "m_conv_b": nrm((DEPTH, D_M), 0.02),
        "m_w_q": nrm((DEPTH, M_HEADS, M_DH, M_DH), M_DH ** -0.5),
        "m_w_k": nrm((DEPTH, M_HEADS, M_DH, M_DH), M_DH ** -0.5),
        "m_w_v": nrm((DEPTH, M_HEADS, M_DH, M_DH), M_DH ** -0.5),
        "m_b_i": nrm((DEPTH, M_HEADS), 0.1),
        "m_b_f": 3.0 + 3.0 * jax.random.uniform(next(ks), (DEPTH, M_HEADS), f32),
        "m_norm_g": 1.0 + nrm((DEPTH, D_M), 0.02),
        "m_skip": 1.0 + nrm((DEPTH, D_M), 0.02),
        "w_out": nrm((DEPTH, D_MIX, D_MODEL), BETA * D_MIX ** -0.5),
        "ln1_g": 1.0 + nrm((DEPTH, D_MODEL), 0.02),
        "ln1_b": nrm((DEPTH, D_MODEL), 0.02),
        "ln2_g": 1.0 + nrm((DEPTH, D_MODEL), 0.02),
        "ln2_b": nrm((DEPTH, D_MODEL), 0.02),
        "moe_w_group": nrm((DEPTH, D_MODEL, N_GROUPS), D_MODEL ** -0.5),
        "moe_b_group": nrm((DEPTH, N_GROUPS), 0.01),
        "moe_w_expert": nrm((DEPTH, D_MODEL, N_EXPERTS), D_MODEL ** -0.5),
        "moe_b_expert": nrm((DEPTH, N_EXPERTS), 0.01),
        "moe_w_gate": nrm((DEPTH, N_EXPERTS, D_MODEL, D_EXPERT), D_MODEL ** -0.5),
        "moe_w_up": nrm((DEPTH, N_EXPERTS, D_MODEL, D_EXPERT), BETA * D_MODEL ** -0.5),
        "moe_w_down": nrm((DEPTH, N_EXPERTS, D_EXPERT, D_MODEL), BETA * D_EXPERT ** -0.5),
    }


def reference(x_prompt, x_sample, state_rg_h, state_rg_conv, state_m_C, state_m_n, state_m_m, state_m_conv,
              w_in, rg_conv_w, rg_conv_b, rg_w_a, rg_b_a, rg_w_x, rg_b_x, rg_lambda,
              m_conv_w, m_conv_b, m_w_q, m_w_k, m_w_v, m_b_i, m_b_f, m_norm_g, m_skip,
              w_out, ln1_g, ln1_b, ln2_g, ln2_b,
              moe_w_group, moe_b_group, moe_w_expert, moe_b_expert, moe_w_gate, moe_w_up, moe_w_down):
    f32 = jnp.float32
    dt = x_prompt.dtype
    B = x_prompt.shape[0]
    xp, xs = x_prompt, x_sample
    p_new = [[] for _ in range(6)]
    s_new = [[] for _ in range(6)]
    for l in range(DEPTH):
        lp = (w_in[l], rg_conv_w[l], rg_conv_b[l], rg_w_a[l], rg_b_a[l], rg_w_x[l], rg_b_x[l], rg_lambda[l],
              m_conv_w[l], m_conv_b[l], m_w_q[l], m_w_k[l], m_w_v[l], m_b_i[l], m_b_f[l], m_norm_g[l], m_skip[l],
              w_out[l], ln1_g[l], ln1_b[l], ln2_g[l], ln2_b[l],
              moe_w_group[l], moe_b_group[l], moe_w_expert[l], moe_b_expert[l], moe_w_gate[l], moe_w_up[l], moe_w_down[l])
        xp, ps = decoder_layer(xp,
                               jnp.zeros((B, D_RG), f32), jnp.zeros((B, CONV_W - 1, D_RG), dt),
                               jnp.zeros((B, M_HEADS, M_DH, M_DH), f32), jnp.zeros((B, M_HEADS, M_DH), f32),
                               jnp.full((B, M_HEADS), M_INIT, f32), jnp.zeros((B, CONV_W - 1, D_M), dt),
                               *lp)
        xs, ss = decoder_layer(xs, state_rg_h[l], state_rg_conv[l], state_m_C[l], state_m_n[l], state_m_m[l],
                               state_m_conv[l], *lp)
        for j in range(6):
            p_new[j].append(ps[j].astype(dt))
            s_new[j].append(ss[j].astype(dt))
    p_rg_h, p_rg_conv, p_m_C, p_m_n, p_m_m, p_m_conv = [jnp.stack(a, axis=0) for a in p_new]
    s_rg_h, s_rg_conv, s_m_C, s_m_n, s_m_m, s_m_conv = [jnp.stack(a, axis=0) for a in s_new]
    return (xp, xs, p_rg_h, p_rg_conv, p_m_C, p_m_n, p_m_m, p_m_conv,
            s_rg_h, s_rg_conv, s_m_C, s_m_n, s_m_m, s_m_conv)
```

```python
import functools

import jax
import jax.numpy as jnp
from jax import lax
from jax.experimental import pallas as pl
from jax.experimental.pallas import tpu as pltpu

f32 = jnp.float32
bf16 = jnp.bfloat16

D_MODEL = 1024
D_RG = 512
RG_BLOCKS = 8
RG_C = 8.0
D_M = 512
M_HEADS = 4
M_DH = 128
CONV_W = 4
N_GROUPS = 4
EXPERTS_PER_GROUP = 8
N_EXPERTS = 32
D_EXPERT = 256
ALPHA = 2.0 ** 0.25
LN_EPS = 1e-5
M_INIT = -1.0e4

LANES = 128
SUBLANES = 8
TM = 256
N_PROJ = 2 * D_RG + 2 * D_M + LANES
RG_ROWS_PROMPT = 256
M_ROWS = 128
VMEM_LIMIT = 56 * 1024 * 1024


def _cparams(n_axes):
    return pltpu.CompilerParams(dimension_semantics=("arbitrary",) * n_axes,
                                vmem_limit_bytes=VMEM_LIMIT)


def _inproj_kernel(xp_ref, xs_ref, w_ref, o_ref, *, n_p):
    i = pl.program_id(0)

    def run(x_ref):
        o_ref[...] = jnp.dot(x_ref[...].astype(bf16), w_ref[...], preferred_element_type=f32)

    @pl.when(i < n_p)
    def _():
        run(xp_ref)

    @pl.when(i >= n_p)
    def _():
        run(xs_ref)


def _inproj(xp, xs, w):
    n_p, n_s = xp.shape[0] // TM, xs.shape[0] // TM
    return pl.pallas_call(
        functools.partial(_inproj_kernel, n_p=n_p),
        grid=(n_p + n_s,),
        in_specs=[
            pl.BlockSpec((TM, D_MODEL), lambda i: (jnp.minimum(i, n_p - 1), 0)),
            pl.BlockSpec((TM, D_MODEL), lambda i: (jnp.maximum(i - n_p, 0), 0)),
            pl.BlockSpec((D_MODEL, N_PROJ), lambda i: (0, 0)),
        ],
        out_specs=pl.BlockSpec((TM, N_PROJ), lambda i: (i, 0)),
        out_shape=jax.ShapeDtypeStruct(((n_p + n_s) * TM, N_PROJ), f32),
        compiler_params=_cparams(1),
        name="inproj",
    )(xp, xs, w)


def _causal_conv(xs_scr, x, cw_ref, cb_ref, S, L, C):
    xs_scr[:, SUBLANES:SUBLANES + L, :] = x.reshape(S, L, C)
    acc = cb_ref[...] + cw_ref[CONV_W - 1:CONV_W, :] * x
    for j in range(CONV_W - 1):
        lo = SUBLANES - (CONV_W - 1) + j
        acc = acc + cw_ref[j:j + 1, :] * xs_scr[:, lo:lo + L, :].reshape(S * L, C)
    return acc


def _softplus(x):
    return jnp.maximum(x, 0.0) + jnp.log1p(jnp.exp(-jnp.abs(x)))


def _rglru_kernel(x_ref, g_ref, buf_ref, h0_ref, cw_ref, cb_ref, wa_ref, ba_ref, wx_ref, bx_ref,
                  lam_ref, y_ref, hN_ref, cN_ref, xs_scr, *, S, L):
    R = S * L
    t = pl.program_id(1)

    @pl.when(t == 0)
    def _():
        xs_scr[:, 0:SUBLANES, :] = buf_ref[...]
        hN_ref[...] = h0_ref[...]

    x = x_ref[...]
    xc = _causal_conv(xs_scr, x, cw_ref, cb_ref, S, L, D_RG)
    xcb = xc.astype(bf16)
    r = jax.nn.sigmoid(jnp.dot(xcb, wa_ref[...], preferred_element_type=f32) + ba_ref[...])
    ig = jax.nn.sigmoid(jnp.dot(xcb, wx_ref[...], preferred_element_type=f32) + bx_ref[...])
    log_a = (-RG_C) * r * _softplus(-lam_ref[...])
    a = jnp.exp(log_a)
    th = jnp.tanh(log_a)
    u = jnp.sqrt(-2.0 * th / (1.0 - th)) * ig * xc

    tin = lax.broadcasted_iota(jnp.int32, (R, D_RG), 0) % L
    s = 1
    while s < L:
        ok = tin >= s
        a_sh = pltpu.roll(a, s, 0)
        u_sh = pltpu.roll(u, s, 0)
        u = jnp.where(ok, a * u_sh + u, u)
        a = jnp.where(ok, a * a_sh, a)
        s *= 2
    h_prev = jnp.broadcast_to(hN_ref[...], (S, L, D_RG)).reshape(R, D_RG)
    h = a * h_prev + u

    y_ref[...] = h * jax.nn.gelu(g_ref[...], approximate=True)
    hN_ref[...] = h.reshape(S, L, D_RG)[:, L - 1:L, :]
    xs_scr[:, 0:SUBLANES, :] = xs_scr[:, L:L + SUBLANES, :]
    cN_ref[...] = xs_scr[:, SUBLANES - (CONV_W - 1):SUBLANES, :]


def _rglru(proj, row0, B, T, S, L, buf8, h0, cw, cb, wa, ba, wx, bx, lam):
    R = S * L
    nb, nt = B // S, T // L
    blk0 = row0 // R
    row_map = lambda b, t: (blk0 + b * nt + t, 0)
    const2 = lambda b, t: (0, 0)
    return pl.pallas_call(
        functools.partial(_rglru_kernel, S=S, L=L),
        grid=(nb, nt),
        in_specs=[
            pl.BlockSpec((R, D_RG), row_map),
            pl.BlockSpec((R, D_RG), lambda b, t: (blk0 + b * nt + t, 1)),
            pl.BlockSpec((S, SUBLANES, D_RG), lambda b, t: (b, 0, 0)),
            pl.BlockSpec((S, 1, D_RG), lambda b, t: (b, 0, 0)),
            pl.BlockSpec((CONV_W, D_RG), const2),
            pl.BlockSpec((1, D_RG), const2),
            pl.BlockSpec((D_RG, D_RG), const2),
            pl.BlockSpec((1, D_RG), const2),
            pl.BlockSpec((D_RG, D_RG), const2),
            pl.BlockSpec((1, D_RG), const2),
            pl.BlockSpec((1, D_RG), const2),
        ],
        out_specs=[
            pl.BlockSpec((R, D_RG), lambda b, t: (b * nt + t, 0)),
            pl.BlockSpec((S, 1, D_RG), lambda b, t: (b, 0, 0)),
            pl.BlockSpec((S, CONV_W - 1, D_RG), lambda b, t: (b, 0, 0)),
        ],
        out_shape=[
            jax.ShapeDtypeStruct((B * T, D_RG), f32),
            jax.ShapeDtypeStruct((B, 1, D_RG), f32),
            jax.ShapeDtypeStruct((B, CONV_W - 1, D_RG), f32),
        ],
        scratch_shapes=[pltpu.VMEM((S, SUBLANES + L, D_RG), f32)],
        compiler_params=_cparams(2),
        name="rglru",
    )(proj, proj, buf8, h0, cw, cb, wa, ba, wx, bx, lam)


def _seg_scan(x, op, fill, tin, L, reverse=False):
    s = 1
    while s < L:
        if reverse:
            sh = pltpu.roll(x, LANES - s, 1)
            ok = tin < L - s
        else:
            sh = pltpu.roll(x, s, 1)
            ok = tin >= s
        x = op(x, jnp.where(ok, sh, fill))
        s *= 2
    return x


def _mlstm_kernel(x_ref, z_ref, gt_ref, buf_ref, c0_ref, n0_ref, m0_ref,
                  cw_ref, cb_ref, wq_ref, wk_ref, wv_ref, bif_ref, ng_ref, sk_ref,
                  y_ref, cN_ref, nN_ref, mN_ref, bN_ref, xs_scr, m_scr, *, S, L):
    R = S * L
    j = pl.program_id(1)

    @pl.when(j == 0)
    def _():
        xs_scr[:, 0:SUBLANES, :] = buf_ref[...]
        cN_ref[...] = c0_ref[...]
        nN_ref[...] = n0_ref[...]
        m_scr[...] = m0_ref[...]

    x = x_ref[...]
    xc = _causal_conv(xs_scr, x, cw_ref, cb_ref, S, L, D_M)
    xa = xc * jax.nn.sigmoid(xc)
    xab = xa.astype(bf16)
    xb = x.astype(bf16)

    gT = (gt_ref[...] + bif_ref[...]).T
    il = gT[0:SUBLANES, :]
    fpre = pltpu.roll(il, M_HEADS, 0)
    fl = -_softplus(-fpre)
    tin = lax.broadcasted_iota(jnp.int32, (SUBLANES, LANES), 1) % L
    bcum = _seg_scan(fl, jnp.add, 0.0, tin, L)
    a = il - bcum
    m_prev = m_scr[...]
    big_m = jnp.maximum(m_prev, _seg_scan(a, jnp.maximum, -jnp.inf, tin, L))
    m_t = bcum + big_m
    m_last = _seg_scan(big_m, jnp.maximum, -jnp.inf, tin, L, reverse=True)
    rows = [big_m, jnp.exp(m_prev - big_m), jnp.exp(-m_t), jnp.exp(a - m_last), jnp.exp(m_prev - m_last)]
    n_q = len(rows)
    packed = jnp.concatenate(rows + [jnp.zeros((LANES - n_q * SUBLANES, LANES), f32)], axis=0)
    cols = packed.T

    def col(q, h):
        return cols[:, SUBLANES * q + h:SUBLANES * q + h + 1]

    mN_ref[...] = m_t
    m_scr[...] = jnp.broadcast_to(m_t[:, LANES - 1:LANES], (SUBLANES, LANES))

    ti = lax.broadcasted_iota(jnp.int32, (R, R), 0)
    si = lax.broadcasted_iota(jnp.int32, (R, R), 1)
    mask = (si <= ti) & ((ti // L) == (si // L))

    hn_heads = []
    for h in range(M_HEADS):
        hs = slice(h * M_DH, (h + 1) * M_DH)
        q = jnp.dot(xab[:, hs], wq_ref[h], preferred_element_type=f32)
        k = jnp.dot(xab[:, hs], wk_ref[h], preferred_element_type=f32) * (M_DH ** -0.5)
        v = jnp.dot(xb[:, hs], wv_ref[h], preferred_element_type=f32)
        qb, kb, vb = q.astype(bf16), k.astype(bf16), v.astype(bf16)
        qk = lax.dot_general(qb, kb, (((1,), (1,)), ((), ())), preferred_element_type=f32)
        sm = qk * jnp.exp(jnp.where(mask, a[h:h + 1, :] - col(0, h), -jnp.inf))
        num_in = jnp.dot(sm.astype(bf16), vb, preferred_element_type=f32)
        den_in = jnp.sum(sm, axis=1, keepdims=True)
        g_col, e_col, w_col = col(1, h), col(2, h), col(3, h)
        wv = (w_col * v).astype(bf16)
        wk = w_col * k
        pieces = []
        for b in range(S):
            rs = slice(b * L, (b + 1) * L)
            c_old = cN_ref[b, h]
            n_old = nN_ref[b, h]
            q_c = jnp.dot(qb[rs], c_old.astype(bf16), preferred_element_type=f32)
            q_n = jnp.sum(q[rs] * n_old, axis=1, keepdims=True)
            num = num_in[rs] + g_col[rs] * q_c
            den = den_in[rs] + g_col[rs] * q_n
            pieces.append(num / jnp.maximum(jnp.abs(den), e_col[rs]))
            g_end = col(4, h)[(b + 1) * L - 1:(b + 1) * L, :]
            kv = lax.dot_general(kb[rs], wv[rs], (((0,), (0,)), ((), ())), preferred_element_type=f32)
            cN_ref[b, h] = g_end * c_old + kv
            nN_ref[b, h] = g_end * n_old + jnp.sum(wk[rs], axis=0, keepdims=True)
        hh = pieces[0] if S == 1 else jnp.concatenate(pieces, axis=0)
        mu = jnp.mean(hh, axis=1, keepdims=True)
        var = jnp.mean(jnp.square(hh - mu), axis=1, keepdims=True)
        hn_heads.append((hh - mu) * lax.rsqrt(var + LN_EPS))
    hn = jnp.concatenate(hn_heads, axis=1) * ng_ref[...]
    y_ref[...] = jax.nn.sigmoid(z_ref[...]) * (hn + sk_ref[...] * xa)
    xs_scr[:, 0:SUBLANES, :] = xs_scr[:, L:L + SUBLANES, :]
    bN_ref[...] = xs_scr[:, SUBLANES - (CONV_W - 1):SUBLANES, :]


def _mlstm(proj, row0, B, T, S, L, buf8, c0, n0, m0_lanes, cw, cb, wq, wk, wv, bif, ng, sk):
    R = S * L
    assert R == M_ROWS == LANES
    nb, nc = B // S, T // L
    assert nb * S == B and nc * L == T and (S == 1 or nc == 1)
    blk0 = row0 // R
    const2 = lambda b, j: (0, 0)
    const3 = lambda b, j: (0, 0, 0)
    gate_blk = (2 * D_RG + 2 * D_M) // LANES
    return pl.pallas_call(
        functools.partial(_mlstm_kernel, S=S, L=L),
        grid=(nb, nc),
        in_specs=[
            pl.BlockSpec((R, D_M), lambda b, j: (blk0 + b * nc + j, 2)),
            pl.BlockSpec((R, D_M), lambda b, j: (blk0 + b * nc + j, 3)),
            pl.BlockSpec((R, LANES), lambda b, j: (blk0 + b * nc + j, gate_blk)),
            pl.BlockSpec((S, SUBLANES, D_M), lambda b, j: (b, 0, 0)),
            pl.BlockSpec((S, M_HEADS, M_DH, M_DH), lambda b, j: (b, 0, 0, 0)),
            pl.BlockSpec((S, M_HEADS, 1, M_DH), lambda b, j: (b, 0, 0, 0)),
            pl.BlockSpec((SUBLANES, LANES), lambda b, j: (0, b)),
            pl.BlockSpec((CONV_W, D_M), const2),
            pl.BlockSpec((1, D_M), const2),
            pl.BlockSpec((M_HEADS, M_DH, M_DH), const3),
            pl.BlockSpec((M_HEADS, M_DH, M_DH), const3),
            pl.BlockSpec((M_HEADS, M_DH, M_DH), const3),
            pl.BlockSpec((1, LANES), const2),
            pl.BlockSpec((1, D_M), const2),
            pl.BlockSpec((1, D_M), const2),
        ],
        out_specs=[
            pl.BlockSpec((R, D_M), lambda b, j: (b * nc + j, 0)),
            pl.BlockSpec((S, M_HEADS, M_DH, M_DH), lambda b, j: (b, 0, 0, 0)),
            pl.BlockSpec((S, M_HEADS, 1, M_DH), lambda b, j: (b, 0, 0, 0)),
            pl.BlockSpec((SUBLANES, LANES), lambda b, j: (0, b * nc + j)),
            pl.BlockSpec((S, CONV_W - 1, D_M), lambda b, j: (b, 0, 0)),
        ],
        out_shape=[
            jax.ShapeDtypeStruct((B * T, D_M), f32),
            jax.ShapeDtypeStruct((B, M_HEADS, M_DH, M_DH), f32),
            jax.ShapeDtypeStruct((B, M_HEADS, 1, M_DH), f32),
            jax.ShapeDtypeStruct((SUBLANES, B * T), f32),
            jax.ShapeDtypeStruct((B, CONV_W - 1, D_M), f32),
        ],
        scratch_shapes=[pltpu.VMEM((S, SUBLANES + L, D_M), f32), pltpu.VMEM((SUBLANES, LANES), f32)],
        compiler_params=_cparams(2),
        name="mlstm",
    )(proj, proj, proj, buf8, c0, n0, m0_lanes, cw, cb, wq, wk, wv, bif, ng, sk)


def _layer_norm(x, g, b):
    mu = jnp.mean(x, axis=-1, keepdims=True)
    var = jnp.mean(jnp.square(x - mu), axis=-1, keepdims=True)
    return (x - mu) * lax.rsqrt(var + LN_EPS) * g + b


def _first_lane_of_max(vals, lane_f):
    vmax = jnp.max(vals, axis=1, keepdims=True)
    idx = jnp.min(jnp.where(vals == vmax, lane_f, float(LANES)), axis=1, keepdims=True)
    return vmax, idx


def _outproj_kernel(xp_ref, xs_ref, rgp_ref, rgs_ref, mp_ref, ms_ref, wo_ref, g1_ref, b1_ref,
                    wrh_ref, wrl_ref, br_ref, x1_ref, info_ref, cnt_ref, *, n_p):
    i = pl.program_id(0)

    @pl.when(i == 0)
    def _():
        cnt_ref[...] = jnp.zeros_like(cnt_ref)

    def run(x_ref, rg_ref, m_ref):
        mix = jnp.dot(rg_ref[...].astype(bf16), wo_ref[0:D_RG, :], preferred_element_type=f32)
        mix = mix + jnp.dot(m_ref[...].astype(bf16), wo_ref[D_RG:D_RG + D_M, :], preferred_element_type=f32)
        x1 = _layer_norm(ALPHA * x_ref[...] + mix, g1_ref[...], b1_ref[...])
        x1_ref[...] = x1

        hi = x1.astype(bf16)
        lo = (x1 - hi.astype(f32)).astype(bf16)
        lg = (jnp.dot(hi, wrh_ref[...], preferred_element_type=f32)
              + jnp.dot(lo, wrh_ref[...], preferred_element_type=f32)
              + jnp.dot(hi, wrl_ref[...], preferred_element_type=f32)) + br_ref[...]
        lane = lax.broadcasted_iota(jnp.int32, (TM, LANES), 1)
        lane_f = lane.astype(f32)
        neg = -jnp.inf
        gl = jnp.where(lane < N_GROUPS, lg, neg)
        gmax, gidx = _first_lane_of_max(gl, lane_f)
        p_g = 1.0 / jnp.sum(jnp.exp(gl - gmax), axis=1, keepdims=True)
        e_lo = float(N_GROUPS) + float(EXPERTS_PER_GROUP) * gidx
        el = jnp.where((lane_f >= e_lo) & (lane_f < e_lo + float(EXPERTS_PER_GROUP)), lg, neg)
        v1, i1 = _first_lane_of_max(el, lane_f)
        v2, i2 = _first_lane_of_max(jnp.where(lane_f == i1, neg, el), lane_f)
        d = jnp.exp(v2 - v1)
        w1 = p_g / (1.0 + d)
        w2 = p_g * d / (1.0 + d)
        id1 = i1 - float(N_GROUPS)
        id2 = i2 - float(N_GROUPS)

        hot1 = lane_f == id1
        hot2 = lane_f == id2
        onehot = (hot1 | hot2).astype(bf16)
        ti = lax.broadcasted_iota(jnp.int32, (TM, TM), 0)
        si = lax.broadcasted_iota(jnp.int32, (TM, TM), 1)
        tri = (si <= ti).astype(bf16)
        cum = jnp.dot(tri, onehot, preferred_element_type=f32)
        rank = cum - 1.0 + cnt_ref[...]
        r1 = jnp.sum(jnp.where(hot1, rank, 0.0), axis=1, keepdims=True)
        r2 = jnp.sum(jnp.where(hot2, rank, 0.0), axis=1, keepdims=True)
        cnt_ref[...] = cnt_ref[...] + cum[TM - 1:TM, :]

        info = jnp.zeros((TM, LANES), f32)
        for c, val in enumerate((id1, id2, w1, w2, r1, r2)):
            info = jnp.where(lane == c, val, info)
        info_ref[...] = info

    @pl.when(i < n_p)
    def _():
        run(xp_ref, rgp_ref, mp_ref)

    @pl.when(i >= n_p)
    def _():
        run(xs_ref, rgs_ref, ms_ref)


def _outproj(xp, xs, rgp, rgs, mp, ms, wo, g1, b1, wrh, wrl, br):
    n_p, n_s = xp.shape[0] // TM, xs.shape[0] // TM
    n = n_p + n_s
    pmap = lambda i: (jnp.minimum(i, n_p - 1), 0)
    smap = lambda i: (jnp.maximum(i - n_p, 0), 0)
    const = lambda i: (0, 0)
    return pl.pallas_call(
        functools.partial(_outproj_kernel, n_p=n_p),
        grid=(n,),
        in_specs=[
            pl.BlockSpec((TM, D_MODEL), pmap), pl.BlockSpec((TM, D_MODEL), smap),
            pl.BlockSpec((TM, D_RG), pmap), pl.BlockSpec((TM, D_RG), smap),
            pl.BlockSpec((TM, D_M), pmap), pl.BlockSpec((TM, D_M), smap),
            pl.BlockSpec((D_RG + D_M, D_MODEL), const),
            pl.BlockSpec((1, D_MODEL), const), pl.BlockSpec((1, D_MODEL), const),
            pl.BlockSpec((D_MODEL, LANES), const), pl.BlockSpec((D_MODEL, LANES), const),
            pl.BlockSpec((1, LANES), const),
        ],
        out_specs=[
            pl.BlockSpec((TM, D_MODEL), lambda i: (i, 0)),
            pl.BlockSpec((TM, LANES), lambda i: (i, 0)),
            pl.BlockSpec((1, LANES), const),
        ],
        out_shape=[
            jax.ShapeDtypeStruct((n * TM, D_MODEL), f32),
            jax.ShapeDtypeStruct((n * TM, LANES), f32),
            jax.ShapeDtypeStruct((1, LANES), f32),
        ],
        compiler_params=_cparams(1),
        name="outproj",
    )(xp, xs, rgp, rgs, mp, ms, wo, g1, b1, wrh, wrl, br)


def _row_copy(src_ref, src_row, dst_ref, dst_row, sem):
    return pltpu.make_async_copy(src_ref.at[pl.ds(src_row, 1)], dst_ref.at[pl.ds(dst_row, 1)], sem)


def _dispatch_kernel(pos_ref, x1_ref, xs_in_ref, xs_ref, sem):
    del xs_in_ref

    def start(r, c):
        _row_copy(x1_ref, r, xs_ref, pos_ref[0, 0, 2 * r], sem).start()
        _row_copy(x1_ref, r, xs_ref, pos_ref[0, 0, 2 * r + 1], sem).start()
        return c

    lax.fori_loop(0, TM, start, 0)

    def wait(r, c):
        _row_copy(x1_ref, 0, xs_ref, 0, sem).wait()
        _row_copy(x1_ref, 0, xs_ref, 0, sem).wait()
        return c

    lax.fori_loop(0, TM, wait, 0)


def _dispatch(pos, x1, xs_init):
    n = x1.shape[0] // TM
    return pl.pallas_call(
        _dispatch_kernel,
        grid=(n,),
        in_specs=[
            pl.BlockSpec((1, 1, 2 * TM), lambda i: (i, 0, 0), memory_space=pltpu.SMEM),
            pl.BlockSpec((TM, D_MODEL), lambda i: (i, 0)),
            pl.BlockSpec(memory_space=pl.ANY),
        ],
        out_specs=pl.BlockSpec(memory_space=pl.ANY),
        out_shape=jax.ShapeDtypeStruct(xs_init.shape, f32),
        scratch_shapes=[pltpu.SemaphoreType.DMA(())],
        input_output_aliases={2: 0},
        compiler_params=_cparams(1),
        name="dispatch",
    )(pos, x1, xs_init)


def _expert_kernel(te_ref, nu_ref, x_ref, wg_ref, wu_ref, wd_ref, y_ref):
    i = pl.program_id(0)

    @pl.when(i < nu_ref[0])
    def _():
        x = x_ref[...].astype(bf16)
        hg = jnp.dot(x, wg_ref[0].astype(bf16), preferred_element_type=f32)
        hu = jnp.dot(x, wu_ref[0].astype(bf16), preferred_element_type=f32)
        hmid = (hg * jax.nn.sigmoid(hg) * hu).astype(bf16)
        y_ref[...] = jnp.dot(hmid, wd_ref[0].astype(bf16), preferred_element_type=f32)

    @pl.when(i >= nu_ref[0])
    def _():
        y_ref[...] = jnp.zeros_like(y_ref)


def _experts(tile_expert, n_used, xs, wg, wu, wd):
    nt = xs.shape[0] // TM
    wmap = lambda i, te, nu: (te[i], 0, 0)
    return pl.pallas_call(
        _expert_kernel,
        grid_spec=pltpu.PrefetchScalarGridSpec(
            num_scalar_prefetch=2,
            grid=(nt,),
            in_specs=[
                pl.BlockSpec((TM, D_MODEL), lambda i, te, nu: (i, 0)),
                pl.BlockSpec((1, D_MODEL, D_EXPERT), wmap),
                pl.BlockSpec((1, D_MODEL, D_EXPERT), wmap),
                pl.BlockSpec((1, D_EXPERT, D_MODEL), wmap),
            ],
            out_specs=pl.BlockSpec((TM, D_MODEL), lambda i, te, nu: (i, 0)),
        ),
        out_shape=jax.ShapeDtypeStruct(xs.shape, f32),
        compiler_params=_cparams(1),
        name="experts",
    )(tile_expert, n_used, xs, wg, wu, wd)


def _combine_kernel(pos_ref, x1_ref, info_ref, ys_ref, g2_ref, b2_ref, op_ref, os_ref,
                    ya_scr, yb_scr, sem, *, n_p):
    i = pl.program_id(0)

    def start(r, c):
        _row_copy(ys_ref, pos_ref[0, 0, 2 * r], ya_scr, r, sem).start()
        _row_copy(ys_ref, pos_ref[0, 0, 2 * r + 1], yb_scr, r, sem).start()
        return c

    lax.fori_loop(0, TM, start, 0)

    def wait(r, c):
        _row_copy(ys_ref, 0, ya_scr, 0, sem).wait()
        _row_copy(ys_ref, 0, yb_scr, 0, sem).wait()
        return c

    lax.fori_loop(0, TM, wait, 0)

    info = info_ref[...]
    ffn = info[:, 2:3] * ya_scr[...] + info[:, 3:4] * yb_scr[...]
    out = _layer_norm(ALPHA * x1_ref[...] + ffn, g2_ref[...], b2_ref[...])

    @pl.when(i < n_p)
    def _():
        op_ref[...] = out

    @pl.when(i >= n_p)
    def _():
        os_ref[...] = out


def _combine(pos, x1, info, ys, g2, b2, n_p):
    n = x1.shape[0] // TM
    n_s = n - n_p
    const = lambda i: (0, 0)
    return pl.pallas_call(
        functools.partial(_combine_kernel, n_p=n_p),
        grid=(n,),
        in_specs=[
            pl.BlockSpec((1, 1, 2 * TM), lambda i: (i, 0, 0), memory_space=pltpu.SMEM),
            pl.BlockSpec((TM, D_MODEL), lambda i: (i, 0)),
            pl.BlockSpec((TM, LANES), lambda i: (i, 0)),
            pl.BlockSpec(memory_space=pl.ANY),
            pl.BlockSpec((1, D_MODEL), const), pl.BlockSpec((1, D_MODEL), const),
        ],
        out_specs=[
            pl.BlockSpec((TM, D_MODEL), lambda i: (jnp.minimum(i, n_p - 1), 0)),
            pl.BlockSpec((TM, D_MODEL), lambda i: (jnp.maximum(i - n_p, 0), 0)),
        ],
        out_shape=[
            jax.ShapeDtypeStruct((n_p * TM, D_MODEL), f32),
            jax.ShapeDtypeStruct((n_s * TM, D_MODEL), f32),
        ],
        scratch_shapes=[pltpu.VMEM((TM, D_MODEL), f32), pltpu.VMEM((TM, D_MODEL), f32),
                        pltpu.SemaphoreType.DMA(())],
        compiler_params=_cparams(1),
        name="combine",
    )(pos, x1, info, ys, g2, b2)


def _block_diag(w):
    n, d, _ = w.shape
    eye = jnp.eye(n, dtype=w.dtype)
    return (eye[:, None, :, None] * w[:, :, None, :]).reshape(n * d, n * d)


def _pad_history(buf):
    return jnp.pad(buf, ((0, 0), (SUBLANES - (CONV_W - 1), 0), (0, 0)))


def kernel(x_prompt, x_sample, state_rg_h, state_rg_conv, state_m_C, state_m_n, state_m_m, state_m_conv, w_in, rg_conv_w, rg_conv_b, rg_w_a, rg_b_a, rg_w_x, rg_b_x, rg_lambda, m_conv_w, m_conv_b, m_w_q, m_w_k, m_w_v, m_b_i, m_b_f, m_norm_g, m_skip, w_out, ln1_g, ln1_b, ln2_g, ln2_b, moe_w_group, moe_b_group, moe_w_expert, moe_b_expert, moe_w_gate, moe_w_up, moe_w_down):
    BP, TP, _ = x_prompt.shape
    BS, TS, _ = x_sample.shape
    n_prompt, n_sample = BP * TP, BS * TS
    xp = x_prompt.reshape(n_prompt, D_MODEL)
    xs = x_sample.reshape(n_sample, D_MODEL)
    l = 0

    w_in_p = jnp.pad(w_in[l], ((0, 0), (0, N_PROJ - w_in.shape[-1]))).astype(bf16)
    wa = _block_diag(rg_w_a[l]).astype(bf16)
    wx = _block_diag(rg_w_x[l]).astype(bf16)
    row = lambda v: v.reshape(1, -1)
    bif = jnp.pad(jnp.concatenate([m_b_i[l], m_b_f[l]]), (0, LANES - 2 * M_HEADS)).reshape(1, LANES)
    wq, wk, wv = m_w_q[l].astype(bf16), m_w_k[l].astype(bf16), m_w_v[l].astype(bf16)
    w_route = jnp.pad(jnp.concatenate([moe_w_group[l], moe_w_expert[l]], axis=1),
                      ((0, 0), (0, LANES - N_GROUPS - N_EXPERTS)))
    wr_hi = w_route.astype(bf16)
    wr_lo = (w_route - wr_hi.astype(f32)).astype(bf16)
    b_route = jnp.pad(jnp.concatenate([moe_b_group[l], moe_b_expert[l]]),
                      (0, LANES - N_GROUPS - N_EXPERTS)).reshape(1, LANES)

    proj = _inproj(xp, xs, w_in_p)

    rg_args = (rg_conv_w[l], row(rg_conv_b[l]), wa, row(rg_b_a[l]), wx, row(rg_b_x[l]), row(rg_lambda[l]))
    zeros = lambda *s: jnp.zeros(s, f32)
    s_blk = M_ROWS // TS
    yrg_p, p_rg_h, p_rg_conv = _rglru(proj, 0, BP, TP, 1, RG_ROWS_PROMPT, zeros(BP, SUBLANES, D_RG),
                                      zeros(BP, 1, D_RG), *rg_args)
    yrg_s, s_rg_h, s_rg_conv = _rglru(proj, n_prompt, BS, TS, s_blk, TS, _pad_history(state_rg_conv[l]),
                                      state_rg_h[l].reshape(BS, 1, D_RG), *rg_args)

    m_args = (m_conv_w[l], row(m_conv_b[l]), wq, wk, wv, bif, row(m_norm_g[l]), row(m_skip[l]))
    m0_p = jnp.full((SUBLANES, BP * LANES), M_INIT, f32)
    ym_p, p_m_C, p_m_n, p_mt, p_m_conv = _mlstm(
        proj, 0, BP, TP, 1, M_ROWS, zeros(BP, SUBLANES, D_M), zeros(BP, M_HEADS, M_DH, M_DH),
        zeros(BP, M_HEADS, 1, M_DH), m0_p, *m_args)
    m0_s = jnp.pad(jnp.repeat(state_m_m[l].T, TS, axis=1), ((0, SUBLANES - M_HEADS), (0, 0)))
    ym_s, s_m_C, s_m_n, s_mt, s_m_conv = _mlstm(
        proj, n_prompt, BS, TS, s_blk, TS, _pad_history(state_m_conv[l]), state_m_C[l],
        state_m_n[l].reshape(BS, M_HEADS, 1, M_DH), m0_s, *m_args)

    x1, info, cnt = _outproj(xp, xs, yrg_p, yrg_s, ym_p, ym_s, w_out[l].astype(bf16),
                             row(ln1_g[l]), row(ln1_b[l]), wr_hi, wr_lo, b_route)

    n_tok = n_prompt + n_sample
    n_tiles = (n_tok * 2) // TM + N_EXPERTS
    counts = cnt[0, :N_EXPERTS].astype(jnp.int32)
    padded = ((counts + TM - 1) // TM) * TM
    ends = jnp.cumsum(padded)
    offs = ends - padded
    ids = info[:, 0:2].astype(jnp.int32)
    pos = (offs[ids] + info[:, 4:6].astype(jnp.int32)).reshape(n_tok // TM, 1, 2 * TM)
    tile_expert = jnp.minimum(
        jnp.searchsorted(ends // TM, jnp.arange(n_tiles, dtype=jnp.int32), side="right"),
        N_EXPERTS - 1).astype(jnp.int32)
    n_used = (ends[-1] // TM).reshape(1).astype(jnp.int32)

    x_sorted = _dispatch(pos, x1, jnp.zeros((n_tiles * TM, D_MODEL), f32))
    y_sorted = _experts(tile_expert, n_used, x_sorted, moe_w_gate[l], moe_w_up[l], moe_w_down[l])
    y_p, y_s = _combine(pos, x1, info, y_sorted, row(ln2_g[l]), row(ln2_b[l]), n_prompt // TM)

    def last_m(mt, B, T):
        return mt[:M_HEADS].reshape(M_HEADS, B, T)[:, :, T - 1].T[None]

    return (y_p.reshape(BP, TP, D_MODEL), y_s.reshape(BS, TS, D_MODEL),
            p_rg_h.reshape(1, BP, D_RG), p_rg_conv[None], p_m_C[None], p_m_n.reshape(1, BP, M_HEADS, M_DH),
            last_m(p_mt, BP, TP), p_m_conv[None],
            s_rg_h.reshape(1, BS, D_RG), s_rg_conv[None], s_m_C[None], s_m_n.reshape(1, BS, M_HEADS, M_DH),
            last_m(s_mt, BS, TS), s_m_conv[None])
```

```python
import functools

import jax
import jax.numpy as jnp
from jax import lax
from jax.experimental import pallas as pl
from jax.experimental.pallas import tpu as pltpu

f32 = jnp.float32
bf16 = jnp.bfloat16

D_MODEL = 1024
D_RG = 512
RG_BLOCKS = 8
RG_C = 8.0
D_M = 512
M_HEADS = 4
M_DH = 128
CONV_W = 4
N_GROUPS = 4
EXPERTS_PER_GROUP = 8
N_EXPERTS = 32
D_EXPERT = 256
ALPHA = 2.0 ** 0.25
LN_EPS = 1e-5
M_INIT = -1.0e4

LANES = 128
SUBLANES = 8
TM = 256
N_PROJ = 2 * D_RG + 2 * D_M + LANES
RG_ROWS_PROMPT = 256
M_ROWS = 128
M_BLOCKS_PROMPT = 4
M_BLOCKS_SAMPLE = 2
VMEM_LIMIT = 56 * 1024 * 1024


def _cparams(n_axes):
    return pltpu.CompilerParams(dimension_semantics=("arbitrary",) * n_axes,
                                vmem_limit_bytes=VMEM_LIMIT)


def _inproj_kernel(xp_ref, xs_ref, w_ref, op_ref, os_ref, *, n_p):
    i = pl.program_id(0)

    def run(x_ref, o_ref):
        o_ref[...] = jnp.dot(x_ref[...].astype(bf16), w_ref[...], preferred_element_type=f32)

    @pl.when(i < n_p)
    def _():
        run(xp_ref, op_ref)

    @pl.when(i >= n_p)
    def _():
        run(xs_ref, os_ref)


def _inproj(xp, xs, w):
    n_p, n_s = xp.shape[0] // TM, xs.shape[0] // TM
    pmap = lambda i: (jnp.minimum(i, n_p - 1), 0)
    smap = lambda i: (jnp.maximum(i - n_p, 0), 0)
    return pl.pallas_call(
        functools.partial(_inproj_kernel, n_p=n_p),
        grid=(n_p + n_s,),
        in_specs=[
            pl.BlockSpec((TM, D_MODEL), pmap),
            pl.BlockSpec((TM, D_MODEL), smap),
            pl.BlockSpec((D_MODEL, N_PROJ), lambda i: (0, 0)),
        ],
        out_specs=[pl.BlockSpec((TM, N_PROJ), pmap), pl.BlockSpec((TM, N_PROJ), smap)],
        out_shape=[jax.ShapeDtypeStruct((n_p * TM, N_PROJ), f32),
                   jax.ShapeDtypeStruct((n_s * TM, N_PROJ), f32)],
        compiler_params=_cparams(1),
        name="inproj",
    )(xp, xs, w)


def _causal_conv(xs_scr, x, cw_ref, cb_ref, S, L, C):
    xs_scr[:, SUBLANES:SUBLANES + L, :] = x.reshape(S, L, C)
    acc = cb_ref[...] + cw_ref[CONV_W - 1:CONV_W, :] * x
    for j in range(CONV_W - 1):
        lo = SUBLANES - (CONV_W - 1) + j
        acc = acc + cw_ref[j:j + 1, :] * xs_scr[:, lo:lo + L, :].reshape(S * L, C)
    return acc


def _softplus(x):
    return jnp.maximum(x, 0.0) + jnp.log1p(jnp.exp(-jnp.abs(x)))


def _rglru_kernel(x_ref, g_ref, buf_ref, h0_ref, cw_ref, cb_ref, wa_ref, ba_ref, wx_ref, bx_ref,
                  lam_ref, y_ref, hN_ref, cN_ref, xs_scr, *, S, L):
    R = S * L
    t = pl.program_id(1)

    @pl.when(t == 0)
    def _():
        xs_scr[:, 0:SUBLANES, :] = buf_ref[...]
        hN_ref[...] = h0_ref[...]

    x = x_ref[...]
    xc = _causal_conv(xs_scr, x, cw_ref, cb_ref, S, L, D_RG)
    xcb = xc.astype(bf16)
    r = jax.nn.sigmoid(jnp.dot(xcb, wa_ref[...], preferred_element_type=f32) + ba_ref[...])
    ig = jax.nn.sigmoid(jnp.dot(xcb, wx_ref[...], preferred_element_type=f32) + bx_ref[...])
    log_a = (-RG_C) * r * _softplus(-lam_ref[...])
    a = jnp.exp(log_a)
    th = jnp.tanh(log_a)
    u = jnp.sqrt(-2.0 * th / (1.0 - th)) * ig * xc

    tin = lax.broadcasted_iota(jnp.int32, (R, D_RG), 0) % L
    s = 1
    while s < L:
        ok = tin >= s
        a_sh = pltpu.roll(a, s, 0)
        u_sh = pltpu.roll(u, s, 0)
        u = jnp.where(ok, a * u_sh + u, u)
        a = jnp.where(ok, a * a_sh, a)
        s *= 2
    h_prev = jnp.broadcast_to(hN_ref[...], (S, L, D_RG)).reshape(R, D_RG)
    h = a * h_prev + u

    y_ref[...] = h * jax.nn.gelu(g_ref[...], approximate=True)
    hN_ref[...] = h.reshape(S, L, D_RG)[:, L - 1:L, :]
    xs_scr[:, 0:SUBLANES, :] = xs_scr[:, L:L + SUBLANES, :]
    cN_ref[...] = xs_scr[:, SUBLANES - (CONV_W - 1):SUBLANES, :]


def _rglru(proj, row0, B, T, S, L, buf8, h0, cw, cb, wa, ba, wx, bx, lam):
    R = S * L
    nb, nt = B // S, T // L
    blk0 = row0 // R
    row_map = lambda b, t: (blk0 + b * nt + t, 0)
    const2 = lambda b, t: (0, 0)
    return pl.pallas_call(
        functools.partial(_rglru_kernel, S=S, L=L),
        grid=(nb, nt),
        in_specs=[
            pl.BlockSpec((R, D_RG), row_map),
            pl.BlockSpec((R, D_RG), lambda b, t: (blk0 + b * nt + t, 1)),
            pl.BlockSpec((S, SUBLANES, D_RG), lambda b, t: (b, 0, 0)),
            pl.BlockSpec((S, 1, D_RG), lambda b, t: (b, 0, 0)),
            pl.BlockSpec((CONV_W, D_RG), const2),
            pl.BlockSpec((1, D_RG), const2),
            pl.BlockSpec((D_RG, D_RG), const2),
            pl.BlockSpec((1, D_RG), const2),
            pl.BlockSpec((D_RG, D_RG), const2),
            pl.BlockSpec((1, D_RG), const2),
            pl.BlockSpec((1, D_RG), const2),
        ],
        out_specs=[
            pl.BlockSpec((R, D_RG), lambda b, t: (b * nt + t, 0)),
            pl.BlockSpec((S, 1, D_RG), lambda b, t: (b, 0, 0)),
            pl.BlockSpec((S, CONV_W - 1, D_RG), lambda b, t: (b, 0, 0)),
        ],
        out_shape=[
            jax.ShapeDtypeStruct((B * T, D_RG), f32),
            jax.ShapeDtypeStruct((B, 1, D_RG), f32),
            jax.ShapeDtypeStruct((B, CONV_W - 1, D_RG), f32),
        ],
        scratch_shapes=[pltpu.VMEM((S, SUBLANES + L, D_RG), f32)],
        compiler_params=_cparams(2),
        name="rglru",
    )(proj, proj, buf8, h0, cw, cb, wa, ba, wx, bx, lam)


def _seg_scan(x, op, fill, tin, L, reverse=False):
    s = 1
    while s < L:
        if reverse:
            sh = pltpu.roll(x, LANES - s, 1)
            ok = tin < L - s
        else:
            sh = pltpu.roll(x, s, 1)
            ok = tin >= s
        x = op(x, jnp.where(ok, sh, fill))
        s *= 2
    return x


def _mlstm_kernel(x_ref, z_ref, gt_ref, buf_ref, c0_ref, n0_ref, m0_ref,
                  cw_ref, cb_ref, wq_ref, wk_ref, wv_ref, bif_ref, ng_ref, sk_ref,
                  y_ref, cN_ref, nN_ref, mN_ref, bN_ref, xs_scr, m_scr, *, S, L, G):
    for g in range(G):
        sq = pl.ds(g * S, S)
        _mlstm_block(x_ref.at[g], z_ref.at[g], gt_ref.at[g], buf_ref.at[sq], c0_ref.at[sq], n0_ref.at[sq],
                     m0_ref.at[g], cw_ref, cb_ref, wq_ref, wk_ref, wv_ref, bif_ref, ng_ref, sk_ref,
                     y_ref.at[g], cN_ref.at[sq], nN_ref.at[sq], mN_ref.at[g], bN_ref.at[sq],
                     xs_scr.at[sq], m_scr.at[g], S=S, L=L)


def _mlstm_block(x_ref, z_ref, gt_ref, buf_ref, c0_ref, n0_ref, m0_ref,
                 cw_ref, cb_ref, wq_ref, wk_ref, wv_ref, bif_ref, ng_ref, sk_ref,
                 y_ref, cN_ref, nN_ref, mN_ref, bN_ref, xs_scr, m_scr, *, S, L):
    R = S * L
    j = pl.program_id(1)

    @pl.when(j == 0)
    def _():
        xs_scr[:, 0:SUBLANES, :] = buf_ref[...]
        cN_ref[...] = c0_ref[...]
        nN_ref[...] = n0_ref[...]
        m_scr[...] = m0_ref[...]

    x = x_ref[...]
    xc = _causal_conv(xs_scr, x, cw_ref, cb_ref, S, L, D_M)
    xa = xc * jax.nn.sigmoid(xc)
    xab = xa.astype(bf16)
    xb = x.astype(bf16)

    gT = (gt_ref[...] + bif_ref[...]).T
    il = gT[0:SUBLANES, :]
    fpre = pltpu.roll(il, M_HEADS, 0)
    fl = -_softplus(-fpre)
    tin = lax.broadcasted_iota(jnp.int32, (SUBLANES, LANES), 1) % L
    bcum = _seg_scan(fl, jnp.add, 0.0, tin, L)
    a = il - bcum
    m_prev = m_scr[...]
    big_m = jnp.maximum(m_prev, _seg_scan(a, jnp.maximum, -jnp.inf, tin, L))
    m_t = bcum + big_m
    m_last = _seg_scan(big_m, jnp.maximum, -jnp.inf, tin, L, reverse=True)
    rows = [big_m, jnp.exp(m_prev - big_m), jnp.exp(-m_t), jnp.exp(a - m_last), jnp.exp(m_prev - m_last)]
    n_q = len(rows)
    packed = jnp.concatenate(rows + [jnp.zeros((LANES - n_q * SUBLANES, LANES), f32)], axis=0)
    cols = packed.T

    def col(q, h):
        return cols[:, SUBLANES * q + h:SUBLANES * q + h + 1]

    mN_ref[...] = m_t
    m_scr[...] = jnp.broadcast_to(m_t[:, LANES - 1:LANES], (SUBLANES, LANES))

    ti = lax.broadcasted_iota(jnp.int32, (R, R), 0)
    si = lax.broadcasted_iota(jnp.int32, (R, R), 1)
    mask = (si <= ti) & ((ti // L) == (si // L))

    hn_heads = []
    for h in range(M_HEADS):
        hs = slice(h * M_DH, (h + 1) * M_DH)
        q = jnp.dot(xab[:, hs], wq_ref[h], preferred_element_type=f32)
        k = jnp.dot(xab[:, hs], wk_ref[h], preferred_element_type=f32) * (M_DH ** -0.5)
        v = jnp.dot(xb[:, hs], wv_ref[h], preferred_element_type=f32)
        qb, kb, vb = q.astype(bf16), k.astype(bf16), v.astype(bf16)
        qk = lax.dot_general(qb, kb, (((1,), (1,)), ((), ())), preferred_element_type=f32)
        sm = qk * jnp.exp(jnp.where(mask, a[h:h + 1, :] - col(0, h), -jnp.inf))
        num_in = jnp.dot(sm.astype(bf16), vb, preferred_element_type=f32)
        den_in = jnp.sum(sm, axis=1, keepdims=True)
        g_col, e_col, w_col = col(1, h), col(2, h), col(3, h)
        wv = (w_col * v).astype(bf16)
        wk = w_col * k
        pieces = []
        for b in range(S):
            rs = slice(b * L, (b + 1) * L)
            c_old = cN_ref[b, h]
            n_old = nN_ref[b, h]
            q_c = jnp.dot(qb[rs], c_old.astype(bf16), preferred_element_type=f32)
            q_n = jnp.sum(q[rs] * n_old, axis=1, keepdims=True)
            num = num_in[rs] + g_col[rs] * q_c
            den = den_in[rs] + g_col[rs] * q_n
            pieces.append(num / jnp.maximum(jnp.abs(den), e_col[rs]))
            g_end = col(4, h)[(b + 1) * L - 1:(b + 1) * L, :]
            kv = lax.dot_general(kb[rs], wv[rs], (((0,), (0,)), ((), ())), preferred_element_type=f32)
            cN_ref[b, h] = g_end * c_old + kv
            nN_ref[b, h] = g_end * n_old + jnp.sum(wk[rs], axis=0, keepdims=True)
        hh = pieces[0] if S == 1 else jnp.concatenate(pieces, axis=0)
        mu = jnp.mean(hh, axis=1, keepdims=True)
        var = jnp.mean(jnp.square(hh - mu), axis=1, keepdims=True)
        hn_heads.append((hh - mu) * lax.rsqrt(var + LN_EPS))
    hn = jnp.concatenate(hn_heads, axis=1) * ng_ref[...]
    y_ref[...] = jax.nn.sigmoid(z_ref[...]) * (hn + sk_ref[...] * xa)
    xs_scr[:, 0:SUBLANES, :] = xs_scr[:, L:L + SUBLANES, :]
    bN_ref[...] = xs_scr[:, SUBLANES - (CONV_W - 1):SUBLANES, :]


def _mlstm(proj, B, T, S, L, G, buf8, c0, n0, m0_lanes, cw, cb, wq, wk, wv, bif, ng, sk):
    R = S * L
    assert R == M_ROWS == LANES
    nblk, nc = B // S, T // L
    nb = nblk // G
    assert nblk * S == B and nc * L == T and nb * G == nblk and (S == 1 or nc == 1)
    proj4 = proj.reshape(nblk, nc, R, N_PROJ)
    m0_3 = m0_lanes.reshape(SUBLANES, nblk, LANES).transpose(1, 0, 2)
    const2 = lambda b, j: (0, 0)
    const3 = lambda b, j: (0, 0, 0)
    gate_blk = (2 * D_RG + 2 * D_M) // LANES
    GS = G * S
    outs = pl.pallas_call(
        functools.partial(_mlstm_kernel, S=S, L=L, G=G),
        grid=(nb, nc),
        in_specs=[
            pl.BlockSpec((G, None, R, D_M), lambda b, j: (b, j, 0, 2)),
            pl.BlockSpec((G, None, R, D_M), lambda b, j: (b, j, 0, 3)),
            pl.BlockSpec((G, None, R, LANES), lambda b, j: (b, j, 0, gate_blk)),
            pl.BlockSpec((GS, SUBLANES, D_M), lambda b, j: (b, 0, 0)),
            pl.BlockSpec((GS, M_HEADS, M_DH, M_DH), lambda b, j: (b, 0, 0, 0)),
            pl.BlockSpec((GS, M_HEADS, 1, M_DH), lambda b, j: (b, 0, 0, 0)),
            pl.BlockSpec((G, SUBLANES, LANES), lambda b, j: (b, 0, 0)),
            pl.BlockSpec((CONV_W, D_M), const2),
            pl.BlockSpec((1, D_M), const2),
            pl.BlockSpec((M_HEADS, M_DH, M_DH), const3),
            pl.BlockSpec((M_HEADS, M_DH, M_DH), const3),
            pl.BlockSpec((M_HEADS, M_DH, M_DH), const3),
            pl.BlockSpec((1, LANES), const2),
            pl.BlockSpec((1, D_M), const2),
            pl.BlockSpec((1, D_M), const2),
        ],
        out_specs=[
            pl.BlockSpec((G, None, R, D_M), lambda b, j: (b, j, 0, 0)),
            pl.BlockSpec((GS, M_HEADS, M_DH, M_DH), lambda b, j: (b, 0, 0, 0)),
            pl.BlockSpec((GS, M_HEADS, 1, M_DH), lambda b, j: (b, 0, 0, 0)),
            pl.BlockSpec((G, None, SUBLANES, LANES), lambda b, j: (b, j, 0, 0)),
            pl.BlockSpec((GS, CONV_W - 1, D_M), lambda b, j: (b, 0, 0)),
        ],
        out_shape=[
            jax.ShapeDtypeStruct((nblk, nc, R, D_M), f32),
            jax.ShapeDtypeStruct((B, M_HEADS, M_DH, M_DH), f32),
            jax.ShapeDtypeStruct((B, M_HEADS, 1, M_DH), f32),
            jax.ShapeDtypeStruct((nblk, nc, SUBLANES, LANES), f32),
            jax.ShapeDtypeStruct((B, CONV_W - 1, D_M), f32),
        ],
        scratch_shapes=[pltpu.VMEM((GS, SUBLANES + L, D_M), f32), pltpu.VMEM((G, SUBLANES, LANES), f32)],
        compiler_params=_cparams(2),
        name="mlstm",
    )(proj4, proj4, proj4, buf8, c0, n0, m0_3, cw, cb, wq, wk, wv, bif, ng, sk)
    y, c_new, n_new, m_t, b_new = outs
    m_last = m_t[:, nc - 1, :M_HEADS, :].reshape(nblk, M_HEADS, S, L)[:, :, :, L - 1]
    m_last = m_last.transpose(0, 2, 1).reshape(B, M_HEADS)
    return y.reshape(B * T, D_M), c_new, n_new, m_last, b_new


def _layer_norm(x, g, b):
    mu = jnp.mean(x, axis=-1, keepdims=True)
    var = jnp.mean(jnp.square(x - mu), axis=-1, keepdims=True)
    return (x - mu) * lax.rsqrt(var + LN_EPS) * g + b


def _first_lane_of_max(vals, lane_f):
    vmax = jnp.max(vals, axis=1, keepdims=True)
    idx = jnp.min(jnp.where(vals == vmax, lane_f, float(LANES)), axis=1, keepdims=True)
    return vmax, idx


def _outproj_kernel(xp_ref, xs_ref, rgp_ref, rgs_ref, mp_ref, ms_ref, wo_ref, g1_ref, b1_ref,
                    wrh_ref, wrl_ref, br_ref, x1_ref, info_ref, cnt_ref, *, n_p):
    i = pl.program_id(0)

    @pl.when(i == 0)
    def _():
        cnt_ref[...] = jnp.zeros_like(cnt_ref)

    def run(x_ref, rg_ref, m_ref):
        mix = jnp.dot(rg_ref[...].astype(bf16), wo_ref[0:D_RG, :], preferred_element_type=f32)
        mix = mix + jnp.dot(m_ref[...].astype(bf16), wo_ref[D_RG:D_RG + D_M, :], preferred_element_type=f32)
        x1 = _layer_norm(ALPHA * x_ref[...] + mix, g1_ref[...], b1_ref[...])
        x1_ref[...] = x1

        hi = x1.astype(bf16)
        lo = (x1 - hi.astype(f32)).astype(bf16)
        lg = (jnp.dot(hi, wrh_ref[...], preferred_element_type=f32)
              + jnp.dot(lo, wrh_ref[...], preferred_element_type=f32)
              + jnp.dot(hi, wrl_ref[...], preferred_element_type=f32)) + br_ref[...]
        lane = lax.broadcasted_iota(jnp.int32, (TM, LANES), 1)
        lane_f = lane.astype(f32)
        neg = -jnp.inf
        gl = jnp.where(lane < N_GROUPS, lg, neg)
        gmax, gidx = _first_lane_of_max(gl, lane_f)
        p_g = 1.0 / jnp.sum(jnp.exp(gl - gmax), axis=1, keepdims=True)
        e_lo = float(N_GROUPS) + float(EXPERTS_PER_GROUP) * gidx
        el = jnp.where((lane_f >= e_lo) & (lane_f < e_lo + float(EXPERTS_PER_GROUP)), lg, neg)
        v1, i1 = _first_lane_of_max(el, lane_f)
        v2, i2 = _first_lane_of_max(jnp.where(lane_f == i1, neg, el), lane_f)
        d = jnp.exp(v2 - v1)
        w1 = p_g / (1.0 + d)
        w2 = p_g * d / (1.0 + d)
        id1 = i1 - float(N_GROUPS)
        id2 = i2 - float(N_GROUPS)

        hot1 = lane_f == id1
        hot2 = lane_f == id2
        onehot = (hot1 | hot2).astype(bf16)
        ti = lax.broadcasted_iota(jnp.int32, (TM, TM), 0)
        si = lax.broadcasted_iota(jnp.int32, (TM, TM), 1)
        tri = (si <= ti).astype(bf16)
        cum = jnp.dot(tri, onehot, preferred_element_type=f32)
        rank = cum - 1.0 + cnt_ref[...]
        r1 = jnp.sum(jnp.where(hot1, rank, 0.0), axis=1, keepdims=True)
        r2 = jnp.sum(jnp.where(hot2, rank, 0.0), axis=1, keepdims=True)
        cnt_ref[...] = cnt_ref[...] + cum[TM - 1:TM, :]

        info = jnp.zeros((TM, LANES), f32)
        for c, val in enumerate((id1, id2, w1, w2, r1, r2)):
            info = jnp.where(lane == c, val, info)
        info_ref[...] = info

    @pl.when(i < n_p)
    def _():
        run(xp_ref, rgp_ref, mp_ref)

    @pl.when(i >= n_p)
    def _():
        run(xs_ref, rgs_ref, ms_ref)


def _outproj(xp, xs, rgp, rgs, mp, ms, wo, g1, b1, wrh, wrl, br):
    n_p, n_s = xp.shape[0] // TM, xs.shape[0] // TM
    n = n_p + n_s
    pmap = lambda i: (jnp.minimum(i, n_p - 1), 0)
    smap = lambda i: (jnp.maximum(i - n_p, 0), 0)
    const = lambda i: (0, 0)
    return pl.pallas_call(
        functools.partial(_outproj_kernel, n_p=n_p),
        grid=(n,),
        in_specs=[
            pl.BlockSpec((TM, D_MODEL), pmap), pl.BlockSpec((TM, D_MODEL), smap),
            pl.BlockSpec((TM, D_RG), pmap), pl.BlockSpec((TM, D_RG), smap),
            pl.BlockSpec((TM, D_M), pmap), pl.BlockSpec((TM, D_M), smap),
            pl.BlockSpec((D_RG + D_M, D_MODEL), const),
            pl.BlockSpec((1, D_MODEL), const), pl.BlockSpec((1, D_MODEL), const),
            pl.BlockSpec((D_MODEL, LANES), const), pl.BlockSpec((D_MODEL, LANES), const),
            pl.BlockSpec((1, LANES), const),
        ],
        out_specs=[
            pl.BlockSpec((TM, D_MODEL), lambda i: (i, 0)),
            pl.BlockSpec((TM, LANES), lambda i: (i, 0)),
            pl.BlockSpec((1, LANES), const),
        ],
        out_shape=[
            jax.ShapeDtypeStruct((n * TM, D_MODEL), f32),
            jax.ShapeDtypeStruct((n * TM, LANES), f32),
            jax.ShapeDtypeStruct((1, LANES), f32),
        ],
        compiler_params=_cparams(1),
        name="outproj",
    )(xp, xs, rgp, rgs, mp, ms, wo, g1, b1, wrh, wrl, br)


def _row_copy(src_ref, src_row, dst_ref, dst_row, sem):
    return pltpu.make_async_copy(src_ref.at[pl.ds(src_row, 1)], dst_ref.at[pl.ds(dst_row, 1)], sem)


def _tile_rows_wait(src_ref, dst_ref, sem):
    pltpu.make_async_copy(src_ref.at[pl.ds(0, TM)], dst_ref.at[pl.ds(0, TM)], sem).wait()


def _dispatch_kernel(lt_ref, pos_ref, x1_ref, xs_ref, zero_scr, sem, zsem):
    i = pl.program_id(0)
    n = pl.num_programs(0)

    def zero_copy(e):
        row0 = pl.multiple_of(lt_ref[e], TM)
        return pltpu.make_async_copy(zero_scr, xs_ref.at[pl.ds(row0, TM)], zsem)

    def zero_tile(t):
        return pltpu.make_async_copy(zero_scr, xs_ref.at[pl.ds(pl.multiple_of(t * TM, TM), TM)], zsem)

    @pl.when(i == 0)
    def _():
        zero_scr[...] = jnp.zeros_like(zero_scr)
        for e in range(N_EXPERTS):
            @pl.when(lt_ref[e] >= 0)
            def _():
                zero_copy(e).start()
        n_used = lt_ref[N_EXPERTS]
        n_tiles = xs_ref.shape[0] // TM
        lax.fori_loop(n_used, n_tiles, lambda t, c: (zero_tile(t).start(), c)[1], 0)
        for e in range(N_EXPERTS):
            @pl.when(lt_ref[e] >= 0)
            def _():
                zero_copy(e).wait()
        lax.fori_loop(n_used, n_tiles, lambda t, c: (zero_tile(t).wait(), c)[1], 0)

    base = i * TM

    def start(r, c):
        _row_copy(x1_ref, base + r, xs_ref, pos_ref[0, 0, 2 * r], sem).start()
        _row_copy(x1_ref, base + r, xs_ref, pos_ref[0, 0, 2 * r + 1], sem).start()
        return c

    lax.fori_loop(0, TM, start, 0, unroll=8)

    def wait_step():
        _tile_rows_wait(x1_ref, xs_ref, sem)
        _tile_rows_wait(x1_ref, xs_ref, sem)

    @pl.when(i > 0)
    def _():
        wait_step()

    @pl.when(i == n - 1)
    def _():
        wait_step()


def _dispatch(last_tile_row, pos, x1, n_rows):
    n = x1.shape[0] // TM
    return pl.pallas_call(
        _dispatch_kernel,
        grid_spec=pltpu.PrefetchScalarGridSpec(
            num_scalar_prefetch=1,
            grid=(n,),
            in_specs=[
                pl.BlockSpec((1, 1, 2 * TM), lambda i, lt: (i, 0, 0), memory_space=pltpu.SMEM),
                pl.BlockSpec(memory_space=pl.ANY),
            ],
            out_specs=pl.BlockSpec(memory_space=pl.ANY),
            scratch_shapes=[pltpu.VMEM((TM, D_MODEL), f32), pltpu.SemaphoreType.DMA(()),
                            pltpu.SemaphoreType.DMA(())],
        ),
        out_shape=jax.ShapeDtypeStruct((n_rows, D_MODEL), f32),
        compiler_params=_cparams(1),
        name="dispatch",
    )(last_tile_row, pos, x1)


def _expert_kernel(te_ref, nu_ref, x_ref, wg_ref, wu_ref, wd_ref, y_ref):
    @pl.when(pl.program_id(0) < nu_ref[0])
    def _():
        x = x_ref[...].astype(bf16)
        hg = jnp.dot(x, wg_ref[0].astype(bf16), preferred_element_type=f32)
        hu = jnp.dot(x, wu_ref[0].astype(bf16), preferred_element_type=f32)
        hmid = (hg * jax.nn.sigmoid(hg) * hu).astype(bf16)
        y_ref[...] = jnp.dot(hmid, wd_ref[0].astype(bf16), preferred_element_type=f32)

    @pl.when(pl.program_id(0) >= nu_ref[0])
    def _():
        y_ref[...] = jnp.zeros_like(y_ref)


def _experts(tile_expert, n_used, xs, wg, wu, wd):
    nt = xs.shape[0] // TM
    wmap = lambda i, te, nu: (te[i], 0, 0)
    xmap = lambda i, te, nu: (jnp.minimum(i, nu[0] - 1), 0)
    return pl.pallas_call(
        _expert_kernel,
        grid_spec=pltpu.PrefetchScalarGridSpec(
            num_scalar_prefetch=2,
            grid=(nt,),
            in_specs=[
                pl.BlockSpec((TM, D_MODEL), xmap),
                pl.BlockSpec((1, D_MODEL, D_EXPERT), wmap),
                pl.BlockSpec((1, D_MODEL, D_EXPERT), wmap),
                pl.BlockSpec((1, D_EXPERT, D_MODEL), wmap),
            ],
            out_specs=pl.BlockSpec((TM, D_MODEL), lambda i, te, nu: (i, 0)),
        ),
        out_shape=jax.ShapeDtypeStruct(xs.shape, f32),
        compiler_params=_cparams(1),
        name="experts",
    )(tile_expert, n_used, xs, wg, wu, wd)


def _combine_kernel(pos_ref, posn_ref, x1_ref, info_ref, ys_ref, g2_ref, b2_ref, op_ref, os_ref,
                    ya_scr, yb_scr, sem, *, n_p):
    i = pl.program_id(0)
    n = pl.num_programs(0)
    slot = i % 2

    def gather(p_ref, s):
        def start(r, c):
            _row_copy(ys_ref, p_ref[0, 0, 2 * r], ya_scr.at[s], r, sem.at[s]).start()
            _row_copy(ys_ref, p_ref[0, 0, 2 * r + 1], yb_scr.at[s], r, sem.at[s]).start()
            return c

        lax.fori_loop(0, TM, start, 0, unroll=8)

    @pl.when(i == 0)
    def _():
        gather(pos_ref, 0)

    @pl.when(i + 1 < n)
    def _():
        gather(posn_ref, 1 - slot)

    _tile_rows_wait(ys_ref, ya_scr.at[slot], sem.at[slot])
    _tile_rows_wait(ys_ref, yb_scr.at[slot], sem.at[slot])

    info = info_ref[...]
    ffn = info[:, 2:3] * ya_scr[slot] + info[:, 3:4] * yb_scr[slot]
    out = _layer_norm(ALPHA * x1_ref[...] + ffn, g2_ref[...], b2_ref[...])

    @pl.when(i < n_p)
    def _():
        op_ref[...] = out

    @pl.when(i >= n_p)
    def _():
        os_ref[...] = out


def _combine(pos, x1, info, ys, g2, b2, n_p):
    n = x1.shape[0] // TM
    n_s = n - n_p
    const = lambda i: (0, 0)
    return pl.pallas_call(
        functools.partial(_combine_kernel, n_p=n_p),
        grid=(n,),
        in_specs=[
            pl.BlockSpec((1, 1, 2 * TM), lambda i: (i, 0, 0), memory_space=pltpu.SMEM),
            pl.BlockSpec((1, 1, 2 * TM), lambda i: (jnp.minimum(i + 1, n - 1), 0, 0),
                         memory_space=pltpu.SMEM),
            pl.BlockSpec((TM, D_MODEL), lambda i: (i, 0)),
            pl.BlockSpec((TM, LANES), lambda i: (i, 0)),
            pl.BlockSpec(memory_space=pl.ANY),
            pl.BlockSpec((1, D_MODEL), const), pl.BlockSpec((1, D_MODEL), const),
        ],
        out_specs=[
            pl.BlockSpec((TM, D_MODEL), lambda i: (jnp.minimum(i, n_p - 1), 0)),
            pl.BlockSpec((TM, D_MODEL), lambda i: (jnp.maximum(i - n_p, 0), 0)),
        ],
        out_shape=[
            jax.ShapeDtypeStruct((n_p * TM, D_MODEL), f32),
            jax.ShapeDtypeStruct((n_s * TM, D_MODEL), f32),
        ],
        scratch_shapes=[pltpu.VMEM((2, TM, D_MODEL), f32), pltpu.VMEM((2, TM, D_MODEL), f32),
                        pltpu.SemaphoreType.DMA((2,))],
        compiler_params=_cparams(1),
        name="combine",
    )(pos, pos, x1, info, ys, g2, b2)


def _block_diag(w):
    n, d, _ = w.shape
    eye = jnp.eye(n, dtype=w.dtype)
    return (eye[:, None, :, None] * w[:, :, None, :]).reshape(n * d, n * d)


def _pad_history(buf):
    return jnp.pad(buf, ((0, 0), (SUBLANES - (CONV_W - 1), 0), (0, 0)))


def kernel(x_prompt, x_sample, state_rg_h, state_rg_conv, state_m_C, state_m_n, state_m_m, state_m_conv, w_in, rg_conv_w, rg_conv_b, rg_w_a, rg_b_a, rg_w_x, rg_b_x, rg_lambda, m_conv_w, m_conv_b, m_w_q, m_w_k, m_w_v, m_b_i, m_b_f, m_norm_g, m_skip, w_out, ln1_g, ln1_b, ln2_g, ln2_b, moe_w_group, moe_b_group, moe_w_expert, moe_b_expert, moe_w_gate, moe_w_up, moe_w_down):
    BP, TP, _ = x_prompt.shape
    BS, TS, _ = x_sample.shape
    n_prompt, n_sample = BP * TP, BS * TS
    xp = x_prompt.reshape(n_prompt, D_MODEL)
    xs = x_sample.reshape(n_sample, D_MODEL)
    l = 0

    w_in_p = jnp.pad(w_in[l], ((0, 0), (0, N_PROJ - w_in.shape[-1]))).astype(bf16)
    wa = _block_diag(rg_w_a[l]).astype(bf16)
    wx = _block_diag(rg_w_x[l]).astype(bf16)
    row = lambda v: v.reshape(1, -1)
    bif = jnp.pad(jnp.concatenate([m_b_i[l], m_b_f[l]]), (0, LANES - 2 * M_HEADS)).reshape(1, LANES)
    wq, wk, wv = m_w_q[l].astype(bf16), m_w_k[l].astype(bf16), m_w_v[l].astype(bf16)
    w_route = jnp.pad(jnp.concatenate([moe_w_group[l], moe_w_expert[l]], axis=1),
                      ((0, 0), (0, LANES - N_GROUPS - N_EXPERTS)))
    wr_hi = w_route.astype(bf16)
    wr_lo = (w_route - wr_hi.astype(f32)).astype(bf16)
    b_route = jnp.pad(jnp.concatenate([moe_b_group[l], moe_b_expert[l]]),
                      (0, LANES - N_GROUPS - N_EXPERTS)).reshape(1, LANES)

    proj_p, proj_s = _inproj(xp, xs, w_in_p)

    rg_args = (rg_conv_w[l], row(rg_conv_b[l]), wa, row(rg_b_a[l]), wx, row(rg_b_x[l]), row(rg_lambda[l]))
    zeros = lambda *s: jnp.zeros(s, f32)
    s_blk = M_ROWS // TS
    yrg_p, p_rg_h, p_rg_conv = _rglru(proj_p, 0, BP, TP, 1, RG_ROWS_PROMPT, zeros(BP, SUBLANES, D_RG),
                                      zeros(BP, 1, D_RG), *rg_args)
    yrg_s, s_rg_h, s_rg_conv = _rglru(proj_s, 0, BS, TS, s_blk, TS, _pad_history(state_rg_conv[l]),
                                      state_rg_h[l].reshape(BS, 1, D_RG), *rg_args)

    m_args = (m_conv_w[l], row(m_conv_b[l]), wq, wk, wv, bif, row(m_norm_g[l]), row(m_skip[l]))
    m0_p = jnp.full((SUBLANES, BP * LANES), M_INIT, f32)
    ym_p, p_m_C, p_m_n, p_m_m, p_m_conv = _mlstm(
        proj_p, BP, TP, 1, M_ROWS, M_BLOCKS_PROMPT, zeros(BP, SUBLANES, D_M), zeros(BP, M_HEADS, M_DH, M_DH),
        zeros(BP, M_HEADS, 1, M_DH), m0_p, *m_args)
    m0_s = jnp.pad(jnp.repeat(state_m_m[l].T, TS, axis=1), ((0, SUBLANES - M_HEADS), (0, 0)))
    ym_s, s_m_C, s_m_n, s_m_m, s_m_conv = _mlstm(
        proj_s, BS, TS, s_blk, TS, M_BLOCKS_SAMPLE, _pad_history(state_m_conv[l]), state_m_C[l],
        state_m_n[l].reshape(BS, M_HEADS, 1, M_DH), m0_s, *m_args)

    x1, info, cnt = _outproj(xp, xs, yrg_p, yrg_s, ym_p, ym_s, w_out[l].astype(bf16),
                             row(ln1_g[l]), row(ln1_b[l]), wr_hi, wr_lo, b_route)

    n_tok = n_prompt + n_sample
    n_tiles = (n_tok * 2) // TM + N_EXPERTS
    counts = cnt[0, :N_EXPERTS].astype(jnp.int32)
    padded = ((counts + TM - 1) // TM) * TM
    ends = jnp.cumsum(padded)
    offs = ends - padded
    experts = jnp.arange(N_EXPERTS, dtype=jnp.int32)
    ids = info[:, 0:2].astype(jnp.int32)
    offs_of_id = jnp.sum(jnp.where(ids[:, :, None] == experts, offs, 0), axis=-1)
    pos = (offs_of_id + info[:, 4:6].astype(jnp.int32)).reshape(n_tok // TM, 1, 2 * TM)
    tiles = jnp.arange(n_tiles, dtype=jnp.int32)
    tile_expert = jnp.minimum(jnp.sum(tiles[:, None] >= (ends // TM)[None, :], axis=1),
                              N_EXPERTS - 1).astype(jnp.int32)
    n_used = (ends[-1] // TM).reshape(1).astype(jnp.int32)
    last_tile_row = jnp.concatenate([jnp.where(counts > 0, ends - TM, -1).astype(jnp.int32), n_used])

    x_sorted = _dispatch(last_tile_row, pos, x1, n_tiles * TM)
    y_sorted = _experts(tile_expert, n_used, x_sorted, moe_w_gate[l], moe_w_up[l], moe_w_down[l])
    y_p, y_s = _combine(pos, x1, info, y_sorted, row(ln2_g[l]), row(ln2_b[l]), n_prompt // TM)

    return (y_p.reshape(BP, TP, D_MODEL), y_s.reshape(BS, TS, D_MODEL),
            p_rg_h.reshape(1, BP, D_RG), p_rg_conv[None], p_m_C[None], p_m_n.reshape(1, BP, M_HEADS, M_DH),
            p_m_m[None], p_m_conv[None],
            s_rg_h.reshape(1, BS, D_RG), s_rg_conv[None], s_m_C[None], s_m_n.reshape(1, BS, M_HEADS, M_DH),
            s_m_m[None], s_m_conv[None])
```

```python
import functools

import jax
import jax.numpy as jnp
from jax import lax
from jax.experimental import pallas as pl
from jax.experimental.pallas import tpu as pltpu

f32 = jnp.float32
bf16 = jnp.bfloat16

D_MODEL = 1024
D_RG = 512
RG_BLOCKS = 8
RG_C = 8.0
D_M = 512
M_HEADS = 4
M_DH = 128
CONV_W = 4
N_GROUPS = 4
EXPERTS_PER_GROUP = 8
N_EXPERTS = 32
D_EXPERT = 256
ALPHA = 2.0 ** 0.25
LN_EPS = 1e-5
M_INIT = -1.0e4

LANES = 128
SUBLANES = 8
TM = 256
N_PROJ = 2 * D_RG + 2 * D_M + LANES
RG_ROWS_PROMPT = 256
M_ROWS = 128
M_BLOCKS_PROMPT = 4
M_BLOCKS_SAMPLE = 2
VMEM_LIMIT = 56 * 1024 * 1024


def _cparams(n_axes):
    return pltpu.CompilerParams(dimension_semantics=("arbitrary",) * n_axes,
                                vmem_limit_bytes=VMEM_LIMIT)


def _inproj_kernel(xp_ref, xs_ref, w_ref, op_ref, os_ref, *, n_p):
    i = pl.program_id(0)

    def run(x_ref, o_ref):
        o_ref[...] = jnp.dot(x_ref[...].astype(bf16), w_ref[...], preferred_element_type=f32)

    @pl.when(i < n_p)
    def _():
        run(xp_ref, op_ref)

    @pl.when(i >= n_p)
    def _():
        run(xs_ref, os_ref)


def _inproj(xp, xs, w):
    n_p, n_s = xp.shape[0] // TM, xs.shape[0] // TM
    pmap = lambda i: (jnp.minimum(i, n_p - 1), 0)
    smap = lambda i: (jnp.maximum(i - n_p, 0), 0)
    return pl.pallas_call(
        functools.partial(_inproj_kernel, n_p=n_p),
        grid=(n_p + n_s,),
        in_specs=[
            pl.BlockSpec((TM, D_MODEL), pmap),
            pl.BlockSpec((TM, D_MODEL), smap),
            pl.BlockSpec((D_MODEL, N_PROJ), lambda i: (0, 0)),
        ],
        out_specs=[pl.BlockSpec((TM, N_PROJ), pmap), pl.BlockSpec((TM, N_PROJ), smap)],
        out_shape=[jax.ShapeDtypeStruct((n_p * TM, N_PROJ), f32),
                   jax.ShapeDtypeStruct((n_s * TM, N_PROJ), f32)],
        compiler_params=_cparams(1),
        name="inproj",
    )(xp, xs, w)


def _causal_conv(xs_scr, x, cw_ref, cb_ref, S, L, C):
    xs_scr[:, SUBLANES:SUBLANES + L, :] = x.reshape(S, L, C)
    acc = cb_ref[...] + cw_ref[CONV_W - 1:CONV_W, :] * x
    for j in range(CONV_W - 1):
        lo = SUBLANES - (CONV_W - 1) + j
        acc = acc + cw_ref[j:j + 1, :] * xs_scr[:, lo:lo + L, :].reshape(S * L, C)
    return acc


def _softplus(x):
    return jnp.maximum(x, 0.0) + jnp.log1p(jnp.exp(-jnp.abs(x)))


def _rglru_kernel(x_ref, g_ref, buf_ref, h0_ref, cw_ref, cb_ref, wa_ref, ba_ref, wx_ref, bx_ref,
                  lam_ref, y_ref, hN_ref, cN_ref, xs_scr, *, S, L):
    R = S * L
    t = pl.program_id(1)

    @pl.when(t == 0)
    def _():
        xs_scr[:, 0:SUBLANES, :] = buf_ref[...]
        hN_ref[...] = h0_ref[...]

    x = x_ref[...]
    xc = _causal_conv(xs_scr, x, cw_ref, cb_ref, S, L, D_RG)
    xcb = xc.astype(bf16)
    r = jax.nn.sigmoid(jnp.dot(xcb, wa_ref[...], preferred_element_type=f32) + ba_ref[...])
    ig = jax.nn.sigmoid(jnp.dot(xcb, wx_ref[...], preferred_element_type=f32) + bx_ref[...])
    log_a = (-RG_C) * r * _softplus(-lam_ref[...])
    a = jnp.exp(log_a)
    th = jnp.tanh(log_a)
    u = jnp.sqrt(-2.0 * th / (1.0 - th)) * ig * xc

    tin = lax.broadcasted_iota(jnp.int32, (R, D_RG), 0) % L
    s = 1
    while s < L:
        ok = tin >= s
        a_sh = pltpu.roll(a, s, 0)
        u_sh = pltpu.roll(u, s, 0)
        u = jnp.where(ok, a * u_sh + u, u)
        a = jnp.where(ok, a * a_sh, a)
        s *= 2
    h_prev = jnp.broadcast_to(hN_ref[...], (S, L, D_RG)).reshape(R, D_RG)
    h = a * h_prev + u

    y_ref[...] = h * jax.nn.gelu(g_ref[...], approximate=True)
    hN_ref[...] = h.reshape(S, L, D_RG)[:, L - 1:L, :]
    xs_scr[:, 0:SUBLANES, :] = xs_scr[:, L:L + SUBLANES, :]
    cN_ref[...] = xs_scr[:, SUBLANES - (CONV_W - 1):SUBLANES, :]


def _rglru(proj, row0, B, T, S, L, buf8, h0, cw, cb, wa, ba, wx, bx, lam):
    R = S * L
    nb, nt = B // S, T // L
    blk0 = row0 // R
    row_map = lambda b, t: (blk0 + b * nt + t, 0)
    const2 = lambda b, t: (0, 0)
    return pl.pallas_call(
        functools.partial(_rglru_kernel, S=S, L=L),
        grid=(nb, nt),
        in_specs=[
            pl.BlockSpec((R, D_RG), row_map),
            pl.BlockSpec((R, D_RG), lambda b, t: (blk0 + b * nt + t, 1)),
            pl.BlockSpec((S, SUBLANES, D_RG), lambda b, t: (b, 0, 0)),
            pl.BlockSpec((S, 1, D_RG), lambda b, t: (b, 0, 0)),
            pl.BlockSpec((CONV_W, D_RG), const2),
            pl.BlockSpec((1, D_RG), const2),
            pl.BlockSpec((D_RG, D_RG), const2),
            pl.BlockSpec((1, D_RG), const2),
            pl.BlockSpec((D_RG, D_RG), const2),
            pl.BlockSpec((1, D_RG), const2),
            pl.BlockSpec((1, D_RG), const2),
        ],
        out_specs=[
            pl.BlockSpec((R, D_RG), lambda b, t: (b * nt + t, 0)),
            pl.BlockSpec((S, 1, D_RG), lambda b, t: (b, 0, 0)),
            pl.BlockSpec((S, CONV_W - 1, D_RG), lambda b, t: (b, 0, 0)),
        ],
        out_shape=[
            jax.ShapeDtypeStruct((B * T, D_RG), f32),
            jax.ShapeDtypeStruct((B, 1, D_RG), f32),
            jax.ShapeDtypeStruct((B, CONV_W - 1, D_RG), f32),
        ],
        scratch_shapes=[pltpu.VMEM((S, SUBLANES + L, D_RG), f32)],
        compiler_params=_cparams(2),
        name="rglru",
    )(proj, proj, buf8, h0, cw, cb, wa, ba, wx, bx, lam)


def _seg_scan(x, op, fill, tin, L, reverse=False):
    s = 1
    while s < L:
        if reverse:
            sh = pltpu.roll(x, LANES - s, 1)
            ok = tin < L - s
        else:
            sh = pltpu.roll(x, s, 1)
            ok = tin >= s
        x = op(x, jnp.where(ok, sh, fill))
        s *= 2
    return x


def _mlstm_kernel(x_ref, z_ref, gt_ref, buf_ref, c0_ref, n0_ref, m0_ref,
                  cw_ref, cb_ref, wq_ref, wk_ref, wv_ref, bif_ref, ng_ref, sk_ref,
                  y_ref, cN_ref, nN_ref, mN_ref, bN_ref, xs_scr, m_scr, *, S, L, G):
    for g in range(G):
        sq = pl.ds(g * S, S)
        _mlstm_block(x_ref.at[g], z_ref.at[g], gt_ref.at[g], buf_ref.at[sq], c0_ref.at[sq], n0_ref.at[sq],
                     m0_ref.at[g], cw_ref, cb_ref, wq_ref, wk_ref, wv_ref, bif_ref, ng_ref, sk_ref,
                     y_ref.at[g], cN_ref.at[sq], nN_ref.at[sq], mN_ref.at[g], bN_ref.at[sq],
                     xs_scr.at[sq], m_scr.at[g], S=S, L=L)


def _mlstm_block(x_ref, z_ref, gt_ref, buf_ref, c0_ref, n0_ref, m0_ref,
                 cw_ref, cb_ref, wq_ref, wk_ref, wv_ref, bif_ref, ng_ref, sk_ref,
                 y_ref, cN_ref, nN_ref, mN_ref, bN_ref, xs_scr, m_scr, *, S, L):
    R = S * L
    j = pl.program_id(1)

    @pl.when(j == 0)
    def _():
        xs_scr[:, 0:SUBLANES, :] = buf_ref[...]
        cN_ref[...] = c0_ref[...]
        nN_ref[...] = n0_ref[...]
        m_scr[...] = m0_ref[...]

    x = x_ref[...]
    xc = _causal_conv(xs_scr, x, cw_ref, cb_ref, S, L, D_M)
    xa = xc * jax.nn.sigmoid(xc)
    xab = xa.astype(bf16)
    xb = x.astype(bf16)

    gT = (gt_ref[...] + bif_ref[...]).T
    il = gT[0:SUBLANES, :]
    fpre = pltpu.roll(il, M_HEADS, 0)
    fl = -_softplus(-fpre)
    tin = lax.broadcasted_iota(jnp.int32, (SUBLANES, LANES), 1) % L
    bcum = _seg_scan(fl, jnp.add, 0.0, tin, L)
    a = il - bcum
    m_prev = m_scr[...]
    big_m = jnp.maximum(m_prev, _seg_scan(a, jnp.maximum, -jnp.inf, tin, L))
    m_t = bcum + big_m
    m_last = _seg_scan(big_m, jnp.maximum, -jnp.inf, tin, L, reverse=True)
    rows = [big_m, jnp.exp(m_prev - big_m), jnp.exp(-m_t), jnp.exp(a - m_last), jnp.exp(m_prev - m_last)]
    n_q = len(rows)
    packed = jnp.concatenate(rows + [jnp.zeros((LANES - n_q * SUBLANES, LANES), f32)], axis=0)
    cols = packed.T

    def col(q, h):
        return cols[:, SUBLANES * q + h:SUBLANES * q + h + 1]

    mN_ref[...] = m_t
    m_scr[...] = jnp.broadcast_to(m_t[:, LANES - 1:LANES], (SUBLANES, LANES))

    ti = lax.broadcasted_iota(jnp.int32, (R, R), 0)
    si = lax.broadcasted_iota(jnp.int32, (R, R), 1)
    mask = (si <= ti) & ((ti // L) == (si // L))

    hn_heads = []
    for h in range(M_HEADS):
        hs = slice(h * M_DH, (h + 1) * M_DH)
        q = jnp.dot(xab[:, hs], wq_ref[h], preferred_element_type=f32)
        k = jnp.dot(xab[:, hs], wk_ref[h], preferred_element_type=f32) * (M_DH ** -0.5)
        v = jnp.dot(xb[:, hs], wv_ref[h], preferred_element_type=f32)
        qb, kb, vb = q.astype(bf16), k.astype(bf16), v.astype(bf16)
        qk = lax.dot_general(qb, kb, (((1,), (1,)), ((), ())), preferred_element_type=f32)
        sm = qk * jnp.exp(jnp.where(mask, a[h:h + 1, :] - col(0, h), -jnp.inf))
        num_in = jnp.dot(sm.astype(bf16), vb, preferred_element_type=f32)
        den_in = jnp.sum(sm, axis=1, keepdims=True)
        g_col, e_col, w_col = col(1, h), col(2, h), col(3, h)
        wv = (w_col * v).astype(bf16)
        wk = w_col * k
        pieces = []
        for b in range(S):
            rs = slice(b * L, (b + 1) * L)
            c_old = cN_ref[b, h]
            n_old = nN_ref[b, h]
            q_c = jnp.dot(qb[rs], c_old.astype(bf16), preferred_element_type=f32)
            q_n = jnp.sum(q[rs] * n_old, axis=1, keepdims=True)
            num = num_in[rs] + g_col[rs] * q_c
            den = den_in[rs] + g_col[rs] * q_n
            pieces.append(num / jnp.maximum(jnp.abs(den), e_col[rs]))
            g_end = col(4, h)[(b + 1) * L - 1:(b + 1) * L, :]
            kv = lax.dot_general(kb[rs], wv[rs], (((0,), (0,)), ((), ())), preferred_element_type=f32)
            cN_ref[b, h] = g_end * c_old + kv
            nN_ref[b, h] = g_end * n_old + jnp.sum(wk[rs], axis=0, keepdims=True)
        hh = pieces[0] if S == 1 else jnp.concatenate(pieces, axis=0)
        mu = jnp.mean(hh, axis=1, keepdims=True)
        var = jnp.mean(jnp.square(hh - mu), axis=1, keepdims=True)
        hn_heads.append((hh - mu) * lax.rsqrt(var + LN_EPS))
    hn = jnp.concatenate(hn_heads, axis=1) * ng_ref[...]
    y_ref[...] = jax.nn.sigmoid(z_ref[...]) * (hn + sk_ref[...] * xa)
    xs_scr[:, 0:SUBLANES, :] = xs_scr[:, L:L + SUBLANES, :]
    bN_ref[...] = xs_scr[:, SUBLANES - (CONV_W - 1):SUBLANES, :]


def _mlstm(proj, B, T, S, L, G, buf8, c0, n0, m0_lanes, cw, cb, wq, wk, wv, bif, ng, sk):
    R = S * L
    assert R == M_ROWS == LANES
    nblk, nc = B // S, T // L
    nb = nblk // G
    assert nblk * S == B and nc * L == T and nb * G == nblk and (S == 1 or nc == 1)
    proj4 = proj.reshape(nblk, nc, R, N_PROJ)
    m0_3 = m0_lanes.reshape(SUBLANES, nblk, LANES).transpose(1, 0, 2)
    const2 = lambda b, j: (0, 0)
    const3 = lambda b, j: (0, 0, 0)
    gate_blk = (2 * D_RG + 2 * D_M) // LANES
    GS = G * S
    outs = pl.pallas_call(
        functools.partial(_mlstm_kernel, S=S, L=L, G=G),
        grid=(nb, nc),
        in_specs=[
            pl.BlockSpec((G, None, R, D_M), lambda b, j: (b, j, 0, 2)),
            pl.BlockSpec((G, None, R, D_M), lambda b, j: (b, j, 0, 3)),
            pl.BlockSpec((G, None, R, LANES), lambda b, j: (b, j, 0, gate_blk)),
            pl.BlockSpec((GS, SUBLANES, D_M), lambda b, j: (b, 0, 0)),
            pl.BlockSpec((GS, M_HEADS, M_DH, M_DH), lambda b, j: (b, 0, 0, 0)),
            pl.BlockSpec((GS, M_HEADS, 1, M_DH), lambda b, j: (b, 0, 0, 0)),
            pl.BlockSpec((G, SUBLANES, LANES), lambda b, j: (b, 0, 0)),
            pl.BlockSpec((CONV_W, D_M), const2),
            pl.BlockSpec((1, D_M), const2),
            pl.BlockSpec((M_HEADS, M_DH, M_DH), const3),
            pl.BlockSpec((M_HEADS, M_DH, M_DH), const3),
            pl.BlockSpec((M_HEADS, M_DH, M_DH), const3),
            pl.BlockSpec((1, LANES), const2),
            pl.BlockSpec((1, D_M), const2),
            pl.BlockSpec((1, D_M), const2),
        ],
        out_specs=[
            pl.BlockSpec((G, None, R, D_M), lambda b, j: (b, j, 0, 0)),
            pl.BlockSpec((GS, M_HEADS, M_DH, M_DH), lambda b, j: (b, 0, 0, 0)),
            pl.BlockSpec((GS, M_HEADS, 1, M_DH), lambda b, j: (b, 0, 0, 0)),
            pl.BlockSpec((G, None, SUBLANES, LANES), lambda b, j: (b, j, 0, 0)),
            pl.BlockSpec((GS, CONV_W - 1, D_M), lambda b, j: (b, 0, 0)),
        ],
        out_shape=[
            jax.ShapeDtypeStruct((nblk, nc, R, D_M), f32),
            jax.ShapeDtypeStruct((B, M_HEADS, M_DH, M_DH), f32),
            jax.ShapeDtypeStruct((B, M_HEADS, 1, M_DH), f32),
            jax.ShapeDtypeStruct((nblk, nc, SUBLANES, LANES), f32),
            jax.ShapeDtypeStruct((B, CONV_W - 1, D_M), f32),
        ],
        scratch_shapes=[pltpu.VMEM((GS, SUBLANES + L, D_M), f32), pltpu.VMEM((G, SUBLANES, LANES), f32)],
        compiler_params=_cparams(2),
        name="mlstm",
    )(proj4, proj4, proj4, buf8, c0, n0, m0_3, cw, cb, wq, wk, wv, bif, ng, sk)
    y, c_new, n_new, m_t, b_new = outs
    m_last = m_t[:, nc - 1, :M_HEADS, :].reshape(nblk, M_HEADS, S, L)[:, :, :, L - 1]
    m_last = m_last.transpose(0, 2, 1).reshape(B, M_HEADS)
    return y.reshape(B * T, D_M), c_new, n_new, m_last, b_new


def _layer_norm(x, g, b):
    mu = jnp.mean(x, axis=-1, keepdims=True)
    var = jnp.mean(jnp.square(x - mu), axis=-1, keepdims=True)
    return (x - mu) * lax.rsqrt(var + LN_EPS) * g + b


def _first_lane_of_max(vals, lane_f):
    vmax = jnp.max(vals, axis=1, keepdims=True)
    idx = jnp.min(jnp.where(vals == vmax, lane_f, float(LANES)), axis=1, keepdims=True)
    return vmax, idx


def _outproj_kernel(xp_ref, xs_ref, rgp_ref, rgs_ref, mp_ref, ms_ref, wo_ref, g1_ref, b1_ref,
                    wrh_ref, wrl_ref, br_ref, x1_ref, info_ref, cnt_ref, *, n_p):
    i = pl.program_id(0)

    @pl.when(i == 0)
    def _():
        cnt_ref[...] = jnp.zeros_like(cnt_ref)

    def run(x_ref, rg_ref, m_ref):
        mix = jnp.dot(rg_ref[...].astype(bf16), wo_ref[0:D_RG, :], preferred_element_type=f32)
        mix = mix + jnp.dot(m_ref[...].astype(bf16), wo_ref[D_RG:D_RG + D_M, :], preferred_element_type=f32)
        x1 = _layer_norm(ALPHA * x_ref[...] + mix, g1_ref[...], b1_ref[...])
        x1_ref[...] = x1

        hi = x1.astype(bf16)
        lo = (x1 - hi.astype(f32)).astype(bf16)
        lg = (jnp.dot(hi, wrh_ref[...], preferred_element_type=f32)
              + jnp.dot(lo, wrh_ref[...], preferred_element_type=f32)
              + jnp.dot(hi, wrl_ref[...], preferred_element_type=f32)) + br_ref[...]
        lane = lax.broadcasted_iota(jnp.int32, (TM, LANES), 1)
        lane_f = lane.astype(f32)
        neg = -jnp.inf
        gl = jnp.where(lane < N_GROUPS, lg, neg)
        gmax, gidx = _first_lane_of_max(gl, lane_f)
        p_g = 1.0 / jnp.sum(jnp.exp(gl - gmax), axis=1, keepdims=True)
        e_lo = float(N_GROUPS) + float(EXPERTS_PER_GROUP) * gidx
        el = jnp.where((lane_f >= e_lo) & (lane_f < e_lo + float(EXPERTS_PER_GROUP)), lg, neg)
        v1, i1 = _first_lane_of_max(el, lane_f)
        v2, i2 = _first_lane_of_max(jnp.where(lane_f == i1, neg, el), lane_f)
        d = jnp.exp(v2 - v1)
        w1 = p_g / (1.0 + d)
        w2 = p_g * d / (1.0 + d)
        id1 = i1 - float(N_GROUPS)
        id2 = i2 - float(N_GROUPS)

        hot1 = lane_f == id1
        hot2 = lane_f == id2
        onehot = (hot1 | hot2).astype(bf16)
        ti = lax.broadcasted_iota(jnp.int32, (TM, TM), 0)
        si = lax.broadcasted_iota(jnp.int32, (TM, TM), 1)
        tri = (si <= ti).astype(bf16)
        cum = jnp.dot(tri, onehot, preferred_element_type=f32)
        rank = cum - 1.0 + cnt_ref[...]
        r1 = jnp.sum(jnp.where(hot1, rank, 0.0), axis=1, keepdims=True)
        r2 = jnp.sum(jnp.where(hot2, rank, 0.0), axis=1, keepdims=True)
        cnt_ref[...] = cnt_ref[...] + cum[TM - 1:TM, :]

        info = jnp.zeros((TM, LANES), f32)
        for c, val in enumerate((id1, id2, w1, w2, r1, r2)):
            info = jnp.where(lane == c, val, info)
        info_ref[...] = info

    @pl.when(i < n_p)
    def _():
        run(xp_ref, rgp_ref, mp_ref)

    @pl.when(i >= n_p)
    def _():
        run(xs_ref, rgs_ref, ms_ref)


def _outproj(xp, xs, rgp, rgs, mp, ms, wo, g1, b1, wrh, wrl, br):
    n_p, n_s = xp.shape[0] // TM, xs.shape[0] // TM
    n = n_p + n_s
    pmap = lambda i: (jnp.minimum(i, n_p - 1), 0)
    smap = lambda i: (jnp.maximum(i - n_p, 0), 0)
    const = lambda i: (0, 0)
    return pl.pallas_call(
        functools.partial(_outproj_kernel, n_p=n_p),
        grid=(n,),
        in_specs=[
            pl.BlockSpec((TM, D_MODEL), pmap), pl.BlockSpec((TM, D_MODEL), smap),
            pl.BlockSpec((TM, D_RG), pmap), pl.BlockSpec((TM, D_RG), smap),
            pl.BlockSpec((TM, D_M), pmap), pl.BlockSpec((TM, D_M), smap),
            pl.BlockSpec((D_RG + D_M, D_MODEL), const),
            pl.BlockSpec((1, D_MODEL), const), pl.BlockSpec((1, D_MODEL), const),
            pl.BlockSpec((D_MODEL, LANES), const), pl.BlockSpec((D_MODEL, LANES), const),
            pl.BlockSpec((1, LANES), const),
        ],
        out_specs=[
            pl.BlockSpec((TM, D_MODEL), lambda i: (i, 0)),
            pl.BlockSpec((TM, LANES), lambda i: (i, 0)),
            pl.BlockSpec((1, LANES), const),
        ],
        out_shape=[
            jax.ShapeDtypeStruct((n * TM, D_MODEL), f32),
            jax.ShapeDtypeStruct((n * TM, LANES), f32),
            jax.ShapeDtypeStruct((1, LANES), f32),
        ],
        compiler_params=_cparams(1),
        name="outproj",
    )(xp, xs, rgp, rgs, mp, ms, wo, g1, b1, wrh, wrl, br)


def _row_copy(src_ref, src_row, dst_ref, dst_row, sem):
    return pltpu.make_async_copy(src_ref.at[pl.ds(src_row, 1)], dst_ref.at[pl.ds(dst_row, 1)], sem)


def _tile_rows_wait(src_ref, dst_ref, sem):
    pltpu.make_async_copy(src_ref.at[pl.ds(0, TM)], dst_ref.at[pl.ds(0, TM)], sem).wait()


def _dispatch_kernel(lt_ref, pos_ref, x1_ref, xs_ref, zero_scr, xbuf, sem, zsem, lsem):
    i = pl.program_id(0)
    n = pl.num_programs(0)

    def zero_copy(e):
        row0 = pl.multiple_of(lt_ref[e], TM)
        return pltpu.make_async_copy(zero_scr, xs_ref.at[pl.ds(row0, TM)], zsem)

    def zero_tile(t):
        return pltpu.make_async_copy(zero_scr, xs_ref.at[pl.ds(pl.multiple_of(t * TM, TM), TM)], zsem)

    @pl.when(i == 0)
    def _():
        zero_scr[...] = jnp.zeros_like(zero_scr)
        for e in range(N_EXPERTS):
            @pl.when(lt_ref[e] >= 0)
            def _():
                zero_copy(e).start()
        n_used = lt_ref[N_EXPERTS]
        n_tiles = xs_ref.shape[0] // TM
        lax.fori_loop(n_used, n_tiles, lambda t, c: (zero_tile(t).start(), c)[1], 0)
        for e in range(N_EXPERTS):
            @pl.when(lt_ref[e] >= 0)
            def _():
                zero_copy(e).wait()
        lax.fori_loop(n_used, n_tiles, lambda t, c: (zero_tile(t).wait(), c)[1], 0)

    def load(t, s):
        return pltpu.make_async_copy(x1_ref.at[pl.ds(pl.multiple_of(t * TM, TM), TM)], xbuf.at[s], lsem.at[s])

    @pl.when(i == 0)
    def _():
        load(0, 0).start()

    @pl.when(i + 1 < n)
    def _():
        load(i + 1, (i + 1) % 3).start()

    slot = i % 3
    load(i, slot).wait()

    def start(r, c):
        _row_copy(xbuf.at[slot], r, xs_ref, pos_ref[0, 0, 2 * r], sem.at[slot]).start()
        _row_copy(xbuf.at[slot], r, xs_ref, pos_ref[0, 0, 2 * r + 1], sem.at[slot]).start()
        return c

    lax.fori_loop(0, TM, start, 0, unroll=8)

    def wait_rows(s):
        _tile_rows_wait(xbuf.at[s], xs_ref, sem.at[s])
        _tile_rows_wait(xbuf.at[s], xs_ref, sem.at[s])

    @pl.when(i > 0)
    def _():
        wait_rows((i + 2) % 3)

    @pl.when(i == n - 1)
    def _():
        wait_rows(slot)


def _dispatch(last_tile_row, pos, x1, n_rows):
    n = x1.shape[0] // TM
    return pl.pallas_call(
        _dispatch_kernel,
        grid_spec=pltpu.PrefetchScalarGridSpec(
            num_scalar_prefetch=1,
            grid=(n,),
            in_specs=[
                pl.BlockSpec((1, 1, 2 * TM), lambda i, lt: (i, 0, 0), memory_space=pltpu.SMEM),
                pl.BlockSpec(memory_space=pl.ANY),
            ],
            out_specs=pl.BlockSpec(memory_space=pl.ANY),
            scratch_shapes=[pltpu.VMEM((TM, D_MODEL), f32), pltpu.VMEM((3, TM, D_MODEL), f32),
                            pltpu.SemaphoreType.DMA((3,)), pltpu.SemaphoreType.DMA(()),
                            pltpu.SemaphoreType.DMA((3,))],
        ),
        out_shape=jax.ShapeDtypeStruct((n_rows, D_MODEL), f32),
        compiler_params=_cparams(1),
        name="dispatch",
    )(last_tile_row, pos, x1)


def _expert_kernel(te_ref, nu_ref, x_ref, wg_ref, wu_ref, wd_ref, y_ref):
    @pl.when(pl.program_id(0) < nu_ref[0])
    def _():
        x = x_ref[...].astype(bf16)
        hg = jnp.dot(x, wg_ref[0].astype(bf16), preferred_element_type=f32)
        hu = jnp.dot(x, wu_ref[0].astype(bf16), preferred_element_type=f32)
        hmid = (hg * jax.nn.sigmoid(hg) * hu).astype(bf16)
        y_ref[...] = jnp.dot(hmid, wd_ref[0].astype(bf16), preferred_element_type=f32)

    @pl.when(pl.program_id(0) >= nu_ref[0])
    def _():
        y_ref[...] = jnp.zeros_like(y_ref)


def _experts(tile_expert, n_used, xs, wg, wu, wd):
    nt = xs.shape[0] // TM
    wmap = lambda i, te, nu: (te[i], 0, 0)
    xmap = lambda i, te, nu: (jnp.minimum(i, nu[0] - 1), 0)
    return pl.pallas_call(
        _expert_kernel,
        grid_spec=pltpu.PrefetchScalarGridSpec(
            num_scalar_prefetch=2,
            grid=(nt,),
            in_specs=[
                pl.BlockSpec((TM, D_MODEL), xmap),
                pl.BlockSpec((1, D_MODEL, D_EXPERT), wmap),
                pl.BlockSpec((1, D_MODEL, D_EXPERT), wmap),
                pl.BlockSpec((1, D_EXPERT, D_MODEL), wmap),
            ],
            out_specs=pl.BlockSpec((TM, D_MODEL), lambda i, te, nu: (i, 0)),
        ),
        out_shape=jax.ShapeDtypeStruct(xs.shape, f32),
        compiler_params=_cparams(1),
        name="experts",
    )(tile_expert, n_used, xs, wg, wu, wd)


def _combine_kernel(pos_ref, posn_ref, x1_ref, info_ref, ys_ref, g2_ref, b2_ref, op_ref, os_ref,
                    ya_scr, yb_scr, sem, *, n_p):
    i = pl.program_id(0)
    n = pl.num_programs(0)
    slot = i % 2

    def gather(p_ref, s):
        def start(r, c):
            _row_copy(ys_ref, p_ref[0, 0, 2 * r], ya_scr.at[s], r, sem.at[s]).start()
            _row_copy(ys_ref, p_ref[0, 0, 2 * r + 1], yb_scr.at[s], r, sem.at[s]).start()
            return c

        lax.fori_loop(0, TM, start, 0, unroll=8)

    @pl.when(i == 0)
    def _():
        gather(pos_ref, 0)

    @pl.when(i + 1 < n)
    def _():
        gather(posn_ref, 1 - slot)

    _tile_rows_wait(ys_ref, ya_scr.at[slot], sem.at[slot])
    _tile_rows_wait(ys_ref, yb_scr.at[slot], sem.at[slot])

    info = info_ref[...]
    ffn = info[:, 2:3] * ya_scr[slot] + info[:, 3:4] * yb_scr[slot]
    out = _layer_norm(ALPHA * x1_ref[...] + ffn, g2_ref[...], b2_ref[...])

    @pl.when(i < n_p)
    def _():
        op_ref[...] = out

    @pl.when(i >= n_p)
    def _():
        os_ref[...] = out


def _combine(pos, x1, info, ys, g2, b2, n_p):
    n = x1.shape[0] // TM
    n_s = n - n_p
    const = lambda i: (0, 0)
    return pl.pallas_call(
        functools.partial(_combine_kernel, n_p=n_p),
        grid=(n,),
        in_specs=[
            pl.BlockSpec((1, 1, 2 * TM), lambda i: (i, 0, 0), memory_space=pltpu.SMEM),
            pl.BlockSpec((1, 1, 2 * TM), lambda i: (jnp.minimum(i + 1, n - 1), 0, 0),
                         memory_space=pltpu.SMEM),
            pl.BlockSpec((TM, D_MODEL), lambda i: (i, 0)),
            pl.BlockSpec((TM, LANES), lambda i: (i, 0)),
            pl.BlockSpec(memory_space=pl.ANY),
            pl.BlockSpec((1, D_MODEL), const), pl.BlockSpec((1, D_MODEL), const),
        ],
        out_specs=[
            pl.BlockSpec((TM, D_MODEL), lambda i: (jnp.minimum(i, n_p - 1), 0)),
            pl.BlockSpec((TM, D_MODEL), lambda i: (jnp.maximum(i - n_p, 0), 0)),
        ],
        out_shape=[
            jax.ShapeDtypeStruct((n_p * TM, D_MODEL), f32),
            jax.ShapeDtypeStruct((n_s * TM, D_MODEL), f32),
        ],
        scratch_shapes=[pltpu.VMEM((2, TM, D_MODEL), f32), pltpu.VMEM((2, TM, D_MODEL), f32),
                        pltpu.SemaphoreType.DMA((2,))],
        compiler_params=_cparams(1),
        name="combine",
    )(pos, pos, x1, info, ys, g2, b2)


def _block_diag(w):
    n, d, _ = w.shape
    eye = jnp.eye(n, dtype=w.dtype)
    return (eye[:, None, :, None] * w[:, :, None, :]).reshape(n * d, n * d)


def _pad_history(buf):
    return jnp.pad(buf, ((0, 0), (SUBLANES - (CONV_W - 1), 0), (0, 0)))


def kernel(x_prompt, x_sample, state_rg_h, state_rg_conv, state_m_C, state_m_n, state_m_m, state_m_conv, w_in, rg_conv_w, rg_conv_b, rg_w_a, rg_b_a, rg_w_x, rg_b_x, rg_lambda, m_conv_w, m_conv_b, m_w_q, m_w_k, m_w_v, m_b_i, m_b_f, m_norm_g, m_skip, w_out, ln1_g, ln1_b, ln2_g, ln2_b, moe_w_group, moe_b_group, moe_w_expert, moe_b_expert, moe_w_gate, moe_w_up, moe_w_down):
    BP, TP, _ = x_prompt.shape
    BS, TS, _ = x_sample.shape
    n_prompt, n_sample = BP * TP, BS * TS
    xp = x_prompt.reshape(n_prompt, D_MODEL)
    xs = x_sample.reshape(n_sample, D_MODEL)
    l = 0

    w_in_p = jnp.pad(w_in[l], ((0, 0), (0, N_PROJ - w_in.shape[-1]))).astype(bf16)
    wa = _block_diag(rg_w_a[l]).astype(bf16)
    wx = _block_diag(rg_w_x[l]).astype(bf16)
    row = lambda v: v.reshape(1, -1)
    bif = jnp.pad(jnp.concatenate([m_b_i[l], m_b_f[l]]), (0, LANES - 2 * M_HEADS)).reshape(1, LANES)
    wq, wk, wv = m_w_q[l].astype(bf16), m_w_k[l].astype(bf16), m_w_v[l].astype(bf16)
    w_route = jnp.pad(jnp.concatenate([moe_w_group[l], moe_w_expert[l]], axis=1),
                      ((0, 0), (0, LANES - N_GROUPS - N_EXPERTS)))
    wr_hi = w_route.astype(bf16)
    wr_lo = (w_route - wr_hi.astype(f32)).astype(bf16)
    b_route = jnp.pad(jnp.concatenate([moe_b_group[l], moe_b_expert[l]]),
                      (0, LANES - N_GROUPS - N_EXPERTS)).reshape(1, LANES)

    proj_p, proj_s = _inproj(xp, xs, w_in_p)

    rg_args = (rg_conv_w[l], row(rg_conv_b[l]), wa, row(rg_b_a[l]), wx, row(rg_b_x[l]), row(rg_lambda[l]))
    zeros = lambda *s: jnp.zeros(s, f32)
    s_blk = M_ROWS // TS
    yrg_p, p_rg_h, p_rg_conv = _rglru(proj_p, 0, BP, TP, 1, RG_ROWS_PROMPT, zeros(BP, SUBLANES, D_RG),
                                      zeros(BP, 1, D_RG), *rg_args)
    yrg_s, s_rg_h, s_rg_conv = _rglru(proj_s, 0, BS, TS, s_blk, TS, _pad_history(state_rg_conv[l]),
                                      state_rg_h[l].reshape(BS, 1, D_RG), *rg_args)

    m_args = (m_conv_w[l], row(m_conv_b[l]), wq, wk, wv, bif, row(m_norm_g[l]), row(m_skip[l]))
    m0_p = jnp.full((SUBLANES, BP * LANES), M_INIT, f32)
    ym_p, p_m_C, p_m_n, p_m_m, p_m_conv = _mlstm(
        proj_p, BP, TP, 1, M_ROWS, M_BLOCKS_PROMPT, zeros(BP, SUBLANES, D_M), zeros(BP, M_HEADS, M_DH, M_DH),
        zeros(BP, M_HEADS, 1, M_DH), m0_p, *m_args)
    m0_s = jnp.pad(jnp.repeat(state_m_m[l].T, TS, axis=1), ((0, SUBLANES - M_HEADS), (0, 0)))
    ym_s, s_m_C, s_m_n, s_m_m, s_m_conv = _mlstm(
        proj_s, BS, TS, s_blk, TS, M_BLOCKS_SAMPLE, _pad_history(state_m_conv[l]), state_m_C[l],
        state_m_n[l].reshape(BS, M_HEADS, 1, M_DH), m0_s, *m_args)

    x1, info, cnt = _outproj(xp, xs, yrg_p, yrg_s, ym_p, ym_s, w_out[l].astype(bf16),
                             row(ln1_g[l]), row(ln1_b[l]), wr_hi, wr_lo, b_route)

    n_tok = n_prompt + n_sample
    n_tiles = (n_tok * 2) // TM + N_EXPERTS
    counts = cnt[0, :N_EXPERTS].astype(jnp.int32)
    padded = ((counts + TM - 1) // TM) * TM
    ends = jnp.cumsum(padded)
    offs = ends - padded
    experts = jnp.arange(N_EXPERTS, dtype=jnp.int32)
    ids = info[:, 0:2].astype(jnp.int32)
    offs_of_id = jnp.sum(jnp.where(ids[:, :, None] == experts, offs, 0), axis=-1)
    pos = (offs_of_id + info[:, 4:6].astype(jnp.int32)).reshape(n_tok // TM, 1, 2 * TM)
    tiles = jnp.arange(n_tiles, dtype=jnp.int32)
    tile_expert = jnp.minimum(jnp.sum(tiles[:, None] >= (ends // TM)[None, :], axis=1),
                              N_EXPERTS - 1).astype(jnp.int32)
    n_used = (ends[-1] // TM).reshape(1).astype(jnp.int32)
    last_tile_row = jnp.concatenate([jnp.where(counts > 0, ends - TM, -1).astype(jnp.int32), n_used])

    x_sorted = _dispatch(last_tile_row, pos, x1, n_tiles * TM)
    y_sorted = _experts(tile_expert, n_used, x_sorted, moe_w_gate[l], moe_w_up[l], moe_w_down[l])
    y_p, y_s = _combine(pos, x1, info, y_sorted, row(ln2_g[l]), row(ln2_b[l]), n_prompt // TM)

    return (y_p.reshape(BP, TP, D_MODEL), y_s.reshape(BS, TS, D_MODEL),
            p_rg_h.reshape(1, BP, D_RG), p_rg_conv[None], p_m_C[None], p_m_n.reshape(1, BP, M_HEADS, M_DH),
            p_m_m[None], p_m_conv[None],
            s_rg_h.reshape(1, BS, D_RG), s_rg_conv[None], s_m_C[None], s_m_n.reshape(1, BS, M_HEADS, M_DH),
            s_m_m[None], s_m_conv[None])
```

```python
import functools

import jax
import jax.numpy as jnp
from jax import lax
from jax.experimental import pallas as pl
from jax.experimental.pallas import tpu as pltpu

f32 = jnp.float32
bf16 = jnp.bfloat16

D_MODEL = 1024
D_RG = 512
RG_BLOCKS = 8
RG_C = 8.0
D_M = 512
M_HEADS = 4
M_DH = 128
CONV_W = 4
N_GROUPS = 4
EXPERTS_PER_GROUP = 8
N_EXPERTS = 32
D_EXPERT = 256
ALPHA = 2.0 ** 0.25
LN_EPS = 1e-5
M_INIT = -1.0e4

LANES = 128
SUBLANES = 8
TM = 256
N_PROJ = 2 * D_RG + 2 * D_M + LANES
RG_ROWS_PROMPT = 256
M_ROWS = 128
M_BLOCKS_PROMPT = 8
M_BLOCKS_SAMPLE = 2
VMEM_LIMIT = 56 * 1024 * 1024


def _cparams(n_axes):
    return pltpu.CompilerParams(dimension_semantics=("arbitrary",) * n_axes,
                                vmem_limit_bytes=VMEM_LIMIT)


def _inproj_kernel(xp_ref, xs_ref, w_ref, op_ref, os_ref, *, n_p):
    i = pl.program_id(0)

    def run(x_ref, o_ref):
        o_ref[...] = jnp.dot(x_ref[...].astype(bf16), w_ref[...], preferred_element_type=f32)

    @pl.when(i < n_p)
    def _():
        run(xp_ref, op_ref)

    @pl.when(i >= n_p)
    def _():
        run(xs_ref, os_ref)


def _inproj(xp, xs, w):
    n_p, n_s = xp.shape[0] // TM, xs.shape[0] // TM
    pmap = lambda i: (jnp.minimum(i, n_p - 1), 0)
    smap = lambda i: (jnp.maximum(i - n_p, 0), 0)
    return pl.pallas_call(
        functools.partial(_inproj_kernel, n_p=n_p),
        grid=(n_p + n_s,),
        in_specs=[
            pl.BlockSpec((TM, D_MODEL), pmap),
            pl.BlockSpec((TM, D_MODEL), smap),
            pl.BlockSpec((D_MODEL, N_PROJ), lambda i: (0, 0)),
        ],
        out_specs=[pl.BlockSpec((TM, N_PROJ), pmap), pl.BlockSpec((TM, N_PROJ), smap)],
        out_shape=[jax.ShapeDtypeStruct((n_p * TM, N_PROJ), f32),
                   jax.ShapeDtypeStruct((n_s * TM, N_PROJ), f32)],
        compiler_params=_cparams(1),
        name="inproj",
    )(xp, xs, w)


def _causal_conv(xs_scr, x, cw_ref, cb_ref, S, L, C):
    xs_scr[:, SUBLANES:SUBLANES + L, :] = x.reshape(S, L, C)
    acc = cb_ref[...] + cw_ref[CONV_W - 1:CONV_W, :] * x
    for j in range(CONV_W - 1):
        lo = SUBLANES - (CONV_W - 1) + j
        acc = acc + cw_ref[j:j + 1, :] * xs_scr[:, lo:lo + L, :].reshape(S * L, C)
    return acc


def _softplus(x):
    return jnp.maximum(x, 0.0) + jnp.log1p(jnp.exp(-jnp.abs(x)))


def _rglru_kernel(x_ref, g_ref, buf_ref, h0_ref, cw_ref, cb_ref, wa_ref, ba_ref, wx_ref, bx_ref,
                  lam_ref, y_ref, hN_ref, cN_ref, xs_scr, *, S, L):
    R = S * L
    t = pl.program_id(1)

    @pl.when(t == 0)
    def _():
        xs_scr[:, 0:SUBLANES, :] = buf_ref[...]
        hN_ref[...] = h0_ref[...]

    x = x_ref[...]
    xc = _causal_conv(xs_scr, x, cw_ref, cb_ref, S, L, D_RG)
    xcb = xc.astype(bf16)
    r = jax.nn.sigmoid(jnp.dot(xcb, wa_ref[...], preferred_element_type=f32) + ba_ref[...])
    ig = jax.nn.sigmoid(jnp.dot(xcb, wx_ref[...], preferred_element_type=f32) + bx_ref[...])
    log_a = (-RG_C) * r * _softplus(-lam_ref[...])
    a = jnp.exp(log_a)
    th = jnp.tanh(log_a)
    u = jnp.sqrt(-2.0 * th / (1.0 - th)) * ig * xc

    tin = lax.broadcasted_iota(jnp.int32, (R, D_RG), 0) % L
    s = 1
    while s < L:
        ok = tin >= s
        a_sh = pltpu.roll(a, s, 0)
        u_sh = pltpu.roll(u, s, 0)
        u = jnp.where(ok, a * u_sh + u, u)
        a = jnp.where(ok, a * a_sh, a)
        s *= 2
    h_prev = jnp.broadcast_to(hN_ref[...], (S, L, D_RG)).reshape(R, D_RG)
    h = a * h_prev + u

    y_ref[...] = h * jax.nn.gelu(g_ref[...], approximate=True)
    hN_ref[...] = h.reshape(S, L, D_RG)[:, L - 1:L, :]
    xs_scr[:, 0:SUBLANES, :] = xs_scr[:, L:L + SUBLANES, :]
    cN_ref[...] = xs_scr[:, SUBLANES - (CONV_W - 1):SUBLANES, :]


def _rglru(proj, row0, B, T, S, L, buf8, h0, cw, cb, wa, ba, wx, bx, lam):
    R = S * L
    nb, nt = B // S, T // L
    blk0 = row0 // R
    row_map = lambda b, t: (blk0 + b * nt + t, 0)
    const2 = lambda b, t: (0, 0)
    return pl.pallas_call(
        functools.partial(_rglru_kernel, S=S, L=L),
        grid=(nb, nt),
        in_specs=[
            pl.BlockSpec((R, D_RG), row_map),
            pl.BlockSpec((R, D_RG), lambda b, t: (blk0 + b * nt + t, 1)),
            pl.BlockSpec((S, SUBLANES, D_RG), lambda b, t: (b, 0, 0)),
            pl.BlockSpec((S, 1, D_RG), lambda b, t: (b, 0, 0)),
            pl.BlockSpec((CONV_W, D_RG), const2),
            pl.BlockSpec((1, D_RG), const2),
            pl.BlockSpec((D_RG, D_RG), const2),
            pl.BlockSpec((1, D_RG), const2),
            pl.BlockSpec((D_RG, D_RG), const2),
            pl.BlockSpec((1, D_RG), const2),
            pl.BlockSpec((1, D_RG), const2),
        ],
        out_specs=[
            pl.BlockSpec((R, D_RG), lambda b, t: (b * nt + t, 0)),
            pl.BlockSpec((S, 1, D_RG), lambda b, t: (b, 0, 0)),
            pl.BlockSpec((S, CONV_W - 1, D_RG), lambda b, t: (b, 0, 0)),
        ],
        out_shape=[
            jax.ShapeDtypeStruct((B * T, D_RG), f32),
            jax.ShapeDtypeStruct((B, 1, D_RG), f32),
            jax.ShapeDtypeStruct((B, CONV_W - 1, D_RG), f32),
        ],
        scratch_shapes=[pltpu.VMEM((S, SUBLANES + L, D_RG), f32)],
        compiler_params=_cparams(2),
        name="rglru",
    )(proj, proj, buf8, h0, cw, cb, wa, ba, wx, bx, lam)


def _seg_scan(x, op, fill, tin, L, reverse=False):
    s = 1
    while s < L:
        if reverse:
            sh = pltpu.roll(x, LANES - s, 1)
            ok = tin < L - s
        else:
            sh = pltpu.roll(x, s, 1)
            ok = tin >= s
        x = op(x, jnp.where(ok, sh, fill))
        s *= 2
    return x


def _mlstm_kernel(x_ref, z_ref, gt_ref, buf_ref, c0_ref, n0_ref, m0_ref,
                  cw_ref, cb_ref, wqk_ref, wv_ref, wkT_ref, bif_ref, ng_ref, sk_ref,
                  y_ref, cN_ref, nN_ref, mN_ref, bN_ref, xs_scr, m_scr, *, S, L, G):
    R = S * L
    GS, GR = G * S, G * R
    j = pl.program_id(1)

    @pl.when(j == 0)
    def _():
        xs_scr[:, 0:SUBLANES, :] = buf_ref[...]
        cN_ref[...] = c0_ref[...]
        nN_ref[...] = n0_ref[...]
        m_scr[...] = m0_ref[...]

    x = x_ref[...].reshape(GR, D_M)
    xc = _causal_conv(xs_scr, x, cw_ref, cb_ref, GS, L, D_M)
    xa = xc * jax.nn.sigmoid(xc)
    xab = xa.astype(bf16)
    xb = x.astype(bf16)

    il_parts, f_parts = [], []
    for g in range(G):
        gT = (gt_ref[g] + bif_ref[...]).T
        il_parts.append(gT[0:SUBLANES, :])
        f_parts.append(pltpu.roll(gT[0:SUBLANES, :], M_HEADS, 0))
    il = jnp.concatenate(il_parts, axis=0)
    fl = -_softplus(-jnp.concatenate(f_parts, axis=0))
    tin = lax.broadcasted_iota(jnp.int32, (G * SUBLANES, LANES), 1) % L
    bcum = _seg_scan(fl, jnp.add, 0.0, tin, L)
    a = il - bcum
    m_prev = m_scr[...].reshape(G * SUBLANES, LANES)
    big_m = jnp.maximum(m_prev, _seg_scan(a, jnp.maximum, -jnp.inf, tin, L))
    m_t = bcum + big_m
    if S == 1:
        m_last = jnp.broadcast_to(big_m[:, LANES - 1:LANES], big_m.shape)
    else:
        m_last = _seg_scan(big_m, jnp.maximum, -jnp.inf, tin, L, reverse=True)
    rows = [big_m, jnp.exp(m_prev - big_m), jnp.exp(-m_t), jnp.exp(a - m_last), jnp.exp(m_prev - m_last)]
    mN_ref[...] = m_t.reshape(G, SUBLANES, LANES)
    m_scr[...] = jnp.broadcast_to(m_t[:, LANES - 1:LANES], m_t.shape).reshape(G, SUBLANES, LANES)
    pad_rows = jnp.zeros((LANES - len(rows) * SUBLANES, LANES), f32)
    cols = []
    for g in range(G):
        gs = slice(g * SUBLANES, (g + 1) * SUBLANES)
        cols.append(jnp.concatenate([r[gs] for r in rows] + [pad_rows], axis=0).T)

    def col(g, q, h):
        return cols[g][:, SUBLANES * q + h:SUBLANES * q + h + 1]

    qs, ks, vs, kTs = [], [], [], []
    for h in range(M_HEADS):
        hs = slice(h * M_DH, (h + 1) * M_DH)
        qk_h = jnp.dot(xab[:, hs], wqk_ref[h], preferred_element_type=f32)
        qs.append(qk_h[:, 0:M_DH])
        ks.append(qk_h[:, M_DH:2 * M_DH] * (M_DH ** -0.5))
        vs.append(jnp.dot(xb[:, hs], wv_ref[h], preferred_element_type=f32))
        if S == 1:
            kT_h = lax.dot_general(wkT_ref[h], xab[:, hs], (((1,), (1,)), ((), ())),
                                   preferred_element_type=f32)
            kTs.append((kT_h * (M_DH ** -0.5)).astype(bf16))

    ti = lax.broadcasted_iota(jnp.int32, (R, R), 0)
    si = lax.broadcasted_iota(jnp.int32, (R, R), 1)
    mask = (si <= ti) & ((ti // L) == (si // L))
    ones_b = jnp.ones((R, M_DH), bf16)
    pairs = [(g, h) for g in range(G) for h in range(M_HEADS)]
    blk = lambda g: slice(g * R, (g + 1) * R)
    seqs = [slice(b * L, (b + 1) * L) for b in range(S)]
    q = {(g, h): qs[h][blk(g)] for g, h in pairs}
    k = {(g, h): ks[h][blk(g)] for g, h in pairs}
    v = {(g, h): vs[h][blk(g)] for g, h in pairs}
    qb = {p: q[p].astype(bf16) for p in pairs}
    kb = {p: k[p].astype(bf16) for p in pairs}
    qk = {p: lax.dot_general(qb[p], kb[p], (((1,), (1,)), ((), ())), preferred_element_type=f32)
          for p in pairs}
    sm = {}
    for g, h in pairs:
        a_row = a[g * SUBLANES + h:g * SUBLANES + h + 1, :]
        decay = jnp.exp(jnp.where(mask, a_row - col(g, 0, h), -jnp.inf))
        sm[g, h] = (qk[g, h] * decay).astype(bf16)
    nd = {p: jnp.dot(sm[p], jnp.concatenate([v[p].astype(bf16), ones_b], axis=1),
                     preferred_element_type=f32) for p in pairs}
    c_old = {(g, h, b): cN_ref[g * S + b, h] for g, h in pairs for b in range(S)}
    n_old = {(g, h, b): nN_ref[g * S + b, h] for g, h in pairs for b in range(S)}
    q_c = {(g, h, b): jnp.dot(qb[g, h][seqs[b]], c_old[g, h, b].astype(bf16), preferred_element_type=f32)
           for g, h in pairs for b in range(S)}
    hh = {}
    for g, h in pairs:
        g_col, e_col = col(g, 1, h), col(g, 2, h)
        pieces = []
        for b, rs in enumerate(seqs):
            q_n = jnp.sum(q[g, h][rs] * n_old[g, h, b], axis=1, keepdims=True)
            num = nd[g, h][rs, 0:M_DH] + g_col[rs] * q_c[g, h, b]
            den = nd[g, h][rs, M_DH:2 * M_DH] + g_col[rs] * q_n
            pieces.append(num / jnp.maximum(jnp.abs(den), e_col[rs]))
        hh[g, h] = pieces[0] if S == 1 else jnp.concatenate(pieces, axis=0)
    mu = {p: jnp.mean(hh[p], axis=1, keepdims=True) for p in pairs}
    var = {p: jnp.mean(jnp.square(hh[p] - mu[p]), axis=1, keepdims=True) for p in pairs}
    hn_blocks = [jnp.concatenate([(hh[g, h] - mu[g, h]) * lax.rsqrt(var[g, h] + LN_EPS)
                                  for h in range(M_HEADS)], axis=1) for g in range(G)]
    new_c, new_n = [], []
    for g, h in pairs:
        w_col = col(g, 3, h)
        wv = (w_col * v[g, h]).astype(bf16)
        wk = w_col * k[g, h]
        for b, rs in enumerate(seqs):
            g_end = col(g, 4, h)[(b + 1) * L - 1:(b + 1) * L, :]
            if S == 1:
                kv = jnp.dot(kTs[h][:, blk(g)], wv, preferred_element_type=f32)
            else:
                kv = lax.dot_general(kb[g, h][rs], wv[rs], (((0,), (0,)), ((), ())),
                                     preferred_element_type=f32)
            new_c.append((g * S + b, h, g_end * c_old[g, h, b] + kv))
            new_n.append((g * S + b, h, g_end * n_old[g, h, b] + jnp.sum(wk[rs], axis=0, keepdims=True)))

    hn = jnp.concatenate(hn_blocks, axis=0) * ng_ref[...]
    y = jax.nn.sigmoid(z_ref[...].reshape(GR, D_M)) * (hn + sk_ref[...] * xa)
    y_ref[...] = y.reshape(G, R, D_M)
    for sq, h, val in new_c:
        cN_ref[sq, h] = val
    for sq, h, val in new_n:
        nN_ref[sq, h] = val
    xs_scr[:, 0:SUBLANES, :] = xs_scr[:, L:L + SUBLANES, :]
    bN_ref[...] = xs_scr[:, SUBLANES - (CONV_W - 1):SUBLANES, :]


def _mlstm(proj, B, T, S, L, G, buf8, c0, n0, m0_lanes, cw, cb, wqk, wv, wkT, bif, ng, sk):
    R = S * L
    assert R == M_ROWS == LANES
    nblk, nc = B // S, T // L
    nb = nblk // G
    assert nblk * S == B and nc * L == T and nb * G == nblk and (S == 1 or nc == 1)
    proj4 = proj.reshape(nblk, nc, R, N_PROJ)
    m0_3 = m0_lanes.reshape(SUBLANES, nblk, LANES).transpose(1, 0, 2)
    const2 = lambda b, j: (0, 0)
    const3 = lambda b, j: (0, 0, 0)
    gate_blk = (2 * D_RG + 2 * D_M) // LANES
    GS = G * S
    outs = pl.pallas_call(
        functools.partial(_mlstm_kernel, S=S, L=L, G=G),
        grid=(nb, nc),
        in_specs=[
            pl.BlockSpec((G, None, R, D_M), lambda b, j: (b, j, 0, 2)),
            pl.BlockSpec((G, None, R, D_M), lambda b, j: (b, j, 0, 3)),
            pl.BlockSpec((G, None, R, LANES), lambda b, j: (b, j, 0, gate_blk)),
            pl.BlockSpec((GS, SUBLANES, D_M), lambda b, j: (b, 0, 0)),
            pl.BlockSpec((GS, M_HEADS, M_DH, M_DH), lambda b, j: (b, 0, 0, 0)),
            pl.BlockSpec((GS, M_HEADS, 1, M_DH), lambda b, j: (b, 0, 0, 0)),
            pl.BlockSpec((G, SUBLANES, LANES), lambda b, j: (b, 0, 0)),
            pl.BlockSpec((CONV_W, D_M), const2),
            pl.BlockSpec((1, D_M), const2),
            pl.BlockSpec((M_HEADS, M_DH, 2 * M_DH), const3),
            pl.BlockSpec((M_HEADS, M_DH, M_DH), const3),
            pl.BlockSpec((M_HEADS, M_DH, M_DH), const3),
            pl.BlockSpec((1, LANES), const2),
            pl.BlockSpec((1, D_M), const2),
            pl.BlockSpec((1, D_M), const2),
        ],
        out_specs=[
            pl.BlockSpec((G, None, R, D_M), lambda b, j: (b, j, 0, 0)),
            pl.BlockSpec((GS, M_HEADS, M_DH, M_DH), lambda b, j: (b, 0, 0, 0)),
            pl.BlockSpec((GS, M_HEADS, 1, M_DH), lambda b, j: (b, 0, 0, 0)),
            pl.BlockSpec((G, None, SUBLANES, LANES), lambda b, j: (b, j, 0, 0)),
            pl.BlockSpec((GS, CONV_W - 1, D_M), lambda b, j: (b, 0, 0)),
        ],
        out_shape=[
            jax.ShapeDtypeStruct((nblk, nc, R, D_M), f32),
            jax.ShapeDtypeStruct((B, M_HEADS, M_DH, M_DH), f32),
            jax.ShapeDtypeStruct((B, M_HEADS, 1, M_DH), f32),
            jax.ShapeDtypeStruct((nblk, nc, SUBLANES, LANES), f32),
            jax.ShapeDtypeStruct((B, CONV_W - 1, D_M), f32),
        ],
        scratch_shapes=[pltpu.VMEM((GS, SUBLANES + L, D_M), f32), pltpu.VMEM((G, SUBLANES, LANES), f32)],
        compiler_params=_cparams(2),
        name="mlstm",
    )(proj4, proj4, proj4, buf8, c0, n0, m0_3, cw, cb, wqk, wv, wkT, bif, ng, sk)
    y, c_new, n_new, m_t, b_new = outs
    m_last = m_t[:, nc - 1, :M_HEADS, :].reshape(nblk, M_HEADS, S, L)[:, :, :, L - 1]
    m_last = m_last.transpose(0, 2, 1).reshape(B, M_HEADS)
    return y.reshape(B * T, D_M), c_new, n_new, m_last, b_new


def _layer_norm(x, g, b):
    mu = jnp.mean(x, axis=-1, keepdims=True)
    var = jnp.mean(jnp.square(x - mu), axis=-1, keepdims=True)
    return (x - mu) * lax.rsqrt(var + LN_EPS) * g + b


def _first_lane_of_max(vals, lane_f):
    vmax = jnp.max(vals, axis=1, keepdims=True)
    idx = jnp.min(jnp.where(vals == vmax, lane_f, float(LANES)), axis=1, keepdims=True)
    return vmax, idx


def _outproj_kernel(xp_ref, xs_ref, rgp_ref, rgs_ref, mp_ref, ms_ref, wo_ref, g1_ref, b1_ref,
                    wrh_ref, wrl_ref, br_ref, x1_ref, info_ref, cnt_ref, *, n_p):
    i = pl.program_id(0)

    @pl.when(i == 0)
    def _():
        cnt_ref[...] = jnp.zeros_like(cnt_ref)

    def run(x_ref, rg_ref, m_ref):
        mix = jnp.dot(rg_ref[...].astype(bf16), wo_ref[0:D_RG, :], preferred_element_type=f32)
        mix = mix + jnp.dot(m_ref[...].astype(bf16), wo_ref[D_RG:D_RG + D_M, :], preferred_element_type=f32)
        x1 = _layer_norm(ALPHA * x_ref[...] + mix, g1_ref[...], b1_ref[...])
        x1_ref[...] = x1

        hi = x1.astype(bf16)
        lo = (x1 - hi.astype(f32)).astype(bf16)
        lg = (jnp.dot(hi, wrh_ref[...], preferred_element_type=f32)
              + jnp.dot(lo, wrh_ref[...], preferred_element_type=f32)
              + jnp.dot(hi, wrl_ref[...], preferred_element_type=f32)) + br_ref[...]
        lane = lax.broadcasted_iota(jnp.int32, (TM, LANES), 1)
        lane_f = lane.astype(f32)
        neg = -jnp.inf
        gl = jnp.where(lane < N_GROUPS, lg, neg)
        gmax, gidx = _first_lane_of_max(gl, lane_f)
        p_g = 1.0 / jnp.sum(jnp.exp(gl - gmax), axis=1, keepdims=True)
        e_lo = float(N_GROUPS) + float(EXPERTS_PER_GROUP) * gidx
        el = jnp.where((lane_f >= e_lo) & (lane_f < e_lo + float(EXPERTS_PER_GROUP)), lg, neg)
        v1, i1 = _first_lane_of_max(el, lane_f)
        v2, i2 = _first_lane_of_max(jnp.where(lane_f == i1, neg, el), lane_f)
        d = jnp.exp(v2 - v1)
        w1 = p_g / (1.0 + d)
        w2 = p_g * d / (1.0 + d)
        id1 = i1 - float(N_GROUPS)
        id2 = i2 - float(N_GROUPS)

        hot1 = lane_f == id1
        hot2 = lane_f == id2
        onehot = (hot1 | hot2).astype(bf16)
        ti = lax.broadcasted_iota(jnp.int32, (TM, TM), 0)
        si = lax.broadcasted_iota(jnp.int32, (TM, TM), 1)
        tri = (si <= ti).astype(bf16)
        cum = jnp.dot(tri, onehot, preferred_element_type=f32)
        rank = cum - 1.0 + cnt_ref[...]
        r1 = jnp.sum(jnp.where(hot1, rank, 0.0), axis=1, keepdims=True)
        r2 = jnp.sum(jnp.where(hot2, rank, 0.0), axis=1, keepdims=True)
        cnt_ref[...] = cnt_ref[...] + cum[TM - 1:TM, :]

        info = jnp.zeros((TM, LANES), f32)
        for c, val in enumerate((id1, id2, w1, w2, r1, r2)):
            info = jnp.where(lane == c, val, info)
        info_ref[...] = info

    @pl.when(i < n_p)
    def _():
        run(xp_ref, rgp_ref, mp_ref)

    @pl.when(i >= n_p)
    def _():
        run(xs_ref, rgs_ref, ms_ref)


def _outproj(xp, xs, rgp, rgs, mp, ms, wo, g1, b1, wrh, wrl, br):
    n_p, n_s = xp.shape[0] // TM, xs.shape[0] // TM
    n = n_p + n_s
    pmap = lambda i: (jnp.minimum(i, n_p - 1), 0)
    smap = lambda i: (jnp.maximum(i - n_p, 0), 0)
    const = lambda i: (0, 0)
    return pl.pallas_call(
        functools.partial(_outproj_kernel, n_p=n_p),
        grid=(n,),
        in_specs=[
            pl.BlockSpec((TM, D_MODEL), pmap), pl.BlockSpec((TM, D_MODEL), smap),
            pl.BlockSpec((TM, D_RG), pmap), pl.BlockSpec((TM, D_RG), smap),
            pl.BlockSpec((TM, D_M), pmap), pl.BlockSpec((TM, D_M), smap),
            pl.BlockSpec((D_RG + D_M, D_MODEL), const),
            pl.BlockSpec((1, D_MODEL), const), pl.BlockSpec((1, D_MODEL), const),
            pl.BlockSpec((D_MODEL, LANES), const), pl.BlockSpec((D_MODEL, LANES), const),
            pl.BlockSpec((1, LANES), const),
        ],
        out_specs=[
            pl.BlockSpec((TM, D_MODEL), lambda i: (i, 0)),
            pl.BlockSpec((TM, LANES), lambda i: (i, 0)),
            pl.BlockSpec((1, LANES), const),
        ],
        out_shape=[
            jax.ShapeDtypeStruct((n * TM, D_MODEL), f32),
            jax.ShapeDtypeStruct((n * TM, LANES), f32),
            jax.ShapeDtypeStruct((1, LANES), f32),
        ],
        compiler_params=_cparams(1),
        name="outproj",
    )(xp, xs, rgp, rgs, mp, ms, wo, g1, b1, wrh, wrl, br)


def _row_copy(src_ref, src_row, dst_ref, dst_row, sem):
    return pltpu.make_async_copy(src_ref.at[pl.ds(src_row, 1)], dst_ref.at[pl.ds(dst_row, 1)], sem)


def _tile_rows_wait(src_ref, dst_ref, sem):
    pltpu.make_async_copy(src_ref.at[pl.ds(0, TM)], dst_ref.at[pl.ds(0, TM)], sem).wait()


def _dispatch_kernel(lt_ref, pos_ref, x1_ref, xs_ref, zero_scr, xbuf, sem, zsem, lsem):
    i = pl.program_id(0)
    n = pl.num_programs(0)

    def zero_copy(e):
        row0 = pl.multiple_of(lt_ref[e], TM)
        return pltpu.make_async_copy(zero_scr, xs_ref.at[pl.ds(row0, TM)], zsem)

    def zero_tile(t):
        return pltpu.make_async_copy(zero_scr, xs_ref.at[pl.ds(pl.multiple_of(t * TM, TM), TM)], zsem)

    @pl.when(i == 0)
    def _():
        zero_scr[...] = jnp.zeros_like(zero_scr)
        for e in range(N_EXPERTS):
            @pl.when(lt_ref[e] >= 0)
            def _():
                zero_copy(e).start()
        n_used = lt_ref[N_EXPERTS]
        n_tiles = xs_ref.shape[0] // TM
        lax.fori_loop(n_used, n_tiles, lambda t, c: (zero_tile(t).start(), c)[1], 0)
        for e in range(N_EXPERTS):
            @pl.when(lt_ref[e] >= 0)
            def _():
                zero_copy(e).wait()
        lax.fori_loop(n_used, n_tiles, lambda t, c: (zero_tile(t).wait(), c)[1], 0)

    def load(t, s):
        return pltpu.make_async_copy(x1_ref.at[pl.ds(pl.multiple_of(t * TM, TM), TM)], xbuf.at[s], lsem.at[s])

    @pl.when(i == 0)
    def _():
        load(0, 0).start()

    @pl.when(i + 1 < n)
    def _():
        load(i + 1, (i + 1) % 3).start()

    slot = i % 3
    load(i, slot).wait()

    def start(r, c):
        _row_copy(xbuf.at[slot], r, xs_ref, pos_ref[0, 0, 2 * r], sem.at[slot]).start()
        _row_copy(xbuf.at[slot], r, xs_ref, pos_ref[0, 0, 2 * r + 1], sem.at[slot]).start()
        return c

    lax.fori_loop(0, TM, start, 0, unroll=8)

    def wait_rows(s):
        _tile_rows_wait(xbuf.at[s], xs_ref, sem.at[s])
        _tile_rows_wait(xbuf.at[s], xs_ref, sem.at[s])

    @pl.when(i > 0)
    def _():
        wait_rows((i + 2) % 3)

    @pl.when(i == n - 1)
    def _():
        wait_rows(slot)


def _dispatch(last_tile_row, pos, x1, n_rows):
    n = x1.shape[0] // TM
    return pl.pallas_call(
        _dispatch_kernel,
        grid_spec=pltpu.PrefetchScalarGridSpec(
            num_scalar_prefetch=1,
            grid=(n,),
            in_specs=[
                pl.BlockSpec((1, 1, 2 * TM), lambda i, lt: (i, 0, 0), memory_space=pltpu.SMEM),
                pl.BlockSpec(memory_space=pl.ANY),
            ],
            out_specs=pl.BlockSpec(memory_space=pl.ANY),
            scratch_shapes=[pltpu.VMEM((TM, D_MODEL), f32), pltpu.VMEM((3, TM, D_MODEL), f32),
                            pltpu.SemaphoreType.DMA((3,)), pltpu.SemaphoreType.DMA(()),
                            pltpu.SemaphoreType.DMA((3,))],
        ),
        out_shape=jax.ShapeDtypeStruct((n_rows, D_MODEL), f32),
        compiler_params=_cparams(1),
        name="dispatch",
    )(last_tile_row, pos, x1)


def _expert_kernel(te_ref, nu_ref, x_ref, wg_ref, wu_ref, wd_ref, y_ref):
    @pl.when(pl.program_id(0) < nu_ref[0])
    def _():
        x = x_ref[...].astype(bf16)
        hg = jnp.dot(x, wg_ref[0].astype(bf16), preferred_element_type=f32)
        hu = jnp.dot(x, wu_ref[0].astype(bf16), preferred_element_type=f32)
        hmid = (hg * jax.nn.sigmoid(hg) * hu).astype(bf16)
        y_ref[...] = jnp.dot(hmid, wd_ref[0].astype(bf16), preferred_element_type=f32)

    @pl.when(pl.program_id(0) >= nu_ref[0])
    def _():
        y_ref[...] = jnp.zeros_like(y_ref)


def _experts(tile_expert, n_used, xs, wg, wu, wd):
    nt = xs.shape[0] // TM
    wmap = lambda i, te, nu: (te[i], 0, 0)
    xmap = lambda i, te, nu: (jnp.minimum(i, nu[0] - 1), 0)
    return pl.pallas_call(
        _expert_kernel,
        grid_spec=pltpu.PrefetchScalarGridSpec(
            num_scalar_prefetch=2,
            grid=(nt,),
            in_specs=[
                pl.BlockSpec((TM, D_MODEL), xmap),
                pl.BlockSpec((1, D_MODEL, D_EXPERT), wmap),
                pl.BlockSpec((1, D_MODEL, D_EXPERT), wmap),
                pl.BlockSpec((1, D_EXPERT, D_MODEL), wmap),
            ],
            out_specs=pl.BlockSpec((TM, D_MODEL), lambda i, te, nu: (i, 0)),
        ),
        out_shape=jax.ShapeDtypeStruct(xs.shape, f32),
        compiler_params=_cparams(1),
        name="experts",
    )(tile_expert, n_used, xs, wg, wu, wd)


def _combine_kernel(pos_ref, posn_ref, x1_ref, info_ref, ys_ref, g2_ref, b2_ref, op_ref, os_ref,
                    ya_scr, yb_scr, sem, *, n_p):
    i = pl.program_id(0)
    n = pl.num_programs(0)
    slot = i % 2

    def gather(p_ref, s):
        def start(r, c):
            _row_copy(ys_ref, p_ref[0, 0, 2 * r], ya_scr.at[s], r, sem.at[s]).start()
            _row_copy(ys_ref, p_ref[0, 0, 2 * r + 1], yb_scr.at[s], r, sem.at[s]).start()
            return c

        lax.fori_loop(0, TM, start, 0, unroll=8)

    @pl.when(i == 0)
    def _():
        gather(pos_ref, 0)

    @pl.when(i + 1 < n)
    def _():
        gather(posn_ref, 1 - slot)

    _tile_rows_wait(ys_ref, ya_scr.at[slot], sem.at[slot])
    _tile_rows_wait(ys_ref, yb_scr.at[slot], sem.at[slot])

    info = info_ref[...]
    ffn = info[:, 2:3] * ya_scr[slot] + info[:, 3:4] * yb_scr[slot]
    out = _layer_norm(ALPHA * x1_ref[...] + ffn, g2_ref[...], b2_ref[...])

    @pl.when(i < n_p)
    def _():
        op_ref[...] = out

    @pl.when(i >= n_p)
    def _():
        os_ref[...] = out


def _combine(pos, x1, info, ys, g2, b2, n_p):
    n = x1.shape[0] // TM
    n_s = n - n_p
    const = lambda i: (0, 0)
    return pl.pallas_call(
        functools.partial(_combine_kernel, n_p=n_p),
        grid=(n,),
        in_specs=[
            pl.BlockSpec((1, 1, 2 * TM), lambda i: (i, 0, 0), memory_space=pltpu.SMEM),
            pl.BlockSpec((1, 1, 2 * TM), lambda i: (jnp.minimum(i + 1, n - 1), 0, 0),
                         memory_space=pltpu.SMEM),
            pl.BlockSpec((TM, D_MODEL), lambda i: (i, 0)),
            pl.BlockSpec((TM, LANES), lambda i: (i, 0)),
            pl.BlockSpec(memory_space=pl.ANY),
            pl.BlockSpec((1, D_MODEL), const), pl.BlockSpec((1, D_MODEL), const),
        ],
        out_specs=[
            pl.BlockSpec((TM, D_MODEL), lambda i: (jnp.minimum(i, n_p - 1), 0)),
            pl.BlockSpec((TM, D_MODEL), lambda i: (jnp.maximum(i - n_p, 0), 0)),
        ],
        out_shape=[
            jax.ShapeDtypeStruct((n_p * TM, D_MODEL), f32),
            jax.ShapeDtypeStruct((n_s * TM, D_MODEL), f32),
        ],
        scratch_shapes=[pltpu.VMEM((2, TM, D_MODEL), f32), pltpu.VMEM((2, TM, D_MODEL), f32),
                        pltpu.SemaphoreType.DMA((2,))],
        compiler_params=_cparams(1),
        name="combine",
    )(pos, pos, x1, info, ys, g2, b2)


def _block_diag(w):
    n, d, _ = w.shape
    eye = jnp.eye(n, dtype=w.dtype)
    return (eye[:, None, :, None] * w[:, :, None, :]).reshape(n * d, n * d)


def _pad_history(buf):
    return jnp.pad(buf, ((0, 0), (SUBLANES - (CONV_W - 1), 0), (0, 0)))


def kernel(x_prompt, x_sample, state_rg_h, state_rg_conv, state_m_C, state_m_n, state_m_m, state_m_conv, w_in, rg_conv_w, rg_conv_b, rg_w_a, rg_b_a, rg_w_x, rg_b_x, rg_lambda, m_conv_w, m_conv_b, m_w_q, m_w_k, m_w_v, m_b_i, m_b_f, m_norm_g, m_skip, w_out, ln1_g, ln1_b, ln2_g, ln2_b, moe_w_group, moe_b_group, moe_w_expert, moe_b_expert, moe_w_gate, moe_w_up, moe_w_down):
    BP, TP, _ = x_prompt.shape
    BS, TS, _ = x_sample.shape
    n_prompt, n_sample = BP * TP, BS * TS
    xp = x_prompt.reshape(n_prompt, D_MODEL)
    xs = x_sample.reshape(n_sample, D_MODEL)
    l = 0

    w_in_p = jnp.pad(w_in[l], ((0, 0), (0, N_PROJ - w_in.shape[-1]))).astype(bf16)
    wa = _block_diag(rg_w_a[l]).astype(bf16)
    wx = _block_diag(rg_w_x[l]).astype(bf16)
    row = lambda v: v.reshape(1, -1)
    bif = jnp.pad(jnp.concatenate([m_b_i[l], m_b_f[l]]), (0, LANES - 2 * M_HEADS)).reshape(1, LANES)
    wqk = jnp.concatenate([m_w_q[l], m_w_k[l]], axis=-1).astype(bf16)
    wv = m_w_v[l].astype(bf16)
    wkT = m_w_k[l].transpose(0, 2, 1).astype(bf16)
    w_route = jnp.pad(jnp.concatenate([moe_w_group[l], moe_w_expert[l]], axis=1),
                      ((0, 0), (0, LANES - N_GROUPS - N_EXPERTS)))
    wr_hi = w_route.astype(bf16)
    wr_lo = (w_route - wr_hi.astype(f32)).astype(bf16)
    b_route = jnp.pad(jnp.concatenate([moe_b_group[l], moe_b_expert[l]]),
                      (0, LANES - N_GROUPS - N_EXPERTS)).reshape(1, LANES)

    proj_p, proj_s = _inproj(xp, xs, w_in_p)

    rg_args = (rg_conv_w[l], row(rg_conv_b[l]), wa, row(rg_b_a[l]), wx, row(rg_b_x[l]), row(rg_lambda[l]))
    zeros = lambda *s: jnp.zeros(s, f32)
    s_blk = M_ROWS // TS
    yrg_p, p_rg_h, p_rg_conv = _rglru(proj_p, 0, BP, TP, 1, RG_ROWS_PROMPT, zeros(BP, SUBLANES, D_RG),
                                      zeros(BP, 1, D_RG), *rg_args)
    yrg_s, s_rg_h, s_rg_conv = _rglru(proj_s, 0, BS, TS, s_blk, TS, _pad_history(state_rg_conv[l]),
                                      state_rg_h[l].reshape(BS, 1, D_RG), *rg_args)

    m_args = (m_conv_w[l], row(m_conv_b[l]), wqk, wv, wkT, bif, row(m_norm_g[l]), row(m_skip[l]))
    m0_p = jnp.full((SUBLANES, BP * LANES), M_INIT, f32)
    ym_p, p_m_C, p_m_n, p_m_m, p_m_conv = _mlstm(
        proj_p, BP, TP, 1, M_ROWS, M_BLOCKS_PROMPT, zeros(BP, SUBLANES, D_M), zeros(BP, M_HEADS, M_DH, M_DH),
        zeros(BP, M_HEADS, 1, M_DH), m0_p, *m_args)
    m0_s = jnp.pad(jnp.repeat(state_m_m[l].T, TS, axis=1), ((0, SUBLANES - M_HEADS), (0, 0)))
    ym_s, s_m_C, s_m_n, s_m_m, s_m_conv = _mlstm(
        proj_s, BS, TS, s_blk, TS, M_BLOCKS_SAMPLE, _pad_history(state_m_conv[l]), state_m_C[l],
        state_m_n[l].reshape(BS, M_HEADS, 1, M_DH), m0_s, *m_args)

    x1, info, cnt = _outproj(xp, xs, yrg_p, yrg_s, ym_p, ym_s, w_out[l].astype(bf16),
                             row(ln1_g[l]), row(ln1_b[l]), wr_hi, wr_lo, b_route)

    n_tok = n_prompt + n_sample
    n_tiles = (n_tok * 2) // TM + N_EXPERTS
    counts = cnt[0, :N_EXPERTS].astype(jnp.int32)
    padded = ((counts + TM - 1) // TM) * TM
    ends = jnp.cumsum(padded)
    offs = ends - padded
    experts = jnp.arange(N_EXPERTS, dtype=jnp.int32)
    ids = info[:, 0:2].astype(jnp.int32)
    offs_of_id = jnp.sum(jnp.where(ids[:, :, None] == experts, offs, 0), axis=-1)
    pos = (offs_of_id + info[:, 4:6].astype(jnp.int32)).reshape(n_tok // TM, 1, 2 * TM)
    tiles = jnp.arange(n_tiles, dtype=jnp.int32)
    tile_expert = jnp.minimum(jnp.sum(tiles[:, None] >= (ends // TM)[None, :], axis=1),
                              N_EXPERTS - 1).astype(jnp.int32)
    n_used = (ends[-1] // TM).reshape(1).astype(jnp.int32)
    last_tile_row = jnp.concatenate([jnp.where(counts > 0, ends - TM, -1).astype(jnp.int32), n_used])

    x_sorted = _dispatch(last_tile_row, pos, x1, n_tiles * TM)
    y_sorted = _experts(tile_expert, n_used, x_sorted, moe_w_gate[l], moe_w_up[l], moe_w_down[l])
    y_p, y_s = _combine(pos, x1, info, y_sorted, row(ln2_g[l]), row(ln2_b[l]), n_prompt // TM)

    return (y_p.reshape(BP, TP, D_MODEL), y_s.reshape(BS, TS, D_MODEL),
            p_rg_h.reshape(1, BP, D_RG), p_rg_conv[None], p_m_C[None], p_m_n.reshape(1, BP, M_HEADS, M_DH),
            p_m_m[None], p_m_conv[None],
            s_rg_h.reshape(1, BS, D_RG), s_rg_conv[None], s_m_C[None], s_m_n.reshape(1, BS, M_HEADS, M_DH),
            s_m_m[None], s_m_conv[None])
```

```python
import functools

import jax
import jax.numpy as jnp
from jax import lax
from jax.experimental import pallas as pl
from jax.experimental.pallas import tpu as pltpu

f32 = jnp.float32
bf16 = jnp.bfloat16

D_MODEL = 1024
D_RG = 512
RG_BLOCKS = 8
RG_C = 8.0
D_M = 512
M_HEADS = 4
M_DH = 128
CONV_W = 4
N_GROUPS = 4
EXPERTS_PER_GROUP = 8
N_EXPERTS = 32
D_EXPERT = 256
ALPHA = 2.0 ** 0.25
LN_EPS = 1e-5
M_INIT = -1.0e4

LANES = 128
SUBLANES = 8
TM = 256
N_PROJ = 2 * D_RG + 2 * D_M + LANES
RG_ROWS_PROMPT = 256
M_ROWS = 128
M_BLOCKS_PROMPT = 8
M_BLOCKS_SAMPLE = 2
VMEM_LIMIT = 56 * 1024 * 1024


def _cparams(n_axes):
    return pltpu.CompilerParams(dimension_semantics=("arbitrary",) * n_axes,
                                vmem_limit_bytes=VMEM_LIMIT)


def _inproj_kernel(xp_ref, xs_ref, w_ref, op_ref, os_ref, *, n_p):
    i = pl.program_id(0)

    def run(x_ref, o_ref):
        o_ref[...] = jnp.dot(x_ref[...].astype(bf16), w_ref[...], preferred_element_type=f32)

    @pl.when(i < n_p)
    def _():
        run(xp_ref, op_ref)

    @pl.when(i >= n_p)
    def _():
        run(xs_ref, os_ref)


def _inproj(xp, xs, w):
    n_p, n_s = xp.shape[0] // TM, xs.shape[0] // TM
    pmap = lambda i: (jnp.minimum(i, n_p - 1), 0)
    smap = lambda i: (jnp.maximum(i - n_p, 0), 0)
    return pl.pallas_call(
        functools.partial(_inproj_kernel, n_p=n_p),
        grid=(n_p + n_s,),
        in_specs=[
            pl.BlockSpec((TM, D_MODEL), pmap),
            pl.BlockSpec((TM, D_MODEL), smap),
            pl.BlockSpec((D_MODEL, N_PROJ), lambda i: (0, 0)),
        ],
        out_specs=[pl.BlockSpec((TM, N_PROJ), pmap), pl.BlockSpec((TM, N_PROJ), smap)],
        out_shape=[jax.ShapeDtypeStruct((n_p * TM, N_PROJ), f32),
                   jax.ShapeDtypeStruct((n_s * TM, N_PROJ), f32)],
        compiler_params=_cparams(1),
        name="inproj",
    )(xp, xs, w)


def _causal_conv(xs_scr, x, cw_ref, cb_ref, S, L, C):
    xs_scr[:, SUBLANES:SUBLANES + L, :] = x.reshape(S, L, C)
    acc = cb_ref[...] + cw_ref[CONV_W - 1:CONV_W, :] * x
    for j in range(CONV_W - 1):
        lo = SUBLANES - (CONV_W - 1) + j
        acc = acc + cw_ref[j:j + 1, :] * xs_scr[:, lo:lo + L, :].reshape(S * L, C)
    return acc


def _softplus(x):
    return jnp.maximum(x, 0.0) + jnp.log1p(jnp.exp(-jnp.abs(x)))


def _rglru_kernel(x_ref, g_ref, buf_ref, h0_ref, cw_ref, cb_ref, wa_ref, ba_ref, wx_ref, bx_ref,
                  lam_ref, y_ref, hN_ref, cN_ref, xs_scr, *, S, L):
    R = S * L
    t = pl.program_id(1)

    @pl.when(t == 0)
    def _():
        xs_scr[:, 0:SUBLANES, :] = buf_ref[...]
        hN_ref[...] = h0_ref[...]

    x = x_ref[...]
    xc = _causal_conv(xs_scr, x, cw_ref, cb_ref, S, L, D_RG)
    xcb = xc.astype(bf16)
    r = jax.nn.sigmoid(jnp.dot(xcb, wa_ref[...], preferred_element_type=f32) + ba_ref[...])
    ig = jax.nn.sigmoid(jnp.dot(xcb, wx_ref[...], preferred_element_type=f32) + bx_ref[...])
    log_a = (-RG_C) * r * _softplus(-lam_ref[...])
    a = jnp.exp(log_a)
    th = jnp.tanh(log_a)
    u = jnp.sqrt(-2.0 * th / (1.0 - th)) * ig * xc

    n_grp, grp_per_seq = R // SUBLANES, L // SUBLANES
    a3 = a.reshape(n_grp, SUBLANES, D_RG)
    u3 = u.reshape(n_grp, SUBLANES, D_RG)
    sub = lax.broadcasted_iota(jnp.int32, (n_grp, SUBLANES, D_RG), 1)
    s = 1
    while s < SUBLANES:
        ok = sub >= s
        a_sh = pltpu.roll(a3, s, 1)
        u_sh = pltpu.roll(u3, s, 1)
        u3 = jnp.where(ok, a3 * u_sh + u3, u3)
        a3 = jnp.where(ok, a3 * a_sh, a3)
        s *= 2
    h0 = hN_ref[...]
    groups = []
    for kg in range(n_grp):
        carry = h0[kg // grp_per_seq] if kg % grp_per_seq == 0 else groups[-1][SUBLANES - 1:SUBLANES, :]
        groups.append(a3[kg] * carry + u3[kg])
    h = jnp.concatenate(groups, axis=0)

    y_ref[...] = h * jax.nn.gelu(g_ref[...], approximate=True)
    hN_ref[...] = h.reshape(S, L, D_RG)[:, L - 1:L, :]
    xs_scr[:, 0:SUBLANES, :] = xs_scr[:, L:L + SUBLANES, :]
    cN_ref[...] = xs_scr[:, SUBLANES - (CONV_W - 1):SUBLANES, :]


def _rglru(proj, row0, B, T, S, L, buf8, h0, cw, cb, wa, ba, wx, bx, lam):
    R = S * L
    nb, nt = B // S, T // L
    blk0 = row0 // R
    row_map = lambda b, t: (blk0 + b * nt + t, 0)
    const2 = lambda b, t: (0, 0)
    return pl.pallas_call(
        functools.partial(_rglru_kernel, S=S, L=L),
        grid=(nb, nt),
        in_specs=[
            pl.BlockSpec((R, D_RG), row_map),
            pl.BlockSpec((R, D_RG), lambda b, t: (blk0 + b * nt + t, 1)),
            pl.BlockSpec((S, SUBLANES, D_RG), lambda b, t: (b, 0, 0)),
            pl.BlockSpec((S, 1, D_RG), lambda b, t: (b, 0, 0)),
            pl.BlockSpec((CONV_W, D_RG), const2),
            pl.BlockSpec((1, D_RG), const2),
            pl.BlockSpec((D_RG, D_RG), const2),
            pl.BlockSpec((1, D_RG), const2),
            pl.BlockSpec((D_RG, D_RG), const2),
            pl.BlockSpec((1, D_RG), const2),
            pl.BlockSpec((1, D_RG), const2),
        ],
        out_specs=[
            pl.BlockSpec((R, D_RG), lambda b, t: (b * nt + t, 0)),
            pl.BlockSpec((S, 1, D_RG), lambda b, t: (b, 0, 0)),
            pl.BlockSpec((S, CONV_W - 1, D_RG), lambda b, t: (b, 0, 0)),
        ],
        out_shape=[
            jax.ShapeDtypeStruct((B * T, D_RG), f32),
            jax.ShapeDtypeStruct((B, 1, D_RG), f32),
            jax.ShapeDtypeStruct((B, CONV_W - 1, D_RG), f32),
        ],
        scratch_shapes=[pltpu.VMEM((S, SUBLANES + L, D_RG), f32)],
        compiler_params=_cparams(2),
        name="rglru",
    )(proj, proj, buf8, h0, cw, cb, wa, ba, wx, bx, lam)


def _seg_scan(x, op, fill, tin, L, reverse=False):
    s = 1
    while s < L:
        if reverse:
            sh = pltpu.roll(x, LANES - s, 1)
            ok = tin < L - s
        else:
            sh = pltpu.roll(x, s, 1)
            ok = tin >= s
        x = op(x, jnp.where(ok, sh, fill))
        s *= 2
    return x


def _mlstm_kernel(x_ref, z_ref, gt_ref, buf_ref, c0_ref, n0_ref, m0_ref,
                  cw_ref, cb_ref, wqk_ref, wv_ref, wkT_ref, bif_ref, ng_ref, sk_ref,
                  y_ref, cN_ref, nN_ref, mN_ref, bN_ref, xs_scr, m_scr, *, S, L, G):
    R = S * L
    GS, GR = G * S, G * R
    j = pl.program_id(1)

    @pl.when(j == 0)
    def _():
        xs_scr[:, 0:SUBLANES, :] = buf_ref[...]
        cN_ref[...] = c0_ref[...]
        nN_ref[...] = n0_ref[...]
        m_scr[...] = m0_ref[...]

    x = x_ref[...].reshape(GR, D_M)
    xc = _causal_conv(xs_scr, x, cw_ref, cb_ref, GS, L, D_M)
    xa = xc * jax.nn.sigmoid(xc)
    xab = xa.astype(bf16)
    xb = x.astype(bf16)

    il_parts, f_parts = [], []
    for g in range(G):
        gT = (gt_ref[g] + bif_ref[...]).T
        il_parts.append(gT[0:SUBLANES, :])
        f_parts.append(pltpu.roll(gT[0:SUBLANES, :], M_HEADS, 0))
    il = jnp.concatenate(il_parts, axis=0)
    fl = -_softplus(-jnp.concatenate(f_parts, axis=0))
    tin = lax.broadcasted_iota(jnp.int32, (G * SUBLANES, LANES), 1) % L
    bcum = _seg_scan(fl, jnp.add, 0.0, tin, L)
    a = il - bcum
    m_prev = m_scr[...].reshape(G * SUBLANES, LANES)
    big_m = jnp.maximum(m_prev, _seg_scan(a, jnp.maximum, -jnp.inf, tin, L))
    m_t = bcum + big_m
    if S == 1:
        m_last = jnp.broadcast_to(big_m[:, LANES - 1:LANES], big_m.shape)
    else:
        m_last = _seg_scan(big_m, jnp.maximum, -jnp.inf, tin, L, reverse=True)
    rows = [big_m, jnp.exp(m_prev - big_m), jnp.exp(-m_t), jnp.exp(a - m_last), jnp.exp(m_prev - m_last)]
    mN_ref[...] = m_t.reshape(G, SUBLANES, LANES)
    m_scr[...] = jnp.broadcast_to(m_t[:, LANES - 1:LANES], m_t.shape).reshape(G, SUBLANES, LANES)
    pad_rows = jnp.zeros((LANES - len(rows) * SUBLANES, LANES), f32)
    cols = []
    for g in range(G):
        gs = slice(g * SUBLANES, (g + 1) * SUBLANES)
        cols.append(jnp.concatenate([r[gs] for r in rows] + [pad_rows], axis=0).T)

    def col(g, q, h):
        return cols[g][:, SUBLANES * q + h:SUBLANES * q + h + 1]

    qs, ks, vs, kTs = [], [], [], []
    for h in range(M_HEADS):
        hs = slice(h * M_DH, (h + 1) * M_DH)
        qk_h = jnp.dot(xab[:, hs], wqk_ref[h], preferred_element_type=f32)
        qs.append(qk_h[:, 0:M_DH])
        ks.append(qk_h[:, M_DH:2 * M_DH] * (M_DH ** -0.5))
        vs.append(jnp.dot(xb[:, hs], wv_ref[h], preferred_element_type=f32))
        if S == 1:
            kT_h = lax.dot_general(wkT_ref[h], xab[:, hs], (((1,), (1,)), ((), ())),
                                   preferred_element_type=f32)
            kTs.append((kT_h * (M_DH ** -0.5)).astype(bf16))

    ti = lax.broadcasted_iota(jnp.int32, (R, R), 0)
    si = lax.broadcasted_iota(jnp.int32, (R, R), 1)
    mask = (si <= ti) & ((ti // L) == (si // L))
    ones_b = jnp.ones((R, M_DH), bf16)
    pairs = [(g, h) for g in range(G) for h in range(M_HEADS)]
    blk = lambda g: slice(g * R, (g + 1) * R)
    seqs = [slice(b * L, (b + 1) * L) for b in range(S)]
    q = {(g, h): qs[h][blk(g)] for g, h in pairs}
    k = {(g, h): ks[h][blk(g)] for g, h in pairs}
    v = {(g, h): vs[h][blk(g)] for g, h in pairs}
    qb = {p: q[p].astype(bf16) for p in pairs}
    kb = {p: k[p].astype(bf16) for p in pairs}
    qk = {p: lax.dot_general(qb[p], kb[p], (((1,), (1,)), ((), ())), preferred_element_type=f32)
          for p in pairs}
    sm = {}
    for g, h in pairs:
        a_row = a[g * SUBLANES + h:g * SUBLANES + h + 1, :]
        decay = jnp.exp(jnp.where(mask, a_row - col(g, 0, h), -jnp.inf))
        sm[g, h] = (qk[g, h] * decay).astype(bf16)
    nd = {p: jnp.dot(sm[p], jnp.concatenate([v[p].astype(bf16), ones_b], axis=1),
                     preferred_element_type=f32) for p in pairs}
    c_old = {(g, h, b): cN_ref[g * S + b, h] for g, h in pairs for b in range(S)}
    n_old = {(g, h, b): nN_ref[g * S + b, h] for g, h in pairs for b in range(S)}
    q_c = {(g, h, b): jnp.dot(qb[g, h][seqs[b]], c_old[g, h, b].astype(bf16), preferred_element_type=f32)
           for g, h in pairs for b in range(S)}
    hh = {}
    for g, h in pairs:
        g_col, e_col = col(g, 1, h), col(g, 2, h)
        pieces = []
        for b, rs in enumerate(seqs):
            q_n = jnp.sum(q[g, h][rs] * n_old[g, h, b], axis=1, keepdims=True)
            num = nd[g, h][rs, 0:M_DH] + g_col[rs] * q_c[g, h, b]
            den = nd[g, h][rs, M_DH:2 * M_DH] + g_col[rs] * q_n
            pieces.append(num / jnp.maximum(jnp.abs(den), e_col[rs]))
        hh[g, h] = pieces[0] if S == 1 else jnp.concatenate(pieces, axis=0)
    mu = {p: jnp.mean(hh[p], axis=1, keepdims=True) for p in pairs}
    var = {p: jnp.mean(jnp.square(hh[p] - mu[p]), axis=1, keepdims=True) for p in pairs}
    hn_blocks = [jnp.concatenate([(hh[g, h] - mu[g, h]) * lax.rsqrt(var[g, h] + LN_EPS)
                                  for h in range(M_HEADS)], axis=1) for g in range(G)]
    new_c, new_n = [], []
    for g, h in pairs:
        w_col = col(g, 3, h)
        wv = (w_col * v[g, h]).astype(bf16)
        wk = w_col * k[g, h]
        for b, rs in enumerate(seqs):
            g_end = col(g, 4, h)[(b + 1) * L - 1:(b + 1) * L, :]
            if S == 1:
                kv = jnp.dot(kTs[h][:, blk(g)], wv, preferred_element_type=f32)
            else:
                kv = lax.dot_general(kb[g, h][rs], wv[rs], (((0,), (0,)), ((), ())),
                                     preferred_element_type=f32)
            new_c.append((g * S + b, h, g_end * c_old[g, h, b] + kv))
            new_n.append((g * S + b, h, g_end * n_old[g, h, b] + jnp.sum(wk[rs], axis=0, keepdims=True)))

    hn = jnp.concatenate(hn_blocks, axis=0) * ng_ref[...]
    y = jax.nn.sigmoid(z_ref[...].reshape(GR, D_M)) * (hn + sk_ref[...] * xa)
    y_ref[...] = y.reshape(G, R, D_M)
    for sq, h, val in new_c:
        cN_ref[sq, h] = val
    for sq, h, val in new_n:
        nN_ref[sq, h] = val
    xs_scr[:, 0:SUBLANES, :] = xs_scr[:, L:L + SUBLANES, :]
    bN_ref[...] = xs_scr[:, SUBLANES - (CONV_W - 1):SUBLANES, :]


def _mlstm(proj, B, T, S, L, G, buf8, c0, n0, m0_lanes, cw, cb, wqk, wv, wkT, bif, ng, sk):
    R = S * L
    assert R == M_ROWS == LANES
    nblk, nc = B // S, T // L
    nb = nblk // G
    assert nblk * S == B and nc * L == T and nb * G == nblk and (S == 1 or nc == 1)
    proj4 = proj.reshape(nblk, nc, R, N_PROJ)
    m0_3 = m0_lanes.reshape(SUBLANES, nblk, LANES).transpose(1, 0, 2)
    const2 = lambda b, j: (0, 0)
    const3 = lambda b, j: (0, 0, 0)
    gate_blk = (2 * D_RG + 2 * D_M) // LANES
    GS = G * S
    outs = pl.pallas_call(
        functools.partial(_mlstm_kernel, S=S, L=L, G=G),
        grid=(nb, nc),
        in_specs=[
            pl.BlockSpec((G, None, R, D_M), lambda b, j: (b, j, 0, 2)),
            pl.BlockSpec((G, None, R, D_M), lambda b, j: (b, j, 0, 3)),
            pl.BlockSpec((G, None, R, LANES), lambda b, j: (b, j, 0, gate_blk)),
            pl.BlockSpec((GS, SUBLANES, D_M), lambda b, j: (b, 0, 0)),
            pl.BlockSpec((GS, M_HEADS, M_DH, M_DH), lambda b, j: (b, 0, 0, 0)),
            pl.BlockSpec((GS, M_HEADS, 1, M_DH), lambda b, j: (b, 0, 0, 0)),
            pl.BlockSpec((G, SUBLANES, LANES), lambda b, j: (b, 0, 0)),
            pl.BlockSpec((CONV_W, D_M), const2),
            pl.BlockSpec((1, D_M), const2),
            pl.BlockSpec((M_HEADS, M_DH, 2 * M_DH), const3),
            pl.BlockSpec((M_HEADS, M_DH, M_DH), const3),
            pl.BlockSpec((M_HEADS, M_DH, M_DH), const3),
            pl.BlockSpec((1, LANES), const2),
            pl.BlockSpec((1, D_M), const2),
            pl.BlockSpec((1, D_M), const2),
        ],
        out_specs=[
            pl.BlockSpec((G, None, R, D_M), lambda b, j: (b, j, 0, 0)),
            pl.BlockSpec((GS, M_HEADS, M_DH, M_DH), lambda b, j: (b, 0, 0, 0)),
            pl.BlockSpec((GS, M_HEADS, 1, M_DH), lambda b, j: (b, 0, 0, 0)),
            pl.BlockSpec((G, None, SUBLANES, LANES), lambda b, j: (b, j, 0, 0)),
            pl.BlockSpec((GS, CONV_W - 1, D_M), lambda b, j: (b, 0, 0)),
        ],
        out_shape=[
            jax.ShapeDtypeStruct((nblk, nc, R, D_M), f32),
            jax.ShapeDtypeStruct((B, M_HEADS, M_DH, M_DH), f32),
            jax.ShapeDtypeStruct((B, M_HEADS, 1, M_DH), f32),
            jax.ShapeDtypeStruct((nblk, nc, SUBLANES, LANES), f32),
            jax.ShapeDtypeStruct((B, CONV_W - 1, D_M), f32),
        ],
        scratch_shapes=[pltpu.VMEM((GS, SUBLANES + L, D_M), f32), pltpu.VMEM((G, SUBLANES, LANES), f32)],
        compiler_params=_cparams(2),
        name="mlstm",
    )(proj4, proj4, proj4, buf8, c0, n0, m0_3, cw, cb, wqk, wv, wkT, bif, ng, sk)
    y, c_new, n_new, m_t, b_new = outs
    m_last = m_t[:, nc - 1, :M_HEADS, :].reshape(nblk, M_HEADS, S, L)[:, :, :, L - 1]
    m_last = m_last.transpose(0, 2, 1).reshape(B, M_HEADS)
    return y.reshape(B * T, D_M), c_new, n_new, m_last, b_new


def _layer_norm(x, g, b):
    mu = jnp.mean(x, axis=-1, keepdims=True)
    var = jnp.mean(jnp.square(x - mu), axis=-1, keepdims=True)
    return (x - mu) * lax.rsqrt(var + LN_EPS) * g + b


def _first_lane_of_max(vals, lane_f):
    vmax = jnp.max(vals, axis=1, keepdims=True)
    idx = jnp.min(jnp.where(vals == vmax, lane_f, float(LANES)), axis=1, keepdims=True)
    return vmax, idx


def _outproj_kernel(xp_ref, xs_ref, rgp_ref, rgs_ref, mp_ref, ms_ref, wo_ref, g1_ref, b1_ref,
                    wrh_ref, wrl_ref, br_ref, x1_ref, info_ref, cnt_ref, *, n_p):
    i = pl.program_id(0)

    @pl.when(i == 0)
    def _():
        cnt_ref[...] = jnp.zeros_like(cnt_ref)

    def run(x_ref, rg_ref, m_ref):
        mix = jnp.dot(rg_ref[...].astype(bf16), wo_ref[0:D_RG, :], preferred_element_type=f32)
        mix = mix + jnp.dot(m_ref[...].astype(bf16), wo_ref[D_RG:D_RG + D_M, :], preferred_element_type=f32)
        x1 = _layer_norm(ALPHA * x_ref[...] + mix, g1_ref[...], b1_ref[...])
        x1_ref[...] = x1

        hi = x1.astype(bf16)
        lo = (x1 - hi.astype(f32)).astype(bf16)
        lg = (jnp.dot(hi, wrh_ref[...], preferred_element_type=f32)
              + jnp.dot(lo, wrh_ref[...], preferred_element_type=f32)
              + jnp.dot(hi, wrl_ref[...], preferred_element_type=f32)) + br_ref[...]
        lane = lax.broadcasted_iota(jnp.int32, (TM, LANES), 1)
        lane_f = lane.astype(f32)
        neg = -jnp.inf
        gl = jnp.where(lane < N_GROUPS, lg, neg)
        gmax, gidx = _first_lane_of_max(gl, lane_f)
        p_g = 1.0 / jnp.sum(jnp.exp(gl - gmax), axis=1, keepdims=True)
        e_lo = float(N_GROUPS) + float(EXPERTS_PER_GROUP) * gidx
        el = jnp.where((lane_f >= e_lo) & (lane_f < e_lo + float(EXPERTS_PER_GROUP)), lg, neg)
        v1, i1 = _first_lane_of_max(el, lane_f)
        v2, i2 = _first_lane_of_max(jnp.where(lane_f == i1, neg, el), lane_f)
        d = jnp.exp(v2 - v1)
        w1 = p_g / (1.0 + d)
        w2 = p_g * d / (1.0 + d)
        id1 = i1 - float(N_GROUPS)
        id2 = i2 - float(N_GROUPS)

        hot1 = lane_f == id1
        hot2 = lane_f == id2
        onehot = (hot1 | hot2).astype(bf16)
        ti = lax.broadcasted_iota(jnp.int32, (TM, TM), 0)
        si = lax.broadcasted_iota(jnp.int32, (TM, TM), 1)
        tri = (si <= ti).astype(bf16)
        cum = jnp.dot(tri, onehot, preferred_element_type=f32)
        rank = cum - 1.0 + cnt_ref[...]
        r1 = jnp.sum(jnp.where(hot1, rank, 0.0), axis=1, keepdims=True)
        r2 = jnp.sum(jnp.where(hot2, rank, 0.0), axis=1, keepdims=True)
        cnt_ref[...] = cnt_ref[...] + cum[TM - 1:TM, :]

        info = jnp.zeros((TM, LANES), f32)
        for c, val in enumerate((id1, id2, w1, w2, r1, r2)):
            info = jnp.where(lane == c, val, info)
        info_ref[...] = info

    @pl.when(i < n_p)
    def _():
        run(xp_ref, rgp_ref, mp_ref)

    @pl.when(i >= n_p)
    def _():
        run(xs_ref, rgs_ref, ms_ref)


def _outproj(xp, xs, rgp, rgs, mp, ms, wo, g1, b1, wrh, wrl, br):
    n_p, n_s = xp.shape[0] // TM, xs.shape[0] // TM
    n = n_p + n_s
    pmap = lambda i: (jnp.minimum(i, n_p - 1), 0)
    smap = lambda i: (jnp.maximum(i - n_p, 0), 0)
    const = lambda i: (0, 0)
    return pl.pallas_call(
        functools.partial(_outproj_kernel, n_p=n_p),
        grid=(n,),
        in_specs=[
            pl.BlockSpec((TM, D_MODEL), pmap), pl.BlockSpec((TM, D_MODEL), smap),
            pl.BlockSpec((TM, D_RG), pmap), pl.BlockSpec((TM, D_RG), smap),
            pl.BlockSpec((TM, D_M), pmap), pl.BlockSpec((TM, D_M), smap),
            pl.BlockSpec((D_RG + D_M, D_MODEL), const),
            pl.BlockSpec((1, D_MODEL), const), pl.BlockSpec((1, D_MODEL), const),
            pl.BlockSpec((D_MODEL, LANES), const), pl.BlockSpec((D_MODEL, LANES), const),
            pl.BlockSpec((1, LANES), const),
        ],
        out_specs=[
            pl.BlockSpec((TM, D_MODEL), lambda i: (i, 0)),
            pl.BlockSpec((TM, LANES), lambda i: (i, 0)),
            pl.BlockSpec((1, LANES), const),
        ],
        out_shape=[
            jax.ShapeDtypeStruct((n * TM, D_MODEL), f32),
            jax.ShapeDtypeStruct((n * TM, LANES), f32),
            jax.ShapeDtypeStruct((1, LANES), f32),
        ],
        compiler_params=_cparams(1),
        name="outproj",
    )(xp, xs, rgp, rgs, mp, ms, wo, g1, b1, wrh, wrl, br)


def _row_copy(src_ref, src_row, dst_ref, dst_row, sem):
    return pltpu.make_async_copy(src_ref.at[pl.ds(src_row, 1)], dst_ref.at[pl.ds(dst_row, 1)], sem)


def _tile_rows_wait(src_ref, dst_ref, sem):
    pltpu.make_async_copy(src_ref.at[pl.ds(0, TM)], dst_ref.at[pl.ds(0, TM)], sem).wait()


def _dispatch_kernel(lt_ref, pos_ref, x1_ref, xs_ref, zero_scr, xbuf, sem, zsem, lsem):
    i = pl.program_id(0)
    n = pl.num_programs(0)

    def zero_copy(e):
        row0 = pl.multiple_of(lt_ref[e], TM)
        return pltpu.make_async_copy(zero_scr, xs_ref.at[pl.ds(row0, TM)], zsem)

    def zero_tile(t):
        return pltpu.make_async_copy(zero_scr, xs_ref.at[pl.ds(pl.multiple_of(t * TM, TM), TM)], zsem)

    @pl.when(i == 0)
    def _():
        zero_scr[...] = jnp.zeros_like(zero_scr)
        for e in range(N_EXPERTS):
            @pl.when(lt_ref[e] >= 0)
            def _():
                zero_copy(e).start()
        n_used = lt_ref[N_EXPERTS]
        n_tiles = xs_ref.shape[0] // TM
        lax.fori_loop(n_used, n_tiles, lambda t, c: (zero_tile(t).start(), c)[1], 0)
        for e in range(N_EXPERTS):
            @pl.when(lt_ref[e] >= 0)
            def _():
                zero_copy(e).wait()
        lax.fori_loop(n_used, n_tiles, lambda t, c: (zero_tile(t).wait(), c)[1], 0)

    def load(t, s):
        return pltpu.make_async_copy(x1_ref.at[pl.ds(pl.multiple_of(t * TM, TM), TM)], xbuf.at[s], lsem.at[s])

    @pl.when(i == 0)
    def _():
        load(0, 0).start()

    @pl.when(i + 1 < n)
    def _():
        load(i + 1, (i + 1) % 3).start()

    slot = i % 3
    load(i, slot).wait()

    def start(r, c):
        _row_copy(xbuf.at[slot], r, xs_ref, pos_ref[0, 0, 2 * r], sem.at[slot]).start(priority=0)
        _row_copy(xbuf.at[slot], r, xs_ref, pos_ref[0, 0, 2 * r + 1], sem.at[slot]).start(priority=1)
        return c

    lax.fori_loop(0, TM, start, 0, unroll=8)

    def wait_rows(s):
        _tile_rows_wait(xbuf.at[s], xs_ref, sem.at[s])
        _tile_rows_wait(xbuf.at[s], xs_ref, sem.at[s])

    @pl.when(i > 0)
    def _():
        wait_rows((i + 2) % 3)

    @pl.when(i == n - 1)
    def _():
        wait_rows(slot)


def _dispatch(last_tile_row, pos, x1, n_rows):
    n = x1.shape[0] // TM
    return pl.pallas_call(
        _dispatch_kernel,
        grid_spec=pltpu.PrefetchScalarGridSpec(
            num_scalar_prefetch=1,
            grid=(n,),
            in_specs=[
                pl.BlockSpec((1, 1, 2 * TM), lambda i, lt: (i, 0, 0), memory_space=pltpu.SMEM),
                pl.BlockSpec(memory_space=pl.ANY),
            ],
            out_specs=pl.BlockSpec(memory_space=pl.ANY),
            scratch_shapes=[pltpu.VMEM((TM, D_MODEL), f32), pltpu.VMEM((3, TM, D_MODEL), f32),
                            pltpu.SemaphoreType.DMA((3,)), pltpu.SemaphoreType.DMA(()),
                            pltpu.SemaphoreType.DMA((3,))],
        ),
        out_shape=jax.ShapeDtypeStruct((n_rows, D_MODEL), f32),
        compiler_params=_cparams(1),
        name="dispatch",
    )(last_tile_row, pos, x1)


def _expert_kernel(te_ref, nu_ref, x_ref, wg_ref, wu_ref, wd_ref, y_ref):
    @pl.when(pl.program_id(0) < nu_ref[0])
    def _():
        x = x_ref[...].astype(bf16)
        hg = jnp.dot(x, wg_ref[0].astype(bf16), preferred_element_type=f32)
        hu = jnp.dot(x, wu_ref[0].astype(bf16), preferred_element_type=f32)
        hmid = (hg * jax.nn.sigmoid(hg) * hu).astype(bf16)
        y_ref[...] = jnp.dot(hmid, wd_ref[0].astype(bf16), preferred_element_type=f32)

    @pl.when(pl.program_id(0) >= nu_ref[0])
    def _():
        y_ref[...] = jnp.zeros_like(y_ref)


def _experts(tile_expert, n_used, xs, wg, wu, wd):
    nt = xs.shape[0] // TM
    wmap = lambda i, te, nu: (te[i], 0, 0)
    xmap = lambda i, te, nu: (jnp.minimum(i, nu[0] - 1), 0)
    return pl.pallas_call(
        _expert_kernel,
        grid_spec=pltpu.PrefetchScalarGridSpec(
            num_scalar_prefetch=2,
            grid=(nt,),
            in_specs=[
                pl.BlockSpec((TM, D_MODEL), xmap),
                pl.BlockSpec((1, D_MODEL, D_EXPERT), wmap),
                pl.BlockSpec((1, D_MODEL, D_EXPERT), wmap),
                pl.BlockSpec((1, D_EXPERT, D_MODEL), wmap),
            ],
            out_specs=pl.BlockSpec((TM, D_MODEL), lambda i, te, nu: (i, 0)),
        ),
        out_shape=jax.ShapeDtypeStruct(xs.shape, f32),
        compiler_params=_cparams(1),
        name="experts",
    )(tile_expert, n_used, xs, wg, wu, wd)


def _combine_kernel(pos_ref, posn_ref, x1_ref, info_ref, ys_ref, g2_ref, b2_ref, op_ref, os_ref,
                    ya_scr, yb_scr, sem, *, n_p):
    i = pl.program_id(0)
    n = pl.num_programs(0)
    slot = i % 2

    def gather(p_ref, s):
        def start(r, c):
            _row_copy(ys_ref, p_ref[0, 0, 2 * r], ya_scr.at[s], r, sem.at[s]).start(priority=0)
            _row_copy(ys_ref, p_ref[0, 0, 2 * r + 1], yb_scr.at[s], r, sem.at[s]).start(priority=1)
            return c

        lax.fori_loop(0, TM, start, 0, unroll=8)

    @pl.when(i == 0)
    def _():
        gather(pos_ref, 0)

    @pl.when(i + 1 < n)
    def _():
        gather(posn_ref, 1 - slot)

    _tile_rows_wait(ys_ref, ya_scr.at[slot], sem.at[slot])
    _tile_rows_wait(ys_ref, yb_scr.at[slot], sem.at[slot])

    info = info_ref[...]
    ffn = info[:, 2:3] * ya_scr[slot] + info[:, 3:4] * yb_scr[slot]
    out = _layer_norm(ALPHA * x1_ref[...] + ffn, g2_ref[...], b2_ref[...])

    @pl.when(i < n_p)
    def _():
        op_ref[...] = out

    @pl.when(i >= n_p)
    def _():
        os_ref[...] = out


def _combine(pos, x1, info, ys, g2, b2, n_p):
    n = x1.shape[0] // TM
    n_s = n - n_p
    const = lambda i: (0, 0)
    return pl.pallas_call(
        functools.partial(_combine_kernel, n_p=n_p),
        grid=(n,),
        in_specs=[
            pl.BlockSpec((1, 1, 2 * TM), lambda i: (i, 0, 0), memory_space=pltpu.SMEM),
            pl.BlockSpec((1, 1, 2 * TM), lambda i: (jnp.minimum(i + 1, n - 1), 0, 0),
                         memory_space=pltpu.SMEM),
            pl.BlockSpec((TM, D_MODEL), lambda i: (i, 0)),
            pl.BlockSpec((TM, LANES), lambda i: (i, 0)),
            pl.BlockSpec(memory_space=pl.ANY),
            pl.BlockSpec((1, D_MODEL), const), pl.BlockSpec((1, D_MODEL), const),
        ],
        out_specs=[
            pl.BlockSpec((TM, D_MODEL), lambda i: (jnp.minimum(i, n_p - 1), 0)),
            pl.BlockSpec((TM, D_MODEL), lambda i: (jnp.maximum(i - n_p, 0), 0)),
        ],
        out_shape=[
            jax.ShapeDtypeStruct((n_p * TM, D_MODEL), f32),
            jax.ShapeDtypeStruct((n_s * TM, D_MODEL), f32),
        ],
        scratch_shapes=[pltpu.VMEM((2, TM, D_MODEL), f32), pltpu.VMEM((2, TM, D_MODEL), f32),
                        pltpu.SemaphoreType.DMA((2,))],
        compiler_params=_cparams(1),
        name="combine",
    )(pos, pos, x1, info, ys, g2, b2)


def _block_diag(w):
    n, d, _ = w.shape
    eye = jnp.eye(n, dtype=w.dtype)
    return (eye[:, None, :, None] * w[:, :, None, :]).reshape(n * d, n * d)


def _pad_history(buf):
    return jnp.pad(buf, ((0, 0), (SUBLANES - (CONV_W - 1), 0), (0, 0)))


def kernel(x_prompt, x_sample, state_rg_h, state_rg_conv, state_m_C, state_m_n, state_m_m, state_m_conv, w_in, rg_conv_w, rg_conv_b, rg_w_a, rg_b_a, rg_w_x, rg_b_x, rg_lambda, m_conv_w, m_conv_b, m_w_q, m_w_k, m_w_v, m_b_i, m_b_f, m_norm_g, m_skip, w_out, ln1_g, ln1_b, ln2_g, ln2_b, moe_w_group, moe_b_group, moe_w_expert, moe_b_expert, moe_w_gate, moe_w_up, moe_w_down):
    BP, TP, _ = x_prompt.shape
    BS, TS, _ = x_sample.shape
    n_prompt, n_sample = BP * TP, BS * TS
    xp = x_prompt.reshape(n_prompt, D_MODEL)
    xs = x_sample.reshape(n_sample, D_MODEL)
    l = 0

    w_in_p = jnp.pad(w_in[l], ((0, 0), (0, N_PROJ - w_in.shape[-1]))).astype(bf16)
    wa = _block_diag(rg_w_a[l]).astype(bf16)
    wx = _block_diag(rg_w_x[l]).astype(bf16)
    row = lambda v: v.reshape(1, -1)
    bif = jnp.pad(jnp.concatenate([m_b_i[l], m_b_f[l]]), (0, LANES - 2 * M_HEADS)).reshape(1, LANES)
    wqk = jnp.concatenate([m_w_q[l], m_w_k[l]], axis=-1).astype(bf16)
    wv = m_w_v[l].astype(bf16)
    wkT = m_w_k[l].transpose(0, 2, 1).astype(bf16)
    w_route = jnp.pad(jnp.concatenate([moe_w_group[l], moe_w_expert[l]], axis=1),
                      ((0, 0), (0, LANES - N_GROUPS - N_EXPERTS)))
    wr_hi = w_route.astype(bf16)
    wr_lo = (w_route - wr_hi.astype(f32)).astype(bf16)
    b_route = jnp.pad(jnp.concatenate([moe_b_group[l], moe_b_expert[l]]),
                      (0, LANES - N_GROUPS - N_EXPERTS)).reshape(1, LANES)

    proj_p, proj_s = _inproj(xp, xs, w_in_p)

    rg_args = (rg_conv_w[l], row(rg_conv_b[l]), wa, row(rg_b_a[l]), wx, row(rg_b_x[l]), row(rg_lambda[l]))
    zeros = lambda *s: jnp.zeros(s, f32)
    s_blk = M_ROWS // TS
    yrg_p, p_rg_h, p_rg_conv = _rglru(proj_p, 0, BP, TP, 1, RG_ROWS_PROMPT, zeros(BP, SUBLANES, D_RG),
                                      zeros(BP, 1, D_RG), *rg_args)
    yrg_s, s_rg_h, s_rg_conv = _rglru(proj_s, 0, BS, TS, s_blk, TS, _pad_history(state_rg_conv[l]),
                                      state_rg_h[l].reshape(BS, 1, D_RG), *rg_args)

    m_args = (m_conv_w[l], row(m_conv_b[l]), wqk, wv, wkT, bif, row(m_norm_g[l]), row(m_skip[l]))
    m0_p = jnp.full((SUBLANES, BP * LANES), M_INIT, f32)
    ym_p, p_m_C, p_m_n, p_m_m, p_m_conv = _mlstm(
        proj_p, BP, TP, 1, M_ROWS, M_BLOCKS_PROMPT, zeros(BP, SUBLANES, D_M), zeros(BP, M_HEADS, M_DH, M_DH),
        zeros(BP, M_HEADS, 1, M_DH), m0_p, *m_args)
    m0_s = jnp.pad(jnp.repeat(state_m_m[l].T, TS, axis=1), ((0, SUBLANES - M_HEADS), (0, 0)))
    ym_s, s_m_C, s_m_n, s_m_m, s_m_conv = _mlstm(
        proj_s, BS, TS, s_blk, TS, M_BLOCKS_SAMPLE, _pad_history(state_m_conv[l]), state_m_C[l],
        state_m_n[l].reshape(BS, M_HEADS, 1, M_DH), m0_s, *m_args)

    x1, info, cnt = _outproj(xp, xs, yrg_p, yrg_s, ym_p, ym_s, w_out[l].astype(bf16),
                             row(ln1_g[l]), row(ln1_b[l]), wr_hi, wr_lo, b_route)

    n_tok = n_prompt + n_sample
    n_tiles = (n_tok * 2) // TM + N_EXPERTS
    counts = cnt[0, :N_EXPERTS].astype(jnp.int32)
    padded = ((counts + TM - 1) // TM) * TM
    ends = jnp.cumsum(padded)
    offs = ends - padded
    experts = jnp.arange(N_EXPERTS, dtype=jnp.int32)
    ids = info[:, 0:2].astype(jnp.int32)
    offs_of_id = jnp.sum(jnp.where(ids[:, :, None] == experts, offs, 0), axis=-1)
    pos = (offs_of_id + info[:, 4:6].astype(jnp.int32)).reshape(n_tok // TM, 1, 2 * TM)
    tiles = jnp.arange(n_tiles, dtype=jnp.int32)
    tile_expert = jnp.minimum(jnp.sum(tiles[:, None] >= (ends // TM)[None, :], axis=1),
                              N_EXPERTS - 1).astype(jnp.int32)
    n_used = (ends[-1] // TM).reshape(1).astype(jnp.int32)
    last_tile_row = jnp.concatenate([jnp.where(counts > 0, ends - TM, -1).astype(jnp.int32), n_used])

    x_sorted = _dispatch(last_tile_row, pos, x1, n_tiles * TM)
    y_sorted = _experts(tile_expert, n_used, x_sorted, moe_w_gate[l], moe_w_up[l], moe_w_down[l])
    y_p, y_s = _combine(pos, x1, info, y_sorted, row(ln2_g[l]), row(ln2_b[l]), n_prompt // TM)

    return (y_p.reshape(BP, TP, D_MODEL), y_s.reshape(BS, TS, D_MODEL),
            p_rg_h.reshape(1, BP, D_RG), p_rg_conv[None], p_m_C[None], p_m_n.reshape(1, BP, M_HEADS, M_DH),
            p_m_m[None], p_m_conv[None],
            s_rg_h.reshape(1, BS, D_RG), s_rg_conv[None], s_m_C[None], s_m_n.reshape(1, BS, M_HEADS, M_DH),
            s_m_m[None], s_m_conv[None])
```

```python
import functools

import jax
import jax.numpy as jnp
from jax import lax
from jax.experimental import pallas as pl
from jax.experimental.pallas import tpu as pltpu

f32 = jnp.float32
bf16 = jnp.bfloat16

D_MODEL = 1024
D_RG = 512
RG_BLOCKS = 8
RG_C = 8.0
D_M = 512
M_HEADS = 4
M_DH = 128
CONV_W = 4
N_GROUPS = 4
EXPERTS_PER_GROUP = 8
N_EXPERTS = 32
D_EXPERT = 256
ALPHA = 2.0 ** 0.25
LN_EPS = 1e-5
M_INIT = -1.0e4

LANES = 128
SUBLANES = 8
TM = 256
N_PROJ = 2 * D_RG + 2 * D_M + LANES
RG_ROWS_PROMPT = 256
M_ROWS = 128
M_BLOCKS_PROMPT = 8
M_BLOCKS_SAMPLE = 2
VMEM_LIMIT = 56 * 1024 * 1024


def _cparams(n_axes):
    return pltpu.CompilerParams(dimension_semantics=("arbitrary",) * n_axes,
                                vmem_limit_bytes=VMEM_LIMIT)


def _inproj_kernel(xp_ref, xs_ref, w_ref, op_ref, os_ref, *, n_p):
    i = pl.program_id(0)

    def run(x_ref, o_ref):
        o_ref[...] = jnp.dot(x_ref[...].astype(bf16), w_ref[...], preferred_element_type=f32)

    @pl.when(i < n_p)
    def _():
        run(xp_ref, op_ref)

    @pl.when(i >= n_p)
    def _():
        run(xs_ref, os_ref)


def _inproj(xp, xs, w):
    n_p, n_s = xp.shape[0] // TM, xs.shape[0] // TM
    pmap = lambda i: (jnp.minimum(i, n_p - 1), 0)
    smap = lambda i: (jnp.maximum(i - n_p, 0), 0)
    return pl.pallas_call(
        functools.partial(_inproj_kernel, n_p=n_p),
        grid=(n_p + n_s,),
        in_specs=[
            pl.BlockSpec((TM, D_MODEL), pmap),
            pl.BlockSpec((TM, D_MODEL), smap),
            pl.BlockSpec((D_MODEL, N_PROJ), lambda i: (0, 0)),
        ],
        out_specs=[pl.BlockSpec((TM, N_PROJ), pmap), pl.BlockSpec((TM, N_PROJ), smap)],
        out_shape=[jax.ShapeDtypeStruct((n_p * TM, N_PROJ), f32),
                   jax.ShapeDtypeStruct((n_s * TM, N_PROJ), f32)],
        compiler_params=_cparams(1),
        name="inproj",
    )(xp, xs, w)


def _causal_conv(xs_scr, x, cw_ref, cb_ref, S, L, C):
    xs_scr[:, SUBLANES:SUBLANES + L, :] = x.reshape(S, L, C)
    acc = cb_ref[...] + cw_ref[CONV_W - 1:CONV_W, :] * x
    for j in range(CONV_W - 1):
        lo = SUBLANES - (CONV_W - 1) + j
        acc = acc + cw_ref[j:j + 1, :] * xs_scr[:, lo:lo + L, :].reshape(S * L, C)
    return acc


def _softplus(x):
    return jnp.maximum(x, 0.0) + jnp.log1p(jnp.exp(-jnp.abs(x)))


def _rglru_kernel(x_ref, g_ref, buf_ref, h0_ref, cw_ref, cb_ref, wa_ref, ba_ref, wx_ref, bx_ref,
                  lam_ref, y_ref, hN_ref, cN_ref, xs_scr, *, S, L):
    R = S * L
    t = pl.program_id(1)

    @pl.when(t == 0)
    def _():
        xs_scr[:, 0:SUBLANES, :] = buf_ref[...]
        hN_ref[...] = h0_ref[...]

    x = x_ref[...]
    xc = _causal_conv(xs_scr, x, cw_ref, cb_ref, S, L, D_RG)
    xcb = xc.astype(bf16)
    r = jax.nn.sigmoid(jnp.dot(xcb, wa_ref[...], preferred_element_type=f32) + ba_ref[...])
    ig = jax.nn.sigmoid(jnp.dot(xcb, wx_ref[...], preferred_element_type=f32) + bx_ref[...])
    log_a = (-RG_C) * r * _softplus(-lam_ref[...])
    a = jnp.exp(log_a)
    th = jnp.tanh(log_a)
    u = jnp.sqrt(-2.0 * th / (1.0 - th)) * ig * xc

    n_grp, grp_per_seq = R // SUBLANES, L // SUBLANES
    a3 = a.reshape(n_grp, SUBLANES, D_RG)
    u3 = u.reshape(n_grp, SUBLANES, D_RG)
    sub = lax.broadcasted_iota(jnp.int32, (n_grp, SUBLANES, D_RG), 1)
    s = 1
    while s < SUBLANES:
        ok = sub >= s
        a_sh = pltpu.roll(a3, s, 1)
        u_sh = pltpu.roll(u3, s, 1)
        u3 = jnp.where(ok, a3 * u_sh + u3, u3)
        a3 = jnp.where(ok, a3 * a_sh, a3)
        s *= 2
    h0 = hN_ref[...]
    groups = []
    for kg in range(n_grp):
        carry = h0[kg // grp_per_seq] if kg % grp_per_seq == 0 else groups[-1][SUBLANES - 1:SUBLANES, :]
        groups.append(a3[kg] * carry + u3[kg])
    h = jnp.concatenate(groups, axis=0)

    y_ref[...] = h * jax.nn.gelu(g_ref[...], approximate=True)
    hN_ref[...] = h.reshape(S, L, D_RG)[:, L - 1:L, :]
    xs_scr[:, 0:SUBLANES, :] = xs_scr[:, L:L + SUBLANES, :]
    cN_ref[...] = xs_scr[:, SUBLANES - (CONV_W - 1):SUBLANES, :]


def _rglru(proj, row0, B, T, S, L, buf8, h0, cw, cb, wa, ba, wx, bx, lam):
    R = S * L
    nb, nt = B // S, T // L
    blk0 = row0 // R
    row_map = lambda b, t: (blk0 + b * nt + t, 0)
    const2 = lambda b, t: (0, 0)
    return pl.pallas_call(
        functools.partial(_rglru_kernel, S=S, L=L),
        grid=(nb, nt),
        in_specs=[
            pl.BlockSpec((R, D_RG), row_map),
            pl.BlockSpec((R, D_RG), lambda b, t: (blk0 + b * nt + t, 1)),
            pl.BlockSpec((S, SUBLANES, D_RG), lambda b, t: (b, 0, 0)),
            pl.BlockSpec((S, 1, D_RG), lambda b, t: (b, 0, 0)),
            pl.BlockSpec((CONV_W, D_RG), const2),
            pl.BlockSpec((1, D_RG), const2),
            pl.BlockSpec((D_RG, D_RG), const2),
            pl.BlockSpec((1, D_RG), const2),
            pl.BlockSpec((D_RG, D_RG), const2),
            pl.BlockSpec((1, D_RG), const2),
            pl.BlockSpec((1, D_RG), const2),
        ],
        out_specs=[
            pl.BlockSpec((R, D_RG), lambda b, t: (b * nt + t, 0)),
            pl.BlockSpec((S, 1, D_RG), lambda b, t: (b, 0, 0)),
            pl.BlockSpec((S, CONV_W - 1, D_RG), lambda b, t: (b, 0, 0)),
        ],
        out_shape=[
            jax.ShapeDtypeStruct((B * T, D_RG), f32),
            jax.ShapeDtypeStruct((B, 1, D_RG), f32),
            jax.ShapeDtypeStruct((B, CONV_W - 1, D_RG), f32),
        ],
        scratch_shapes=[pltpu.VMEM((S, SUBLANES + L, D_RG), f32)],
        compiler_params=_cparams(2),
        name="rglru",
    )(proj, proj, buf8, h0, cw, cb, wa, ba, wx, bx, lam)


def _seg_scan(x, op, fill, tin, L, reverse=False):
    s = 1
    while s < L:
        if reverse:
            sh = pltpu.roll(x, LANES - s, 1)
            ok = tin < L - s
        else:
            sh = pltpu.roll(x, s, 1)
            ok = tin >= s
        x = op(x, jnp.where(ok, sh, fill))
        s *= 2
    return x


def _mlstm_kernel(x_ref, z_ref, gt_ref, buf_ref, c0_ref, n0_ref, m0_ref,
                  cw_ref, cb_ref, wqk_ref, wv_ref, wkT_ref, bif_ref, ng_ref, sk_ref,
                  y_ref, cN_ref, nN_ref, mN_ref, bN_ref, xs_scr, m_scr, *, S, L, G):
    R = S * L
    GS, GR = G * S, G * R
    j = pl.program_id(1)

    @pl.when(j == 0)
    def _():
        xs_scr[:, 0:SUBLANES, :] = buf_ref[...]
        cN_ref[...] = c0_ref[...]
        nN_ref[...] = n0_ref[...]
        m_scr[...] = m0_ref[...]

    x = x_ref[...].reshape(GR, D_M)
    xc = _causal_conv(xs_scr, x, cw_ref, cb_ref, GS, L, D_M)
    xa = xc * jax.nn.sigmoid(xc)
    xab = xa.astype(bf16)
    xb = x.astype(bf16)

    il_parts, f_parts = [], []
    for g in range(G):
        gT = (gt_ref[g] + bif_ref[...]).T
        il_parts.append(gT[0:SUBLANES, :])
        f_parts.append(pltpu.roll(gT[0:SUBLANES, :], M_HEADS, 0))
    il = jnp.concatenate(il_parts, axis=0)
    fl = -_softplus(-jnp.concatenate(f_parts, axis=0))
    tin = lax.broadcasted_iota(jnp.int32, (G * SUBLANES, LANES), 1) % L
    bcum = _seg_scan(fl, jnp.add, 0.0, tin, L)
    a = il - bcum
    m_prev = m_scr[...].reshape(G * SUBLANES, LANES)
    big_m = jnp.maximum(m_prev, _seg_scan(a, jnp.maximum, -jnp.inf, tin, L))
    m_t = bcum + big_m
    if S == 1:
        m_last = jnp.broadcast_to(big_m[:, LANES - 1:LANES], big_m.shape)
    else:
        m_last = _seg_scan(big_m, jnp.maximum, -jnp.inf, tin, L, reverse=True)
    rows = [big_m, jnp.exp(m_prev - big_m), jnp.exp(-m_t), jnp.exp(a - m_last), jnp.exp(m_prev - m_last)]
    mN_ref[...] = m_t.reshape(G, SUBLANES, LANES)
    m_scr[...] = jnp.broadcast_to(m_t[:, LANES - 1:LANES], m_t.shape).reshape(G, SUBLANES, LANES)
    pad_rows = jnp.zeros((LANES - len(rows) * SUBLANES, LANES), f32)
    cols = []
    for g in range(G):
        gs = slice(g * SUBLANES, (g + 1) * SUBLANES)
        cols.append(jnp.concatenate([r[gs] for r in rows] + [pad_rows], axis=0).T)

    def col(g, q, h):
        return cols[g][:, SUBLANES * q + h:SUBLANES * q + h + 1]

    qs, ks, vs, kTs = [], [], [], []
    for h in range(M_HEADS):
        hs = slice(h * M_DH, (h + 1) * M_DH)
        qk_h = jnp.dot(xab[:, hs], wqk_ref[h], preferred_element_type=f32)
        qs.append(qk_h[:, 0:M_DH])
        ks.append(qk_h[:, M_DH:2 * M_DH] * (M_DH ** -0.5))
        vs.append(jnp.dot(xb[:, hs], wv_ref[h], preferred_element_type=f32))
        if S == 1:
            kT_h = lax.dot_general(wkT_ref[h], xab[:, hs], (((1,), (1,)), ((), ())),
                                   preferred_element_type=f32)
            kTs.append((kT_h * (M_DH ** -0.5)).astype(bf16))

    ti = lax.broadcasted_iota(jnp.int32, (R, R), 0)
    si = lax.broadcasted_iota(jnp.int32, (R, R), 1)
    mask = (si <= ti) & ((ti // L) == (si // L))
    ones_b = jnp.ones((R, M_DH), bf16)
    pairs = [(g, h) for g in range(G) for h in range(M_HEADS)]
    blk = lambda g: slice(g * R, (g + 1) * R)
    seqs = [slice(b * L, (b + 1) * L) for b in range(S)]
    q = {(g, h): qs[h][blk(g)] for g, h in pairs}
    k = {(g, h): ks[h][blk(g)] for g, h in pairs}
    v = {(g, h): vs[h][blk(g)] for g, h in pairs}
    qb = {p: q[p].astype(bf16) for p in pairs}
    kb = {p: k[p].astype(bf16) for p in pairs}
    qk = {p: lax.dot_general(qb[p], kb[p], (((1,), (1,)), ((), ())), preferred_element_type=f32)
          for p in pairs}
    sm = {}
    for g, h in pairs:
        a_row = a[g * SUBLANES + h:g * SUBLANES + h + 1, :]
        decay = jnp.exp(jnp.where(mask, a_row - col(g, 0, h), -jnp.inf))
        sm[g, h] = (qk[g, h] * decay).astype(bf16)
    nd = {p: jnp.dot(sm[p], jnp.concatenate([v[p].astype(bf16), ones_b], axis=1),
                     preferred_element_type=f32) for p in pairs}
    c_old = {(g, h, b): cN_ref[g * S + b, h] for g, h in pairs for b in range(S)}
    n_old = {(g, h, b): nN_ref[g * S + b, h] for g, h in pairs for b in range(S)}
    q_c = {(g, h, b): jnp.dot(qb[g, h][seqs[b]], c_old[g, h, b].astype(bf16), preferred_element_type=f32)
           for g, h in pairs for b in range(S)}
    hh = {}
    for g, h in pairs:
        g_col, e_col = col(g, 1, h), col(g, 2, h)
        pieces = []
        for b, rs in enumerate(seqs):
            q_n = jnp.sum(q[g, h][rs] * n_old[g, h, b], axis=1, keepdims=True)
            num = nd[g, h][rs, 0:M_DH] + g_col[rs] * q_c[g, h, b]
            den = nd[g, h][rs, M_DH:2 * M_DH] + g_col[rs] * q_n
            pieces.append(num / jnp.maximum(jnp.abs(den), e_col[rs]))
        hh[g, h] = pieces[0] if S == 1 else jnp.concatenate(pieces, axis=0)
    mu = {p: jnp.mean(hh[p], axis=1, keepdims=True) for p in pairs}
    var = {p: jnp.mean(jnp.square(hh[p] - mu[p]), axis=1, keepdims=True) for p in pairs}
    hn_blocks = [jnp.concatenate([(hh[g, h] - mu[g, h]) * lax.rsqrt(var[g, h] + LN_EPS)
                                  for h in range(M_HEADS)], axis=1) for g in range(G)]
    new_c, new_n = [], []
    for g, h in pairs:
        w_col = col(g, 3, h)
        wv = (w_col * v[g, h]).astype(bf16)
        wk = w_col * k[g, h]
        for b, rs in enumerate(seqs):
            g_end = col(g, 4, h)[(b + 1) * L - 1:(b + 1) * L, :]
            if S == 1:
                kv = jnp.dot(kTs[h][:, blk(g)], wv, preferred_element_type=f32)
            else:
                kv = lax.dot_general(kb[g, h][rs], wv[rs], (((0,), (0,)), ((), ())),
                                     preferred_element_type=f32)
            new_c.append((g * S + b, h, g_end * c_old[g, h, b] + kv))
            new_n.append((g * S + b, h, g_end * n_old[g, h, b] + jnp.sum(wk[rs], axis=0, keepdims=True)))

    hn = jnp.concatenate(hn_blocks, axis=0) * ng_ref[...]
    y = jax.nn.sigmoid(z_ref[...].reshape(GR, D_M)) * (hn + sk_ref[...] * xa)
    y_ref[...] = y.reshape(G, R, D_M)
    for sq, h, val in new_c:
        cN_ref[sq, h] = val
    for sq, h, val in new_n:
        nN_ref[sq, h] = val
    xs_scr[:, 0:SUBLANES, :] = xs_scr[:, L:L + SUBLANES, :]
    bN_ref[...] = xs_scr[:, SUBLANES - (CONV_W - 1):SUBLANES, :]


def _mlstm(proj, B, T, S, L, G, buf8, c0, n0, m0_lanes, cw, cb, wqk, wv, wkT, bif, ng, sk):
    R = S * L
    assert R == M_ROWS == LANES
    nblk, nc = B // S, T // L
    nb = nblk // G
    assert nblk * S == B and nc * L == T and nb * G == nblk and (S == 1 or nc == 1)
    proj4 = proj.reshape(nblk, nc, R, N_PROJ)
    m0_3 = m0_lanes.reshape(SUBLANES, nblk, LANES).transpose(1, 0, 2)
    const2 = lambda b, j: (0, 0)
    const3 = lambda b, j: (0, 0, 0)
    gate_blk = (2 * D_RG + 2 * D_M) // LANES
    GS = G * S
    outs = pl.pallas_call(
        functools.partial(_mlstm_kernel, S=S, L=L, G=G),
        grid=(nb, nc),
        in_specs=[
            pl.BlockSpec((G, None, R, D_M), lambda b, j: (b, j, 0, 2)),
            pl.BlockSpec((G, None, R, D_M), lambda b, j: (b, j, 0, 3)),
            pl.BlockSpec((G, None, R, LANES), lambda b, j: (b, j, 0, gate_blk)),
            pl.BlockSpec((GS, SUBLANES, D_M), lambda b, j: (b, 0, 0)),
            pl.BlockSpec((GS, M_HEADS, M_DH, M_DH), lambda b, j: (b, 0, 0, 0)),
            pl.BlockSpec((GS, M_HEADS, 1, M_DH), lambda b, j: (b, 0, 0, 0)),
            pl.BlockSpec((G, SUBLANES, LANES), lambda b, j: (b, 0, 0)),
            pl.BlockSpec((CONV_W, D_M), const2),
            pl.BlockSpec((1, D_M), const2),
            pl.BlockSpec((M_HEADS, M_DH, 2 * M_DH), const3),
            pl.BlockSpec((M_HEADS, M_DH, M_DH), const3),
            pl.BlockSpec((M_HEADS, M_DH, M_DH), const3),
            pl.BlockSpec((1, LANES), const2),
            pl.BlockSpec((1, D_M), const2),
            pl.BlockSpec((1, D_M), const2),
        ],
        out_specs=[
            pl.BlockSpec((G, None, R, D_M), lambda b, j: (b, j, 0, 0)),
            pl.BlockSpec((GS, M_HEADS, M_DH, M_DH), lambda b, j: (b, 0, 0, 0)),
            pl.BlockSpec((GS, M_HEADS, 1, M_DH), lambda b, j: (b, 0, 0, 0)),
            pl.BlockSpec((G, None, SUBLANES, LANES), lambda b, j: (b, j, 0, 0)),
            pl.BlockSpec((GS, CONV_W - 1, D_M), lambda b, j: (b, 0, 0)),
        ],
        out_shape=[
            jax.ShapeDtypeStruct((nblk, nc, R, D_M), f32),
            jax.ShapeDtypeStruct((B, M_HEADS, M_DH, M_DH), f32),
            jax.ShapeDtypeStruct((B, M_HEADS, 1, M_DH), f32),
            jax.ShapeDtypeStruct((nblk, nc, SUBLANES, LANES), f32),
            jax.ShapeDtypeStruct((B, CONV_W - 1, D_M), f32),
        ],
        scratch_shapes=[pltpu.VMEM((GS, SUBLANES + L, D_M), f32), pltpu.VMEM((G, SUBLANES, LANES), f32)],
        compiler_params=_cparams(2),
        name="mlstm",
    )(proj4, proj4, proj4, buf8, c0, n0, m0_3, cw, cb, wqk, wv, wkT, bif, ng, sk)
    y, c_new, n_new, m_t, b_new = outs
    m_last = m_t[:, nc - 1, :M_HEADS, :].reshape(nblk, M_HEADS, S, L)[:, :, :, L - 1]
    m_last = m_last.transpose(0, 2, 1).reshape(B, M_HEADS)
    return y.reshape(B * T, D_M), c_new, n_new, m_last, b_new


def _layer_norm(x, g, b):
    mu = jnp.mean(x, axis=-1, keepdims=True)
    var = jnp.mean(jnp.square(x - mu), axis=-1, keepdims=True)
    return (x - mu) * lax.rsqrt(var + LN_EPS) * g + b


def _first_lane_of_max(vals, lane_f):
    vmax = jnp.max(vals, axis=1, keepdims=True)
    idx = jnp.min(jnp.where(vals == vmax, lane_f, float(LANES)), axis=1, keepdims=True)
    return vmax, idx


def _outproj_kernel(xp_ref, xs_ref, rgp_ref, rgs_ref, mp_ref, ms_ref, wo_ref, g1_ref, b1_ref,
                    wrh_ref, wrl_ref, br_ref, x1_ref, info_ref, cnt_ref, *, n_p):
    i = pl.program_id(0)

    @pl.when(i == 0)
    def _():
        cnt_ref[...] = jnp.zeros_like(cnt_ref)

    def run(x_ref, rg_ref, m_ref):
        mix = jnp.dot(rg_ref[...].astype(bf16), wo_ref[0:D_RG, :], preferred_element_type=f32)
        mix = mix + jnp.dot(m_ref[...].astype(bf16), wo_ref[D_RG:D_RG + D_M, :], preferred_element_type=f32)
        x1 = _layer_norm(ALPHA * x_ref[...] + mix, g1_ref[...], b1_ref[...])
        x1_ref[...] = x1

        hi = x1.astype(bf16)
        lo = (x1 - hi.astype(f32)).astype(bf16)
        lg = (jnp.dot(hi, wrh_ref[...], preferred_element_type=f32)
              + jnp.dot(lo, wrh_ref[...], preferred_element_type=f32)
              + jnp.dot(hi, wrl_ref[...], preferred_element_type=f32)) + br_ref[...]
        lane = lax.broadcasted_iota(jnp.int32, (TM, LANES), 1)
        lane_f = lane.astype(f32)
        neg = -jnp.inf
        gl = jnp.where(lane < N_GROUPS, lg, neg)
        gmax, gidx = _first_lane_of_max(gl, lane_f)
        p_g = 1.0 / jnp.sum(jnp.exp(gl - gmax), axis=1, keepdims=True)
        e_lo = float(N_GROUPS) + float(EXPERTS_PER_GROUP) * gidx
        el = jnp.where((lane_f >= e_lo) & (lane_f < e_lo + float(EXPERTS_PER_GROUP)), lg, neg)
        v1, i1 = _first_lane_of_max(el, lane_f)
        v2, i2 = _first_lane_of_max(jnp.where(lane_f == i1, neg, el), lane_f)
        d = jnp.exp(v2 - v1)
        w1 = p_g / (1.0 + d)
        w2 = p_g * d / (1.0 + d)
        id1 = i1 - float(N_GROUPS)
        id2 = i2 - float(N_GROUPS)

        hot1 = lane_f == id1
        hot2 = lane_f == id2
        onehot = (hot1 | hot2).astype(bf16)
        ti = lax.broadcasted_iota(jnp.int32, (TM, TM), 0)
        si = lax.broadcasted_iota(jnp.int32, (TM, TM), 1)
        tri = (si <= ti).astype(bf16)
        cum = jnp.dot(tri, onehot, preferred_element_type=f32)
        rank = cum - 1.0 + cnt_ref[...]
        r1 = jnp.sum(jnp.where(hot1, rank, 0.0), axis=1, keepdims=True)
        r2 = jnp.sum(jnp.where(hot2, rank, 0.0), axis=1, keepdims=True)
        cnt_ref[...] = cnt_ref[...] + cum[TM - 1:TM, :]

        info = jnp.zeros((TM, LANES), f32)
        for c, val in enumerate((id1, id2, w1, w2, r1, r2)):
            info = jnp.where(lane == c, val, info)
        info_ref[...] = info

    @pl.when(i < n_p)
    def _():
        run(xp_ref, rgp_ref, mp_ref)

    @pl.when(i >= n_p)
    def _():
        run(xs_ref, rgs_ref, ms_ref)


def _outproj(xp, xs, rgp, rgs, mp, ms, wo, g1, b1, wrh, wrl, br):
    n_p, n_s = xp.shape[0] // TM, xs.shape[0] // TM
    n = n_p + n_s
    pmap = lambda i: (jnp.minimum(i, n_p - 1), 0)
    smap = lambda i: (jnp.maximum(i - n_p, 0), 0)
    const = lambda i: (0, 0)
    return pl.pallas_call(
        functools.partial(_outproj_kernel, n_p=n_p),
        grid=(n,),
        in_specs=[
            pl.BlockSpec((TM, D_MODEL), pmap), pl.BlockSpec((TM, D_MODEL), smap),
            pl.BlockSpec((TM, D_RG), pmap), pl.BlockSpec((TM, D_RG), smap),
            pl.BlockSpec((TM, D_M), pmap), pl.BlockSpec((TM, D_M), smap),
            pl.BlockSpec((D_RG + D_M, D_MODEL), const),
            pl.BlockSpec((1, D_MODEL), const), pl.BlockSpec((1, D_MODEL), const),
            pl.BlockSpec((D_MODEL, LANES), const), pl.BlockSpec((D_MODEL, LANES), const),
            pl.BlockSpec((1, LANES), const),
        ],
        out_specs=[
            pl.BlockSpec((TM, D_MODEL), lambda i: (i, 0)),
            pl.BlockSpec((TM, LANES), lambda i: (i, 0)),
            pl.BlockSpec((1, LANES), const),
        ],
        out_shape=[
            jax.ShapeDtypeStruct((n * TM, D_MODEL), f32),
            jax.ShapeDtypeStruct((n * TM, LANES), f32),
            jax.ShapeDtypeStruct((1, LANES), f32),
        ],
        compiler_params=_cparams(1),
        name="outproj",
    )(xp, xs, rgp, rgs, mp, ms, wo, g1, b1, wrh, wrl, br)


def _row_copy(src_ref, src_row, dst_ref, dst_row, sem):
    return pltpu.make_async_copy(src_ref.at[pl.ds(src_row, 1)], dst_ref.at[pl.ds(dst_row, 1)], sem)


def _tile_rows_wait(src_ref, dst_ref, sem):
    pltpu.make_async_copy(src_ref.at[pl.ds(0, TM)], dst_ref.at[pl.ds(0, TM)], sem).wait()


def _invert_kernel(gap_ref, pos_ref, src_ref):
    i = pl.program_id(0)

    @pl.when(i == 0)
    def _():
        def clear(p, c):
            src_ref[p] = 0
            return c

        for e in range(N_EXPERTS + 1):
            lax.fori_loop(gap_ref[2 * e], gap_ref[2 * e + 1], clear, 0)

    base = i * TM

    def fill(r, c):
        src_ref[pos_ref[0, 0, 2 * r]] = base + r
        src_ref[pos_ref[0, 0, 2 * r + 1]] = base + r
        return c

    lax.fori_loop(0, TM, fill, 0, unroll=8)


def _invert(gaps, pos, n_rows):
    n = pos.shape[0]
    return pl.pallas_call(
        _invert_kernel,
        grid_spec=pltpu.PrefetchScalarGridSpec(
            num_scalar_prefetch=1,
            grid=(n,),
            in_specs=[pl.BlockSpec((1, 1, 2 * TM), lambda i, gp: (i, 0, 0), memory_space=pltpu.SMEM)],
            out_specs=pl.BlockSpec(memory_space=pltpu.SMEM),
        ),
        out_shape=jax.ShapeDtypeStruct((n_rows,), jnp.int32),
        compiler_params=_cparams(1),
        name="invert",
    )(gaps, pos)


def _expert_kernel(te_ref, nu_ref, src_ref, srcn_ref, x1_ref, wg_ref, wu_ref, wd_ref, y_ref, xbuf, sem):
    i = pl.program_id(0)
    n_used = nu_ref[0]
    slot = i % 2

    def gather(s_ref, s):
        def start(r, c):
            _row_copy(x1_ref, s_ref[0, 0, r], xbuf.at[s], r, sem.at[s]).start()
            return c

        lax.fori_loop(0, TM, start, 0, unroll=8)

    @pl.when(i == 0)
    def _():
        gather(src_ref, 0)

    @pl.when(i + 1 < n_used)
    def _():
        gather(srcn_ref, 1 - slot)

    @pl.when(i < n_used)
    def _():
        _tile_rows_wait(x1_ref, xbuf.at[slot], sem.at[slot])
        x = xbuf[slot].astype(bf16)
        hg = jnp.dot(x, wg_ref[0].astype(bf16), preferred_element_type=f32)
        hu = jnp.dot(x, wu_ref[0].astype(bf16), preferred_element_type=f32)
        hmid = (hg * jax.nn.sigmoid(hg) * hu).astype(bf16)
        y_ref[...] = jnp.dot(hmid, wd_ref[0].astype(bf16), preferred_element_type=f32)

    @pl.when(i >= n_used)
    def _():
        y_ref[...] = jnp.zeros_like(y_ref)


def _experts(tile_expert, n_used, src, x1, wg, wu, wd):
    nt = src.shape[0]
    wmap = lambda i, te, nu: (te[i], 0, 0)
    return pl.pallas_call(
        _expert_kernel,
        grid_spec=pltpu.PrefetchScalarGridSpec(
            num_scalar_prefetch=2,
            grid=(nt,),
            in_specs=[
                pl.BlockSpec((1, 1, TM), lambda i, te, nu: (i, 0, 0), memory_space=pltpu.SMEM),
                pl.BlockSpec((1, 1, TM), lambda i, te, nu: (jnp.minimum(i + 1, nt - 1), 0, 0),
                             memory_space=pltpu.SMEM),
                pl.BlockSpec(memory_space=pl.ANY),
                pl.BlockSpec((1, D_MODEL, D_EXPERT), wmap),
                pl.BlockSpec((1, D_MODEL, D_EXPERT), wmap),
                pl.BlockSpec((1, D_EXPERT, D_MODEL), wmap),
            ],
            out_specs=pl.BlockSpec((TM, D_MODEL), lambda i, te, nu: (i, 0)),
            scratch_shapes=[pltpu.VMEM((2, TM, D_MODEL), f32), pltpu.SemaphoreType.DMA((2,))],
        ),
        out_shape=jax.ShapeDtypeStruct((nt * TM, D_MODEL), f32),
        compiler_params=_cparams(1),
        name="experts",
    )(tile_expert, n_used, src, src, x1, wg, wu, wd)


def _combine_kernel(pos_ref, posn_ref, x1_ref, info_ref, ys_ref, g2_ref, b2_ref, op_ref, os_ref,
                    ya_scr, yb_scr, sem, *, n_p):
    i = pl.program_id(0)
    n = pl.num_programs(0)
    slot = i % 2

    def gather(p_ref, s):
        def start(r, c):
            _row_copy(ys_ref, p_ref[0, 0, 2 * r], ya_scr.at[s], r, sem.at[s]).start(priority=0)
            _row_copy(ys_ref, p_ref[0, 0, 2 * r + 1], yb_scr.at[s], r, sem.at[s]).start(priority=1)
            return c

        lax.fori_loop(0, TM, start, 0, unroll=8)

    @pl.when(i == 0)
    def _():
        gather(pos_ref, 0)

    @pl.when(i + 1 < n)
    def _():
        gather(posn_ref, 1 - slot)

    _tile_rows_wait(ys_ref, ya_scr.at[slot], sem.at[slot])
    _tile_rows_wait(ys_ref, yb_scr.at[slot], sem.at[slot])

    info = info_ref[...]
    ffn = info[:, 2:3] * ya_scr[slot] + info[:, 3:4] * yb_scr[slot]
    out = _layer_norm(ALPHA * x1_ref[...] + ffn, g2_ref[...], b2_ref[...])

    @pl.when(i < n_p)
    def _():
        op_ref[...] = out

    @pl.when(i >= n_p)
    def _():
        os_ref[...] = out


def _combine(pos, x1, info, ys, g2, b2, n_p):
    n = x1.shape[0] // TM
    n_s = n - n_p
    const = lambda i: (0, 0)
    return pl.pallas_call(
        functools.partial(_combine_kernel, n_p=n_p),
        grid=(n,),
        in_specs=[
            pl.BlockSpec((1, 1, 2 * TM), lambda i: (i, 0, 0), memory_space=pltpu.SMEM),
            pl.BlockSpec((1, 1, 2 * TM), lambda i: (jnp.minimum(i + 1, n - 1), 0, 0),
                         memory_space=pltpu.SMEM),
            pl.BlockSpec((TM, D_MODEL), lambda i: (i, 0)),
            pl.BlockSpec((TM, LANES), lambda i: (i, 0)),
            pl.BlockSpec(memory_space=pl.ANY),
            pl.BlockSpec((1, D_MODEL), const), pl.BlockSpec((1, D_MODEL), const),
        ],
        out_specs=[
            pl.BlockSpec((TM, D_MODEL), lambda i: (jnp.minimum(i, n_p - 1), 0)),
            pl.BlockSpec((TM, D_MODEL), lambda i: (jnp.maximum(i - n_p, 0), 0)),
        ],
        out_shape=[
            jax.ShapeDtypeStruct((n_p * TM, D_MODEL), f32),
            jax.ShapeDtypeStruct((n_s * TM, D_MODEL), f32),
        ],
        scratch_shapes=[pltpu.VMEM((2, TM, D_MODEL), f32), pltpu.VMEM((2, TM, D_MODEL), f32),
                        pltpu.SemaphoreType.DMA((2,))],
        compiler_params=_cparams(1),
        name="combine",
    )(pos, pos, x1, info, ys, g2, b2)


def _block_diag(w):
    n, d, _ = w.shape
    eye = jnp.eye(n, dtype=w.dtype)
    return (eye[:, None, :, None] * w[:, :, None, :]).reshape(n * d, n * d)


def _pad_history(buf):
    return jnp.pad(buf, ((0, 0), (SUBLANES - (CONV_W - 1), 0), (0, 0)))


def kernel(x_prompt, x_sample, state_rg_h, state_rg_conv, state_m_C, state_m_n, state_m_m, state_m_conv, w_in, rg_conv_w, rg_conv_b, rg_w_a, rg_b_a, rg_w_x, rg_b_x, rg_lambda, m_conv_w, m_conv_b, m_w_q, m_w_k, m_w_v, m_b_i, m_b_f, m_norm_g, m_skip, w_out, ln1_g, ln1_b, ln2_g, ln2_b, moe_w_group, moe_b_group, moe_w_expert, moe_b_expert, moe_w_gate, moe_w_up, moe_w_down):
    BP, TP, _ = x_prompt.shape
    BS, TS, _ = x_sample.shape
    n_prompt, n_sample = BP * TP, BS * TS
    xp = x_prompt.reshape(n_prompt, D_MODEL)
    xs = x_sample.reshape(n_sample, D_MODEL)
    l = 0

    w_in_p = jnp.pad(w_in[l], ((0, 0), (0, N_PROJ - w_in.shape[-1]))).astype(bf16)
    wa = _block_diag(rg_w_a[l]).astype(bf16)
    wx = _block_diag(rg_w_x[l]).astype(bf16)
    row = lambda v: v.reshape(1, -1)
    bif = jnp.pad(jnp.concatenate([m_b_i[l], m_b_f[l]]), (0, LANES - 2 * M_HEADS)).reshape(1, LANES)
    wqk = jnp.concatenate([m_w_q[l], m_w_k[l]], axis=-1).astype(bf16)
    wv = m_w_v[l].astype(bf16)
    wkT = m_w_k[l].transpose(0, 2, 1).astype(bf16)
    w_route = jnp.pad(jnp.concatenate([moe_w_group[l], moe_w_expert[l]], axis=1),
                      ((0, 0), (0, LANES - N_GROUPS - N_EXPERTS)))
    wr_hi = w_route.astype(bf16)
    wr_lo = (w_route - wr_hi.astype(f32)).astype(bf16)
    b_route = jnp.pad(jnp.concatenate([moe_b_group[l], moe_b_expert[l]]),
                      (0, LANES - N_GROUPS - N_EXPERTS)).reshape(1, LANES)

    proj_p, proj_s = _inproj(xp, xs, w_in_p)

    rg_args = (rg_conv_w[l], row(rg_conv_b[l]), wa, row(rg_b_a[l]), wx, row(rg_b_x[l]), row(rg_lambda[l]))
    zeros = lambda *s: jnp.zeros(s, f32)
    s_blk = M_ROWS // TS
    yrg_p, p_rg_h, p_rg_conv = _rglru(proj_p, 0, BP, TP, 1, RG_ROWS_PROMPT, zeros(BP, SUBLANES, D_RG),
                                      zeros(BP, 1, D_RG), *rg_args)
    yrg_s, s_rg_h, s_rg_conv = _rglru(proj_s, 0, BS, TS, s_blk, TS, _pad_history(state_rg_conv[l]),
                                      state_rg_h[l].reshape(BS, 1, D_RG), *rg_args)

    m_args = (m_conv_w[l], row(m_conv_b[l]), wqk, wv, wkT, bif, row(m_norm_g[l]), row(m_skip[l]))
    m0_p = jnp.full((SUBLANES, BP * LANES), M_INIT, f32)
    ym_p, p_m_C, p_m_n, p_m_m, p_m_conv = _mlstm(
        proj_p, BP, TP, 1, M_ROWS, M_BLOCKS_PROMPT, zeros(BP, SUBLANES, D_M), zeros(BP, M_HEADS, M_DH, M_DH),
        zeros(BP, M_HEADS, 1, M_DH), m0_p, *m_args)
    m0_s = jnp.pad(jnp.repeat(state_m_m[l].T, TS, axis=1), ((0, SUBLANES - M_HEADS), (0, 0)))
    ym_s, s_m_C, s_m_n, s_m_m, s_m_conv = _mlstm(
        proj_s, BS, TS, s_blk, TS, M_BLOCKS_SAMPLE, _pad_history(state_m_conv[l]), state_m_C[l],
        state_m_n[l].reshape(BS, M_HEADS, 1, M_DH), m0_s, *m_args)

    x1, info, cnt = _outproj(xp, xs, yrg_p, yrg_s, ym_p, ym_s, w_out[l].astype(bf16),
                             row(ln1_g[l]), row(ln1_b[l]), wr_hi, wr_lo, b_route)

    n_tok = n_prompt + n_sample
    n_tiles = (n_tok * 2) // TM + N_EXPERTS
    counts = cnt[0, :N_EXPERTS].astype(jnp.int32)
    padded = ((counts + TM - 1) // TM) * TM
    ends = jnp.cumsum(padded)
    offs = ends - padded
    experts = jnp.arange(N_EXPERTS, dtype=jnp.int32)
    ids = info[:, 0:2].astype(jnp.int32)
    offs_of_id = jnp.sum(jnp.where(ids[:, :, None] == experts, offs, 0), axis=-1)
    pos = (offs_of_id + info[:, 4:6].astype(jnp.int32)).reshape(n_tok // TM, 1, 2 * TM)
    tiles = jnp.arange(n_tiles, dtype=jnp.int32)
    tile_expert = jnp.minimum(jnp.sum(tiles[:, None] >= (ends // TM)[None, :], axis=1),
                              N_EXPERTS - 1).astype(jnp.int32)
    n_used = (ends[-1] // TM).reshape(1).astype(jnp.int32)
    gaps = jnp.stack([jnp.concatenate([offs + counts, ends[-1:]]),
                      jnp.concatenate([ends, jnp.full((1,), n_tiles * TM, jnp.int32)])],
                     axis=1).reshape(-1).astype(jnp.int32)

    src = _invert(gaps, pos, n_tiles * TM).reshape(n_tiles, 1, TM)
    y_sorted = _experts(tile_expert, n_used, src, x1, moe_w_gate[l], moe_w_up[l], moe_w_down[l])
    y_p, y_s = _combine(pos, x1, info, y_sorted, row(ln2_g[l]), row(ln2_b[l]), n_prompt // TM)

    return (y_p.reshape(BP, TP, D_MODEL), y_s.reshape(BS, TS, D_MODEL),
            p_rg_h.reshape(1, BP, D_RG), p_rg_conv[None], p_m_C[None], p_m_n.reshape(1, BP, M_HEADS, M_DH),
            p_m_m[None], p_m_conv[None],
            s_rg_h.reshape(1, BS, D_RG), s_rg_conv[None], s_m_C[None], s_m_n.reshape(1, BS, M_HEADS, M_DH),
            s_m_m[None], s_m_conv[None])
```

```python
import functools

import jax
import jax.numpy as jnp
from jax import lax
from jax.experimental import pallas as pl
from jax.experimental.pallas import tpu as pltpu

f32 = jnp.float32
bf16 = jnp.bfloat16

D_MODEL = 1024
D_RG = 512
RG_BLOCKS = 8
RG_C = 8.0
D_M = 512
M_HEADS = 4
M_DH = 128
CONV_W = 4
N_GROUPS = 4
EXPERTS_PER_GROUP = 8
N_EXPERTS = 32
D_EXPERT = 256
ALPHA = 2.0 ** 0.25
LN_EPS = 1e-5
M_INIT = -1.0e4

LANES = 128
SUBLANES = 8
TM = 256
TM_MOE = 128
PAIRS_PER_GROUP = EXPERTS_PER_GROUP * (EXPERTS_PER_GROUP - 1) // 2
N_CLASSES = N_GROUPS * PAIRS_PER_GROUP
X_ROW = D_MODEL + LANES
N_PROJ = 2 * D_RG + 2 * D_M + LANES
RG_ROWS_PROMPT = 256
M_ROWS = 128
M_BLOCKS_PROMPT = 8
M_BLOCKS_SAMPLE = 2
VMEM_LIMIT = 56 * 1024 * 1024


def _cparams(n_axes):
    return pltpu.CompilerParams(dimension_semantics=("arbitrary",) * n_axes,
                                vmem_limit_bytes=VMEM_LIMIT)


def _inproj_kernel(xp_ref, xs_ref, w_ref, op_ref, os_ref, *, n_p):
    i = pl.program_id(0)

    def run(x_ref, o_ref):
        o_ref[...] = jnp.dot(x_ref[...].astype(bf16), w_ref[...], preferred_element_type=f32)

    @pl.when(i < n_p)
    def _():
        run(xp_ref, op_ref)

    @pl.when(i >= n_p)
    def _():
        run(xs_ref, os_ref)


def _inproj(xp, xs, w):
    n_p, n_s = xp.shape[0] // TM, xs.shape[0] // TM
    pmap = lambda i: (jnp.minimum(i, n_p - 1), 0)
    smap = lambda i: (jnp.maximum(i - n_p, 0), 0)
    return pl.pallas_call(
        functools.partial(_inproj_kernel, n_p=n_p),
        grid=(n_p + n_s,),
        in_specs=[
            pl.BlockSpec((TM, D_MODEL), pmap),
            pl.BlockSpec((TM, D_MODEL), smap),
            pl.BlockSpec((D_MODEL, N_PROJ), lambda i: (0, 0)),
        ],
        out_specs=[pl.BlockSpec((TM, N_PROJ), pmap), pl.BlockSpec((TM, N_PROJ), smap)],
        out_shape=[jax.ShapeDtypeStruct((n_p * TM, N_PROJ), f32),
                   jax.ShapeDtypeStruct((n_s * TM, N_PROJ), f32)],
        compiler_params=_cparams(1),
        name="inproj",
    )(xp, xs, w)


def _causal_conv(xs_scr, x, cw_ref, cb_ref, S, L, C):
    xs_scr[:, SUBLANES:SUBLANES + L, :] = x.reshape(S, L, C)
    acc = cb_ref[...] + cw_ref[CONV_W - 1:CONV_W, :] * x
    for j in range(CONV_W - 1):
        lo = SUBLANES - (CONV_W - 1) + j
        acc = acc + cw_ref[j:j + 1, :] * xs_scr[:, lo:lo + L, :].reshape(S * L, C)
    return acc


def _softplus(x):
    return jnp.maximum(x, 0.0) + jnp.log1p(jnp.exp(-jnp.abs(x)))


def _rglru_kernel(x_ref, g_ref, buf_ref, h0_ref, cw_ref, cb_ref, wa_ref, ba_ref, wx_ref, bx_ref,
                  lam_ref, y_ref, hN_ref, cN_ref, xs_scr, *, S, L):
    R = S * L
    t = pl.program_id(1)

    @pl.when(t == 0)
    def _():
        xs_scr[:, 0:SUBLANES, :] = buf_ref[...]
        hN_ref[...] = h0_ref[...]

    x = x_ref[...]
    xc = _causal_conv(xs_scr, x, cw_ref, cb_ref, S, L, D_RG)
    xcb = xc.astype(bf16)
    r = jax.nn.sigmoid(jnp.dot(xcb, wa_ref[...], preferred_element_type=f32) + ba_ref[...])
    ig = jax.nn.sigmoid(jnp.dot(xcb, wx_ref[...], preferred_element_type=f32) + bx_ref[...])
    log_a = (-RG_C) * r * _softplus(-lam_ref[...])
    a = jnp.exp(log_a)
    th = jnp.tanh(log_a)
    u = jnp.sqrt(-2.0 * th / (1.0 - th)) * ig * xc

    n_grp, grp_per_seq = R // SUBLANES, L // SUBLANES
    a3 = a.reshape(n_grp, SUBLANES, D_RG)
    u3 = u.reshape(n_grp, SUBLANES, D_RG)
    sub = lax.broadcasted_iota(jnp.int32, (n_grp, SUBLANES, D_RG), 1)
    s = 1
    while s < SUBLANES:
        ok = sub >= s
        a_sh = pltpu.roll(a3, s, 1)
        u_sh = pltpu.roll(u3, s, 1)
        u3 = jnp.where(ok, a3 * u_sh + u3, u3)
        a3 = jnp.where(ok, a3 * a_sh, a3)
        s *= 2
    h0 = hN_ref[...]
    groups = []
    for kg in range(n_grp):
        carry = h0[kg // grp_per_seq] if kg % grp_per_seq == 0 else groups[-1][SUBLANES - 1:SUBLANES, :]
        groups.append(a3[kg] * carry + u3[kg])
    h = jnp.concatenate(groups, axis=0)

    y_ref[...] = h * jax.nn.gelu(g_ref[...], approximate=True)
    hN_ref[...] = h.reshape(S, L, D_RG)[:, L - 1:L, :]
    xs_scr[:, 0:SUBLANES, :] = xs_scr[:, L:L + SUBLANES, :]
    cN_ref[...] = xs_scr[:, SUBLANES - (CONV_W - 1):SUBLANES, :]


def _rglru(proj, row0, B, T, S, L, buf8, h0, cw, cb, wa, ba, wx, bx, lam):
    R = S * L
    nb, nt = B // S, T // L
    blk0 = row0 // R
    row_map = lambda b, t: (blk0 + b * nt + t, 0)
    const2 = lambda b, t: (0, 0)
    return pl.pallas_call(
        functools.partial(_rglru_kernel, S=S, L=L),
        grid=(nb, nt),
        in_specs=[
            pl.BlockSpec((R, D_RG), row_map),
            pl.BlockSpec((R, D_RG), lambda b, t: (blk0 + b * nt + t, 1)),
            pl.BlockSpec((S, SUBLANES, D_RG), lambda b, t: (b, 0, 0)),
            pl.BlockSpec((S, 1, D_RG), lambda b, t: (b, 0, 0)),
            pl.BlockSpec((CONV_W, D_RG), const2),
            pl.BlockSpec((1, D_RG), const2),
            pl.BlockSpec((D_RG, D_RG), const2),
            pl.BlockSpec((1, D_RG), const2),
            pl.BlockSpec((D_RG, D_RG), const2),
            pl.BlockSpec((1, D_RG), const2),
            pl.BlockSpec((1, D_RG), const2),
        ],
        out_specs=[
            pl.BlockSpec((R, D_RG), lambda b, t: (b * nt + t, 0)),
            pl.BlockSpec((S, 1, D_RG), lambda b, t: (b, 0, 0)),
            pl.BlockSpec((S, CONV_W - 1, D_RG), lambda b, t: (b, 0, 0)),
        ],
        out_shape=[
            jax.ShapeDtypeStruct((B * T, D_RG), f32),
            jax.ShapeDtypeStruct((B, 1, D_RG), f32),
            jax.ShapeDtypeStruct((B, CONV_W - 1, D_RG), f32),
        ],
        scratch_shapes=[pltpu.VMEM((S, SUBLANES + L, D_RG), f32)],
        compiler_params=_cparams(2),
        name="rglru",
    )(proj, proj, buf8, h0, cw, cb, wa, ba, wx, bx, lam)


def _seg_scan(x, op, fill, tin, L, reverse=False):
    s = 1
    while s < L:
        if reverse:
            sh = pltpu.roll(x, LANES - s, 1)
            ok = tin < L - s
        else:
            sh = pltpu.roll(x, s, 1)
            ok = tin >= s
        x = op(x, jnp.where(ok, sh, fill))
        s *= 2
    return x


def _mlstm_kernel(x_ref, z_ref, gt_ref, buf_ref, c0_ref, n0_ref, m0_ref,
                  cw_ref, cb_ref, wqk_ref, wv_ref, wkT_ref, bif_ref, ng_ref, sk_ref,
                  y_ref, cN_ref, nN_ref, mN_ref, bN_ref, xs_scr, m_scr, *, S, L, G):
    R = S * L
    GS, GR = G * S, G * R
    j = pl.program_id(1)

    @pl.when(j == 0)
    def _():
        xs_scr[:, 0:SUBLANES, :] = buf_ref[...]
        cN_ref[...] = c0_ref[...]
        nN_ref[...] = n0_ref[...]
        m_scr[...] = m0_ref[...]

    x = x_ref[...].reshape(GR, D_M)
    xc = _causal_conv(xs_scr, x, cw_ref, cb_ref, GS, L, D_M)
    xa = xc * jax.nn.sigmoid(xc)
    xab = xa.astype(bf16)
    xb = x.astype(bf16)

    il_parts, f_parts = [], []
    for g in range(G):
        gT = (gt_ref[g] + bif_ref[...]).T
        il_parts.append(gT[0:SUBLANES, :])
        f_parts.append(pltpu.roll(gT[0:SUBLANES, :], M_HEADS, 0))
    il = jnp.concatenate(il_parts, axis=0)
    fl = -_softplus(-jnp.concatenate(f_parts, axis=0))
    tin = lax.broadcasted_iota(jnp.int32, (G * SUBLANES, LANES), 1) % L
    bcum = _seg_scan(fl, jnp.add, 0.0, tin, L)
    a = il - bcum
    m_prev = m_scr[...].reshape(G * SUBLANES, LANES)
    big_m = jnp.maximum(m_prev, _seg_scan(a, jnp.maximum, -jnp.inf, tin, L))
    m_t = bcum + big_m
    if S == 1:
        m_last = jnp.broadcast_to(big_m[:, LANES - 1:LANES], big_m.shape)
    else:
        m_last = _seg_scan(big_m, jnp.maximum, -jnp.inf, tin, L, reverse=True)
    rows = [big_m, jnp.exp(m_prev - big_m), jnp.exp(-m_t), jnp.exp(a - m_last), jnp.exp(m_prev - m_last)]
    mN_ref[...] = m_t.reshape(G, SUBLANES, LANES)
    m_scr[...] = jnp.broadcast_to(m_t[:, LANES - 1:LANES], m_t.shape).reshape(G, SUBLANES, LANES)
    pad_rows = jnp.zeros((LANES - len(rows) * SUBLANES, LANES), f32)
    cols = []
    for g in range(G):
        gs = slice(g * SUBLANES, (g + 1) * SUBLANES)
        cols.append(jnp.concatenate([r[gs] for r in rows] + [pad_rows], axis=0).T)

    def col(g, q, h):
        return cols[g][:, SUBLANES * q + h:SUBLANES * q + h + 1]

    qs, ks, vs, kTs = [], [], [], []
    for h in range(M_HEADS):
        hs = slice(h * M_DH, (h + 1) * M_DH)
        qk_h = jnp.dot(xab[:, hs], wqk_ref[h], preferred_element_type=f32)
        qs.append(qk_h[:, 0:M_DH])
        ks.append(qk_h[:, M_DH:2 * M_DH] * (M_DH ** -0.5))
        vs.append(jnp.dot(xb[:, hs], wv_ref[h], preferred_element_type=f32))
        if S == 1:
            kT_h = lax.dot_general(wkT_ref[h], xab[:, hs], (((1,), (1,)), ((), ())),
                                   preferred_element_type=f32)
            kTs.append((kT_h * (M_DH ** -0.5)).astype(bf16))

    ti = lax.broadcasted_iota(jnp.int32, (R, R), 0)
    si = lax.broadcasted_iota(jnp.int32, (R, R), 1)
    mask = (si <= ti) & ((ti // L) == (si // L))
    ones_b = jnp.ones((R, M_DH), bf16)
    pairs = [(g, h) for g in range(G) for h in range(M_HEADS)]
    blk = lambda g: slice(g * R, (g + 1) * R)
    seqs = [slice(b * L, (b + 1) * L) for b in range(S)]
    q = {(g, h): qs[h][blk(g)] for g, h in pairs}
    k = {(g, h): ks[h][blk(g)] for g, h in pairs}
    v = {(g, h): vs[h][blk(g)] for g, h in pairs}
    qb = {p: q[p].astype(bf16) for p in pairs}
    kb = {p: k[p].astype(bf16) for p in pairs}
    qk = {p: lax.dot_general(qb[p], kb[p], (((1,), (1,)), ((), ())), preferred_element_type=f32)
          for p in pairs}
    sm = {}
    for g, h in pairs:
        a_row = a[g * SUBLANES + h:g * SUBLANES + h + 1, :]
        decay = jnp.exp(jnp.where(mask, a_row - col(g, 0, h), -jnp.inf))
        sm[g, h] = (qk[g, h] * decay).astype(bf16)
    nd = {p: jnp.dot(sm[p], jnp.concatenate([v[p].astype(bf16), ones_b], axis=1),
                     preferred_element_type=f32) for p in pairs}
    c_old = {(g, h, b): cN_ref[g * S + b, h] for g, h in pairs for b in range(S)}
    n_old = {(g, h, b): nN_ref[g * S + b, h] for g, h in pairs for b in range(S)}
    q_c = {(g, h, b): jnp.dot(qb[g, h][seqs[b]], c_old[g, h, b].astype(bf16), preferred_element_type=f32)
           for g, h in pairs for b in range(S)}
    hh = {}
    for g, h in pairs:
        g_col, e_col = col(g, 1, h), col(g, 2, h)
        pieces = []
        for b, rs in enumerate(seqs):
            q_n = jnp.sum(q[g, h][rs] * n_old[g, h, b], axis=1, keepdims=True)
            num = nd[g, h][rs, 0:M_DH] + g_col[rs] * q_c[g, h, b]
            den = nd[g, h][rs, M_DH:2 * M_DH] + g_col[rs] * q_n
            pieces.append(num / jnp.maximum(jnp.abs(den), e_col[rs]))
        hh[g, h] = pieces[0] if S == 1 else jnp.concatenate(pieces, axis=0)
    mu = {p: jnp.mean(hh[p], axis=1, keepdims=True) for p in pairs}
    var = {p: jnp.mean(jnp.square(hh[p] - mu[p]), axis=1, keepdims=True) for p in pairs}
    hn_blocks = [jnp.concatenate([(hh[g, h] - mu[g, h]) * lax.rsqrt(var[g, h] + LN_EPS)
                                  for h in range(M_HEADS)], axis=1) for g in range(G)]
    new_c, new_n = [], []
    for g, h in pairs:
        w_col = col(g, 3, h)
        wv = (w_col * v[g, h]).astype(bf16)
        wk = w_col * k[g, h]
        for b, rs in enumerate(seqs):
            g_end = col(g, 4, h)[(b + 1) * L - 1:(b + 1) * L, :]
            if S == 1:
                kv = jnp.dot(kTs[h][:, blk(g)], wv, preferred_element_type=f32)
            else:
                kv = lax.dot_general(kb[g, h][rs], wv[rs], (((0,), (0,)), ((), ())),
                                     preferred_element_type=f32)
            new_c.append((g * S + b, h, g_end * c_old[g, h, b] + kv))
            new_n.append((g * S + b, h, g_end * n_old[g, h, b] + jnp.sum(wk[rs], axis=0, keepdims=True)))

    hn = jnp.concatenate(hn_blocks, axis=0) * ng_ref[...]
    y = jax.nn.sigmoid(z_ref[...].reshape(GR, D_M)) * (hn + sk_ref[...] * xa)
    y_ref[...] = y.reshape(G, R, D_M)
    for sq, h, val in new_c:
        cN_ref[sq, h] = val
    for sq, h, val in new_n:
        nN_ref[sq, h] = val
    xs_scr[:, 0:SUBLANES, :] = xs_scr[:, L:L + SUBLANES, :]
    bN_ref[...] = xs_scr[:, SUBLANES - (CONV_W - 1):SUBLANES, :]


def _mlstm(proj, B, T, S, L, G, buf8, c0, n0, m0_lanes, cw, cb, wqk, wv, wkT, bif, ng, sk):
    R = S * L
    assert R == M_ROWS == LANES
    nblk, nc = B // S, T // L
    nb = nblk // G
    assert nblk * S == B and nc * L == T and nb * G == nblk and (S == 1 or nc == 1)
    proj4 = proj.reshape(nblk, nc, R, N_PROJ)
    m0_3 = m0_lanes.reshape(SUBLANES, nblk, LANES).transpose(1, 0, 2)
    const2 = lambda b, j: (0, 0)
    const3 = lambda b, j: (0, 0, 0)
    gate_blk = (2 * D_RG + 2 * D_M) // LANES
    GS = G * S
    outs = pl.pallas_call(
        functools.partial(_mlstm_kernel, S=S, L=L, G=G),
        grid=(nb, nc),
        in_specs=[
            pl.BlockSpec((G, None, R, D_M), lambda b, j: (b, j, 0, 2)),
            pl.BlockSpec((G, None, R, D_M), lambda b, j: (b, j, 0, 3)),
            pl.BlockSpec((G, None, R, LANES), lambda b, j: (b, j, 0, gate_blk)),
            pl.BlockSpec((GS, SUBLANES, D_M), lambda b, j: (b, 0, 0)),
            pl.BlockSpec((GS, M_HEADS, M_DH, M_DH), lambda b, j: (b, 0, 0, 0)),
            pl.BlockSpec((GS, M_HEADS, 1, M_DH), lambda b, j: (b, 0, 0, 0)),
            pl.BlockSpec((G, SUBLANES, LANES), lambda b, j: (b, 0, 0)),
            pl.BlockSpec((CONV_W, D_M), const2),
            pl.BlockSpec((1, D_M), const2),
            pl.BlockSpec((M_HEADS, M_DH, 2 * M_DH), const3),
            pl.BlockSpec((M_HEADS, M_DH, M_DH), const3),
            pl.BlockSpec((M_HEADS, M_DH, M_DH), const3),
            pl.BlockSpec((1, LANES), const2),
            pl.BlockSpec((1, D_M), const2),
            pl.BlockSpec((1, D_M), const2),
        ],
        out_specs=[
            pl.BlockSpec((G, None, R, D_M), lambda b, j: (b, j, 0, 0)),
            pl.BlockSpec((GS, M_HEADS, M_DH, M_DH), lambda b, j: (b, 0, 0, 0)),
            pl.BlockSpec((GS, M_HEADS, 1, M_DH), lambda b, j: (b, 0, 0, 0)),
            pl.BlockSpec((G, None, SUBLANES, LANES), lambda b, j: (b, j, 0, 0)),
            pl.BlockSpec((GS, CONV_W - 1, D_M), lambda b, j: (b, 0, 0)),
        ],
        out_shape=[
            jax.ShapeDtypeStruct((nblk, nc, R, D_M), f32),
            jax.ShapeDtypeStruct((B, M_HEADS, M_DH, M_DH), f32),
            jax.ShapeDtypeStruct((B, M_HEADS, 1, M_DH), f32),
            jax.ShapeDtypeStruct((nblk, nc, SUBLANES, LANES), f32),
            jax.ShapeDtypeStruct((B, CONV_W - 1, D_M), f32),
        ],
        scratch_shapes=[pltpu.VMEM((GS, SUBLANES + L, D_M), f32), pltpu.VMEM((G, SUBLANES, LANES), f32)],
        compiler_params=_cparams(2),
        name="mlstm",
    )(proj4, proj4, proj4, buf8, c0, n0, m0_3, cw, cb, wqk, wv, wkT, bif, ng, sk)
    y, c_new, n_new, m_t, b_new = outs
    m_last = m_t[:, nc - 1, :M_HEADS, :].reshape(nblk, M_HEADS, S, L)[:, :, :, L - 1]
    m_last = m_last.transpose(0, 2, 1).reshape(B, M_HEADS)
    return y.reshape(B * T, D_M), c_new, n_new, m_last, b_new


def _layer_norm(x, g, b):
    mu = jnp.mean(x, axis=-1, keepdims=True)
    var = jnp.mean(jnp.square(x - mu), axis=-1, keepdims=True)
    return (x - mu) * lax.rsqrt(var + LN_EPS) * g + b


def _first_lane_of_max(vals, lane_f):
    vmax = jnp.max(vals, axis=1, keepdims=True)
    idx = jnp.min(jnp.where(vals == vmax, lane_f, float(LANES)), axis=1, keepdims=True)
    return vmax, idx


def _outproj_kernel(xp_ref, xs_ref, rgp_ref, rgs_ref, mp_ref, ms_ref, wo_ref, g1_ref, b1_ref,
                    wrh_ref, wrl_ref, br_ref, x1_ref, cnt_ref, *, n_p):
    i = pl.program_id(0)

    @pl.when(i == 0)
    def _():
        cnt_ref[...] = jnp.zeros_like(cnt_ref)

    def run(x_ref, rg_ref, m_ref):
        mix = jnp.dot(rg_ref[...].astype(bf16), wo_ref[0:D_RG, :], preferred_element_type=f32)
        mix = mix + jnp.dot(m_ref[...].astype(bf16), wo_ref[D_RG:D_RG + D_M, :], preferred_element_type=f32)
        x1 = _layer_norm(ALPHA * x_ref[...] + mix, g1_ref[...], b1_ref[...])
        x1_ref[:, 0:D_MODEL] = x1

        hi = x1.astype(bf16)
        lo = (x1 - hi.astype(f32)).astype(bf16)
        lg = (jnp.dot(hi, wrh_ref[...], preferred_element_type=f32)
              + jnp.dot(lo, wrh_ref[...], preferred_element_type=f32)
              + jnp.dot(hi, wrl_ref[...], preferred_element_type=f32)) + br_ref[...]
        lane = lax.broadcasted_iota(jnp.int32, (TM, LANES), 1)
        lane_f = lane.astype(f32)
        neg = -jnp.inf
        gl = jnp.where(lane < N_GROUPS, lg, neg)
        gmax, gidx = _first_lane_of_max(gl, lane_f)
        p_g = 1.0 / jnp.sum(jnp.exp(gl - gmax), axis=1, keepdims=True)
        e_lo = float(N_GROUPS) + float(EXPERTS_PER_GROUP) * gidx
        el = jnp.where((lane_f >= e_lo) & (lane_f < e_lo + float(EXPERTS_PER_GROUP)), lg, neg)
        v1, i1 = _first_lane_of_max(el, lane_f)
        v2, i2 = _first_lane_of_max(jnp.where(lane_f == i1, neg, el), lane_f)
        d = jnp.exp(v2 - v1)
        w1 = p_g / (1.0 + d)
        w2 = p_g * d / (1.0 + d)

        first_low = i1 < i2
        j_lo = jnp.minimum(i1, i2) - e_lo
        j_hi = jnp.maximum(i1, i2) - e_lo
        w_lo = jnp.where(first_low, w1, w2)
        w_hi = jnp.where(first_low, w2, w1)
        n_first = float(EXPERTS_PER_GROUP - 1) * j_lo - 0.5 * j_lo * (j_lo - 1.0)
        cls = float(PAIRS_PER_GROUP) * gidx + n_first + (j_hi - j_lo - 1.0)

        hot = lane_f == cls
        ti = lax.broadcasted_iota(jnp.int32, (TM, TM), 0)
        si = lax.broadcasted_iota(jnp.int32, (TM, TM), 1)
        tri = (si <= ti).astype(bf16)
        cum = jnp.dot(tri, hot.astype(bf16), preferred_element_type=f32)
        rank = jnp.sum(jnp.where(hot, cum - 1.0 + cnt_ref[...], 0.0), axis=1, keepdims=True)
        cnt_ref[...] = cnt_ref[...] + cum[TM - 1:TM, :]

        info = jnp.zeros((TM, LANES), f32)
        for c, val in enumerate((cls, rank, w_lo, w_hi)):
            info = jnp.where(lane == c, val, info)
        x1_ref[:, D_MODEL:D_MODEL + LANES] = info

    @pl.when(i < n_p)
    def _():
        run(xp_ref, rgp_ref, mp_ref)

    @pl.when(i >= n_p)
    def _():
        run(xs_ref, rgs_ref, ms_ref)


def _outproj(xp, xs, rgp, rgs, mp, ms, wo, g1, b1, wrh, wrl, br):
    n_p, n_s = xp.shape[0] // TM, xs.shape[0] // TM
    n = n_p + n_s
    pmap = lambda i: (jnp.minimum(i, n_p - 1), 0)
    smap = lambda i: (jnp.maximum(i - n_p, 0), 0)
    const = lambda i: (0, 0)
    return pl.pallas_call(
        functools.partial(_outproj_kernel, n_p=n_p),
        grid=(n,),
        in_specs=[
            pl.BlockSpec((TM, D_MODEL), pmap), pl.BlockSpec((TM, D_MODEL), smap),
            pl.BlockSpec((TM, D_RG), pmap), pl.BlockSpec((TM, D_RG), smap),
            pl.BlockSpec((TM, D_M), pmap), pl.BlockSpec((TM, D_M), smap),
            pl.BlockSpec((D_RG + D_M, D_MODEL), const),
            pl.BlockSpec((1, D_MODEL), const), pl.BlockSpec((1, D_MODEL), const),
            pl.BlockSpec((D_MODEL, LANES), const), pl.BlockSpec((D_MODEL, LANES), const),
            pl.BlockSpec((1, LANES), const),
        ],
        out_specs=[
            pl.BlockSpec((TM, X_ROW), lambda i: (i, 0)),
            pl.BlockSpec((1, LANES), const),
        ],
        out_shape=[
            jax.ShapeDtypeStruct((n * TM, X_ROW), f32),
            jax.ShapeDtypeStruct((1, LANES), f32),
        ],
        compiler_params=_cparams(1),
        name="outproj",
    )(xp, xs, rgp, rgs, mp, ms, wo, g1, b1, wrh, wrl, br)


def _row_copy(src_ref, src_row, dst_ref, dst_row, sem):
    return pltpu.make_async_copy(src_ref.at[pl.ds(src_row, 1)], dst_ref.at[pl.ds(dst_row, 1)], sem)


def _tile_rows_wait(src_ref, dst_ref, sem):
    pltpu.make_async_copy(src_ref.at[pl.ds(0, TM)], dst_ref.at[pl.ds(0, TM)], sem).wait()


def _dispatch_kernel(lt_ref, pos_ref, x1_ref, xs_ref, zero_scr, xbuf, sem, zsem, lsem):
    i = pl.program_id(0)
    n = pl.num_programs(0)

    def zero_tile(row0):
        return pltpu.make_async_copy(zero_scr, xs_ref.at[pl.ds(pl.multiple_of(row0, TM_MOE), TM_MOE)], zsem)

    @pl.when(i == 0)
    def _():
        zero_scr[...] = jnp.zeros_like(zero_scr)
        for c in range(N_CLASSES):
            @pl.when(lt_ref[c] >= 0)
            def _():
                zero_tile(lt_ref[c]).start()
        n_used = lt_ref[N_CLASSES]
        n_tiles = xs_ref.shape[0] // TM_MOE
        lax.fori_loop(n_used, n_tiles, lambda t, c: (zero_tile(t * TM_MOE).start(), c)[1], 0)
        for c in range(N_CLASSES):
            @pl.when(lt_ref[c] >= 0)
            def _():
                zero_tile(lt_ref[c]).wait()
        lax.fori_loop(n_used, n_tiles, lambda t, c: (zero_tile(t * TM_MOE).wait(), c)[1], 0)

    def load(t, s):
        return pltpu.make_async_copy(x1_ref.at[pl.ds(pl.multiple_of(t * TM, TM), TM)], xbuf.at[s], lsem.at[s])

    @pl.when(i == 0)
    def _():
        load(0, 0).start()

    @pl.when(i + 1 < n)
    def _():
        load(i + 1, (i + 1) % 3).start()

    slot = i % 3
    load(i, slot).wait()

    def start(r, c):
        _row_copy(xbuf.at[slot], r, xs_ref, pos_ref[0, 0, r], sem.at[slot]).start()
        return c

    lax.fori_loop(0, TM, start, 0, unroll=8)

    @pl.when(i > 0)
    def _():
        _tile_rows_wait(xbuf.at[(i + 2) % 3], xs_ref, sem.at[(i + 2) % 3])

    @pl.when(i == n - 1)
    def _():
        _tile_rows_wait(xbuf.at[slot], xs_ref, sem.at[slot])


def _dispatch(last_tile_row, pos, x1, n_rows):
    n = x1.shape[0] // TM
    return pl.pallas_call(
        _dispatch_kernel,
        grid_spec=pltpu.PrefetchScalarGridSpec(
            num_scalar_prefetch=1,
            grid=(n,),
            in_specs=[
                pl.BlockSpec((1, 1, TM), lambda i, lt: (i, 0, 0), memory_space=pltpu.SMEM),
                pl.BlockSpec(memory_space=pl.ANY),
            ],
            out_specs=pl.BlockSpec(memory_space=pl.ANY),
            scratch_shapes=[pltpu.VMEM((TM_MOE, X_ROW), f32), pltpu.VMEM((3, TM, X_ROW), f32),
                            pltpu.SemaphoreType.DMA((3,)), pltpu.SemaphoreType.DMA(()),
                            pltpu.SemaphoreType.DMA((3,))],
        ),
        out_shape=jax.ShapeDtypeStruct((n_rows, X_ROW), f32),
        compiler_params=_cparams(1),
        name="dispatch",
    )(last_tile_row, pos, x1)


def _expert_kernel(ta_ref, tb_ref, nu_ref, x_ref, wga_ref, wua_ref, wda_ref, wgb_ref, wub_ref, wdb_ref,
                   g2_ref, b2_ref, y_ref):
    i = pl.program_id(0)

    @pl.when(i < nu_ref[0])
    def _():
        x = x_ref[:, 0:D_MODEL]
        info = x_ref[:, D_MODEL:X_ROW]
        xb = x.astype(bf16)
        hg_a = jnp.dot(xb, wga_ref[0].astype(bf16), preferred_element_type=f32)
        hg_b = jnp.dot(xb, wgb_ref[0].astype(bf16), preferred_element_type=f32)
        hu_a = jnp.dot(xb, wua_ref[0].astype(bf16), preferred_element_type=f32)
        hu_b = jnp.dot(xb, wub_ref[0].astype(bf16), preferred_element_type=f32)
        mid_a = (hg_a * jax.nn.sigmoid(hg_a) * hu_a).astype(bf16)
        mid_b = (hg_b * jax.nn.sigmoid(hg_b) * hu_b).astype(bf16)
        y_a = jnp.dot(mid_a, wda_ref[0].astype(bf16), preferred_element_type=f32)
        y_b = jnp.dot(mid_b, wdb_ref[0].astype(bf16), preferred_element_type=f32)
        ffn = info[:, 2:3] * y_a + info[:, 3:4] * y_b
        y_ref[...] = _layer_norm(ALPHA * x + ffn, g2_ref[...], b2_ref[...])

    @pl.when(i >= nu_ref[0])
    def _():
        y_ref[...] = jnp.zeros_like(y_ref)


def _experts(tile_ea, tile_eb, n_used, xs, wg, wu, wd, g2, b2):
    nt = xs.shape[0] // TM_MOE
    amap = lambda i, ta, tb, nu: (ta[i], 0, 0)
    bmap = lambda i, ta, tb, nu: (tb[i], 0, 0)
    const = lambda i, ta, tb, nu: (0, 0)
    xmap = lambda i, ta, tb, nu: (jnp.minimum(i, nu[0] - 1), 0)
    w_in = (1, D_MODEL, D_EXPERT)
    w_out = (1, D_EXPERT, D_MODEL)
    return pl.pallas_call(
        _expert_kernel,
        grid_spec=pltpu.PrefetchScalarGridSpec(
            num_scalar_prefetch=3,
            grid=(nt,),
            in_specs=[
                pl.BlockSpec((TM_MOE, X_ROW), xmap),
                pl.BlockSpec(w_in, amap), pl.BlockSpec(w_in, amap), pl.BlockSpec(w_out, amap),
                pl.BlockSpec(w_in, bmap), pl.BlockSpec(w_in, bmap), pl.BlockSpec(w_out, bmap),
                pl.BlockSpec((1, D_MODEL), const), pl.BlockSpec((1, D_MODEL), const),
            ],
            out_specs=pl.BlockSpec((TM_MOE, D_MODEL), lambda i, ta, tb, nu: (i, 0)),
        ),
        out_shape=jax.ShapeDtypeStruct((nt * TM_MOE, D_MODEL), f32),
        compiler_params=_cparams(1),
        name="experts",
    )(tile_ea, tile_eb, n_used, xs, wg, wu, wd, wg, wu, wd, g2, b2)


def _collect_kernel(pos_ref, posn_ref, ys_ref, op_ref, os_ref, ybuf, sem, *, n_p):
    i = pl.program_id(0)
    n = pl.num_programs(0)
    slot = i % 2

    def gather(p_ref, s):
        def start(r, c):
            _row_copy(ys_ref, p_ref[0, 0, r], ybuf.at[s], r, sem.at[s]).start()
            return c

        lax.fori_loop(0, TM, start, 0, unroll=8)

    @pl.when(i == 0)
    def _():
        gather(pos_ref, 0)

    @pl.when(i + 1 < n)
    def _():
        gather(posn_ref, 1 - slot)

    _tile_rows_wait(ys_ref, ybuf.at[slot], sem.at[slot])

    @pl.when(i < n_p)
    def _():
        op_ref[...] = ybuf[slot]

    @pl.when(i >= n_p)
    def _():
        os_ref[...] = ybuf[slot]


def _collect(pos, ys, n_p):
    n = pos.shape[0]
    n_s = n - n_p
    return pl.pallas_call(
        functools.partial(_collect_kernel, n_p=n_p),
        grid=(n,),
        in_specs=[
            pl.BlockSpec((1, 1, TM), lambda i: (i, 0, 0), memory_space=pltpu.SMEM),
            pl.BlockSpec((1, 1, TM), lambda i: (jnp.minimum(i + 1, n - 1), 0, 0), memory_space=pltpu.SMEM),
            pl.BlockSpec(memory_space=pl.ANY),
        ],
        out_specs=[
            pl.BlockSpec((TM, D_MODEL), lambda i: (jnp.minimum(i, n_p - 1), 0)),
            pl.BlockSpec((TM, D_MODEL), lambda i: (jnp.maximum(i - n_p, 0), 0)),
        ],
        out_shape=[
            jax.ShapeDtypeStruct((n_p * TM, D_MODEL), f32),
            jax.ShapeDtypeStruct((n_s * TM, D_MODEL), f32),
        ],
        scratch_shapes=[pltpu.VMEM((2, TM, D_MODEL), f32), pltpu.SemaphoreType.DMA((2,))],
        compiler_params=_cparams(1),
        name="collect",
    )(pos, pos, ys)


def _block_diag(w):
    n, d, _ = w.shape
    eye = jnp.eye(n, dtype=w.dtype)
    return (eye[:, None, :, None] * w[:, :, None, :]).reshape(n * d, n * d)


def _pad_history(buf):
    return jnp.pad(buf, ((0, 0), (SUBLANES - (CONV_W - 1), 0), (0, 0)))


def kernel(x_prompt, x_sample, state_rg_h, state_rg_conv, state_m_C, state_m_n, state_m_m, state_m_conv, w_in, rg_conv_w, rg_conv_b, rg_w_a, rg_b_a, rg_w_x, rg_b_x, rg_lambda, m_conv_w, m_conv_b, m_w_q, m_w_k, m_w_v, m_b_i, m_b_f, m_norm_g, m_skip, w_out, ln1_g, ln1_b, ln2_g, ln2_b, moe_w_group, moe_b_group, moe_w_expert, moe_b_expert, moe_w_gate, moe_w_up, moe_w_down):
    BP, TP, _ = x_prompt.shape
    BS, TS, _ = x_sample.shape
    n_prompt, n_sample = BP * TP, BS * TS
    xp = x_prompt.reshape(n_prompt, D_MODEL)
    xs = x_sample.reshape(n_sample, D_MODEL)
    l = 0

    w_in_p = jnp.pad(w_in[l], ((0, 0), (0, N_PROJ - w_in.shape[-1]))).astype(bf16)
    wa = _block_diag(rg_w_a[l]).astype(bf16)
    wx = _block_diag(rg_w_x[l]).astype(bf16)
    row = lambda v: v.reshape(1, -1)
    bif = jnp.pad(jnp.concatenate([m_b_i[l], m_b_f[l]]), (0, LANES - 2 * M_HEADS)).reshape(1, LANES)
    wqk = jnp.concatenate([m_w_q[l], m_w_k[l]], axis=-1).astype(bf16)
    wv = m_w_v[l].astype(bf16)
    wkT = m_w_k[l].transpose(0, 2, 1).astype(bf16)
    w_route = jnp.pad(jnp.concatenate([moe_w_group[l], moe_w_expert[l]], axis=1),
                      ((0, 0), (0, LANES - N_GROUPS - N_EXPERTS)))
    wr_hi = w_route.astype(bf16)
    wr_lo = (w_route - wr_hi.astype(f32)).astype(bf16)
    b_route = jnp.pad(jnp.concatenate([moe_b_group[l], moe_b_expert[l]]),
                      (0, LANES - N_GROUPS - N_EXPERTS)).reshape(1, LANES)

    proj_p, proj_s = _inproj(xp, xs, w_in_p)

    rg_args = (rg_conv_w[l], row(rg_conv_b[l]), wa, row(rg_b_a[l]), wx, row(rg_b_x[l]), row(rg_lambda[l]))
    zeros = lambda *s: jnp.zeros(s, f32)
    s_blk = M_ROWS // TS
    yrg_p, p_rg_h, p_rg_conv = _rglru(proj_p, 0, BP, TP, 1, RG_ROWS_PROMPT, zeros(BP, SUBLANES, D_RG),
                                      zeros(BP, 1, D_RG), *rg_args)
    yrg_s, s_rg_h, s_rg_conv = _rglru(proj_s, 0, BS, TS, s_blk, TS, _pad_history(state_rg_conv[l]),
                                      state_rg_h[l].reshape(BS, 1, D_RG), *rg_args)

    m_args = (m_conv_w[l], row(m_conv_b[l]), wqk, wv, wkT, bif, row(m_norm_g[l]), row(m_skip[l]))
    m0_p = jnp.full((SUBLANES, BP * LANES), M_INIT, f32)
    ym_p, p_m_C, p_m_n, p_m_m, p_m_conv = _mlstm(
        proj_p, BP, TP, 1, M_ROWS, M_BLOCKS_PROMPT, zeros(BP, SUBLANES, D_M), zeros(BP, M_HEADS, M_DH, M_DH),
        zeros(BP, M_HEADS, 1, M_DH), m0_p, *m_args)
    m0_s = jnp.pad(jnp.repeat(state_m_m[l].T, TS, axis=1), ((0, SUBLANES - M_HEADS), (0, 0)))
    ym_s, s_m_C, s_m_n, s_m_m, s_m_conv = _mlstm(
        proj_s, BS, TS, s_blk, TS, M_BLOCKS_SAMPLE, _pad_history(state_m_conv[l]), state_m_C[l],
        state_m_n[l].reshape(BS, M_HEADS, 1, M_DH), m0_s, *m_args)

    x1, cnt = _outproj(xp, xs, yrg_p, yrg_s, ym_p, ym_s, w_out[l].astype(bf16),
                       row(ln1_g[l]), row(ln1_b[l]), wr_hi, wr_lo, b_route)

    n_tok = n_prompt + n_sample
    n_tiles = n_tok // TM_MOE + N_CLASSES
    counts = cnt[0, :N_CLASSES].astype(jnp.int32)
    padded = ((counts + TM_MOE - 1) // TM_MOE) * TM_MOE
    ends = jnp.cumsum(padded)
    offs = ends - padded
    classes = jnp.arange(N_CLASSES, dtype=jnp.int32)
    cls = x1[:, D_MODEL].astype(jnp.int32)
    rank = x1[:, D_MODEL + 1].astype(jnp.int32)
    pos = jnp.sum(jnp.where(cls[:, None] == classes, offs, 0), axis=-1) + rank
    pos = pos.reshape(n_tok // TM, 1, TM)
    tiles = jnp.arange(n_tiles, dtype=jnp.int32)
    tile_cls = jnp.minimum(jnp.sum(tiles[:, None] >= (ends // TM_MOE)[None, :], axis=1), N_CLASSES - 1)
    pair_a, pair_b = [], []
    for g in range(N_GROUPS):
        for ja in range(EXPERTS_PER_GROUP):
            for jb in range(ja + 1, EXPERTS_PER_GROUP):
                pair_a.append(g * EXPERTS_PER_GROUP + ja)
                pair_b.append(g * EXPERTS_PER_GROUP + jb)
    on_cls = tile_cls[:, None] == classes
    tile_ea = jnp.sum(jnp.where(on_cls, jnp.array(pair_a, jnp.int32), 0), axis=-1).astype(jnp.int32)
    tile_eb = jnp.sum(jnp.where(on_cls, jnp.array(pair_b, jnp.int32), 0), axis=-1).astype(jnp.int32)
    n_used = (ends[-1] // TM_MOE).reshape(1).astype(jnp.int32)
    last_tile_row = jnp.concatenate([jnp.where(counts > 0, ends - TM_MOE, -1).astype(jnp.int32), n_used])

    x_sorted = _dispatch(last_tile_row, pos, x1, n_tiles * TM_MOE)
    y_sorted = _experts(tile_ea, tile_eb, n_used, x_sorted, moe_w_gate[l], moe_w_up[l], moe_w_down[l],
                        row(ln2_g[l]), row(ln2_b[l]))
    y_p, y_s = _collect(pos, y_sorted, n_prompt // TM)

    return (y_p.reshape(BP, TP, D_MODEL), y_s.reshape(BS, TS, D_MODEL),
            p_rg_h.reshape(1, BP, D_RG), p_rg_conv[None], p_m_C[None], p_m_n.reshape(1, BP, M_HEADS, M_DH),
            p_m_m[None], p_m_conv[None],
            s_rg_h.reshape(1, BS, D_RG), s_rg_conv[None], s_m_C[None], s_m_n.reshape(1, BS, M_HEADS, M_DH),
            s_m_m[None], s_m_conv[None])
```

```python
import functools

import jax
import jax.numpy as jnp
from jax import lax
from jax.experimental import pallas as pl
from jax.experimental.pallas import tpu as pltpu

f32 = jnp.float32
bf16 = jnp.bfloat16

D_MODEL = 1024
D_RG = 512
RG_BLOCKS = 8
RG_C = 8.0
D_M = 512
M_HEADS = 4
M_DH = 128
CONV_W = 4
N_GROUPS = 4
EXPERTS_PER_GROUP = 8
N_EXPERTS = 32
D_EXPERT = 256
ALPHA = 2.0 ** 0.25
LN_EPS = 1e-5
M_INIT = -1.0e4

LANES = 128
SUBLANES = 8
TM = 512
TM_MOE = 128
W_SLOTS = 4
W_AHEAD = 3
PAIRS_PER_GROUP = EXPERTS_PER_GROUP * (EXPERTS_PER_GROUP - 1) // 2
N_CLASSES = N_GROUPS * PAIRS_PER_GROUP
X_ROW = D_MODEL + LANES
N_PROJ = 2 * D_RG + 2 * D_M + LANES
RG_ROWS_PROMPT = 512
M_ROWS = 128
M_BLOCKS_PROMPT = 8
M_BLOCKS_SAMPLE = 2
VMEM_LIMIT = 56 * 1024 * 1024


def _cparams(n_axes):
    return pltpu.CompilerParams(dimension_semantics=("arbitrary",) * n_axes,
                                vmem_limit_bytes=VMEM_LIMIT)


def _inproj_kernel(xp_ref, xs_ref, w_ref, op_ref, os_ref, *, n_p):
    i = pl.program_id(0)

    def run(x_ref, o_ref):
        o_ref[...] = jnp.dot(x_ref[...].astype(bf16), w_ref[...], preferred_element_type=f32)

    @pl.when(i < n_p)
    def _():
        run(xp_ref, op_ref)

    @pl.when(i >= n_p)
    def _():
        run(xs_ref, os_ref)


def _inproj(xp, xs, w):
    n_p, n_s = xp.shape[0] // TM, xs.shape[0] // TM
    pmap = lambda i: (jnp.minimum(i, n_p - 1), 0)
    smap = lambda i: (jnp.maximum(i - n_p, 0), 0)
    return pl.pallas_call(
        functools.partial(_inproj_kernel, n_p=n_p),
        grid=(n_p + n_s,),
        in_specs=[
            pl.BlockSpec((TM, D_MODEL), pmap),
            pl.BlockSpec((TM, D_MODEL), smap),
            pl.BlockSpec((D_MODEL, N_PROJ), lambda i: (0, 0)),
        ],
        out_specs=[pl.BlockSpec((TM, N_PROJ), pmap), pl.BlockSpec((TM, N_PROJ), smap)],
        out_shape=[jax.ShapeDtypeStruct((n_p * TM, N_PROJ), f32),
                   jax.ShapeDtypeStruct((n_s * TM, N_PROJ), f32)],
        compiler_params=_cparams(1),
        name="inproj",
    )(xp, xs, w)


def _causal_conv(xs_scr, x, cw_ref, cb_ref, S, L, C):
    xs_scr[:, SUBLANES:SUBLANES + L, :] = x.reshape(S, L, C)
    acc = cb_ref[...] + cw_ref[CONV_W - 1:CONV_W, :] * x
    for j in range(CONV_W - 1):
        lo = SUBLANES - (CONV_W - 1) + j
        acc = acc + cw_ref[j:j + 1, :] * xs_scr[:, lo:lo + L, :].reshape(S * L, C)
    return acc


def _softplus(x):
    return jnp.maximum(x, 0.0) + jnp.log1p(jnp.exp(-jnp.abs(x)))


def _rglru_kernel(x_ref, g_ref, buf_ref, h0_ref, cw_ref, cb_ref, wa_ref, ba_ref, wx_ref, bx_ref,
                  lam_ref, y_ref, hN_ref, cN_ref, xs_scr, *, S, L):
    R = S * L
    t = pl.program_id(1)

    @pl.when(t == 0)
    def _():
        xs_scr[:, 0:SUBLANES, :] = buf_ref[...]
        hN_ref[...] = h0_ref[...]

    x = x_ref[...]
    xc = _causal_conv(xs_scr, x, cw_ref, cb_ref, S, L, D_RG)
    xcb = xc.astype(bf16)
    r = jax.nn.sigmoid(jnp.dot(xcb, wa_ref[...], preferred_element_type=f32) + ba_ref[...])
    ig = jax.nn.sigmoid(jnp.dot(xcb, wx_ref[...], preferred_element_type=f32) + bx_ref[...])
    log_a = (-RG_C) * r * _softplus(-lam_ref[...])
    a = jnp.exp(log_a)
    th = jnp.tanh(log_a)
    u = jnp.sqrt(-2.0 * th / (1.0 - th)) * ig * xc

    n_grp, grp_per_seq = R // SUBLANES, L // SUBLANES
    a3 = a.reshape(n_grp, SUBLANES, D_RG)
    u3 = u.reshape(n_grp, SUBLANES, D_RG)
    sub = lax.broadcasted_iota(jnp.int32, (n_grp, SUBLANES, D_RG), 1)
    s = 1
    while s < SUBLANES:
        ok = sub >= s
        a_sh = pltpu.roll(a3, s, 1)
        u_sh = pltpu.roll(u3, s, 1)
        u3 = jnp.where(ok, a3 * u_sh + u3, u3)
        a3 = jnp.where(ok, a3 * a_sh, a3)
        s *= 2
    h0 = hN_ref[...]
    groups = []
    for kg in range(n_grp):
        carry = h0[kg // grp_per_seq] if kg % grp_per_seq == 0 else groups[-1][SUBLANES - 1:SUBLANES, :]
        groups.append(a3[kg] * carry + u3[kg])
    h = jnp.concatenate(groups, axis=0)

    y_ref[...] = h * jax.nn.gelu(g_ref[...], approximate=True)
    hN_ref[...] = h.reshape(S, L, D_RG)[:, L - 1:L, :]
    xs_scr[:, 0:SUBLANES, :] = xs_scr[:, L:L + SUBLANES, :]
    cN_ref[...] = xs_scr[:, SUBLANES - (CONV_W - 1):SUBLANES, :]


def _rglru(proj, row0, B, T, S, L, buf8, h0, cw, cb, wa, ba, wx, bx, lam):
    R = S * L
    nb, nt = B // S, T // L
    blk0 = row0 // R
    row_map = lambda b, t: (blk0 + b * nt + t, 0)
    const2 = lambda b, t: (0, 0)
    return pl.pallas_call(
        functools.partial(_rglru_kernel, S=S, L=L),
        grid=(nb, nt),
        in_specs=[
            pl.BlockSpec((R, D_RG), row_map),
            pl.BlockSpec((R, D_RG), lambda b, t: (blk0 + b * nt + t, 1)),
            pl.BlockSpec((S, SUBLANES, D_RG), lambda b, t: (b, 0, 0)),
            pl.BlockSpec((S, 1, D_RG), lambda b, t: (b, 0, 0)),
            pl.BlockSpec((CONV_W, D_RG), const2),
            pl.BlockSpec((1, D_RG), const2),
            pl.BlockSpec((D_RG, D_RG), const2),
            pl.BlockSpec((1, D_RG), const2),
            pl.BlockSpec((D_RG, D_RG), const2),
            pl.BlockSpec((1, D_RG), const2),
            pl.BlockSpec((1, D_RG), const2),
        ],
        out_specs=[
            pl.BlockSpec((R, D_RG), lambda b, t: (b * nt + t, 0)),
            pl.BlockSpec((S, 1, D_RG), lambda b, t: (b, 0, 0)),
            pl.BlockSpec((S, CONV_W - 1, D_RG), lambda b, t: (b, 0, 0)),
        ],
        out_shape=[
            jax.ShapeDtypeStruct((B * T, D_RG), f32),
            jax.ShapeDtypeStruct((B, 1, D_RG), f32),
            jax.ShapeDtypeStruct((B, CONV_W - 1, D_RG), f32),
        ],
        scratch_shapes=[pltpu.VMEM((S, SUBLANES + L, D_RG), f32)],
        compiler_params=_cparams(2),
        name="rglru",
    )(proj, proj, buf8, h0, cw, cb, wa, ba, wx, bx, lam)


def _seg_scan(x, op, fill, tin, L, reverse=False):
    s = 1
    while s < L:
        if reverse:
            sh = pltpu.roll(x, LANES - s, 1)
            ok = tin < L - s
        else:
            sh = pltpu.roll(x, s, 1)
            ok = tin >= s
        x = op(x, jnp.where(ok, sh, fill))
        s *= 2
    return x


def _mlstm_kernel(x_ref, z_ref, gt_ref, buf_ref, c0_ref, n0_ref, m0_ref,
                  cw_ref, cb_ref, wqk_ref, wv_ref, wkT_ref, bif_ref, ng_ref, sk_ref,
                  y_ref, cN_ref, nN_ref, mN_ref, bN_ref, xs_scr, m_scr, *, S, L, G):
    R = S * L
    GS, GR = G * S, G * R
    j = pl.program_id(1)

    @pl.when(j == 0)
    def _():
        xs_scr[:, 0:SUBLANES, :] = buf_ref[...]
        cN_ref[...] = c0_ref[...]
        nN_ref[...] = n0_ref[...]
        m_scr[...] = m0_ref[...]

    x = x_ref[...].reshape(GR, D_M)
    xc = _causal_conv(xs_scr, x, cw_ref, cb_ref, GS, L, D_M)
    xa = xc * jax.nn.sigmoid(xc)
    xab = xa.astype(bf16)
    xb = x.astype(bf16)

    il_parts, f_parts = [], []
    for g in range(G):
        gT = (gt_ref[g] + bif_ref[...]).T
        il_parts.append(gT[0:SUBLANES, :])
        f_parts.append(pltpu.roll(gT[0:SUBLANES, :], M_HEADS, 0))
    il = jnp.concatenate(il_parts, axis=0)
    fl = -_softplus(-jnp.concatenate(f_parts, axis=0))
    tin = lax.broadcasted_iota(jnp.int32, (G * SUBLANES, LANES), 1) % L
    bcum = _seg_scan(fl, jnp.add, 0.0, tin, L)
    a = il - bcum
    m_prev = m_scr[...].reshape(G * SUBLANES, LANES)
    big_m = jnp.maximum(m_prev, _seg_scan(a, jnp.maximum, -jnp.inf, tin, L))
    m_t = bcum + big_m
    if S == 1:
        m_last = jnp.broadcast_to(big_m[:, LANES - 1:LANES], big_m.shape)
    else:
        m_last = _seg_scan(big_m, jnp.maximum, -jnp.inf, tin, L, reverse=True)
    rows = [big_m, jnp.exp(m_prev - big_m), jnp.exp(-m_t), jnp.exp(a - m_last), jnp.exp(m_prev - m_last)]
    mN_ref[...] = m_t.reshape(G, SUBLANES, LANES)
    m_scr[...] = jnp.broadcast_to(m_t[:, LANES - 1:LANES], m_t.shape).reshape(G, SUBLANES, LANES)
    pad_rows = jnp.zeros((LANES - len(rows) * SUBLANES, LANES), f32)
    cols = []
    for g in range(G):
        gs = slice(g * SUBLANES, (g + 1) * SUBLANES)
        cols.append(jnp.concatenate([r[gs] for r in rows] + [pad_rows], axis=0).T)

    def col(g, q, h):
        return cols[g][:, SUBLANES * q + h:SUBLANES * q + h + 1]

    qs, ks, vs, kTs = [], [], [], []
    for h in range(M_HEADS):
        hs = slice(h * M_DH, (h + 1) * M_DH)
        qk_h = jnp.dot(xab[:, hs], wqk_ref[h], preferred_element_type=f32)
        qs.append(qk_h[:, 0:M_DH])
        ks.append(qk_h[:, M_DH:2 * M_DH] * (M_DH ** -0.5))
        vs.append(jnp.dot(xb[:, hs], wv_ref[h], preferred_element_type=f32))
        if S == 1:
            kT_h = lax.dot_general(wkT_ref[h], xab[:, hs], (((1,), (1,)), ((), ())),
                                   preferred_element_type=f32)
            kTs.append((kT_h * (M_DH ** -0.5)).astype(bf16))

    ti = lax.broadcasted_iota(jnp.int32, (R, R), 0)
    si = lax.broadcasted_iota(jnp.int32, (R, R), 1)
    mask = (si <= ti) & ((ti // L) == (si // L))
    ones_b = jnp.ones((R, M_DH), bf16)
    pairs = [(g, h) for g in range(G) for h in range(M_HEADS)]
    blk = lambda g: slice(g * R, (g + 1) * R)
    seqs = [slice(b * L, (b + 1) * L) for b in range(S)]
    q = {(g, h): qs[h][blk(g)] for g, h in pairs}
    k = {(g, h): ks[h][blk(g)] for g, h in pairs}
    v = {(g, h): vs[h][blk(g)] for g, h in pairs}
    qb = {p: q[p].astype(bf16) for p in pairs}
    kb = {p: k[p].astype(bf16) for p in pairs}
    qk = {p: lax.dot_general(qb[p], kb[p], (((1,), (1,)), ((), ())), preferred_element_type=f32)
          for p in pairs}
    sm = {}
    for g, h in pairs:
        a_row = a[g * SUBLANES + h:g * SUBLANES + h + 1, :]
        decay = jnp.exp(jnp.where(mask, a_row - col(g, 0, h), -jnp.inf))
        sm[g, h] = (qk[g, h] * decay).astype(bf16)
    nd = {p: jnp.dot(sm[p], jnp.concatenate([v[p].astype(bf16), ones_b], axis=1),
                     preferred_element_type=f32) for p in pairs}
    c_old = {(g, h, b): cN_ref[g * S + b, h] for g, h in pairs for b in range(S)}
    n_old = {(g, h, b): nN_ref[g * S + b, h] for g, h in pairs for b in range(S)}
    q_c = {(g, h, b): jnp.dot(qb[g, h][seqs[b]], c_old[g, h, b].astype(bf16), preferred_element_type=f32)
           for g, h in pairs for b in range(S)}
    hh = {}
    for g, h in pairs:
        g_col, e_col = col(g, 1, h), col(g, 2, h)
        pieces = []
        for b, rs in enumerate(seqs):
            q_n = jnp.sum(q[g, h][rs] * n_old[g, h, b], axis=1, keepdims=True)
            num = nd[g, h][rs, 0:M_DH] + g_col[rs] * q_c[g, h, b]
            den = nd[g, h][rs, M_DH:2 * M_DH] + g_col[rs] * q_n
            pieces.append(num / jnp.maximum(jnp.abs(den), e_col[rs]))
        hh[g, h] = pieces[0] if S == 1 else jnp.concatenate(pieces, axis=0)
    mu = {p: jnp.mean(hh[p], axis=1, keepdims=True) for p in pairs}
    var = {p: jnp.mean(jnp.square(hh[p] - mu[p]), axis=1, keepdims=True) for p in pairs}
    hn_blocks = [jnp.concatenate([(hh[g, h] - mu[g, h]) * lax.rsqrt(var[g, h] + LN_EPS)
                                  for h in range(M_HEADS)], axis=1) for g in range(G)]
    new_c, new_n = [], []
    for g, h in pairs:
        w_col = col(g, 3, h)
        wv = (w_col * v[g, h]).astype(bf16)
        wk = w_col * k[g, h]
        for b, rs in enumerate(seqs):
            g_end = col(g, 4, h)[(b + 1) * L - 1:(b + 1) * L, :]
            if S == 1:
                kv = jnp.dot(kTs[h][:, blk(g)], wv, preferred_element_type=f32)
            else:
                kv = lax.dot_general(kb[g, h][rs], wv[rs], (((0,), (0,)), ((), ())),
                                     preferred_element_type=f32)
            new_c.append((g * S + b, h, g_end * c_old[g, h, b] + kv))
            new_n.append((g * S + b, h, g_end * n_old[g, h, b] + jnp.sum(wk[rs], axis=0, keepdims=True)))

    hn = jnp.concatenate(hn_blocks, axis=0) * ng_ref[...]
    y = jax.nn.sigmoid(z_ref[...].reshape(GR, D_M)) * (hn + sk_ref[...] * xa)
    y_ref[...] = y.reshape(G, R, D_M)
    for sq, h, val in new_c:
        cN_ref[sq, h] = val
    for sq, h, val in new_n:
        nN_ref[sq, h] = val
    xs_scr[:, 0:SUBLANES, :] = xs_scr[:, L:L + SUBLANES, :]
    bN_ref[...] = xs_scr[:, SUBLANES - (CONV_W - 1):SUBLANES, :]


def _mlstm(proj, B, T, S, L, G, buf8, c0, n0, m0_lanes, cw, cb, wqk, wv, wkT, bif, ng, sk):
    R = S * L
    assert R == M_ROWS == LANES
    nblk, nc = B // S, T // L
    nb = nblk // G
    assert nblk * S == B and nc * L == T and nb * G == nblk and (S == 1 or nc == 1)
    proj4 = proj.reshape(nblk, nc, R, N_PROJ)
    m0_3 = m0_lanes.reshape(SUBLANES, nblk, LANES).transpose(1, 0, 2)
    const2 = lambda b, j: (0, 0)
    const3 = lambda b, j: (0, 0, 0)
    gate_blk = (2 * D_RG + 2 * D_M) // LANES
    GS = G * S
    outs = pl.pallas_call(
        functools.partial(_mlstm_kernel, S=S, L=L, G=G),
        grid=(nb, nc),
        in_specs=[
            pl.BlockSpec((G, None, R, D_M), lambda b, j: (b, j, 0, 2)),
            pl.BlockSpec((G, None, R, D_M), lambda b, j: (b, j, 0, 3)),
            pl.BlockSpec((G, None, R, LANES), lambda b, j: (b, j, 0, gate_blk)),
            pl.BlockSpec((GS, SUBLANES, D_M), lambda b, j: (b, 0, 0)),
            pl.BlockSpec((GS, M_HEADS, M_DH, M_DH), lambda b, j: (b, 0, 0, 0)),
            pl.BlockSpec((GS, M_HEADS, 1, M_DH), lambda b, j: (b, 0, 0, 0)),
            pl.BlockSpec((G, SUBLANES, LANES), lambda b, j: (b, 0, 0)),
            pl.BlockSpec((CONV_W, D_M), const2),
            pl.BlockSpec((1, D_M), const2),
            pl.BlockSpec((M_HEADS, M_DH, 2 * M_DH), const3),
            pl.BlockSpec((M_HEADS, M_DH, M_DH), const3),
            pl.BlockSpec((M_HEADS, M_DH, M_DH), const3),
            pl.BlockSpec((1, LANES), const2),
            pl.BlockSpec((1, D_M), const2),
            pl.BlockSpec((1, D_M), const2),
        ],
        out_specs=[
            pl.BlockSpec((G, None, R, D_M), lambda b, j: (b, j, 0, 0)),
            pl.BlockSpec((GS, M_HEADS, M_DH, M_DH), lambda b, j: (b, 0, 0, 0)),
            pl.BlockSpec((GS, M_HEADS, 1, M_DH), lambda b, j: (b, 0, 0, 0)),
            pl.BlockSpec((G, None, SUBLANES, LANES), lambda b, j: (b, j, 0, 0)),
            pl.BlockSpec((GS, CONV_W - 1, D_M), lambda b, j: (b, 0, 0)),
        ],
        out_shape=[
            jax.ShapeDtypeStruct((nblk, nc, R, D_M), f32),
            jax.ShapeDtypeStruct((B, M_HEADS, M_DH, M_DH), f32),
            jax.ShapeDtypeStruct((B, M_HEADS, 1, M_DH), f32),
            jax.ShapeDtypeStruct((nblk, nc, SUBLANES, LANES), f32),
            jax.ShapeDtypeStruct((B, CONV_W - 1, D_M), f32),
        ],
        scratch_shapes=[pltpu.VMEM((GS, SUBLANES + L, D_M), f32), pltpu.VMEM((G, SUBLANES, LANES), f32)],
        compiler_params=_cparams(2),
        name="mlstm",
    )(proj4, proj4, proj4, buf8, c0, n0, m0_3, cw, cb, wqk, wv, wkT, bif, ng, sk)
    y, c_new, n_new, m_t, b_new = outs
    m_last = m_t[:, nc - 1, :M_HEADS, :].reshape(nblk, M_HEADS, S, L)[:, :, :, L - 1]
    m_last = m_last.transpose(0, 2, 1).reshape(B, M_HEADS)
    return y.reshape(B * T, D_M), c_new, n_new, m_last, b_new


def _layer_norm(x, g, b):
    mu = jnp.mean(x, axis=-1, keepdims=True)
    var = jnp.mean(jnp.square(x - mu), axis=-1, keepdims=True)
    return (x - mu) * lax.rsqrt(var + LN_EPS) * g + b


def _first_lane_of_max(vals, lane_f):
    vmax = jnp.max(vals, axis=1, keepdims=True)
    idx = jnp.min(jnp.where(vals == vmax, lane_f, float(LANES)), axis=1, keepdims=True)
    return vmax, idx


def _outproj_kernel(xp_ref, xs_ref, rgp_ref, rgs_ref, mp_ref, ms_ref, wo_ref, g1_ref, b1_ref,
                    wrh_ref, wrl_ref, br_ref, x1_ref, cnt_ref, *, n_p):
    i = pl.program_id(0)

    @pl.when(i == 0)
    def _():
        cnt_ref[...] = jnp.zeros_like(cnt_ref)

    def run(x_ref, rg_ref, m_ref):
        mix = jnp.dot(rg_ref[...].astype(bf16), wo_ref[0:D_RG, :], preferred_element_type=f32)
        mix = mix + jnp.dot(m_ref[...].astype(bf16), wo_ref[D_RG:D_RG + D_M, :], preferred_element_type=f32)
        x1 = _layer_norm(ALPHA * x_ref[...] + mix, g1_ref[...], b1_ref[...])
        x1_ref[:, 0:D_MODEL] = x1

        hi = x1.astype(bf16)
        lo = (x1 - hi.astype(f32)).astype(bf16)
        lg = (jnp.dot(hi, wrh_ref[...], preferred_element_type=f32)
              + jnp.dot(lo, wrh_ref[...], preferred_element_type=f32)
              + jnp.dot(hi, wrl_ref[...], preferred_element_type=f32)) + br_ref[...]
        lane = lax.broadcasted_iota(jnp.int32, (TM, LANES), 1)
        lane_f = lane.astype(f32)
        neg = -jnp.inf
        gl = jnp.where(lane < N_GROUPS, lg, neg)
        gmax, gidx = _first_lane_of_max(gl, lane_f)
        p_g = 1.0 / jnp.sum(jnp.exp(gl - gmax), axis=1, keepdims=True)
        e_lo = float(N_GROUPS) + float(EXPERTS_PER_GROUP) * gidx
        el = jnp.where((lane_f >= e_lo) & (lane_f < e_lo + float(EXPERTS_PER_GROUP)), lg, neg)
        v1, i1 = _first_lane_of_max(el, lane_f)
        v2, i2 = _first_lane_of_max(jnp.where(lane_f == i1, neg, el), lane_f)
        d = jnp.exp(v2 - v1)
        w1 = p_g / (1.0 + d)
        w2 = p_g * d / (1.0 + d)

        first_low = i1 < i2
        j_lo = jnp.minimum(i1, i2) - e_lo
        j_hi = jnp.maximum(i1, i2) - e_lo
        w_lo = jnp.where(first_low, w1, w2)
        w_hi = jnp.where(first_low, w2, w1)
        n_first = float(EXPERTS_PER_GROUP - 1) * j_lo - 0.5 * j_lo * (j_lo - 1.0)
        cls = float(PAIRS_PER_GROUP) * gidx + n_first + (j_hi - j_lo - 1.0)

        hot = lane_f == cls
        ti = lax.broadcasted_iota(jnp.int32, (TM, TM), 0)
        si = lax.broadcasted_iota(jnp.int32, (TM, TM), 1)
        tri = (si <= ti).astype(bf16)
        cum = jnp.dot(tri, hot.astype(bf16), preferred_element_type=f32)
        rank = jnp.sum(jnp.where(hot, cum - 1.0 + cnt_ref[...], 0.0), axis=1, keepdims=True)
        cnt_ref[...] = cnt_ref[...] + cum[TM - 1:TM, :]

        info = jnp.zeros((TM, LANES), f32)
        for c, val in enumerate((cls, rank, w_lo, w_hi)):
            info = jnp.where(lane == c, val, info)
        x1_ref[:, D_MODEL:D_MODEL + LANES] = info

    @pl.when(i < n_p)
    def _():
        run(xp_ref, rgp_ref, mp_ref)

    @pl.when(i >= n_p)
    def _():
        run(xs_ref, rgs_ref, ms_ref)


def _outproj(xp, xs, rgp, rgs, mp, ms, wo, g1, b1, wrh, wrl, br):
    n_p, n_s = xp.shape[0] // TM, xs.shape[0] // TM
    n = n_p + n_s
    pmap = lambda i: (jnp.minimum(i, n_p - 1), 0)
    smap = lambda i: (jnp.maximum(i - n_p, 0), 0)
    const = lambda i: (0, 0)
    return pl.pallas_call(
        functools.partial(_outproj_kernel, n_p=n_p),
        grid=(n,),
        in_specs=[
            pl.BlockSpec((TM, D_MODEL), pmap), pl.BlockSpec((TM, D_MODEL), smap),
            pl.BlockSpec((TM, D_RG), pmap), pl.BlockSpec((TM, D_RG), smap),
            pl.BlockSpec((TM, D_M), pmap), pl.BlockSpec((TM, D_M), smap),
            pl.BlockSpec((D_RG + D_M, D_MODEL), const),
            pl.BlockSpec((1, D_MODEL), const), pl.BlockSpec((1, D_MODEL), const),
            pl.BlockSpec((D_MODEL, LANES), const), pl.BlockSpec((D_MODEL, LANES), const),
            pl.BlockSpec((1, LANES), const),
        ],
        out_specs=[
            pl.BlockSpec((TM, X_ROW), lambda i: (i, 0)),
            pl.BlockSpec((1, LANES), const),
        ],
        out_shape=[
            jax.ShapeDtypeStruct((n * TM, X_ROW), f32),
            jax.ShapeDtypeStruct((1, LANES), f32),
        ],
        compiler_params=_cparams(1),
        name="outproj",
    )(xp, xs, rgp, rgs, mp, ms, wo, g1, b1, wrh, wrl, br)


def _row_copy(src_ref, src_row, dst_ref, dst_row, sem):
    return pltpu.make_async_copy(src_ref.at[pl.ds(src_row, 1)], dst_ref.at[pl.ds(dst_row, 1)], sem)


def _tile_rows_wait(src_ref, dst_ref, sem):
    pltpu.make_async_copy(src_ref.at[pl.ds(0, TM)], dst_ref.at[pl.ds(0, TM)], sem).wait()


def _dispatch_kernel(lt_ref, pos_ref, x1_ref, xs_ref, zero_scr, xbuf, sem, zsem, lsem):
    i = pl.program_id(0)
    n = pl.num_programs(0)

    def zero_tile(row0):
        return pltpu.make_async_copy(zero_scr, xs_ref.at[pl.ds(pl.multiple_of(row0, TM_MOE), TM_MOE)], zsem)

    @pl.when(i == 0)
    def _():
        zero_scr[...] = jnp.zeros_like(zero_scr)
        for c in range(N_CLASSES):
            @pl.when(lt_ref[c] >= 0)
            def _():
                zero_tile(lt_ref[c]).start()
        n_used = lt_ref[N_CLASSES]
        n_tiles = xs_ref.shape[0] // TM_MOE
        lax.fori_loop(n_used, n_tiles, lambda t, c: (zero_tile(t * TM_MOE).start(), c)[1], 0)
        for c in range(N_CLASSES):
            @pl.when(lt_ref[c] >= 0)
            def _():
                zero_tile(lt_ref[c]).wait()
        lax.fori_loop(n_used, n_tiles, lambda t, c: (zero_tile(t * TM_MOE).wait(), c)[1], 0)

    def load(t, s):
        return pltpu.make_async_copy(x1_ref.at[pl.ds(pl.multiple_of(t * TM, TM), TM)], xbuf.at[s], lsem.at[s])

    @pl.when(i == 0)
    def _():
        load(0, 0).start()

    @pl.when(i + 1 < n)
    def _():
        load(i + 1, (i + 1) % 3).start()

    slot = i % 3
    load(i, slot).wait()

    def start(r, c):
        _row_copy(xbuf.at[slot], r, xs_ref, pos_ref[0, 0, r], sem.at[slot]).start()
        return c

    lax.fori_loop(0, TM, start, 0, unroll=8)

    @pl.when(i > 0)
    def _():
        _tile_rows_wait(xbuf.at[(i + 2) % 3], xs_ref, sem.at[(i + 2) % 3])

    @pl.when(i == n - 1)
    def _():
        _tile_rows_wait(xbuf.at[slot], xs_ref, sem.at[slot])


def _dispatch(last_tile_row, pos, x1, n_rows):
    n = x1.shape[0] // TM
    return pl.pallas_call(
        _dispatch_kernel,
        grid_spec=pltpu.PrefetchScalarGridSpec(
            num_scalar_prefetch=1,
            grid=(n,),
            in_specs=[
                pl.BlockSpec((1, 1, TM), lambda i, lt: (i, 0, 0), memory_space=pltpu.SMEM),
                pl.BlockSpec(memory_space=pl.ANY),
            ],
            out_specs=pl.BlockSpec(memory_space=pl.ANY),
            scratch_shapes=[pltpu.VMEM((TM_MOE, X_ROW), f32), pltpu.VMEM((3, TM, X_ROW), f32),
                            pltpu.SemaphoreType.DMA((3,)), pltpu.SemaphoreType.DMA(()),
                            pltpu.SemaphoreType.DMA((3,))],
        ),
        out_shape=jax.ShapeDtypeStruct((n_rows, X_ROW), f32),
        compiler_params=_cparams(1),
        name="dispatch",
    )(last_tile_row, pos, x1)


def _expert_kernel(ta_ref, tb_ref, ca_ref, cb_ref, sa_ref, sb_ref, nu_ref,
                   x_ref, wg_hbm, wu_hbm, wd_hbm, g2_ref, b2_ref, y_ref,
                   wga, wua, wda, wgb, wub, wdb, sem_a, sem_b):
    i = pl.program_id(0)
    n_used = nu_ref[0]
    side_a = (ta_ref, ca_ref, sa_ref, (wga, wua, wda), sem_a)
    side_b = (tb_ref, cb_ref, sb_ref, (wgb, wub, wdb), sem_b)

    def copies(side, j):
        t_ref, _, s_ref, bufs, sem = side
        e, s = t_ref[j], s_ref[j]
        return [pltpu.make_async_copy(w.at[e], buf.at[s], sem.at[s])
                for w, buf in zip((wg_hbm, wu_hbm, wd_hbm), bufs)]

    def fetch(j):
        @pl.when(j < n_used)
        def _():
            for side in (side_a, side_b):
                @pl.when(side[1][j] == 1)
                def _():
                    for cp in copies(side, j):
                        cp.start()

    @pl.when(i == 0)
    def _():
        for j in range(W_AHEAD):
            fetch(j)

    fetch(i + W_AHEAD)

    @pl.when(i < n_used)
    def _():
        for side in (side_a, side_b):
            @pl.when(side[1][i] == 1)
            def _():
                for cp in copies(side, i):
                    cp.wait()
        sa, sb = sa_ref[i], sb_ref[i]
        x = x_ref[:, 0:D_MODEL]
        info = x_ref[:, D_MODEL:X_ROW]
        xb = x.astype(bf16)
        hg_a = jnp.dot(xb, wga[sa].astype(bf16), preferred_element_type=f32)
        hg_b = jnp.dot(xb, wgb[sb].astype(bf16), preferred_element_type=f32)
        hu_a = jnp.dot(xb, wua[sa].astype(bf16), preferred_element_type=f32)
        hu_b = jnp.dot(xb, wub[sb].astype(bf16), preferred_element_type=f32)
        mid_a = (hg_a * jax.nn.sigmoid(hg_a) * hu_a).astype(bf16)
        mid_b = (hg_b * jax.nn.sigmoid(hg_b) * hu_b).astype(bf16)
        y_a = jnp.dot(mid_a, wda[sa].astype(bf16), preferred_element_type=f32)
        y_b = jnp.dot(mid_b, wdb[sb].astype(bf16), preferred_element_type=f32)
        ffn = info[:, 2:3] * y_a + info[:, 3:4] * y_b
        y_ref[...] = _layer_norm(ALPHA * x + ffn, g2_ref[...], b2_ref[...])

    @pl.when(i >= n_used)
    def _():
        y_ref[...] = jnp.zeros_like(y_ref)


def _experts(tile_ea, tile_eb, n_used, xs, wg, wu, wd, g2, b2):
    nt = xs.shape[0] // TM_MOE

    def ring_plan(tile_e):
        opens = jnp.concatenate([jnp.ones((1,), jnp.int32), (tile_e[1:] != tile_e[:-1]).astype(jnp.int32)])
        return opens, (jnp.cumsum(opens) - 1) % W_SLOTS

    open_a, slot_a = ring_plan(tile_ea)
    open_b, slot_b = ring_plan(tile_eb)
    const = lambda i, *_: (0, 0)
    xmap = lambda i, ta, tb, ca, cb, sa, sb, nu: (jnp.minimum(i, nu[0] - 1), 0)
    w_in = pltpu.VMEM((W_SLOTS, D_MODEL, D_EXPERT), f32)
    w_out = pltpu.VMEM((W_SLOTS, D_EXPERT, D_MODEL), f32)
    return pl.pallas_call(
        _expert_kernel,
        grid_spec=pltpu.PrefetchScalarGridSpec(
            num_scalar_prefetch=7,
            grid=(nt,),
            in_specs=[
                pl.BlockSpec((TM_MOE, X_ROW), xmap),
                pl.BlockSpec(memory_space=pl.ANY), pl.BlockSpec(memory_space=pl.ANY),
                pl.BlockSpec(memory_space=pl.ANY),
                pl.BlockSpec((1, D_MODEL), const), pl.BlockSpec((1, D_MODEL), const),
            ],
            out_specs=pl.BlockSpec((TM_MOE, D_MODEL), lambda i, *_: (i, 0)),
            scratch_shapes=[w_in, w_in, w_out, w_in, w_in, w_out,
                            pltpu.SemaphoreType.DMA((W_SLOTS,)), pltpu.SemaphoreType.DMA((W_SLOTS,))],
        ),
        out_shape=jax.ShapeDtypeStruct((nt * TM_MOE, D_MODEL), f32),
        compiler_params=_cparams(1),
        name="experts",
    )(tile_ea, tile_eb, open_a, open_b, slot_a.astype(jnp.int32), slot_b.astype(jnp.int32), n_used,
      xs, wg, wu, wd, g2, b2)


def _collect_kernel(pos_ref, posn_ref, ys_ref, op_ref, os_ref, ybuf, sem, *, n_p):
    i = pl.program_id(0)
    n = pl.num_programs(0)
    slot = i % 2

    def gather(p_ref, s):
        def start(r, c):
            _row_copy(ys_ref, p_ref[0, 0, r], ybuf.at[s], r, sem.at[s]).start()
            return c

        lax.fori_loop(0, TM, start, 0, unroll=8)

    @pl.when(i == 0)
    def _():
        gather(pos_ref, 0)

    @pl.when(i + 1 < n)
    def _():
        gather(posn_ref, 1 - slot)

    _tile_rows_wait(ys_ref, ybuf.at[slot], sem.at[slot])

    @pl.when(i < n_p)
    def _():
        op_ref[...] = ybuf[slot]

    @pl.when(i >= n_p)
    def _():
        os_ref[...] = ybuf[slot]


def _collect(pos, ys, n_p):
    n = pos.shape[0]
    n_s = n - n_p
    return pl.pallas_call(
        functools.partial(_collect_kernel, n_p=n_p),
        grid=(n,),
        in_specs=[
            pl.BlockSpec((1, 1, TM), lambda i: (i, 0, 0), memory_space=pltpu.SMEM),
            pl.BlockSpec((1, 1, TM), lambda i: (jnp.minimum(i + 1, n - 1), 0, 0), memory_space=pltpu.SMEM),
            pl.BlockSpec(memory_space=pl.ANY),
        ],
        out_specs=[
            pl.BlockSpec((TM, D_MODEL), lambda i: (jnp.minimum(i, n_p - 1), 0)),
            pl.BlockSpec((TM, D_MODEL), lambda i: (jnp.maximum(i - n_p, 0), 0)),
        ],
        out_shape=[
            jax.ShapeDtypeStruct((n_p * TM, D_MODEL), f32),
            jax.ShapeDtypeStruct((n_s * TM, D_MODEL), f32),
        ],
        scratch_shapes=[pltpu.VMEM((2, TM, D_MODEL), f32), pltpu.SemaphoreType.DMA((2,))],
        compiler_params=_cparams(1),
        name="collect",
    )(pos, pos, ys)


def _block_diag(w):
    n, d, _ = w.shape
    eye = jnp.eye(n, dtype=w.dtype)
    return (eye[:, None, :, None] * w[:, :, None, :]).reshape(n * d, n * d)


def _pad_history(buf):
    return jnp.pad(buf, ((0, 0), (SUBLANES - (CONV_W - 1), 0), (0, 0)))


def kernel(x_prompt, x_sample, state_rg_h, state_rg_conv, state_m_C, state_m_n, state_m_m, state_m_conv, w_in, rg_conv_w, rg_conv_b, rg_w_a, rg_b_a, rg_w_x, rg_b_x, rg_lambda, m_conv_w, m_conv_b, m_w_q, m_w_k, m_w_v, m_b_i, m_b_f, m_norm_g, m_skip, w_out, ln1_g, ln1_b, ln2_g, ln2_b, moe_w_group, moe_b_group, moe_w_expert, moe_b_expert, moe_w_gate, moe_w_up, moe_w_down):
    BP, TP, _ = x_prompt.shape
    BS, TS, _ = x_sample.shape
    n_prompt, n_sample = BP * TP, BS * TS
    xp = x_prompt.reshape(n_prompt, D_MODEL)
    xs = x_sample.reshape(n_sample, D_MODEL)
    l = 0

    w_in_p = jnp.pad(w_in[l], ((0, 0), (0, N_PROJ - w_in.shape[-1]))).astype(bf16)
    wa = _block_diag(rg_w_a[l]).astype(bf16)
    wx = _block_diag(rg_w_x[l]).astype(bf16)
    row = lambda v: v.reshape(1, -1)
    bif = jnp.pad(jnp.concatenate([m_b_i[l], m_b_f[l]]), (0, LANES - 2 * M_HEADS)).reshape(1, LANES)
    wqk = jnp.concatenate([m_w_q[l], m_w_k[l]], axis=-1).astype(bf16)
    wv = m_w_v[l].astype(bf16)
    wkT = m_w_k[l].transpose(0, 2, 1).astype(bf16)
    w_route = jnp.pad(jnp.concatenate([moe_w_group[l], moe_w_expert[l]], axis=1),
                      ((0, 0), (0, LANES - N_GROUPS - N_EXPERTS)))
    wr_hi = w_route.astype(bf16)
    wr_lo = (w_route - wr_hi.astype(f32)).astype(bf16)
    b_route = jnp.pad(jnp.concatenate([moe_b_group[l], moe_b_expert[l]]),
                      (0, LANES - N_GROUPS - N_EXPERTS)).reshape(1, LANES)

    proj_p, proj_s = _inproj(xp, xs, w_in_p)

    rg_args = (rg_conv_w[l], row(rg_conv_b[l]), wa, row(rg_b_a[l]), wx, row(rg_b_x[l]), row(rg_lambda[l]))
    zeros = lambda *s: jnp.zeros(s, f32)
    s_blk = M_ROWS // TS
    yrg_p, p_rg_h, p_rg_conv = _rglru(proj_p, 0, BP, TP, 1, RG_ROWS_PROMPT, zeros(BP, SUBLANES, D_RG),
                                      zeros(BP, 1, D_RG), *rg_args)
    yrg_s, s_rg_h, s_rg_conv = _rglru(proj_s, 0, BS, TS, s_blk, TS, _pad_history(state_rg_conv[l]),
                                      state_rg_h[l].reshape(BS, 1, D_RG), *rg_args)

    m_args = (m_conv_w[l], row(m_conv_b[l]), wqk, wv, wkT, bif, row(m_norm_g[l]), row(m_skip[l]))
    m0_p = jnp.full((SUBLANES, BP * LANES), M_INIT, f32)
    ym_p, p_m_C, p_m_n, p_m_m, p_m_conv = _mlstm(
        proj_p, BP, TP, 1, M_ROWS, M_BLOCKS_PROMPT, zeros(BP, SUBLANES, D_M), zeros(BP, M_HEADS, M_DH, M_DH),
        zeros(BP, M_HEADS, 1, M_DH), m0_p, *m_args)
    m0_s = jnp.pad(jnp.repeat(state_m_m[l].T, TS, axis=1), ((0, SUBLANES - M_HEADS), (0, 0)))
    ym_s, s_m_C, s_m_n, s_m_m, s_m_conv = _mlstm(
        proj_s, BS, TS, s_blk, TS, M_BLOCKS_SAMPLE, _pad_history(state_m_conv[l]), state_m_C[l],
        state_m_n[l].reshape(BS, M_HEADS, 1, M_DH), m0_s, *m_args)

    x1, cnt = _outproj(xp, xs, yrg_p, yrg_s, ym_p, ym_s, w_out[l].astype(bf16),
                       row(ln1_g[l]), row(ln1_b[l]), wr_hi, wr_lo, b_route)

    n_tok = n_prompt + n_sample
    n_tiles = n_tok // TM_MOE + N_CLASSES
    counts = cnt[0, :N_CLASSES].astype(jnp.int32)
    padded = ((counts + TM_MOE - 1) // TM_MOE) * TM_MOE
    ends = jnp.cumsum(padded)
    offs = ends - padded
    classes = jnp.arange(N_CLASSES, dtype=jnp.int32)
    cls = x1[:, D_MODEL].astype(jnp.int32)
    rank = x1[:, D_MODEL + 1].astype(jnp.int32)
    pos = jnp.sum(jnp.where(cls[:, None] == classes, offs, 0), axis=-1) + rank
    pos = pos.reshape(n_tok // TM, 1, TM)
    tiles = jnp.arange(n_tiles, dtype=jnp.int32)
    tile_cls = jnp.minimum(jnp.sum(tiles[:, None] >= (ends // TM_MOE)[None, :], axis=1), N_CLASSES - 1)
    pair_a, pair_b = [], []
    for g in range(N_GROUPS):
        for ja in range(EXPERTS_PER_GROUP):
            for jb in range(ja + 1, EXPERTS_PER_GROUP):
                pair_a.append(g * EXPERTS_PER_GROUP + ja)
                pair_b.append(g * EXPERTS_PER_GROUP + jb)
    on_cls = tile_cls[:, None] == classes
    tile_ea = jnp.sum(jnp.where(on_cls, jnp.array(pair_a, jnp.int32), 0), axis=-1).astype(jnp.int32)
    tile_eb = jnp.sum(jnp.where(on_cls, jnp.array(pair_b, jnp.int32), 0), axis=-1).astype(jnp.int32)
    n_used = (ends[-1] // TM_MOE).reshape(1).astype(jnp.int32)
    last_tile_row = jnp.concatenate([jnp.where(counts > 0, ends - TM_MOE, -1).astype(jnp.int32), n_used])

    x_sorted = _dispatch(last_tile_row, pos, x1, n_tiles * TM_MOE)
    y_sorted = _experts(tile_ea, tile_eb, n_used, x_sorted, moe_w_gate[l], moe_w_up[l], moe_w_down[l],
                        row(ln2_g[l]), row(ln2_b[l]))
    y_p, y_s = _collect(pos, y_sorted, n_prompt // TM)

    return (y_p.reshape(BP, TP, D_MODEL), y_s.reshape(BS, TS, D_MODEL),
            p_rg_h.reshape(1, BP, D_RG), p_rg_conv[None], p_m_C[None], p_m_n.reshape(1, BP, M_HEADS, M_DH),
            p_m_m[None], p_m_conv[None],
            s_rg_h.reshape(1, BS, D_RG), s_rg_conv[None], s_m_C[None], s_m_n.reshape(1, BS, M_HEADS, M_DH),
            s_m_m[None], s_m_conv[None])
```

```python
import functools

import jax
import jax.numpy as jnp
from jax import lax
from jax.experimental import pallas as pl
from jax.experimental.pallas import tpu as pltpu

f32 = jnp.float32
bf16 = jnp.bfloat16

D_MODEL = 1024
D_RG = 512
RG_BLOCKS = 8
RG_C = 8.0
D_M = 512
M_HEADS = 4
M_DH = 128
CONV_W = 4
N_GROUPS = 4
EXPERTS_PER_GROUP = 8
N_EXPERTS = 32
D_EXPERT = 256
ALPHA = 2.0 ** 0.25
LN_EPS = 1e-5
M_INIT = -1.0e4

LANES = 128
SUBLANES = 8
TM = 512
TM_MOE = 128
MOE_TILES = 2
W_AHEAD = 4
W_SLOTS = W_AHEAD + MOE_TILES
PAIRS_PER_GROUP = EXPERTS_PER_GROUP * (EXPERTS_PER_GROUP - 1) // 2
N_CLASSES = N_GROUPS * PAIRS_PER_GROUP
X_ROW = D_MODEL + LANES
N_PROJ = 2 * D_RG + 2 * D_M + LANES
RG_ROWS_PROMPT = 512
M_ROWS = 128
M_BLOCKS_PROMPT = 8
M_BLOCKS_SAMPLE = 2
VMEM_LIMIT = 56 * 1024 * 1024


def _cparams(n_axes):
    return pltpu.CompilerParams(dimension_semantics=("arbitrary",) * n_axes,
                                vmem_limit_bytes=VMEM_LIMIT)


def _inproj_kernel(xp_ref, xs_ref, w_ref, op_ref, os_ref, *, n_p):
    i = pl.program_id(0)

    def run(x_ref, o_ref):
        o_ref[...] = jnp.dot(x_ref[...].astype(bf16), w_ref[...], preferred_element_type=f32)

    @pl.when(i < n_p)
    def _():
        run(xp_ref, op_ref)

    @pl.when(i >= n_p)
    def _():
        run(xs_ref, os_ref)


def _inproj(xp, xs, w):
    n_p, n_s = xp.shape[0] // TM, xs.shape[0] // TM
    pmap = lambda i: (jnp.minimum(i, n_p - 1), 0)
    smap = lambda i: (jnp.maximum(i - n_p, 0), 0)
    return pl.pallas_call(
        functools.partial(_inproj_kernel, n_p=n_p),
        grid=(n_p + n_s,),
        in_specs=[
            pl.BlockSpec((TM, D_MODEL), pmap),
            pl.BlockSpec((TM, D_MODEL), smap),
            pl.BlockSpec((D_MODEL, N_PROJ), lambda i: (0, 0)),
        ],
        out_specs=[pl.BlockSpec((TM, N_PROJ), pmap), pl.BlockSpec((TM, N_PROJ), smap)],
        out_shape=[jax.ShapeDtypeStruct((n_p * TM, N_PROJ), f32),
                   jax.ShapeDtypeStruct((n_s * TM, N_PROJ), f32)],
        compiler_params=_cparams(1),
        name="inproj",
    )(xp, xs, w)


def _causal_conv(xs_scr, x, cw_ref, cb_ref, S, L, C):
    xs_scr[:, SUBLANES:SUBLANES + L, :] = x.reshape(S, L, C)
    acc = cb_ref[...] + cw_ref[CONV_W - 1:CONV_W, :] * x
    for j in range(CONV_W - 1):
        lo = SUBLANES - (CONV_W - 1) + j
        acc = acc + cw_ref[j:j + 1, :] * xs_scr[:, lo:lo + L, :].reshape(S * L, C)
    return acc


def _softplus(x):
    return jnp.maximum(x, 0.0) + jnp.log1p(jnp.exp(-jnp.abs(x)))


def _rglru_kernel(x_ref, g_ref, buf_ref, h0_ref, cw_ref, cb_ref, wa_ref, ba_ref, wx_ref, bx_ref,
                  lam_ref, y_ref, hN_ref, cN_ref, xs_scr, *, S, L):
    R = S * L
    t = pl.program_id(1)

    @pl.when(t == 0)
    def _():
        xs_scr[:, 0:SUBLANES, :] = buf_ref[...]
        hN_ref[...] = h0_ref[...]

    x = x_ref[...]
    xc = _causal_conv(xs_scr, x, cw_ref, cb_ref, S, L, D_RG)
    xcb = xc.astype(bf16)
    r = jax.nn.sigmoid(jnp.dot(xcb, wa_ref[...], preferred_element_type=f32) + ba_ref[...])
    ig = jax.nn.sigmoid(jnp.dot(xcb, wx_ref[...], preferred_element_type=f32) + bx_ref[...])
    log_a = (-RG_C) * r * _softplus(-lam_ref[...])
    a = jnp.exp(log_a)
    th = jnp.tanh(log_a)
    u = jnp.sqrt(-2.0 * th / (1.0 - th)) * ig * xc

    n_grp, grp_per_seq = R // SUBLANES, L // SUBLANES
    a3 = a.reshape(n_grp, SUBLANES, D_RG)
    u3 = u.reshape(n_grp, SUBLANES, D_RG)
    sub = lax.broadcasted_iota(jnp.int32, (n_grp, SUBLANES, D_RG), 1)
    s = 1
    while s < SUBLANES:
        ok = sub >= s
        a_sh = pltpu.roll(a3, s, 1)
        u_sh = pltpu.roll(u3, s, 1)
        u3 = jnp.where(ok, a3 * u_sh + u3, u3)
        a3 = jnp.where(ok, a3 * a_sh, a3)
        s *= 2
    h0 = hN_ref[...]
    groups = []
    for kg in range(n_grp):
        carry = h0[kg // grp_per_seq] if kg % grp_per_seq == 0 else groups[-1][SUBLANES - 1:SUBLANES, :]
        groups.append(a3[kg] * carry + u3[kg])
    h = jnp.concatenate(groups, axis=0)

    y_ref[...] = h * jax.nn.gelu(g_ref[...], approximate=True)
    hN_ref[...] = h.reshape(S, L, D_RG)[:, L - 1:L, :]
    xs_scr[:, 0:SUBLANES, :] = xs_scr[:, L:L + SUBLANES, :]
    cN_ref[...] = xs_scr[:, SUBLANES - (CONV_W - 1):SUBLANES, :]


def _rglru(proj, row0, B, T, S, L, buf8, h0, cw, cb, wa, ba, wx, bx, lam):
    R = S * L
    nb, nt = B // S, T // L
    blk0 = row0 // R
    row_map = lambda b, t: (blk0 + b * nt + t, 0)
    const2 = lambda b, t: (0, 0)
    return pl.pallas_call(
        functools.partial(_rglru_kernel, S=S, L=L),
        grid=(nb, nt),
        in_specs=[
            pl.BlockSpec((R, D_RG), row_map),
            pl.BlockSpec((R, D_RG), lambda b, t: (blk0 + b * nt + t, 1)),
            pl.BlockSpec((S, SUBLANES, D_RG), lambda b, t: (b, 0, 0)),
            pl.BlockSpec((S, 1, D_RG), lambda b, t: (b, 0, 0)),
            pl.BlockSpec((CONV_W, D_RG), const2),
            pl.BlockSpec((1, D_RG), const2),
            pl.BlockSpec((D_RG, D_RG), const2),
            pl.BlockSpec((1, D_RG), const2),
            pl.BlockSpec((D_RG, D_RG), const2),
            pl.BlockSpec((1, D_RG), const2),
            pl.BlockSpec((1, D_RG), const2),
        ],
        out_specs=[
            pl.BlockSpec((R, D_RG), lambda b, t: (b * nt + t, 0)),
            pl.BlockSpec((S, 1, D_RG), lambda b, t: (b, 0, 0)),
            pl.BlockSpec((S, CONV_W - 1, D_RG), lambda b, t: (b, 0, 0)),
        ],
        out_shape=[
            jax.ShapeDtypeStruct((B * T, D_RG), f32),
            jax.ShapeDtypeStruct((B, 1, D_RG), f32),
            jax.ShapeDtypeStruct((B, CONV_W - 1, D_RG), f32),
        ],
        scratch_shapes=[pltpu.VMEM((S, SUBLANES + L, D_RG), f32)],
        compiler_params=_cparams(2),
        name="rglru",
    )(proj, proj, buf8, h0, cw, cb, wa, ba, wx, bx, lam)


def _seg_scan(x, op, fill, tin, L, reverse=False):
    s = 1
    while s < L:
        if reverse:
            sh = pltpu.roll(x, LANES - s, 1)
            ok = tin < L - s
        else:
            sh = pltpu.roll(x, s, 1)
            ok = tin >= s
        x = op(x, jnp.where(ok, sh, fill))
        s *= 2
    return x


def _mlstm_kernel(x_ref, z_ref, gt_ref, buf_ref, c0_ref, n0_ref, m0_ref,
                  cw_ref, cb_ref, wqk_ref, wv_ref, wkT_ref, bif_ref, ng_ref, sk_ref,
                  y_ref, cN_ref, nN_ref, mN_ref, bN_ref, xs_scr, m_scr, *, S, L, G):
    R = S * L
    GS, GR = G * S, G * R
    j = pl.program_id(1)

    @pl.when(j == 0)
    def _():
        xs_scr[:, 0:SUBLANES, :] = buf_ref[...]
        cN_ref[...] = c0_ref[...]
        nN_ref[...] = n0_ref[...]
        m_scr[...] = m0_ref[...]

    x = x_ref[...].reshape(GR, D_M)
    xc = _causal_conv(xs_scr, x, cw_ref, cb_ref, GS, L, D_M)
    xa = xc * jax.nn.sigmoid(xc)
    xab = xa.astype(bf16)
    xb = x.astype(bf16)

    il_parts, f_parts = [], []
    for g in range(G):
        gT = (gt_ref[g] + bif_ref[...]).T
        il_parts.append(gT[0:SUBLANES, :])
        f_parts.append(pltpu.roll(gT[0:SUBLANES, :], M_HEADS, 0))
    il = jnp.concatenate(il_parts, axis=0)
    fl = -_softplus(-jnp.concatenate(f_parts, axis=0))
    tin = lax.broadcasted_iota(jnp.int32, (G * SUBLANES, LANES), 1) % L
    bcum = _seg_scan(fl, jnp.add, 0.0, tin, L)
    a = il - bcum
    m_prev = m_scr[...].reshape(G * SUBLANES, LANES)
    big_m = jnp.maximum(m_prev, _seg_scan(a, jnp.maximum, -jnp.inf, tin, L))
    m_t = bcum + big_m
    if S == 1:
        m_last = jnp.broadcast_to(big_m[:, LANES - 1:LANES], big_m.shape)
    else:
        m_last = _seg_scan(big_m, jnp.maximum, -jnp.inf, tin, L, reverse=True)
    rows = [big_m, jnp.exp(m_prev - big_m), jnp.exp(-m_t), jnp.exp(a - m_last), jnp.exp(m_prev - m_last)]
    mN_ref[...] = m_t.reshape(G, SUBLANES, LANES)
    m_scr[...] = jnp.broadcast_to(m_t[:, LANES - 1:LANES], m_t.shape).reshape(G, SUBLANES, LANES)
    pad_rows = jnp.zeros((LANES - len(rows) * SUBLANES, LANES), f32)
    cols = []
    for g in range(G):
        gs = slice(g * SUBLANES, (g + 1) * SUBLANES)
        cols.append(jnp.concatenate([r[gs] for r in rows] + [pad_rows], axis=0).T)

    def col(g, q, h):
        return cols[g][:, SUBLANES * q + h:SUBLANES * q + h + 1]

    qs, ks, vs, kTs = [], [], [], []
    for h in range(M_HEADS):
        hs = slice(h * M_DH, (h + 1) * M_DH)
        qk_h = jnp.dot(xab[:, hs], wqk_ref[h], preferred_element_type=f32)
        qs.append(qk_h[:, 0:M_DH])
        ks.append(qk_h[:, M_DH:2 * M_DH] * (M_DH ** -0.5))
        vs.append(jnp.dot(xb[:, hs], wv_ref[h], preferred_element_type=f32))
        if S == 1:
            kT_h = lax.dot_general(wkT_ref[h], xab[:, hs], (((1,), (1,)), ((), ())),
                                   preferred_element_type=f32)
            kTs.append((kT_h * (M_DH ** -0.5)).astype(bf16))

    ti = lax.broadcasted_iota(jnp.int32, (R, R), 0)
    si = lax.broadcasted_iota(jnp.int32, (R, R), 1)
    mask = (si <= ti) & ((ti // L) == (si // L))
    ones_b = jnp.ones((R, M_DH), bf16)
    pairs = [(g, h) for g in range(G) for h in range(M_HEADS)]
    blk = lambda g: slice(g * R, (g + 1) * R)
    seqs = [slice(b * L, (b + 1) * L) for b in range(S)]
    q = {(g, h): qs[h][blk(g)] for g, h in pairs}
    k = {(g, h): ks[h][blk(g)] for g, h in pairs}
    v = {(g, h): vs[h][blk(g)] for g, h in pairs}
    qb = {p: q[p].astype(bf16) for p in pairs}
    kb = {p: k[p].astype(bf16) for p in pairs}
    qk = {p: lax.dot_general(qb[p], kb[p], (((1,), (1,)), ((), ())), preferred_element_type=f32)
          for p in pairs}
    sm = {}
    for g, h in pairs:
        a_row = a[g * SUBLANES + h:g * SUBLANES + h + 1, :]
        decay = jnp.exp(jnp.where(mask, a_row - col(g, 0, h), -jnp.inf))
        sm[g, h] = (qk[g, h] * decay).astype(bf16)
    nd = {p: jnp.dot(sm[p], jnp.concatenate([v[p].astype(bf16), ones_b], axis=1),
                     preferred_element_type=f32) for p in pairs}
    c_old = {(g, h, b): cN_ref[g * S + b, h] for g, h in pairs for b in range(S)}
    n_old = {(g, h, b): nN_ref[g * S + b, h] for g, h in pairs for b in range(S)}
    q_c = {(g, h, b): jnp.dot(qb[g, h][seqs[b]], c_old[g, h, b].astype(bf16), preferred_element_type=f32)
           for g, h in pairs for b in range(S)}
    hh = {}
    for g, h in pairs:
        g_col, e_col = col(g, 1, h), col(g, 2, h)
        pieces = []
        for b, rs in enumerate(seqs):
            q_n = jnp.sum(q[g, h][rs] * n_old[g, h, b], axis=1, keepdims=True)
            num = nd[g, h][rs, 0:M_DH] + g_col[rs] * q_c[g, h, b]
            den = nd[g, h][rs, M_DH:2 * M_DH] + g_col[rs] * q_n
            pieces.append(num / jnp.maximum(jnp.abs(den), e_col[rs]))
        hh[g, h] = pieces[0] if S == 1 else jnp.concatenate(pieces, axis=0)
    mu = {p: jnp.mean(hh[p], axis=1, keepdims=True) for p in pairs}
    var = {p: jnp.mean(jnp.square(hh[p] - mu[p]), axis=1, keepdims=True) for p in pairs}
    hn_blocks = [jnp.concatenate([(hh[g, h] - mu[g, h]) * lax.rsqrt(var[g, h] + LN_EPS)
                                  for h in range(M_HEADS)], axis=1) for g in range(G)]
    new_c, new_n = [], []
    for g, h in pairs:
        w_col = col(g, 3, h)
        wv = (w_col * v[g, h]).astype(bf16)
        wk = w_col * k[g, h]
        for b, rs in enumerate(seqs):
            g_end = col(g, 4, h)[(b + 1) * L - 1:(b + 1) * L, :]
            if S == 1:
                kv = jnp.dot(kTs[h][:, blk(g)], wv, preferred_element_type=f32)
            else:
                kv = lax.dot_general(kb[g, h][rs], wv[rs], (((0,), (0,)), ((), ())),
                                     preferred_element_type=f32)
            new_c.append((g * S + b, h, g_end * c_old[g, h, b] + kv))
            new_n.append((g * S + b, h, g_end * n_old[g, h, b] + jnp.sum(wk[rs], axis=0, keepdims=True)))

    hn = jnp.concatenate(hn_blocks, axis=0) * ng_ref[...]
    y = jax.nn.sigmoid(z_ref[...].reshape(GR, D_M)) * (hn + sk_ref[...] * xa)
    y_ref[...] = y.reshape(G, R, D_M)
    for sq, h, val in new_c:
        cN_ref[sq, h] = val
    for sq, h, val in new_n:
        nN_ref[sq, h] = val
    xs_scr[:, 0:SUBLANES, :] = xs_scr[:, L:L + SUBLANES, :]
    bN_ref[...] = xs_scr[:, SUBLANES - (CONV_W - 1):SUBLANES, :]


def _mlstm(proj, B, T, S, L, G, buf8, c0, n0, m0_lanes, cw, cb, wqk, wv, wkT, bif, ng, sk):
    R = S * L
    assert R == M_ROWS == LANES
    nblk, nc = B // S, T // L
    nb = nblk // G
    assert nblk * S == B and nc * L == T and nb * G == nblk and (S == 1 or nc == 1)
    proj4 = proj.reshape(nblk, nc, R, N_PROJ)
    m0_3 = m0_lanes.reshape(SUBLANES, nblk, LANES).transpose(1, 0, 2)
    const2 = lambda b, j: (0, 0)
    const3 = lambda b, j: (0, 0, 0)
    gate_blk = (2 * D_RG + 2 * D_M) // LANES
    GS = G * S
    outs = pl.pallas_call(
        functools.partial(_mlstm_kernel, S=S, L=L, G=G),
        grid=(nb, nc),
        in_specs=[
            pl.BlockSpec((G, None, R, D_M), lambda b, j: (b, j, 0, 2)),
            pl.BlockSpec((G, None, R, D_M), lambda b, j: (b, j, 0, 3)),
            pl.BlockSpec((G, None, R, LANES), lambda b, j: (b, j, 0, gate_blk)),
            pl.BlockSpec((GS, SUBLANES, D_M), lambda b, j: (b, 0, 0)),
            pl.BlockSpec((GS, M_HEADS, M_DH, M_DH), lambda b, j: (b, 0, 0, 0)),
            pl.BlockSpec((GS, M_HEADS, 1, M_DH), lambda b, j: (b, 0, 0, 0)),
            pl.BlockSpec((G, SUBLANES, LANES), lambda b, j: (b, 0, 0)),
            pl.BlockSpec((CONV_W, D_M), const2),
            pl.BlockSpec((1, D_M), const2),
            pl.BlockSpec((M_HEADS, M_DH, 2 * M_DH), const3),
            pl.BlockSpec((M_HEADS, M_DH, M_DH), const3),
            pl.BlockSpec((M_HEADS, M_DH, M_DH), const3),
            pl.BlockSpec((1, LANES), const2),
            pl.BlockSpec((1, D_M), const2),
            pl.BlockSpec((1, D_M), const2),
        ],
        out_specs=[
            pl.BlockSpec((G, None, R, D_M), lambda b, j: (b, j, 0, 0)),
            pl.BlockSpec((GS, M_HEADS, M_DH, M_DH), lambda b, j: (b, 0, 0, 0)),
            pl.BlockSpec((GS, M_HEADS, 1, M_DH), lambda b, j: (b, 0, 0, 0)),
            pl.BlockSpec((G, None, SUBLANES, LANES), lambda b, j: (b, j, 0, 0)),
            pl.BlockSpec((GS, CONV_W - 1, D_M), lambda b, j: (b, 0, 0)),
        ],
        out_shape=[
            jax.ShapeDtypeStruct((nblk, nc, R, D_M), f32),
            jax.ShapeDtypeStruct((B, M_HEADS, M_DH, M_DH), f32),
            jax.ShapeDtypeStruct((B, M_HEADS, 1, M_DH), f32),
            jax.ShapeDtypeStruct((nblk, nc, SUBLANES, LANES), f32),
            jax.ShapeDtypeStruct((B, CONV_W - 1, D_M), f32),
        ],
        scratch_shapes=[pltpu.VMEM((GS, SUBLANES + L, D_M), f32), pltpu.VMEM((G, SUBLANES, LANES), f32)],
        compiler_params=_cparams(2),
        name="mlstm",
    )(proj4, proj4, proj4, buf8, c0, n0, m0_3, cw, cb, wqk, wv, wkT, bif, ng, sk)
    y, c_new, n_new, m_t, b_new = outs
    m_last = m_t[:, nc - 1, :M_HEADS, :].reshape(nblk, M_HEADS, S, L)[:, :, :, L - 1]
    m_last = m_last.transpose(0, 2, 1).reshape(B, M_HEADS)
    return y.reshape(B * T, D_M), c_new, n_new, m_last, b_new


def _layer_norm(x, g, b):
    mu = jnp.mean(x, axis=-1, keepdims=True)
    var = jnp.mean(jnp.square(x - mu), axis=-1, keepdims=True)
    return (x - mu) * lax.rsqrt(var + LN_EPS) * g + b


def _first_lane_of_max(vals, lane_f):
    vmax = jnp.max(vals, axis=1, keepdims=True)
    idx = jnp.min(jnp.where(vals == vmax, lane_f, float(LANES)), axis=1, keepdims=True)
    return vmax, idx


def _outproj_kernel(xp_ref, xs_ref, rgp_ref, rgs_ref, mp_ref, ms_ref, wo_ref, g1_ref, b1_ref,
                    wrh_ref, wrl_ref, br_ref, x1_ref, cnt_ref, *, n_p):
    i = pl.program_id(0)

    @pl.when(i == 0)
    def _():
        cnt_ref[...] = jnp.zeros_like(cnt_ref)

    def run(x_ref, rg_ref, m_ref):
        mix = jnp.dot(rg_ref[...].astype(bf16), wo_ref[0:D_RG, :], preferred_element_type=f32)
        mix = mix + jnp.dot(m_ref[...].astype(bf16), wo_ref[D_RG:D_RG + D_M, :], preferred_element_type=f32)
        x1 = _layer_norm(ALPHA * x_ref[...] + mix, g1_ref[...], b1_ref[...])
        x1_ref[:, 0:D_MODEL] = x1

        hi = x1.astype(bf16)
        lo = (x1 - hi.astype(f32)).astype(bf16)
        lg = (jnp.dot(hi, wrh_ref[...], preferred_element_type=f32)
              + jnp.dot(lo, wrh_ref[...], preferred_element_type=f32)
              + jnp.dot(hi, wrl_ref[...], preferred_element_type=f32)) + br_ref[...]
        lane = lax.broadcasted_iota(jnp.int32, (TM, LANES), 1)
        lane_f = lane.astype(f32)
        neg = -jnp.inf
        gl = jnp.where(lane < N_GROUPS, lg, neg)
        gmax, gidx = _first_lane_of_max(gl, lane_f)
        p_g = 1.0 / jnp.sum(jnp.exp(gl - gmax), axis=1, keepdims=True)
        e_lo = float(N_GROUPS) + float(EXPERTS_PER_GROUP) * gidx
        el = jnp.where((lane_f >= e_lo) & (lane_f < e_lo + float(EXPERTS_PER_GROUP)), lg, neg)
        v1, i1 = _first_lane_of_max(el, lane_f)
        v2, i2 = _first_lane_of_max(jnp.where(lane_f == i1, neg, el), lane_f)
        d = jnp.exp(v2 - v1)
        w1 = p_g / (1.0 + d)
        w2 = p_g * d / (1.0 + d)

        first_low = i1 < i2
        j_lo = jnp.minimum(i1, i2) - e_lo
        j_hi = jnp.maximum(i1, i2) - e_lo
        w_lo = jnp.where(first_low, w1, w2)
        w_hi = jnp.where(first_low, w2, w1)
        n_first = float(EXPERTS_PER_GROUP - 1) * j_lo - 0.5 * j_lo * (j_lo - 1.0)
        cls = float(PAIRS_PER_GROUP) * gidx + n_first + (j_hi - j_lo - 1.0)

        hot = lane_f == cls
        ti = lax.broadcasted_iota(jnp.int32, (TM, TM), 0)
        si = lax.broadcasted_iota(jnp.int32, (TM, TM), 1)
        tri = (si <= ti).astype(bf16)
        cum = jnp.dot(tri, hot.astype(bf16), preferred_element_type=f32)
        rank = jnp.sum(jnp.where(hot, cum - 1.0 + cnt_ref[...], 0.0), axis=1, keepdims=True)
        cnt_ref[...] = cnt_ref[...] + cum[TM - 1:TM, :]

        info = jnp.zeros((TM, LANES), f32)
        for c, val in enumerate((cls, rank, w_lo, w_hi)):
            info = jnp.where(lane == c, val, info)
        x1_ref[:, D_MODEL:D_MODEL + LANES] = info

    @pl.when(i < n_p)
    def _():
        run(xp_ref, rgp_ref, mp_ref)

    @pl.when(i >= n_p)
    def _():
        run(xs_ref, rgs_ref, ms_ref)


def _outproj(xp, xs, rgp, rgs, mp, ms, wo, g1, b1, wrh, wrl, br):
    n_p, n_s = xp.shape[0] // TM, xs.shape[0] // TM
    n = n_p + n_s
    pmap = lambda i: (jnp.minimum(i, n_p - 1), 0)
    smap = lambda i: (jnp.maximum(i - n_p, 0), 0)
    const = lambda i: (0, 0)
    return pl.pallas_call(
        functools.partial(_outproj_kernel, n_p=n_p),
        grid=(n,),
        in_specs=[
            pl.BlockSpec((TM, D_MODEL), pmap), pl.BlockSpec((TM, D_MODEL), smap),
            pl.BlockSpec((TM, D_RG), pmap), pl.BlockSpec((TM, D_RG), smap),
            pl.BlockSpec((TM, D_M), pmap), pl.BlockSpec((TM, D_M), smap),
            pl.BlockSpec((D_RG + D_M, D_MODEL), const),
            pl.BlockSpec((1, D_MODEL), const), pl.BlockSpec((1, D_MODEL), const),
            pl.BlockSpec((D_MODEL, LANES), const), pl.BlockSpec((D_MODEL, LANES), const),
            pl.BlockSpec((1, LANES), const),
        ],
        out_specs=[
            pl.BlockSpec((TM, X_ROW), lambda i: (i, 0)),
            pl.BlockSpec((1, LANES), const),
        ],
        out_shape=[
            jax.ShapeDtypeStruct((n * TM, X_ROW), f32),
            jax.ShapeDtypeStruct((1, LANES), f32),
        ],
        compiler_params=_cparams(1),
        name="outproj",
    )(xp, xs, rgp, rgs, mp, ms, wo, g1, b1, wrh, wrl, br)


def _row_copy(src_ref, src_row, dst_ref, dst_row, sem):
    return pltpu.make_async_copy(src_ref.at[pl.ds(src_row, 1)], dst_ref.at[pl.ds(dst_row, 1)], sem)


def _tile_rows_wait(src_ref, dst_ref, sem):
    pltpu.make_async_copy(src_ref.at[pl.ds(0, TM)], dst_ref.at[pl.ds(0, TM)], sem).wait()


def _dispatch_kernel(pad_ref, pos_ref, x1_ref, xs_ref, zero_scr, xbuf, sem, zsem, lsem):
    i = pl.program_id(0)
    n = pl.num_programs(0)

    def zero_tile(t):
        return pltpu.make_async_copy(zero_scr, xs_ref.at[pl.ds(pl.multiple_of(t * TM_MOE, TM_MOE), TM_MOE)], zsem)

    def zero_chunk(row0):
        return pltpu.make_async_copy(zero_scr.at[pl.ds(0, SUBLANES)],
                                     xs_ref.at[pl.ds(pl.multiple_of(row0, SUBLANES), SUBLANES)], zsem)

    @pl.when(i == 0)
    def _():
        zero_scr[...] = jnp.zeros_like(zero_scr)
        n_used = pad_ref[2 * N_CLASSES]
        n_tiles = xs_ref.shape[0] // TM_MOE
        for go in (lambda cp: cp.start(), lambda cp: cp.wait()):
            lax.fori_loop(n_used, n_tiles, lambda t, c: (go(zero_tile(t)), c)[1], 0)

            def per_class(cl, c):
                row0 = pad_ref[2 * cl]
                lax.fori_loop(0, pad_ref[2 * cl + 1],
                              lambda k, c2: (go(zero_chunk(row0 + k * SUBLANES)), c2)[1], 0)
                return c

            lax.fori_loop(0, N_CLASSES, per_class, 0)

    def load(t, s):
        return pltpu.make_async_copy(x1_ref.at[pl.ds(pl.multiple_of(t * TM, TM), TM)], xbuf.at[s], lsem.at[s])

    @pl.when(i == 0)
    def _():
        load(0, 0).start()

    @pl.when(i + 1 < n)
    def _():
        load(i + 1, (i + 1) % 3).start()

    slot = i % 3
    load(i, slot).wait()

    def start(r, c):
        _row_copy(xbuf.at[slot], r, xs_ref, pos_ref[0, 0, r], sem.at[slot]).start()
        return c

    lax.fori_loop(0, TM, start, 0, unroll=8)

    @pl.when(i > 0)
    def _():
        _tile_rows_wait(xbuf.at[(i + 2) % 3], xs_ref, sem.at[(i + 2) % 3])

    @pl.when(i == n - 1)
    def _():
        _tile_rows_wait(xbuf.at[slot], xs_ref, sem.at[slot])


def _dispatch(pad_plan, pos, x1, n_rows):
    n = x1.shape[0] // TM
    return pl.pallas_call(
        _dispatch_kernel,
        grid_spec=pltpu.PrefetchScalarGridSpec(
            num_scalar_prefetch=1,
            grid=(n,),
            in_specs=[
                pl.BlockSpec((1, 1, TM), lambda i, lt: (i, 0, 0), memory_space=pltpu.SMEM),
                pl.BlockSpec(memory_space=pl.ANY),
            ],
            out_specs=pl.BlockSpec(memory_space=pl.ANY),
            scratch_shapes=[pltpu.VMEM((TM_MOE, X_ROW), f32), pltpu.VMEM((3, TM, X_ROW), f32),
                            pltpu.SemaphoreType.DMA((3,)), pltpu.SemaphoreType.DMA(()),
                            pltpu.SemaphoreType.DMA((3,))],
        ),
        out_shape=jax.ShapeDtypeStruct((n_rows, X_ROW), f32),
        compiler_params=_cparams(1),
        name="dispatch",
    )(pad_plan, pos, x1)


def _expert_kernel(ta_ref, tb_ref, ca_ref, cb_ref, sa_ref, sb_ref, nu_ref,
                   x_ref, wg_hbm, wu_hbm, wd_hbm, g2_ref, b2_ref, y_ref,
                   wga, wua, wda, wgb, wub, wdb, sem_a, sem_b):
    i = pl.program_id(0)
    n_used = nu_ref[0]
    first = i * MOE_TILES
    side_a = (ta_ref, ca_ref, sa_ref, (wga, wua, wda), sem_a)
    side_b = (tb_ref, cb_ref, sb_ref, (wgb, wub, wdb), sem_b)

    def copies(side, j):
        t_ref, _, s_ref, bufs, sem = side
        e, s = t_ref[j], s_ref[j]
        return [pltpu.make_async_copy(w.at[e], buf.at[s], sem.at[s])
                for w, buf in zip((wg_hbm, wu_hbm, wd_hbm), bufs)]

    def fetch(j):
        @pl.when(j < n_used)
        def _():
            for side in (side_a, side_b):
                @pl.when(side[1][j] == 1)
                def _():
                    for cp in copies(side, j):
                        cp.start()

    @pl.when(i == 0)
    def _():
        for j in range(W_AHEAD):
            fetch(j)

    for t in range(MOE_TILES):
        fetch(first + W_AHEAD + t)

    @pl.when(first < n_used)
    def _():
        for t in range(MOE_TILES):
            for side in (side_a, side_b):
                @pl.when(side[1][first + t] == 1)
                def _():
                    for cp in copies(side, first + t):
                        cp.wait()
        rows = [slice(t * TM_MOE, (t + 1) * TM_MOE) for t in range(MOE_TILES)]
        x = [x_ref[r, 0:D_MODEL] for r in rows]
        info = [x_ref[r, D_MODEL:X_ROW] for r in rows]
        xb = [v.astype(bf16) for v in x]
        units = [(t, bufs, s_ref[first + t]) for t in range(MOE_TILES)
                 for bufs, s_ref in (((wga, wua, wda), sa_ref), ((wgb, wub, wdb), sb_ref))]
        hg = [jnp.dot(xb[t], bufs[0][s].astype(bf16), preferred_element_type=f32) for t, bufs, s in units]
        hu = [jnp.dot(xb[t], bufs[1][s].astype(bf16), preferred_element_type=f32) for t, bufs, s in units]
        mid = [(g * jax.nn.sigmoid(g) * u).astype(bf16) for g, u in zip(hg, hu)]
        ys = [jnp.dot(m, bufs[2][s].astype(bf16), preferred_element_type=f32)
              for m, (t, bufs, s) in zip(mid, units)]
        for t in range(MOE_TILES):
            ffn = info[t][:, 2:3] * ys[2 * t] + info[t][:, 3:4] * ys[2 * t + 1]
            y_ref[rows[t], :] = _layer_norm(ALPHA * x[t] + ffn, g2_ref[...], b2_ref[...])

    @pl.when(first >= n_used)
    def _():
        y_ref[...] = jnp.zeros_like(y_ref)


def _experts(tile_ea, tile_eb, n_used, xs, wg, wu, wd, g2, b2):
    nt = xs.shape[0] // TM_MOE

    def ring_plan(tile_e):
        opens = jnp.concatenate([jnp.ones((1,), jnp.int32), (tile_e[1:] != tile_e[:-1]).astype(jnp.int32)])
        return opens, (jnp.cumsum(opens) - 1) % W_SLOTS

    open_a, slot_a = ring_plan(tile_ea)
    open_b, slot_b = ring_plan(tile_eb)
    assert nt % MOE_TILES == 0
    n_used = ((n_used + MOE_TILES - 1) // MOE_TILES) * MOE_TILES
    const = lambda i, *_: (0, 0)
    xmap = lambda i, ta, tb, ca, cb, sa, sb, nu: (jnp.minimum(i, nu[0] // MOE_TILES - 1), 0)
    w_in = pltpu.VMEM((W_SLOTS, D_MODEL, D_EXPERT), f32)
    w_out = pltpu.VMEM((W_SLOTS, D_EXPERT, D_MODEL), f32)
    return pl.pallas_call(
        _expert_kernel,
        grid_spec=pltpu.PrefetchScalarGridSpec(
            num_scalar_prefetch=7,
            grid=(nt // MOE_TILES,),
            in_specs=[
                pl.BlockSpec((MOE_TILES * TM_MOE, X_ROW), xmap),
                pl.BlockSpec(memory_space=pl.ANY), pl.BlockSpec(memory_space=pl.ANY),
                pl.BlockSpec(memory_space=pl.ANY),
                pl.BlockSpec((1, D_MODEL), const), pl.BlockSpec((1, D_MODEL), const),
            ],
            out_specs=pl.BlockSpec((MOE_TILES * TM_MOE, D_MODEL), lambda i, *_: (i, 0)),
            scratch_shapes=[w_in, w_in, w_out, w_in, w_in, w_out,
                            pltpu.SemaphoreType.DMA((W_SLOTS,)), pltpu.SemaphoreType.DMA((W_SLOTS,))],
        ),
        out_shape=jax.ShapeDtypeStruct((nt * TM_MOE, D_MODEL), f32),
        compiler_params=_cparams(1),
        name="experts",
    )(tile_ea, tile_eb, open_a, open_b, slot_a.astype(jnp.int32), slot_b.astype(jnp.int32), n_used,
      xs, wg, wu, wd, g2, b2)


def _collect_kernel(pos_ref, posn_ref, ys_ref, op_ref, os_ref, ybuf, sem, *, n_p):
    i = pl.program_id(0)
    n = pl.num_programs(0)
    slot = i % 2

    def gather(p_ref, s):
        def start(r, c):
            _row_copy(ys_ref, p_ref[0, 0, r], ybuf.at[s], r, sem.at[s]).start()
            return c

        lax.fori_loop(0, TM, start, 0, unroll=8)

    @pl.when(i == 0)
    def _():
        gather(pos_ref, 0)

    @pl.when(i + 1 < n)
    def _():
        gather(posn_ref, 1 - slot)

    _tile_rows_wait(ys_ref, ybuf.at[slot], sem.at[slot])

    @pl.when(i < n_p)
    def _():
        op_ref[...] = ybuf[slot]

    @pl.when(i >= n_p)
    def _():
        os_ref[...] = ybuf[slot]


def _collect(pos, ys, n_p):
    n = pos.shape[0]
    n_s = n - n_p
    return pl.pallas_call(
        functools.partial(_collect_kernel, n_p=n_p),
        grid=(n,),
        in_specs=[
            pl.BlockSpec((1, 1, TM), lambda i: (i, 0, 0), memory_space=pltpu.SMEM),
            pl.BlockSpec((1, 1, TM), lambda i: (jnp.minimum(i + 1, n - 1), 0, 0), memory_space=pltpu.SMEM),
            pl.BlockSpec(memory_space=pl.ANY),
        ],
        out_specs=[
            pl.BlockSpec((TM, D_MODEL), lambda i: (jnp.minimum(i, n_p - 1), 0)),
            pl.BlockSpec((TM, D_MODEL), lambda i: (jnp.maximum(i - n_p, 0), 0)),
        ],
        out_shape=[
            jax.ShapeDtypeStruct((n_p * TM, D_MODEL), f32),
            jax.ShapeDtypeStruct((n_s * TM, D_MODEL), f32),
        ],
        scratch_shapes=[pltpu.VMEM((2, TM, D_MODEL), f32), pltpu.SemaphoreType.DMA((2,))],
        compiler_params=_cparams(1),
        name="collect",
    )(pos, pos, ys)


def _block_diag(w):
    n, d, _ = w.shape
    eye = jnp.eye(n, dtype=w.dtype)
    return (eye[:, None, :, None] * w[:, :, None, :]).reshape(n * d, n * d)


def _pad_history(buf):
    return jnp.pad(buf, ((0, 0), (SUBLANES - (CONV_W - 1), 0), (0, 0)))


def kernel(x_prompt, x_sample, state_rg_h, state_rg_conv, state_m_C, state_m_n, state_m_m, state_m_conv, w_in, rg_conv_w, rg_conv_b, rg_w_a, rg_b_a, rg_w_x, rg_b_x, rg_lambda, m_conv_w, m_conv_b, m_w_q, m_w_k, m_w_v, m_b_i, m_b_f, m_norm_g, m_skip, w_out, ln1_g, ln1_b, ln2_g, ln2_b, moe_w_group, moe_b_group, moe_w_expert, moe_b_expert, moe_w_gate, moe_w_up, moe_w_down):
    BP, TP, _ = x_prompt.shape
    BS, TS, _ = x_sample.shape
    n_prompt, n_sample = BP * TP, BS * TS
    xp = x_prompt.reshape(n_prompt, D_MODEL)
    xs = x_sample.reshape(n_sample, D_MODEL)
    l = 0

    w_in_p = jnp.pad(w_in[l], ((0, 0), (0, N_PROJ - w_in.shape[-1]))).astype(bf16)
    wa = _block_diag(rg_w_a[l]).astype(bf16)
    wx = _block_diag(rg_w_x[l]).astype(bf16)
    row = lambda v: v.reshape(1, -1)
    bif = jnp.pad(jnp.concatenate([m_b_i[l], m_b_f[l]]), (0, LANES - 2 * M_HEADS)).reshape(1, LANES)
    wqk = jnp.concatenate([m_w_q[l], m_w_k[l]], axis=-1).astype(bf16)
    wv = m_w_v[l].astype(bf16)
    wkT = m_w_k[l].transpose(0, 2, 1).astype(bf16)
    w_route = jnp.pad(jnp.concatenate([moe_w_group[l], moe_w_expert[l]], axis=1),
                      ((0, 0), (0, LANES - N_GROUPS - N_EXPERTS)))
    wr_hi = w_route.astype(bf16)
    wr_lo = (w_route - wr_hi.astype(f32)).astype(bf16)
    b_route = jnp.pad(jnp.concatenate([moe_b_group[l], moe_b_expert[l]]),
                      (0, LANES - N_GROUPS - N_EXPERTS)).reshape(1, LANES)

    proj_p, proj_s = _inproj(xp, xs, w_in_p)

    rg_args = (rg_conv_w[l], row(rg_conv_b[l]), wa, row(rg_b_a[l]), wx, row(rg_b_x[l]), row(rg_lambda[l]))
    zeros = lambda *s: jnp.zeros(s, f32)
    s_blk = M_ROWS // TS
    yrg_p, p_rg_h, p_rg_conv = _rglru(proj_p, 0, BP, TP, 1, RG_ROWS_PROMPT, zeros(BP, SUBLANES, D_RG),
                                      zeros(BP, 1, D_RG), *rg_args)
    yrg_s, s_rg_h, s_rg_conv = _rglru(proj_s, 0, BS, TS, s_blk, TS, _pad_history(state_rg_conv[l]),
                                      state_rg_h[l].reshape(BS, 1, D_RG), *rg_args)

    m_args = (m_conv_w[l], row(m_conv_b[l]), wqk, wv, wkT, bif, row(m_norm_g[l]), row(m_skip[l]))
    m0_p = jnp.full((SUBLANES, BP * LANES), M_INIT, f32)
    ym_p, p_m_C, p_m_n, p_m_m, p_m_conv = _mlstm(
        proj_p, BP, TP, 1, M_ROWS, M_BLOCKS_PROMPT, zeros(BP, SUBLANES, D_M), zeros(BP, M_HEADS, M_DH, M_DH),
        zeros(BP, M_HEADS, 1, M_DH), m0_p, *m_args)
    m0_s = jnp.pad(jnp.repeat(state_m_m[l].T, TS, axis=1), ((0, SUBLANES - M_HEADS), (0, 0)))
    ym_s, s_m_C, s_m_n, s_m_m, s_m_conv = _mlstm(
        proj_s, BS, TS, s_blk, TS, M_BLOCKS_SAMPLE, _pad_history(state_m_conv[l]), state_m_C[l],
        state_m_n[l].reshape(BS, M_HEADS, 1, M_DH), m0_s, *m_args)

    x1, cnt = _outproj(xp, xs, yrg_p, yrg_s, ym_p, ym_s, w_out[l].astype(bf16),
                       row(ln1_g[l]), row(ln1_b[l]), wr_hi, wr_lo, b_route)

    n_tok = n_prompt + n_sample
    n_tiles = n_tok // TM_MOE + N_CLASSES
    counts = cnt[0, :N_CLASSES].astype(jnp.int32)
    padded = ((counts + TM_MOE - 1) // TM_MOE) * TM_MOE
    ends = jnp.cumsum(padded)
    offs = ends - padded
    classes = jnp.arange(N_CLASSES, dtype=jnp.int32)
    cls = x1[:, D_MODEL].astype(jnp.int32)
    rank = x1[:, D_MODEL + 1].astype(jnp.int32)
    pos = jnp.sum(jnp.where(cls[:, None] == classes, offs, 0), axis=-1) + rank
    pos = pos.reshape(n_tok // TM, 1, TM)
    tiles = jnp.arange(n_tiles, dtype=jnp.int32)
    tile_cls = jnp.minimum(jnp.sum(tiles[:, None] >= (ends // TM_MOE)[None, :], axis=1), N_CLASSES - 1)
    pair_a, pair_b = [], []
    for g in range(N_GROUPS):
        for ja in range(EXPERTS_PER_GROUP):
            for jb in range(ja + 1, EXPERTS_PER_GROUP):
                pair_a.append(g * EXPERTS_PER_GROUP + ja)
                pair_b.append(g * EXPERTS_PER_GROUP + jb)
    on_cls = tile_cls[:, None] == classes
    tile_ea = jnp.sum(jnp.where(on_cls, jnp.array(pair_a, jnp.int32), 0), axis=-1).astype(jnp.int32)
    tile_eb = jnp.sum(jnp.where(on_cls, jnp.array(pair_b, jnp.int32), 0), axis=-1).astype(jnp.int32)
    n_used = (ends[-1] // TM_MOE).reshape(1).astype(jnp.int32)
    pad_row0 = ((offs + counts) // SUBLANES) * SUBLANES
    pad_plan = jnp.concatenate([jnp.stack([pad_row0, (ends - pad_row0) // SUBLANES], axis=1).reshape(-1),
                                     n_used]).astype(jnp.int32)

    x_sorted = _dispatch(pad_plan, pos, x1, n_tiles * TM_MOE)
    y_sorted = _experts(tile_ea, tile_eb, n_used, x_sorted, moe_w_gate[l], moe_w_up[l], moe_w_down[l],
                        row(ln2_g[l]), row(ln2_b[l]))
    y_p, y_s = _collect(pos, y_sorted, n_prompt // TM)

    return (y_p.reshape(BP, TP, D_MODEL), y_s.reshape(BS, TS, D_MODEL),
            p_rg_h.reshape(1, BP, D_RG), p_rg_conv[None], p_m_C[None], p_m_n.reshape(1, BP, M_HEADS, M_DH),
            p_m_m[None], p_m_conv[None],
            s_rg_h.reshape(1, BS, D_RG), s_rg_conv[None], s_m_C[None], s_m_n.reshape(1, BS, M_HEADS, M_DH),
            s_m_m[None], s_m_conv[None])
```

```python
import functools

import jax
import jax.numpy as jnp
from jax import lax
from jax.experimental import pallas as pl
from jax.experimental.pallas import tpu as pltpu

f32 = jnp.float32
bf16 = jnp.bfloat16

D_MODEL = 1024
D_RG = 512
RG_BLOCKS = 8
RG_C = 8.0
D_M = 512
M_HEADS = 4
M_DH = 128
CONV_W = 4
N_GROUPS = 4
EXPERTS_PER_GROUP = 8
N_EXPERTS = 32
D_EXPERT = 256
ALPHA = 2.0 ** 0.25
LN_EPS = 1e-5
M_INIT = -1.0e4

LANES = 128
SUBLANES = 8
TM = 512
TM_MOE = 128
MOE_TILES = 2
W_AHEAD = 4
W_SLOTS = W_AHEAD + MOE_TILES
PAIRS_PER_GROUP = EXPERTS_PER_GROUP * (EXPERTS_PER_GROUP - 1) // 2
N_CLASSES = N_GROUPS * PAIRS_PER_GROUP
X_ROW = D_MODEL + LANES
N_PROJ = 2 * D_RG + 2 * D_M + LANES
RG_ROWS_PROMPT = 512
M_ROWS = 128
M_BLOCKS_PROMPT = 8
M_BLOCKS_SAMPLE = 2
VMEM_LIMIT = 56 * 1024 * 1024


def _cparams(n_axes):
    return pltpu.CompilerParams(dimension_semantics=("arbitrary",) * n_axes,
                                vmem_limit_bytes=VMEM_LIMIT)


def _inproj_kernel(xp_ref, xs_ref, w_ref, op_ref, os_ref, *, n_p):
    i = pl.program_id(0)

    def run(x_ref, o_ref):
        o_ref[...] = jnp.dot(x_ref[...].astype(bf16), w_ref[...], preferred_element_type=f32)

    @pl.when(i < n_p)
    def _():
        run(xp_ref, op_ref)

    @pl.when(i >= n_p)
    def _():
        run(xs_ref, os_ref)


def _inproj(xp, xs, w):
    n_p, n_s = xp.shape[0] // TM, xs.shape[0] // TM
    pmap = lambda i: (jnp.minimum(i, n_p - 1), 0)
    smap = lambda i: (jnp.maximum(i - n_p, 0), 0)
    return pl.pallas_call(
        functools.partial(_inproj_kernel, n_p=n_p),
        grid=(n_p + n_s,),
        in_specs=[
            pl.BlockSpec((TM, D_MODEL), pmap),
            pl.BlockSpec((TM, D_MODEL), smap),
            pl.BlockSpec((D_MODEL, N_PROJ), lambda i: (0, 0)),
        ],
        out_specs=[pl.BlockSpec((TM, N_PROJ), pmap), pl.BlockSpec((TM, N_PROJ), smap)],
        out_shape=[jax.ShapeDtypeStruct((n_p * TM, N_PROJ), f32),
                   jax.ShapeDtypeStruct((n_s * TM, N_PROJ), f32)],
        compiler_params=_cparams(1),
        name="inproj",
    )(xp, xs, w)


def _causal_conv(xs_scr, x, cw_ref, cb_ref, S, L, C):
    xs_scr[:, SUBLANES:SUBLANES + L, :] = x.reshape(S, L, C)
    acc = cb_ref[...] + cw_ref[CONV_W - 1:CONV_W, :] * x
    for j in range(CONV_W - 1):
        lo = SUBLANES - (CONV_W - 1) + j
        acc = acc + cw_ref[j:j + 1, :] * xs_scr[:, lo:lo + L, :].reshape(S * L, C)
    return acc


def _softplus(x):
    return jnp.maximum(x, 0.0) + jnp.log1p(jnp.exp(-jnp.abs(x)))


def _rglru_kernel(x_ref, g_ref, buf_ref, h0_ref, cw_ref, cb_ref, wa_ref, ba_ref, wx_ref, bx_ref,
                  lam_ref, y_ref, hN_ref, cN_ref, xs_scr, *, S, L):
    R = S * L
    t = pl.program_id(1)

    @pl.when(t == 0)
    def _():
        xs_scr[:, 0:SUBLANES, :] = buf_ref[...]
        hN_ref[...] = h0_ref[...]

    x = x_ref[...]
    xc = _causal_conv(xs_scr, x, cw_ref, cb_ref, S, L, D_RG)
    xcb = xc.astype(bf16)
    r = jax.nn.sigmoid(jnp.dot(xcb, wa_ref[...], preferred_element_type=f32) + ba_ref[...])
    ig = jax.nn.sigmoid(jnp.dot(xcb, wx_ref[...], preferred_element_type=f32) + bx_ref[...])
    log_a = (-RG_C) * r * _softplus(-lam_ref[...])
    a = jnp.exp(log_a)
    th = jnp.tanh(log_a)
    u = jnp.sqrt(-2.0 * th / (1.0 - th)) * ig * xc

    n_grp, grp_per_seq = R // SUBLANES, L // SUBLANES
    a3 = a.reshape(n_grp, SUBLANES, D_RG)
    u3 = u.reshape(n_grp, SUBLANES, D_RG)
    sub = lax.broadcasted_iota(jnp.int32, (n_grp, SUBLANES, D_RG), 1)
    s = 1
    while s < SUBLANES:
        ok = sub >= s
        a_sh = pltpu.roll(a3, s, 1)
        u_sh = pltpu.roll(u3, s, 1)
        u3 = jnp.where(ok, a3 * u_sh + u3, u3)
        a3 = jnp.where(ok, a3 * a_sh, a3)
        s *= 2
    h0 = hN_ref[...]
    groups = []
    for kg in range(n_grp):
        carry = h0[kg // grp_per_seq] if kg % grp_per_seq == 0 else groups[-1][SUBLANES - 1:SUBLANES, :]
        groups.append(a3[kg] * carry + u3[kg])
    h = jnp.concatenate(groups, axis=0)

    y_ref[...] = h * jax.nn.gelu(g_ref[...], approximate=True)
    hN_ref[...] = h.reshape(S, L, D_RG)[:, L - 1:L, :]
    xs_scr[:, 0:SUBLANES, :] = xs_scr[:, L:L + SUBLANES, :]
    cN_ref[...] = xs_scr[:, SUBLANES - (CONV_W - 1):SUBLANES, :]


def _rglru(proj, row0, B, T, S, L, buf8, h0, cw, cb, wa, ba, wx, bx, lam):
    R = S * L
    nb, nt = B // S, T // L
    blk0 = row0 // R
    row_map = lambda b, t: (blk0 + b * nt + t, 0)
    const2 = lambda b, t: (0, 0)
    return pl.pallas_call(
        functools.partial(_rglru_kernel, S=S, L=L),
        grid=(nb, nt),
        in_specs=[
            pl.BlockSpec((R, D_RG), row_map),
            pl.BlockSpec((R, D_RG), lambda b, t: (blk0 + b * nt + t, 1)),
            pl.BlockSpec((S, SUBLANES, D_RG), lambda b, t: (b, 0, 0)),
            pl.BlockSpec((S, 1, D_RG), lambda b, t: (b, 0, 0)),
            pl.BlockSpec((CONV_W, D_RG), const2),
            pl.BlockSpec((1, D_RG), const2),
            pl.BlockSpec((D_RG, D_RG), const2),
            pl.BlockSpec((1, D_RG), const2),
            pl.BlockSpec((D_RG, D_RG), const2),
            pl.BlockSpec((1, D_RG), const2),
            pl.BlockSpec((1, D_RG), const2),
        ],
        out_specs=[
            pl.BlockSpec((R, D_RG), lambda b, t: (b * nt + t, 0)),
            pl.BlockSpec((S, 1, D_RG), lambda b, t: (b, 0, 0)),
            pl.BlockSpec((S, CONV_W - 1, D_RG), lambda b, t: (b, 0, 0)),
        ],
        out_shape=[
            jax.ShapeDtypeStruct((B * T, D_RG), f32),
            jax.ShapeDtypeStruct((B, 1, D_RG), f32),
            jax.ShapeDtypeStruct((B, CONV_W - 1, D_RG), f32),
        ],
        scratch_shapes=[pltpu.VMEM((S, SUBLANES + L, D_RG), f32)],
        compiler_params=_cparams(2),
        name="rglru",
    )(proj, proj, buf8, h0, cw, cb, wa, ba, wx, bx, lam)


def _seg_scan(x, op, fill, tin, L, reverse=False):
    s = 1
    while s < L:
        if reverse:
            sh = pltpu.roll(x, LANES - s, 1)
            ok = tin < L - s
        else:
            sh = pltpu.roll(x, s, 1)
            ok = tin >= s
        x = op(x, jnp.where(ok, sh, fill))
        s *= 2
    return x


def _mlstm_kernel(x_ref, z_ref, gt_ref, buf_ref, c0_ref, n0_ref, m0_ref,
                  cw_ref, cb_ref, wqk_ref, wv_ref, wkT_ref, bif_ref, ng_ref, sk_ref,
                  y_ref, cN_ref, nN_ref, mN_ref, bN_ref, xs_scr, m_scr, *, S, L, G):
    R = S * L
    GS, GR = G * S, G * R
    j = pl.program_id(1)

    @pl.when(j == 0)
    def _():
        xs_scr[:, 0:SUBLANES, :] = buf_ref[...]
        cN_ref[...] = c0_ref[...]
        nN_ref[...] = n0_ref[...]
        m_scr[...] = m0_ref[...]

    x = x_ref[...].reshape(GR, D_M)
    xc = _causal_conv(xs_scr, x, cw_ref, cb_ref, GS, L, D_M)
    xa = xc * jax.nn.sigmoid(xc)
    xab = xa.astype(bf16)
    xb = x.astype(bf16)

    il_parts, f_parts = [], []
    for g in range(G):
        gT = (gt_ref[g] + bif_ref[...]).T
        il_parts.append(gT[0:SUBLANES, :])
        f_parts.append(pltpu.roll(gT[0:SUBLANES, :], M_HEADS, 0))
    il = jnp.concatenate(il_parts, axis=0)
    fl = -_softplus(-jnp.concatenate(f_parts, axis=0))
    tin = lax.broadcasted_iota(jnp.int32, (G * SUBLANES, LANES), 1) % L
    bcum = _seg_scan(fl, jnp.add, 0.0, tin, L)
    a = il - bcum
    m_prev = m_scr[...].reshape(G * SUBLANES, LANES)
    big_m = jnp.maximum(m_prev, _seg_scan(a, jnp.maximum, -jnp.inf, tin, L))
    m_t = bcum + big_m
    if S == 1:
        m_last = jnp.broadcast_to(big_m[:, LANES - 1:LANES], big_m.shape)
    else:
        m_last = _seg_scan(big_m, jnp.maximum, -jnp.inf, tin, L, reverse=True)
    rows = [big_m, jnp.exp(m_prev - big_m), jnp.exp(-m_t), jnp.exp(a - m_last), jnp.exp(m_prev - m_last)]
    mN_ref[...] = m_t.reshape(G, SUBLANES, LANES)
    m_scr[...] = jnp.broadcast_to(m_t[:, LANES - 1:LANES], m_t.shape).reshape(G, SUBLANES, LANES)
    pad_rows = jnp.zeros((LANES - len(rows) * SUBLANES, LANES), f32)
    cols = []
    for g in range(G):
        gs = slice(g * SUBLANES, (g + 1) * SUBLANES)
        cols.append(jnp.concatenate([r[gs] for r in rows] + [pad_rows], axis=0).T)

    def col(g, q, h):
        return cols[g][:, SUBLANES * q + h:SUBLANES * q + h + 1]

    qs, ks, vs, kTs = [], [], [], []
    for h in range(M_HEADS):
        hs = slice(h * M_DH, (h + 1) * M_DH)
        qk_h = jnp.dot(xab[:, hs], wqk_ref[h], preferred_element_type=f32)
        qs.append(qk_h[:, 0:M_DH])
        ks.append(qk_h[:, M_DH:2 * M_DH] * (M_DH ** -0.5))
        vs.append(jnp.dot(xb[:, hs], wv_ref[h], preferred_element_type=f32))
        if S == 1:
            kT_h = lax.dot_general(wkT_ref[h], xab[:, hs], (((1,), (1,)), ((), ())),
                                   preferred_element_type=f32)
            kTs.append((kT_h * (M_DH ** -0.5)).astype(bf16))

    ti = lax.broadcasted_iota(jnp.int32, (R, R), 0)
    si = lax.broadcasted_iota(jnp.int32, (R, R), 1)
    mask = (si <= ti) & ((ti // L) == (si // L))
    ones_b = jnp.ones((R, M_DH), bf16)
    pairs = [(g, h) for g in range(G) for h in range(M_HEADS)]
    blk = lambda g: slice(g * R, (g + 1) * R)
    seqs = [slice(b * L, (b + 1) * L) for b in range(S)]
    q = {(g, h): qs[h][blk(g)] for g, h in pairs}
    k = {(g, h): ks[h][blk(g)] for g, h in pairs}
    v = {(g, h): vs[h][blk(g)] for g, h in pairs}
    qb = {p: q[p].astype(bf16) for p in pairs}
    kb = {p: k[p].astype(bf16) for p in pairs}
    qk = {p: lax.dot_general(qb[p], kb[p], (((1,), (1,)), ((), ())), preferred_element_type=f32)
          for p in pairs}
    sm = {}
    for g, h in pairs:
        a_row = a[g * SUBLANES + h:g * SUBLANES + h + 1, :]
        decay = jnp.exp(jnp.where(mask, a_row - col(g, 0, h), -jnp.inf))
        sm[g, h] = (qk[g, h] * decay).astype(bf16)
    nd = {p: jnp.dot(sm[p], jnp.concatenate([v[p].astype(bf16), ones_b], axis=1),
                     preferred_element_type=f32) for p in pairs}
    c_old = {(g, h, b): cN_ref[g * S + b, h] for g, h in pairs for b in range(S)}
    n_old = {(g, h, b): nN_ref[g * S + b, h] for g, h in pairs for b in range(S)}
    q_c = {(g, h, b): jnp.dot(qb[g, h][seqs[b]], c_old[g, h, b].astype(bf16), preferred_element_type=f32)
           for g, h in pairs for b in range(S)}
    hh = {}
    for g, h in pairs:
        g_col, e_col = col(g, 1, h), col(g, 2, h)
        pieces = []
        for b, rs in enumerate(seqs):
            q_n = jnp.sum(q[g, h][rs] * n_old[g, h, b], axis=1, keepdims=True)
            num = nd[g, h][rs, 0:M_DH] + g_col[rs] * q_c[g, h, b]
            den = nd[g, h][rs, M_DH:2 * M_DH] + g_col[rs] * q_n
            pieces.append(num / jnp.maximum(jnp.abs(den), e_col[rs]))
        hh[g, h] = pieces[0] if S == 1 else jnp.concatenate(pieces, axis=0)
    mu = {p: jnp.mean(hh[p], axis=1, keepdims=True) for p in pairs}
    var = {p: jnp.mean(jnp.square(hh[p] - mu[p]), axis=1, keepdims=True) for p in pairs}
    hn_blocks = [jnp.concatenate([(hh[g, h] - mu[g, h]) * lax.rsqrt(var[g, h] + LN_EPS)
                                  for h in range(M_HEADS)], axis=1) for g in range(G)]
    new_c, new_n = [], []
    for g, h in pairs:
        w_col = col(g, 3, h)
        wv = (w_col * v[g, h]).astype(bf16)
        wk = w_col * k[g, h]
        for b, rs in enumerate(seqs):
            g_end = col(g, 4, h)[(b + 1) * L - 1:(b + 1) * L, :]
            if S == 1:
                kv = jnp.dot(kTs[h][:, blk(g)], wv, preferred_element_type=f32)
            else:
                kv = lax.dot_general(kb[g, h][rs], wv[rs], (((0,), (0,)), ((), ())),
                                     preferred_element_type=f32)
            new_c.append((g * S + b, h, g_end * c_old[g, h, b] + kv))
            new_n.append((g * S + b, h, g_end * n_old[g, h, b] + jnp.sum(wk[rs], axis=0, keepdims=True)))

    hn = jnp.concatenate(hn_blocks, axis=0) * ng_ref[...]
    y = jax.nn.sigmoid(z_ref[...].reshape(GR, D_M)) * (hn + sk_ref[...] * xa)
    y_ref[...] = y.reshape(G, R, D_M)
    for sq, h, val in new_c:
        cN_ref[sq, h] = val
    for sq, h, val in new_n:
        nN_ref[sq, h] = val
    xs_scr[:, 0:SUBLANES, :] = xs_scr[:, L:L + SUBLANES, :]
    bN_ref[...] = xs_scr[:, SUBLANES - (CONV_W - 1):SUBLANES, :]


def _mlstm(proj, B, T, S, L, G, buf8, c0, n0, m0_lanes, cw, cb, wqk, wv, wkT, bif, ng, sk):
    R = S * L
    assert R == M_ROWS == LANES
    nblk, nc = B // S, T // L
    nb = nblk // G
    assert nblk * S == B and nc * L == T and nb * G == nblk and (S == 1 or nc == 1)
    proj4 = proj.reshape(nblk, nc, R, N_PROJ)
    m0_3 = m0_lanes.reshape(SUBLANES, nblk, LANES).transpose(1, 0, 2)
    const2 = lambda b, j: (0, 0)
    const3 = lambda b, j: (0, 0, 0)
    gate_blk = (2 * D_RG + 2 * D_M) // LANES
    GS = G * S
    outs = pl.pallas_call(
        functools.partial(_mlstm_kernel, S=S, L=L, G=G),
        grid=(nb, nc),
        in_specs=[
            pl.BlockSpec((G, None, R, D_M), lambda b, j: (b, j, 0, 2)),
            pl.BlockSpec((G, None, R, D_M), lambda b, j: (b, j, 0, 3)),
            pl.BlockSpec((G, None, R, LANES), lambda b, j: (b, j, 0, gate_blk)),
            pl.BlockSpec((GS, SUBLANES, D_M), lambda b, j: (b, 0, 0)),
            pl.BlockSpec((GS, M_HEADS, M_DH, M_DH), lambda b, j: (b, 0, 0, 0)),
            pl.BlockSpec((GS, M_HEADS, 1, M_DH), lambda b, j: (b, 0, 0, 0)),
            pl.BlockSpec((G, SUBLANES, LANES), lambda b, j: (b, 0, 0)),
            pl.BlockSpec((CONV_W, D_M), const2),
            pl.BlockSpec((1, D_M), const2),
            pl.BlockSpec((M_HEADS, M_DH, 2 * M_DH), const3),
            pl.BlockSpec((M_HEADS, M_DH, M_DH), const3),
            pl.BlockSpec((M_HEADS, M_DH, M_DH), const3),
            pl.BlockSpec((1, LANES), const2),
            pl.BlockSpec((1, D_M), const2),
            pl.BlockSpec((1, D_M), const2),
        ],
        out_specs=[
            pl.BlockSpec((G, None, R, D_M), lambda b, j: (b, j, 0, 0)),
            pl.BlockSpec((GS, M_HEADS, M_DH, M_DH), lambda b, j: (b, 0, 0, 0)),
            pl.BlockSpec((GS, M_HEADS, 1, M_DH), lambda b, j: (b, 0, 0, 0)),
            pl.BlockSpec((G, None, SUBLANES, LANES), lambda b, j: (b, j, 0, 0)),
            pl.BlockSpec((GS, CONV_W - 1, D_M), lambda b, j: (b, 0, 0)),
        ],
        out_shape=[
            jax.ShapeDtypeStruct((nblk, nc, R, D_M), f32),
            jax.ShapeDtypeStruct((B, M_HEADS, M_DH, M_DH), f32),
            jax.ShapeDtypeStruct((B, M_HEADS, 1, M_DH), f32),
            jax.ShapeDtypeStruct((nblk, nc, SUBLANES, LANES), f32),
            jax.ShapeDtypeStruct((B, CONV_W - 1, D_M), f32),
        ],
        scratch_shapes=[pltpu.VMEM((GS, SUBLANES + L, D_M), f32), pltpu.VMEM((G, SUBLANES, LANES), f32)],
        compiler_params=_cparams(2),
        name="mlstm",
    )(proj4, proj4, proj4, buf8, c0, n0, m0_3, cw, cb, wqk, wv, wkT, bif, ng, sk)
    y, c_new, n_new, m_t, b_new = outs
    m_last = m_t[:, nc - 1, :M_HEADS, :].reshape(nblk, M_HEADS, S, L)[:, :, :, L - 1]
    m_last = m_last.transpose(0, 2, 1).reshape(B, M_HEADS)
    return y.reshape(B * T, D_M), c_new, n_new, m_last, b_new


def _layer_norm(x, g, b):
    mu = jnp.mean(x, axis=-1, keepdims=True)
    var = jnp.mean(jnp.square(x - mu), axis=-1, keepdims=True)
    return (x - mu) * lax.rsqrt(var + LN_EPS) * g + b


def _first_lane_of_max(vals, lane_f):
    vmax = jnp.max(vals, axis=1, keepdims=True)
    idx = jnp.min(jnp.where(vals == vmax, lane_f, float(LANES)), axis=1, keepdims=True)
    return vmax, idx


def _outproj_kernel(xp_ref, xs_ref, rgp_ref, rgs_ref, mp_ref, ms_ref, wo_ref, g1_ref, b1_ref,
                    wr_ref, br_ref, x1_ref, cnt_ref, *, n_p):
    i = pl.program_id(0)

    @pl.when(i == 0)
    def _():
        cnt_ref[...] = jnp.zeros_like(cnt_ref)

    def run(x_ref, rg_ref, m_ref):
        heads = jnp.concatenate([rg_ref[...].astype(bf16), m_ref[...].astype(bf16)], axis=1)
        mix = jnp.dot(heads, wo_ref[...], preferred_element_type=f32)
        x1 = _layer_norm(ALPHA * x_ref[...] + mix, g1_ref[...], b1_ref[...])
        x1_ref[:, 0:D_MODEL] = x1

        hi = x1.astype(bf16)
        lo = (x1 - hi.astype(f32)).astype(bf16)
        hi_terms = jnp.dot(hi, wr_ref[...], preferred_element_type=f32)
        lo_term = jnp.dot(lo, wr_ref[:, 0:LANES], preferred_element_type=f32)
        lg = hi_terms[:, 0:LANES] + hi_terms[:, LANES:2 * LANES] + lo_term + br_ref[...]
        lane = lax.broadcasted_iota(jnp.int32, (TM, LANES), 1)
        lane_f = lane.astype(f32)
        neg = -jnp.inf
        gl = jnp.where(lane < N_GROUPS, lg, neg)
        gmax, gidx = _first_lane_of_max(gl, lane_f)
        p_g = 1.0 / jnp.sum(jnp.exp(gl - gmax), axis=1, keepdims=True)
        e_lo = float(N_GROUPS) + float(EXPERTS_PER_GROUP) * gidx
        el = jnp.where((lane_f >= e_lo) & (lane_f < e_lo + float(EXPERTS_PER_GROUP)), lg, neg)
        v1, i1 = _first_lane_of_max(el, lane_f)
        v2, i2 = _first_lane_of_max(jnp.where(lane_f == i1, neg, el), lane_f)
        d = jnp.exp(v2 - v1)
        w1 = p_g / (1.0 + d)
        w2 = p_g * d / (1.0 + d)

        first_low = i1 < i2
        j_lo = jnp.minimum(i1, i2) - e_lo
        j_hi = jnp.maximum(i1, i2) - e_lo
        w_lo = jnp.where(first_low, w1, w2)
        w_hi = jnp.where(first_low, w2, w1)
        n_first = float(EXPERTS_PER_GROUP - 1) * j_lo - 0.5 * j_lo * (j_lo - 1.0)
        cls = float(PAIRS_PER_GROUP) * gidx + n_first + (j_hi - j_lo - 1.0)

        hot = lane_f == cls
        ti = lax.broadcasted_iota(jnp.int32, (TM, TM), 0)
        si = lax.broadcasted_iota(jnp.int32, (TM, TM), 1)
        tri = (si <= ti).astype(bf16)
        cum = jnp.dot(tri, hot.astype(bf16), preferred_element_type=f32)
        rank = jnp.sum(jnp.where(hot, cum - 1.0 + cnt_ref[...], 0.0), axis=1, keepdims=True)
        cnt_ref[...] = cnt_ref[...] + cum[TM - 1:TM, :]

        info = jnp.zeros((TM, LANES), f32)
        for c, val in enumerate((cls, rank, w_lo, w_hi)):
            info = jnp.where(lane == c, val, info)
        x1_ref[:, D_MODEL:D_MODEL + LANES] = info

    @pl.when(i < n_p)
    def _():
        run(xp_ref, rgp_ref, mp_ref)

    @pl.when(i >= n_p)
    def _():
        run(xs_ref, rgs_ref, ms_ref)


def _outproj(xp, xs, rgp, rgs, mp, ms, wo, g1, b1, wr, br):
    n_p, n_s = xp.shape[0] // TM, xs.shape[0] // TM
    n = n_p + n_s
    pmap = lambda i: (jnp.minimum(i, n_p - 1), 0)
    smap = lambda i: (jnp.maximum(i - n_p, 0), 0)
    const = lambda i: (0, 0)
    return pl.pallas_call(
        functools.partial(_outproj_kernel, n_p=n_p),
        grid=(n,),
        in_specs=[
            pl.BlockSpec((TM, D_MODEL), pmap), pl.BlockSpec((TM, D_MODEL), smap),
            pl.BlockSpec((TM, D_RG), pmap), pl.BlockSpec((TM, D_RG), smap),
            pl.BlockSpec((TM, D_M), pmap), pl.BlockSpec((TM, D_M), smap),
            pl.BlockSpec((D_RG + D_M, D_MODEL), const),
            pl.BlockSpec((1, D_MODEL), const), pl.BlockSpec((1, D_MODEL), const),
            pl.BlockSpec((D_MODEL, 2 * LANES), const),
            pl.BlockSpec((1, LANES), const),
        ],
        out_specs=[
            pl.BlockSpec((TM, X_ROW), lambda i: (i, 0)),
            pl.BlockSpec((1, LANES), const),
        ],
        out_shape=[
            jax.ShapeDtypeStruct((n * TM, X_ROW), f32),
            jax.ShapeDtypeStruct((1, LANES), f32),
        ],
        compiler_params=_cparams(1),
        name="outproj",
    )(xp, xs, rgp, rgs, mp, ms, wo, g1, b1, wr, br)


def _row_copy(src_ref, src_row, dst_ref, dst_row, sem):
    return pltpu.make_async_copy(src_ref.at[pl.ds(src_row, 1)], dst_ref.at[pl.ds(dst_row, 1)], sem)


def _tile_rows_wait(src_ref, dst_ref, sem):
    pltpu.make_async_copy(src_ref.at[pl.ds(0, TM)], dst_ref.at[pl.ds(0, TM)], sem).wait()


def _dispatch_kernel(pad_ref, pos_ref, x1_ref, xs_ref, zero_scr, xbuf, sem, zsem, lsem):
    i = pl.program_id(0)
    n = pl.num_programs(0)

    def zero_tile(t):
        return pltpu.make_async_copy(zero_scr, xs_ref.at[pl.ds(pl.multiple_of(t * TM_MOE, TM_MOE), TM_MOE)], zsem)

    def zero_chunk(row0):
        return pltpu.make_async_copy(zero_scr.at[pl.ds(0, SUBLANES)],
                                     xs_ref.at[pl.ds(pl.multiple_of(row0, SUBLANES), SUBLANES)], zsem)

    @pl.when(i == 0)
    def _():
        zero_scr[...] = jnp.zeros_like(zero_scr)
        n_used = pad_ref[2 * N_CLASSES]
        n_tiles = xs_ref.shape[0] // TM_MOE
        for go in (lambda cp: cp.start(), lambda cp: cp.wait()):
            lax.fori_loop(n_used, n_tiles, lambda t, c: (go(zero_tile(t)), c)[1], 0)

            def per_class(cl, c):
                row0 = pad_ref[2 * cl]
                lax.fori_loop(0, pad_ref[2 * cl + 1],
                              lambda k, c2: (go(zero_chunk(row0 + k * SUBLANES)), c2)[1], 0)
                return c

            lax.fori_loop(0, N_CLASSES, per_class, 0)

    def load(t, s):
        return pltpu.make_async_copy(x1_ref.at[pl.ds(pl.multiple_of(t * TM, TM), TM)], xbuf.at[s], lsem.at[s])

    @pl.when(i == 0)
    def _():
        load(0, 0).start()

    @pl.when(i + 1 < n)
    def _():
        load(i + 1, (i + 1) % 3).start()

    slot = i % 3
    load(i, slot).wait()

    def start(r, c):
        _row_copy(xbuf.at[slot], r, xs_ref, pos_ref[0, 0, r], sem.at[slot]).start()
        return c

    lax.fori_loop(0, TM, start, 0, unroll=8)

    @pl.when(i > 0)
    def _():
        _tile_rows_wait(xbuf.at[(i + 2) % 3], xs_ref, sem.at[(i + 2) % 3])

    @pl.when(i == n - 1)
    def _():
        _tile_rows_wait(xbuf.at[slot], xs_ref, sem.at[slot])


def _dispatch(pad_plan, pos, x1, n_rows):
    n = x1.shape[0] // TM
    return pl.pallas_call(
        _dispatch_kernel,
        grid_spec=pltpu.PrefetchScalarGridSpec(
            num_scalar_prefetch=1,
            grid=(n,),
            in_specs=[
                pl.BlockSpec((1, 1, TM), lambda i, lt: (i, 0, 0), memory_space=pltpu.SMEM),
                pl.BlockSpec(memory_space=pl.ANY),
            ],
            out_specs=pl.BlockSpec(memory_space=pl.ANY),
            scratch_shapes=[pltpu.VMEM((TM_MOE, X_ROW), f32), pltpu.VMEM((3, TM, X_ROW), f32),
                            pltpu.SemaphoreType.DMA((3,)), pltpu.SemaphoreType.DMA(()),
                            pltpu.SemaphoreType.DMA((3,))],
        ),
        out_shape=jax.ShapeDtypeStruct((n_rows, X_ROW), f32),
        compiler_params=_cparams(1),
        name="dispatch",
    )(pad_plan, pos, x1)


def _expert_kernel(ta_ref, tb_ref, ca_ref, cb_ref, sa_ref, sb_ref, nu_ref,
                   x_ref, wg_hbm, wu_hbm, wd_hbm, g2_ref, b2_ref, y_ref,
                   wga, wua, wda, wgb, wub, wdb, sem_a, sem_b):
    i = pl.program_id(0)
    n_used = nu_ref[0]
    first = i * MOE_TILES
    side_a = (ta_ref, ca_ref, sa_ref, (wga, wua, wda), sem_a)
    side_b = (tb_ref, cb_ref, sb_ref, (wgb, wub, wdb), sem_b)

    def copies(side, j):
        t_ref, _, s_ref, bufs, sem = side
        e, s = t_ref[j], s_ref[j]
        return [pltpu.make_async_copy(w.at[e], buf.at[s], sem.at[s])
                for w, buf in zip((wg_hbm, wu_hbm, wd_hbm), bufs)]

    def fetch(j):
        @pl.when(j < n_used)
        def _():
            for side in (side_a, side_b):
                @pl.when(side[1][j] == 1)
                def _():
                    for cp in copies(side, j):
                        cp.start()

    @pl.when(i == 0)
    def _():
        for j in range(W_AHEAD):
            fetch(j)

    for t in range(MOE_TILES):
        fetch(first + W_AHEAD + t)

    @pl.when(first < n_used)
    def _():
        for t in range(MOE_TILES):
            for side in (side_a, side_b):
                @pl.when(side[1][first + t] == 1)
                def _():
                    for cp in copies(side, first + t):
                        cp.wait()
        rows = [slice(t * TM_MOE, (t + 1) * TM_MOE) for t in range(MOE_TILES)]
        x = [x_ref[r, 0:D_MODEL] for r in rows]
        info = [x_ref[r, D_MODEL:X_ROW] for r in rows]
        xb = [v.astype(bf16) for v in x]
        units = [(t, bufs, s_ref[first + t]) for t in range(MOE_TILES)
                 for bufs, s_ref in (((wga, wua, wda), sa_ref), ((wgb, wub, wdb), sb_ref))]
        hg = [jnp.dot(xb[t], bufs[0][s].astype(bf16), preferred_element_type=f32) for t, bufs, s in units]
        hu = [jnp.dot(xb[t], bufs[1][s].astype(bf16), preferred_element_type=f32) for t, bufs, s in units]
        mid = [(g * jax.nn.sigmoid(g) * u).astype(bf16) for g, u in zip(hg, hu)]
        ys = [jnp.dot(m, bufs[2][s].astype(bf16), preferred_element_type=f32)
              for m, (t, bufs, s) in zip(mid, units)]
        for t in range(MOE_TILES):
            ffn = info[t][:, 2:3] * ys[2 * t] + info[t][:, 3:4] * ys[2 * t + 1]
            y_ref[rows[t], :] = _layer_norm(ALPHA * x[t] + ffn, g2_ref[...], b2_ref[...])

    @pl.when(first >= n_used)
    def _():
        y_ref[...] = jnp.zeros_like(y_ref)


def _experts(tile_ea, tile_eb, n_used, xs, wg, wu, wd, g2, b2):
    nt = xs.shape[0] // TM_MOE

    def ring_plan(tile_e):
        opens = jnp.concatenate([jnp.ones((1,), jnp.int32), (tile_e[1:] != tile_e[:-1]).astype(jnp.int32)])
        return opens, (jnp.cumsum(opens) - 1) % W_SLOTS

    open_a, slot_a = ring_plan(tile_ea)
    open_b, slot_b = ring_plan(tile_eb)
    assert nt % MOE_TILES == 0
    n_used = ((n_used + MOE_TILES - 1) // MOE_TILES) * MOE_TILES
    const = lambda i, *_: (0, 0)
    xmap = lambda i, ta, tb, ca, cb, sa, sb, nu: (jnp.minimum(i, nu[0] // MOE_TILES - 1), 0)
    w_in = pltpu.VMEM((W_SLOTS, D_MODEL, D_EXPERT), f32)
    w_out = pltpu.VMEM((W_SLOTS, D_EXPERT, D_MODEL), f32)
    return pl.pallas_call(
        _expert_kernel,
        grid_spec=pltpu.PrefetchScalarGridSpec(
            num_scalar_prefetch=7,
            grid=(nt // MOE_TILES,),
            in_specs=[
                pl.BlockSpec((MOE_TILES * TM_MOE, X_ROW), xmap),
                pl.BlockSpec(memory_space=pl.ANY), pl.BlockSpec(memory_space=pl.ANY),
                pl.BlockSpec(memory_space=pl.ANY),
                pl.BlockSpec((1, D_MODEL), const), pl.BlockSpec((1, D_MODEL), const),
            ],
            out_specs=pl.BlockSpec((MOE_TILES * TM_MOE, D_MODEL), lambda i, *_: (i, 0)),
            scratch_shapes=[w_in, w_in, w_out, w_in, w_in, w_out,
                            pltpu.SemaphoreType.DMA((W_SLOTS,)), pltpu.SemaphoreType.DMA((W_SLOTS,))],
        ),
        out_shape=jax.ShapeDtypeStruct((nt * TM_MOE, D_MODEL), f32),
        compiler_params=_cparams(1),
        name="experts",
    )(tile_ea, tile_eb, open_a, open_b, slot_a.astype(jnp.int32), slot_b.astype(jnp.int32), n_used,
      xs, wg, wu, wd, g2, b2)


def _collect_kernel(pos_ref, posn_ref, ys_ref, op_ref, os_ref, ybuf, sem, *, n_p):
    i = pl.program_id(0)
    n = pl.num_programs(0)
    slot = i % 2

    def gather(p_ref, s):
        def start(r, c):
            _row_copy(ys_ref, p_ref[0, 0, r], ybuf.at[s], r, sem.at[s]).start()
            return c

        lax.fori_loop(0, TM, start, 0, unroll=8)

    @pl.when(i == 0)
    def _():
        gather(pos_ref, 0)

    @pl.when(i + 1 < n)
    def _():
        gather(posn_ref, 1 - slot)

    _tile_rows_wait(ys_ref, ybuf.at[slot], sem.at[slot])

    @pl.when(i < n_p)
    def _():
        op_ref[...] = ybuf[slot]

    @pl.when(i >= n_p)
    def _():
        os_ref[...] = ybuf[slot]


def _collect(pos, ys, n_p):
    n = pos.shape[0]
    n_s = n - n_p
    return pl.pallas_call(
        functools.partial(_collect_kernel, n_p=n_p),
        grid=(n,),
        in_specs=[
            pl.BlockSpec((1, 1, TM), lambda i: (i, 0, 0), memory_space=pltpu.SMEM),
            pl.BlockSpec((1, 1, TM), lambda i: (jnp.minimum(i + 1, n - 1), 0, 0), memory_space=pltpu.SMEM),
            pl.BlockSpec(memory_space=pl.ANY),
        ],
        out_specs=[
            pl.BlockSpec((TM, D_MODEL), lambda i: (jnp.minimum(i, n_p - 1), 0)),
            pl.BlockSpec((TM, D_MODEL), lambda i: (jnp.maximum(i - n_p, 0), 0)),
        ],
        out_shape=[
            jax.ShapeDtypeStruct((n_p * TM, D_MODEL), f32),
            jax.ShapeDtypeStruct((n_s * TM, D_MODEL), f32),
        ],
        scratch_shapes=[pltpu.VMEM((2, TM, D_MODEL), f32), pltpu.SemaphoreType.DMA((2,))],
        compiler_params=_cparams(1),
        name="collect",
    )(pos, pos, ys)


def _block_diag(w):
    n, d, _ = w.shape
    eye = jnp.eye(n, dtype=w.dtype)
    return (eye[:, None, :, None] * w[:, :, None, :]).reshape(n * d, n * d)


def _pair_walk(n):
    total = n * (n - 1) // 2

    def extend(path, used):
        if len(path) == total:
            return path
        a, b = path[-1]
        for c in range(n):
            for nxt in ((a, c), (c, b)):
                key = frozenset(nxt)
                if len(key) == 2 and key not in used:
                    out = extend(path + [nxt], used | {key})
                    if out:
                        return out
        return None

    return extend([(0, 1)], {frozenset((0, 1))})


def _pad_history(buf):
    return jnp.pad(buf, ((0, 0), (SUBLANES - (CONV_W - 1), 0), (0, 0)))


def kernel(x_prompt, x_sample, state_rg_h, state_rg_conv, state_m_C, state_m_n, state_m_m, state_m_conv, w_in, rg_conv_w, rg_conv_b, rg_w_a, rg_b_a, rg_w_x, rg_b_x, rg_lambda, m_conv_w, m_conv_b, m_w_q, m_w_k, m_w_v, m_b_i, m_b_f, m_norm_g, m_skip, w_out, ln1_g, ln1_b, ln2_g, ln2_b, moe_w_group, moe_b_group, moe_w_expert, moe_b_expert, moe_w_gate, moe_w_up, moe_w_down):
    BP, TP, _ = x_prompt.shape
    BS, TS, _ = x_sample.shape
    n_prompt, n_sample = BP * TP, BS * TS
    xp = x_prompt.reshape(n_prompt, D_MODEL)
    xs = x_sample.reshape(n_sample, D_MODEL)
    l = 0

    w_in_p = jnp.pad(w_in[l], ((0, 0), (0, N_PROJ - w_in.shape[-1]))).astype(bf16)
    wa = _block_diag(rg_w_a[l]).astype(bf16)
    wx = _block_diag(rg_w_x[l]).astype(bf16)
    row = lambda v: v.reshape(1, -1)
    bif = jnp.pad(jnp.concatenate([m_b_i[l], m_b_f[l]]), (0, LANES - 2 * M_HEADS)).reshape(1, LANES)
    wqk = jnp.concatenate([m_w_q[l], m_w_k[l]], axis=-1).astype(bf16)
    wv = m_w_v[l].astype(bf16)
    wkT = m_w_k[l].transpose(0, 2, 1).astype(bf16)
    w_route = jnp.pad(jnp.concatenate([moe_w_group[l], moe_w_expert[l]], axis=1),
                      ((0, 0), (0, LANES - N_GROUPS - N_EXPERTS)))
    wr_hi = w_route.astype(bf16)
    wr_lo = (w_route - wr_hi.astype(f32)).astype(bf16)
    wr = jnp.concatenate([wr_hi, wr_lo], axis=1)
    b_route = jnp.pad(jnp.concatenate([moe_b_group[l], moe_b_expert[l]]),
                      (0, LANES - N_GROUPS - N_EXPERTS)).reshape(1, LANES)

    proj_p, proj_s = _inproj(xp, xs, w_in_p)

    rg_args = (rg_conv_w[l], row(rg_conv_b[l]), wa, row(rg_b_a[l]), wx, row(rg_b_x[l]), row(rg_lambda[l]))
    zeros = lambda *s: jnp.zeros(s, f32)
    s_blk = M_ROWS // TS
    yrg_p, p_rg_h, p_rg_conv = _rglru(proj_p, 0, BP, TP, 1, RG_ROWS_PROMPT, zeros(BP, SUBLANES, D_RG),
                                      zeros(BP, 1, D_RG), *rg_args)
    yrg_s, s_rg_h, s_rg_conv = _rglru(proj_s, 0, BS, TS, s_blk, TS, _pad_history(state_rg_conv[l]),
                                      state_rg_h[l].reshape(BS, 1, D_RG), *rg_args)

    m_args = (m_conv_w[l], row(m_conv_b[l]), wqk, wv, wkT, bif, row(m_norm_g[l]), row(m_skip[l]))
    m0_p = jnp.full((SUBLANES, BP * LANES), M_INIT, f32)
    ym_p, p_m_C, p_m_n, p_m_m, p_m_conv = _mlstm(
        proj_p, BP, TP, 1, M_ROWS, M_BLOCKS_PROMPT, zeros(BP, SUBLANES, D_M), zeros(BP, M_HEADS, M_DH, M_DH),
        zeros(BP, M_HEADS, 1, M_DH), m0_p, *m_args)
    m0_s = jnp.pad(jnp.repeat(state_m_m[l].T, TS, axis=1), ((0, SUBLANES - M_HEADS), (0, 0)))
    ym_s, s_m_C, s_m_n, s_m_m, s_m_conv = _mlstm(
        proj_s, BS, TS, s_blk, TS, M_BLOCKS_SAMPLE, _pad_history(state_m_conv[l]), state_m_C[l],
        state_m_n[l].reshape(BS, M_HEADS, 1, M_DH), m0_s, *m_args)

    x1, cnt = _outproj(xp, xs, yrg_p, yrg_s, ym_p, ym_s, w_out[l].astype(bf16),
                       row(ln1_g[l]), row(ln1_b[l]), wr, b_route)

    n_tok = n_prompt + n_sample
    n_tiles = n_tok // TM_MOE + N_CLASSES
    lay_cls, lay_a, lay_b = [], [], []
    for g in range(N_GROUPS):
        for ja, jb in _pair_walk(EXPERTS_PER_GROUP):
            assert ja < jb
            n_first = (EXPERTS_PER_GROUP - 1) * ja - ja * (ja - 1) // 2
            lay_cls.append(PAIRS_PER_GROUP * g + n_first + jb - ja - 1)
            lay_a.append(g * EXPERTS_PER_GROUP + ja)
            lay_b.append(g * EXPERTS_PER_GROUP + jb)
    classes = jnp.arange(N_CLASSES, dtype=jnp.int32)
    in_slot = jnp.array(lay_cls, jnp.int32)[:, None] == classes
    counts = jnp.sum(jnp.where(in_slot, cnt[0, :N_CLASSES].astype(jnp.int32), 0), axis=1)
    padded = ((counts + TM_MOE - 1) // TM_MOE) * TM_MOE
    ends = jnp.cumsum(padded)
    offs = ends - padded
    offs_of_cls = jnp.sum(jnp.where(in_slot, offs[:, None], 0), axis=0)
    cls = x1[:, D_MODEL].astype(jnp.int32)
    rank = x1[:, D_MODEL + 1].astype(jnp.int32)
    pos = jnp.sum(jnp.where(cls[:, None] == classes, offs_of_cls, 0), axis=-1) + rank
    pos = pos.reshape(n_tok // TM, 1, TM)
    tiles = jnp.arange(n_tiles, dtype=jnp.int32)
    tile_slot = jnp.minimum(jnp.sum(tiles[:, None] >= (ends // TM_MOE)[None, :], axis=1), N_CLASSES - 1)
    on_slot = tile_slot[:, None] == classes
    tile_ea = jnp.sum(jnp.where(on_slot, jnp.array(lay_a, jnp.int32), 0), axis=-1).astype(jnp.int32)
    tile_eb = jnp.sum(jnp.where(on_slot, jnp.array(lay_b, jnp.int32), 0), axis=-1).astype(jnp.int32)
    n_used = (ends[-1] // TM_MOE).reshape(1).astype(jnp.int32)
    pad_row0 = ((offs + counts) // SUBLANES) * SUBLANES
    pad_plan = jnp.concatenate([jnp.stack([pad_row0, (ends - pad_row0) // SUBLANES], axis=1).reshape(-1),
                                     n_used]).astype(jnp.int32)

    x_sorted = _dispatch(pad_plan, pos, x1, n_tiles * TM_MOE)
    y_sorted = _experts(tile_ea, tile_eb, n_used, x_sorted, moe_w_gate[l], moe_w_up[l], moe_w_down[l],
                        row(ln2_g[l]), row(ln2_b[l]))
    y_p, y_s = _collect(pos, y_sorted, n_prompt // TM)

    return (y_p.reshape(BP, TP, D_MODEL), y_s.reshape(BS, TS, D_MODEL),
            p_rg_h.reshape(1, BP, D_RG), p_rg_conv[None], p_m_C[None], p_m_n.reshape(1, BP, M_HEADS, M_DH),
            p_m_m[None], p_m_conv[None],
            s_rg_h.reshape(1, BS, D_RG), s_rg_conv[None], s_m_C[None], s_m_n.reshape(1, BS, M_HEADS, M_DH),
            s_m_m[None], s_m_conv[None])
```

```python
import functools

import jax
import jax.numpy as jnp
from jax import lax
from jax.experimental import pallas as pl
from jax.experimental.pallas import tpu as pltpu

f32 = jnp.float32
bf16 = jnp.bfloat16

D_MODEL = 1024
D_RG = 512
RG_BLOCKS = 8
RG_C = 8.0
D_M = 512
M_HEADS = 4
M_DH = 128
CONV_W = 4
N_GROUPS = 4
EXPERTS_PER_GROUP = 8
N_EXPERTS = 32
D_EXPERT = 256
ALPHA = 2.0 ** 0.25
LN_EPS = 1e-5
M_INIT = -1.0e4

LANES = 128
SUBLANES = 8
TM = 512
TM_MOE = 128
MOE_TILES = 2
W_AHEAD = 4
W_SLOTS = W_AHEAD + MOE_TILES
PAIRS_PER_GROUP = EXPERTS_PER_GROUP * (EXPERTS_PER_GROUP - 1) // 2
N_CLASSES = N_GROUPS * PAIRS_PER_GROUP
X_ROW = D_MODEL + LANES
N_PROJ = 2 * D_RG + 2 * D_M + LANES
RG_ROWS_PROMPT = 512
M_ROWS = 128
M_BLOCKS_PROMPT = 8
M_BLOCKS_SAMPLE = 2
VMEM_LIMIT = 56 * 1024 * 1024


def _cparams(n_axes):
    return pltpu.CompilerParams(dimension_semantics=("arbitrary",) * n_axes,
                                vmem_limit_bytes=VMEM_LIMIT)


def _inproj_kernel(xp_ref, xs_ref, w_ref, op_ref, os_ref, *, n_p):
    i = pl.program_id(0)

    def run(x_ref, o_ref):
        o_ref[...] = jnp.dot(x_ref[...].astype(bf16), w_ref[...], preferred_element_type=f32)

    @pl.when(i < n_p)
    def _():
        run(xp_ref, op_ref)

    @pl.when(i >= n_p)
    def _():
        run(xs_ref, os_ref)


def _inproj(xp, xs, w):
    n_p, n_s = xp.shape[0] // TM, xs.shape[0] // TM
    pmap = lambda i: (jnp.minimum(i, n_p - 1), 0)
    smap = lambda i: (jnp.maximum(i - n_p, 0), 0)
    return pl.pallas_call(
        functools.partial(_inproj_kernel, n_p=n_p),
        grid=(n_p + n_s,),
        in_specs=[
            pl.BlockSpec((TM, D_MODEL), pmap),
            pl.BlockSpec((TM, D_MODEL), smap),
            pl.BlockSpec((D_MODEL, N_PROJ), lambda i: (0, 0)),
        ],
        out_specs=[pl.BlockSpec((TM, N_PROJ), pmap), pl.BlockSpec((TM, N_PROJ), smap)],
        out_shape=[jax.ShapeDtypeStruct((n_p * TM, N_PROJ), f32),
                   jax.ShapeDtypeStruct((n_s * TM, N_PROJ), f32)],
        compiler_params=_cparams(1),
        name="inproj",
    )(xp, xs, w)


def _causal_conv(tail_scr, x, cw_ref, cb_ref, S, L, C):
    tail = tail_scr[...]
    sub = lax.broadcasted_iota(jnp.int32, (S, SUBLANES, C), 1)
    acc = cb_ref[...] + cw_ref[CONV_W - 1:CONV_W, :] * x
    for d in range(1, CONV_W):
        back = pltpu.roll(x, d, 0).reshape(S, L, C)
        head = jnp.where(sub < d, pltpu.roll(tail, d, 1), back[:, 0:SUBLANES, :])
        if L > SUBLANES:
            back = jnp.concatenate([head, back[:, SUBLANES:, :]], axis=1)
        else:
            back = head
        acc = acc + cw_ref[CONV_W - 1 - d:CONV_W - d, :] * back.reshape(S * L, C)
    tail_scr[...] = x.reshape(S, L, C)[:, L - SUBLANES:, :]
    return acc


def _softplus(x):
    return jnp.maximum(x, 0.0) + jnp.log1p(jnp.exp(-jnp.abs(x)))


def _rglru_kernel(x_ref, g_ref, buf_ref, h0_ref, cw_ref, cb_ref, wa_ref, ba_ref, wx_ref, bx_ref,
                  lam_ref, y_ref, hN_ref, cN_ref, xs_scr, *, S, L):
    R = S * L
    t = pl.program_id(1)

    @pl.when(t == 0)
    def _():
        xs_scr[...] = buf_ref[...]
        hN_ref[...] = h0_ref[...]

    x = x_ref[...]
    xc = _causal_conv(xs_scr, x, cw_ref, cb_ref, S, L, D_RG)
    xcb = xc.astype(bf16)
    r = jax.nn.sigmoid(jnp.dot(xcb, wa_ref[...], preferred_element_type=f32) + ba_ref[...])
    ig = jax.nn.sigmoid(jnp.dot(xcb, wx_ref[...], preferred_element_type=f32) + bx_ref[...])
    log_a = (-RG_C) * r * _softplus(-lam_ref[...])
    a = jnp.exp(log_a)
    th = jnp.tanh(log_a)
    u = jnp.sqrt(-2.0 * th / (1.0 - th)) * ig * xc

    n_grp, grp_per_seq = R // SUBLANES, L // SUBLANES
    a3 = a.reshape(n_grp, SUBLANES, D_RG)
    u3 = u.reshape(n_grp, SUBLANES, D_RG)
    sub = lax.broadcasted_iota(jnp.int32, (n_grp, SUBLANES, D_RG), 1)
    s = 1
    while s < SUBLANES:
        ok = sub >= s
        a_sh = pltpu.roll(a3, s, 1)
        u_sh = pltpu.roll(u3, s, 1)
        u3 = jnp.where(ok, a3 * u_sh + u3, u3)
        a3 = jnp.where(ok, a3 * a_sh, a3)
        s *= 2
    h0 = hN_ref[...]
    groups = []
    for kg in range(n_grp):
        carry = h0[kg // grp_per_seq] if kg % grp_per_seq == 0 else groups[-1][SUBLANES - 1:SUBLANES, :]
        groups.append(a3[kg] * carry + u3[kg])
    h = jnp.concatenate(groups, axis=0)

    y_ref[...] = h * jax.nn.gelu(g_ref[...], approximate=True)
    hN_ref[...] = h.reshape(S, L, D_RG)[:, L - 1:L, :]
    cN_ref[...] = xs_scr[:, SUBLANES - (CONV_W - 1):SUBLANES, :]


def _rglru(proj, row0, B, T, S, L, buf8, h0, cw, cb, wa, ba, wx, bx, lam):
    R = S * L
    nb, nt = B // S, T // L
    blk0 = row0 // R
    row_map = lambda b, t: (blk0 + b * nt + t, 0)
    const2 = lambda b, t: (0, 0)
    return pl.pallas_call(
        functools.partial(_rglru_kernel, S=S, L=L),
        grid=(nb, nt),
        in_specs=[
            pl.BlockSpec((R, D_RG), row_map),
            pl.BlockSpec((R, D_RG), lambda b, t: (blk0 + b * nt + t, 1)),
            pl.BlockSpec((S, SUBLANES, D_RG), lambda b, t: (b, 0, 0)),
            pl.BlockSpec((S, 1, D_RG), lambda b, t: (b, 0, 0)),
            pl.BlockSpec((CONV_W, D_RG), const2),
            pl.BlockSpec((1, D_RG), const2),
            pl.BlockSpec((D_RG, D_RG), const2),
            pl.BlockSpec((1, D_RG), const2),
            pl.BlockSpec((D_RG, D_RG), const2),
            pl.BlockSpec((1, D_RG), const2),
            pl.BlockSpec((1, D_RG), const2),
        ],
        out_specs=[
            pl.BlockSpec((R, D_RG), lambda b, t: (b * nt + t, 0)),
            pl.BlockSpec((S, 1, D_RG), lambda b, t: (b, 0, 0)),
            pl.BlockSpec((S, CONV_W - 1, D_RG), lambda b, t: (b, 0, 0)),
        ],
        out_shape=[
            jax.ShapeDtypeStruct((B * T, D_RG), f32),
            jax.ShapeDtypeStruct((B, 1, D_RG), f32),
            jax.ShapeDtypeStruct((B, CONV_W - 1, D_RG), f32),
        ],
        scratch_shapes=[pltpu.VMEM((S, SUBLANES, D_RG), f32)],
        compiler_params=_cparams(2),
        name="rglru",
    )(proj, proj, buf8, h0, cw, cb, wa, ba, wx, bx, lam)


def _seg_scan(x, op, fill, tin, L, reverse=False):
    s = 1
    while s < L:
        if reverse:
            sh = pltpu.roll(x, LANES - s, 1)
            ok = tin < L - s
        else:
            sh = pltpu.roll(x, s, 1)
            ok = tin >= s
        x = op(x, jnp.where(ok, sh, fill))
        s *= 2
    return x


def _mlstm_kernel(x_ref, z_ref, gt_ref, buf_ref, c0_ref, n0_ref, m0_ref,
                  cw_ref, cb_ref, wqk_ref, wv_ref, wkT_ref, bif_ref, ng_ref, sk_ref,
                  y_ref, cN_ref, nN_ref, mN_ref, bN_ref, xs_scr, m_scr, *, S, L, G):
    R = S * L
    GS, GR = G * S, G * R
    j = pl.program_id(1)

    @pl.when(j == 0)
    def _():
        xs_scr[...] = buf_ref[...]
        cN_ref[...] = c0_ref[...]
        nN_ref[...] = n0_ref[...]
        m_scr[...] = m0_ref[...]

    x = x_ref[...].reshape(GR, D_M)
    xc = _causal_conv(xs_scr, x, cw_ref, cb_ref, GS, L, D_M)
    xa = xc * jax.nn.sigmoid(xc)
    xab = xa.astype(bf16)
    xb = x.astype(bf16)

    il_parts, f_parts = [], []
    for g in range(G):
        gT = (gt_ref[g] + bif_ref[...]).T
        il_parts.append(gT[0:SUBLANES, :])
        f_parts.append(pltpu.roll(gT[0:SUBLANES, :], M_HEADS, 0))
    il = jnp.concatenate(il_parts, axis=0)
    fl = -_softplus(-jnp.concatenate(f_parts, axis=0))
    tin = lax.broadcasted_iota(jnp.int32, (G * SUBLANES, LANES), 1) % L
    bcum = _seg_scan(fl, jnp.add, 0.0, tin, L)
    a = il - bcum
    m_prev = m_scr[...].reshape(G * SUBLANES, LANES)
    big_m = jnp.maximum(m_prev, _seg_scan(a, jnp.maximum, -jnp.inf, tin, L))
    m_t = bcum + big_m
    if S == 1:
        m_last = jnp.broadcast_to(big_m[:, LANES - 1:LANES], big_m.shape)
    else:
        m_last = _seg_scan(big_m, jnp.maximum, -jnp.inf, tin, L, reverse=True)
    rows = [big_m, jnp.exp(m_prev - big_m), jnp.exp(-m_t), jnp.exp(a - m_last), jnp.exp(m_prev - m_last)]
    mN_ref[...] = m_t.reshape(G, SUBLANES, LANES)
    m_scr[...] = jnp.broadcast_to(m_t[:, LANES - 1:LANES], m_t.shape).reshape(G, SUBLANES, LANES)
    pad_rows = jnp.zeros((LANES - len(rows) * SUBLANES, LANES), f32)
    cols = []
    for g in range(G):
        gs = slice(g * SUBLANES, (g + 1) * SUBLANES)
        cols.append(jnp.concatenate([r[gs] for r in rows] + [pad_rows], axis=0).T)

    def col(g, q, h):
        return cols[g][:, SUBLANES * q + h:SUBLANES * q + h + 1]

    qs, ks, vs, kTs = [], [], [], []
    for h in range(M_HEADS):
        hs = slice(h * M_DH, (h + 1) * M_DH)
        qk_h = jnp.dot(xab[:, hs], wqk_ref[h], preferred_element_type=f32)
        qs.append(qk_h[:, 0:M_DH])
        ks.append(qk_h[:, M_DH:2 * M_DH] * (M_DH ** -0.5))
        vs.append(jnp.dot(xb[:, hs], wv_ref[h], preferred_element_type=f32))
        if S == 1:
            kT_h = lax.dot_general(wkT_ref[h], xab[:, hs], (((1,), (1,)), ((), ())),
                                   preferred_element_type=f32)
            kTs.append((kT_h * (M_DH ** -0.5)).astype(bf16))

    ti = lax.broadcasted_iota(jnp.int32, (R, R), 0)
    si = lax.broadcasted_iota(jnp.int32, (R, R), 1)
    mask = (si <= ti) & ((ti // L) == (si // L))
    ones_b = jnp.ones((R, M_DH), bf16)
    pairs = [(g, h) for g in range(G) for h in range(M_HEADS)]
    blk = lambda g: slice(g * R, (g + 1) * R)
    seqs = [slice(b * L, (b + 1) * L) for b in range(S)]
    q = {(g, h): qs[h][blk(g)] for g, h in pairs}
    k = {(g, h): ks[h][blk(g)] for g, h in pairs}
    v = {(g, h): vs[h][blk(g)] for g, h in pairs}
    qb = {p: q[p].astype(bf16) for p in pairs}
    kb = {p: k[p].astype(bf16) for p in pairs}
    qk = {p: lax.dot_general(qb[p], kb[p], (((1,), (1,)), ((), ())), preferred_element_type=f32)
          for p in pairs}
    sm = {}
    for g, h in pairs:
        a_row = a[g * SUBLANES + h:g * SUBLANES + h + 1, :]
        decay = jnp.exp(jnp.where(mask, a_row - col(g, 0, h), -jnp.inf))
        sm[g, h] = (qk[g, h] * decay).astype(bf16)
    nd = {p: jnp.dot(sm[p], jnp.concatenate([v[p].astype(bf16), ones_b], axis=1),
                     preferred_element_type=f32) for p in pairs}
    c_old = {(g, h, b): cN_ref[g * S + b, h] for g, h in pairs for b in range(S)}
    n_old = {(g, h, b): nN_ref[g * S + b, h] for g, h in pairs for b in range(S)}
    q_c = {(g, h, b): jnp.dot(qb[g, h][seqs[b]], c_old[g, h, b].astype(bf16), preferred_element_type=f32)
           for g, h in pairs for b in range(S)}
    hh = {}
    for g, h in pairs:
        g_col, e_col = col(g, 1, h), col(g, 2, h)
        pieces = []
        for b, rs in enumerate(seqs):
            q_n = jnp.sum(q[g, h][rs] * n_old[g, h, b], axis=1, keepdims=True)
            num = nd[g, h][rs, 0:M_DH] + g_col[rs] * q_c[g, h, b]
            den = nd[g, h][rs, M_DH:2 * M_DH] + g_col[rs] * q_n
            pieces.append(num / jnp.maximum(jnp.abs(den), e_col[rs]))
        hh[g, h] = pieces[0] if S == 1 else jnp.concatenate(pieces, axis=0)
    mu = {p: jnp.mean(hh[p], axis=1, keepdims=True) for p in pairs}
    var = {p: jnp.mean(jnp.square(hh[p] - mu[p]), axis=1, keepdims=True) for p in pairs}
    hn_blocks = [jnp.concatenate([(hh[g, h] - mu[g, h]) * lax.rsqrt(var[g, h] + LN_EPS)
                                  for h in range(M_HEADS)], axis=1) for g in range(G)]
    new_c, new_n = [], []
    for g, h in pairs:
        w_col = col(g, 3, h)
        wv = (w_col * v[g, h]).astype(bf16)
        wk = w_col * k[g, h]
        for b, rs in enumerate(seqs):
            g_end = col(g, 4, h)[(b + 1) * L - 1:(b + 1) * L, :]
            if S == 1:
                kv = jnp.dot(kTs[h][:, blk(g)], wv, preferred_element_type=f32)
            else:
                kv = lax.dot_general(kb[g, h][rs], wv[rs], (((0,), (0,)), ((), ())),
                                     preferred_element_type=f32)
            new_c.append((g * S + b, h, g_end * c_old[g, h, b] + kv))
            new_n.append((g * S + b, h, g_end * n_old[g, h, b] + jnp.sum(wk[rs], axis=0, keepdims=True)))

    hn = jnp.concatenate(hn_blocks, axis=0) * ng_ref[...]
    y = jax.nn.sigmoid(z_ref[...].reshape(GR, D_M)) * (hn + sk_ref[...] * xa)
    y_ref[...] = y.reshape(G, R, D_M)
    for sq, h, val in new_c:
        cN_ref[sq, h] = val
    for sq, h, val in new_n:
        nN_ref[sq, h] = val
    bN_ref[...] = xs_scr[:, SUBLANES - (CONV_W - 1):SUBLANES, :]


def _mlstm(proj, B, T, S, L, G, buf8, c0, n0, m0_lanes, cw, cb, wqk, wv, wkT, bif, ng, sk):
    R = S * L
    assert R == M_ROWS == LANES
    nblk, nc = B // S, T // L
    nb = nblk // G
    assert nblk * S == B and nc * L == T and nb * G == nblk and (S == 1 or nc == 1)
    proj4 = proj.reshape(nblk, nc, R, N_PROJ)
    m0_3 = m0_lanes.reshape(SUBLANES, nblk, LANES).transpose(1, 0, 2)
    const2 = lambda b, j: (0, 0)
    const3 = lambda b, j: (0, 0, 0)
    gate_blk = (2 * D_RG + 2 * D_M) // LANES
    GS = G * S
    outs = pl.pallas_call(
        functools.partial(_mlstm_kernel, S=S, L=L, G=G),
        grid=(nb, nc),
        in_specs=[
            pl.BlockSpec((G, None, R, D_M), lambda b, j: (b, j, 0, 2)),
            pl.BlockSpec((G, None, R, D_M), lambda b, j: (b, j, 0, 3)),
            pl.BlockSpec((G, None, R, LANES), lambda b, j: (b, j, 0, gate_blk)),
            pl.BlockSpec((GS, SUBLANES, D_M), lambda b, j: (b, 0, 0)),
            pl.BlockSpec((GS, M_HEADS, M_DH, M_DH), lambda b, j: (b, 0, 0, 0)),
            pl.BlockSpec((GS, M_HEADS, 1, M_DH), lambda b, j: (b, 0, 0, 0)),
            pl.BlockSpec((G, SUBLANES, LANES), lambda b, j: (b, 0, 0)),
            pl.BlockSpec((CONV_W, D_M), const2),
            pl.BlockSpec((1, D_M), const2),
            pl.BlockSpec((M_HEADS, M_DH, 2 * M_DH), const3),
            pl.BlockSpec((M_HEADS, M_DH, M_DH), const3),
            pl.BlockSpec((M_HEADS, M_DH, M_DH), const3),
            pl.BlockSpec((1, LANES), const2),
            pl.BlockSpec((1, D_M), const2),
            pl.BlockSpec((1, D_M), const2),
        ],
        out_specs=[
            pl.BlockSpec((G, None, R, D_M), lambda b, j: (b, j, 0, 0)),
            pl.BlockSpec((GS, M_HEADS, M_DH, M_DH), lambda b, j: (b, 0, 0, 0)),
            pl.BlockSpec((GS, M_HEADS, 1, M_DH), lambda b, j: (b, 0, 0, 0)),
            pl.BlockSpec((G, None, SUBLANES, LANES), lambda b, j: (b, j, 0, 0)),
            pl.BlockSpec((GS, CONV_W - 1, D_M), lambda b, j: (b, 0, 0)),
        ],
        out_shape=[
            jax.ShapeDtypeStruct((nblk, nc, R, D_M), f32),
            jax.ShapeDtypeStruct((B, M_HEADS, M_DH, M_DH), f32),
            jax.ShapeDtypeStruct((B, M_HEADS, 1, M_DH), f32),
            jax.ShapeDtypeStruct((nblk, nc, SUBLANES, LANES), f32),
            jax.ShapeDtypeStruct((B, CONV_W - 1, D_M), f32),
        ],
        scratch_shapes=[pltpu.VMEM((GS, SUBLANES, D_M), f32), pltpu.VMEM((G, SUBLANES, LANES), f32)],
        compiler_params=_cparams(2),
        name="mlstm",
    )(proj4, proj4, proj4, buf8, c0, n0, m0_3, cw, cb, wqk, wv, wkT, bif, ng, sk)
    y, c_new, n_new, m_t, b_new = outs
    m_last = m_t[:, nc - 1, :M_HEADS, :].reshape(nblk, M_HEADS, S, L)[:, :, :, L - 1]
    m_last = m_last.transpose(0, 2, 1).reshape(B, M_HEADS)
    return y.reshape(B * T, D_M), c_new, n_new, m_last, b_new


def _layer_norm(x, g, b):
    mu = jnp.mean(x, axis=-1, keepdims=True)
    var = jnp.mean(jnp.square(x - mu), axis=-1, keepdims=True)
    return (x - mu) * lax.rsqrt(var + LN_EPS) * g + b


def _first_lane_of_max(vals, lane_f):
    vmax = jnp.max(vals, axis=1, keepdims=True)
    idx = jnp.min(jnp.where(vals == vmax, lane_f, float(LANES)), axis=1, keepdims=True)
    return vmax, idx


def _outproj_kernel(xp_ref, xs_ref, rgp_ref, rgs_ref, mp_ref, ms_ref, wo_ref, g1_ref, b1_ref,
                    wr_ref, br_ref, x1_ref, meta_ref, cnt_ref, *, n_p):
    i = pl.program_id(0)

    @pl.when(i == 0)
    def _():
        cnt_ref[...] = jnp.zeros_like(cnt_ref)

    def run(x_ref, rg_ref, m_ref):
        heads = jnp.concatenate([rg_ref[...].astype(bf16), m_ref[...].astype(bf16)], axis=1)
        mix = jnp.dot(heads, wo_ref[...], preferred_element_type=f32)
        x1 = _layer_norm(ALPHA * x_ref[...] + mix, g1_ref[...], b1_ref[...])
        x1_ref[:, 0:D_MODEL] = x1

        hi = x1.astype(bf16)
        lo = (x1 - hi.astype(f32)).astype(bf16)
        hi_terms = jnp.dot(hi, wr_ref[...], preferred_element_type=f32)
        lo_term = jnp.dot(lo, wr_ref[:, 0:LANES], preferred_element_type=f32)
        lg = hi_terms[:, 0:LANES] + hi_terms[:, LANES:2 * LANES] + lo_term + br_ref[...]
        lane = lax.broadcasted_iota(jnp.int32, (TM, LANES), 1)
        lane_f = lane.astype(f32)
        neg = -jnp.inf
        gl = jnp.where(lane < N_GROUPS, lg, neg)
        gmax, gidx = _first_lane_of_max(gl, lane_f)
        p_g = 1.0 / jnp.sum(jnp.exp(gl - gmax), axis=1, keepdims=True)
        e_lo = float(N_GROUPS) + float(EXPERTS_PER_GROUP) * gidx
        el = jnp.where((lane_f >= e_lo) & (lane_f < e_lo + float(EXPERTS_PER_GROUP)), lg, neg)
        v1, i1 = _first_lane_of_max(el, lane_f)
        v2, i2 = _first_lane_of_max(jnp.where(lane_f == i1, neg, el), lane_f)
        d = jnp.exp(v2 - v1)
        w1 = p_g / (1.0 + d)
        w2 = p_g * d / (1.0 + d)

        first_low = i1 < i2
        j_lo = jnp.minimum(i1, i2) - e_lo
        j_hi = jnp.maximum(i1, i2) - e_lo
        w_lo = jnp.where(first_low, w1, w2)
        w_hi = jnp.where(first_low, w2, w1)
        n_first = float(EXPERTS_PER_GROUP - 1) * j_lo - 0.5 * j_lo * (j_lo - 1.0)
        cls = float(PAIRS_PER_GROUP) * gidx + n_first + (j_hi - j_lo - 1.0)

        hot = lane_f == cls
        ti = lax.broadcasted_iota(jnp.int32, (TM, TM), 0)
        si = lax.broadcasted_iota(jnp.int32, (TM, TM), 1)
        tri = (si <= ti).astype(bf16)
        cum = jnp.dot(tri, hot.astype(bf16), preferred_element_type=f32)
        rank = jnp.sum(jnp.where(hot, cum - 1.0 + cnt_ref[...], 0.0), axis=1, keepdims=True)
        cnt_ref[...] = cnt_ref[...] + cum[TM - 1:TM, :]

        info = jnp.zeros((TM, LANES), f32)
        for c, val in enumerate((cls, rank, w_lo, w_hi)):
            info = jnp.where(lane == c, val, info)
        x1_ref[:, D_MODEL:D_MODEL + LANES] = info
        meta_ref[0] = info.T[0:SUBLANES, :].astype(jnp.int32)

    @pl.when(i < n_p)
    def _():
        run(xp_ref, rgp_ref, mp_ref)

    @pl.when(i >= n_p)
    def _():
        run(xs_ref, rgs_ref, ms_ref)


def _outproj(xp, xs, rgp, rgs, mp, ms, wo, g1, b1, wr, br):
    n_p, n_s = xp.shape[0] // TM, xs.shape[0] // TM
    n = n_p + n_s
    pmap = lambda i: (jnp.minimum(i, n_p - 1), 0)
    smap = lambda i: (jnp.maximum(i - n_p, 0), 0)
    const = lambda i: (0, 0)
    return pl.pallas_call(
        functools.partial(_outproj_kernel, n_p=n_p),
        grid=(n,),
        in_specs=[
            pl.BlockSpec((TM, D_MODEL), pmap), pl.BlockSpec((TM, D_MODEL), smap),
            pl.BlockSpec((TM, D_RG), pmap), pl.BlockSpec((TM, D_RG), smap),
            pl.BlockSpec((TM, D_M), pmap), pl.BlockSpec((TM, D_M), smap),
            pl.BlockSpec((D_RG + D_M, D_MODEL), const),
            pl.BlockSpec((1, D_MODEL), const), pl.BlockSpec((1, D_MODEL), const),
            pl.BlockSpec((D_MODEL, 2 * LANES), const),
            pl.BlockSpec((1, LANES), const),
        ],
        out_specs=[
            pl.BlockSpec((TM, X_ROW), lambda i: (i, 0)),
            pl.BlockSpec((1, SUBLANES, TM), lambda i: (i, 0, 0)),
            pl.BlockSpec((1, LANES), const),
        ],
        out_shape=[
            jax.ShapeDtypeStruct((n * TM, X_ROW), f32),
            jax.ShapeDtypeStruct((n, SUBLANES, TM), jnp.int32),
            jax.ShapeDtypeStruct((1, LANES), f32),
        ],
        compiler_params=_cparams(1),
        name="outproj",
    )(xp, xs, rgp, rgs, mp, ms, wo, g1, b1, wr, br)


def _row_copy(src_ref, src_row, dst_ref, dst_row, sem):
    return pltpu.make_async_copy(src_ref.at[pl.ds(src_row, 1)], dst_ref.at[pl.ds(dst_row, 1)], sem)


def _tile_rows_wait(src_ref, dst_ref, sem):
    pltpu.make_async_copy(src_ref.at[pl.ds(0, TM)], dst_ref.at[pl.ds(0, TM)], sem).wait()


def _sorted_row(offs_ref, meta_ref, r):
    return offs_ref[meta_ref[0, 0, r]] + meta_ref[0, 1, r]


def _dispatch_kernel(pad_ref, offs_ref, meta_ref, x1_ref, xs_ref, zero_scr, xbuf, sem, zsem, lsem):
    i = pl.program_id(0)
    n = pl.num_programs(0)

    def zero_tile(t):
        return pltpu.make_async_copy(zero_scr, xs_ref.at[pl.ds(pl.multiple_of(t * TM_MOE, TM_MOE), TM_MOE)], zsem)

    def zero_chunk(row0):
        return pltpu.make_async_copy(zero_scr.at[pl.ds(0, SUBLANES)],
                                     xs_ref.at[pl.ds(pl.multiple_of(row0, SUBLANES), SUBLANES)], zsem)

    @pl.when(i == 0)
    def _():
        zero_scr[...] = jnp.zeros_like(zero_scr)
        n_used = pad_ref[2 * N_CLASSES]
        n_tiles = xs_ref.shape[0] // TM_MOE
        for go in (lambda cp: cp.start(), lambda cp: cp.wait()):
            lax.fori_loop(n_used, n_tiles, lambda t, c: (go(zero_tile(t)), c)[1], 0)

            def per_class(cl, c):
                row0 = pad_ref[2 * cl]
                lax.fori_loop(0, pad_ref[2 * cl + 1],
                              lambda k, c2: (go(zero_chunk(row0 + k * SUBLANES)), c2)[1], 0)
                return c

            lax.fori_loop(0, N_CLASSES, per_class, 0)

    def load(t, s):
        return pltpu.make_async_copy(x1_ref.at[pl.ds(pl.multiple_of(t * TM, TM), TM)], xbuf.at[s], lsem.at[s])

    @pl.when(i == 0)
    def _():
        load(0, 0).start()

    @pl.when(i + 1 < n)
    def _():
        load(i + 1, (i + 1) % 3).start()

    slot = i % 3
    load(i, slot).wait()

    def start(r, c):
        _row_copy(xbuf.at[slot], r, xs_ref, _sorted_row(offs_ref, meta_ref, r), sem.at[slot]).start()
        return c

    lax.fori_loop(0, TM, start, 0, unroll=8)

    @pl.when(i > 0)
    def _():
        _tile_rows_wait(xbuf.at[(i + 2) % 3], xs_ref, sem.at[(i + 2) % 3])

    @pl.when(i == n - 1)
    def _():
        _tile_rows_wait(xbuf.at[slot], xs_ref, sem.at[slot])


def _dispatch(pad_plan, offs_of_cls, meta, x1, n_rows):
    n = x1.shape[0] // TM
    return pl.pallas_call(
        _dispatch_kernel,
        grid_spec=pltpu.PrefetchScalarGridSpec(
            num_scalar_prefetch=2,
            grid=(n,),
            in_specs=[
                pl.BlockSpec((1, SUBLANES, TM), lambda i, *_: (i, 0, 0), memory_space=pltpu.SMEM),
                pl.BlockSpec(memory_space=pl.ANY),
            ],
            out_specs=pl.BlockSpec(memory_space=pl.ANY),
            scratch_shapes=[pltpu.VMEM((TM_MOE, X_ROW), f32), pltpu.VMEM((3, TM, X_ROW), f32),
                            pltpu.SemaphoreType.DMA((3,)), pltpu.SemaphoreType.DMA(()),
                            pltpu.SemaphoreType.DMA((3,))],
        ),
        out_shape=jax.ShapeDtypeStruct((n_rows, X_ROW), f32),
        compiler_params=_cparams(1),
        name="dispatch",
    )(pad_plan, offs_of_cls, meta, x1)


def _expert_kernel(ta_ref, tb_ref, ca_ref, cb_ref, sa_ref, sb_ref, nu_ref,
                   x_ref, wg_hbm, wu_hbm, wd_hbm, g2_ref, b2_ref, y_ref,
                   wga, wua, wda, wgb, wub, wdb, sem_a, sem_b):
    i = pl.program_id(0)
    n_used = nu_ref[0]
    first = i * MOE_TILES
    side_a = (ta_ref, ca_ref, sa_ref, (wga, wua, wda), sem_a)
    side_b = (tb_ref, cb_ref, sb_ref, (wgb, wub, wdb), sem_b)

    def copies(side, j):
        t_ref, _, s_ref, bufs, sem = side
        e, s = t_ref[j], s_ref[j]
        return [pltpu.make_async_copy(w.at[e], buf.at[s], sem.at[s])
                for w, buf in zip((wg_hbm, wu_hbm, wd_hbm), bufs)]

    def fetch(j):
        @pl.when(j < n_used)
        def _():
            for side in (side_a, side_b):
                @pl.when(side[1][j] == 1)
                def _():
                    for cp in copies(side, j):
                        cp.start()

    @pl.when(i == 0)
    def _():
        for j in range(W_AHEAD):
            fetch(j)

    for t in range(MOE_TILES):
        fetch(first + W_AHEAD + t)

    @pl.when(first < n_used)
    def _():
        for t in range(MOE_TILES):
            for side in (side_a, side_b):
                @pl.when(side[1][first + t] == 1)
                def _():
                    for cp in copies(side, first + t):
                        cp.wait()
        rows = [slice(t * TM_MOE, (t + 1) * TM_MOE) for t in range(MOE_TILES)]
        x = [x_ref[r, 0:D_MODEL] for r in rows]
        info = [x_ref[r, D_MODEL:X_ROW] for r in rows]
        xb = [v.astype(bf16) for v in x]
        units = [(t, bufs, s_ref[first + t]) for t in range(MOE_TILES)
                 for bufs, s_ref in (((wga, wua, wda), sa_ref), ((wgb, wub, wdb), sb_ref))]
        hg = [jnp.dot(xb[t], bufs[0][s].astype(bf16), preferred_element_type=f32) for t, bufs, s in units]
        hu = [jnp.dot(xb[t], bufs[1][s].astype(bf16), preferred_element_type=f32) for t, bufs, s in units]
        mid = [(g * jax.nn.sigmoid(g) * u).astype(bf16) for g, u in zip(hg, hu)]
        ys = [jnp.dot(m, bufs[2][s].astype(bf16), preferred_element_type=f32)
              for m, (t, bufs, s) in zip(mid, units)]
        for t in range(MOE_TILES):
            ffn = info[t][:, 2:3] * ys[2 * t] + info[t][:, 3:4] * ys[2 * t + 1]
            y_ref[rows[t], :] = _layer_norm(ALPHA * x[t] + ffn, g2_ref[...], b2_ref[...])

    @pl.when(first >= n_used)
    def _():
        y_ref[...] = jnp.zeros_like(y_ref)


def _experts(tile_ea, tile_eb, n_used, xs, wg, wu, wd, g2, b2):
    nt = xs.shape[0] // TM_MOE

    def ring_plan(tile_e):
        opens = jnp.concatenate([jnp.ones((1,), jnp.int32), (tile_e[1:] != tile_e[:-1]).astype(jnp.int32)])
        return opens, (jnp.cumsum(opens) - 1) % W_SLOTS

    open_a, slot_a = ring_plan(tile_ea)
    open_b, slot_b = ring_plan(tile_eb)
    assert nt % MOE_TILES == 0
    n_used = ((n_used + MOE_TILES - 1) // MOE_TILES) * MOE_TILES
    const = lambda i, *_: (0, 0)
    xmap = lambda i, ta, tb, ca, cb, sa, sb, nu: (jnp.minimum(i, nu[0] // MOE_TILES - 1), 0)
    w_in = pltpu.VMEM((W_SLOTS, D_MODEL, D_EXPERT), f32)
    w_out = pltpu.VMEM((W_SLOTS, D_EXPERT, D_MODEL), f32)
    return pl.pallas_call(
        _expert_kernel,
        grid_spec=pltpu.PrefetchScalarGridSpec(
            num_scalar_prefetch=7,
            grid=(nt // MOE_TILES,),
            in_specs=[
                pl.BlockSpec((MOE_TILES * TM_MOE, X_ROW), xmap),
                pl.BlockSpec(memory_space=pl.ANY), pl.BlockSpec(memory_space=pl.ANY),
                pl.BlockSpec(memory_space=pl.ANY),
                pl.BlockSpec((1, D_MODEL), const), pl.BlockSpec((1, D_MODEL), const),
            ],
            out_specs=pl.BlockSpec((MOE_TILES * TM_MOE, D_MODEL), lambda i, *_: (i, 0)),
            scratch_shapes=[w_in, w_in, w_out, w_in, w_in, w_out,
                            pltpu.SemaphoreType.DMA((W_SLOTS,)), pltpu.SemaphoreType.DMA((W_SLOTS,))],
        ),
        out_shape=jax.ShapeDtypeStruct((nt * TM_MOE, D_MODEL), f32),
        compiler_params=_cparams(1),
        name="experts",
    )(tile_ea, tile_eb, open_a, open_b, slot_a.astype(jnp.int32), slot_b.astype(jnp.int32), n_used,
      xs, wg, wu, wd, g2, b2)


def _collect_kernel(offs_ref, meta_ref, metan_ref, ys_ref, op_ref, os_ref, ybuf, sem, *, n_p):
    i = pl.program_id(0)
    n = pl.num_programs(0)
    slot = i % 2

    def gather(m_ref, s):
        def start(r, c):
            _row_copy(ys_ref, _sorted_row(offs_ref, m_ref, r), ybuf.at[s], r, sem.at[s]).start()
            return c

        lax.fori_loop(0, TM, start, 0, unroll=8)

    @pl.when(i == 0)
    def _():
        gather(meta_ref, 0)

    @pl.when(i + 1 < n)
    def _():
        gather(metan_ref, 1 - slot)

    _tile_rows_wait(ys_ref, ybuf.at[slot], sem.at[slot])

    @pl.when(i < n_p)
    def _():
        op_ref[...] = ybuf[slot]

    @pl.when(i >= n_p)
    def _():
        os_ref[...] = ybuf[slot]


def _collect(offs_of_cls, meta, ys, n_p):
    n = meta.shape[0]
    n_s = n - n_p
    return pl.pallas_call(
        functools.partial(_collect_kernel, n_p=n_p),
        grid_spec=pltpu.PrefetchScalarGridSpec(
            num_scalar_prefetch=1,
            grid=(n,),
            in_specs=[
                pl.BlockSpec((1, SUBLANES, TM), lambda i, *_: (i, 0, 0), memory_space=pltpu.SMEM),
                pl.BlockSpec((1, SUBLANES, TM), lambda i, *_: (jnp.minimum(i + 1, n - 1), 0, 0),
                             memory_space=pltpu.SMEM),
                pl.BlockSpec(memory_space=pl.ANY),
            ],
            out_specs=[
                pl.BlockSpec((TM, D_MODEL), lambda i, *_: (jnp.minimum(i, n_p - 1), 0)),
                pl.BlockSpec((TM, D_MODEL), lambda i, *_: (jnp.maximum(i - n_p, 0), 0)),
            ],
            scratch_shapes=[pltpu.VMEM((2, TM, D_MODEL), f32), pltpu.SemaphoreType.DMA((2,))],
        ),
        out_shape=[
            jax.ShapeDtypeStruct((n_p * TM, D_MODEL), f32),
            jax.ShapeDtypeStruct((n_s * TM, D_MODEL), f32),
        ],
        compiler_params=_cparams(1),
        name="collect",
    )(offs_of_cls, meta, meta, ys)


def _block_diag(w):
    n, d, _ = w.shape
    eye = jnp.eye(n, dtype=w.dtype)
    return (eye[:, None, :, None] * w[:, :, None, :]).reshape(n * d, n * d)


def _pair_walk(n):
    total = n * (n - 1) // 2

    def extend(path, used):
        if len(path) == total:
            return path
        a, b = path[-1]
        for c in range(n):
            for nxt in ((a, c), (c, b)):
                key = frozenset(nxt)
                if len(key) == 2 and key not in used:
                    out = extend(path + [nxt], used | {key})
                    if out:
                        return out
        return None

    return extend([(0, 1)], {frozenset((0, 1))})


def _pad_history(buf):
    return jnp.pad(buf, ((0, 0), (SUBLANES - (CONV_W - 1), 0), (0, 0)))


def kernel(x_prompt, x_sample, state_rg_h, state_rg_conv, state_m_C, state_m_n, state_m_m, state_m_conv, w_in, rg_conv_w, rg_conv_b, rg_w_a, rg_b_a, rg_w_x, rg_b_x, rg_lambda, m_conv_w, m_conv_b, m_w_q, m_w_k, m_w_v, m_b_i, m_b_f, m_norm_g, m_skip, w_out, ln1_g, ln1_b, ln2_g, ln2_b, moe_w_group, moe_b_group, moe_w_expert, moe_b_expert, moe_w_gate, moe_w_up, moe_w_down):
    BP, TP, _ = x_prompt.shape
    BS, TS, _ = x_sample.shape
    n_prompt, n_sample = BP * TP, BS * TS
    xp = x_prompt.reshape(n_prompt, D_MODEL)
    xs = x_sample.reshape(n_sample, D_MODEL)
    l = 0

    w_in_p = jnp.pad(w_in[l], ((0, 0), (0, N_PROJ - w_in.shape[-1]))).astype(bf16)
    wa = _block_diag(rg_w_a[l]).astype(bf16)
    wx = _block_diag(rg_w_x[l]).astype(bf16)
    row = lambda v: v.reshape(1, -1)
    bif = jnp.pad(jnp.concatenate([m_b_i[l], m_b_f[l]]), (0, LANES - 2 * M_HEADS)).reshape(1, LANES)
    wqk = jnp.concatenate([m_w_q[l], m_w_k[l]], axis=-1).astype(bf16)
    wv = m_w_v[l].astype(bf16)
    wkT = m_w_k[l].transpose(0, 2, 1).astype(bf16)
    w_route = jnp.pad(jnp.concatenate([moe_w_group[l], moe_w_expert[l]], axis=1),
                      ((0, 0), (0, LANES - N_GROUPS - N_EXPERTS)))
    wr_hi = w_route.astype(bf16)
    wr_lo = (w_route - wr_hi.astype(f32)).astype(bf16)
    wr = jnp.concatenate([wr_hi, wr_lo], axis=1)
    b_route = jnp.pad(jnp.concatenate([moe_b_group[l], moe_b_expert[l]]),
                      (0, LANES - N_GROUPS - N_EXPERTS)).reshape(1, LANES)

    proj_p, proj_s = _inproj(xp, xs, w_in_p)

    rg_args = (rg_conv_w[l], row(rg_conv_b[l]), wa, row(rg_b_a[l]), wx, row(rg_b_x[l]), row(rg_lambda[l]))
    zeros = lambda *s: jnp.zeros(s, f32)
    s_blk = M_ROWS // TS
    yrg_p, p_rg_h, p_rg_conv = _rglru(proj_p, 0, BP, TP, 1, RG_ROWS_PROMPT, zeros(BP, SUBLANES, D_RG),
                                      zeros(BP, 1, D_RG), *rg_args)
    yrg_s, s_rg_h, s_rg_conv = _rglru(proj_s, 0, BS, TS, s_blk, TS, _pad_history(state_rg_conv[l]),
                                      state_rg_h[l].reshape(BS, 1, D_RG), *rg_args)

    m_args = (m_conv_w[l], row(m_conv_b[l]), wqk, wv, wkT, bif, row(m_norm_g[l]), row(m_skip[l]))
    m0_p = jnp.full((SUBLANES, BP * LANES), M_INIT, f32)
    ym_p, p_m_C, p_m_n, p_m_m, p_m_conv = _mlstm(
        proj_p, BP, TP, 1, M_ROWS, M_BLOCKS_PROMPT, zeros(BP, SUBLANES, D_M), zeros(BP, M_HEADS, M_DH, M_DH),
        zeros(BP, M_HEADS, 1, M_DH), m0_p, *m_args)
    m0_s = jnp.pad(jnp.repeat(state_m_m[l].T, TS, axis=1), ((0, SUBLANES - M_HEADS), (0, 0)))
    ym_s, s_m_C, s_m_n, s_m_m, s_m_conv = _mlstm(
        proj_s, BS, TS, s_blk, TS, M_BLOCKS_SAMPLE, _pad_history(state_m_conv[l]), state_m_C[l],
        state_m_n[l].reshape(BS, M_HEADS, 1, M_DH), m0_s, *m_args)

    x1, meta, cnt = _outproj(xp, xs, yrg_p, yrg_s, ym_p, ym_s, w_out[l].astype(bf16),
                             row(ln1_g[l]), row(ln1_b[l]), wr, b_route)

    n_tok = n_prompt + n_sample
    n_tiles = n_tok // TM_MOE + N_CLASSES
    lay_cls, lay_a, lay_b = [], [], []
    for g in range(N_GROUPS):
        for ja, jb in _pair_walk(EXPERTS_PER_GROUP):
            assert ja < jb
            n_first = (EXPERTS_PER_GROUP - 1) * ja - ja * (ja - 1) // 2
            lay_cls.append(PAIRS_PER_GROUP * g + n_first + jb - ja - 1)
            lay_a.append(g * EXPERTS_PER_GROUP + ja)
            lay_b.append(g * EXPERTS_PER_GROUP + jb)
    classes = jnp.arange(N_CLASSES, dtype=jnp.int32)
    in_slot = jnp.array(lay_cls, jnp.int32)[:, None] == classes
    counts = jnp.sum(jnp.where(in_slot, cnt[0, :N_CLASSES].astype(jnp.int32), 0), axis=1)
    padded = ((counts + TM_MOE - 1) // TM_MOE) * TM_MOE
    ends = jnp.cumsum(padded)
    offs = ends - padded
    offs_of_cls = jnp.sum(jnp.where(in_slot, offs[:, None], 0), axis=0).astype(jnp.int32)
    tiles = jnp.arange(n_tiles, dtype=jnp.int32)
    tile_slot = jnp.minimum(jnp.sum(tiles[:, None] >= (ends // TM_MOE)[None, :], axis=1), N_CLASSES - 1)
    on_slot = tile_slot[:, None] == classes
    tile_ea = jnp.sum(jnp.where(on_slot, jnp.array(lay_a, jnp.int32), 0), axis=-1).astype(jnp.int32)
    tile_eb = jnp.sum(jnp.where(on_slot, jnp.array(lay_b, jnp.int32), 0), axis=-1).astype(jnp.int32)
    n_used = (ends[-1] // TM_MOE).reshape(1).astype(jnp.int32)
    pad_row0 = ((offs + counts) // SUBLANES) * SUBLANES
    pad_plan = jnp.concatenate([jnp.stack([pad_row0, (ends - pad_row0) // SUBLANES], axis=1).reshape(-1),
                                     n_used]).astype(jnp.int32)

    x_sorted = _dispatch(pad_plan, offs_of_cls, meta, x1, n_tiles * TM_MOE)
    y_sorted = _experts(tile_ea, tile_eb, n_used, x_sorted, moe_w_gate[l], moe_w_up[l], moe_w_down[l],
                        row(ln2_g[l]), row(ln2_b[l]))
    y_p, y_s = _collect(offs_of_cls, meta, y_sorted, n_prompt // TM)

    return (y_p.reshape(BP, TP, D_MODEL), y_s.reshape(BS, TS, D_MODEL),
            p_rg_h.reshape(1, BP, D_RG), p_rg_conv[None], p_m_C[None], p_m_n.reshape(1, BP, M_HEADS, M_DH),
            p_m_m[None], p_m_conv[None],
            s_rg_h.reshape(1, BS, D_RG), s_rg_conv[None], s_m_C[None], s_m_n.reshape(1, BS, M_HEADS, M_DH),
            s_m_m[None], s_m_conv[None])
```

```python
import functools

import jax
import jax.numpy as jnp
from jax import lax
from jax.experimental import pallas as pl
from jax.experimental.pallas import tpu as pltpu

f32 = jnp.float32
bf16 = jnp.bfloat16

D_MODEL = 1024
D_RG = 512
RG_BLOCKS = 8
RG_C = 8.0
D_M = 512
M_HEADS = 4
M_DH = 128
CONV_W = 4
N_GROUPS = 4
EXPERTS_PER_GROUP = 8
N_EXPERTS = 32
D_EXPERT = 256
ALPHA = 2.0 ** 0.25
LN_EPS = 1e-5
M_INIT = -1.0e4

LANES = 128
SUBLANES = 8
TM = 512
TM_MOE = 128
MOE_TILES = 4
W_AHEAD = 4
W_SLOTS = W_AHEAD + MOE_TILES
PAIRS_PER_GROUP = EXPERTS_PER_GROUP * (EXPERTS_PER_GROUP - 1) // 2
N_CLASSES = N_GROUPS * PAIRS_PER_GROUP
X_ROW = D_MODEL + LANES
N_PROJ = 2 * D_RG + 2 * D_M + LANES
RG_ROWS_PROMPT = 512
M_ROWS = 128
M_BLOCKS_PROMPT = 8
M_BLOCKS_SAMPLE = 2
VMEM_LIMIT = 56 * 1024 * 1024


def _cparams(n_axes):
    return pltpu.CompilerParams(dimension_semantics=("arbitrary",) * n_axes,
                                vmem_limit_bytes=VMEM_LIMIT)


def _inproj_kernel(xp_ref, xs_ref, w_ref, op_ref, os_ref, *, n_p):
    i = pl.program_id(0)

    def run(x_ref, o_ref):
        o_ref[...] = jnp.dot(x_ref[...].astype(bf16), w_ref[...], preferred_element_type=f32)

    @pl.when(i < n_p)
    def _():
        run(xp_ref, op_ref)

    @pl.when(i >= n_p)
    def _():
        run(xs_ref, os_ref)


def _inproj(xp, xs, w):
    n_p, n_s = xp.shape[0] // TM, xs.shape[0] // TM
    pmap = lambda i: (jnp.minimum(i, n_p - 1), 0)
    smap = lambda i: (jnp.maximum(i - n_p, 0), 0)
    return pl.pallas_call(
        functools.partial(_inproj_kernel, n_p=n_p),
        grid=(n_p + n_s,),
        in_specs=[
            pl.BlockSpec((TM, D_MODEL), pmap),
            pl.BlockSpec((TM, D_MODEL), smap),
            pl.BlockSpec((D_MODEL, N_PROJ), lambda i: (0, 0)),
        ],
        out_specs=[pl.BlockSpec((TM, N_PROJ), pmap), pl.BlockSpec((TM, N_PROJ), smap)],
        out_shape=[jax.ShapeDtypeStruct((n_p * TM, N_PROJ), f32),
                   jax.ShapeDtypeStruct((n_s * TM, N_PROJ), f32)],
        compiler_params=_cparams(1),
        name="inproj",
    )(xp, xs, w)


def _causal_conv(tail_scr, x, cw_ref, cb_ref, S, L, C):
    tail = tail_scr[...]
    sub = lax.broadcasted_iota(jnp.int32, (S, SUBLANES, C), 1)
    acc = cb_ref[...] + cw_ref[CONV_W - 1:CONV_W, :] * x
    for d in range(1, CONV_W):
        back = pltpu.roll(x, d, 0).reshape(S, L, C)
        head = jnp.where(sub < d, pltpu.roll(tail, d, 1), back[:, 0:SUBLANES, :])
        if L > SUBLANES:
            back = jnp.concatenate([head, back[:, SUBLANES:, :]], axis=1)
        else:
            back = head
        acc = acc + cw_ref[CONV_W - 1 - d:CONV_W - d, :] * back.reshape(S * L, C)
    tail_scr[...] = x.reshape(S, L, C)[:, L - SUBLANES:, :]
    return acc


def _softplus(x):
    return jnp.maximum(x, 0.0) + jnp.log1p(jnp.exp(-jnp.abs(x)))


def _rglru_kernel(x_ref, g_ref, buf_ref, h0_ref, cw_ref, cb_ref, wa_ref, ba_ref, wx_ref, bx_ref,
                  lam_ref, y_ref, hN_ref, cN_ref, xs_scr, *, S, L):
    R = S * L
    t = pl.program_id(1)

    @pl.when(t == 0)
    def _():
        xs_scr[...] = buf_ref[...]
        hN_ref[...] = h0_ref[...]

    x = x_ref[...]
    xc = _causal_conv(xs_scr, x, cw_ref, cb_ref, S, L, D_RG)
    xcb = xc.astype(bf16)
    r = jax.nn.sigmoid(jnp.dot(xcb, wa_ref[...], preferred_element_type=f32) + ba_ref[...])
    ig = jax.nn.sigmoid(jnp.dot(xcb, wx_ref[...], preferred_element_type=f32) + bx_ref[...])
    log_a = (-RG_C) * r * _softplus(-lam_ref[...])
    a = jnp.exp(log_a)
    th = jnp.tanh(log_a)
    u = jnp.sqrt(-2.0 * th / (1.0 - th)) * ig * xc

    n_grp, grp_per_seq = R // SUBLANES, L // SUBLANES
    a3 = a.reshape(n_grp, SUBLANES, D_RG)
    u3 = u.reshape(n_grp, SUBLANES, D_RG)
    sub = lax.broadcasted_iota(jnp.int32, (n_grp, SUBLANES, D_RG), 1)
    s = 1
    while s < SUBLANES:
        ok = sub >= s
        a_sh = pltpu.roll(a3, s, 1)
        u_sh = pltpu.roll(u3, s, 1)
        u3 = jnp.where(ok, a3 * u_sh + u3, u3)
        a3 = jnp.where(ok, a3 * a_sh, a3)
        s *= 2
    h0 = hN_ref[...]
    groups = []
    for kg in range(n_grp):
        carry = h0[kg // grp_per_seq] if kg % grp_per_seq == 0 else groups[-1][SUBLANES - 1:SUBLANES, :]
        groups.append(a3[kg] * carry + u3[kg])
    h = jnp.concatenate(groups, axis=0)

    y_ref[...] = h * jax.nn.gelu(g_ref[...], approximate=True)
    hN_ref[...] = h.reshape(S, L, D_RG)[:, L - 1:L, :]
    cN_ref[...] = xs_scr[:, SUBLANES - (CONV_W - 1):SUBLANES, :]


def _rglru(proj, row0, B, T, S, L, buf8, h0, cw, cb, wa, ba, wx, bx, lam):
    R = S * L
    nb, nt = B // S, T // L
    blk0 = row0 // R
    row_map = lambda b, t: (blk0 + b * nt + t, 0)
    const2 = lambda b, t: (0, 0)
    return pl.pallas_call(
        functools.partial(_rglru_kernel, S=S, L=L),
        grid=(nb, nt),
        in_specs=[
            pl.BlockSpec((R, D_RG), row_map),
            pl.BlockSpec((R, D_RG), lambda b, t: (blk0 + b * nt + t, 1)),
            pl.BlockSpec((S, SUBLANES, D_RG), lambda b, t: (b, 0, 0)),
            pl.BlockSpec((S, 1, D_RG), lambda b, t: (b, 0, 0)),
            pl.BlockSpec((CONV_W, D_RG), const2),
            pl.BlockSpec((1, D_RG), const2),
            pl.BlockSpec((D_RG, D_RG), const2),
            pl.BlockSpec((1, D_RG), const2),
            pl.BlockSpec((D_RG, D_RG), const2),
            pl.BlockSpec((1, D_RG), const2),
            pl.BlockSpec((1, D_RG), const2),
        ],
        out_specs=[
            pl.BlockSpec((R, D_RG), lambda b, t: (b * nt + t, 0)),
            pl.BlockSpec((S, 1, D_RG), lambda b, t: (b, 0, 0)),
            pl.BlockSpec((S, CONV_W - 1, D_RG), lambda b, t: (b, 0, 0)),
        ],
        out_shape=[
            jax.ShapeDtypeStruct((B * T, D_RG), f32),
            jax.ShapeDtypeStruct((B, 1, D_RG), f32),
            jax.ShapeDtypeStruct((B, CONV_W - 1, D_RG), f32),
        ],
        scratch_shapes=[pltpu.VMEM((S, SUBLANES, D_RG), f32)],
        compiler_params=_cparams(2),
        name="rglru",
    )(proj, proj, buf8, h0, cw, cb, wa, ba, wx, bx, lam)


def _seg_scan(x, op, fill, tin, L, reverse=False):
    s = 1
    while s < L:
        if reverse:
            sh = pltpu.roll(x, LANES - s, 1)
            ok = tin < L - s
        else:
            sh = pltpu.roll(x, s, 1)
            ok = tin >= s
        x = op(x, jnp.where(ok, sh, fill))
        s *= 2
    return x


def _mlstm_kernel(*refs, S, L, G, n_riders):
    (x_ref, z_ref, gt_ref, buf_ref, c0_ref, n0_ref, m0_ref,
     cw_ref, cb_ref, wqk_ref, wv_ref, wkT_ref, bif_ref, ng_ref, sk_ref) = refs[:15]
    rider_in = refs[15:15 + n_riders]
    y_ref, cN_ref, nN_ref, mN_ref, bN_ref = refs[15 + n_riders:20 + n_riders]
    rider_out = refs[20 + n_riders:20 + 2 * n_riders]
    xs_scr, m_scr = refs[20 + 2 * n_riders:]
    for src, dst in zip(rider_in, rider_out):
        dst[...] = src[...].astype(bf16)

    R = S * L
    GS, GR = G * S, G * R
    j = pl.program_id(1)

    @pl.when(j == 0)
    def _():
        xs_scr[...] = buf_ref[...]
        cN_ref[...] = c0_ref[...]
        nN_ref[...] = n0_ref[...]
        m_scr[...] = m0_ref[...]

    x = x_ref[...].reshape(GR, D_M)
    xc = _causal_conv(xs_scr, x, cw_ref, cb_ref, GS, L, D_M)
    xa = xc * jax.nn.sigmoid(xc)
    xab = xa.astype(bf16)
    xb = x.astype(bf16)

    il_parts, f_parts = [], []
    for g in range(G):
        gT = (gt_ref[g] + bif_ref[...]).T
        il_parts.append(gT[0:SUBLANES, :])
        f_parts.append(pltpu.roll(gT[0:SUBLANES, :], M_HEADS, 0))
    il = jnp.concatenate(il_parts, axis=0)
    fl = -_softplus(-jnp.concatenate(f_parts, axis=0))
    tin = lax.broadcasted_iota(jnp.int32, (G * SUBLANES, LANES), 1) % L
    bcum = _seg_scan(fl, jnp.add, 0.0, tin, L)
    a = il - bcum
    m_prev = m_scr[...].reshape(G * SUBLANES, LANES)
    big_m = jnp.maximum(m_prev, _seg_scan(a, jnp.maximum, -jnp.inf, tin, L))
    m_t = bcum + big_m
    if S == 1:
        m_last = jnp.broadcast_to(big_m[:, LANES - 1:LANES], big_m.shape)
    else:
        m_last = _seg_scan(big_m, jnp.maximum, -jnp.inf, tin, L, reverse=True)
    rows = [big_m, jnp.exp(m_prev - big_m), jnp.exp(-m_t), jnp.exp(a - m_last), jnp.exp(m_prev - m_last)]
    mN_ref[...] = m_t.reshape(G, SUBLANES, LANES)
    m_scr[...] = jnp.broadcast_to(m_t[:, LANES - 1:LANES], m_t.shape).reshape(G, SUBLANES, LANES)
    pad_rows = jnp.zeros((LANES - len(rows) * SUBLANES, LANES), f32)
    cols = []
    for g in range(G):
        gs = slice(g * SUBLANES, (g + 1) * SUBLANES)
        cols.append(jnp.concatenate([r[gs] for r in rows] + [pad_rows], axis=0).T)

    def col(g, q, h):
        return cols[g][:, SUBLANES * q + h:SUBLANES * q + h + 1]

    qs, ks, vs, kTs = [], [], [], []
    for h in range(M_HEADS):
        hs = slice(h * M_DH, (h + 1) * M_DH)
        qk_h = jnp.dot(xab[:, hs], wqk_ref[h], preferred_element_type=f32)
        qs.append(qk_h[:, 0:M_DH])
        ks.append(qk_h[:, M_DH:2 * M_DH] * (M_DH ** -0.5))
        vs.append(jnp.dot(xb[:, hs], wv_ref[h], preferred_element_type=f32))
        if S == 1:
            kT_h = lax.dot_general(wkT_ref[h], xab[:, hs], (((1,), (1,)), ((), ())),
                                   preferred_element_type=f32)
            kTs.append((kT_h * (M_DH ** -0.5)).astype(bf16))

    ti = lax.broadcasted_iota(jnp.int32, (R, R), 0)
    si = lax.broadcasted_iota(jnp.int32, (R, R), 1)
    mask = (si <= ti) & ((ti // L) == (si // L))
    ones_b = jnp.ones((R, M_DH), bf16)
    pairs = [(g, h) for g in range(G) for h in range(M_HEADS)]
    blk = lambda g: slice(g * R, (g + 1) * R)
    seqs = [slice(b * L, (b + 1) * L) for b in range(S)]
    q = {(g, h): qs[h][blk(g)] for g, h in pairs}
    k = {(g, h): ks[h][blk(g)] for g, h in pairs}
    v = {(g, h): vs[h][blk(g)] for g, h in pairs}
    qb = {p: q[p].astype(bf16) for p in pairs}
    kb = {p: k[p].astype(bf16) for p in pairs}
    qk = {p: lax.dot_general(qb[p], kb[p], (((1,), (1,)), ((), ())), preferred_element_type=f32)
          for p in pairs}
    sm = {}
    for g, h in pairs:
        a_row = a[g * SUBLANES + h:g * SUBLANES + h + 1, :]
        decay = jnp.exp(jnp.where(mask, a_row - col(g, 0, h), -jnp.inf))
        sm[g, h] = (qk[g, h] * decay).astype(bf16)
    nd = {p: jnp.dot(sm[p], jnp.concatenate([v[p].astype(bf16), ones_b], axis=1),
                     preferred_element_type=f32) for p in pairs}
    c_old = {(g, h, b): cN_ref[g * S + b, h] for g, h in pairs for b in range(S)}
    n_old = {(g, h, b): nN_ref[g * S + b, h] for g, h in pairs for b in range(S)}
    q_c = {(g, h, b): jnp.dot(qb[g, h][seqs[b]], c_old[g, h, b].astype(bf16), preferred_element_type=f32)
           for g, h in pairs for b in range(S)}
    hh = {}
    for g, h in pairs:
        g_col, e_col = col(g, 1, h), col(g, 2, h)
        pieces = []
        for b, rs in enumerate(seqs):
            q_n = jnp.sum(q[g, h][rs] * n_old[g, h, b], axis=1, keepdims=True)
            num = nd[g, h][rs, 0:M_DH] + g_col[rs] * q_c[g, h, b]
            den = nd[g, h][rs, M_DH:2 * M_DH] + g_col[rs] * q_n
            pieces.append(num / jnp.maximum(jnp.abs(den), e_col[rs]))
        hh[g, h] = pieces[0] if S == 1 else jnp.concatenate(pieces, axis=0)
    mu = {p: jnp.mean(hh[p], axis=1, keepdims=True) for p in pairs}
    var = {p: jnp.mean(jnp.square(hh[p] - mu[p]), axis=1, keepdims=True) for p in pairs}
    hn_blocks = [jnp.concatenate([(hh[g, h] - mu[g, h]) * lax.rsqrt(var[g, h] + LN_EPS)
                                  for h in range(M_HEADS)], axis=1) for g in range(G)]
    new_c, new_n = [], []
    for g, h in pairs:
        w_col = col(g, 3, h)
        wv = (w_col * v[g, h]).astype(bf16)
        wk = w_col * k[g, h]
        for b, rs in enumerate(seqs):
            g_end = col(g, 4, h)[(b + 1) * L - 1:(b + 1) * L, :]
            if S == 1:
                kv = jnp.dot(kTs[h][:, blk(g)], wv, preferred_element_type=f32)
            else:
                kv = lax.dot_general(kb[g, h][rs], wv[rs], (((0,), (0,)), ((), ())),
                                     preferred_element_type=f32)
            new_c.append((g * S + b, h, g_end * c_old[g, h, b] + kv))
            new_n.append((g * S + b, h, g_end * n_old[g, h, b] + jnp.sum(wk[rs], axis=0, keepdims=True)))

    hn = jnp.concatenate(hn_blocks, axis=0) * ng_ref[...]
    y = jax.nn.sigmoid(z_ref[...].reshape(GR, D_M)) * (hn + sk_ref[...] * xa)
    y_ref[...] = y.reshape(G, R, D_M)
    for sq, h, val in new_c:
        cN_ref[sq, h] = val
    for sq, h, val in new_n:
        nN_ref[sq, h] = val
    bN_ref[...] = xs_scr[:, SUBLANES - (CONV_W - 1):SUBLANES, :]


def _mlstm(proj, B, T, S, L, G, buf8, c0, n0, m0_lanes, cw, cb, wqk, wv, wkT, bif, ng, sk, riders=()):
    R = S * L
    assert R == M_ROWS == LANES
    nblk, nc = B // S, T // L
    nb = nblk // G
    assert nblk * S == B and nc * L == T and nb * G == nblk and (S == 1 or nc == 1)
    steps = nb * nc
    assert all(r.shape[0] % steps == 0 for r in riders)
    rider_specs = [pl.BlockSpec((r.shape[0] // steps,) + r.shape[1:], lambda b, j: (b * nc + j, 0, 0))
                   for r in riders]
    proj4 = proj.reshape(nblk, nc, R, N_PROJ)
    m0_3 = m0_lanes.reshape(SUBLANES, nblk, LANES).transpose(1, 0, 2)
    const2 = lambda b, j: (0, 0)
    const3 = lambda b, j: (0, 0, 0)
    gate_blk = (2 * D_RG + 2 * D_M) // LANES
    GS = G * S
    outs = pl.pallas_call(
        functools.partial(_mlstm_kernel, S=S, L=L, G=G, n_riders=len(riders)),
        grid=(nb, nc),
        in_specs=[
            pl.BlockSpec((G, None, R, D_M), lambda b, j: (b, j, 0, 2)),
            pl.BlockSpec((G, None, R, D_M), lambda b, j: (b, j, 0, 3)),
            pl.BlockSpec((G, None, R, LANES), lambda b, j: (b, j, 0, gate_blk)),
            pl.BlockSpec((GS, SUBLANES, D_M), lambda b, j: (b, 0, 0)),
            pl.BlockSpec((GS, M_HEADS, M_DH, M_DH), lambda b, j: (b, 0, 0, 0)),
            pl.BlockSpec((GS, M_HEADS, 1, M_DH), lambda b, j: (b, 0, 0, 0)),
            pl.BlockSpec((G, SUBLANES, LANES), lambda b, j: (b, 0, 0)),
            pl.BlockSpec((CONV_W, D_M), const2),
            pl.BlockSpec((1, D_M), const2),
            pl.BlockSpec((M_HEADS, M_DH, 2 * M_DH), const3),
            pl.BlockSpec((M_HEADS, M_DH, M_DH), const3),
            pl.BlockSpec((M_HEADS, M_DH, M_DH), const3),
            pl.BlockSpec((1, LANES), const2),
            pl.BlockSpec((1, D_M), const2),
            pl.BlockSpec((1, D_M), const2),
        ] + rider_specs,
        out_specs=[
            pl.BlockSpec((G, None, R, D_M), lambda b, j: (b, j, 0, 0)),
            pl.BlockSpec((GS, M_HEADS, M_DH, M_DH), lambda b, j: (b, 0, 0, 0)),
            pl.BlockSpec((GS, M_HEADS, 1, M_DH), lambda b, j: (b, 0, 0, 0)),
            pl.BlockSpec((G, None, SUBLANES, LANES), lambda b, j: (b, j, 0, 0)),
            pl.BlockSpec((GS, CONV_W - 1, D_M), lambda b, j: (b, 0, 0)),
        ] + rider_specs,
        out_shape=[
            jax.ShapeDtypeStruct((nblk, nc, R, D_M), f32),
            jax.ShapeDtypeStruct((B, M_HEADS, M_DH, M_DH), f32),
            jax.ShapeDtypeStruct((B, M_HEADS, 1, M_DH), f32),
            jax.ShapeDtypeStruct((nblk, nc, SUBLANES, LANES), f32),
            jax.ShapeDtypeStruct((B, CONV_W - 1, D_M), f32),
        ] + [jax.ShapeDtypeStruct(r.shape, bf16) for r in riders],
        scratch_shapes=[pltpu.VMEM((GS, SUBLANES, D_M), f32), pltpu.VMEM((G, SUBLANES, LANES), f32)],
        compiler_params=_cparams(2),
        name="mlstm",
    )(proj4, proj4, proj4, buf8, c0, n0, m0_3, cw, cb, wqk, wv, wkT, bif, ng, sk, *riders)
    y, c_new, n_new, m_t, b_new = outs[:5]
    m_last = m_t[:, nc - 1, :M_HEADS, :].reshape(nblk, M_HEADS, S, L)[:, :, :, L - 1]
    m_last = m_last.transpose(0, 2, 1).reshape(B, M_HEADS)
    return (y.reshape(B * T, D_M), c_new, n_new, m_last, b_new) + tuple(outs[5:])


def _layer_norm(x, g, b):
    mu = jnp.mean(x, axis=-1, keepdims=True)
    var = jnp.mean(jnp.square(x - mu), axis=-1, keepdims=True)
    return (x - mu) * lax.rsqrt(var + LN_EPS) * g + b


def _first_lane_of_max(vals, lane_f):
    vmax = jnp.max(vals, axis=1, keepdims=True)
    idx = jnp.min(jnp.where(vals == vmax, lane_f, float(LANES)), axis=1, keepdims=True)
    return vmax, idx


def _outproj_kernel(xp_ref, xs_ref, rgp_ref, rgs_ref, mp_ref, ms_ref, wo_ref, g1_ref, b1_ref,
                    wr_ref, br_ref, x1_ref, cnt_ref, *, n_p):
    i = pl.program_id(0)

    @pl.when(i == 0)
    def _():
        cnt_ref[...] = jnp.zeros_like(cnt_ref)

    def run(x_ref, rg_ref, m_ref):
        heads = jnp.concatenate([rg_ref[...].astype(bf16), m_ref[...].astype(bf16)], axis=1)
        mix = jnp.dot(heads, wo_ref[...], preferred_element_type=f32)
        x1 = _layer_norm(ALPHA * x_ref[...] + mix, g1_ref[...], b1_ref[...])
        x1_ref[:, 0:D_MODEL] = x1

        hi = x1.astype(bf16)
        lo = (x1 - hi.astype(f32)).astype(bf16)
        hi_terms = jnp.dot(hi, wr_ref[...], preferred_element_type=f32)
        lo_term = jnp.dot(lo, wr_ref[:, 0:LANES], preferred_element_type=f32)
        lg = hi_terms[:, 0:LANES] + hi_terms[:, LANES:2 * LANES] + lo_term + br_ref[...]
        lane = lax.broadcasted_iota(jnp.int32, (TM, LANES), 1)
        lane_f = lane.astype(f32)
        neg = -jnp.inf
        gl = jnp.where(lane < N_GROUPS, lg, neg)
        gmax, gidx = _first_lane_of_max(gl, lane_f)
        p_g = 1.0 / jnp.sum(jnp.exp(gl - gmax), axis=1, keepdims=True)
        e_lo = float(N_GROUPS) + float(EXPERTS_PER_GROUP) * gidx
        el = jnp.where((lane_f >= e_lo) & (lane_f < e_lo + float(EXPERTS_PER_GROUP)), lg, neg)
        v1, i1 = _first_lane_of_max(el, lane_f)
        v2, i2 = _first_lane_of_max(jnp.where(lane_f == i1, neg, el), lane_f)
        d = jnp.exp(v2 - v1)
        w1 = p_g / (1.0 + d)
        w2 = p_g * d / (1.0 + d)

        first_low = i1 < i2
        j_lo = jnp.minimum(i1, i2) - e_lo
        j_hi = jnp.maximum(i1, i2) - e_lo
        w_lo = jnp.where(first_low, w1, w2)
        w_hi = jnp.where(first_low, w2, w1)
        n_first = float(EXPERTS_PER_GROUP - 1) * j_lo - 0.5 * j_lo * (j_lo - 1.0)
        cls = float(PAIRS_PER_GROUP) * gidx + n_first + (j_hi - j_lo - 1.0)

        hot = lane_f == cls
        ti = lax.broadcasted_iota(jnp.int32, (TM, TM), 0)
        si = lax.broadcasted_iota(jnp.int32, (TM, TM), 1)
        tri = (si <= ti).astype(bf16)
        cum = jnp.dot(tri, hot.astype(bf16), preferred_element_type=f32)
        rank = jnp.sum(jnp.where(hot, cum - 1.0 + cnt_ref[...], 0.0), axis=1, keepdims=True)
        cnt_ref[...] = cnt_ref[...] + cum[TM - 1:TM, :]

        info = jnp.zeros((TM, LANES), f32)
        for c, val in enumerate((cls, rank, w_lo, w_hi)):
            info = jnp.where(lane == c, val, info)
        x1_ref[:, D_MODEL:D_MODEL + LANES] = info

    @pl.when(i < n_p)
    def _():
        run(xp_ref, rgp_ref, mp_ref)

    @pl.when(i >= n_p)
    def _():
        run(xs_ref, rgs_ref, ms_ref)


def _outproj(xp, xs, rgp, rgs, mp, ms, wo, g1, b1, wr, br):
    n_p, n_s = xp.shape[0] // TM, xs.shape[0] // TM
    n = n_p + n_s
    pmap = lambda i: (jnp.minimum(i, n_p - 1), 0)
    smap = lambda i: (jnp.maximum(i - n_p, 0), 0)
    const = lambda i: (0, 0)
    return pl.pallas_call(
        functools.partial(_outproj_kernel, n_p=n_p),
        grid=(n,),
        in_specs=[
            pl.BlockSpec((TM, D_MODEL), pmap), pl.BlockSpec((TM, D_MODEL), smap),
            pl.BlockSpec((TM, D_RG), pmap), pl.BlockSpec((TM, D_RG), smap),
            pl.BlockSpec((TM, D_M), pmap), pl.BlockSpec((TM, D_M), smap),
            pl.BlockSpec((D_RG + D_M, D_MODEL), const),
            pl.BlockSpec((1, D_MODEL), const), pl.BlockSpec((1, D_MODEL), const),
            pl.BlockSpec((D_MODEL, 2 * LANES), const),
            pl.BlockSpec((1, LANES), const),
        ],
        out_specs=[
            pl.BlockSpec((TM, X_ROW), lambda i: (i, 0)),
            pl.BlockSpec((1, LANES), const),
        ],
        out_shape=[
            jax.ShapeDtypeStruct((n * TM, X_ROW), f32),
            jax.ShapeDtypeStruct((1, LANES), f32),
        ],
        compiler_params=_cparams(1),
        name="outproj",
    )(xp, xs, rgp, rgs, mp, ms, wo, g1, b1, wr, br)


def _row_copy(src_ref, src_row, dst_ref, dst_row, sem):
    return pltpu.make_async_copy(src_ref.at[pl.ds(src_row, 1)], dst_ref.at[pl.ds(dst_row, 1)], sem)


def _tile_rows_wait(src_ref, dst_ref, sem):
    pltpu.make_async_copy(src_ref.at[pl.ds(0, TM)], dst_ref.at[pl.ds(0, TM)], sem).wait()


def _dispatch_kernel(pad_ref, pos_ref, x1_ref, xs_ref, zero_scr, xbuf, sem, zsem, lsem):
    i = pl.program_id(0)
    n = pl.num_programs(0)

    def zero_tile(t):
        return pltpu.make_async_copy(zero_scr, xs_ref.at[pl.ds(pl.multiple_of(t * TM_MOE, TM_MOE), TM_MOE)], zsem)

    def zero_chunk(row0):
        return pltpu.make_async_copy(zero_scr.at[pl.ds(0, SUBLANES)],
                                     xs_ref.at[pl.ds(pl.multiple_of(row0, SUBLANES), SUBLANES)], zsem)

    @pl.when(i == 0)
    def _():
        zero_scr[...] = jnp.zeros_like(zero_scr)
        n_used = pad_ref[2 * N_CLASSES]
        n_tiles = xs_ref.shape[0] // TM_MOE
        for go in (lambda cp: cp.start(), lambda cp: cp.wait()):
            lax.fori_loop(n_used, n_tiles, lambda t, c: (go(zero_tile(t)), c)[1], 0)

            def per_class(cl, c):
                row0 = pad_ref[2 * cl]
                lax.fori_loop(0, pad_ref[2 * cl + 1],
                              lambda k, c2: (go(zero_chunk(row0 + k * SUBLANES)), c2)[1], 0)
                return c

            lax.fori_loop(0, N_CLASSES, per_class, 0)

    def load(t, s):
        return pltpu.make_async_copy(x1_ref.at[pl.ds(pl.multiple_of(t * TM, TM), TM)], xbuf.at[s], lsem.at[s])

    @pl.when(i == 0)
    def _():
        load(0, 0).start()

    @pl.when(i + 1 < n)
    def _():
        load(i + 1, (i + 1) % 3).start()

    slot = i % 3
    load(i, slot).wait()

    def start(r, c):
        _row_copy(xbuf.at[slot], r, xs_ref, pos_ref[0, 0, r], sem.at[slot]).start()
        return c

    lax.fori_loop(0, TM, start, 0, unroll=8)

    @pl.when(i > 0)
    def _():
        _tile_rows_wait(xbuf.at[(i + 2) % 3], xs_ref, sem.at[(i + 2) % 3])

    @pl.when(i == n - 1)
    def _():
        _tile_rows_wait(xbuf.at[slot], xs_ref, sem.at[slot])


def _dispatch(pad_plan, pos, x1, n_rows):
    n = x1.shape[0] // TM
    return pl.pallas_call(
        _dispatch_kernel,
        grid_spec=pltpu.PrefetchScalarGridSpec(
            num_scalar_prefetch=1,
            grid=(n,),
            in_specs=[
                pl.BlockSpec((1, 1, TM), lambda i, *_: (i, 0, 0), memory_space=pltpu.SMEM),
                pl.BlockSpec(memory_space=pl.ANY),
            ],
            out_specs=pl.BlockSpec(memory_space=pl.ANY),
            scratch_shapes=[pltpu.VMEM((TM_MOE, X_ROW), f32), pltpu.VMEM((3, TM, X_ROW), f32),
                            pltpu.SemaphoreType.DMA((3,)), pltpu.SemaphoreType.DMA(()),
                            pltpu.SemaphoreType.DMA((3,))],
        ),
        out_shape=jax.ShapeDtypeStruct((n_rows, X_ROW), f32),
        compiler_params=_cparams(1),
        name="dispatch",
    )(pad_plan, pos, x1)


def _expert_kernel(ta_ref, tb_ref, ca_ref, cb_ref, sa_ref, sb_ref, nu_ref,
                   x_ref, wg_hbm, wu_hbm, wd_hbm, g2_ref, b2_ref, y_ref,
                   wga, wua, wda, wgb, wub, wdb, sem_a, sem_b):
    i = pl.program_id(0)
    n_used = nu_ref[0]
    first = i * MOE_TILES
    side_a = (ta_ref, ca_ref, sa_ref, (wga, wua, wda), sem_a)
    side_b = (tb_ref, cb_ref, sb_ref, (wgb, wub, wdb), sem_b)

    def copies(side, j):
        t_ref, _, s_ref, bufs, sem = side
        e, s = t_ref[j], s_ref[j]
        return [pltpu.make_async_copy(w.at[e], buf.at[s], sem.at[s])
                for w, buf in zip((wg_hbm, wu_hbm, wd_hbm), bufs)]

    def fetch(j):
        @pl.when(j < n_used)
        def _():
            for side in (side_a, side_b):
                @pl.when(side[1][j] == 1)
                def _():
                    for cp in copies(side, j):
                        cp.start()

    @pl.when(i == 0)
    def _():
        for j in range(W_AHEAD):
            fetch(j)

    for t in range(MOE_TILES):
        fetch(first + W_AHEAD + t)

    @pl.when(first < n_used)
    def _():
        for t in range(MOE_TILES):
            for side in (side_a, side_b):
                @pl.when(side[1][first + t] == 1)
                def _():
                    for cp in copies(side, first + t):
                        cp.wait()
        rows = [slice(t * TM_MOE, (t + 1) * TM_MOE) for t in range(MOE_TILES)]
        x = [x_ref[r, 0:D_MODEL] for r in rows]
        info = [x_ref[r, D_MODEL:X_ROW] for r in rows]
        xb = [v.astype(bf16) for v in x]
        units = [(t, bufs, s_ref[first + t]) for t in range(MOE_TILES)
                 for bufs, s_ref in (((wga, wua, wda), sa_ref), ((wgb, wub, wdb), sb_ref))]
        hg = [jnp.dot(xb[t], bufs[0][s], preferred_element_type=f32) for t, bufs, s in units]
        hu = [jnp.dot(xb[t], bufs[1][s], preferred_element_type=f32) for t, bufs, s in units]
        mid = [(g * jax.nn.sigmoid(g) * u).astype(bf16) for g, u in zip(hg, hu)]
        ys = [jnp.dot(m, bufs[2][s], preferred_element_type=f32)
              for m, (t, bufs, s) in zip(mid, units)]
        for t in range(MOE_TILES):
            ffn = info[t][:, 2:3] * ys[2 * t] + info[t][:, 3:4] * ys[2 * t + 1]
            y_ref[rows[t], :] = _layer_norm(ALPHA * x[t] + ffn, g2_ref[...], b2_ref[...])

    @pl.when(first >= n_used)
    def _():
        y_ref[...] = jnp.zeros_like(y_ref)


def _experts(tile_ea, tile_eb, n_used, xs, wg, wu, wd, g2, b2):
    nt = xs.shape[0] // TM_MOE

    def ring_plan(tile_e):
        opens = jnp.concatenate([jnp.ones((1,), jnp.int32), (tile_e[1:] != tile_e[:-1]).astype(jnp.int32)])
        return opens, (jnp.cumsum(opens) - 1) % W_SLOTS

    open_a, slot_a = ring_plan(tile_ea)
    open_b, slot_b = ring_plan(tile_eb)
    assert nt % MOE_TILES == 0
    n_used = ((n_used + MOE_TILES - 1) // MOE_TILES) * MOE_TILES
    const = lambda i, *_: (0, 0)
    xmap = lambda i, ta, tb, ca, cb, sa, sb, nu: (jnp.minimum(i, nu[0] // MOE_TILES - 1), 0)
    w_in = pltpu.VMEM((W_SLOTS, D_MODEL, D_EXPERT), bf16)
    w_out = pltpu.VMEM((W_SLOTS, D_EXPERT, D_MODEL), bf16)
    return pl.pallas_call(
        _expert_kernel,
        grid_spec=pltpu.PrefetchScalarGridSpec(
            num_scalar_prefetch=7,
            grid=(nt // MOE_TILES,),
            in_specs=[
                pl.BlockSpec((MOE_TILES * TM_MOE, X_ROW), xmap),
                pl.BlockSpec(memory_space=pl.ANY), pl.BlockSpec(memory_space=pl.ANY),
                pl.BlockSpec(memory_space=pl.ANY),
                pl.BlockSpec((1, D_MODEL), const), pl.BlockSpec((1, D_MODEL), const),
            ],
            out_specs=pl.BlockSpec((MOE_TILES * TM_MOE, D_MODEL), lambda i, *_: (i, 0)),
            scratch_shapes=[w_in, w_in, w_out, w_in, w_in, w_out,
                            pltpu.SemaphoreType.DMA((W_SLOTS,)), pltpu.SemaphoreType.DMA((W_SLOTS,))],
        ),
        out_shape=jax.ShapeDtypeStruct((nt * TM_MOE, D_MODEL), f32),
        compiler_params=_cparams(1),
        name="experts",
    )(tile_ea, tile_eb, open_a, open_b, slot_a.astype(jnp.int32), slot_b.astype(jnp.int32), n_used,
      xs, wg, wu, wd, g2, b2)


def _collect_kernel(pos_ref, posn_ref, ys_ref, op_ref, os_ref, ybuf, sem, *, n_p):
    i = pl.program_id(0)
    n = pl.num_programs(0)
    slot = i % 2

    def gather(p_ref, s):
        def start(r, c):
            _row_copy(ys_ref, p_ref[0, 0, r], ybuf.at[s], r, sem.at[s]).start()
            return c

        lax.fori_loop(0, TM, start, 0, unroll=8)

    @pl.when(i == 0)
    def _():
        gather(pos_ref, 0)

    @pl.when(i + 1 < n)
    def _():
        gather(posn_ref, 1 - slot)

    _tile_rows_wait(ys_ref, ybuf.at[slot], sem.at[slot])

    @pl.when(i < n_p)
    def _():
        op_ref[...] = ybuf[slot]

    @pl.when(i >= n_p)
    def _():
        os_ref[...] = ybuf[slot]


def _collect(pos, ys, n_p):
    n = pos.shape[0]
    n_s = n - n_p
    return pl.pallas_call(
        functools.partial(_collect_kernel, n_p=n_p),
        grid=(n,),
        in_specs=[
            pl.BlockSpec((1, 1, TM), lambda i: (i, 0, 0), memory_space=pltpu.SMEM),
            pl.BlockSpec((1, 1, TM), lambda i: (jnp.minimum(i + 1, n - 1), 0, 0), memory_space=pltpu.SMEM),
            pl.BlockSpec(memory_space=pl.ANY),
        ],
        out_specs=[
            pl.BlockSpec((TM, D_MODEL), lambda i: (jnp.minimum(i, n_p - 1), 0)),
            pl.BlockSpec((TM, D_MODEL), lambda i: (jnp.maximum(i - n_p, 0), 0)),
        ],
        out_shape=[
            jax.ShapeDtypeStruct((n_p * TM, D_MODEL), f32),
            jax.ShapeDtypeStruct((n_s * TM, D_MODEL), f32),
        ],
        scratch_shapes=[pltpu.VMEM((2, TM, D_MODEL), f32), pltpu.SemaphoreType.DMA((2,))],
        compiler_params=_cparams(1),
        name="collect",
    )(pos, pos, ys)


def _block_diag(w):
    n, d, _ = w.shape
    eye = jnp.eye(n, dtype=w.dtype)
    return (eye[:, None, :, None] * w[:, :, None, :]).reshape(n * d, n * d)


def _pair_walk(n):
    total = n * (n - 1) // 2

    def extend(path, used):
        if len(path) == total:
            return path
        a, b = path[-1]
        for c in range(n):
            for nxt in ((a, c), (c, b)):
                key = frozenset(nxt)
                if len(key) == 2 and key not in used:
                    out = extend(path + [nxt], used | {key})
                    if out:
                        return out
        return None

    return extend([(0, 1)], {frozenset((0, 1))})


def _pad_history(buf):
    return jnp.pad(buf, ((0, 0), (SUBLANES - (CONV_W - 1), 0), (0, 0)))


def kernel(x_prompt, x_sample, state_rg_h, state_rg_conv, state_m_C, state_m_n, state_m_m, state_m_conv, w_in, rg_conv_w, rg_conv_b, rg_w_a, rg_b_a, rg_w_x, rg_b_x, rg_lambda, m_conv_w, m_conv_b, m_w_q, m_w_k, m_w_v, m_b_i, m_b_f, m_norm_g, m_skip, w_out, ln1_g, ln1_b, ln2_g, ln2_b, moe_w_group, moe_b_group, moe_w_expert, moe_b_expert, moe_w_gate, moe_w_up, moe_w_down):
    BP, TP, _ = x_prompt.shape
    BS, TS, _ = x_sample.shape
    n_prompt, n_sample = BP * TP, BS * TS
    xp = x_prompt.reshape(n_prompt, D_MODEL)
    xs = x_sample.reshape(n_sample, D_MODEL)
    l = 0

    w_in_p = jnp.pad(w_in[l], ((0, 0), (0, N_PROJ - w_in.shape[-1]))).astype(bf16)
    wa = _block_diag(rg_w_a[l]).astype(bf16)
    wx = _block_diag(rg_w_x[l]).astype(bf16)
    row = lambda v: v.reshape(1, -1)
    bif = jnp.pad(jnp.concatenate([m_b_i[l], m_b_f[l]]), (0, LANES - 2 * M_HEADS)).reshape(1, LANES)
    wqk = jnp.concatenate([m_w_q[l], m_w_k[l]], axis=-1).astype(bf16)
    wv = m_w_v[l].astype(bf16)
    wkT = m_w_k[l].transpose(0, 2, 1).astype(bf16)
    w_route = jnp.pad(jnp.concatenate([moe_w_group[l], moe_w_expert[l]], axis=1),
                      ((0, 0), (0, LANES - N_GROUPS - N_EXPERTS)))
    wr_hi = w_route.astype(bf16)
    wr_lo = (w_route - wr_hi.astype(f32)).astype(bf16)
    wr = jnp.concatenate([wr_hi, wr_lo], axis=1)
    b_route = jnp.pad(jnp.concatenate([moe_b_group[l], moe_b_expert[l]]),
                      (0, LANES - N_GROUPS - N_EXPERTS)).reshape(1, LANES)

    proj_p, proj_s = _inproj(xp, xs, w_in_p)

    rg_args = (rg_conv_w[l], row(rg_conv_b[l]), wa, row(rg_b_a[l]), wx, row(rg_b_x[l]), row(rg_lambda[l]))
    zeros = lambda *s: jnp.zeros(s, f32)
    s_blk = M_ROWS // TS
    yrg_p, p_rg_h, p_rg_conv = _rglru(proj_p, 0, BP, TP, 1, RG_ROWS_PROMPT, zeros(BP, SUBLANES, D_RG),
                                      zeros(BP, 1, D_RG), *rg_args)
    yrg_s, s_rg_h, s_rg_conv = _rglru(proj_s, 0, BS, TS, s_blk, TS, _pad_history(state_rg_conv[l]),
                                      state_rg_h[l].reshape(BS, 1, D_RG), *rg_args)

    m_args = (m_conv_w[l], row(m_conv_b[l]), wqk, wv, wkT, bif, row(m_norm_g[l]), row(m_skip[l]))
    m0_p = jnp.full((SUBLANES, BP * LANES), M_INIT, f32)
    ym_p, p_m_C, p_m_n, p_m_m, p_m_conv, wg_b, wu_b, wd_b = _mlstm(
        proj_p, BP, TP, 1, M_ROWS, M_BLOCKS_PROMPT, zeros(BP, SUBLANES, D_M), zeros(BP, M_HEADS, M_DH, M_DH),
        zeros(BP, M_HEADS, 1, M_DH), m0_p, *m_args, riders=(moe_w_gate[l], moe_w_up[l], moe_w_down[l]))
    m0_s = jnp.pad(jnp.repeat(state_m_m[l].T, TS, axis=1), ((0, SUBLANES - M_HEADS), (0, 0)))
    ym_s, s_m_C, s_m_n, s_m_m, s_m_conv = _mlstm(
        proj_s, BS, TS, s_blk, TS, M_BLOCKS_SAMPLE, _pad_history(state_m_conv[l]), state_m_C[l],
        state_m_n[l].reshape(BS, M_HEADS, 1, M_DH), m0_s, *m_args)

    x1, cnt = _outproj(xp, xs, yrg_p, yrg_s, ym_p, ym_s, w_out[l].astype(bf16),
                       row(ln1_g[l]), row(ln1_b[l]), wr, b_route)

    n_tok = n_prompt + n_sample
    n_tiles = n_tok // TM_MOE + N_CLASSES
    lay_cls, lay_a, lay_b = [], [], []
    for g in range(N_GROUPS):
        for ja, jb in _pair_walk(EXPERTS_PER_GROUP):
            assert ja < jb
            n_first = (EXPERTS_PER_GROUP - 1) * ja - ja * (ja - 1) // 2
            lay_cls.append(PAIRS_PER_GROUP * g + n_first + jb - ja - 1)
            lay_a.append(g * EXPERTS_PER_GROUP + ja)
            lay_b.append(g * EXPERTS_PER_GROUP + jb)
    classes = jnp.arange(N_CLASSES, dtype=jnp.int32)
    in_slot = jnp.array(lay_cls, jnp.int32)[:, None] == classes
    counts = jnp.sum(jnp.where(in_slot, cnt[0, :N_CLASSES].astype(jnp.int32), 0), axis=1)
    padded = ((counts + TM_MOE - 1) // TM_MOE) * TM_MOE
    ends = jnp.cumsum(padded)
    offs = ends - padded
    offs_of_cls = jnp.sum(jnp.where(in_slot, offs[:, None], 0), axis=0)
    cls = x1[:, D_MODEL].astype(jnp.int32)
    rank = x1[:, D_MODEL + 1].astype(jnp.int32)
    pos = jnp.sum(jnp.where(cls[:, None] == classes, offs_of_cls, 0), axis=-1) + rank
    pos = pos.reshape(n_tok // TM, 1, TM)
    tiles = jnp.arange(n_tiles, dtype=jnp.int32)
    tile_slot = jnp.minimum(jnp.sum(tiles[:, None] >= (ends // TM_MOE)[None, :], axis=1), N_CLASSES - 1)
    on_slot = tile_slot[:, None] == classes
    tile_ea = jnp.sum(jnp.where(on_slot, jnp.array(lay_a, jnp.int32), 0), axis=-1).astype(jnp.int32)
    tile_eb = jnp.sum(jnp.where(on_slot, jnp.array(lay_b, jnp.int32), 0), axis=-1).astype(jnp.int32)
    n_used = (ends[-1] // TM_MOE).reshape(1).astype(jnp.int32)
    pad_row0 = ((offs + counts) // SUBLANES) * SUBLANES
    pad_plan = jnp.concatenate([jnp.stack([pad_row0, (ends - pad_row0) // SUBLANES], axis=1).reshape(-1),
                                     n_used]).astype(jnp.int32)

    x_sorted = _dispatch(pad_plan, pos, x1, n_tiles * TM_MOE)
    y_sorted = _experts(tile_ea, tile_eb, n_used, x_sorted, wg_b, wu_b, wd_b,
                        row(ln2_g[l]), row(ln2_b[l]))
    y_p, y_s = _collect(pos, y_sorted, n_prompt // TM)

    return (y_p.reshape(BP, TP, D_MODEL), y_s.reshape(BS, TS, D_MODEL),
            p_rg_h.reshape(1, BP, D_RG), p_rg_conv[None], p_m_C[None], p_m_n.reshape(1, BP, M_HEADS, M_DH),
            p_m_m[None], p_m_conv[None],
            s_rg_h.reshape(1, BS, D_RG), s_rg_conv[None], s_m_C[None], s_m_n.reshape(1, BS, M_HEADS, M_DH),
            s_m_m[None], s_m_conv[None])
```

```python
import functools

import jax
import jax.numpy as jnp
from jax import lax
from jax.experimental import pallas as pl
from jax.experimental.pallas import tpu as pltpu

f32 = jnp.float32
bf16 = jnp.bfloat16

D_MODEL = 1024
D_RG = 512
RG_C = 8.0
D_M = 512
M_HEADS = 4
M_DH = 128
CONV_W = 4
N_GROUPS = 4
EXPERTS_PER_GROUP = 8
N_EXPERTS = 32
D_EXPERT = 256
ALPHA = 2.0 ** 0.25
LN_EPS = 1e-5
M_INIT = -1.0e4

LANES = 128
SUBLANES = 8
TM = 512
TM_MOVE = 1024
TM_MOE = 128
MOE_TILES = 4
W_AHEAD = 4
W_SLOTS = W_AHEAD + MOE_TILES
PAIRS_PER_GROUP = EXPERTS_PER_GROUP * (EXPERTS_PER_GROUP - 1) // 2
N_CLASSES = N_GROUPS * PAIRS_PER_GROUP
X_ROW = D_MODEL + LANES
N_PROJ = 2 * D_RG + 2 * D_M + LANES
RG_ROWS_PROMPT = 512
M_ROWS = 128
M_BLOCKS_PROMPT = 8
M_BLOCKS_SAMPLE = 2
V7X_VMEM_BYTES = 64 * 1024 * 1024
VMEM_LIMIT = V7X_VMEM_BYTES - 8 * 1024 * 1024


def _cparams(n_axes):
    return pltpu.CompilerParams(dimension_semantics=("arbitrary",) * n_axes,
                                vmem_limit_bytes=VMEM_LIMIT)


def _inproj_kernel(xp_ref, xs_ref, w_ref, op_ref, os_ref, *, n_p):
    i = pl.program_id(0)

    def run(x_ref, o_ref):
        o_ref[...] = jnp.dot(x_ref[...].astype(bf16), w_ref[...], preferred_element_type=f32)

    @pl.when(i < n_p)
    def _():
        run(xp_ref, op_ref)

    @pl.when(i >= n_p)
    def _():
        run(xs_ref, os_ref)


def _inproj(xp, xs, w):
    n_p, n_s = xp.shape[0] // TM, xs.shape[0] // TM
    pmap = lambda i: (jnp.minimum(i, n_p - 1), 0)
    smap = lambda i: (jnp.maximum(i - n_p, 0), 0)
    return pl.pallas_call(
        functools.partial(_inproj_kernel, n_p=n_p),
        grid=(n_p + n_s,),
        in_specs=[
            pl.BlockSpec((TM, D_MODEL), pmap),
            pl.BlockSpec((TM, D_MODEL), smap),
            pl.BlockSpec((D_MODEL, N_PROJ), lambda i: (0, 0)),
        ],
        out_specs=[pl.BlockSpec((TM, N_PROJ), pmap), pl.BlockSpec((TM, N_PROJ), smap)],
        out_shape=[jax.ShapeDtypeStruct((n_p * TM, N_PROJ), f32),
                   jax.ShapeDtypeStruct((n_s * TM, N_PROJ), f32)],
        compiler_params=_cparams(1),
        name="inproj",
    )(xp, xs, w)


def _causal_conv(tail_scr, x, cw_ref, cb_ref, S, L, C):
    tail = tail_scr[...]
    sub = lax.broadcasted_iota(jnp.int32, (S, SUBLANES, C), 1)
    acc = cb_ref[...] + cw_ref[CONV_W - 1:CONV_W, :] * x
    for d in range(1, CONV_W):
        back = pltpu.roll(x, d, 0).reshape(S, L, C)
        head = jnp.where(sub < d, pltpu.roll(tail, d, 1), back[:, 0:SUBLANES, :])
        if L > SUBLANES:
            back = jnp.concatenate([head, back[:, SUBLANES:, :]], axis=1)
        else:
            back = head
        acc = acc + cw_ref[CONV_W - 1 - d:CONV_W - d, :] * back.reshape(S * L, C)
    tail_scr[...] = x.reshape(S, L, C)[:, L - SUBLANES:, :]
    return acc


def _softplus(x):
    return jnp.maximum(x, 0.0) + jnp.log1p(jnp.exp(-jnp.abs(x)))


def _rglru_kernel(x_ref, g_ref, buf_ref, h0_ref, cw_ref, cb_ref, wa_ref, ba_ref, wx_ref, bx_ref,
                  lam_ref, y_ref, hN_ref, cN_ref, xs_scr, *, S, L):
    R = S * L
    t = pl.program_id(1)

    @pl.when(t == 0)
    def _():
        xs_scr[...] = buf_ref[...]
        hN_ref[...] = h0_ref[...]

    x = x_ref[...]
    xc = _causal_conv(xs_scr, x, cw_ref, cb_ref, S, L, D_RG)
    xcb = xc.astype(bf16)
    r = jax.nn.sigmoid(jnp.dot(xcb, wa_ref[...], preferred_element_type=f32) + ba_ref[...])
    ig = jax.nn.sigmoid(jnp.dot(xcb, wx_ref[...], preferred_element_type=f32) + bx_ref[...])
    log_a = (-RG_C) * r * _softplus(-lam_ref[...])
    a = jnp.exp(log_a)
    th = jnp.tanh(log_a)
    u = jnp.sqrt(-2.0 * th / (1.0 - th)) * ig * xc

    n_grp, grp_per_seq = R // SUBLANES, L // SUBLANES
    a3 = a.reshape(n_grp, SUBLANES, D_RG)
    u3 = u.reshape(n_grp, SUBLANES, D_RG)
    sub = lax.broadcasted_iota(jnp.int32, (n_grp, SUBLANES, D_RG), 1)
    s = 1
    while s < SUBLANES:
        ok = sub >= s
        a_sh = pltpu.roll(a3, s, 1)
        u_sh = pltpu.roll(u3, s, 1)
        u3 = jnp.where(ok, a3 * u_sh + u3, u3)
        a3 = jnp.where(ok, a3 * a_sh, a3)
        s *= 2
    h0 = hN_ref[...]
    groups = []
    for kg in range(n_grp):
        carry = h0[kg // grp_per_seq] if kg % grp_per_seq == 0 else groups[-1][SUBLANES - 1:SUBLANES, :]
        groups.append(a3[kg] * carry + u3[kg])
    h = jnp.concatenate(groups, axis=0)

    y_ref[...] = h * jax.nn.gelu(g_ref[...], approximate=True)
    hN_ref[...] = h.reshape(S, L, D_RG)[:, L - 1:L, :]
    cN_ref[...] = xs_scr[:, SUBLANES - (CONV_W - 1):SUBLANES, :]


def _rglru(proj, row0, B, T, S, L, buf8, h0, cw, cb, wa, ba, wx, bx, lam):
    R = S * L
    nb, nt = B // S, T // L
    blk0 = row0 // R
    row_map = lambda b, t: (blk0 + b * nt + t, 0)
    const2 = lambda b, t: (0, 0)
    return pl.pallas_call(
        functools.partial(_rglru_kernel, S=S, L=L),
        grid=(nb, nt),
        in_specs=[
            pl.BlockSpec((R, D_RG), row_map),
            pl.BlockSpec((R, D_RG), lambda b, t: (blk0 + b * nt + t, 1)),
            pl.BlockSpec((S, SUBLANES, D_RG), lambda b, t: (b, 0, 0)),
            pl.BlockSpec((S, 1, D_RG), lambda b, t: (b, 0, 0)),
            pl.BlockSpec((CONV_W, D_RG), const2),
            pl.BlockSpec((1, D_RG), const2),
            pl.BlockSpec((D_RG, D_RG), const2),
            pl.BlockSpec((1, D_RG), const2),
            pl.BlockSpec((D_RG, D_RG), const2),
            pl.BlockSpec((1, D_RG), const2),
            pl.BlockSpec((1, D_RG), const2),
        ],
        out_specs=[
            pl.BlockSpec((R, D_RG), lambda b, t: (b * nt + t, 0)),
            pl.BlockSpec((S, 1, D_RG), lambda b, t: (b, 0, 0)),
            pl.BlockSpec((S, CONV_W - 1, D_RG), lambda b, t: (b, 0, 0)),
        ],
        out_shape=[
            jax.ShapeDtypeStruct((B * T, D_RG), f32),
            jax.ShapeDtypeStruct((B, 1, D_RG), f32),
            jax.ShapeDtypeStruct((B, CONV_W - 1, D_RG), f32),
        ],
        scratch_shapes=[pltpu.VMEM((S, SUBLANES, D_RG), f32)],
        compiler_params=_cparams(2),
        name="rglru",
    )(proj, proj, buf8, h0, cw, cb, wa, ba, wx, bx, lam)


def _seg_scan(x, op, fill, tin, L, reverse=False):
    s = 1
    while s < L:
        if reverse:
            sh = pltpu.roll(x, LANES - s, 1)
            ok = tin < L - s
        else:
            sh = pltpu.roll(x, s, 1)
            ok = tin >= s
        x = op(x, jnp.where(ok, sh, fill))
        s *= 2
    return x


def _mlstm_kernel(*refs, S, L, G, n_riders):
    (x_ref, z_ref, gt_ref, buf_ref, c0_ref, n0_ref, m0_ref,
     cw_ref, cb_ref, wqk_ref, wv_ref, wkT_ref, bif_ref, ng_ref, sk_ref) = refs[:15]
    rider_in = refs[15:15 + n_riders]
    y_ref, cN_ref, nN_ref, mN_ref, bN_ref = refs[15 + n_riders:20 + n_riders]
    rider_out = refs[20 + n_riders:20 + 2 * n_riders]
    xs_scr, m_scr = refs[20 + 2 * n_riders:]
    for src, dst in zip(rider_in, rider_out):
        dst[...] = src[...].astype(bf16)

    R = S * L
    GS, GR = G * S, G * R
    j = pl.program_id(1)

    @pl.when(j == 0)
    def _():
        xs_scr[...] = buf_ref[...]
        cN_ref[...] = c0_ref[...]
        nN_ref[...] = n0_ref[...]
        m_scr[...] = m0_ref[...]

    x = x_ref[...].reshape(GR, D_M)
    xc = _causal_conv(xs_scr, x, cw_ref, cb_ref, GS, L, D_M)
    xa = xc * jax.nn.sigmoid(xc)
    xab = xa.astype(bf16)
    xb = x.astype(bf16)

    il_parts, f_parts = [], []
    for g in range(G):
        gT = (gt_ref[g] + bif_ref[...]).T
        il_parts.append(gT[0:SUBLANES, :])
        f_parts.append(pltpu.roll(gT[0:SUBLANES, :], M_HEADS, 0))
    il = jnp.concatenate(il_parts, axis=0)
    fl = -_softplus(-jnp.concatenate(f_parts, axis=0))
    tin = lax.broadcasted_iota(jnp.int32, (G * SUBLANES, LANES), 1) % L
    bcum = _seg_scan(fl, jnp.add, 0.0, tin, L)
    a = il - bcum
    m_prev = m_scr[...].reshape(G * SUBLANES, LANES)
    big_m = jnp.maximum(m_prev, _seg_scan(a, jnp.maximum, -jnp.inf, tin, L))
    m_t = bcum + big_m
    if S == 1:
        m_last = jnp.broadcast_to(big_m[:, LANES - 1:LANES], big_m.shape)
    else:
        m_last = _seg_scan(big_m, jnp.maximum, -jnp.inf, tin, L, reverse=True)
    rows = [big_m, jnp.exp(m_prev - big_m), jnp.exp(-m_t), jnp.exp(a - m_last), jnp.exp(m_prev - m_last)]
    mN_ref[...] = m_t.reshape(G, SUBLANES, LANES)
    m_scr[...] = jnp.broadcast_to(m_t[:, LANES - 1:LANES], m_t.shape).reshape(G, SUBLANES, LANES)
    pad_rows = jnp.zeros((LANES - len(rows) * SUBLANES, LANES), f32)
    cols = []
    for g in range(G):
        gs = slice(g * SUBLANES, (g + 1) * SUBLANES)
        cols.append(jnp.concatenate([r[gs] for r in rows] + [pad_rows], axis=0).T)

    def col(g, q, h):
        return cols[g][:, SUBLANES * q + h:SUBLANES * q + h + 1]

    qs, ks, vs, kTs = [], [], [], []
    for h in range(M_HEADS):
        hs = slice(h * M_DH, (h + 1) * M_DH)
        qk_h = jnp.dot(xab[:, hs], wqk_ref[h], preferred_element_type=f32)
        qs.append(qk_h[:, 0:M_DH])
        ks.append(qk_h[:, M_DH:2 * M_DH] * (M_DH ** -0.5))
        vs.append(jnp.dot(xb[:, hs], wv_ref[h], preferred_element_type=f32))
        if S == 1:
            kT_h = lax.dot_general(wkT_ref[h], xab[:, hs], (((1,), (1,)), ((), ())),
                                   preferred_element_type=f32)
            kTs.append((kT_h * (M_DH ** -0.5)).astype(bf16))

    ti = lax.broadcasted_iota(jnp.int32, (R, R), 0)
    si = lax.broadcasted_iota(jnp.int32, (R, R), 1)
    mask = (si <= ti) & ((ti // L) == (si // L))
    ones_b = jnp.ones((R, M_DH), bf16)
    pairs = [(g, h) for g in range(G) for h in range(M_HEADS)]
    blk = lambda g: slice(g * R, (g + 1) * R)
    seqs = [slice(b * L, (b + 1) * L) for b in range(S)]
    q = {(g, h): qs[h][blk(g)] for g, h in pairs}
    k = {(g, h): ks[h][blk(g)] for g, h in pairs}
    v = {(g, h): vs[h][blk(g)] for g, h in pairs}
    qb = {p: q[p].astype(bf16) for p in pairs}
    kb = {p: k[p].astype(bf16) for p in pairs}
    qk = {p: lax.dot_general(qb[p], kb[p], (((1,), (1,)), ((), ())), preferred_element_type=f32)
          for p in pairs}
    sm = {}
    for g, h in pairs:
        a_row = a[g * SUBLANES + h:g * SUBLANES + h + 1, :]
        decay = jnp.exp(jnp.where(mask, a_row - col(g, 0, h), -jnp.inf))
        sm[g, h] = (qk[g, h] * decay).astype(bf16)
    nd = {p: jnp.dot(sm[p], jnp.concatenate([v[p].astype(bf16), ones_b], axis=1),
                     preferred_element_type=f32) for p in pairs}
    c_old = {(g, h, b): cN_ref[g * S + b, h] for g, h in pairs for b in range(S)}
    n_old = {(g, h, b): nN_ref[g * S + b, h] for g, h in pairs for b in range(S)}
    q_c = {(g, h, b): jnp.dot(qb[g, h][seqs[b]], c_old[g, h, b].astype(bf16), preferred_element_type=f32)
           for g, h in pairs for b in range(S)}
    hh = {}
    for g, h in pairs:
        g_col, e_col = col(g, 1, h), col(g, 2, h)
        pieces = []
        for b, rs in enumerate(seqs):
            q_n = jnp.sum(q[g, h][rs] * n_old[g, h, b], axis=1, keepdims=True)
            num = nd[g, h][rs, 0:M_DH] + g_col[rs] * q_c[g, h, b]
            den = nd[g, h][rs, M_DH:2 * M_DH] + g_col[rs] * q_n
            pieces.append(num / jnp.maximum(jnp.abs(den), e_col[rs]))
        hh[g, h] = pieces[0] if S == 1 else jnp.concatenate(pieces, axis=0)
    mu = {p: jnp.mean(hh[p], axis=1, keepdims=True) for p in pairs}
    var = {p: jnp.mean(jnp.square(hh[p] - mu[p]), axis=1, keepdims=True) for p in pairs}
    hn_blocks = [jnp.concatenate([(hh[g, h] - mu[g, h]) * lax.rsqrt(var[g, h] + LN_EPS)
                                  for h in range(M_HEADS)], axis=1) for g in range(G)]
    new_c, new_n = [], []
    for g, h in pairs:
        w_col = col(g, 3, h)
        wv = (w_col * v[g, h]).astype(bf16)
        wk = w_col * k[g, h]
        for b, rs in enumerate(seqs):
            g_end = col(g, 4, h)[(b + 1) * L - 1:(b + 1) * L, :]
            if S == 1:
                kv = jnp.dot(kTs[h][:, blk(g)], wv, preferred_element_type=f32)
            else:
                kv = lax.dot_general(kb[g, h][rs], wv[rs], (((0,), (0,)), ((), ())),
                                     preferred_element_type=f32)
            new_c.append((g * S + b, h, g_end * c_old[g, h, b] + kv))
            new_n.append((g * S + b, h, g_end * n_old[g, h, b] + jnp.sum(wk[rs], axis=0, keepdims=True)))

    hn = jnp.concatenate(hn_blocks, axis=0) * ng_ref[...]
    y = jax.nn.sigmoid(z_ref[...].reshape(GR, D_M)) * (hn + sk_ref[...] * xa)
    y_ref[...] = y.reshape(G, R, D_M)
    for sq, h, val in new_c:
        cN_ref[sq, h] = val
    for sq, h, val in new_n:
        nN_ref[sq, h] = val
    bN_ref[...] = xs_scr[:, SUBLANES - (CONV_W - 1):SUBLANES, :]


def _mlstm(proj, B, T, S, L, G, buf8, c0, n0, m0_lanes, cw, cb, wqk, wv, wkT, bif, ng, sk, riders=()):
    R = S * L
    assert R == M_ROWS == LANES
    nblk, nc = B // S, T // L
    nb = nblk // G
    assert nblk * S == B and nc * L == T and nb * G == nblk and (S == 1 or nc == 1)
    steps = nb * nc
    assert all(r.shape[0] % steps == 0 for r in riders)
    rider_specs = [pl.BlockSpec((r.shape[0] // steps,) + r.shape[1:], lambda b, j: (b * nc + j, 0, 0))
                   for r in riders]
    proj4 = proj.reshape(nblk, nc, R, N_PROJ)
    m0_3 = m0_lanes.reshape(SUBLANES, nblk, LANES).transpose(1, 0, 2)
    const2 = lambda b, j: (0, 0)
    const3 = lambda b, j: (0, 0, 0)
    gate_blk = (2 * D_RG + 2 * D_M) // LANES
    GS = G * S
    outs = pl.pallas_call(
        functools.partial(_mlstm_kernel, S=S, L=L, G=G, n_riders=len(riders)),
        grid=(nb, nc),
        in_specs=[
            pl.BlockSpec((G, None, R, D_M), lambda b, j: (b, j, 0, 2)),
            pl.BlockSpec((G, None, R, D_M), lambda b, j: (b, j, 0, 3)),
            pl.BlockSpec((G, None, R, LANES), lambda b, j: (b, j, 0, gate_blk)),
            pl.BlockSpec((GS, SUBLANES, D_M), lambda b, j: (b, 0, 0)),
            pl.BlockSpec((GS, M_HEADS, M_DH, M_DH), lambda b, j: (b, 0, 0, 0)),
            pl.BlockSpec((GS, M_HEADS, 1, M_DH), lambda b, j: (b, 0, 0, 0)),
            pl.BlockSpec((G, SUBLANES, LANES), lambda b, j: (b, 0, 0)),
            pl.BlockSpec((CONV_W, D_M), const2),
            pl.BlockSpec((1, D_M), const2),
            pl.BlockSpec((M_HEADS, M_DH, 2 * M_DH), const3),
            pl.BlockSpec((M_HEADS, M_DH, M_DH), const3),
            pl.BlockSpec((M_HEADS, M_DH, M_DH), const3),
            pl.BlockSpec((1, LANES), const2),
            pl.BlockSpec((1, D_M), const2),
            pl.BlockSpec((1, D_M), const2),
        ] + rider_specs,
        out_specs=[
            pl.BlockSpec((G, None, R, D_M), lambda b, j: (b, j, 0, 0)),
            pl.BlockSpec((GS, M_HEADS, M_DH, M_DH), lambda b, j: (b, 0, 0, 0)),
            pl.BlockSpec((GS, M_HEADS, 1, M_DH), lambda b, j: (b, 0, 0, 0)),
            pl.BlockSpec((G, None, SUBLANES, LANES), lambda b, j: (b, j, 0, 0)),
            pl.BlockSpec((GS, CONV_W - 1, D_M), lambda b, j: (b, 0, 0)),
        ] + rider_specs,
        out_shape=[
            jax.ShapeDtypeStruct((nblk, nc, R, D_M), f32),
            jax.ShapeDtypeStruct((B, M_HEADS, M_DH, M_DH), f32),
            jax.ShapeDtypeStruct((B, M_HEADS, 1, M_DH), f32),
            jax.ShapeDtypeStruct((nblk, nc, SUBLANES, LANES), f32),
            jax.ShapeDtypeStruct((B, CONV_W - 1, D_M), f32),
        ] + [jax.ShapeDtypeStruct(r.shape, bf16) for r in riders],
        scratch_shapes=[pltpu.VMEM((GS, SUBLANES, D_M), f32), pltpu.VMEM((G, SUBLANES, LANES), f32)],
        compiler_params=_cparams(2),
        name="mlstm",
    )(proj4, proj4, proj4, buf8, c0, n0, m0_3, cw, cb, wqk, wv, wkT, bif, ng, sk, *riders)
    y, c_new, n_new, m_t, b_new = outs[:5]
    m_last = m_t[:, nc - 1, :M_HEADS, :].reshape(nblk, M_HEADS, S, L)[:, :, :, L - 1]
    m_last = m_last.transpose(0, 2, 1).reshape(B, M_HEADS)
    return (y.reshape(B * T, D_M), c_new, n_new, m_last, b_new) + tuple(outs[5:])


def _layer_norm(x, g, b):
    mu = jnp.mean(x, axis=-1, keepdims=True)
    var = jnp.mean(jnp.square(x - mu), axis=-1, keepdims=True)
    return (x - mu) * lax.rsqrt(var + LN_EPS) * g + b


def _first_lane_of_max(vals, lane_f):
    vmax = jnp.max(vals, axis=1, keepdims=True)
    idx = jnp.min(jnp.where(vals == vmax, lane_f, float(LANES)), axis=1, keepdims=True)
    return vmax, idx


def _outproj_kernel(xp_ref, xs_ref, rgp_ref, rgs_ref, mp_ref, ms_ref, wo_ref, g1_ref, b1_ref,
                    wr_ref, br_ref, x1_ref, cnt_ref, *, n_p):
    i = pl.program_id(0)

    @pl.when(i == 0)
    def _():
        cnt_ref[...] = jnp.zeros_like(cnt_ref)

    def run(x_ref, rg_ref, m_ref):
        heads = jnp.concatenate([rg_ref[...].astype(bf16), m_ref[...].astype(bf16)], axis=1)
        mix = jnp.dot(heads, wo_ref[...], preferred_element_type=f32)
        x1 = _layer_norm(ALPHA * x_ref[...] + mix, g1_ref[...], b1_ref[...])
        x1_ref[:, 0:D_MODEL] = x1

        hi = x1.astype(bf16)
        lo = (x1 - hi.astype(f32)).astype(bf16)
        hi_terms = jnp.dot(hi, wr_ref[...], preferred_element_type=f32)
        lo_term = jnp.dot(lo, wr_ref[:, 0:LANES], preferred_element_type=f32)
        lg = hi_terms[:, 0:LANES] + hi_terms[:, LANES:2 * LANES] + lo_term + br_ref[...]
        lane = lax.broadcasted_iota(jnp.int32, (TM, LANES), 1)
        lane_f = lane.astype(f32)
        neg = -jnp.inf
        gl = jnp.where(lane < N_GROUPS, lg, neg)
        gmax, gidx = _first_lane_of_max(gl, lane_f)
        p_g = 1.0 / jnp.sum(jnp.exp(gl - gmax), axis=1, keepdims=True)
        e_lo = float(N_GROUPS) + float(EXPERTS_PER_GROUP) * gidx
        el = jnp.where((lane_f >= e_lo) & (lane_f < e_lo + float(EXPERTS_PER_GROUP)), lg, neg)
        v1, i1 = _first_lane_of_max(el, lane_f)
        v2, i2 = _first_lane_of_max(jnp.where(lane_f == i1, neg, el), lane_f)
        d = jnp.exp(v2 - v1)
        w1 = p_g / (1.0 + d)
        w2 = p_g * d / (1.0 + d)

        first_low = i1 < i2
        j_lo = jnp.minimum(i1, i2) - e_lo
        j_hi = jnp.maximum(i1, i2) - e_lo
        w_lo = jnp.where(first_low, w1, w2)
        w_hi = jnp.where(first_low, w2, w1)
        n_first = float(EXPERTS_PER_GROUP - 1) * j_lo - 0.5 * j_lo * (j_lo - 1.0)
        cls = float(PAIRS_PER_GROUP) * gidx + n_first + (j_hi - j_lo - 1.0)

        hot = lane_f == cls
        ti = lax.broadcasted_iota(jnp.int32, (TM, TM), 0)
        si = lax.broadcasted_iota(jnp.int32, (TM, TM), 1)
        tri = (si <= ti).astype(bf16)
        cum = jnp.dot(tri, hot.astype(bf16), preferred_element_type=f32)
        rank = jnp.sum(jnp.where(hot, cum - 1.0 + cnt_ref[...], 0.0), axis=1, keepdims=True)
        cnt_ref[...] = cnt_ref[...] + cum[TM - 1:TM, :]

        info = jnp.zeros((TM, LANES), f32)
        for c, val in enumerate((cls, rank, w_lo, w_hi)):
            info = jnp.where(lane == c, val, info)
        x1_ref[:, D_MODEL:D_MODEL + LANES] = info

    @pl.when(i < n_p)
    def _():
        run(xp_ref, rgp_ref, mp_ref)

    @pl.when(i >= n_p)
    def _():
        run(xs_ref, rgs_ref, ms_ref)


def _outproj(xp, xs, rgp, rgs, mp, ms, wo, g1, b1, wr, br):
    n_p, n_s = xp.shape[0] // TM, xs.shape[0] // TM
    n = n_p + n_s
    pmap = lambda i: (jnp.minimum(i, n_p - 1), 0)
    smap = lambda i: (jnp.maximum(i - n_p, 0), 0)
    const = lambda i: (0, 0)
    return pl.pallas_call(
        functools.partial(_outproj_kernel, n_p=n_p),
        grid=(n,),
        in_specs=[
            pl.BlockSpec((TM, D_MODEL), pmap), pl.BlockSpec((TM, D_MODEL), smap),
            pl.BlockSpec((TM, D_RG), pmap), pl.BlockSpec((TM, D_RG), smap),
            pl.BlockSpec((TM, D_M), pmap), pl.BlockSpec((TM, D_M), smap),
            pl.BlockSpec((D_RG + D_M, D_MODEL), const),
            pl.BlockSpec((1, D_MODEL), const), pl.BlockSpec((1, D_MODEL), const),
            pl.BlockSpec((D_MODEL, 2 * LANES), const),
            pl.BlockSpec((1, LANES), const),
        ],
        out_specs=[
            pl.BlockSpec((TM, X_ROW), lambda i: (i, 0)),
            pl.BlockSpec((1, LANES), const),
        ],
        out_shape=[
            jax.ShapeDtypeStruct((n * TM, X_ROW), f32),
            jax.ShapeDtypeStruct((1, LANES), f32),
        ],
        compiler_params=_cparams(1),
        name="outproj",
    )(xp, xs, rgp, rgs, mp, ms, wo, g1, b1, wr, br)


def _row_copy(src_ref, src_row, dst_ref, dst_row, sem):
    return pltpu.make_async_copy(src_ref.at[pl.ds(src_row, 1)], dst_ref.at[pl.ds(dst_row, 1)], sem)


def _tile_rows_wait(src_ref, dst_ref, sem):
    pltpu.make_async_copy(src_ref.at[pl.ds(0, TM_MOVE)], dst_ref.at[pl.ds(0, TM_MOVE)], sem).wait()


def _dispatch_kernel(pad_ref, pos_ref, x1_ref, xs_ref, zero_scr, xbuf, sem, zsem, lsem):
    i = pl.program_id(0)
    n = pl.num_programs(0)

    def zero_tile(t):
        return pltpu.make_async_copy(zero_scr, xs_ref.at[pl.ds(pl.multiple_of(t * TM_MOE, TM_MOE), TM_MOE)], zsem)

    def zero_chunk(row0):
        return pltpu.make_async_copy(zero_scr.at[pl.ds(0, SUBLANES)],
                                     xs_ref.at[pl.ds(pl.multiple_of(row0, SUBLANES), SUBLANES)], zsem)

    @pl.when(i == 0)
    def _():
        zero_scr[...] = jnp.zeros_like(zero_scr)
        n_used = pad_ref[2 * N_CLASSES]
        n_tiles = xs_ref.shape[0] // TM_MOE
        for go in (lambda cp: cp.start(), lambda cp: cp.wait()):
            lax.fori_loop(n_used, n_tiles, lambda t, c: (go(zero_tile(t)), c)[1], 0)

            def per_class(cl, c):
                row0 = pad_ref[2 * cl]
                lax.fori_loop(0, pad_ref[2 * cl + 1],
                              lambda k, c2: (go(zero_chunk(row0 + k * SUBLANES)), c2)[1], 0)
                return c

            lax.fori_loop(0, N_CLASSES, per_class, 0)

    def load(t, s):
        rows = pl.ds(pl.multiple_of(t * TM_MOVE, TM_MOVE), TM_MOVE)
        return pltpu.make_async_copy(x1_ref.at[rows], xbuf.at[s], lsem.at[s])

    @pl.when(i == 0)
    def _():
        load(0, 0).start()

    @pl.when(i + 1 < n)
    def _():
        load(i + 1, (i + 1) % 3).start()

    slot = i % 3
    load(i, slot).wait()

    def start(r, c):
        _row_copy(xbuf.at[slot], r, xs_ref, pos_ref[0, 0, r], sem.at[slot]).start()
        return c

    lax.fori_loop(0, TM_MOVE, start, 0, unroll=8)

    @pl.when(i > 0)
    def _():
        _tile_rows_wait(xbuf.at[(i + 2) % 3], xs_ref, sem.at[(i + 2) % 3])

    @pl.when(i == n - 1)
    def _():
        _tile_rows_wait(xbuf.at[slot], xs_ref, sem.at[slot])


def _dispatch(pad_plan, pos, x1, n_rows):
    n = x1.shape[0] // TM_MOVE
    return pl.pallas_call(
        _dispatch_kernel,
        grid_spec=pltpu.PrefetchScalarGridSpec(
            num_scalar_prefetch=1,
            grid=(n,),
            in_specs=[
                pl.BlockSpec((1, 1, TM_MOVE), lambda i, *_: (i, 0, 0), memory_space=pltpu.SMEM),
                pl.BlockSpec(memory_space=pl.ANY),
            ],
            out_specs=pl.BlockSpec(memory_space=pl.ANY),
            scratch_shapes=[pltpu.VMEM((TM_MOE, X_ROW), f32), pltpu.VMEM((3, TM_MOVE, X_ROW), f32),
                            pltpu.SemaphoreType.DMA((3,)), pltpu.SemaphoreType.DMA(()),
                            pltpu.SemaphoreType.DMA((3,))],
        ),
        out_shape=jax.ShapeDtypeStruct((n_rows, X_ROW), f32),
        compiler_params=_cparams(1),
        name="dispatch",
    )(pad_plan, pos, x1)


def _expert_kernel(ta_ref, tb_ref, ca_ref, cb_ref, sa_ref, sb_ref, nu_ref,
                   x_ref, wg_hbm, wu_hbm, wd_hbm, g2_ref, b2_ref, y_ref,
                   wga, wua, wda, wgb, wub, wdb, sem_a, sem_b):
    i = pl.program_id(0)
    n_used = nu_ref[0]
    first = i * MOE_TILES
    side_a = (ta_ref, ca_ref, sa_ref, (wga, wua, wda), sem_a)
    side_b = (tb_ref, cb_ref, sb_ref, (wgb, wub, wdb), sem_b)

    def copies(side, j):
        t_ref, _, s_ref, bufs, sem = side
        e, s = t_ref[j], s_ref[j]
        return [pltpu.make_async_copy(w.at[e], buf.at[s], sem.at[s])
                for w, buf in zip((wg_hbm, wu_hbm, wd_hbm), bufs)]

    def fetch(j):
        @pl.when(j < n_used)
        def _():
            for side in (side_a, side_b):
                @pl.when(side[1][j] == 1)
                def _():
                    for cp in copies(side, j):
                        cp.start()

    @pl.when(i == 0)
    def _():
        for j in range(W_AHEAD):
            fetch(j)

    for t in range(MOE_TILES):
        fetch(first + W_AHEAD + t)

    @pl.when(first < n_used)
    def _():
        for t in range(MOE_TILES):
            for side in (side_a, side_b):
                @pl.when(side[1][first + t] == 1)
                def _():
                    for cp in copies(side, first + t):
                        cp.wait()
        rows = [slice(t * TM_MOE, (t + 1) * TM_MOE) for t in range(MOE_TILES)]
        x = [x_ref[r, 0:D_MODEL] for r in rows]
        info = [x_ref[r, D_MODEL:X_ROW] for r in rows]
        xb = [v.astype(bf16) for v in x]
        units = [(t, bufs, s_ref[first + t]) for t in range(MOE_TILES)
                 for bufs, s_ref in (((wga, wua, wda), sa_ref), ((wgb, wub, wdb), sb_ref))]
        hg = [jnp.dot(xb[t], bufs[0][s], preferred_element_type=f32) for t, bufs, s in units]
        hu = [jnp.dot(xb[t], bufs[1][s], preferred_element_type=f32) for t, bufs, s in units]
        mid = [(g * jax.nn.sigmoid(g) * u).astype(bf16) for g, u in zip(hg, hu)]
        ys = [jnp.dot(m, bufs[2][s], preferred_element_type=f32)
              for m, (t, bufs, s) in zip(mid, units)]
        for t in range(MOE_TILES):
            ffn = info[t][:, 2:3] * ys[2 * t] + info[t][:, 3:4] * ys[2 * t + 1]
            y_ref[rows[t], :] = _layer_norm(ALPHA * x[t] + ffn, g2_ref[...], b2_ref[...])

    @pl.when(first >= n_used)
    def _():
        y_ref[...] = jnp.zeros_like(y_ref)


def _experts(tile_ea, tile_eb, n_used, xs, wg, wu, wd, g2, b2):
    nt = xs.shape[0] // TM_MOE

    def ring_plan(tile_e):
        opens = jnp.concatenate([jnp.ones((1,), jnp.int32), (tile_e[1:] != tile_e[:-1]).astype(jnp.int32)])
        return opens, (jnp.cumsum(opens) - 1) % W_SLOTS

    open_a, slot_a = ring_plan(tile_ea)
    open_b, slot_b = ring_plan(tile_eb)
    assert nt % MOE_TILES == 0
    n_used = ((n_used + MOE_TILES - 1) // MOE_TILES) * MOE_TILES
    const = lambda i, *_: (0, 0)
    xmap = lambda i, ta, tb, ca, cb, sa, sb, nu: (jnp.minimum(i, nu[0] // MOE_TILES - 1), 0)
    w_in = pltpu.VMEM((W_SLOTS, D_MODEL, D_EXPERT), bf16)
    w_out = pltpu.VMEM((W_SLOTS, D_EXPERT, D_MODEL), bf16)
    return pl.pallas_call(
        _expert_kernel,
        grid_spec=pltpu.PrefetchScalarGridSpec(
            num_scalar_prefetch=7,
            grid=(nt // MOE_TILES,),
            in_specs=[
                pl.BlockSpec((MOE_TILES * TM_MOE, X_ROW), xmap),
                pl.BlockSpec(memory_space=pl.ANY), pl.BlockSpec(memory_space=pl.ANY),
                pl.BlockSpec(memory_space=pl.ANY),
                pl.BlockSpec((1, D_MODEL), const), pl.BlockSpec((1, D_MODEL), const),
            ],
            out_specs=pl.BlockSpec((MOE_TILES * TM_MOE, D_MODEL), lambda i, *_: (i, 0)),
            scratch_shapes=[w_in, w_in, w_out, w_in, w_in, w_out,
                            pltpu.SemaphoreType.DMA((W_SLOTS,)), pltpu.SemaphoreType.DMA((W_SLOTS,))],
        ),
        out_shape=jax.ShapeDtypeStruct((nt * TM_MOE, D_MODEL), f32),
        compiler_params=_cparams(1),
        name="experts",
    )(tile_ea, tile_eb, open_a, open_b, slot_a.astype(jnp.int32), slot_b.astype(jnp.int32), n_used,
      xs, wg, wu, wd, g2, b2)


def _collect_kernel(pos_ref, posn_ref, ys_ref, op_ref, os_ref, ybuf, sem, *, n_p):
    i = pl.program_id(0)
    n = pl.num_programs(0)
    slot = i % 2

    def gather(p_ref, s):
        def start(r, c):
            _row_copy(ys_ref, p_ref[0, 0, r], ybuf.at[s], r, sem.at[s]).start()
            return c

        lax.fori_loop(0, TM_MOVE, start, 0, unroll=8)

    @pl.when(i == 0)
    def _():
        gather(pos_ref, 0)

    @pl.when(i + 1 < n)
    def _():
        gather(posn_ref, 1 - slot)

    _tile_rows_wait(ys_ref, ybuf.at[slot], sem.at[slot])

    @pl.when(i < n_p)
    def _():
        op_ref[...] = ybuf[slot]

    @pl.when(i >= n_p)
    def _():
        os_ref[...] = ybuf[slot]


def _collect(pos, ys, n_p):
    n = pos.shape[0]
    n_s = n - n_p
    return pl.pallas_call(
        functools.partial(_collect_kernel, n_p=n_p),
        grid=(n,),
        in_specs=[
            pl.BlockSpec((1, 1, TM_MOVE), lambda i: (i, 0, 0), memory_space=pltpu.SMEM),
            pl.BlockSpec((1, 1, TM_MOVE), lambda i: (jnp.minimum(i + 1, n - 1), 0, 0),
                         memory_space=pltpu.SMEM),
            pl.BlockSpec(memory_space=pl.ANY),
        ],
        out_specs=[
            pl.BlockSpec((TM_MOVE, D_MODEL), lambda i: (jnp.minimum(i, n_p - 1), 0)),
            pl.BlockSpec((TM_MOVE, D_MODEL), lambda i: (jnp.maximum(i - n_p, 0), 0)),
        ],
        out_shape=[
            jax.ShapeDtypeStruct((n_p * TM_MOVE, D_MODEL), f32),
            jax.ShapeDtypeStruct((n_s * TM_MOVE, D_MODEL), f32),
        ],
        scratch_shapes=[pltpu.VMEM((2, TM_MOVE, D_MODEL), f32), pltpu.SemaphoreType.DMA((2,))],
        compiler_params=_cparams(1),
        name="collect",
    )(pos, pos, ys)


def _block_diag(w):
    n, d, _ = w.shape
    eye = jnp.eye(n, dtype=w.dtype)
    return (eye[:, None, :, None] * w[:, :, None, :]).reshape(n * d, n * d)


def _pair_walk(n):
    total = n * (n - 1) // 2

    def extend(path, used):
        if len(path) == total:
            return path
        a, b = path[-1]
        for c in range(n):
            for nxt in ((a, c), (c, b)):
                key = frozenset(nxt)
                if len(key) == 2 and key not in used:
                    out = extend(path + [nxt], used | {key})
                    if out:
                        return out
        return None

    return extend([(0, 1)], {frozenset((0, 1))})


def _pad_history(buf):
    return jnp.pad(buf, ((0, 0), (SUBLANES - (CONV_W - 1), 0), (0, 0)))


def kernel(x_prompt, x_sample, state_rg_h, state_rg_conv, state_m_C, state_m_n, state_m_m, state_m_conv, w_in, rg_conv_w, rg_conv_b, rg_w_a, rg_b_a, rg_w_x, rg_b_x, rg_lambda, m_conv_w, m_conv_b, m_w_q, m_w_k, m_w_v, m_b_i, m_b_f, m_norm_g, m_skip, w_out, ln1_g, ln1_b, ln2_g, ln2_b, moe_w_group, moe_b_group, moe_w_expert, moe_b_expert, moe_w_gate, moe_w_up, moe_w_down):
    BP, TP, _ = x_prompt.shape
    BS, TS, _ = x_sample.shape
    n_prompt, n_sample = BP * TP, BS * TS
    xp = x_prompt.reshape(n_prompt, D_MODEL)
    xs = x_sample.reshape(n_sample, D_MODEL)
    l = 0

    w_in_p = jnp.pad(w_in[l], ((0, 0), (0, N_PROJ - w_in.shape[-1]))).astype(bf16)
    wa = _block_diag(rg_w_a[l]).astype(bf16)
    wx = _block_diag(rg_w_x[l]).astype(bf16)
    row = lambda v: v.reshape(1, -1)
    bif = jnp.pad(jnp.concatenate([m_b_i[l], m_b_f[l]]), (0, LANES - 2 * M_HEADS)).reshape(1, LANES)
    wqk = jnp.concatenate([m_w_q[l], m_w_k[l]], axis=-1).astype(bf16)
    wv = m_w_v[l].astype(bf16)
    wkT = m_w_k[l].transpose(0, 2, 1).astype(bf16)
    w_route = jnp.pad(jnp.concatenate([moe_w_group[l], moe_w_expert[l]], axis=1),
                      ((0, 0), (0, LANES - N_GROUPS - N_EXPERTS)))
    wr_hi = w_route.astype(bf16)
    wr_lo = (w_route - wr_hi.astype(f32)).astype(bf16)
    wr = jnp.concatenate([wr_hi, wr_lo], axis=1)
    b_route = jnp.pad(jnp.concatenate([moe_b_group[l], moe_b_expert[l]]),
                      (0, LANES - N_GROUPS - N_EXPERTS)).reshape(1, LANES)

    proj_p, proj_s = _inproj(xp, xs, w_in_p)

    rg_args = (rg_conv_w[l], row(rg_conv_b[l]), wa, row(rg_b_a[l]), wx, row(rg_b_x[l]), row(rg_lambda[l]))
    zeros = lambda *s: jnp.zeros(s, f32)
    s_blk = M_ROWS // TS
    yrg_p, p_rg_h, p_rg_conv = _rglru(proj_p, 0, BP, TP, 1, RG_ROWS_PROMPT, zeros(BP, SUBLANES, D_RG),
                                      zeros(BP, 1, D_RG), *rg_args)
    yrg_s, s_rg_h, s_rg_conv = _rglru(proj_s, 0, BS, TS, s_blk, TS, _pad_history(state_rg_conv[l]),
                                      state_rg_h[l].reshape(BS, 1, D_RG), *rg_args)

    m_args = (m_conv_w[l], row(m_conv_b[l]), wqk, wv, wkT, bif, row(m_norm_g[l]), row(m_skip[l]))
    m0_p = jnp.full((SUBLANES, BP * LANES), M_INIT, f32)
    ym_p, p_m_C, p_m_n, p_m_m, p_m_conv, wg_b, wu_b, wd_b = _mlstm(
        proj_p, BP, TP, 1, M_ROWS, M_BLOCKS_PROMPT, zeros(BP, SUBLANES, D_M), zeros(BP, M_HEADS, M_DH, M_DH),
        zeros(BP, M_HEADS, 1, M_DH), m0_p, *m_args, riders=(moe_w_gate[l], moe_w_up[l], moe_w_down[l]))
    m0_s = jnp.pad(jnp.repeat(state_m_m[l].T, TS, axis=1), ((0, SUBLANES - M_HEADS), (0, 0)))
    ym_s, s_m_C, s_m_n, s_m_m, s_m_conv = _mlstm(
        proj_s, BS, TS, s_blk, TS, M_BLOCKS_SAMPLE, _pad_history(state_m_conv[l]), state_m_C[l],
        state_m_n[l].reshape(BS, M_HEADS, 1, M_DH), m0_s, *m_args)

    x1, cnt = _outproj(xp, xs, yrg_p, yrg_s, ym_p, ym_s, w_out[l].astype(bf16),
                       row(ln1_g[l]), row(ln1_b[l]), wr, b_route)

    n_tok = n_prompt + n_sample
    n_tiles = n_tok // TM_MOE + N_CLASSES
    lay_cls, lay_a, lay_b = [], [], []
    for g in range(N_GROUPS):
        for ja, jb in _pair_walk(EXPERTS_PER_GROUP):
            assert ja < jb
            n_first = (EXPERTS_PER_GROUP - 1) * ja - ja * (ja - 1) // 2
            lay_cls.append(PAIRS_PER_GROUP * g + n_first + jb - ja - 1)
            lay_a.append(g * EXPERTS_PER_GROUP + ja)
            lay_b.append(g * EXPERTS_PER_GROUP + jb)
    classes = jnp.arange(N_CLASSES, dtype=jnp.int32)
    in_slot = jnp.array(lay_cls, jnp.int32)[:, None] == classes
    counts = jnp.sum(jnp.where(in_slot, cnt[0, :N_CLASSES].astype(jnp.int32), 0), axis=1)
    padded = ((counts + TM_MOE - 1) // TM_MOE) * TM_MOE
    ends = jnp.cumsum(padded)
    offs = ends - padded
    offs_of_cls = jnp.sum(jnp.where(in_slot, offs[:, None], 0), axis=0)
    cls = x1[:, D_MODEL].astype(jnp.int32)
    rank = x1[:, D_MODEL + 1].astype(jnp.int32)
    pos = jnp.sum(jnp.where(cls[:, None] == classes, offs_of_cls, 0), axis=-1) + rank
    pos = pos.reshape(n_tok // TM_MOVE, 1, TM_MOVE)
    tiles = jnp.arange(n_tiles, dtype=jnp.int32)
    tile_slot = jnp.minimum(jnp.sum(tiles[:, None] >= (ends // TM_MOE)[None, :], axis=1), N_CLASSES - 1)
    on_slot = tile_slot[:, None] == classes
    tile_ea = jnp.sum(jnp.where(on_slot, jnp.array(lay_a, jnp.int32), 0), axis=-1).astype(jnp.int32)
    tile_eb = jnp.sum(jnp.where(on_slot, jnp.array(lay_b, jnp.int32), 0), axis=-1).astype(jnp.int32)
    n_used = (ends[-1] // TM_MOE).reshape(1).astype(jnp.int32)
    pad_row0 = ((offs + counts) // SUBLANES) * SUBLANES
    pad_plan = jnp.concatenate([jnp.stack([pad_row0, (ends - pad_row0) // SUBLANES], axis=1).reshape(-1),
                                     n_used]).astype(jnp.int32)

    x_sorted = _dispatch(pad_plan, pos, x1, n_tiles * TM_MOE)
    y_sorted = _experts(tile_ea, tile_eb, n_used, x_sorted, wg_b, wu_b, wd_b,
                        row(ln2_g[l]), row(ln2_b[l]))
    y_p, y_s = _collect(pos, y_sorted, n_prompt // TM_MOVE)

    return (y_p.reshape(BP, TP, D_MODEL), y_s.reshape(BS, TS, D_MODEL),
            p_rg_h.reshape(1, BP, D_RG), p_rg_conv[None], p_m_C[None], p_m_n.reshape(1, BP, M_HEADS, M_DH),
            p_m_m[None], p_m_conv[None],
            s_rg_h.reshape(1, BS, D_RG), s_rg_conv[None], s_m_C[None], s_m_n.reshape(1, BS, M_HEADS, M_DH),
            s_m_m[None], s_m_conv[None])
```

```python
import functools

import jax
import jax.numpy as jnp
from jax import lax
from jax.experimental import pallas as pl
from jax.experimental.pallas import tpu as pltpu

f32 = jnp.float32
bf16 = jnp.bfloat16

D_MODEL = 1024
D_RG = 512
RG_C = 8.0
D_M = 512
M_HEADS = 4
M_DH = 128
CONV_W = 4
N_GROUPS = 4
EXPERTS_PER_GROUP = 8
N_EXPERTS = 32
D_EXPERT = 256
ALPHA = 2.0 ** 0.25
LN_EPS = 1e-5
M_INIT = -1.0e4

LANES = 128
SUBLANES = 8
TM = 512
TM_MOVE = 1024
TM_MOE = 128
MOE_TILES = 4
W_AHEAD = 4
W_SLOTS = W_AHEAD + MOE_TILES
PAIRS_PER_GROUP = EXPERTS_PER_GROUP * (EXPERTS_PER_GROUP - 1) // 2
N_CLASSES = N_GROUPS * PAIRS_PER_GROUP
X_ROW = D_MODEL + LANES
N_PROJ = 2 * D_RG + 2 * D_M + LANES
RG_ROWS_PROMPT = 512
M_ROWS = 128
M_BLOCKS_PROMPT = 8
M_BLOCKS_SAMPLE = 2
V7X_VMEM_BYTES = 64 * 1024 * 1024
VMEM_LIMIT = V7X_VMEM_BYTES - 8 * 1024 * 1024


def _cparams(n_axes):
    return pltpu.CompilerParams(dimension_semantics=("arbitrary",) * n_axes,
                                vmem_limit_bytes=VMEM_LIMIT)


def _inproj_kernel(xp_ref, xs_ref, w_ref, op_ref, os_ref, *, n_p):
    i = pl.program_id(0)

    def run(x_ref, o_ref):
        o_ref[...] = jnp.dot(x_ref[...].astype(bf16), w_ref[...], preferred_element_type=f32)

    @pl.when(i < n_p)
    def _():
        run(xp_ref, op_ref)

    @pl.when(i >= n_p)
    def _():
        run(xs_ref, os_ref)


def _inproj(xp, xs, w):
    n_p, n_s = xp.shape[0] // TM, xs.shape[0] // TM
    pmap = lambda i: (jnp.minimum(i, n_p - 1), 0)
    smap = lambda i: (jnp.maximum(i - n_p, 0), 0)
    return pl.pallas_call(
        functools.partial(_inproj_kernel, n_p=n_p),
        grid=(n_p + n_s,),
        in_specs=[
            pl.BlockSpec((TM, D_MODEL), pmap),
            pl.BlockSpec((TM, D_MODEL), smap),
            pl.BlockSpec((D_MODEL, N_PROJ), lambda i: (0, 0)),
        ],
        out_specs=[pl.BlockSpec((TM, N_PROJ), pmap), pl.BlockSpec((TM, N_PROJ), smap)],
        out_shape=[jax.ShapeDtypeStruct((n_p * TM, N_PROJ), f32),
                   jax.ShapeDtypeStruct((n_s * TM, N_PROJ), f32)],
        compiler_params=_cparams(1),
        name="inproj",
    )(xp, xs, w)


def _causal_conv(tail_scr, x, cw_ref, cb_ref, S, L, C):
    tail = tail_scr[...]
    sub = lax.broadcasted_iota(jnp.int32, (S, SUBLANES, C), 1)
    acc = cb_ref[...] + cw_ref[CONV_W - 1:CONV_W, :] * x
    for d in range(1, CONV_W):
        back = pltpu.roll(x, d, 0).reshape(S, L, C)
        head = jnp.where(sub < d, pltpu.roll(tail, d, 1), back[:, 0:SUBLANES, :])
        if L > SUBLANES:
            back = jnp.concatenate([head, back[:, SUBLANES:, :]], axis=1)
        else:
            back = head
        acc = acc + cw_ref[CONV_W - 1 - d:CONV_W - d, :] * back.reshape(S * L, C)
    tail_scr[...] = x.reshape(S, L, C)[:, L - SUBLANES:, :]
    return acc


def _softplus(x):
    return jnp.maximum(x, 0.0) + jnp.log1p(jnp.exp(-jnp.abs(x)))


def _rglru_kernel(x_ref, g_ref, buf_ref, h0_ref, cw_ref, cb_ref, wa_ref, ba_ref, wx_ref, bx_ref,
                  lam_ref, y_ref, hN_ref, cN_ref, xs_scr, *, S, L):
    R = S * L
    t = pl.program_id(1)

    @pl.when(t == 0)
    def _():
        xs_scr[...] = buf_ref[...]
        hN_ref[...] = h0_ref[...]

    x = x_ref[...]
    xc = _causal_conv(xs_scr, x, cw_ref, cb_ref, S, L, D_RG)
    xcb = xc.astype(bf16)
    r = jax.nn.sigmoid(jnp.dot(xcb, wa_ref[...], preferred_element_type=f32) + ba_ref[...])
    ig = jax.nn.sigmoid(jnp.dot(xcb, wx_ref[...], preferred_element_type=f32) + bx_ref[...])
    log_a = (-RG_C) * r * _softplus(-lam_ref[...])
    a = jnp.exp(log_a)
    th = jnp.tanh(log_a)
    u = jnp.sqrt(-2.0 * th / (1.0 - th)) * ig * xc

    n_grp, grp_per_seq = R // SUBLANES, L // SUBLANES
    a3 = a.reshape(n_grp, SUBLANES, D_RG)
    u3 = u.reshape(n_grp, SUBLANES, D_RG)
    sub = lax.broadcasted_iota(jnp.int32, (n_grp, SUBLANES, D_RG), 1)
    s = 1
    while s < SUBLANES:
        ok = sub >= s
        a_sh = pltpu.roll(a3, s, 1)
        u_sh = pltpu.roll(u3, s, 1)
        u3 = jnp.where(ok, a3 * u_sh + u3, u3)
        a3 = jnp.where(ok, a3 * a_sh, a3)
        s *= 2
    h0 = hN_ref[...]
    groups = []
    for kg in range(n_grp):
        carry = h0[kg // grp_per_seq] if kg % grp_per_seq == 0 else groups[-1][SUBLANES - 1:SUBLANES, :]
        groups.append(a3[kg] * carry + u3[kg])
    h = jnp.concatenate(groups, axis=0)

    y_ref[...] = h * jax.nn.gelu(g_ref[...], approximate=True)
    hN_ref[...] = h.reshape(S, L, D_RG)[:, L - 1:L, :]
    cN_ref[...] = xs_scr[:, SUBLANES - (CONV_W - 1):SUBLANES, :]


def _rglru(proj, row0, B, T, S, L, buf8, h0, cw, cb, wa, ba, wx, bx, lam):
    R = S * L
    nb, nt = B // S, T // L
    blk0 = row0 // R
    row_map = lambda b, t: (blk0 + b * nt + t, 0)
    const2 = lambda b, t: (0, 0)
    return pl.pallas_call(
        functools.partial(_rglru_kernel, S=S, L=L),
        grid=(nb, nt),
        in_specs=[
            pl.BlockSpec((R, D_RG), row_map),
            pl.BlockSpec((R, D_RG), lambda b, t: (blk0 + b * nt + t, 1)),
            pl.BlockSpec((S, SUBLANES, D_RG), lambda b, t: (b, 0, 0)),
            pl.BlockSpec((S, 1, D_RG), lambda b, t: (b, 0, 0)),
            pl.BlockSpec((CONV_W, D_RG), const2),
            pl.BlockSpec((1, D_RG), const2),
            pl.BlockSpec((D_RG, D_RG), const2),
            pl.BlockSpec((1, D_RG), const2),
            pl.BlockSpec((D_RG, D_RG), const2),
            pl.BlockSpec((1, D_RG), const2),
            pl.BlockSpec((1, D_RG), const2),
        ],
        out_specs=[
            pl.BlockSpec((R, D_RG), lambda b, t: (b * nt + t, 0)),
            pl.BlockSpec((S, 1, D_RG), lambda b, t: (b, 0, 0)),
            pl.BlockSpec((S, CONV_W - 1, D_RG), lambda b, t: (b, 0, 0)),
        ],
        out_shape=[
            jax.ShapeDtypeStruct((B * T, D_RG), f32),
            jax.ShapeDtypeStruct((B, 1, D_RG), f32),
            jax.ShapeDtypeStruct((B, CONV_W - 1, D_RG), f32),
        ],
        scratch_shapes=[pltpu.VMEM((S, SUBLANES, D_RG), f32)],
        compiler_params=_cparams(2),
        name="rglru",
    )(proj, proj, buf8, h0, cw, cb, wa, ba, wx, bx, lam)


def _seg_scan(x, op, fill, tin, L, reverse=False):
    s = 1
    while s < L:
        if reverse:
            sh = pltpu.roll(x, LANES - s, 1)
            ok = tin < L - s
        else:
            sh = pltpu.roll(x, s, 1)
            ok = tin >= s
        x = op(x, jnp.where(ok, sh, fill))
        s *= 2
    return x


def _mlstm_kernel(*refs, S, L, G, n_riders):
    (x_ref, z_ref, gt_ref, buf_ref, c0_ref, n0_ref, m0_ref,
     cw_ref, cb_ref, wqk_ref, wv_ref, wkT_ref, bif_ref, ng_ref, sk_ref) = refs[:15]
    rider_in = refs[15:15 + n_riders]
    y_ref, cN_ref, nN_ref, mN_ref, bN_ref = refs[15 + n_riders:20 + n_riders]
    rider_out = refs[20 + n_riders:20 + 2 * n_riders]
    xs_scr, m_scr = refs[20 + 2 * n_riders:]
    for src, dst in zip(rider_in, rider_out):
        dst[...] = src[...].astype(bf16)

    R = S * L
    GS, GR = G * S, G * R
    j = pl.program_id(1)

    @pl.when(j == 0)
    def _():
        xs_scr[...] = buf_ref[...]
        cN_ref[...] = c0_ref[...]
        nN_ref[...] = n0_ref[...]
        m_scr[...] = m0_ref[...]

    x = x_ref[...].reshape(GR, D_M)
    xc = _causal_conv(xs_scr, x, cw_ref, cb_ref, GS, L, D_M)
    xa = xc * jax.nn.sigmoid(xc)
    xab = xa.astype(bf16)
    xb = x.astype(bf16)

    il_parts, f_parts = [], []
    for g in range(G):
        gT = (gt_ref[g] + bif_ref[...]).T
        il_parts.append(gT[0:SUBLANES, :])
        f_parts.append(pltpu.roll(gT[0:SUBLANES, :], M_HEADS, 0))
    il = jnp.concatenate(il_parts, axis=0)
    fl = -_softplus(-jnp.concatenate(f_parts, axis=0))
    tin = lax.broadcasted_iota(jnp.int32, (G * SUBLANES, LANES), 1) % L
    bcum = _seg_scan(fl, jnp.add, 0.0, tin, L)
    a = il - bcum
    m_prev = m_scr[...].reshape(G * SUBLANES, LANES)
    big_m = jnp.maximum(m_prev, _seg_scan(a, jnp.maximum, -jnp.inf, tin, L))
    m_t = bcum + big_m
    if S == 1:
        m_last = jnp.broadcast_to(big_m[:, LANES - 1:LANES], big_m.shape)
    else:
        m_last = _seg_scan(big_m, jnp.maximum, -jnp.inf, tin, L, reverse=True)
    rows = [big_m, jnp.exp(m_prev - big_m), jnp.exp(-m_t), jnp.exp(a - m_last), jnp.exp(m_prev - m_last)]
    mN_ref[...] = m_t.reshape(G, SUBLANES, LANES)
    m_scr[...] = jnp.broadcast_to(m_t[:, LANES - 1:LANES], m_t.shape).reshape(G, SUBLANES, LANES)
    pad_rows = jnp.zeros((LANES - len(rows) * SUBLANES, LANES), f32)
    cols = []
    for g in range(G):
        gs = slice(g * SUBLANES, (g + 1) * SUBLANES)
        cols.append(jnp.concatenate([r[gs] for r in rows] + [pad_rows], axis=0).T)

    def col(g, q, h):
        return cols[g][:, SUBLANES * q + h:SUBLANES * q + h + 1]

    qs, ks, vs, kTs = [], [], [], []
    for h in range(M_HEADS):
        hs = slice(h * M_DH, (h + 1) * M_DH)
        qk_h = jnp.dot(xab[:, hs], wqk_ref[h], preferred_element_type=f32)
        qs.append(qk_h[:, 0:M_DH])
        ks.append(qk_h[:, M_DH:2 * M_DH] * (M_DH ** -0.5))
        vs.append(jnp.dot(xb[:, hs], wv_ref[h], preferred_element_type=f32))
        if S == 1:
            kT_h = lax.dot_general(wkT_ref[h], xab[:, hs], (((1,), (1,)), ((), ())),
                                   preferred_element_type=f32)
            kTs.append((kT_h * (M_DH ** -0.5)).astype(bf16))

    ti = lax.broadcasted_iota(jnp.int32, (R, R), 0)
    si = lax.broadcasted_iota(jnp.int32, (R, R), 1)
    mask = (si <= ti) & ((ti // L) == (si // L))
    ones_b = jnp.ones((R, M_DH), bf16)
    pairs = [(g, h) for g in range(G) for h in range(M_HEADS)]
    blk = lambda g: slice(g * R, (g + 1) * R)
    seqs = [slice(b * L, (b + 1) * L) for b in range(S)]
    q = {(g, h): qs[h][blk(g)] for g, h in pairs}
    k = {(g, h): ks[h][blk(g)] for g, h in pairs}
    v = {(g, h): vs[h][blk(g)] for g, h in pairs}
    qb = {p: q[p].astype(bf16) for p in pairs}
    kb = {p: k[p].astype(bf16) for p in pairs}
    qk = {p: lax.dot_general(qb[p], kb[p], (((1,), (1,)), ((), ())), preferred_element_type=f32)
          for p in pairs}
    sm = {}
    for g, h in pairs:
        a_row = a[g * SUBLANES + h:g * SUBLANES + h + 1, :]
        decay = jnp.exp(jnp.where(mask, a_row - col(g, 0, h), -jnp.inf))
        sm[g, h] = (qk[g, h] * decay).astype(bf16)
    nd = {p: jnp.dot(sm[p], jnp.concatenate([v[p].astype(bf16), ones_b], axis=1),
                     preferred_element_type=f32) for p in pairs}
    c_old = {(g, h, b): cN_ref[g * S + b, h] for g, h in pairs for b in range(S)}
    n_old = {(g, h, b): nN_ref[g * S + b, h] for g, h in pairs for b in range(S)}
    q_c = {(g, h, b): jnp.dot(qb[g, h][seqs[b]], c_old[g, h, b].astype(bf16), preferred_element_type=f32)
           for g, h in pairs for b in range(S)}
    hh = {}
    for g, h in pairs:
        g_col, e_col = col(g, 1, h), col(g, 2, h)
        pieces = []
        for b, rs in enumerate(seqs):
            q_n = jnp.sum(q[g, h][rs] * n_old[g, h, b], axis=1, keepdims=True)
            num = nd[g, h][rs, 0:M_DH] + g_col[rs] * q_c[g, h, b]
            den = nd[g, h][rs, M_DH:2 * M_DH] + g_col[rs] * q_n
            pieces.append(num / jnp.maximum(jnp.abs(den), e_col[rs]))
        hh[g, h] = pieces[0] if S == 1 else jnp.concatenate(pieces, axis=0)
    mu = {p: jnp.mean(hh[p], axis=1, keepdims=True) for p in pairs}
    var = {p: jnp.mean(jnp.square(hh[p] - mu[p]), axis=1, keepdims=True) for p in pairs}
    hn_blocks = [jnp.concatenate([(hh[g, h] - mu[g, h]) * lax.rsqrt(var[g, h] + LN_EPS)
                                  for h in range(M_HEADS)], axis=1) for g in range(G)]
    new_c, new_n = [], []
    for g, h in pairs:
        w_col = col(g, 3, h)
        wv = (w_col * v[g, h]).astype(bf16)
        wk = w_col * k[g, h]
        for b, rs in enumerate(seqs):
            g_end = col(g, 4, h)[(b + 1) * L - 1:(b + 1) * L, :]
            if S == 1:
                kv = jnp.dot(kTs[h][:, blk(g)], wv, preferred_element_type=f32)
            else:
                kv = lax.dot_general(kb[g, h][rs], wv[rs], (((0,), (0,)), ((), ())),
                                     preferred_element_type=f32)
            new_c.append((g * S + b, h, g_end * c_old[g, h, b] + kv))
            new_n.append((g * S + b, h, g_end * n_old[g, h, b] + jnp.sum(wk[rs], axis=0, keepdims=True)))

    hn = jnp.concatenate(hn_blocks, axis=0) * ng_ref[...]
    y = jax.nn.sigmoid(z_ref[...].reshape(GR, D_M)) * (hn + sk_ref[...] * xa)
    y_ref[...] = y.reshape(G, R, D_M)
    for sq, h, val in new_c:
        cN_ref[sq, h] = val
    for sq, h, val in new_n:
        nN_ref[sq, h] = val
    bN_ref[...] = xs_scr[:, SUBLANES - (CONV_W - 1):SUBLANES, :]


def _mlstm(proj, B, T, S, L, G, buf8, c0, n0, m0_lanes, cw, cb, wqk, wv, wkT, bif, ng, sk, riders=()):
    R = S * L
    assert R == M_ROWS == LANES
    nblk, nc = B // S, T // L
    nb = nblk // G
    assert nblk * S == B and nc * L == T and nb * G == nblk and (S == 1 or nc == 1)
    steps = nb * nc
    assert all(r.shape[0] % steps == 0 for r in riders)
    rider_specs = [pl.BlockSpec((r.shape[0] // steps,) + r.shape[1:], lambda b, j: (b * nc + j, 0, 0))
                   for r in riders]
    proj4 = proj.reshape(nblk, nc, R, N_PROJ)
    m0_3 = m0_lanes.reshape(SUBLANES, nblk, LANES).transpose(1, 0, 2)
    const2 = lambda b, j: (0, 0)
    const3 = lambda b, j: (0, 0, 0)
    gate_blk = (2 * D_RG + 2 * D_M) // LANES
    GS = G * S
    outs = pl.pallas_call(
        functools.partial(_mlstm_kernel, S=S, L=L, G=G, n_riders=len(riders)),
        grid=(nb, nc),
        in_specs=[
            pl.BlockSpec((G, None, R, D_M), lambda b, j: (b, j, 0, 2)),
            pl.BlockSpec((G, None, R, D_M), lambda b, j: (b, j, 0, 3)),
            pl.BlockSpec((G, None, R, LANES), lambda b, j: (b, j, 0, gate_blk)),
            pl.BlockSpec((GS, SUBLANES, D_M), lambda b, j: (b, 0, 0)),
            pl.BlockSpec((GS, M_HEADS, M_DH, M_DH), lambda b, j: (b, 0, 0, 0)),
            pl.BlockSpec((GS, M_HEADS, 1, M_DH), lambda b, j: (b, 0, 0, 0)),
            pl.BlockSpec((G, SUBLANES, LANES), lambda b, j: (b, 0, 0)),
            pl.BlockSpec((CONV_W, D_M), const2),
            pl.BlockSpec((1, D_M), const2),
            pl.BlockSpec((M_HEADS, M_DH, 2 * M_DH), const3),
            pl.BlockSpec((M_HEADS, M_DH, M_DH), const3),
            pl.BlockSpec((M_HEADS, M_DH, M_DH), const3),
            pl.BlockSpec((1, LANES), const2),
            pl.BlockSpec((1, D_M), const2),
            pl.BlockSpec((1, D_M), const2),
        ] + rider_specs,
        out_specs=[
            pl.BlockSpec((G, None, R, D_M), lambda b, j: (b, j, 0, 0)),
            pl.BlockSpec((GS, M_HEADS, M_DH, M_DH), lambda b, j: (b, 0, 0, 0)),
            pl.BlockSpec((GS, M_HEADS, 1, M_DH), lambda b, j: (b, 0, 0, 0)),
            pl.BlockSpec((G, None, SUBLANES, LANES), lambda b, j: (b, j, 0, 0)),
            pl.BlockSpec((GS, CONV_W - 1, D_M), lambda b, j: (b, 0, 0)),
        ] + rider_specs,
        out_shape=[
            jax.ShapeDtypeStruct((nblk, nc, R, D_M), f32),
            jax.ShapeDtypeStruct((B, M_HEADS, M_DH, M_DH), f32),
            jax.ShapeDtypeStruct((B, M_HEADS, 1, M_DH), f32),
            jax.ShapeDtypeStruct((nblk, nc, SUBLANES, LANES), f32),
            jax.ShapeDtypeStruct((B, CONV_W - 1, D_M), f32),
        ] + [jax.ShapeDtypeStruct(r.shape, bf16) for r in riders],
        scratch_shapes=[pltpu.VMEM((GS, SUBLANES, D_M), f32), pltpu.VMEM((G, SUBLANES, LANES), f32)],
        compiler_params=_cparams(2),
        name="mlstm",
    )(proj4, proj4, proj4, buf8, c0, n0, m0_3, cw, cb, wqk, wv, wkT, bif, ng, sk, *riders)
    y, c_new, n_new, m_t, b_new = outs[:5]
    m_last = m_t[:, nc - 1, :M_HEADS, :].reshape(nblk, M_HEADS, S, L)[:, :, :, L - 1]
    m_last = m_last.transpose(0, 2, 1).reshape(B, M_HEADS)
    return (y.reshape(B * T, D_M), c_new, n_new, m_last, b_new) + tuple(outs[5:])


def _layer_norm(x, g, b):
    mu = jnp.mean(x, axis=-1, keepdims=True)
    var = jnp.mean(jnp.square(x - mu), axis=-1, keepdims=True)
    return (x - mu) * lax.rsqrt(var + LN_EPS) * g + b


def _first_lane_of_max(vals, lane_f):
    vmax = jnp.max(vals, axis=1, keepdims=True)
    idx = jnp.min(jnp.where(vals == vmax, lane_f, float(LANES)), axis=1, keepdims=True)
    return vmax, idx


def _outproj_kernel(xp_ref, xs_ref, rgp_ref, rgs_ref, mp_ref, ms_ref, wo_ref, g1_ref, b1_ref,
                    wr_ref, br_ref, x1_ref, cnt_ref, *, n_p):
    i = pl.program_id(0)

    @pl.when(i == 0)
    def _():
        cnt_ref[...] = jnp.zeros_like(cnt_ref)

    def run(x_ref, rg_ref, m_ref):
        heads = jnp.concatenate([rg_ref[...].astype(bf16), m_ref[...].astype(bf16)], axis=1)
        mix = jnp.dot(heads, wo_ref[...], preferred_element_type=f32)
        x1 = _layer_norm(ALPHA * x_ref[...] + mix, g1_ref[...], b1_ref[...])
        x1_ref[:, 0:D_MODEL] = x1

        hi = x1.astype(bf16)
        lo = (x1 - hi.astype(f32)).astype(bf16)
        hi_terms = jnp.dot(hi, wr_ref[...], preferred_element_type=f32)
        lo_term = jnp.dot(lo, wr_ref[:, 0:LANES], preferred_element_type=f32)
        lg = hi_terms[:, 0:LANES] + hi_terms[:, LANES:2 * LANES] + lo_term + br_ref[...]
        lane = lax.broadcasted_iota(jnp.int32, (TM, LANES), 1)
        lane_f = lane.astype(f32)
        neg = -jnp.inf
        gl = jnp.where(lane < N_GROUPS, lg, neg)
        gmax, gidx = _first_lane_of_max(gl, lane_f)
        p_g = 1.0 / jnp.sum(jnp.exp(gl - gmax), axis=1, keepdims=True)
        e_lo = float(N_GROUPS) + float(EXPERTS_PER_GROUP) * gidx
        el = jnp.where((lane_f >= e_lo) & (lane_f < e_lo + float(EXPERTS_PER_GROUP)), lg, neg)
        v1, i1 = _first_lane_of_max(el, lane_f)
        v2, i2 = _first_lane_of_max(jnp.where(lane_f == i1, neg, el), lane_f)
        d = jnp.exp(v2 - v1)
        w1 = p_g / (1.0 + d)
        w2 = p_g * d / (1.0 + d)

        first_low = i1 < i2
        j_lo = jnp.minimum(i1, i2) - e_lo
        j_hi = jnp.maximum(i1, i2) - e_lo
        w_lo = jnp.where(first_low, w1, w2)
        w_hi = jnp.where(first_low, w2, w1)
        n_first = float(EXPERTS_PER_GROUP - 1) * j_lo - 0.5 * j_lo * (j_lo - 1.0)
        cls = float(PAIRS_PER_GROUP) * gidx + n_first + (j_hi - j_lo - 1.0)

        hot = lane_f == cls
        ti = lax.broadcasted_iota(jnp.int32, (TM, TM), 0)
        si = lax.broadcasted_iota(jnp.int32, (TM, TM), 1)
        tri = (si <= ti).astype(bf16)
        cum = jnp.dot(tri, hot.astype(bf16), preferred_element_type=f32)
        rank = jnp.sum(jnp.where(hot, cum - 1.0 + cnt_ref[...], 0.0), axis=1, keepdims=True)
        cnt_ref[...] = cnt_ref[...] + cum[TM - 1:TM, :]

        info = jnp.zeros((TM, LANES), f32)
        for c, val in enumerate((cls, rank, w_lo, w_hi)):
            info = jnp.where(lane == c, val, info)
        x1_ref[:, D_MODEL:D_MODEL + LANES] = info

    @pl.when(i < n_p)
    def _():
        run(xp_ref, rgp_ref, mp_ref)

    @pl.when(i >= n_p)
    def _():
        run(xs_ref, rgs_ref, ms_ref)


def _outproj(xp, xs, rgp, rgs, mp, ms, wo, g1, b1, wr, br):
    n_p, n_s = xp.shape[0] // TM, xs.shape[0] // TM
    n = n_p + n_s
    pmap = lambda i: (jnp.minimum(i, n_p - 1), 0)
    smap = lambda i: (jnp.maximum(i - n_p, 0), 0)
    const = lambda i: (0, 0)
    return pl.pallas_call(
        functools.partial(_outproj_kernel, n_p=n_p),
        grid=(n,),
        in_specs=[
            pl.BlockSpec((TM, D_MODEL), pmap), pl.BlockSpec((TM, D_MODEL), smap),
            pl.BlockSpec((TM, D_RG), pmap), pl.BlockSpec((TM, D_RG), smap),
            pl.BlockSpec((TM, D_M), pmap), pl.BlockSpec((TM, D_M), smap),
            pl.BlockSpec((D_RG + D_M, D_MODEL), const),
            pl.BlockSpec((1, D_MODEL), const), pl.BlockSpec((1, D_MODEL), const),
            pl.BlockSpec((D_MODEL, 2 * LANES), const),
            pl.BlockSpec((1, LANES), const),
        ],
        out_specs=[
            pl.BlockSpec((TM, X_ROW), lambda i: (i, 0)),
            pl.BlockSpec((1, LANES), const),
        ],
        out_shape=[
            jax.ShapeDtypeStruct((n * TM, X_ROW), f32),
            jax.ShapeDtypeStruct((1, LANES), f32),
        ],
        compiler_params=_cparams(1),
        name="outproj",
    )(xp, xs, rgp, rgs, mp, ms, wo, g1, b1, wr, br)


GRP_MOVE = TM_MOVE // SUBLANES


def _grouped(x):
    return x.reshape(x.shape[0] // SUBLANES, SUBLANES, x.shape[1])


def _tile_rows_wait(hbm_ref, sem):
    rows = hbm_ref.at[pl.ds(0, TM_MOVE)]
    pltpu.make_async_copy(rows, rows, sem).wait()


def _for_each_tile_row(body):
    def group(k, c):
        for j in range(SUBLANES):
            body(k, j, k * SUBLANES + j)
        return c

    lax.fori_loop(0, GRP_MOVE, group, 0)


def _dispatch_kernel(pad_ref, pos_ref, x1_ref, xs_ref, zero_scr, xbuf, sem, zsem, lsem):
    i = pl.program_id(0)
    n = pl.num_programs(0)

    def zero_tile(t):
        return pltpu.make_async_copy(zero_scr, xs_ref.at[pl.ds(pl.multiple_of(t * TM_MOE, TM_MOE), TM_MOE)], zsem)

    def zero_chunk(row0):
        return pltpu.make_async_copy(zero_scr.at[pl.ds(0, SUBLANES)],
                                     xs_ref.at[pl.ds(pl.multiple_of(row0, SUBLANES), SUBLANES)], zsem)

    @pl.when(i == 0)
    def _():
        zero_scr[...] = jnp.zeros_like(zero_scr)
        n_used = pad_ref[2 * N_CLASSES]
        n_tiles = xs_ref.shape[0] // TM_MOE
        for go in (lambda cp: cp.start(), lambda cp: cp.wait()):
            lax.fori_loop(n_used, n_tiles, lambda t, c: (go(zero_tile(t)), c)[1], 0)

            def per_class(cl, c):
                row0 = pad_ref[2 * cl]
                lax.fori_loop(0, pad_ref[2 * cl + 1],
                              lambda k, c2: (go(zero_chunk(row0 + k * SUBLANES)), c2)[1], 0)
                return c

            lax.fori_loop(0, N_CLASSES, per_class, 0)

    def load(t, s):
        return pltpu.make_async_copy(x1_ref.at[pl.ds(t * GRP_MOVE, GRP_MOVE)], xbuf.at[s], lsem.at[s])

    @pl.when(i == 0)
    def _():
        load(0, 0).start()

    @pl.when(i + 1 < n)
    def _():
        load(i + 1, (i + 1) % 3).start()

    slot = i % 3
    load(i, slot).wait()

    def start(k, j, r):
        pltpu.make_async_copy(xbuf.at[slot, k, pl.ds(j, 1)], xs_ref.at[pl.ds(pos_ref[0, 0, r], 1)],
                              sem.at[slot]).start()

    _for_each_tile_row(start)

    @pl.when(i > 0)
    def _():
        _tile_rows_wait(xs_ref, sem.at[(i + 2) % 3])

    @pl.when(i == n - 1)
    def _():
        _tile_rows_wait(xs_ref, sem.at[slot])


def _dispatch(pad_plan, pos, x1, n_rows):
    n = x1.shape[0] // TM_MOVE
    return pl.pallas_call(
        _dispatch_kernel,
        grid_spec=pltpu.PrefetchScalarGridSpec(
            num_scalar_prefetch=1,
            grid=(n,),
            in_specs=[
                pl.BlockSpec((1, 1, TM_MOVE), lambda i, *_: (i, 0, 0), memory_space=pltpu.SMEM),
                pl.BlockSpec(memory_space=pl.ANY),
            ],
            out_specs=pl.BlockSpec(memory_space=pl.ANY),
            scratch_shapes=[pltpu.VMEM((TM_MOE, X_ROW), f32),
                            pltpu.VMEM((3, GRP_MOVE, SUBLANES, X_ROW), f32),
                            pltpu.SemaphoreType.DMA((3,)), pltpu.SemaphoreType.DMA(()),
                            pltpu.SemaphoreType.DMA((3,))],
        ),
        out_shape=jax.ShapeDtypeStruct((n_rows, X_ROW), f32),
        compiler_params=_cparams(1),
        name="dispatch",
    )(pad_plan, pos, _grouped(x1))


def _expert_kernel(ta_ref, tb_ref, ca_ref, cb_ref, sa_ref, sb_ref, nu_ref,
                   x_ref, wg_hbm, wu_hbm, wd_hbm, g2_ref, b2_ref, y_ref,
                   wga, wua, wda, wgb, wub, wdb, sem_a, sem_b):
    i = pl.program_id(0)
    n_used = nu_ref[0]
    first = i * MOE_TILES
    side_a = (ta_ref, ca_ref, sa_ref, (wga, wua, wda), sem_a)
    side_b = (tb_ref, cb_ref, sb_ref, (wgb, wub, wdb), sem_b)

    def copies(side, j):
        t_ref, _, s_ref, bufs, sem = side
        e, s = t_ref[j], s_ref[j]
        return [pltpu.make_async_copy(w.at[e], buf.at[s], sem.at[s])
                for w, buf in zip((wg_hbm, wu_hbm, wd_hbm), bufs)]

    def fetch(j):
        @pl.when(j < n_used)
        def _():
            for side in (side_a, side_b):
                @pl.when(side[1][j] == 1)
                def _():
                    for cp in copies(side, j):
                        cp.start()

    @pl.when(i == 0)
    def _():
        for j in range(W_AHEAD):
            fetch(j)

    for t in range(MOE_TILES):
        fetch(first + W_AHEAD + t)

    @pl.when(first < n_used)
    def _():
        for t in range(MOE_TILES):
            for side in (side_a, side_b):
                @pl.when(side[1][first + t] == 1)
                def _():
                    for cp in copies(side, first + t):
                        cp.wait()
        rows = [slice(t * TM_MOE, (t + 1) * TM_MOE) for t in range(MOE_TILES)]
        x = [x_ref[r, 0:D_MODEL] for r in rows]
        info = [x_ref[r, D_MODEL:X_ROW] for r in rows]
        xb = [v.astype(bf16) for v in x]
        units = [(t, bufs, s_ref[first + t]) for t in range(MOE_TILES)
                 for bufs, s_ref in (((wga, wua, wda), sa_ref), ((wgb, wub, wdb), sb_ref))]
        hg = [jnp.dot(xb[t], bufs[0][s], preferred_element_type=f32) for t, bufs, s in units]
        hu = [jnp.dot(xb[t], bufs[1][s], preferred_element_type=f32) for t, bufs, s in units]
        mid = [(g * jax.nn.sigmoid(g) * u).astype(bf16) for g, u in zip(hg, hu)]
        ys = [jnp.dot(m, bufs[2][s], preferred_element_type=f32)
              for m, (t, bufs, s) in zip(mid, units)]
        for t in range(MOE_TILES):
            ffn = info[t][:, 2:3] * ys[2 * t] + info[t][:, 3:4] * ys[2 * t + 1]
            y_ref[rows[t], :] = _layer_norm(ALPHA * x[t] + ffn, g2_ref[...], b2_ref[...])

    @pl.when(first >= n_used)
    def _():
        y_ref[...] = jnp.zeros_like(y_ref)


def _experts(tile_ea, tile_eb, n_used, xs, wg, wu, wd, g2, b2):
    nt = xs.shape[0] // TM_MOE

    def ring_plan(tile_e):
        opens = jnp.concatenate([jnp.ones((1,), jnp.int32), (tile_e[1:] != tile_e[:-1]).astype(jnp.int32)])
        return opens, (jnp.cumsum(opens) - 1) % W_SLOTS

    open_a, slot_a = ring_plan(tile_ea)
    open_b, slot_b = ring_plan(tile_eb)
    assert nt % MOE_TILES == 0
    n_used = ((n_used + MOE_TILES - 1) // MOE_TILES) * MOE_TILES
    const = lambda i, *_: (0, 0)
    xmap = lambda i, ta, tb, ca, cb, sa, sb, nu: (jnp.minimum(i, nu[0] // MOE_TILES - 1), 0)
    w_in = pltpu.VMEM((W_SLOTS, D_MODEL, D_EXPERT), bf16)
    w_out = pltpu.VMEM((W_SLOTS, D_EXPERT, D_MODEL), bf16)
    return pl.pallas_call(
        _expert_kernel,
        grid_spec=pltpu.PrefetchScalarGridSpec(
            num_scalar_prefetch=7,
            grid=(nt // MOE_TILES,),
            in_specs=[
                pl.BlockSpec((MOE_TILES * TM_MOE, X_ROW), xmap),
                pl.BlockSpec(memory_space=pl.ANY), pl.BlockSpec(memory_space=pl.ANY),
                pl.BlockSpec(memory_space=pl.ANY),
                pl.BlockSpec((1, D_MODEL), const), pl.BlockSpec((1, D_MODEL), const),
            ],
            out_specs=pl.BlockSpec((MOE_TILES * TM_MOE, D_MODEL), lambda i, *_: (i, 0)),
            scratch_shapes=[w_in, w_in, w_out, w_in, w_in, w_out,
                            pltpu.SemaphoreType.DMA((W_SLOTS,)), pltpu.SemaphoreType.DMA((W_SLOTS,))],
        ),
        out_shape=jax.ShapeDtypeStruct((nt * TM_MOE, D_MODEL), f32),
        compiler_params=_cparams(1),
        name="experts",
    )(tile_ea, tile_eb, open_a, open_b, slot_a.astype(jnp.int32), slot_b.astype(jnp.int32), n_used,
      xs, wg, wu, wd, g2, b2)


def _collect_kernel(pos_ref, posn_ref, ys_ref, op_ref, os_ref, ybuf, sem, *, n_p):
    i = pl.program_id(0)
    n = pl.num_programs(0)
    slot = i % 2

    def gather(p_ref, s):
        def start(k, j, r):
            pltpu.make_async_copy(ys_ref.at[pl.ds(p_ref[0, 0, r], 1)], ybuf.at[s, k, pl.ds(j, 1)],
                                  sem.at[s]).start()

        _for_each_tile_row(start)

    @pl.when(i == 0)
    def _():
        gather(pos_ref, 0)

    @pl.when(i + 1 < n)
    def _():
        gather(posn_ref, 1 - slot)

    _tile_rows_wait(ys_ref, sem.at[slot])

    @pl.when(i < n_p)
    def _():
        op_ref[...] = ybuf[slot].reshape(TM_MOVE, D_MODEL)

    @pl.when(i >= n_p)
    def _():
        os_ref[...] = ybuf[slot].reshape(TM_MOVE, D_MODEL)


def _collect(pos, ys, n_p):
    n = pos.shape[0]
    n_s = n - n_p
    return pl.pallas_call(
        functools.partial(_collect_kernel, n_p=n_p),
        grid=(n,),
        in_specs=[
            pl.BlockSpec((1, 1, TM_MOVE), lambda i: (i, 0, 0), memory_space=pltpu.SMEM),
            pl.BlockSpec((1, 1, TM_MOVE), lambda i: (jnp.minimum(i + 1, n - 1), 0, 0),
                         memory_space=pltpu.SMEM),
            pl.BlockSpec(memory_space=pl.ANY),
        ],
        out_specs=[
            pl.BlockSpec((TM_MOVE, D_MODEL), lambda i: (jnp.minimum(i, n_p - 1), 0)),
            pl.BlockSpec((TM_MOVE, D_MODEL), lambda i: (jnp.maximum(i - n_p, 0), 0)),
        ],
        out_shape=[
            jax.ShapeDtypeStruct((n_p * TM_MOVE, D_MODEL), f32),
            jax.ShapeDtypeStruct((n_s * TM_MOVE, D_MODEL), f32),
        ],
        scratch_shapes=[pltpu.VMEM((2, GRP_MOVE, SUBLANES, D_MODEL), f32), pltpu.SemaphoreType.DMA((2,))],
        compiler_params=_cparams(1),
        name="collect",
    )(pos, pos, ys)


def _block_diag(w):
    n, d, _ = w.shape
    eye = jnp.eye(n, dtype=w.dtype)
    return (eye[:, None, :, None] * w[:, :, None, :]).reshape(n * d, n * d)


def _pair_walk(n):
    total = n * (n - 1) // 2

    def extend(path, used):
        if len(path) == total:
            return path
        a, b = path[-1]
        for c in range(n):
            for nxt in ((a, c), (c, b)):
                key = frozenset(nxt)
                if len(key) == 2 and key not in used:
                    out = extend(path + [nxt], used | {key})
                    if out:
                        return out
        return None

    return extend([(0, 1)], {frozenset((0, 1))})


def _pad_history(buf):
    return jnp.pad(buf, ((0, 0), (SUBLANES - (CONV_W - 1), 0), (0, 0)))


def kernel(x_prompt, x_sample, state_rg_h, state_rg_conv, state_m_C, state_m_n, state_m_m, state_m_conv, w_in, rg_conv_w, rg_conv_b, rg_w_a, rg_b_a, rg_w_x, rg_b_x, rg_lambda, m_conv_w, m_conv_b, m_w_q, m_w_k, m_w_v, m_b_i, m_b_f, m_norm_g, m_skip, w_out, ln1_g, ln1_b, ln2_g, ln2_b, moe_w_group, moe_b_group, moe_w_expert, moe_b_expert, moe_w_gate, moe_w_up, moe_w_down):
    BP, TP, _ = x_prompt.shape
    BS, TS, _ = x_sample.shape
    n_prompt, n_sample = BP * TP, BS * TS
    xp = x_prompt.reshape(n_prompt, D_MODEL)
    xs = x_sample.reshape(n_sample, D_MODEL)
    l = 0

    w_in_p = jnp.pad(w_in[l], ((0, 0), (0, N_PROJ - w_in.shape[-1]))).astype(bf16)
    wa = _block_diag(rg_w_a[l]).astype(bf16)
    wx = _block_diag(rg_w_x[l]).astype(bf16)
    row = lambda v: v.reshape(1, -1)
    bif = jnp.pad(jnp.concatenate([m_b_i[l], m_b_f[l]]), (0, LANES - 2 * M_HEADS)).reshape(1, LANES)
    wqk = jnp.concatenate([m_w_q[l], m_w_k[l]], axis=-1).astype(bf16)
    wv = m_w_v[l].astype(bf16)
    wkT = m_w_k[l].transpose(0, 2, 1).astype(bf16)
    w_route = jnp.pad(jnp.concatenate([moe_w_group[l], moe_w_expert[l]], axis=1),
                      ((0, 0), (0, LANES - N_GROUPS - N_EXPERTS)))
    wr_hi = w_route.astype(bf16)
    wr_lo = (w_route - wr_hi.astype(f32)).astype(bf16)
    wr = jnp.concatenate([wr_hi, wr_lo], axis=1)
    b_route = jnp.pad(jnp.concatenate([moe_b_group[l], moe_b_expert[l]]),
                      (0, LANES - N_GROUPS - N_EXPERTS)).reshape(1, LANES)

    proj_p, proj_s = _inproj(xp, xs, w_in_p)

    rg_args = (rg_conv_w[l], row(rg_conv_b[l]), wa, row(rg_b_a[l]), wx, row(rg_b_x[l]), row(rg_lambda[l]))
    zeros = lambda *s: jnp.zeros(s, f32)
    s_blk = M_ROWS // TS
    yrg_p, p_rg_h, p_rg_conv = _rglru(proj_p, 0, BP, TP, 1, RG_ROWS_PROMPT, zeros(BP, SUBLANES, D_RG),
                                      zeros(BP, 1, D_RG), *rg_args)
    yrg_s, s_rg_h, s_rg_conv = _rglru(proj_s, 0, BS, TS, s_blk, TS, _pad_history(state_rg_conv[l]),
                                      state_rg_h[l].reshape(BS, 1, D_RG), *rg_args)

    m_args = (m_conv_w[l], row(m_conv_b[l]), wqk, wv, wkT, bif, row(m_norm_g[l]), row(m_skip[l]))
    m0_p = jnp.full((SUBLANES, BP * LANES), M_INIT, f32)
    ym_p, p_m_C, p_m_n, p_m_m, p_m_conv, wg_b, wu_b, wd_b = _mlstm(
        proj_p, BP, TP, 1, M_ROWS, M_BLOCKS_PROMPT, zeros(BP, SUBLANES, D_M), zeros(BP, M_HEADS, M_DH, M_DH),
        zeros(BP, M_HEADS, 1, M_DH), m0_p, *m_args, riders=(moe_w_gate[l], moe_w_up[l], moe_w_down[l]))
    m0_s = jnp.pad(jnp.repeat(state_m_m[l].T, TS, axis=1), ((0, SUBLANES - M_HEADS), (0, 0)))
    ym_s, s_m_C, s_m_n, s_m_m, s_m_conv = _mlstm(
        proj_s, BS, TS, s_blk, TS, M_BLOCKS_SAMPLE, _pad_history(state_m_conv[l]), state_m_C[l],
        state_m_n[l].reshape(BS, M_HEADS, 1, M_DH), m0_s, *m_args)

    x1, cnt = _outproj(xp, xs, yrg_p, yrg_s, ym_p, ym_s, w_out[l].astype(bf16),
                       row(ln1_g[l]), row(ln1_b[l]), wr, b_route)

    n_tok = n_prompt + n_sample
    n_tiles = n_tok // TM_MOE + N_CLASSES
    lay_cls, lay_a, lay_b = [], [], []
    for g in range(N_GROUPS):
        for ja, jb in _pair_walk(EXPERTS_PER_GROUP):
            assert ja < jb
            n_first = (EXPERTS_PER_GROUP - 1) * ja - ja * (ja - 1) // 2
            lay_cls.append(PAIRS_PER_GROUP * g + n_first + jb - ja - 1)
            lay_a.append(g * EXPERTS_PER_GROUP + ja)
            lay_b.append(g * EXPERTS_PER_GROUP + jb)
    classes = jnp.arange(N_CLASSES, dtype=jnp.int32)
    in_slot = jnp.array(lay_cls, jnp.int32)[:, None] == classes
    counts = jnp.sum(jnp.where(in_slot, cnt[0, :N_CLASSES].astype(jnp.int32), 0), axis=1)
    padded = ((counts + TM_MOE - 1) // TM_MOE) * TM_MOE
    ends = jnp.cumsum(padded)
    offs = ends - padded
    offs_of_cls = jnp.sum(jnp.where(in_slot, offs[:, None], 0), axis=0)
    cls = x1[:, D_MODEL].astype(jnp.int32)
    rank = x1[:, D_MODEL + 1].astype(jnp.int32)
    pos = jnp.sum(jnp.where(cls[:, None] == classes, offs_of_cls, 0), axis=-1) + rank
    pos = pos.reshape(n_tok // TM_MOVE, 1, TM_MOVE)
    tiles = jnp.arange(n_tiles, dtype=jnp.int32)
    tile_slot = jnp.minimum(jnp.sum(tiles[:, None] >= (ends // TM_MOE)[None, :], axis=1), N_CLASSES - 1)
    on_slot = tile_slot[:, None] == classes
    tile_ea = jnp.sum(jnp.where(on_slot, jnp.array(lay_a, jnp.int32), 0), axis=-1).astype(jnp.int32)
    tile_eb = jnp.sum(jnp.where(on_slot, jnp.array(lay_b, jnp.int32), 0), axis=-1).astype(jnp.int32)
    n_used = (ends[-1] // TM_MOE).reshape(1).astype(jnp.int32)
    pad_row0 = ((offs + counts) // SUBLANES) * SUBLANES
    pad_plan = jnp.concatenate([jnp.stack([pad_row0, (ends - pad_row0) // SUBLANES], axis=1).reshape(-1),
                                     n_used]).astype(jnp.int32)

    x_sorted = _dispatch(pad_plan, pos, x1, n_tiles * TM_MOE)
    y_sorted = _experts(tile_ea, tile_eb, n_used, x_sorted, wg_b, wu_b, wd_b,
                        row(ln2_g[l]), row(ln2_b[l]))
    y_p, y_s = _collect(pos, y_sorted, n_prompt // TM_MOVE)

    return (y_p.reshape(BP, TP, D_MODEL), y_s.reshape(BS, TS, D_MODEL),
            p_rg_h.reshape(1, BP, D_RG), p_rg_conv[None], p_m_C[None], p_m_n.reshape(1, BP, M_HEADS, M_DH),
            p_m_m[None], p_m_conv[None],
            s_rg_h.reshape(1, BS, D_RG), s_rg_conv[None], s_m_C[None], s_m_n.reshape(1, BS, M_HEADS, M_DH),
            s_m_m[None], s_m_conv[None])
```

```python
import functools

import jax
import jax.numpy as jnp
from jax import lax
from jax.experimental import pallas as pl
from jax.experimental.pallas import tpu as pltpu

f32 = jnp.float32
bf16 = jnp.bfloat16

D_MODEL = 1024
D_RG = 512
RG_C = 8.0
D_M = 512
M_HEADS = 4
M_DH = 128
CONV_W = 4
N_GROUPS = 4
EXPERTS_PER_GROUP = 8
N_EXPERTS = 32
D_EXPERT = 256
ALPHA = 2.0 ** 0.25
LN_EPS = 1e-5
M_INIT = -1.0e4

LANES = 128
SUBLANES = 8
TM = 512
TM_MOVE = 1024
TM_MOE = 128
MOE_TILES = 4
W_AHEAD = 4
W_SLOTS = W_AHEAD + MOE_TILES
PAIRS_PER_GROUP = EXPERTS_PER_GROUP * (EXPERTS_PER_GROUP - 1) // 2
N_CLASSES = N_GROUPS * PAIRS_PER_GROUP
X_ROW = D_MODEL + LANES
N_PROJ = 2 * D_RG + 2 * D_M + LANES
RG_ROWS_PROMPT = 512
M_ROWS = 128
M_BLOCKS_PROMPT = 8
M_BLOCKS_SAMPLE = 2
V7X_VMEM_BYTES = 64 * 1024 * 1024
VMEM_LIMIT = V7X_VMEM_BYTES - 8 * 1024 * 1024


def _cparams(n_axes):
    return pltpu.CompilerParams(dimension_semantics=("arbitrary",) * n_axes,
                                vmem_limit_bytes=VMEM_LIMIT)


def _inproj_kernel(xp_ref, xs_ref, w_ref, op_ref, os_ref, *, n_p):
    i = pl.program_id(0)

    def run(x_ref, o_ref):
        o_ref[...] = jnp.dot(x_ref[...].astype(bf16), w_ref[...], preferred_element_type=f32)

    @pl.when(i < n_p)
    def _():
        run(xp_ref, op_ref)

    @pl.when(i >= n_p)
    def _():
        run(xs_ref, os_ref)


def _inproj(xp, xs, w):
    n_p, n_s = xp.shape[0] // TM, xs.shape[0] // TM
    pmap = lambda i: (jnp.minimum(i, n_p - 1), 0)
    smap = lambda i: (jnp.maximum(i - n_p, 0), 0)
    return pl.pallas_call(
        functools.partial(_inproj_kernel, n_p=n_p),
        grid=(n_p + n_s,),
        in_specs=[
            pl.BlockSpec((TM, D_MODEL), pmap),
            pl.BlockSpec((TM, D_MODEL), smap),
            pl.BlockSpec((D_MODEL, N_PROJ), lambda i: (0, 0)),
        ],
        out_specs=[pl.BlockSpec((TM, N_PROJ), pmap), pl.BlockSpec((TM, N_PROJ), smap)],
        out_shape=[jax.ShapeDtypeStruct((n_p * TM, N_PROJ), f32),
                   jax.ShapeDtypeStruct((n_s * TM, N_PROJ), f32)],
        compiler_params=_cparams(1),
        name="inproj",
    )(xp, xs, w)


def _causal_conv(tail_scr, x, cw_ref, cb_ref, S, L, C):
    tail = tail_scr[...]
    sub = lax.broadcasted_iota(jnp.int32, (S, SUBLANES, C), 1)
    acc = cb_ref[...] + cw_ref[CONV_W - 1:CONV_W, :] * x
    for d in range(1, CONV_W):
        back = pltpu.roll(x, d, 0).reshape(S, L, C)
        head = jnp.where(sub < d, pltpu.roll(tail, d, 1), back[:, 0:SUBLANES, :])
        if L > SUBLANES:
            back = jnp.concatenate([head, back[:, SUBLANES:, :]], axis=1)
        else:
            back = head
        acc = acc + cw_ref[CONV_W - 1 - d:CONV_W - d, :] * back.reshape(S * L, C)
    tail_scr[...] = x.reshape(S, L, C)[:, L - SUBLANES:, :]
    return acc


def _softplus(x):
    return jnp.maximum(x, 0.0) + jnp.log1p(jnp.exp(-jnp.abs(x)))


def _rglru_kernel(x_ref, g_ref, buf_ref, h0_ref, cw_ref, cb_ref, wa_ref, ba_ref, wx_ref, bx_ref,
                  lam_ref, y_ref, hN_ref, cN_ref, xs_scr, *, S, L):
    R = S * L
    t = pl.program_id(1)

    @pl.when(t == 0)
    def _():
        xs_scr[...] = buf_ref[...]
        hN_ref[...] = h0_ref[...]

    x = x_ref[...]
    xc = _causal_conv(xs_scr, x, cw_ref, cb_ref, S, L, D_RG)
    xcb = xc.astype(bf16)
    r = jax.nn.sigmoid(jnp.dot(xcb, wa_ref[...], preferred_element_type=f32) + ba_ref[...])
    ig = jax.nn.sigmoid(jnp.dot(xcb, wx_ref[...], preferred_element_type=f32) + bx_ref[...])
    log_a = (-RG_C) * r * _softplus(-lam_ref[...])
    a = jnp.exp(log_a)
    th = jnp.tanh(log_a)
    u = jnp.sqrt(-2.0 * th / (1.0 - th)) * ig * xc

    n_grp, grp_per_seq = R // SUBLANES, L // SUBLANES
    a3 = a.reshape(n_grp, SUBLANES, D_RG)
    u3 = u.reshape(n_grp, SUBLANES, D_RG)
    sub = lax.broadcasted_iota(jnp.int32, (n_grp, SUBLANES, D_RG), 1)
    s = 1
    while s < SUBLANES:
        ok = sub >= s
        a_sh = pltpu.roll(a3, s, 1)
        u_sh = pltpu.roll(u3, s, 1)
        u3 = jnp.where(ok, a3 * u_sh + u3, u3)
        a3 = jnp.where(ok, a3 * a_sh, a3)
        s *= 2
    h0 = hN_ref[...]
    groups = []
    for kg in range(n_grp):
        carry = h0[kg // grp_per_seq] if kg % grp_per_seq == 0 else groups[-1][SUBLANES - 1:SUBLANES, :]
        groups.append(a3[kg] * carry + u3[kg])
    h = jnp.concatenate(groups, axis=0)

    y_ref[...] = h * jax.nn.gelu(g_ref[...], approximate=True)
    hN_ref[...] = h.reshape(S, L, D_RG)[:, L - 1:L, :]
    cN_ref[...] = xs_scr[:, SUBLANES - (CONV_W - 1):SUBLANES, :]


def _rglru(proj, row0, B, T, S, L, buf8, h0, cw, cb, wa, ba, wx, bx, lam):
    R = S * L
    nb, nt = B // S, T // L
    blk0 = row0 // R
    row_map = lambda b, t: (blk0 + b * nt + t, 0)
    const2 = lambda b, t: (0, 0)
    return pl.pallas_call(
        functools.partial(_rglru_kernel, S=S, L=L),
        grid=(nb, nt),
        in_specs=[
            pl.BlockSpec((R, D_RG), row_map),
            pl.BlockSpec((R, D_RG), lambda b, t: (blk0 + b * nt + t, 1)),
            pl.BlockSpec((S, SUBLANES, D_RG), lambda b, t: (b, 0, 0)),
            pl.BlockSpec((S, 1, D_RG), lambda b, t: (b, 0, 0)),
            pl.BlockSpec((CONV_W, D_RG), const2),
            pl.BlockSpec((1, D_RG), const2),
            pl.BlockSpec((D_RG, D_RG), const2),
            pl.BlockSpec((1, D_RG), const2),
            pl.BlockSpec((D_RG, D_RG), const2),
            pl.BlockSpec((1, D_RG), const2),
            pl.BlockSpec((1, D_RG), const2),
        ],
        out_specs=[
            pl.BlockSpec((R, D_RG), lambda b, t: (b * nt + t, 0)),
            pl.BlockSpec((S, 1, D_RG), lambda b, t: (b, 0, 0)),
            pl.BlockSpec((S, CONV_W - 1, D_RG), lambda b, t: (b, 0, 0)),
        ],
        out_shape=[
            jax.ShapeDtypeStruct((B * T, D_RG), f32),
            jax.ShapeDtypeStruct((B, 1, D_RG), f32),
            jax.ShapeDtypeStruct((B, CONV_W - 1, D_RG), f32),
        ],
        scratch_shapes=[pltpu.VMEM((S, SUBLANES, D_RG), f32)],
        compiler_params=_cparams(2),
        name="rglru",
    )(proj, proj, buf8, h0, cw, cb, wa, ba, wx, bx, lam)


def _seg_scan(x, op, fill, tin, L, reverse=False):
    s = 1
    while s < L:
        if reverse:
            sh = pltpu.roll(x, LANES - s, 1)
            ok = tin < L - s
        else:
            sh = pltpu.roll(x, s, 1)
            ok = tin >= s
        x = op(x, jnp.where(ok, sh, fill))
        s *= 2
    return x


def _mlstm_kernel(*refs, S, L, G, n_riders):
    (x_ref, z_ref, gt_ref, buf_ref, c0_ref, n0_ref, m0_ref,
     cw_ref, cb_ref, wqk_ref, wv_ref, wkT_ref, bif_ref, ng_ref, sk_ref) = refs[:15]
    rider_in = refs[15:15 + n_riders]
    y_ref, cN_ref, nN_ref, mN_ref, bN_ref = refs[15 + n_riders:20 + n_riders]
    rider_out = refs[20 + n_riders:20 + 2 * n_riders]
    xs_scr, m_scr = refs[20 + 2 * n_riders:]
    for src, dst in zip(rider_in, rider_out):
        dst[...] = src[...].astype(bf16)

    R = S * L
    GS, GR = G * S, G * R
    j = pl.program_id(1)

    @pl.when(j == 0)
    def _():
        xs_scr[...] = buf_ref[...]
        cN_ref[...] = c0_ref[...]
        nN_ref[...] = n0_ref[...]
        m_scr[...] = m0_ref[...]

    x = x_ref[...].reshape(GR, D_M)
    xc = _causal_conv(xs_scr, x, cw_ref, cb_ref, GS, L, D_M)
    xa = xc * jax.nn.sigmoid(xc)
    xab = xa.astype(bf16)
    xb = x.astype(bf16)

    il_parts, f_parts = [], []
    for g in range(G):
        gT = (gt_ref[g] + bif_ref[...]).T
        il_parts.append(gT[0:SUBLANES, :])
        f_parts.append(pltpu.roll(gT[0:SUBLANES, :], M_HEADS, 0))
    il = jnp.concatenate(il_parts, axis=0)
    fl = -_softplus(-jnp.concatenate(f_parts, axis=0))
    tin = lax.broadcasted_iota(jnp.int32, (G * SUBLANES, LANES), 1) % L
    bcum = _seg_scan(fl, jnp.add, 0.0, tin, L)
    a = il - bcum
    m_prev = m_scr[...].reshape(G * SUBLANES, LANES)
    big_m = jnp.maximum(m_prev, _seg_scan(a, jnp.maximum, -jnp.inf, tin, L))
    m_t = bcum + big_m
    if S == 1:
        m_last = jnp.broadcast_to(big_m[:, LANES - 1:LANES], big_m.shape)
    else:
        m_last = _seg_scan(big_m, jnp.maximum, -jnp.inf, tin, L, reverse=True)
    rows = [big_m, jnp.exp(m_prev - big_m), jnp.exp(-m_t), jnp.exp(a - m_last), jnp.exp(m_prev - m_last)]
    mN_ref[...] = m_t.reshape(G, SUBLANES, LANES)
    m_scr[...] = jnp.broadcast_to(m_t[:, LANES - 1:LANES], m_t.shape).reshape(G, SUBLANES, LANES)
    pad_rows = jnp.zeros((LANES - len(rows) * SUBLANES, LANES), f32)
    cols = []
    for g in range(G):
        gs = slice(g * SUBLANES, (g + 1) * SUBLANES)
        cols.append(jnp.concatenate([r[gs] for r in rows] + [pad_rows], axis=0).T)

    def col(g, q, h):
        return cols[g][:, SUBLANES * q + h:SUBLANES * q + h + 1]

    qs, ks, vs, kTs = [], [], [], []
    for h in range(M_HEADS):
        hs = slice(h * M_DH, (h + 1) * M_DH)
        qk_h = jnp.dot(xab[:, hs], wqk_ref[h], preferred_element_type=f32)
        qs.append(qk_h[:, 0:M_DH])
        ks.append(qk_h[:, M_DH:2 * M_DH] * (M_DH ** -0.5))
        vs.append(jnp.dot(xb[:, hs], wv_ref[h], preferred_element_type=f32))
        if S == 1:
            kT_h = lax.dot_general(wkT_ref[h], xab[:, hs], (((1,), (1,)), ((), ())),
                                   preferred_element_type=f32)
            kTs.append((kT_h * (M_DH ** -0.5)).astype(bf16))

    ti = lax.broadcasted_iota(jnp.int32, (R, R), 0)
    si = lax.broadcasted_iota(jnp.int32, (R, R), 1)
    mask = (si <= ti) & ((ti // L) == (si // L))
    ones_b = jnp.ones((R, M_DH), bf16)
    pairs = [(g, h) for g in range(G) for h in range(M_HEADS)]
    blk = lambda g: slice(g * R, (g + 1) * R)
    seqs = [slice(b * L, (b + 1) * L) for b in range(S)]
    q = {(g, h): qs[h][blk(g)] for g, h in pairs}
    k = {(g, h): ks[h][blk(g)] for g, h in pairs}
    v = {(g, h): vs[h][blk(g)] for g, h in pairs}
    qb = {p: q[p].astype(bf16) for p in pairs}
    kb = {p: k[p].astype(bf16) for p in pairs}
    qk = {p: lax.dot_general(qb[p], kb[p], (((1,), (1,)), ((), ())), preferred_element_type=f32)
          for p in pairs}
    sm = {}
    for g, h in pairs:
        a_row = a[g * SUBLANES + h:g * SUBLANES + h + 1, :]
        decay = jnp.exp(jnp.where(mask, a_row - col(g, 0, h), -jnp.inf))
        sm[g, h] = (qk[g, h] * decay).astype(bf16)
    nd = {p: jnp.dot(sm[p], jnp.concatenate([v[p].astype(bf16), ones_b], axis=1),
                     preferred_element_type=f32) for p in pairs}
    c_old = {(g, h, b): cN_ref[g * S + b, h] for g, h in pairs for b in range(S)}
    n_old = {(g, h, b): nN_ref[g * S + b, h] for g, h in pairs for b in range(S)}
    q_c = {(g, h, b): jnp.dot(qb[g, h][seqs[b]], c_old[g, h, b].astype(bf16), preferred_element_type=f32)
           for g, h in pairs for b in range(S)}
    hh = {}
    for g, h in pairs:
        g_col, e_col = col(g, 1, h), col(g, 2, h)
        pieces = []
        for b, rs in enumerate(seqs):
            q_n = jnp.sum(q[g, h][rs] * n_old[g, h, b], axis=1, keepdims=True)
            num = nd[g, h][rs, 0:M_DH] + g_col[rs] * q_c[g, h, b]
            den = nd[g, h][rs, M_DH:2 * M_DH] + g_col[rs] * q_n
            pieces.append(num / jnp.maximum(jnp.abs(den), e_col[rs]))
        hh[g, h] = pieces[0] if S == 1 else jnp.concatenate(pieces, axis=0)
    mu = {p: jnp.mean(hh[p], axis=1, keepdims=True) for p in pairs}
    var = {p: jnp.mean(jnp.square(hh[p] - mu[p]), axis=1, keepdims=True) for p in pairs}
    hn_blocks = [jnp.concatenate([(hh[g, h] - mu[g, h]) * lax.rsqrt(var[g, h] + LN_EPS)
                                  for h in range(M_HEADS)], axis=1) for g in range(G)]
    new_c, new_n = [], []
    for g, h in pairs:
        w_col = col(g, 3, h)
        wv = (w_col * v[g, h]).astype(bf16)
        wk = w_col * k[g, h]
        for b, rs in enumerate(seqs):
            g_end = col(g, 4, h)[(b + 1) * L - 1:(b + 1) * L, :]
            if S == 1:
                kv = jnp.dot(kTs[h][:, blk(g)], wv, preferred_element_type=f32)
            else:
                kv = lax.dot_general(kb[g, h][rs], wv[rs], (((0,), (0,)), ((), ())),
                                     preferred_element_type=f32)
            new_c.append((g * S + b, h, g_end * c_old[g, h, b] + kv))
            new_n.append((g * S + b, h, g_end * n_old[g, h, b] + jnp.sum(wk[rs], axis=0, keepdims=True)))

    hn = jnp.concatenate(hn_blocks, axis=0) * ng_ref[...]
    y = jax.nn.sigmoid(z_ref[...].reshape(GR, D_M)) * (hn + sk_ref[...] * xa)
    y_ref[...] = y.reshape(G, R, D_M)
    for sq, h, val in new_c:
        cN_ref[sq, h] = val
    for sq, h, val in new_n:
        nN_ref[sq, h] = val
    bN_ref[...] = xs_scr[:, SUBLANES - (CONV_W - 1):SUBLANES, :]


def _mlstm(proj, B, T, S, L, G, buf8, c0, n0, m0_lanes, cw, cb, wqk, wv, wkT, bif, ng, sk, riders=()):
    R = S * L
    assert R == M_ROWS == LANES
    nblk, nc = B // S, T // L
    nb = nblk // G
    assert nblk * S == B and nc * L == T and nb * G == nblk and (S == 1 or nc == 1)
    steps = nb * nc
    assert all(r.shape[0] % steps == 0 for r in riders)
    rider_specs = [pl.BlockSpec((r.shape[0] // steps,) + r.shape[1:], lambda b, j: (b * nc + j, 0, 0))
                   for r in riders]
    proj4 = proj.reshape(nblk, nc, R, N_PROJ)
    m0_3 = m0_lanes.reshape(SUBLANES, nblk, LANES).transpose(1, 0, 2)
    const2 = lambda b, j: (0, 0)
    const3 = lambda b, j: (0, 0, 0)
    gate_blk = (2 * D_RG + 2 * D_M) // LANES
    GS = G * S
    outs = pl.pallas_call(
        functools.partial(_mlstm_kernel, S=S, L=L, G=G, n_riders=len(riders)),
        grid=(nb, nc),
        in_specs=[
            pl.BlockSpec((G, None, R, D_M), lambda b, j: (b, j, 0, 2)),
            pl.BlockSpec((G, None, R, D_M), lambda b, j: (b, j, 0, 3)),
            pl.BlockSpec((G, None, R, LANES), lambda b, j: (b, j, 0, gate_blk)),
            pl.BlockSpec((GS, SUBLANES, D_M), lambda b, j: (b, 0, 0)),
            pl.BlockSpec((GS, M_HEADS, M_DH, M_DH), lambda b, j: (b, 0, 0, 0)),
            pl.BlockSpec((GS, M_HEADS, 1, M_DH), lambda b, j: (b, 0, 0, 0)),
            pl.BlockSpec((G, SUBLANES, LANES), lambda b, j: (b, 0, 0)),
            pl.BlockSpec((CONV_W, D_M), const2),
            pl.BlockSpec((1, D_M), const2),
            pl.BlockSpec((M_HEADS, M_DH, 2 * M_DH), const3),
            pl.BlockSpec((M_HEADS, M_DH, M_DH), const3),
            pl.BlockSpec((M_HEADS, M_DH, M_DH), const3),
            pl.BlockSpec((1, LANES), const2),
            pl.BlockSpec((1, D_M), const2),
            pl.BlockSpec((1, D_M), const2),
        ] + rider_specs,
        out_specs=[
            pl.BlockSpec((G, None, R, D_M), lambda b, j: (b, j, 0, 0)),
            pl.BlockSpec((GS, M_HEADS, M_DH, M_DH), lambda b, j: (b, 0, 0, 0)),
            pl.BlockSpec((GS, M_HEADS, 1, M_DH), lambda b, j: (b, 0, 0, 0)),
            pl.BlockSpec((G, None, SUBLANES, LANES), lambda b, j: (b, j, 0, 0)),
            pl.BlockSpec((GS, CONV_W - 1, D_M), lambda b, j: (b, 0, 0)),
        ] + rider_specs,
        out_shape=[
            jax.ShapeDtypeStruct((nblk, nc, R, D_M), f32),
            jax.ShapeDtypeStruct((B, M_HEADS, M_DH, M_DH), f32),
            jax.ShapeDtypeStruct((B, M_HEADS, 1, M_DH), f32),
            jax.ShapeDtypeStruct((nblk, nc, SUBLANES, LANES), f32),
            jax.ShapeDtypeStruct((B, CONV_W - 1, D_M), f32),
        ] + [jax.ShapeDtypeStruct(r.shape, bf16) for r in riders],
        scratch_shapes=[pltpu.VMEM((GS, SUBLANES, D_M), f32), pltpu.VMEM((G, SUBLANES, LANES), f32)],
        compiler_params=_cparams(2),
        name="mlstm",
    )(proj4, proj4, proj4, buf8, c0, n0, m0_3, cw, cb, wqk, wv, wkT, bif, ng, sk, *riders)
    y, c_new, n_new, m_t, b_new = outs[:5]
    m_last = m_t[:, nc - 1, :M_HEADS, :].reshape(nblk, M_HEADS, S, L)[:, :, :, L - 1]
    m_last = m_last.transpose(0, 2, 1).reshape(B, M_HEADS)
    return (y.reshape(B * T, D_M), c_new, n_new, m_last, b_new) + tuple(outs[5:])


def _layer_norm(x, g, b):
    mu = jnp.mean(x, axis=-1, keepdims=True)
    var = jnp.mean(jnp.square(x - mu), axis=-1, keepdims=True)
    return (x - mu) * lax.rsqrt(var + LN_EPS) * g + b


def _first_lane_of_max(vals, lane_f):
    vmax = jnp.max(vals, axis=1, keepdims=True)
    idx = jnp.min(jnp.where(vals == vmax, lane_f, float(LANES)), axis=1, keepdims=True)
    return vmax, idx


def _outproj_kernel(xp_ref, xs_ref, rgp_ref, rgs_ref, mp_ref, ms_ref, wo_ref, g1_ref, b1_ref,
                    wr_ref, br_ref, x1_ref, cnt_ref, *, n_p):
    i = pl.program_id(0)

    @pl.when(i == 0)
    def _():
        cnt_ref[...] = jnp.zeros_like(cnt_ref)

    def run(x_ref, rg_ref, m_ref):
        heads = jnp.concatenate([rg_ref[...].astype(bf16), m_ref[...].astype(bf16)], axis=1)
        mix = jnp.dot(heads, wo_ref[...], preferred_element_type=f32)
        x1 = _layer_norm(ALPHA * x_ref[...] + mix, g1_ref[...], b1_ref[...])
        x1_ref[:, 0:D_MODEL] = x1

        hi = x1.astype(bf16)
        lo = (x1 - hi.astype(f32)).astype(bf16)
        hi_terms = jnp.dot(hi, wr_ref[...], preferred_element_type=f32)
        lo_term = jnp.dot(lo, wr_ref[:, 0:LANES], preferred_element_type=f32)
        lg = hi_terms[:, 0:LANES] + hi_terms[:, LANES:2 * LANES] + lo_term + br_ref[...]
        lane = lax.broadcasted_iota(jnp.int32, (TM, LANES), 1)
        lane_f = lane.astype(f32)
        neg = -jnp.inf
        gl = jnp.where(lane < N_GROUPS, lg, neg)
        gmax, gidx = _first_lane_of_max(gl, lane_f)
        p_g = 1.0 / jnp.sum(jnp.exp(gl - gmax), axis=1, keepdims=True)
        e_lo = float(N_GROUPS) + float(EXPERTS_PER_GROUP) * gidx
        el = jnp.where((lane_f >= e_lo) & (lane_f < e_lo + float(EXPERTS_PER_GROUP)), lg, neg)
        v1, i1 = _first_lane_of_max(el, lane_f)
        v2, i2 = _first_lane_of_max(jnp.where(lane_f == i1, neg, el), lane_f)
        d = jnp.exp(v2 - v1)
        w1 = p_g / (1.0 + d)
        w2 = p_g * d / (1.0 + d)

        first_low = i1 < i2
        j_lo = jnp.minimum(i1, i2) - e_lo
        j_hi = jnp.maximum(i1, i2) - e_lo
        w_lo = jnp.where(first_low, w1, w2)
        w_hi = jnp.where(first_low, w2, w1)
        n_first = float(EXPERTS_PER_GROUP - 1) * j_lo - 0.5 * j_lo * (j_lo - 1.0)
        cls = float(PAIRS_PER_GROUP) * gidx + n_first + (j_hi - j_lo - 1.0)

        hot = lane_f == cls
        ti = lax.broadcasted_iota(jnp.int32, (TM, TM), 0)
        si = lax.broadcasted_iota(jnp.int32, (TM, TM), 1)
        tri = (si <= ti).astype(bf16)
        cum = jnp.dot(tri, hot.astype(bf16), preferred_element_type=f32)
        rank = jnp.sum(jnp.where(hot, cum - 1.0 + cnt_ref[...], 0.0), axis=1, keepdims=True)
        cnt_ref[...] = cnt_ref[...] + cum[TM - 1:TM, :]

        info = jnp.zeros((TM, LANES), f32)
        for c, val in enumerate((cls, rank, w_lo, w_hi)):
            info = jnp.where(lane == c, val, info)
        x1_ref[:, D_MODEL:D_MODEL + LANES] = info

    @pl.when(i < n_p)
    def _():
        run(xp_ref, rgp_ref, mp_ref)

    @pl.when(i >= n_p)
    def _():
        run(xs_ref, rgs_ref, ms_ref)


def _outproj(xp, xs, rgp, rgs, mp, ms, wo, g1, b1, wr, br):
    n_p, n_s = xp.shape[0] // TM, xs.shape[0] // TM
    n = n_p + n_s
    pmap = lambda i: (jnp.minimum(i, n_p - 1), 0)
    smap = lambda i: (jnp.maximum(i - n_p, 0), 0)
    const = lambda i: (0, 0)
    return pl.pallas_call(
        functools.partial(_outproj_kernel, n_p=n_p),
        grid=(n,),
        in_specs=[
            pl.BlockSpec((TM, D_MODEL), pmap), pl.BlockSpec((TM, D_MODEL), smap),
            pl.BlockSpec((TM, D_RG), pmap), pl.BlockSpec((TM, D_RG), smap),
            pl.BlockSpec((TM, D_M), pmap), pl.BlockSpec((TM, D_M), smap),
            pl.BlockSpec((D_RG + D_M, D_MODEL), const),
            pl.BlockSpec((1, D_MODEL), const), pl.BlockSpec((1, D_MODEL), const),
            pl.BlockSpec((D_MODEL, 2 * LANES), const),
            pl.BlockSpec((1, LANES), const),
        ],
        out_specs=[
            pl.BlockSpec((TM, X_ROW), lambda i: (i, 0)),
            pl.BlockSpec((1, LANES), const),
        ],
        out_shape=[
            jax.ShapeDtypeStruct((n * TM, X_ROW), f32),
            jax.ShapeDtypeStruct((1, LANES), f32),
        ],
        compiler_params=_cparams(1),
        name="outproj",
    )(xp, xs, rgp, rgs, mp, ms, wo, g1, b1, wr, br)


GRP_MOVE = TM_MOVE // SUBLANES


def _grouped(x):
    return x.reshape(x.shape[0] // SUBLANES, SUBLANES, x.shape[1])


def _tile_rows_wait(hbm_ref, sem):
    rows = hbm_ref.at[pl.ds(0, TM_MOVE)]
    pltpu.make_async_copy(rows, rows, sem).wait()


def _for_each_tile_row(body):
    def group(k, c):
        for j in range(SUBLANES):
            body(k, j, k * SUBLANES + j)
        return c

    lax.fori_loop(0, GRP_MOVE, group, 0)


def _dispatch_kernel(pad_ref, pos_ref, x1_ref, xs_ref, zero_scr, xbuf, sem, zsem, lsem):
    i = pl.program_id(0)
    n = pl.num_programs(0)

    def zero_tile(t):
        return pltpu.make_async_copy(zero_scr, xs_ref.at[pl.ds(pl.multiple_of(t * TM_MOE, TM_MOE), TM_MOE)], zsem)

    def zero_chunk(row0):
        return pltpu.make_async_copy(zero_scr.at[pl.ds(0, SUBLANES)],
                                     xs_ref.at[pl.ds(pl.multiple_of(row0, SUBLANES), SUBLANES)], zsem)

    @pl.when(i == 0)
    def _():
        zero_scr[...] = jnp.zeros_like(zero_scr)
        n_used = pad_ref[2 * N_CLASSES]
        n_tiles = xs_ref.shape[0] // TM_MOE
        for go in (lambda cp: cp.start(), lambda cp: cp.wait()):
            lax.fori_loop(n_used, n_tiles, lambda t, c: (go(zero_tile(t)), c)[1], 0)

            def per_class(cl, c):
                row0 = pad_ref[2 * cl]
                lax.fori_loop(0, pad_ref[2 * cl + 1],
                              lambda k, c2: (go(zero_chunk(row0 + k * SUBLANES)), c2)[1], 0)
                return c

            lax.fori_loop(0, N_CLASSES, per_class, 0)

    def load(t, s):
        return pltpu.make_async_copy(x1_ref.at[pl.ds(t * GRP_MOVE, GRP_MOVE)], xbuf.at[s], lsem.at[s])

    @pl.when(i == 0)
    def _():
        load(0, 0).start()

    @pl.when(i + 1 < n)
    def _():
        load(i + 1, (i + 1) % 3).start()

    slot = i % 3
    load(i, slot).wait()

    def start(k, j, r):
        pltpu.make_async_copy(xbuf.at[slot, k, pl.ds(j, 1)], xs_ref.at[pl.ds(pos_ref[0, 0, r], 1)],
                              sem.at[slot]).start()

    _for_each_tile_row(start)

    @pl.when(i > 0)
    def _():
        _tile_rows_wait(xs_ref, sem.at[(i + 2) % 3])

    @pl.when(i == n - 1)
    def _():
        _tile_rows_wait(xs_ref, sem.at[slot])


def _dispatch(pad_plan, pos, x1, n_rows):
    n = x1.shape[0] // TM_MOVE
    return pl.pallas_call(
        _dispatch_kernel,
        grid_spec=pltpu.PrefetchScalarGridSpec(
            num_scalar_prefetch=1,
            grid=(n,),
            in_specs=[
                pl.BlockSpec((1, 1, TM_MOVE), lambda i, *_: (i, 0, 0), memory_space=pltpu.SMEM),
                pl.BlockSpec(memory_space=pl.ANY),
            ],
            out_specs=pl.BlockSpec(memory_space=pl.ANY),
            scratch_shapes=[pltpu.VMEM((TM_MOE, X_ROW), f32),
                            pltpu.VMEM((3, GRP_MOVE, SUBLANES, X_ROW), f32),
                            pltpu.SemaphoreType.DMA((3,)), pltpu.SemaphoreType.DMA(()),
                            pltpu.SemaphoreType.DMA((3,))],
        ),
        out_shape=jax.ShapeDtypeStruct((n_rows, X_ROW), f32),
        compiler_params=_cparams(1),
        name="dispatch",
    )(pad_plan, pos, _grouped(x1))


def _expert_kernel(ta_ref, tb_ref, ca_ref, cb_ref, sa_ref, sb_ref, nu_ref,
                   x_ref, wg_hbm, wu_hbm, wd_hbm, g2_ref, b2_ref, y_ref,
                   wga, wua, wda, wgb, wub, wdb, sem_a, sem_b):
    i = pl.program_id(0)
    n_used = nu_ref[0]
    first = i * MOE_TILES
    side_a = (ta_ref, ca_ref, sa_ref, (wga, wua, wda), sem_a)
    side_b = (tb_ref, cb_ref, sb_ref, (wgb, wub, wdb), sem_b)

    def copies(side, j):
        t_ref, _, s_ref, bufs, sem = side
        e, s = t_ref[j], s_ref[j]
        return [pltpu.make_async_copy(w.at[e], buf.at[s], sem.at[s])
                for w, buf in zip((wg_hbm, wu_hbm, wd_hbm), bufs)]

    def fetch(j):
        @pl.when(j < n_used)
        def _():
            for side in (side_a, side_b):
                @pl.when(side[1][j] == 1)
                def _():
                    for cp in copies(side, j):
                        cp.start()

    @pl.when(i == 0)
    def _():
        for j in range(W_AHEAD):
            fetch(j)

    for t in range(MOE_TILES):
        fetch(first + W_AHEAD + t)

    @pl.when(first < n_used)
    def _():
        for t in range(MOE_TILES):
            for side in (side_a, side_b):
                @pl.when(side[1][first + t] == 1)
                def _():
                    for cp in copies(side, first + t):
                        cp.wait()
        rows = [slice(t * TM_MOE, (t + 1) * TM_MOE) for t in range(MOE_TILES)]
        x = [x_ref[r, 0:D_MODEL] for r in rows]
        info = [x_ref[r, D_MODEL:X_ROW] for r in rows]
        xb = [v.astype(bf16) for v in x]
        units = [(t, bufs, s_ref[first + t]) for t in range(MOE_TILES)
                 for bufs, s_ref in (((wga, wua, wda), sa_ref), ((wgb, wub, wdb), sb_ref))]
        hg = [jnp.dot(xb[t], bufs[0][s], preferred_element_type=f32) for t, bufs, s in units]
        hu = [jnp.dot(xb[t], bufs[1][s], preferred_element_type=f32) for t, bufs, s in units]
        mid = [(g * jax.nn.sigmoid(g) * u).astype(bf16) for g, u in zip(hg, hu)]
        ys = [jnp.dot(m, bufs[2][s], preferred_element_type=f32)
              for m, (t, bufs, s) in zip(mid, units)]
        for t in range(MOE_TILES):
            ffn = info[t][:, 2:3] * ys[2 * t] + info[t][:, 3:4] * ys[2 * t + 1]
            y_ref[rows[t], :] = _layer_norm(ALPHA * x[t] + ffn, g2_ref[...], b2_ref[...])

    @pl.when(first >= n_used)
    def _():
        y_ref[...] = jnp.zeros_like(y_ref)


def _experts(tile_ea, tile_eb, n_used, xs, wg, wu, wd, g2, b2):
    nt = xs.shape[0] // TM_MOE

    def ring_plan(tile_e):
        opens = jnp.concatenate([jnp.ones((1,), jnp.int32), (tile_e[1:] != tile_e[:-1]).astype(jnp.int32)])
        return opens, (jnp.cumsum(opens) - 1) % W_SLOTS

    open_a, slot_a = ring_plan(tile_ea)
    open_b, slot_b = ring_plan(tile_eb)
    assert nt % MOE_TILES == 0
    n_used = ((n_used + MOE_TILES - 1) // MOE_TILES) * MOE_TILES
    const = lambda i, *_: (0, 0)
    xmap = lambda i, ta, tb, ca, cb, sa, sb, nu: (jnp.minimum(i, nu[0] // MOE_TILES - 1), 0)
    w_in = pltpu.VMEM((W_SLOTS, D_MODEL, D_EXPERT), bf16)
    w_out = pltpu.VMEM((W_SLOTS, D_EXPERT, D_MODEL), bf16)
    return pl.pallas_call(
        _expert_kernel,
        grid_spec=pltpu.PrefetchScalarGridSpec(
            num_scalar_prefetch=7,
            grid=(nt // MOE_TILES,),
            in_specs=[
                pl.BlockSpec((MOE_TILES * TM_MOE, X_ROW), xmap),
                pl.BlockSpec(memory_space=pl.ANY), pl.BlockSpec(memory_space=pl.ANY),
                pl.BlockSpec(memory_space=pl.ANY),
                pl.BlockSpec((1, D_MODEL), const), pl.BlockSpec((1, D_MODEL), const),
            ],
            out_specs=pl.BlockSpec((MOE_TILES * TM_MOE, D_MODEL), lambda i, *_: (i, 0)),
            scratch_shapes=[w_in, w_in, w_out, w_in, w_in, w_out,
                            pltpu.SemaphoreType.DMA((W_SLOTS,)), pltpu.SemaphoreType.DMA((W_SLOTS,))],
        ),
        out_shape=jax.ShapeDtypeStruct((nt * TM_MOE, D_MODEL), f32),
        compiler_params=_cparams(1),
        name="experts",
    )(tile_ea, tile_eb, open_a, open_b, slot_a.astype(jnp.int32), slot_b.astype(jnp.int32), n_used,
      xs, wg, wu, wd, g2, b2)


def _collect_kernel(pos_ref, posn_ref, ys_ref, op_ref, os_ref, ybuf, sem, *, n_p):
    i = pl.program_id(0)
    n = pl.num_programs(0)
    slot = i % 2

    def gather(p_ref, s):
        def start(k, j, r):
            pltpu.make_async_copy(ys_ref.at[pl.ds(p_ref[0, 0, r], 1)], ybuf.at[s, k, pl.ds(j, 1)],
                                  sem.at[s]).start()

        _for_each_tile_row(start)

    @pl.when(i == 0)
    def _():
        gather(pos_ref, 0)

    @pl.when(i + 1 < n)
    def _():
        gather(posn_ref, 1 - slot)

    _tile_rows_wait(ys_ref, sem.at[slot])

    @pl.when(i < n_p)
    def _():
        op_ref[...] = ybuf[slot].reshape(TM_MOVE, D_MODEL)

    @pl.when(i >= n_p)
    def _():
        os_ref[...] = ybuf[slot].reshape(TM_MOVE, D_MODEL)


def _collect(pos, ys, n_p):
    n = pos.shape[0]
    n_s = n - n_p
    return pl.pallas_call(
        functools.partial(_collect_kernel, n_p=n_p),
        grid=(n,),
        in_specs=[
            pl.BlockSpec((1, 1, TM_MOVE), lambda i: (i, 0, 0), memory_space=pltpu.SMEM),
            pl.BlockSpec((1, 1, TM_MOVE), lambda i: (jnp.minimum(i + 1, n - 1), 0, 0),
                         memory_space=pltpu.SMEM),
            pl.BlockSpec(memory_space=pl.ANY),
        ],
        out_specs=[
            pl.BlockSpec((TM_MOVE, D_MODEL), lambda i: (jnp.minimum(i, n_p - 1), 0)),
            pl.BlockSpec((TM_MOVE, D_MODEL), lambda i: (jnp.maximum(i - n_p, 0), 0)),
        ],
        out_shape=[
            jax.ShapeDtypeStruct((n_p * TM_MOVE, D_MODEL), f32),
            jax.ShapeDtypeStruct((n_s * TM_MOVE, D_MODEL), f32),
        ],
        scratch_shapes=[pltpu.VMEM((2, GRP_MOVE, SUBLANES, D_MODEL), f32), pltpu.SemaphoreType.DMA((2,))],
        compiler_params=_cparams(1),
        name="collect",
    )(pos, pos, ys)


def _block_diag(w):
    n, d, _ = w.shape
    eye = jnp.eye(n, dtype=w.dtype)
    return (eye[:, None, :, None] * w[:, :, None, :]).reshape(n * d, n * d)


def _pair_walk(n):
    total = n * (n - 1) // 2

    def extend(path, used):
        if len(path) == total:
            return path
        a, b = path[-1]
        for c in range(n):
            for nxt in ((a, c), (c, b)):
                key = frozenset(nxt)
                if len(key) == 2 and key not in used:
                    out = extend(path + [nxt], used | {key})
                    if out:
                        return out
        return None

    return extend([(0, 1)], {frozenset((0, 1))})


def _pad_history(buf):
    return jnp.pad(buf, ((0, 0), (SUBLANES - (CONV_W - 1), 0), (0, 0)))


def kernel(x_prompt, x_sample, state_rg_h, state_rg_conv, state_m_C, state_m_n, state_m_m, state_m_conv, w_in, rg_conv_w, rg_conv_b, rg_w_a, rg_b_a, rg_w_x, rg_b_x, rg_lambda, m_conv_w, m_conv_b, m_w_q, m_w_k, m_w_v, m_b_i, m_b_f, m_norm_g, m_skip, w_out, ln1_g, ln1_b, ln2_g, ln2_b, moe_w_group, moe_b_group, moe_w_expert, moe_b_expert, moe_w_gate, moe_w_up, moe_w_down):
    BP, TP, _ = x_prompt.shape
    BS, TS, _ = x_sample.shape
    n_prompt, n_sample = BP * TP, BS * TS
    xp = x_prompt.reshape(n_prompt, D_MODEL)
    xs = x_sample.reshape(n_sample, D_MODEL)
    l = 0

    w_in_p = jnp.pad(w_in[l], ((0, 0), (0, N_PROJ - w_in.shape[-1]))).astype(bf16)
    wa = _block_diag(rg_w_a[l]).astype(bf16)
    wx = _block_diag(rg_w_x[l]).astype(bf16)
    row = lambda v: v.reshape(1, -1)
    bif = jnp.pad(jnp.concatenate([m_b_i[l], m_b_f[l]]), (0, LANES - 2 * M_HEADS)).reshape(1, LANES)
    wqk = jnp.concatenate([m_w_q[l], m_w_k[l]], axis=-1).astype(bf16)
    wv = m_w_v[l].astype(bf16)
    wkT = m_w_k[l].transpose(0, 2, 1).astype(bf16)
    w_route = jnp.pad(jnp.concatenate([moe_w_group[l], moe_w_expert[l]], axis=1),
                      ((0, 0), (0, LANES - N_GROUPS - N_EXPERTS)))
    wr_hi = w_route.astype(bf16)
    wr_lo = (w_route - wr_hi.astype(f32)).astype(bf16)
    wr = jnp.concatenate([wr_hi, wr_lo], axis=1)
    b_route = jnp.pad(jnp.concatenate([moe_b_group[l], moe_b_expert[l]]),
                      (0, LANES - N_GROUPS - N_EXPERTS)).reshape(1, LANES)

    proj_p, proj_s = _inproj(xp, xs, w_in_p)

    rg_args = (rg_conv_w[l], row(rg_conv_b[l]), wa, row(rg_b_a[l]), wx, row(rg_b_x[l]), row(rg_lambda[l]))
    zeros = lambda *s: jnp.zeros(s, f32)
    s_blk = M_ROWS // TS
    yrg_p, p_rg_h, p_rg_conv = _rglru(proj_p, 0, BP, TP, 1, RG_ROWS_PROMPT, zeros(BP, SUBLANES, D_RG),
                                      zeros(BP, 1, D_RG), *rg_args)
    yrg_s, s_rg_h, s_rg_conv = _rglru(proj_s, 0, BS, TS, RG_ROWS_PROMPT // TS, TS,
                                      _pad_history(state_rg_conv[l]), state_rg_h[l].reshape(BS, 1, D_RG), *rg_args)

    m_args = (m_conv_w[l], row(m_conv_b[l]), wqk, wv, wkT, bif, row(m_norm_g[l]), row(m_skip[l]))
    m0_p = jnp.full((SUBLANES, BP * LANES), M_INIT, f32)
    ym_p, p_m_C, p_m_n, p_m_m, p_m_conv, wg_b, wu_b, wd_b = _mlstm(
        proj_p, BP, TP, 1, M_ROWS, M_BLOCKS_PROMPT, zeros(BP, SUBLANES, D_M), zeros(BP, M_HEADS, M_DH, M_DH),
        zeros(BP, M_HEADS, 1, M_DH), m0_p, *m_args, riders=(moe_w_gate[l], moe_w_up[l], moe_w_down[l]))
    m0_s = jnp.pad(jnp.repeat(state_m_m[l].T, TS, axis=1), ((0, SUBLANES - M_HEADS), (0, 0)))
    ym_s, s_m_C, s_m_n, s_m_m, s_m_conv = _mlstm(
        proj_s, BS, TS, s_blk, TS, M_BLOCKS_SAMPLE, _pad_history(state_m_conv[l]), state_m_C[l],
        state_m_n[l].reshape(BS, M_HEADS, 1, M_DH), m0_s, *m_args)

    x1, cnt = _outproj(xp, xs, yrg_p, yrg_s, ym_p, ym_s, w_out[l].astype(bf16),
                       row(ln1_g[l]), row(ln1_b[l]), wr, b_route)

    n_tok = n_prompt + n_sample
    n_tiles = n_tok // TM_MOE + N_CLASSES
    lay_cls, lay_a, lay_b = [], [], []
    for g in range(N_GROUPS):
        for ja, jb in _pair_walk(EXPERTS_PER_GROUP):
            assert ja < jb
            n_first = (EXPERTS_PER_GROUP - 1) * ja - ja * (ja - 1) // 2
            lay_cls.append(PAIRS_PER_GROUP * g + n_first + jb - ja - 1)
            lay_a.append(g * EXPERTS_PER_GROUP + ja)
            lay_b.append(g * EXPERTS_PER_GROUP + jb)
    classes = jnp.arange(N_CLASSES, dtype=jnp.int32)
    in_slot = jnp.array(lay_cls, jnp.int32)[:, None] == classes
    counts = jnp.sum(jnp.where(in_slot, cnt[0, :N_CLASSES].astype(jnp.int32), 0), axis=1)
    padded = ((counts + TM_MOE - 1) // TM_MOE) * TM_MOE
    ends = jnp.cumsum(padded)
    offs = ends - padded
    offs_of_cls = jnp.sum(jnp.where(in_slot, offs[:, None], 0), axis=0)
    cls = x1[:, D_MODEL].astype(jnp.int32)
    rank = x1[:, D_MODEL + 1].astype(jnp.int32)
    pos = jnp.sum(jnp.where(cls[:, None] == classes, offs_of_cls, 0), axis=-1) + rank
    pos = pos.reshape(n_tok // TM_MOVE, 1, TM_MOVE)
    tiles = jnp.arange(n_tiles, dtype=jnp.int32)
    tile_slot = jnp.minimum(jnp.sum(tiles[:, None] >= (ends // TM_MOE)[None, :], axis=1), N_CLASSES - 1)
    on_slot = tile_slot[:, None] == classes
    tile_ea = jnp.sum(jnp.where(on_slot, jnp.array(lay_a, jnp.int32), 0), axis=-1).astype(jnp.int32)
    tile_eb = jnp.sum(jnp.where(on_slot, jnp.array(lay_b, jnp.int32), 0), axis=-1).astype(jnp.int32)
    n_used = (ends[-1] // TM_MOE).reshape(1).astype(jnp.int32)
    pad_row0 = ((offs + counts) // SUBLANES) * SUBLANES
    pad_plan = jnp.concatenate([jnp.stack([pad_row0, (ends - pad_row0) // SUBLANES], axis=1).reshape(-1),
                                     n_used]).astype(jnp.int32)

    x_sorted = _dispatch(pad_plan, pos, x1, n_tiles * TM_MOE)
    y_sorted = _experts(tile_ea, tile_eb, n_used, x_sorted, wg_b, wu_b, wd_b,
                        row(ln2_g[l]), row(ln2_b[l]))
    y_p, y_s = _collect(pos, y_sorted, n_prompt // TM_MOVE)

    return (y_p.reshape(BP, TP, D_MODEL), y_s.reshape(BS, TS, D_MODEL),
            p_rg_h.reshape(1, BP, D_RG), p_rg_conv[None], p_m_C[None], p_m_n.reshape(1, BP, M_HEADS, M_DH),
            p_m_m[None], p_m_conv[None],
            s_rg_h.reshape(1, BS, D_RG), s_rg_conv[None], s_m_C[None], s_m_n.reshape(1, BS, M_HEADS, M_DH),
            s_m_m[None], s_m_conv[None])
```

```python
import functools

import jax
import jax.numpy as jnp
from jax import lax
from jax.experimental import pallas as pl
from jax.experimental.pallas import tpu as pltpu

f32 = jnp.float32
bf16 = jnp.bfloat16

D_MODEL = 1024
D_RG = 512
RG_C = 8.0
D_M = 512
M_HEADS = 4
M_DH = 128
CONV_W = 4
N_GROUPS = 4
EXPERTS_PER_GROUP = 8
N_EXPERTS = 32
D_EXPERT = 256
ALPHA = 2.0 ** 0.25
LN_EPS = 1e-5
M_INIT = -1.0e4

LANES = 128
SUBLANES = 8
TM = 512
TM_MOVE = 1024
OUT_PARTS = 4
TM_MOE = 128
MOE_TILES = 4
W_AHEAD = 4
W_SLOTS = W_AHEAD + MOE_TILES
PAIRS_PER_GROUP = EXPERTS_PER_GROUP * (EXPERTS_PER_GROUP - 1) // 2
N_CLASSES = N_GROUPS * PAIRS_PER_GROUP
X_ROW = D_MODEL + LANES
N_PROJ = 2 * D_RG + 2 * D_M + LANES
RG_ROWS_PROMPT = 512
M_ROWS = 128
M_BLOCKS_PROMPT = 8
M_BLOCKS_SAMPLE = 2
V7X_VMEM_BYTES = 64 * 1024 * 1024
VMEM_LIMIT = V7X_VMEM_BYTES - 8 * 1024 * 1024


def _cparams(n_axes):
    return pltpu.CompilerParams(dimension_semantics=("arbitrary",) * n_axes,
                                vmem_limit_bytes=VMEM_LIMIT)


def _inproj_kernel(xp_ref, xs_ref, w_ref, op_ref, os_ref, *, n_p):
    i = pl.program_id(0)

    def run(x_ref, o_ref):
        o_ref[...] = jnp.dot(x_ref[...].astype(bf16), w_ref[...], preferred_element_type=f32)

    @pl.when(i < n_p)
    def _():
        run(xp_ref, op_ref)

    @pl.when(i >= n_p)
    def _():
        run(xs_ref, os_ref)


def _inproj(xp, xs, w):
    n_p, n_s = xp.shape[0] // TM, xs.shape[0] // TM
    pmap = lambda i: (jnp.minimum(i, n_p - 1), 0)
    smap = lambda i: (jnp.maximum(i - n_p, 0), 0)
    return pl.pallas_call(
        functools.partial(_inproj_kernel, n_p=n_p),
        grid=(n_p + n_s,),
        in_specs=[
            pl.BlockSpec((TM, D_MODEL), pmap),
            pl.BlockSpec((TM, D_MODEL), smap),
            pl.BlockSpec((D_MODEL, N_PROJ), lambda i: (0, 0)),
        ],
        out_specs=[pl.BlockSpec((TM, N_PROJ), pmap), pl.BlockSpec((TM, N_PROJ), smap)],
        out_shape=[jax.ShapeDtypeStruct((n_p * TM, N_PROJ), f32),
                   jax.ShapeDtypeStruct((n_s * TM, N_PROJ), f32)],
        compiler_params=_cparams(1),
        name="inproj",
    )(xp, xs, w)


def _causal_conv(tail_scr, x, cw_ref, cb_ref, S, L, C):
    tail = tail_scr[...]
    sub = lax.broadcasted_iota(jnp.int32, (S, SUBLANES, C), 1)
    acc = cb_ref[...] + cw_ref[CONV_W - 1:CONV_W, :] * x
    for d in range(1, CONV_W):
        back = pltpu.roll(x, d, 0).reshape(S, L, C)
        head = jnp.where(sub < d, pltpu.roll(tail, d, 1), back[:, 0:SUBLANES, :])
        if L > SUBLANES:
            back = jnp.concatenate([head, back[:, SUBLANES:, :]], axis=1)
        else:
            back = head
        acc = acc + cw_ref[CONV_W - 1 - d:CONV_W - d, :] * back.reshape(S * L, C)
    tail_scr[...] = x.reshape(S, L, C)[:, L - SUBLANES:, :]
    return acc


def _softplus(x):
    return jnp.maximum(x, 0.0) + jnp.log1p(jnp.exp(-jnp.abs(x)))


def _rglru_kernel(x_ref, g_ref, buf_ref, h0_ref, cw_ref, cb_ref, wa_ref, ba_ref, wx_ref, bx_ref,
                  lam_ref, y_ref, hN_ref, cN_ref, xs_scr, *, S, L):
    R = S * L
    t = pl.program_id(1)

    @pl.when(t == 0)
    def _():
        xs_scr[...] = buf_ref[...]
        hN_ref[...] = h0_ref[...]

    x = x_ref[...]
    xc = _causal_conv(xs_scr, x, cw_ref, cb_ref, S, L, D_RG)
    xcb = xc.astype(bf16)
    r = jax.nn.sigmoid(jnp.dot(xcb, wa_ref[...], preferred_element_type=f32) + ba_ref[...])
    ig = jax.nn.sigmoid(jnp.dot(xcb, wx_ref[...], preferred_element_type=f32) + bx_ref[...])
    log_a = (-RG_C) * r * _softplus(-lam_ref[...])
    a = jnp.exp(log_a)
    th = jnp.tanh(log_a)
    u = jnp.sqrt(-2.0 * th / (1.0 - th)) * ig * xc

    n_grp, grp_per_seq = R // SUBLANES, L // SUBLANES
    a3 = a.reshape(n_grp, SUBLANES, D_RG)
    u3 = u.reshape(n_grp, SUBLANES, D_RG)
    sub = lax.broadcasted_iota(jnp.int32, (n_grp, SUBLANES, D_RG), 1)
    s = 1
    while s < SUBLANES:
        ok = sub >= s
        a_sh = pltpu.roll(a3, s, 1)
        u_sh = pltpu.roll(u3, s, 1)
        u3 = jnp.where(ok, a3 * u_sh + u3, u3)
        a3 = jnp.where(ok, a3 * a_sh, a3)
        s *= 2
    h0 = hN_ref[...]
    groups = []
    for kg in range(n_grp):
        carry = h0[kg // grp_per_seq] if kg % grp_per_seq == 0 else groups[-1][SUBLANES - 1:SUBLANES, :]
        groups.append(a3[kg] * carry + u3[kg])
    h = jnp.concatenate(groups, axis=0)

    y_ref[...] = h * jax.nn.gelu(g_ref[...], approximate=True)
    hN_ref[...] = h.reshape(S, L, D_RG)[:, L - 1:L, :]
    cN_ref[...] = xs_scr[:, SUBLANES - (CONV_W - 1):SUBLANES, :]


def _rglru(proj, row0, B, T, S, L, buf8, h0, cw, cb, wa, ba, wx, bx, lam):
    R = S * L
    nb, nt = B // S, T // L
    blk0 = row0 // R
    row_map = lambda b, t: (blk0 + b * nt + t, 0)
    const2 = lambda b, t: (0, 0)
    return pl.pallas_call(
        functools.partial(_rglru_kernel, S=S, L=L),
        grid=(nb, nt),
        in_specs=[
            pl.BlockSpec((R, D_RG), row_map),
            pl.BlockSpec((R, D_RG), lambda b, t: (blk0 + b * nt + t, 1)),
            pl.BlockSpec((S, SUBLANES, D_RG), lambda b, t: (b, 0, 0)),
            pl.BlockSpec((S, 1, D_RG), lambda b, t: (b, 0, 0)),
            pl.BlockSpec((CONV_W, D_RG), const2),
            pl.BlockSpec((1, D_RG), const2),
            pl.BlockSpec((D_RG, D_RG), const2),
            pl.BlockSpec((1, D_RG), const2),
            pl.BlockSpec((D_RG, D_RG), const2),
            pl.BlockSpec((1, D_RG), const2),
            pl.BlockSpec((1, D_RG), const2),
        ],
        out_specs=[
            pl.BlockSpec((R, D_RG), lambda b, t: (b * nt + t, 0)),
            pl.BlockSpec((S, 1, D_RG), lambda b, t: (b, 0, 0)),
            pl.BlockSpec((S, CONV_W - 1, D_RG), lambda b, t: (b, 0, 0)),
        ],
        out_shape=[
            jax.ShapeDtypeStruct((B * T, D_RG), f32),
            jax.ShapeDtypeStruct((B, 1, D_RG), f32),
            jax.ShapeDtypeStruct((B, CONV_W - 1, D_RG), f32),
        ],
        scratch_shapes=[pltpu.VMEM((S, SUBLANES, D_RG), f32)],
        compiler_params=_cparams(2),
        name="rglru",
    )(proj, proj, buf8, h0, cw, cb, wa, ba, wx, bx, lam)


def _seg_scan(x, op, fill, tin, L, reverse=False):
    s = 1
    while s < L:
        if reverse:
            sh = pltpu.roll(x, LANES - s, 1)
            ok = tin < L - s
        else:
            sh = pltpu.roll(x, s, 1)
            ok = tin >= s
        x = op(x, jnp.where(ok, sh, fill))
        s *= 2
    return x


def _mlstm_kernel(*refs, S, L, G, n_riders):
    (x_ref, z_ref, gt_ref, buf_ref, c0_ref, n0_ref, m0_ref,
     cw_ref, cb_ref, wqk_ref, wv_ref, wkT_ref, bif_ref, ng_ref, sk_ref) = refs[:15]
    rider_in = refs[15:15 + n_riders]
    y_ref, cN_ref, nN_ref, mN_ref, bN_ref = refs[15 + n_riders:20 + n_riders]
    rider_out = refs[20 + n_riders:20 + 2 * n_riders]
    xs_scr, m_scr = refs[20 + 2 * n_riders:]
    for src, dst in zip(rider_in, rider_out):
        dst[...] = src[...].astype(bf16)

    R = S * L
    GS, GR = G * S, G * R
    j = pl.program_id(1)

    @pl.when(j == 0)
    def _():
        xs_scr[...] = buf_ref[...]
        cN_ref[...] = c0_ref[...]
        nN_ref[...] = n0_ref[...]
        m_scr[...] = m0_ref[...]

    x = x_ref[...].reshape(GR, D_M)
    xc = _causal_conv(xs_scr, x, cw_ref, cb_ref, GS, L, D_M)
    xa = xc * jax.nn.sigmoid(xc)
    xab = xa.astype(bf16)
    xb = x.astype(bf16)

    il_parts, f_parts = [], []
    for g in range(G):
        gT = (gt_ref[g] + bif_ref[...]).T
        il_parts.append(gT[0:SUBLANES, :])
        f_parts.append(pltpu.roll(gT[0:SUBLANES, :], M_HEADS, 0))
    il = jnp.concatenate(il_parts, axis=0)
    fl = -_softplus(-jnp.concatenate(f_parts, axis=0))
    tin = lax.broadcasted_iota(jnp.int32, (G * SUBLANES, LANES), 1) % L
    bcum = _seg_scan(fl, jnp.add, 0.0, tin, L)
    a = il - bcum
    m_prev = m_scr[...].reshape(G * SUBLANES, LANES)
    big_m = jnp.maximum(m_prev, _seg_scan(a, jnp.maximum, -jnp.inf, tin, L))
    m_t = bcum + big_m
    if S == 1:
        m_last = jnp.broadcast_to(big_m[:, LANES - 1:LANES], big_m.shape)
    else:
        m_last = _seg_scan(big_m, jnp.maximum, -jnp.inf, tin, L, reverse=True)
    rows = [big_m, jnp.exp(m_prev - big_m), jnp.exp(-m_t), jnp.exp(a - m_last), jnp.exp(m_prev - m_last)]
    mN_ref[...] = m_t.reshape(G, SUBLANES, LANES)
    m_scr[...] = jnp.broadcast_to(m_t[:, LANES - 1:LANES], m_t.shape).reshape(G, SUBLANES, LANES)
    pad_rows = jnp.zeros((LANES - len(rows) * SUBLANES, LANES), f32)
    cols = []
    for g in range(G):
        gs = slice(g * SUBLANES, (g + 1) * SUBLANES)
        cols.append(jnp.concatenate([r[gs] for r in rows] + [pad_rows], axis=0).T)

    def col(g, q, h):
        return cols[g][:, SUBLANES * q + h:SUBLANES * q + h + 1]

    qs, ks, vs, kTs = [], [], [], []
    for h in range(M_HEADS):
        hs = slice(h * M_DH, (h + 1) * M_DH)
        qk_h = jnp.dot(xab[:, hs], wqk_ref[h], preferred_element_type=f32)
        qs.append(qk_h[:, 0:M_DH])
        ks.append(qk_h[:, M_DH:2 * M_DH] * (M_DH ** -0.5))
        vs.append(jnp.dot(xb[:, hs], wv_ref[h], preferred_element_type=f32))
        if S == 1:
            kT_h = lax.dot_general(wkT_ref[h], xab[:, hs], (((1,), (1,)), ((), ())),
                                   preferred_element_type=f32)
            kTs.append((kT_h * (M_DH ** -0.5)).astype(bf16))

    ti = lax.broadcasted_iota(jnp.int32, (R, R), 0)
    si = lax.broadcasted_iota(jnp.int32, (R, R), 1)
    mask = (si <= ti) & ((ti // L) == (si // L))
    ones_b = jnp.ones((R, M_DH), bf16)
    pairs = [(g, h) for g in range(G) for h in range(M_HEADS)]
    blk = lambda g: slice(g * R, (g + 1) * R)
    seqs = [slice(b * L, (b + 1) * L) for b in range(S)]
    q = {(g, h): qs[h][blk(g)] for g, h in pairs}
    k = {(g, h): ks[h][blk(g)] for g, h in pairs}
    v = {(g, h): vs[h][blk(g)] for g, h in pairs}
    qb = {p: q[p].astype(bf16) for p in pairs}
    kb = {p: k[p].astype(bf16) for p in pairs}
    qk = {p: lax.dot_general(qb[p], kb[p], (((1,), (1,)), ((), ())), preferred_element_type=f32)
          for p in pairs}
    sm = {}
    for g, h in pairs:
        a_row = a[g * SUBLANES + h:g * SUBLANES + h + 1, :]
        decay = jnp.exp(jnp.where(mask, a_row - col(g, 0, h), -jnp.inf))
        sm[g, h] = (qk[g, h] * decay).astype(bf16)
    nd = {p: jnp.dot(sm[p], jnp.concatenate([v[p].astype(bf16), ones_b], axis=1),
                     preferred_element_type=f32) for p in pairs}
    c_old = {(g, h, b): cN_ref[g * S + b, h] for g, h in pairs for b in range(S)}
    n_old = {(g, h, b): nN_ref[g * S + b, h] for g, h in pairs for b in range(S)}
    q_c = {(g, h, b): jnp.dot(qb[g, h][seqs[b]], c_old[g, h, b].astype(bf16), preferred_element_type=f32)
           for g, h in pairs for b in range(S)}
    hh = {}
    for g, h in pairs:
        g_col, e_col = col(g, 1, h), col(g, 2, h)
        pieces = []
        for b, rs in enumerate(seqs):
            q_n = jnp.sum(q[g, h][rs] * n_old[g, h, b], axis=1, keepdims=True)
            num = nd[g, h][rs, 0:M_DH] + g_col[rs] * q_c[g, h, b]
            den = nd[g, h][rs, M_DH:2 * M_DH] + g_col[rs] * q_n
            pieces.append(num / jnp.maximum(jnp.abs(den), e_col[rs]))
        hh[g, h] = pieces[0] if S == 1 else jnp.concatenate(pieces, axis=0)
    mu = {p: jnp.mean(hh[p], axis=1, keepdims=True) for p in pairs}
    var = {p: jnp.mean(jnp.square(hh[p] - mu[p]), axis=1, keepdims=True) for p in pairs}
    hn_blocks = [jnp.concatenate([(hh[g, h] - mu[g, h]) * lax.rsqrt(var[g, h] + LN_EPS)
                                  for h in range(M_HEADS)], axis=1) for g in range(G)]
    new_c, new_n = [], []
    for g, h in pairs:
        w_col = col(g, 3, h)
        wv = (w_col * v[g, h]).astype(bf16)
        wk = w_col * k[g, h]
        for b, rs in enumerate(seqs):
            g_end = col(g, 4, h)[(b + 1) * L - 1:(b + 1) * L, :]
            if S == 1:
                kv = jnp.dot(kTs[h][:, blk(g)], wv, preferred_element_type=f32)
            else:
                kv = lax.dot_general(kb[g, h][rs], wv[rs], (((0,), (0,)), ((), ())),
                                     preferred_element_type=f32)
            new_c.append((g * S + b, h, g_end * c_old[g, h, b] + kv))
            new_n.append((g * S + b, h, g_end * n_old[g, h, b] + jnp.sum(wk[rs], axis=0, keepdims=True)))

    hn = jnp.concatenate(hn_blocks, axis=0) * ng_ref[...]
    y = jax.nn.sigmoid(z_ref[...].reshape(GR, D_M)) * (hn + sk_ref[...] * xa)
    y_ref[...] = y.reshape(G, R, D_M)
    for sq, h, val in new_c:
        cN_ref[sq, h] = val
    for sq, h, val in new_n:
        nN_ref[sq, h] = val
    bN_ref[...] = xs_scr[:, SUBLANES - (CONV_W - 1):SUBLANES, :]


def _mlstm(proj, B, T, S, L, G, buf8, c0, n0, m0_lanes, cw, cb, wqk, wv, wkT, bif, ng, sk, riders=()):
    R = S * L
    assert R == M_ROWS == LANES
    nblk, nc = B // S, T // L
    nb = nblk // G
    assert nblk * S == B and nc * L == T and nb * G == nblk and (S == 1 or nc == 1)
    steps = nb * nc
    assert all(r.shape[0] % steps == 0 for r in riders)
    rider_specs = [pl.BlockSpec((r.shape[0] // steps,) + r.shape[1:], lambda b, j: (b * nc + j, 0, 0))
                   for r in riders]
    proj4 = proj.reshape(nblk, nc, R, N_PROJ)
    m0_3 = m0_lanes.reshape(SUBLANES, nblk, LANES).transpose(1, 0, 2)
    const2 = lambda b, j: (0, 0)
    const3 = lambda b, j: (0, 0, 0)
    gate_blk = (2 * D_RG + 2 * D_M) // LANES
    GS = G * S
    outs = pl.pallas_call(
        functools.partial(_mlstm_kernel, S=S, L=L, G=G, n_riders=len(riders)),
        grid=(nb, nc),
        in_specs=[
            pl.BlockSpec((G, None, R, D_M), lambda b, j: (b, j, 0, 2)),
            pl.BlockSpec((G, None, R, D_M), lambda b, j: (b, j, 0, 3)),
            pl.BlockSpec((G, None, R, LANES), lambda b, j: (b, j, 0, gate_blk)),
            pl.BlockSpec((GS, SUBLANES, D_M), lambda b, j: (b, 0, 0)),
            pl.BlockSpec((GS, M_HEADS, M_DH, M_DH), lambda b, j: (b, 0, 0, 0)),
            pl.BlockSpec((GS, M_HEADS, 1, M_DH), lambda b, j: (b, 0, 0, 0)),
            pl.BlockSpec((G, SUBLANES, LANES), lambda b, j: (b, 0, 0)),
            pl.BlockSpec((CONV_W, D_M), const2),
            pl.BlockSpec((1, D_M), const2),
            pl.BlockSpec((M_HEADS, M_DH, 2 * M_DH), const3),
            pl.BlockSpec((M_HEADS, M_DH, M_DH), const3),
            pl.BlockSpec((M_HEADS, M_DH, M_DH), const3),
            pl.BlockSpec((1, LANES), const2),
            pl.BlockSpec((1, D_M), const2),
            pl.BlockSpec((1, D_M), const2),
        ] + rider_specs,
        out_specs=[
            pl.BlockSpec((G, None, R, D_M), lambda b, j: (b, j, 0, 0)),
            pl.BlockSpec((GS, M_HEADS, M_DH, M_DH), lambda b, j: (b, 0, 0, 0)),
            pl.BlockSpec((GS, M_HEADS, 1, M_DH), lambda b, j: (b, 0, 0, 0)),
            pl.BlockSpec((G, None, SUBLANES, LANES), lambda b, j: (b, j, 0, 0)),
            pl.BlockSpec((GS, CONV_W - 1, D_M), lambda b, j: (b, 0, 0)),
        ] + rider_specs,
        out_shape=[
            jax.ShapeDtypeStruct((nblk, nc, R, D_M), f32),
            jax.ShapeDtypeStruct((B, M_HEADS, M_DH, M_DH), f32),
            jax.ShapeDtypeStruct((B, M_HEADS, 1, M_DH), f32),
            jax.ShapeDtypeStruct((nblk, nc, SUBLANES, LANES), f32),
            jax.ShapeDtypeStruct((B, CONV_W - 1, D_M), f32),
        ] + [jax.ShapeDtypeStruct(r.shape, bf16) for r in riders],
        scratch_shapes=[pltpu.VMEM((GS, SUBLANES, D_M), f32), pltpu.VMEM((G, SUBLANES, LANES), f32)],
        compiler_params=_cparams(2),
        name="mlstm",
    )(proj4, proj4, proj4, buf8, c0, n0, m0_3, cw, cb, wqk, wv, wkT, bif, ng, sk, *riders)
    y, c_new, n_new, m_t, b_new = outs[:5]
    m_last = m_t[:, nc - 1, :M_HEADS, :].reshape(nblk, M_HEADS, S, L)[:, :, :, L - 1]
    m_last = m_last.transpose(0, 2, 1).reshape(B, M_HEADS)
    return (y.reshape(B * T, D_M), c_new, n_new, m_last, b_new) + tuple(outs[5:])


def _layer_norm(x, g, b):
    mu = jnp.mean(x, axis=-1, keepdims=True)
    var = jnp.mean(jnp.square(x - mu), axis=-1, keepdims=True)
    return (x - mu) * lax.rsqrt(var + LN_EPS) * g + b


def _first_lane_of_max(vals, lane_f):
    vmax = jnp.max(vals, axis=1, keepdims=True)
    idx = jnp.min(jnp.where(vals == vmax, lane_f, float(LANES)), axis=1, keepdims=True)
    return vmax, idx


def _outproj_kernel(xp_ref, xs_ref, rgp_ref, rgs_ref, mp_ref, ms_ref, wo_ref, g1_ref, b1_ref,
                    wr_ref, br_ref, x1_ref, cnt_ref, *, n_p):
    i = pl.program_id(0)

    @pl.when(i == 0)
    def _():
        cnt_ref[...] = jnp.zeros_like(cnt_ref)

    def route(lg, lane, lane_f):
        neg = -jnp.inf
        gl = jnp.where(lane < N_GROUPS, lg, neg)
        gmax, gidx = _first_lane_of_max(gl, lane_f)
        p_g = 1.0 / jnp.sum(jnp.exp(gl - gmax), axis=1, keepdims=True)
        e_lo = float(N_GROUPS) + float(EXPERTS_PER_GROUP) * gidx
        el = jnp.where((lane_f >= e_lo) & (lane_f < e_lo + float(EXPERTS_PER_GROUP)), lg, neg)
        v1, i1 = _first_lane_of_max(el, lane_f)
        v2, i2 = _first_lane_of_max(jnp.where(lane_f == i1, neg, el), lane_f)
        d = jnp.exp(v2 - v1)
        w1 = p_g / (1.0 + d)
        w2 = p_g * d / (1.0 + d)
        first_low = i1 < i2
        j_lo = jnp.minimum(i1, i2) - e_lo
        j_hi = jnp.maximum(i1, i2) - e_lo
        n_first = float(EXPERTS_PER_GROUP - 1) * j_lo - 0.5 * j_lo * (j_lo - 1.0)
        cls = float(PAIRS_PER_GROUP) * gidx + n_first + (j_hi - j_lo - 1.0)
        return cls, jnp.where(first_low, w1, w2), jnp.where(first_low, w2, w1)

    def run(x_ref, rg_ref, m_ref):
        rp = TM // OUT_PARTS
        parts = [slice(p * rp, (p + 1) * rp) for p in range(OUT_PARTS)]
        lane = lax.broadcasted_iota(jnp.int32, (rp, LANES), 1)
        lane_f = lane.astype(f32)
        heads = [jnp.concatenate([rg_ref[r, :].astype(bf16), m_ref[r, :].astype(bf16)], axis=1) for r in parts]
        mix = [jnp.dot(h, wo_ref[...], preferred_element_type=f32) for h in heads]
        x1 = [_layer_norm(ALPHA * x_ref[r, :] + mx, g1_ref[...], b1_ref[...]) for r, mx in zip(parts, mix)]
        for r, v in zip(parts, x1):
            x1_ref[r, 0:D_MODEL] = v

        hi = [v.astype(bf16) for v in x1]
        lo = [(v - h.astype(f32)).astype(bf16) for v, h in zip(x1, hi)]
        hi_terms = [jnp.dot(h, wr_ref[...], preferred_element_type=f32) for h in hi]
        lo_term = [jnp.dot(v, wr_ref[:, 0:LANES], preferred_element_type=f32) for v in lo]
        lg = [ht[:, 0:LANES] + ht[:, LANES:2 * LANES] + lt + br_ref[...] for ht, lt in zip(hi_terms, lo_term)]
        routed = [route(v, lane, lane_f) for v in lg]

        ti = lax.broadcasted_iota(jnp.int32, (rp, rp), 0)
        si = lax.broadcasted_iota(jnp.int32, (rp, rp), 1)
        tri = (si <= ti).astype(bf16)
        hot = [lane_f == cls for cls, _, _ in routed]
        cum = [jnp.dot(tri, h.astype(bf16), preferred_element_type=f32) for h in hot]
        seen = cnt_ref[...]
        for r, h, cm, (cls, w_lo, w_hi) in zip(parts, hot, cum, routed):
            rank = jnp.sum(jnp.where(h, cm - 1.0 + seen, 0.0), axis=1, keepdims=True)
            seen = seen + cm[rp - 1:rp, :]
            info = jnp.zeros((rp, LANES), f32)
            for c, val in enumerate((cls, rank, w_lo, w_hi)):
                info = jnp.where(lane == c, val, info)
            x1_ref[r, D_MODEL:D_MODEL + LANES] = info
        cnt_ref[...] = seen

    @pl.when(i < n_p)
    def _():
        run(xp_ref, rgp_ref, mp_ref)

    @pl.when(i >= n_p)
    def _():
        run(xs_ref, rgs_ref, ms_ref)


def _outproj(xp, xs, rgp, rgs, mp, ms, wo, g1, b1, wr, br):
    n_p, n_s = xp.shape[0] // TM, xs.shape[0] // TM
    n = n_p + n_s
    pmap = lambda i: (jnp.minimum(i, n_p - 1), 0)
    smap = lambda i: (jnp.maximum(i - n_p, 0), 0)
    const = lambda i: (0, 0)
    return pl.pallas_call(
        functools.partial(_outproj_kernel, n_p=n_p),
        grid=(n,),
        in_specs=[
            pl.BlockSpec((TM, D_MODEL), pmap), pl.BlockSpec((TM, D_MODEL), smap),
            pl.BlockSpec((TM, D_RG), pmap), pl.BlockSpec((TM, D_RG), smap),
            pl.BlockSpec((TM, D_M), pmap), pl.BlockSpec((TM, D_M), smap),
            pl.BlockSpec((D_RG + D_M, D_MODEL), const),
            pl.BlockSpec((1, D_MODEL), const), pl.BlockSpec((1, D_MODEL), const),
            pl.BlockSpec((D_MODEL, 2 * LANES), const),
            pl.BlockSpec((1, LANES), const),
        ],
        out_specs=[
            pl.BlockSpec((TM, X_ROW), lambda i: (i, 0)),
            pl.BlockSpec((1, LANES), const),
        ],
        out_shape=[
            jax.ShapeDtypeStruct((n * TM, X_ROW), f32),
            jax.ShapeDtypeStruct((1, LANES), f32),
        ],
        compiler_params=_cparams(1),
        name="outproj",
    )(xp, xs, rgp, rgs, mp, ms, wo, g1, b1, wr, br)


GRP_MOVE = TM_MOVE // SUBLANES


def _grouped(x):
    return x.reshape(x.shape[0] // SUBLANES, SUBLANES, x.shape[1])


def _tile_rows_wait(hbm_ref, sem):
    rows = hbm_ref.at[pl.ds(0, TM_MOVE)]
    pltpu.make_async_copy(rows, rows, sem).wait()


def _for_each_tile_row(body):
    def group(k, c):
        for j in range(SUBLANES):
            body(k, j, k * SUBLANES + j)
        return c

    lax.fori_loop(0, GRP_MOVE, group, 0)


def _dispatch_kernel(pad_ref, pos_ref, x1_ref, xs_ref, zero_scr, xbuf, sem, zsem, lsem):
    i = pl.program_id(0)
    n = pl.num_programs(0)

    def zero_tile(t):
        return pltpu.make_async_copy(zero_scr, xs_ref.at[pl.ds(pl.multiple_of(t * TM_MOE, TM_MOE), TM_MOE)], zsem)

    def zero_chunk(row0):
        return pltpu.make_async_copy(zero_scr.at[pl.ds(0, SUBLANES)],
                                     xs_ref.at[pl.ds(pl.multiple_of(row0, SUBLANES), SUBLANES)], zsem)

    @pl.when(i == 0)
    def _():
        zero_scr[...] = jnp.zeros_like(zero_scr)
        n_used = pad_ref[2 * N_CLASSES]
        n_tiles = xs_ref.shape[0] // TM_MOE
        for go in (lambda cp: cp.start(), lambda cp: cp.wait()):
            lax.fori_loop(n_used, n_tiles, lambda t, c: (go(zero_tile(t)), c)[1], 0)

            def per_class(cl, c):
                row0 = pad_ref[2 * cl]
                lax.fori_loop(0, pad_ref[2 * cl + 1],
                              lambda k, c2: (go(zero_chunk(row0 + k * SUBLANES)), c2)[1], 0)
                return c

            lax.fori_loop(0, N_CLASSES, per_class, 0)

    def load(t, s):
        return pltpu.make_async_copy(x1_ref.at[pl.ds(t * GRP_MOVE, GRP_MOVE)], xbuf.at[s], lsem.at[s])

    @pl.when(i == 0)
    def _():
        load(0, 0).start()

    @pl.when(i + 1 < n)
    def _():
        load(i + 1, (i + 1) % 3).start()

    slot = i % 3
    load(i, slot).wait()

    def start(k, j, r):
        pltpu.make_async_copy(xbuf.at[slot, k, pl.ds(j, 1)], xs_ref.at[pl.ds(pos_ref[0, 0, r], 1)],
                              sem.at[slot]).start()

    _for_each_tile_row(start)

    @pl.when(i > 0)
    def _():
        _tile_rows_wait(xs_ref, sem.at[(i + 2) % 3])

    @pl.when(i == n - 1)
    def _():
        _tile_rows_wait(xs_ref, sem.at[slot])


def _dispatch(pad_plan, pos, x1, n_rows):
    n = x1.shape[0] // TM_MOVE
    return pl.pallas_call(
        _dispatch_kernel,
        grid_spec=pltpu.PrefetchScalarGridSpec(
            num_scalar_prefetch=1,
            grid=(n,),
            in_specs=[
                pl.BlockSpec((1, 1, TM_MOVE), lambda i, *_: (i, 0, 0), memory_space=pltpu.SMEM),
                pl.BlockSpec(memory_space=pl.ANY),
            ],
            out_specs=pl.BlockSpec(memory_space=pl.ANY),
            scratch_shapes=[pltpu.VMEM((TM_MOE, X_ROW), f32),
                            pltpu.VMEM((3, GRP_MOVE, SUBLANES, X_ROW), f32),
                            pltpu.SemaphoreType.DMA((3,)), pltpu.SemaphoreType.DMA(()),
                            pltpu.SemaphoreType.DMA((3,))],
        ),
        out_shape=jax.ShapeDtypeStruct((n_rows, X_ROW), f32),
        compiler_params=_cparams(1),
        name="dispatch",
    )(pad_plan, pos, _grouped(x1))


def _expert_kernel(ta_ref, tb_ref, ca_ref, cb_ref, sa_ref, sb_ref, nu_ref,
                   x_ref, wg_hbm, wu_hbm, wd_hbm, g2_ref, b2_ref, y_ref,
                   wga, wua, wda, wgb, wub, wdb, sem_a, sem_b):
    i = pl.program_id(0)
    n_used = nu_ref[0]
    first = i * MOE_TILES
    side_a = (ta_ref, ca_ref, sa_ref, (wga, wua, wda), sem_a)
    side_b = (tb_ref, cb_ref, sb_ref, (wgb, wub, wdb), sem_b)

    def copies(side, j):
        t_ref, _, s_ref, bufs, sem = side
        e, s = t_ref[j], s_ref[j]
        return [pltpu.make_async_copy(w.at[e], buf.at[s], sem.at[s])
                for w, buf in zip((wg_hbm, wu_hbm, wd_hbm), bufs)]

    def fetch(j):
        @pl.when(j < n_used)
        def _():
            for side in (side_a, side_b):
                @pl.when(side[1][j] == 1)
                def _():
                    for cp in copies(side, j):
                        cp.start()

    @pl.when(i == 0)
    def _():
        for j in range(W_AHEAD):
            fetch(j)

    for t in range(MOE_TILES):
        fetch(first + W_AHEAD + t)

    @pl.when(first < n_used)
    def _():
        for t in range(MOE_TILES):
            for side in (side_a, side_b):
                @pl.when(side[1][first + t] == 1)
                def _():
                    for cp in copies(side, first + t):
                        cp.wait()
        rows = [slice(t * TM_MOE, (t + 1) * TM_MOE) for t in range(MOE_TILES)]
        x = [x_ref[r, 0:D_MODEL] for r in rows]
        info = [x_ref[r, D_MODEL:X_ROW] for r in rows]
        xb = [v.astype(bf16) for v in x]
        units = [(t, bufs, s_ref[first + t]) for t in range(MOE_TILES)
                 for bufs, s_ref in (((wga, wua, wda), sa_ref), ((wgb, wub, wdb), sb_ref))]
        hg = [jnp.dot(xb[t], bufs[0][s], preferred_element_type=f32) for t, bufs, s in units]
        hu = [jnp.dot(xb[t], bufs[1][s], preferred_element_type=f32) for t, bufs, s in units]
        mid = [(g * jax.nn.sigmoid(g) * u).astype(bf16) for g, u in zip(hg, hu)]
        ys = [jnp.dot(m, bufs[2][s], preferred_element_type=f32)
              for m, (t, bufs, s) in zip(mid, units)]
        for t in range(MOE_TILES):
            ffn = info[t][:, 2:3] * ys[2 * t] + info[t][:, 3:4] * ys[2 * t + 1]
            y_ref[rows[t], :] = _layer_norm(ALPHA * x[t] + ffn, g2_ref[...], b2_ref[...])

    @pl.when(first >= n_used)
    def _():
        y_ref[...] = jnp.zeros_like(y_ref)


def _experts(tile_ea, tile_eb, n_used, xs, wg, wu, wd, g2, b2):
    nt = xs.shape[0] // TM_MOE

    def ring_plan(tile_e):
        opens = jnp.concatenate([jnp.ones((1,), jnp.int32), (tile_e[1:] != tile_e[:-1]).astype(jnp.int32)])
        return opens, (jnp.cumsum(opens) - 1) % W_SLOTS

    open_a, slot_a = ring_plan(tile_ea)
    open_b, slot_b = ring_plan(tile_eb)
    assert nt % MOE_TILES == 0
    n_used = ((n_used + MOE_TILES - 1) // MOE_TILES) * MOE_TILES
    const = lambda i, *_: (0, 0)
    xmap = lambda i, ta, tb, ca, cb, sa, sb, nu: (jnp.minimum(i, nu[0] // MOE_TILES - 1), 0)
    w_in = pltpu.VMEM((W_SLOTS, D_MODEL, D_EXPERT), bf16)
    w_out = pltpu.VMEM((W_SLOTS, D_EXPERT, D_MODEL), bf16)
    return pl.pallas_call(
        _expert_kernel,
        grid_spec=pltpu.PrefetchScalarGridSpec(
            num_scalar_prefetch=7,
            grid=(nt // MOE_TILES,),
            in_specs=[
                pl.BlockSpec((MOE_TILES * TM_MOE, X_ROW), xmap),
                pl.BlockSpec(memory_space=pl.ANY), pl.BlockSpec(memory_space=pl.ANY),
                pl.BlockSpec(memory_space=pl.ANY),
                pl.BlockSpec((1, D_MODEL), const), pl.BlockSpec((1, D_MODEL), const),
            ],
            out_specs=pl.BlockSpec((MOE_TILES * TM_MOE, D_MODEL), lambda i, *_: (i, 0)),
            scratch_shapes=[w_in, w_in, w_out, w_in, w_in, w_out,
                            pltpu.SemaphoreType.DMA((W_SLOTS,)), pltpu.SemaphoreType.DMA((W_SLOTS,))],
        ),
        out_shape=jax.ShapeDtypeStruct((nt * TM_MOE, D_MODEL), f32),
        compiler_params=_cparams(1),
        name="experts",
    )(tile_ea, tile_eb, open_a, open_b, slot_a.astype(jnp.int32), slot_b.astype(jnp.int32), n_used,
      xs, wg, wu, wd, g2, b2)


def _collect_kernel(pos_ref, posn_ref, ys_ref, op_ref, os_ref, ybuf, sem, *, n_p):
    i = pl.program_id(0)
    n = pl.num_programs(0)
    slot = i % 2

    def gather(p_ref, s):
        def start(k, j, r):
            pltpu.make_async_copy(ys_ref.at[pl.ds(p_ref[0, 0, r], 1)], ybuf.at[s, k, pl.ds(j, 1)],
                                  sem.at[s]).start()

        _for_each_tile_row(start)

    @pl.when(i == 0)
    def _():
        gather(pos_ref, 0)

    @pl.when(i + 1 < n)
    def _():
        gather(posn_ref, 1 - slot)

    _tile_rows_wait(ys_ref, sem.at[slot])

    @pl.when(i < n_p)
    def _():
        op_ref[...] = ybuf[slot].reshape(TM_MOVE, D_MODEL)

    @pl.when(i >= n_p)
    def _():
        os_ref[...] = ybuf[slot].reshape(TM_MOVE, D_MODEL)


def _collect(pos, ys, n_p):
    n = pos.shape[0]
    n_s = n - n_p
    return pl.pallas_call(
        functools.partial(_collect_kernel, n_p=n_p),
        grid=(n,),
        in_specs=[
            pl.BlockSpec((1, 1, TM_MOVE), lambda i: (i, 0, 0), memory_space=pltpu.SMEM),
            pl.BlockSpec((1, 1, TM_MOVE), lambda i: (jnp.minimum(i + 1, n - 1), 0, 0),
                         memory_space=pltpu.SMEM),
            pl.BlockSpec(memory_space=pl.ANY),
        ],
        out_specs=[
            pl.BlockSpec((TM_MOVE, D_MODEL), lambda i: (jnp.minimum(i, n_p - 1), 0)),
            pl.BlockSpec((TM_MOVE, D_MODEL), lambda i: (jnp.maximum(i - n_p, 0), 0)),
        ],
        out_shape=[
            jax.ShapeDtypeStruct((n_p * TM_MOVE, D_MODEL), f32),
            jax.ShapeDtypeStruct((n_s * TM_MOVE, D_MODEL), f32),
        ],
        scratch_shapes=[pltpu.VMEM((2, GRP_MOVE, SUBLANES, D_MODEL), f32), pltpu.SemaphoreType.DMA((2,))],
        compiler_params=_cparams(1),
        name="collect",
    )(pos, pos, ys)


def _block_diag(w):
    n, d, _ = w.shape
    eye = jnp.eye(n, dtype=w.dtype)
    return (eye[:, None, :, None] * w[:, :, None, :]).reshape(n * d, n * d)


def _pair_walk(n):
    total = n * (n - 1) // 2

    def extend(path, used):
        if len(path) == total:
            return path
        a, b = path[-1]
        for c in range(n):
            for nxt in ((a, c), (c, b)):
                key = frozenset(nxt)
                if len(key) == 2 and key not in used:
                    out = extend(path + [nxt], used | {key})
                    if out:
                        return out
        return None

    return extend([(0, 1)], {frozenset((0, 1))})


def _pad_history(buf):
    return jnp.pad(buf, ((0, 0), (SUBLANES - (CONV_W - 1), 0), (0, 0)))


def kernel(x_prompt, x_sample, state_rg_h, state_rg_conv, state_m_C, state_m_n, state_m_m, state_m_conv, w_in, rg_conv_w, rg_conv_b, rg_w_a, rg_b_a, rg_w_x, rg_b_x, rg_lambda, m_conv_w, m_conv_b, m_w_q, m_w_k, m_w_v, m_b_i, m_b_f, m_norm_g, m_skip, w_out, ln1_g, ln1_b, ln2_g, ln2_b, moe_w_group, moe_b_group, moe_w_expert, moe_b_expert, moe_w_gate, moe_w_up, moe_w_down):
    BP, TP, _ = x_prompt.shape
    BS, TS, _ = x_sample.shape
    n_prompt, n_sample = BP * TP, BS * TS
    xp = x_prompt.reshape(n_prompt, D_MODEL)
    xs = x_sample.reshape(n_sample, D_MODEL)
    l = 0

    w_in_p = jnp.pad(w_in[l], ((0, 0), (0, N_PROJ - w_in.shape[-1]))).astype(bf16)
    wa = _block_diag(rg_w_a[l]).astype(bf16)
    wx = _block_diag(rg_w_x[l]).astype(bf16)
    row = lambda v: v.reshape(1, -1)
    bif = jnp.pad(jnp.concatenate([m_b_i[l], m_b_f[l]]), (0, LANES - 2 * M_HEADS)).reshape(1, LANES)
    wqk = jnp.concatenate([m_w_q[l], m_w_k[l]], axis=-1).astype(bf16)
    wv = m_w_v[l].astype(bf16)
    wkT = m_w_k[l].transpose(0, 2, 1).astype(bf16)
    w_route = jnp.pad(jnp.concatenate([moe_w_group[l], moe_w_expert[l]], axis=1),
                      ((0, 0), (0, LANES - N_GROUPS - N_EXPERTS)))
    wr_hi = w_route.astype(bf16)
    wr_lo = (w_route - wr_hi.astype(f32)).astype(bf16)
    wr = jnp.concatenate([wr_hi, wr_lo], axis=1)
    b_route = jnp.pad(jnp.concatenate([moe_b_group[l], moe_b_expert[l]]),
                      (0, LANES - N_GROUPS - N_EXPERTS)).reshape(1, LANES)

    proj_p, proj_s = _inproj(xp, xs, w_in_p)

    rg_args = (rg_conv_w[l], row(rg_conv_b[l]), wa, row(rg_b_a[l]), wx, row(rg_b_x[l]), row(rg_lambda[l]))
    zeros = lambda *s: jnp.zeros(s, f32)
    s_blk = M_ROWS // TS
    yrg_p, p_rg_h, p_rg_conv = _rglru(proj_p, 0, BP, TP, 1, RG_ROWS_PROMPT, zeros(BP, SUBLANES, D_RG),
                                      zeros(BP, 1, D_RG), *rg_args)
    yrg_s, s_rg_h, s_rg_conv = _rglru(proj_s, 0, BS, TS, RG_ROWS_PROMPT // TS, TS,
                                      _pad_history(state_rg_conv[l]), state_rg_h[l].reshape(BS, 1, D_RG), *rg_args)

    m_args = (m_conv_w[l], row(m_conv_b[l]), wqk, wv, wkT, bif, row(m_norm_g[l]), row(m_skip[l]))
    m0_p = jnp.full((SUBLANES, BP * LANES), M_INIT, f32)
    ym_p, p_m_C, p_m_n, p_m_m, p_m_conv, wg_b, wu_b, wd_b = _mlstm(
        proj_p, BP, TP, 1, M_ROWS, M_BLOCKS_PROMPT, zeros(BP, SUBLANES, D_M), zeros(BP, M_HEADS, M_DH, M_DH),
        zeros(BP, M_HEADS, 1, M_DH), m0_p, *m_args, riders=(moe_w_gate[l], moe_w_up[l], moe_w_down[l]))
    m0_s = jnp.pad(jnp.repeat(state_m_m[l].T, TS, axis=1), ((0, SUBLANES - M_HEADS), (0, 0)))
    ym_s, s_m_C, s_m_n, s_m_m, s_m_conv = _mlstm(
        proj_s, BS, TS, s_blk, TS, M_BLOCKS_SAMPLE, _pad_history(state_m_conv[l]), state_m_C[l],
        state_m_n[l].reshape(BS, M_HEADS, 1, M_DH), m0_s, *m_args)

    x1, cnt = _outproj(xp, xs, yrg_p, yrg_s, ym_p, ym_s, w_out[l].astype(bf16),
                       row(ln1_g[l]), row(ln1_b[l]), wr, b_route)

    n_tok = n_prompt + n_sample
    n_tiles = n_tok // TM_MOE + N_CLASSES
    lay_cls, lay_a, lay_b = [], [], []
    for g in range(N_GROUPS):
        for ja, jb in _pair_walk(EXPERTS_PER_GROUP):
            assert ja < jb
            n_first = (EXPERTS_PER_GROUP - 1) * ja - ja * (ja - 1) // 2
            lay_cls.append(PAIRS_PER_GROUP * g + n_first + jb - ja - 1)
            lay_a.append(g * EXPERTS_PER_GROUP + ja)
            lay_b.append(g * EXPERTS_PER_GROUP + jb)
    classes = jnp.arange(N_CLASSES, dtype=jnp.int32)
    in_slot = jnp.array(lay_cls, jnp.int32)[:, None] == classes
    counts = jnp.sum(jnp.where(in_slot, cnt[0, :N_CLASSES].astype(jnp.int32), 0), axis=1)
    padded = ((counts + TM_MOE - 1) // TM_MOE) * TM_MOE
    ends = jnp.cumsum(padded)
    offs = ends - padded
    offs_of_cls = jnp.sum(jnp.where(in_slot, offs[:, None], 0), axis=0)
    cls = x1[:, D_MODEL].astype(jnp.int32)
    rank = x1[:, D_MODEL + 1].astype(jnp.int32)
    pos = jnp.sum(jnp.where(cls[:, None] == classes, offs_of_cls, 0), axis=-1) + rank
    pos = pos.reshape(n_tok // TM_MOVE, 1, TM_MOVE)
    tiles = jnp.arange(n_tiles, dtype=jnp.int32)
    tile_slot = jnp.minimum(jnp.sum(tiles[:, None] >= (ends // TM_MOE)[None, :], axis=1), N_CLASSES - 1)
    on_slot = tile_slot[:, None] == classes
    tile_ea = jnp.sum(jnp.where(on_slot, jnp.array(lay_a, jnp.int32), 0), axis=-1).astype(jnp.int32)
    tile_eb = jnp.sum(jnp.where(on_slot, jnp.array(lay_b, jnp.int32), 0), axis=-1).astype(jnp.int32)
    n_used = (ends[-1] // TM_MOE).reshape(1).astype(jnp.int32)
    pad_row0 = ((offs + counts) // SUBLANES) * SUBLANES
    pad_plan = jnp.concatenate([jnp.stack([pad_row0, (ends - pad_row0) // SUBLANES], axis=1).reshape(-1),
                                     n_used]).astype(jnp.int32)

    x_sorted = _dispatch(pad_plan, pos, x1, n_tiles * TM_MOE)
    y_sorted = _experts(tile_ea, tile_eb, n_used, x_sorted, wg_b, wu_b, wd_b,
                        row(ln2_g[l]), row(ln2_b[l]))
    y_p, y_s = _collect(pos, y_sorted, n_prompt // TM_MOVE)

    return (y_p.reshape(BP, TP, D_MODEL), y_s.reshape(BS, TS, D_MODEL),
            p_rg_h.reshape(1, BP, D_RG), p_rg_conv[None], p_m_C[None], p_m_n.reshape(1, BP, M_HEADS, M_DH),
            p_m_m[None], p_m_conv[None],
            s_rg_h.reshape(1, BS, D_RG), s_rg_conv[None], s_m_C[None], s_m_n.reshape(1, BS, M_HEADS, M_DH),
            s_m_m[None], s_m_conv[None])
```

```python
import functools

import jax
import jax.numpy as jnp
from jax import lax
from jax.experimental import pallas as pl
from jax.experimental.pallas import tpu as pltpu

f32 = jnp.float32
bf16 = jnp.bfloat16

D_MODEL = 1024
D_RG = 512
RG_C = 8.0
D_M = 512
M_HEADS = 4
M_DH = 128
CONV_W = 4
N_GROUPS = 4
EXPERTS_PER_GROUP = 8
N_EXPERTS = 32
D_EXPERT = 256
ALPHA = 2.0 ** 0.25
LN_EPS = 1e-5
M_INIT = -1.0e4

LANES = 128
SUBLANES = 8
TM = 512
TM_MOVE = 1024
OUT_PARTS = 4
TM_MOE = 128
MOE_TILES = 4
W_AHEAD = 8
W_SLOTS = W_AHEAD + MOE_TILES
PAIRS_PER_GROUP = EXPERTS_PER_GROUP * (EXPERTS_PER_GROUP - 1) // 2
N_CLASSES = N_GROUPS * PAIRS_PER_GROUP
X_ROW = D_MODEL + LANES
N_PROJ = 2 * D_RG + 2 * D_M + LANES
RG_ROWS = 512
M_ROWS = 128
M_BLOCKS_PROMPT = 8
M_BLOCKS_SAMPLE = 2
V7X_VMEM_BYTES = 64 * 1024 * 1024
VMEM_LIMIT = V7X_VMEM_BYTES - 8 * 1024 * 1024


def _cparams(n_axes):
    return pltpu.CompilerParams(dimension_semantics=("arbitrary",) * n_axes,
                                vmem_limit_bytes=VMEM_LIMIT)


def _inproj_kernel(xp_ref, xs_ref, w_ref, op_ref, os_ref, *, n_p):
    i = pl.program_id(0)

    def run(x_ref, o_ref):
        o_ref[...] = jnp.dot(x_ref[...].astype(bf16), w_ref[...], preferred_element_type=f32)

    @pl.when(i < n_p)
    def _():
        run(xp_ref, op_ref)

    @pl.when(i >= n_p)
    def _():
        run(xs_ref, os_ref)


def _inproj(xp, xs, w):
    n_p, n_s = xp.shape[0] // TM, xs.shape[0] // TM
    pmap = lambda i: (jnp.minimum(i, n_p - 1), 0)
    smap = lambda i: (jnp.maximum(i - n_p, 0), 0)
    return pl.pallas_call(
        functools.partial(_inproj_kernel, n_p=n_p),
        grid=(n_p + n_s,),
        in_specs=[
            pl.BlockSpec((TM, D_MODEL), pmap),
            pl.BlockSpec((TM, D_MODEL), smap),
            pl.BlockSpec((D_MODEL, N_PROJ), lambda i: (0, 0)),
        ],
        out_specs=[pl.BlockSpec((TM, N_PROJ), pmap), pl.BlockSpec((TM, N_PROJ), smap)],
        out_shape=[jax.ShapeDtypeStruct((n_p * TM, N_PROJ), f32),
                   jax.ShapeDtypeStruct((n_s * TM, N_PROJ), f32)],
        compiler_params=_cparams(1),
        name="inproj",
    )(xp, xs, w)


def _causal_conv(tail_scr, x, cw_ref, cb_ref, S, L, C):
    tail = tail_scr[...]
    sub = lax.broadcasted_iota(jnp.int32, (S, SUBLANES, C), 1)
    acc = cb_ref[...] + cw_ref[CONV_W - 1:CONV_W, :] * x
    for d in range(1, CONV_W):
        back = pltpu.roll(x, d, 0).reshape(S, L, C)
        head = jnp.where(sub < d, pltpu.roll(tail, d, 1), back[:, 0:SUBLANES, :])
        if L > SUBLANES:
            back = jnp.concatenate([head, back[:, SUBLANES:, :]], axis=1)
        else:
            back = head
        acc = acc + cw_ref[CONV_W - 1 - d:CONV_W - d, :] * back.reshape(S * L, C)
    tail_scr[...] = x.reshape(S, L, C)[:, L - SUBLANES:, :]
    return acc


def _softplus(x):
    return jnp.maximum(x, 0.0) + jnp.log1p(jnp.exp(-jnp.abs(x)))


def _rglru_kernel(x_ref, g_ref, buf_ref, h0_ref, cw_ref, cb_ref, wa_ref, ba_ref, wx_ref, bx_ref,
                  lam_ref, y_ref, hN_ref, cN_ref, xs_scr, *, S, L):
    R = S * L
    t = pl.program_id(1)

    @pl.when(t == 0)
    def _():
        xs_scr[...] = buf_ref[...]
        hN_ref[...] = h0_ref[...]

    x = x_ref[...]
    xc = _causal_conv(xs_scr, x, cw_ref, cb_ref, S, L, D_RG)
    xcb = xc.astype(bf16)
    r = jax.nn.sigmoid(jnp.dot(xcb, wa_ref[...], preferred_element_type=f32) + ba_ref[...])
    ig = jax.nn.sigmoid(jnp.dot(xcb, wx_ref[...], preferred_element_type=f32) + bx_ref[...])
    log_a = (-RG_C) * r * _softplus(-lam_ref[...])
    a = jnp.exp(log_a)
    th = jnp.tanh(log_a)
    u = jnp.sqrt(-2.0 * th / (1.0 - th)) * ig * xc

    n_grp, grp_per_seq = R // SUBLANES, L // SUBLANES
    a3 = a.reshape(n_grp, SUBLANES, D_RG)
    u3 = u.reshape(n_grp, SUBLANES, D_RG)
    sub = lax.broadcasted_iota(jnp.int32, (n_grp, SUBLANES, D_RG), 1)
    s = 1
    while s < SUBLANES:
        ok = sub >= s
        a_sh = pltpu.roll(a3, s, 1)
        u_sh = pltpu.roll(u3, s, 1)
        u3 = jnp.where(ok, a3 * u_sh + u3, u3)
        a3 = jnp.where(ok, a3 * a_sh, a3)
        s *= 2
    h0 = hN_ref[...]
    groups = []
    for kg in range(n_grp):
        carry = h0[kg // grp_per_seq] if kg % grp_per_seq == 0 else groups[-1][SUBLANES - 1:SUBLANES, :]
        groups.append(a3[kg] * carry + u3[kg])
    h = jnp.concatenate(groups, axis=0)

    y_ref[...] = h * jax.nn.gelu(g_ref[...], approximate=True)
    hN_ref[...] = h.reshape(S, L, D_RG)[:, L - 1:L, :]
    cN_ref[...] = xs_scr[:, SUBLANES - (CONV_W - 1):SUBLANES, :]


def _rglru(proj, row0, B, T, S, L, buf8, h0, cw, cb, wa, ba, wx, bx, lam):
    R = S * L
    nb, nt = B // S, T // L
    blk0 = row0 // R
    row_map = lambda b, t: (blk0 + b * nt + t, 0)
    const2 = lambda b, t: (0, 0)
    return pl.pallas_call(
        functools.partial(_rglru_kernel, S=S, L=L),
        grid=(nb, nt),
        in_specs=[
            pl.BlockSpec((R, D_RG), row_map),
            pl.BlockSpec((R, D_RG), lambda b, t: (blk0 + b * nt + t, 1)),
            pl.BlockSpec((S, SUBLANES, D_RG), lambda b, t: (b, 0, 0)),
            pl.BlockSpec((S, 1, D_RG), lambda b, t: (b, 0, 0)),
            pl.BlockSpec((CONV_W, D_RG), const2),
            pl.BlockSpec((1, D_RG), const2),
            pl.BlockSpec((D_RG, D_RG), const2),
            pl.BlockSpec((1, D_RG), const2),
            pl.BlockSpec((D_RG, D_RG), const2),
            pl.BlockSpec((1, D_RG), const2),
            pl.BlockSpec((1, D_RG), const2),
        ],
        out_specs=[
            pl.BlockSpec((R, D_RG), lambda b, t: (b * nt + t, 0)),
            pl.BlockSpec((S, 1, D_RG), lambda b, t: (b, 0, 0)),
            pl.BlockSpec((S, CONV_W - 1, D_RG), lambda b, t: (b, 0, 0)),
        ],
        out_shape=[
            jax.ShapeDtypeStruct((B * T, D_RG), f32),
            jax.ShapeDtypeStruct((B, 1, D_RG), f32),
            jax.ShapeDtypeStruct((B, CONV_W - 1, D_RG), f32),
        ],
        scratch_shapes=[pltpu.VMEM((S, SUBLANES, D_RG), f32)],
        compiler_params=_cparams(2),
        name="rglru",
    )(proj, proj, buf8, h0, cw, cb, wa, ba, wx, bx, lam)


def _seg_scan(x, op, fill, tin, L, reverse=False):
    s = 1
    while s < L:
        if reverse:
            sh = pltpu.roll(x, LANES - s, 1)
            ok = tin < L - s
        else:
            sh = pltpu.roll(x, s, 1)
            ok = tin >= s
        x = op(x, jnp.where(ok, sh, fill))
        s *= 2
    return x


def _mlstm_kernel(*refs, S, L, G, n_riders):
    (x_ref, z_ref, gt_ref, buf_ref, c0_ref, n0_ref, m0_ref,
     cw_ref, cb_ref, wqk_ref, wv_ref, wkT_ref, bif_ref, ng_ref, sk_ref) = refs[:15]
    rider_in = refs[15:15 + n_riders]
    y_ref, cN_ref, nN_ref, mN_ref, bN_ref = refs[15 + n_riders:20 + n_riders]
    rider_out = refs[20 + n_riders:20 + 2 * n_riders]
    xs_scr, m_scr = refs[20 + 2 * n_riders:]
    for src, dst in zip(rider_in, rider_out):
        dst[...] = src[...].astype(bf16)

    R = S * L
    GS, GR = G * S, G * R
    j = pl.program_id(1)

    @pl.when(j == 0)
    def _():
        xs_scr[...] = buf_ref[...]
        cN_ref[...] = c0_ref[...]
        nN_ref[...] = n0_ref[...]
        m_scr[...] = m0_ref[...]

    x = x_ref[...].reshape(GR, D_M)
    xc = _causal_conv(xs_scr, x, cw_ref, cb_ref, GS, L, D_M)
    xa = xc * jax.nn.sigmoid(xc)
    xab = xa.astype(bf16)
    xb = x.astype(bf16)

    il_parts, f_parts = [], []
    for g in range(G):
        gT = (gt_ref[g] + bif_ref[...]).T
        il_parts.append(gT[0:SUBLANES, :])
        f_parts.append(pltpu.roll(gT[0:SUBLANES, :], M_HEADS, 0))
    il = jnp.concatenate(il_parts, axis=0)
    fl = -_softplus(-jnp.concatenate(f_parts, axis=0))
    tin = lax.broadcasted_iota(jnp.int32, (G * SUBLANES, LANES), 1) % L
    bcum = _seg_scan(fl, jnp.add, 0.0, tin, L)
    a = il - bcum
    m_prev = m_scr[...].reshape(G * SUBLANES, LANES)
    big_m = jnp.maximum(m_prev, _seg_scan(a, jnp.maximum, -jnp.inf, tin, L))
    m_t = bcum + big_m
    if S == 1:
        m_last = jnp.broadcast_to(big_m[:, LANES - 1:LANES], big_m.shape)
    else:
        m_last = _seg_scan(big_m, jnp.maximum, -jnp.inf, tin, L, reverse=True)
    rows = [big_m, jnp.exp(m_prev - big_m), jnp.exp(-m_t), jnp.exp(a - m_last), jnp.exp(m_prev - m_last)]
    mN_ref[...] = m_t.reshape(G, SUBLANES, LANES)
    m_scr[...] = jnp.broadcast_to(m_t[:, LANES - 1:LANES], m_t.shape).reshape(G, SUBLANES, LANES)
    pad_rows = jnp.zeros((LANES - len(rows) * SUBLANES, LANES), f32)
    cols = []
    for g in range(G):
        gs = slice(g * SUBLANES, (g + 1) * SUBLANES)
        cols.append(jnp.concatenate([r[gs] for r in rows] + [pad_rows], axis=0).T)

    def col(g, q, h):
        return cols[g][:, SUBLANES * q + h:SUBLANES * q + h + 1]

    qs, ks, vs, kTs = [], [], [], []
    for h in range(M_HEADS):
        hs = slice(h * M_DH, (h + 1) * M_DH)
        qk_h = jnp.dot(xab[:, hs], wqk_ref[h], preferred_element_type=f32)
        qs.append(qk_h[:, 0:M_DH])
        ks.append(qk_h[:, M_DH:2 * M_DH] * (M_DH ** -0.5))
        vs.append(jnp.dot(xb[:, hs], wv_ref[h], preferred_element_type=f32))
        if S == 1:
            kT_h = lax.dot_general(wkT_ref[h], xab[:, hs], (((1,), (1,)), ((), ())),
                                   preferred_element_type=f32)
            kTs.append((kT_h * (M_DH ** -0.5)).astype(bf16))

    ti = lax.broadcasted_iota(jnp.int32, (R, R), 0)
    si = lax.broadcasted_iota(jnp.int32, (R, R), 1)
    mask = (si <= ti) & ((ti // L) == (si // L))
    ones_b = jnp.ones((R, M_DH), bf16)
    pairs = [(g, h) for g in range(G) for h in range(M_HEADS)]
    blk = lambda g: slice(g * R, (g + 1) * R)
    seqs = [slice(b * L, (b + 1) * L) for b in range(S)]
    q = {(g, h): qs[h][blk(g)] for g, h in pairs}
    k = {(g, h): ks[h][blk(g)] for g, h in pairs}
    v = {(g, h): vs[h][blk(g)] for g, h in pairs}
    qb = {p: q[p].astype(bf16) for p in pairs}
    kb = {p: k[p].astype(bf16) for p in pairs}
    qk = {p: lax.dot_general(qb[p], kb[p], (((1,), (1,)), ((), ())), preferred_element_type=f32)
          for p in pairs}
    sm = {}
    for g, h in pairs:
        a_row = a[g * SUBLANES + h:g * SUBLANES + h + 1, :]
        decay = jnp.exp(jnp.where(mask, a_row - col(g, 0, h), -jnp.inf))
        sm[g, h] = (qk[g, h] * decay).astype(bf16)
    nd = {p: jnp.dot(sm[p], jnp.concatenate([v[p].astype(bf16), ones_b], axis=1),
                     preferred_element_type=f32) for p in pairs}
    c_old = {(g, h, b): cN_ref[g * S + b, h] for g, h in pairs for b in range(S)}
    n_old = {(g, h, b): nN_ref[g * S + b, h] for g, h in pairs for b in range(S)}
    q_c = {(g, h, b): jnp.dot(qb[g, h][seqs[b]], c_old[g, h, b].astype(bf16), preferred_element_type=f32)
           for g, h in pairs for b in range(S)}
    hh = {}
    for g, h in pairs:
        g_col, e_col = col(g, 1, h), col(g, 2, h)
        pieces = []
        for b, rs in enumerate(seqs):
            q_n = jnp.sum(q[g, h][rs] * n_old[g, h, b], axis=1, keepdims=True)
            num = nd[g, h][rs, 0:M_DH] + g_col[rs] * q_c[g, h, b]
            den = nd[g, h][rs, M_DH:2 * M_DH] + g_col[rs] * q_n
            pieces.append(num / jnp.maximum(jnp.abs(den), e_col[rs]))
        hh[g, h] = pieces[0] if S == 1 else jnp.concatenate(pieces, axis=0)
    mu = {p: jnp.mean(hh[p], axis=1, keepdims=True) for p in pairs}
    var = {p: jnp.mean(jnp.square(hh[p] - mu[p]), axis=1, keepdims=True) for p in pairs}
    hn_blocks = [jnp.concatenate([(hh[g, h] - mu[g, h]) * lax.rsqrt(var[g, h] + LN_EPS)
                                  for h in range(M_HEADS)], axis=1) for g in range(G)]
    new_c, new_n = [], []
    for g, h in pairs:
        w_col = col(g, 3, h)
        wv = (w_col * v[g, h]).astype(bf16)
        wk = w_col * k[g, h]
        for b, rs in enumerate(seqs):
            g_end = col(g, 4, h)[(b + 1) * L - 1:(b + 1) * L, :]
            if S == 1:
                kv = jnp.dot(kTs[h][:, blk(g)], wv, preferred_element_type=f32)
            else:
                kv = lax.dot_general(kb[g, h][rs], wv[rs], (((0,), (0,)), ((), ())),
                                     preferred_element_type=f32)
            new_c.append((g * S + b, h, g_end * c_old[g, h, b] + kv))
            new_n.append((g * S + b, h, g_end * n_old[g, h, b] + jnp.sum(wk[rs], axis=0, keepdims=True)))

    hn = jnp.concatenate(hn_blocks, axis=0) * ng_ref[...]
    y = jax.nn.sigmoid(z_ref[...].reshape(GR, D_M)) * (hn + sk_ref[...] * xa)
    y_ref[...] = y.reshape(G, R, D_M)
    for sq, h, val in new_c:
        cN_ref[sq, h] = val
    for sq, h, val in new_n:
        nN_ref[sq, h] = val
    bN_ref[...] = xs_scr[:, SUBLANES - (CONV_W - 1):SUBLANES, :]


def _mlstm(proj, B, T, S, L, G, buf8, c0, n0, m0_lanes, cw, cb, wqk, wv, wkT, bif, ng, sk, riders=()):
    R = S * L
    assert R == M_ROWS == LANES
    nblk, nc = B // S, T // L
    nb = nblk // G
    assert nblk * S == B and nc * L == T and nb * G == nblk and (S == 1 or nc == 1)
    steps = nb * nc
    assert all(r.shape[0] % steps == 0 for r in riders)
    rider_specs = [pl.BlockSpec((r.shape[0] // steps,) + r.shape[1:], lambda b, j: (b * nc + j, 0, 0))
                   for r in riders]
    proj4 = proj.reshape(nblk, nc, R, N_PROJ)
    m0_3 = m0_lanes.reshape(SUBLANES, nblk, LANES).transpose(1, 0, 2)
    const2 = lambda b, j: (0, 0)
    const3 = lambda b, j: (0, 0, 0)
    gate_blk = (2 * D_RG + 2 * D_M) // LANES
    GS = G * S
    outs = pl.pallas_call(
        functools.partial(_mlstm_kernel, S=S, L=L, G=G, n_riders=len(riders)),
        grid=(nb, nc),
        in_specs=[
            pl.BlockSpec((G, None, R, D_M), lambda b, j: (b, j, 0, 2)),
            pl.BlockSpec((G, None, R, D_M), lambda b, j: (b, j, 0, 3)),
            pl.BlockSpec((G, None, R, LANES), lambda b, j: (b, j, 0, gate_blk)),
            pl.BlockSpec((GS, SUBLANES, D_M), lambda b, j: (b, 0, 0)),
            pl.BlockSpec((GS, M_HEADS, M_DH, M_DH), lambda b, j: (b, 0, 0, 0)),
            pl.BlockSpec((GS, M_HEADS, 1, M_DH), lambda b, j: (b, 0, 0, 0)),
            pl.BlockSpec((G, SUBLANES, LANES), lambda b, j: (b, 0, 0)),
            pl.BlockSpec((CONV_W, D_M), const2),
            pl.BlockSpec((1, D_M), const2),
            pl.BlockSpec((M_HEADS, M_DH, 2 * M_DH), const3),
            pl.BlockSpec((M_HEADS, M_DH, M_DH), const3),
            pl.BlockSpec((M_HEADS, M_DH, M_DH), const3),
            pl.BlockSpec((1, LANES), const2),
            pl.BlockSpec((1, D_M), const2),
            pl.BlockSpec((1, D_M), const2),
        ] + rider_specs,
        out_specs=[
            pl.BlockSpec((G, None, R, D_M), lambda b, j: (b, j, 0, 0)),
            pl.BlockSpec((GS, M_HEADS, M_DH, M_DH), lambda b, j: (b, 0, 0, 0)),
            pl.BlockSpec((GS, M_HEADS, 1, M_DH), lambda b, j: (b, 0, 0, 0)),
            pl.BlockSpec((G, None, SUBLANES, LANES), lambda b, j: (b, j, 0, 0)),
            pl.BlockSpec((GS, CONV_W - 1, D_M), lambda b, j: (b, 0, 0)),
        ] + rider_specs,
        out_shape=[
            jax.ShapeDtypeStruct((nblk, nc, R, D_M), f32),
            jax.ShapeDtypeStruct((B, M_HEADS, M_DH, M_DH), f32),
            jax.ShapeDtypeStruct((B, M_HEADS, 1, M_DH), f32),
            jax.ShapeDtypeStruct((nblk, nc, SUBLANES, LANES), f32),
            jax.ShapeDtypeStruct((B, CONV_W - 1, D_M), f32),
        ] + [jax.ShapeDtypeStruct(r.shape, bf16) for r in riders],
        scratch_shapes=[pltpu.VMEM((GS, SUBLANES, D_M), f32), pltpu.VMEM((G, SUBLANES, LANES), f32)],
        compiler_params=_cparams(2),
        name="mlstm",
    )(proj4, proj4, proj4, buf8, c0, n0, m0_3, cw, cb, wqk, wv, wkT, bif, ng, sk, *riders)
    y, c_new, n_new, m_t, b_new = outs[:5]
    m_last = m_t[:, nc - 1, :M_HEADS, :].reshape(nblk, M_HEADS, S, L)[:, :, :, L - 1]
    m_last = m_last.transpose(0, 2, 1).reshape(B, M_HEADS)
    return (y.reshape(B * T, D_M), c_new, n_new, m_last, b_new) + tuple(outs[5:])


def _layer_norm(x, g, b):
    mu = jnp.mean(x, axis=-1, keepdims=True)
    var = jnp.mean(jnp.square(x - mu), axis=-1, keepdims=True)
    return (x - mu) * lax.rsqrt(var + LN_EPS) * g + b


def _first_lane_of_max(vals, lane_f):
    vmax = jnp.max(vals, axis=1, keepdims=True)
    idx = jnp.min(jnp.where(vals == vmax, lane_f, float(LANES)), axis=1, keepdims=True)
    return vmax, idx


def _outproj_kernel(xp_ref, xs_ref, rgp_ref, rgs_ref, mp_ref, ms_ref, wo_ref, g1_ref, b1_ref,
                    wr_ref, br_ref, x1_ref, cnt_ref, *, n_p):
    i = pl.program_id(0)

    @pl.when(i == 0)
    def _():
        cnt_ref[...] = jnp.zeros_like(cnt_ref)

    def route(lg, lane, lane_f):
        neg = -jnp.inf
        gl = jnp.where(lane < N_GROUPS, lg, neg)
        gmax, gidx = _first_lane_of_max(gl, lane_f)
        p_g = 1.0 / jnp.sum(jnp.exp(gl - gmax), axis=1, keepdims=True)
        e_lo = float(N_GROUPS) + float(EXPERTS_PER_GROUP) * gidx
        el = jnp.where((lane_f >= e_lo) & (lane_f < e_lo + float(EXPERTS_PER_GROUP)), lg, neg)
        v1, i1 = _first_lane_of_max(el, lane_f)
        v2, i2 = _first_lane_of_max(jnp.where(lane_f == i1, neg, el), lane_f)
        d = jnp.exp(v2 - v1)
        w1 = p_g / (1.0 + d)
        w2 = p_g * d / (1.0 + d)
        first_low = i1 < i2
        j_lo = jnp.minimum(i1, i2) - e_lo
        j_hi = jnp.maximum(i1, i2) - e_lo
        n_first = float(EXPERTS_PER_GROUP - 1) * j_lo - 0.5 * j_lo * (j_lo - 1.0)
        cls = float(PAIRS_PER_GROUP) * gidx + n_first + (j_hi - j_lo - 1.0)
        return cls, jnp.where(first_low, w1, w2), jnp.where(first_low, w2, w1)

    def run(x_ref, rg_ref, m_ref):
        rp = TM // OUT_PARTS
        parts = [slice(p * rp, (p + 1) * rp) for p in range(OUT_PARTS)]
        lane = lax.broadcasted_iota(jnp.int32, (rp, LANES), 1)
        lane_f = lane.astype(f32)
        heads = [jnp.concatenate([rg_ref[r, :].astype(bf16), m_ref[r, :].astype(bf16)], axis=1) for r in parts]
        mix = [jnp.dot(h, wo_ref[...], preferred_element_type=f32) for h in heads]
        x1 = [_layer_norm(ALPHA * x_ref[r, :] + mx, g1_ref[...], b1_ref[...]) for r, mx in zip(parts, mix)]
        for r, v in zip(parts, x1):
            x1_ref[r, 0:D_MODEL] = v

        hi = [v.astype(bf16) for v in x1]
        lo = [(v - h.astype(f32)).astype(bf16) for v, h in zip(x1, hi)]
        hi_terms = [jnp.dot(h, wr_ref[...], preferred_element_type=f32) for h in hi]
        lo_term = [jnp.dot(v, wr_ref[:, 0:LANES], preferred_element_type=f32) for v in lo]
        lg = [ht[:, 0:LANES] + ht[:, LANES:2 * LANES] + lt + br_ref[...] for ht, lt in zip(hi_terms, lo_term)]
        routed = [route(v, lane, lane_f) for v in lg]

        ti = lax.broadcasted_iota(jnp.int32, (rp, rp), 0)
        si = lax.broadcasted_iota(jnp.int32, (rp, rp), 1)
        tri = (si <= ti).astype(bf16)
        hot = [lane_f == cls for cls, _, _ in routed]
        cum = [jnp.dot(tri, h.astype(bf16), preferred_element_type=f32) for h in hot]
        seen = cnt_ref[...]
        for r, h, cm, (cls, w_lo, w_hi) in zip(parts, hot, cum, routed):
            rank = jnp.sum(jnp.where(h, cm - 1.0 + seen, 0.0), axis=1, keepdims=True)
            seen = seen + cm[rp - 1:rp, :]
            info = jnp.zeros((rp, LANES), f32)
            for c, val in enumerate((cls, rank, w_lo, w_hi)):
                info = jnp.where(lane == c, val, info)
            x1_ref[r, D_MODEL:D_MODEL + LANES] = info
        cnt_ref[...] = seen

    @pl.when(i < n_p)
    def _():
        run(xp_ref, rgp_ref, mp_ref)

    @pl.when(i >= n_p)
    def _():
        run(xs_ref, rgs_ref, ms_ref)


def _outproj(xp, xs, rgp, rgs, mp, ms, wo, g1, b1, wr, br):
    n_p, n_s = xp.shape[0] // TM, xs.shape[0] // TM
    n = n_p + n_s
    pmap = lambda i: (jnp.minimum(i, n_p - 1), 0)
    smap = lambda i: (jnp.maximum(i - n_p, 0), 0)
    const = lambda i: (0, 0)
    return pl.pallas_call(
        functools.partial(_outproj_kernel, n_p=n_p),
        grid=(n,),
        in_specs=[
            pl.BlockSpec((TM, D_MODEL), pmap), pl.BlockSpec((TM, D_MODEL), smap),
            pl.BlockSpec((TM, D_RG), pmap), pl.BlockSpec((TM, D_RG), smap),
            pl.BlockSpec((TM, D_M), pmap), pl.BlockSpec((TM, D_M), smap),
            pl.BlockSpec((D_RG + D_M, D_MODEL), const),
            pl.BlockSpec((1, D_MODEL), const), pl.BlockSpec((1, D_MODEL), const),
            pl.BlockSpec((D_MODEL, 2 * LANES), const),
            pl.BlockSpec((1, LANES), const),
        ],
        out_specs=[
            pl.BlockSpec((TM, X_ROW), lambda i: (i, 0)),
            pl.BlockSpec((1, LANES), const),
        ],
        out_shape=[
            jax.ShapeDtypeStruct((n * TM, X_ROW), f32),
            jax.ShapeDtypeStruct((1, LANES), f32),
        ],
        compiler_params=_cparams(1),
        name="outproj",
    )(xp, xs, rgp, rgs, mp, ms, wo, g1, b1, wr, br)


GRP_MOVE = TM_MOVE // SUBLANES


def _grouped(x):
    return x.reshape(x.shape[0] // SUBLANES, SUBLANES, x.shape[1])


def _tile_rows_wait(hbm_ref, sem):
    rows = hbm_ref.at[pl.ds(0, TM_MOVE)]
    pltpu.make_async_copy(rows, rows, sem).wait()


def _for_each_tile_row(body):
    def group(k, c):
        for j in range(SUBLANES):
            body(k, j, k * SUBLANES + j)
        return c

    lax.fori_loop(0, GRP_MOVE, group, 0)


def _dispatch_kernel(pad_ref, pos_ref, x1_ref, xs_ref, zero_scr, xbuf, sem, zsem, lsem):
    i = pl.program_id(0)
    n = pl.num_programs(0)

    def zero_tile(t):
        return pltpu.make_async_copy(zero_scr, xs_ref.at[pl.ds(pl.multiple_of(t * TM_MOE, TM_MOE), TM_MOE)], zsem)

    def zero_chunk(row0):
        return pltpu.make_async_copy(zero_scr.at[pl.ds(0, SUBLANES)],
                                     xs_ref.at[pl.ds(pl.multiple_of(row0, SUBLANES), SUBLANES)], zsem)

    @pl.when(i == 0)
    def _():
        zero_scr[...] = jnp.zeros_like(zero_scr)
        n_used = pad_ref[2 * N_CLASSES]
        n_tiles = xs_ref.shape[0] // TM_MOE
        for go in (lambda cp: cp.start(), lambda cp: cp.wait()):
            lax.fori_loop(n_used, n_tiles, lambda t, c: (go(zero_tile(t)), c)[1], 0)

            def per_class(cl, c):
                row0 = pad_ref[2 * cl]
                lax.fori_loop(0, pad_ref[2 * cl + 1],
                              lambda k, c2: (go(zero_chunk(row0 + k * SUBLANES)), c2)[1], 0)
                return c

            lax.fori_loop(0, N_CLASSES, per_class, 0)

    def load(t, s):
        return pltpu.make_async_copy(x1_ref.at[pl.ds(t * GRP_MOVE, GRP_MOVE)], xbuf.at[s], lsem.at[s])

    @pl.when(i == 0)
    def _():
        load(0, 0).start()

    @pl.when(i + 1 < n)
    def _():
        load(i + 1, (i + 1) % 3).start()

    slot = i % 3
    load(i, slot).wait()

    def start(k, j, r):
        pltpu.make_async_copy(xbuf.at[slot, k, pl.ds(j, 1)], xs_ref.at[pl.ds(pos_ref[0, 0, r], 1)],
                              sem.at[slot]).start()

    _for_each_tile_row(start)

    @pl.when(i > 0)
    def _():
        _tile_rows_wait(xs_ref, sem.at[(i + 2) % 3])

    @pl.when(i == n - 1)
    def _():
        _tile_rows_wait(xs_ref, sem.at[slot])


def _dispatch(pad_plan, pos, x1, n_rows):
    n = x1.shape[0] // TM_MOVE
    return pl.pallas_call(
        _dispatch_kernel,
        grid_spec=pltpu.PrefetchScalarGridSpec(
            num_scalar_prefetch=1,
            grid=(n,),
            in_specs=[
                pl.BlockSpec((1, 1, TM_MOVE), lambda i, *_: (i, 0, 0), memory_space=pltpu.SMEM),
                pl.BlockSpec(memory_space=pl.ANY),
            ],
            out_specs=pl.BlockSpec(memory_space=pl.ANY),
            scratch_shapes=[pltpu.VMEM((TM_MOE, X_ROW), f32),
                            pltpu.VMEM((3, GRP_MOVE, SUBLANES, X_ROW), f32),
                            pltpu.SemaphoreType.DMA((3,)), pltpu.SemaphoreType.DMA(()),
                            pltpu.SemaphoreType.DMA((3,))],
        ),
        out_shape=jax.ShapeDtypeStruct((n_rows, X_ROW), f32),
        compiler_params=_cparams(1),
        name="dispatch",
    )(pad_plan, pos, _grouped(x1))


def _expert_kernel(ta_ref, tb_ref, ca_ref, cb_ref, sa_ref, sb_ref, nu_ref,
                   x_ref, wg_hbm, wu_hbm, wd_hbm, g2_ref, b2_ref, y_ref,
                   wga, wua, wda, wgb, wub, wdb, sem_a, sem_b):
    i = pl.program_id(0)
    n_used = nu_ref[0]
    first = i * MOE_TILES
    side_a = (ta_ref, ca_ref, sa_ref, (wga, wua, wda), sem_a)
    side_b = (tb_ref, cb_ref, sb_ref, (wgb, wub, wdb), sem_b)

    def copies(side, j):
        t_ref, _, s_ref, bufs, sem = side
        e, s = t_ref[j], s_ref[j]
        return [pltpu.make_async_copy(w.at[e], buf.at[s], sem.at[s])
                for w, buf in zip((wg_hbm, wu_hbm, wd_hbm), bufs)]

    def fetch(j):
        @pl.when(j < n_used)
        def _():
            for side in (side_a, side_b):
                @pl.when(side[1][j] == 1)
                def _():
                    for cp in copies(side, j):
                        cp.start()

    @pl.when(i == 0)
    def _():
        for j in range(W_AHEAD):
            fetch(j)

    for t in range(MOE_TILES):
        fetch(first + W_AHEAD + t)

    @pl.when(first < n_used)
    def _():
        for t in range(MOE_TILES):
            for side in (side_a, side_b):
                @pl.when(side[1][first + t] == 1)
                def _():
                    for cp in copies(side, first + t):
                        cp.wait()
        rows = [slice(t * TM_MOE, (t + 1) * TM_MOE) for t in range(MOE_TILES)]
        x = [x_ref[r, 0:D_MODEL] for r in rows]
        info = [x_ref[r, D_MODEL:X_ROW] for r in rows]
        xb = [v.astype(bf16) for v in x]
        units = [(t, bufs, s_ref[first + t]) for t in range(MOE_TILES)
                 for bufs, s_ref in (((wga, wua, wda), sa_ref), ((wgb, wub, wdb), sb_ref))]
        hg = [jnp.dot(xb[t], bufs[0][s], preferred_element_type=f32) for t, bufs, s in units]
        hu = [jnp.dot(xb[t], bufs[1][s], preferred_element_type=f32) for t, bufs, s in units]
        mid = [(g * jax.nn.sigmoid(g) * u).astype(bf16) for g, u in zip(hg, hu)]
        ys = [jnp.dot(m, bufs[2][s], preferred_element_type=f32)
              for m, (t, bufs, s) in zip(mid, units)]
        for t in range(MOE_TILES):
            ffn = info[t][:, 2:3] * ys[2 * t] + info[t][:, 3:4] * ys[2 * t + 1]
            y_ref[rows[t], :] = _layer_norm(ALPHA * x[t] + ffn, g2_ref[...], b2_ref[...])

    @pl.when(first >= n_used)
    def _():
        y_ref[...] = jnp.zeros_like(y_ref)


def _experts(tile_ea, tile_eb, n_used, xs, wg, wu, wd, g2, b2):
    nt = xs.shape[0] // TM_MOE

    def ring_plan(tile_e):
        opens = jnp.concatenate([jnp.ones((1,), jnp.int32), (tile_e[1:] != tile_e[:-1]).astype(jnp.int32)])
        return opens, (jnp.cumsum(opens) - 1) % W_SLOTS

    open_a, slot_a = ring_plan(tile_ea)
    open_b, slot_b = ring_plan(tile_eb)
    assert nt % MOE_TILES == 0
    n_used = ((n_used + MOE_TILES - 1) // MOE_TILES) * MOE_TILES
    const = lambda i, *_: (0, 0)
    xmap = lambda i, ta, tb, ca, cb, sa, sb, nu: (jnp.minimum(i, nu[0] // MOE_TILES - 1), 0)
    w_in = pltpu.VMEM((W_SLOTS, D_MODEL, D_EXPERT), bf16)
    w_out = pltpu.VMEM((W_SLOTS, D_EXPERT, D_MODEL), bf16)
    return pl.pallas_call(
        _expert_kernel,
        grid_spec=pltpu.PrefetchScalarGridSpec(
            num_scalar_prefetch=7,
            grid=(nt // MOE_TILES,),
            in_specs=[
                pl.BlockSpec((MOE_TILES * TM_MOE, X_ROW), xmap),
                pl.BlockSpec(memory_space=pl.ANY), pl.BlockSpec(memory_space=pl.ANY),
                pl.BlockSpec(memory_space=pl.ANY),
                pl.BlockSpec((1, D_MODEL), const), pl.BlockSpec((1, D_MODEL), const),
            ],
            out_specs=pl.BlockSpec((MOE_TILES * TM_MOE, D_MODEL), lambda i, *_: (i, 0)),
            scratch_shapes=[w_in, w_in, w_out, w_in, w_in, w_out,
                            pltpu.SemaphoreType.DMA((W_SLOTS,)), pltpu.SemaphoreType.DMA((W_SLOTS,))],
        ),
        out_shape=jax.ShapeDtypeStruct((nt * TM_MOE, D_MODEL), f32),
        compiler_params=_cparams(1),
        name="experts",
    )(tile_ea, tile_eb, open_a, open_b, slot_a.astype(jnp.int32), slot_b.astype(jnp.int32), n_used,
      xs, wg, wu, wd, g2, b2)


def _collect_kernel(pos_ref, posn_ref, ys_ref, op_ref, os_ref, ybuf, sem, *, n_p):
    i = pl.program_id(0)
    n = pl.num_programs(0)
    slot = i % 2

    def gather(p_ref, s):
        def start(k, j, r):
            pltpu.make_async_copy(ys_ref.at[pl.ds(p_ref[0, 0, r], 1)], ybuf.at[s, k, pl.ds(j, 1)],
                                  sem.at[s]).start()

        _for_each_tile_row(start)

    @pl.when(i == 0)
    def _():
        gather(pos_ref, 0)

    @pl.when(i + 1 < n)
    def _():
        gather(posn_ref, 1 - slot)

    _tile_rows_wait(ys_ref, sem.at[slot])

    @pl.when(i < n_p)
    def _():
        op_ref[...] = ybuf[slot].reshape(TM_MOVE, D_MODEL)

    @pl.when(i >= n_p)
    def _():
        os_ref[...] = ybuf[slot].reshape(TM_MOVE, D_MODEL)


def _collect(pos, ys, n_p):
    n = pos.shape[0]
    n_s = n - n_p
    return pl.pallas_call(
        functools.partial(_collect_kernel, n_p=n_p),
        grid=(n,),
        in_specs=[
            pl.BlockSpec((1, 1, TM_MOVE), lambda i: (i, 0, 0), memory_space=pltpu.SMEM),
            pl.BlockSpec((1, 1, TM_MOVE), lambda i: (jnp.minimum(i + 1, n - 1), 0, 0),
                         memory_space=pltpu.SMEM),
            pl.BlockSpec(memory_space=pl.ANY),
        ],
        out_specs=[
            pl.BlockSpec((TM_MOVE, D_MODEL), lambda i: (jnp.minimum(i, n_p - 1), 0)),
            pl.BlockSpec((TM_MOVE, D_MODEL), lambda i: (jnp.maximum(i - n_p, 0), 0)),
        ],
        out_shape=[
            jax.ShapeDtypeStruct((n_p * TM_MOVE, D_MODEL), f32),
            jax.ShapeDtypeStruct((n_s * TM_MOVE, D_MODEL), f32),
        ],
        scratch_shapes=[pltpu.VMEM((2, GRP_MOVE, SUBLANES, D_MODEL), f32), pltpu.SemaphoreType.DMA((2,))],
        compiler_params=_cparams(1),
        name="collect",
    )(pos, pos, ys)


def _block_diag(w):
    n, d, _ = w.shape
    eye = jnp.eye(n, dtype=w.dtype)
    return (eye[:, None, :, None] * w[:, :, None, :]).reshape(n * d, n * d)


def _pair_walk(n):
    total = n * (n - 1) // 2

    def extend(path, used):
        if len(path) == total:
            return path
        a, b = path[-1]
        for c in range(n):
            for nxt in ((a, c), (c, b)):
                key = frozenset(nxt)
                if len(key) == 2 and key not in used:
                    out = extend(path + [nxt], used | {key})
                    if out:
                        return out
        return None

    return extend([(0, 1)], {frozenset((0, 1))})


def _pad_history(buf):
    return jnp.pad(buf, ((0, 0), (SUBLANES - (CONV_W - 1), 0), (0, 0)))


def kernel(x_prompt, x_sample, state_rg_h, state_rg_conv, state_m_C, state_m_n, state_m_m, state_m_conv, w_in, rg_conv_w, rg_conv_b, rg_w_a, rg_b_a, rg_w_x, rg_b_x, rg_lambda, m_conv_w, m_conv_b, m_w_q, m_w_k, m_w_v, m_b_i, m_b_f, m_norm_g, m_skip, w_out, ln1_g, ln1_b, ln2_g, ln2_b, moe_w_group, moe_b_group, moe_w_expert, moe_b_expert, moe_w_gate, moe_w_up, moe_w_down):
    BP, TP, _ = x_prompt.shape
    BS, TS, _ = x_sample.shape
    n_prompt, n_sample = BP * TP, BS * TS
    xp = x_prompt.reshape(n_prompt, D_MODEL)
    xs = x_sample.reshape(n_sample, D_MODEL)
    l = 0

    w_in_p = jnp.pad(w_in[l], ((0, 0), (0, N_PROJ - w_in.shape[-1]))).astype(bf16)
    wa = _block_diag(rg_w_a[l]).astype(bf16)
    wx = _block_diag(rg_w_x[l]).astype(bf16)
    row = lambda v: v.reshape(1, -1)
    bif = jnp.pad(jnp.concatenate([m_b_i[l], m_b_f[l]]), (0, LANES - 2 * M_HEADS)).reshape(1, LANES)
    wqk = jnp.concatenate([m_w_q[l], m_w_k[l]], axis=-1).astype(bf16)
    wv = m_w_v[l].astype(bf16)
    wkT = m_w_k[l].transpose(0, 2, 1).astype(bf16)
    w_route = jnp.pad(jnp.concatenate([moe_w_group[l], moe_w_expert[l]], axis=1),
                      ((0, 0), (0, LANES - N_GROUPS - N_EXPERTS)))
    wr_hi = w_route.astype(bf16)
    wr_lo = (w_route - wr_hi.astype(f32)).astype(bf16)
    wr = jnp.concatenate([wr_hi, wr_lo], axis=1)
    b_route = jnp.pad(jnp.concatenate([moe_b_group[l], moe_b_expert[l]]),
                      (0, LANES - N_GROUPS - N_EXPERTS)).reshape(1, LANES)

    proj_p, proj_s = _inproj(xp, xs, w_in_p)

    rg_args = (rg_conv_w[l], row(rg_conv_b[l]), wa, row(rg_b_a[l]), wx, row(rg_b_x[l]), row(rg_lambda[l]))
    zeros = lambda *s: jnp.zeros(s, f32)
    s_blk = M_ROWS // TS
    yrg_p, p_rg_h, p_rg_conv = _rglru(proj_p, 0, BP, TP, 1, RG_ROWS, zeros(BP, SUBLANES, D_RG),
                                      zeros(BP, 1, D_RG), *rg_args)
    yrg_s, s_rg_h, s_rg_conv = _rglru(proj_s, 0, BS, TS, RG_ROWS // TS, TS,
                                      _pad_history(state_rg_conv[l]), state_rg_h[l].reshape(BS, 1, D_RG), *rg_args)

    m_args = (m_conv_w[l], row(m_conv_b[l]), wqk, wv, wkT, bif, row(m_norm_g[l]), row(m_skip[l]))
    m0_p = jnp.full((SUBLANES, BP * LANES), M_INIT, f32)
    ym_p, p_m_C, p_m_n, p_m_m, p_m_conv, wg_b, wu_b, wd_b = _mlstm(
        proj_p, BP, TP, 1, M_ROWS, M_BLOCKS_PROMPT, zeros(BP, SUBLANES, D_M), zeros(BP, M_HEADS, M_DH, M_DH),
        zeros(BP, M_HEADS, 1, M_DH), m0_p, *m_args, riders=(moe_w_gate[l], moe_w_up[l], moe_w_down[l]))
    m0_s = jnp.pad(jnp.repeat(state_m_m[l].T, TS, axis=1), ((0, SUBLANES - M_HEADS), (0, 0)))
    ym_s, s_m_C, s_m_n, s_m_m, s_m_conv = _mlstm(
        proj_s, BS, TS, s_blk, TS, M_BLOCKS_SAMPLE, _pad_history(state_m_conv[l]), state_m_C[l],
        state_m_n[l].reshape(BS, M_HEADS, 1, M_DH), m0_s, *m_args)

    x1, cnt = _outproj(xp, xs, yrg_p, yrg_s, ym_p, ym_s, w_out[l].astype(bf16),
                       row(ln1_g[l]), row(ln1_b[l]), wr, b_route)

    n_tok = n_prompt + n_sample
    n_tiles = n_tok // TM_MOE + N_CLASSES
    lay_cls, lay_a, lay_b = [], [], []
    for g in range(N_GROUPS):
        for ja, jb in _pair_walk(EXPERTS_PER_GROUP):
            assert ja < jb
            n_first = (EXPERTS_PER_GROUP - 1) * ja - ja * (ja - 1) // 2
            lay_cls.append(PAIRS_PER_GROUP * g + n_first + jb - ja - 1)
            lay_a.append(g * EXPERTS_PER_GROUP + ja)
            lay_b.append(g * EXPERTS_PER_GROUP + jb)
    classes = jnp.arange(N_CLASSES, dtype=jnp.int32)
    in_slot = jnp.array(lay_cls, jnp.int32)[:, None] == classes
    counts = jnp.sum(jnp.where(in_slot, cnt[0, :N_CLASSES].astype(jnp.int32), 0), axis=1)
    padded = ((counts + TM_MOE - 1) // TM_MOE) * TM_MOE
    ends = jnp.cumsum(padded)
    offs = ends - padded
    offs_of_cls = jnp.sum(jnp.where(in_slot, offs[:, None], 0), axis=0)
    cls = x1[:, D_MODEL].astype(jnp.int32)
    rank = x1[:, D_MODEL + 1].astype(jnp.int32)
    pos = jnp.sum(jnp.where(cls[:, None] == classes, offs_of_cls, 0), axis=-1) + rank
    pos = pos.reshape(n_tok // TM_MOVE, 1, TM_MOVE)
    tiles = jnp.arange(n_tiles, dtype=jnp.int32)
    tile_slot = jnp.minimum(jnp.sum(tiles[:, None] >= (ends // TM_MOE)[None, :], axis=1), N_CLASSES - 1)
    on_slot = tile_slot[:, None] == classes
    tile_ea = jnp.sum(jnp.where(on_slot, jnp.array(lay_a, jnp.int32), 0), axis=-1).astype(jnp.int32)
    tile_eb = jnp.sum(jnp.where(on_slot, jnp.array(lay_b, jnp.int32), 0), axis=-1).astype(jnp.int32)
    n_used = (ends[-1] // TM_MOE).reshape(1).astype(jnp.int32)
    pad_row0 = ((offs + counts) // SUBLANES) * SUBLANES
    pad_plan = jnp.concatenate([jnp.stack([pad_row0, (ends - pad_row0) // SUBLANES], axis=1).reshape(-1),
                                     n_used]).astype(jnp.int32)

    x_sorted = _dispatch(pad_plan, pos, x1, n_tiles * TM_MOE)
    y_sorted = _experts(tile_ea, tile_eb, n_used, x_sorted, wg_b, wu_b, wd_b,
                        row(ln2_g[l]), row(ln2_b[l]))
    y_p, y_s = _collect(pos, y_sorted, n_prompt // TM_MOVE)

    return (y_p.reshape(BP, TP, D_MODEL), y_s.reshape(BS, TS, D_MODEL),
            p_rg_h.reshape(1, BP, D_RG), p_rg_conv[None], p_m_C[None], p_m_n.reshape(1, BP, M_HEADS, M_DH),
            p_m_m[None], p_m_conv[None],
            s_rg_h.reshape(1, BS, D_RG), s_rg_conv[None], s_m_C[None], s_m_n.reshape(1, BS, M_HEADS, M_DH),
            s_m_m[None], s_m_conv[None])
```

```python
import functools

import jax
import jax.numpy as jnp
from jax import lax
from jax.experimental import pallas as pl
from jax.experimental.pallas import tpu as pltpu

f32 = jnp.float32
bf16 = jnp.bfloat16

D_MODEL = 1024
D_RG = 512
RG_C = 8.0
D_M = 512
M_HEADS = 4
M_DH = 128
CONV_W = 4
N_GROUPS = 4
EXPERTS_PER_GROUP = 8
N_EXPERTS = 32
D_EXPERT = 256
ALPHA = 2.0 ** 0.25
LN_EPS = 1e-5
M_INIT = -1.0e4

LANES = 128
SUBLANES = 8
TM = 512
IN_RING = 3
TM_MOVE = 1024
OUT_PARTS = 4
TM_MOE = 128
MOE_TILES = 4
W_AHEAD = 8
W_SLOTS = W_AHEAD + MOE_TILES
PAIRS_PER_GROUP = EXPERTS_PER_GROUP * (EXPERTS_PER_GROUP - 1) // 2
N_CLASSES = N_GROUPS * PAIRS_PER_GROUP
X_ROW = D_MODEL + LANES
N_PROJ = 2 * D_RG + 2 * D_M + LANES
RG_ROWS = 512
M_ROWS = 128
M_BLOCKS_PROMPT = 8
M_BLOCKS_SAMPLE = 2
V7X_VMEM_BYTES = 64 * 1024 * 1024
VMEM_LIMIT = V7X_VMEM_BYTES - 8 * 1024 * 1024


def _cparams(n_axes):
    return pltpu.CompilerParams(dimension_semantics=("arbitrary",) * n_axes,
                                vmem_limit_bytes=VMEM_LIMIT)


def _inproj_kernel(xp_ref, xs_ref, w_ref, op_ref, os_ref, xbuf, sem, *, n_p):
    i = pl.program_id(0)
    n = pl.num_programs(0)

    def tile_copy(t, src_ref, src_tile):
        rows = pl.ds(pl.multiple_of(src_tile * TM, TM), TM)
        return pltpu.make_async_copy(src_ref.at[rows], xbuf.at[t % IN_RING], sem.at[t % IN_RING])

    def fetch(t):
        @pl.when(t < n_p)
        def _():
            tile_copy(t, xp_ref, t).start()

        @pl.when((t >= n_p) & (t < n))
        def _():
            tile_copy(t, xs_ref, t - n_p).start()

    @pl.when(i == 0)
    def _():
        for t in range(IN_RING - 1):
            fetch(jnp.int32(t))

    fetch(i + IN_RING - 1)
    tile_copy(i, xp_ref, 0).wait()
    y = jnp.dot(xbuf[i % IN_RING].astype(bf16), w_ref[...], preferred_element_type=f32)

    @pl.when(i < n_p)
    def _():
        op_ref[...] = y

    @pl.when(i >= n_p)
    def _():
        os_ref[...] = y


def _inproj(xp, xs, w):
    n_p, n_s = xp.shape[0] // TM, xs.shape[0] // TM
    pmap = lambda i: (jnp.minimum(i, n_p - 1), 0)
    smap = lambda i: (jnp.maximum(i - n_p, 0), 0)
    return pl.pallas_call(
        functools.partial(_inproj_kernel, n_p=n_p),
        grid=(n_p + n_s,),
        in_specs=[
            pl.BlockSpec(memory_space=pl.ANY),
            pl.BlockSpec(memory_space=pl.ANY),
            pl.BlockSpec((D_MODEL, N_PROJ), lambda i: (0, 0)),
        ],
        out_specs=[pl.BlockSpec((TM, N_PROJ), pmap), pl.BlockSpec((TM, N_PROJ), smap)],
        out_shape=[jax.ShapeDtypeStruct((n_p * TM, N_PROJ), f32),
                   jax.ShapeDtypeStruct((n_s * TM, N_PROJ), f32)],
        scratch_shapes=[pltpu.VMEM((IN_RING, TM, D_MODEL), f32), pltpu.SemaphoreType.DMA((IN_RING,))],
        compiler_params=_cparams(1),
        name="inproj",
    )(xp, xs, w)


def _causal_conv(tail_scr, x, cw_ref, cb_ref, S, L, C):
    tail = tail_scr[...]
    sub = lax.broadcasted_iota(jnp.int32, (S, SUBLANES, C), 1)
    acc = cb_ref[...] + cw_ref[CONV_W - 1:CONV_W, :] * x
    for d in range(1, CONV_W):
        back = pltpu.roll(x, d, 0).reshape(S, L, C)
        head = jnp.where(sub < d, pltpu.roll(tail, d, 1), back[:, 0:SUBLANES, :])
        if L > SUBLANES:
            back = jnp.concatenate([head, back[:, SUBLANES:, :]], axis=1)
        else:
            back = head
        acc = acc + cw_ref[CONV_W - 1 - d:CONV_W - d, :] * back.reshape(S * L, C)
    tail_scr[...] = x.reshape(S, L, C)[:, L - SUBLANES:, :]
    return acc


def _softplus(x):
    return jnp.maximum(x, 0.0) + jnp.log1p(jnp.exp(-jnp.abs(x)))


def _rglru_kernel(x_ref, g_ref, buf_ref, h0_ref, cw_ref, cb_ref, wa_ref, ba_ref, wx_ref, bx_ref,
                  lam_ref, y_ref, hN_ref, cN_ref, xs_scr, *, S, L):
    R = S * L
    t = pl.program_id(1)

    @pl.when(t == 0)
    def _():
        xs_scr[...] = buf_ref[...]
        hN_ref[...] = h0_ref[...]

    x = x_ref[...]
    xc = _causal_conv(xs_scr, x, cw_ref, cb_ref, S, L, D_RG)
    xcb = xc.astype(bf16)
    r = jax.nn.sigmoid(jnp.dot(xcb, wa_ref[...], preferred_element_type=f32) + ba_ref[...])
    ig = jax.nn.sigmoid(jnp.dot(xcb, wx_ref[...], preferred_element_type=f32) + bx_ref[...])
    log_a = (-RG_C) * r * _softplus(-lam_ref[...])
    a = jnp.exp(log_a)
    th = jnp.tanh(log_a)
    u = jnp.sqrt(-2.0 * th / (1.0 - th)) * ig * xc

    n_grp, grp_per_seq = R // SUBLANES, L // SUBLANES
    a3 = a.reshape(n_grp, SUBLANES, D_RG)
    u3 = u.reshape(n_grp, SUBLANES, D_RG)
    sub = lax.broadcasted_iota(jnp.int32, (n_grp, SUBLANES, D_RG), 1)
    s = 1
    while s < SUBLANES:
        ok = sub >= s
        a_sh = pltpu.roll(a3, s, 1)
        u_sh = pltpu.roll(u3, s, 1)
        u3 = jnp.where(ok, a3 * u_sh + u3, u3)
        a3 = jnp.where(ok, a3 * a_sh, a3)
        s *= 2
    h0 = hN_ref[...]
    groups = []
    for kg in range(n_grp):
        carry = h0[kg // grp_per_seq] if kg % grp_per_seq == 0 else groups[-1][SUBLANES - 1:SUBLANES, :]
        groups.append(a3[kg] * carry + u3[kg])
    h = jnp.concatenate(groups, axis=0)

    y_ref[...] = h * jax.nn.gelu(g_ref[...], approximate=True)
    hN_ref[...] = h.reshape(S, L, D_RG)[:, L - 1:L, :]
    cN_ref[...] = xs_scr[:, SUBLANES - (CONV_W - 1):SUBLANES, :]


def _rglru(proj, row0, B, T, S, L, buf8, h0, cw, cb, wa, ba, wx, bx, lam):
    R = S * L
    nb, nt = B // S, T // L
    blk0 = row0 // R
    row_map = lambda b, t: (blk0 + b * nt + t, 0)
    const2 = lambda b, t: (0, 0)
    return pl.pallas_call(
        functools.partial(_rglru_kernel, S=S, L=L),
        grid=(nb, nt),
        in_specs=[
            pl.BlockSpec((R, D_RG), row_map),
            pl.BlockSpec((R, D_RG), lambda b, t: (blk0 + b * nt + t, 1)),
            pl.BlockSpec((S, SUBLANES, D_RG), lambda b, t: (b, 0, 0)),
            pl.BlockSpec((S, 1, D_RG), lambda b, t: (b, 0, 0)),
            pl.BlockSpec((CONV_W, D_RG), const2),
            pl.BlockSpec((1, D_RG), const2),
            pl.BlockSpec((D_RG, D_RG), const2),
            pl.BlockSpec((1, D_RG), const2),
            pl.BlockSpec((D_RG, D_RG), const2),
            pl.BlockSpec((1, D_RG), const2),
            pl.BlockSpec((1, D_RG), const2),
        ],
        out_specs=[
            pl.BlockSpec((R, D_RG), lambda b, t: (b * nt + t, 0)),
            pl.BlockSpec((S, 1, D_RG), lambda b, t: (b, 0, 0)),
            pl.BlockSpec((S, CONV_W - 1, D_RG), lambda b, t: (b, 0, 0)),
        ],
        out_shape=[
            jax.ShapeDtypeStruct((B * T, D_RG), f32),
            jax.ShapeDtypeStruct((B, 1, D_RG), f32),
            jax.ShapeDtypeStruct((B, CONV_W - 1, D_RG), f32),
        ],
        scratch_shapes=[pltpu.VMEM((S, SUBLANES, D_RG), f32)],
        compiler_params=_cparams(2),
        name="rglru",
    )(proj, proj, buf8, h0, cw, cb, wa, ba, wx, bx, lam)


def _seg_scan(x, op, fill, tin, L, reverse=False):
    s = 1
    while s < L:
        if reverse:
            sh = pltpu.roll(x, LANES - s, 1)
            ok = tin < L - s
        else:
            sh = pltpu.roll(x, s, 1)
            ok = tin >= s
        x = op(x, jnp.where(ok, sh, fill))
        s *= 2
    return x


def _mlstm_kernel(*refs, S, L, G, n_riders):
    (x_ref, z_ref, gt_ref, buf_ref, c0_ref, n0_ref, m0_ref,
     cw_ref, cb_ref, wqk_ref, wv_ref, wkT_ref, bif_ref, ng_ref, sk_ref) = refs[:15]
    rider_in = refs[15:15 + n_riders]
    y_ref, cN_ref, nN_ref, mN_ref, bN_ref = refs[15 + n_riders:20 + n_riders]
    rider_out = refs[20 + n_riders:20 + 2 * n_riders]
    xs_scr, m_scr = refs[20 + 2 * n_riders:]
    for src, dst in zip(rider_in, rider_out):
        dst[...] = src[...].astype(bf16)

    R = S * L
    GS, GR = G * S, G * R
    j = pl.program_id(1)

    @pl.when(j == 0)
    def _():
        xs_scr[...] = buf_ref[...]
        cN_ref[...] = c0_ref[...]
        nN_ref[...] = n0_ref[...]
        m_scr[...] = m0_ref[...]

    x = x_ref[...].reshape(GR, D_M)
    xc = _causal_conv(xs_scr, x, cw_ref, cb_ref, GS, L, D_M)
    xa = xc * jax.nn.sigmoid(xc)
    xab = xa.astype(bf16)
    xb = x.astype(bf16)

    il_parts, f_parts = [], []
    for g in range(G):
        gT = (gt_ref[g] + bif_ref[...]).T
        il_parts.append(gT[0:SUBLANES, :])
        f_parts.append(pltpu.roll(gT[0:SUBLANES, :], M_HEADS, 0))
    il = jnp.concatenate(il_parts, axis=0)
    fl = -_softplus(-jnp.concatenate(f_parts, axis=0))
    tin = lax.broadcasted_iota(jnp.int32, (G * SUBLANES, LANES), 1) % L
    bcum = _seg_scan(fl, jnp.add, 0.0, tin, L)
    a = il - bcum
    m_prev = m_scr[...].reshape(G * SUBLANES, LANES)
    big_m = jnp.maximum(m_prev, _seg_scan(a, jnp.maximum, -jnp.inf, tin, L))
    m_t = bcum + big_m
    if S == 1:
        m_last = jnp.broadcast_to(big_m[:, LANES - 1:LANES], big_m.shape)
    else:
        m_last = _seg_scan(big_m, jnp.maximum, -jnp.inf, tin, L, reverse=True)
    rows = [big_m, jnp.exp(m_prev - big_m), jnp.exp(-m_t), jnp.exp(a - m_last), jnp.exp(m_prev - m_last)]
    mN_ref[...] = m_t.reshape(G, SUBLANES, LANES)
    m_scr[...] = jnp.broadcast_to(m_t[:, LANES - 1:LANES], m_t.shape).reshape(G, SUBLANES, LANES)
    pad_rows = jnp.zeros((LANES - len(rows) * SUBLANES, LANES), f32)
    cols = []
    for g in range(G):
        gs = slice(g * SUBLANES, (g + 1) * SUBLANES)
        cols.append(jnp.concatenate([r[gs] for r in rows] + [pad_rows], axis=0).T)

    def col(g, q, h):
        return cols[g][:, SUBLANES * q + h:SUBLANES * q + h + 1]

    qs, ks, vs, kTs = [], [], [], []
    for h in range(M_HEADS):
        hs = slice(h * M_DH, (h + 1) * M_DH)
        qk_h = jnp.dot(xab[:, hs], wqk_ref[h], preferred_element_type=f32)
        qs.append(qk_h[:, 0:M_DH])
        ks.append(qk_h[:, M_DH:2 * M_DH] * (M_DH ** -0.5))
        vs.append(jnp.dot(xb[:, hs], wv_ref[h], preferred_element_type=f32))
        if S == 1:
            kT_h = lax.dot_general(wkT_ref[h], xab[:, hs], (((1,), (1,)), ((), ())),
                                   preferred_element_type=f32)
            kTs.append((kT_h * (M_DH ** -0.5)).astype(bf16))

    ti = lax.broadcasted_iota(jnp.int32, (R, R), 0)
    si = lax.broadcasted_iota(jnp.int32, (R, R), 1)
    mask = (si <= ti) & ((ti // L) == (si // L))
    ones_b = jnp.ones((R, M_DH), bf16)
    pairs = [(g, h) for g in range(G) for h in range(M_HEADS)]
    blk = lambda g: slice(g * R, (g + 1) * R)
    seqs = [slice(b * L, (b + 1) * L) for b in range(S)]
    q = {(g, h): qs[h][blk(g)] for g, h in pairs}
    k = {(g, h): ks[h][blk(g)] for g, h in pairs}
    v = {(g, h): vs[h][blk(g)] for g, h in pairs}
    qb = {p: q[p].astype(bf16) for p in pairs}
    kb = {p: k[p].astype(bf16) for p in pairs}
    qk = {p: lax.dot_general(qb[p], kb[p], (((1,), (1,)), ((), ())), preferred_element_type=f32)
          for p in pairs}
    sm = {}
    for g, h in pairs:
        a_row = a[g * SUBLANES + h:g * SUBLANES + h + 1, :]
        decay = jnp.exp(jnp.where(mask, a_row - col(g, 0, h), -jnp.inf))
        sm[g, h] = (qk[g, h] * decay).astype(bf16)
    nd = {p: jnp.dot(sm[p], jnp.concatenate([v[p].astype(bf16), ones_b], axis=1),
                     preferred_element_type=f32) for p in pairs}
    c_old = {(g, h, b): cN_ref[g * S + b, h] for g, h in pairs for b in range(S)}
    n_old = {(g, h, b): nN_ref[g * S + b, h] for g, h in pairs for b in range(S)}
    q_c = {(g, h, b): jnp.dot(qb[g, h][seqs[b]], c_old[g, h, b].astype(bf16), preferred_element_type=f32)
           for g, h in pairs for b in range(S)}
    hh = {}
    for g, h in pairs:
        g_col, e_col = col(g, 1, h), col(g, 2, h)
        pieces = []
        for b, rs in enumerate(seqs):
            q_n = jnp.sum(q[g, h][rs] * n_old[g, h, b], axis=1, keepdims=True)
            num = nd[g, h][rs, 0:M_DH] + g_col[rs] * q_c[g, h, b]
            den = nd[g, h][rs, M_DH:2 * M_DH] + g_col[rs] * q_n
            pieces.append(num / jnp.maximum(jnp.abs(den), e_col[rs]))
        hh[g, h] = pieces[0] if S == 1 else jnp.concatenate(pieces, axis=0)
    mu = {p: jnp.mean(hh[p], axis=1, keepdims=True) for p in pairs}
    var = {p: jnp.mean(jnp.square(hh[p] - mu[p]), axis=1, keepdims=True) for p in pairs}
    hn_blocks = [jnp.concatenate([(hh[g, h] - mu[g, h]) * lax.rsqrt(var[g, h] + LN_EPS)
                                  for h in range(M_HEADS)], axis=1) for g in range(G)]
    new_c, new_n = [], []
    for g, h in pairs:
        w_col = col(g, 3, h)
        wv = (w_col * v[g, h]).astype(bf16)
        wk = w_col * k[g, h]
        for b, rs in enumerate(seqs):
            g_end = col(g, 4, h)[(b + 1) * L - 1:(b + 1) * L, :]
            if S == 1:
                kv = jnp.dot(kTs[h][:, blk(g)], wv, preferred_element_type=f32)
            else:
                kv = lax.dot_general(kb[g, h][rs], wv[rs], (((0,), (0,)), ((), ())),
                                     preferred_element_type=f32)
            new_c.append((g * S + b, h, g_end * c_old[g, h, b] + kv))
            new_n.append((g * S + b, h, g_end * n_old[g, h, b] + jnp.sum(wk[rs], axis=0, keepdims=True)))

    hn = jnp.concatenate(hn_blocks, axis=0) * ng_ref[...]
    y = jax.nn.sigmoid(z_ref[...].reshape(GR, D_M)) * (hn + sk_ref[...] * xa)
    y_ref[...] = y.reshape(G, R, D_M)
    for sq, h, val in new_c:
        cN_ref[sq, h] = val
    for sq, h, val in new_n:
        nN_ref[sq, h] = val
    bN_ref[...] = xs_scr[:, SUBLANES - (CONV_W - 1):SUBLANES, :]


def _mlstm(proj, B, T, S, L, G, buf8, c0, n0, m0_lanes, cw, cb, wqk, wv, wkT, bif, ng, sk, riders=()):
    R = S * L
    assert R == M_ROWS == LANES
    nblk, nc = B // S, T // L
    nb = nblk // G
    assert nblk * S == B and nc * L == T and nb * G == nblk and (S == 1 or nc == 1)
    steps = nb * nc
    assert all(r.shape[0] % steps == 0 for r in riders)
    rider_specs = [pl.BlockSpec((r.shape[0] // steps,) + r.shape[1:], lambda b, j: (b * nc + j, 0, 0))
                   for r in riders]
    proj4 = proj.reshape(nblk, nc, R, N_PROJ)
    m0_3 = m0_lanes.reshape(SUBLANES, nblk, LANES).transpose(1, 0, 2)
    const2 = lambda b, j: (0, 0)
    const3 = lambda b, j: (0, 0, 0)
    gate_blk = (2 * D_RG + 2 * D_M) // LANES
    GS = G * S
    outs = pl.pallas_call(
        functools.partial(_mlstm_kernel, S=S, L=L, G=G, n_riders=len(riders)),
        grid=(nb, nc),
        in_specs=[
            pl.BlockSpec((G, None, R, D_M), lambda b, j: (b, j, 0, 2)),
            pl.BlockSpec((G, None, R, D_M), lambda b, j: (b, j, 0, 3)),
            pl.BlockSpec((G, None, R, LANES), lambda b, j: (b, j, 0, gate_blk)),
            pl.BlockSpec((GS, SUBLANES, D_M), lambda b, j: (b, 0, 0)),
            pl.BlockSpec((GS, M_HEADS, M_DH, M_DH), lambda b, j: (b, 0, 0, 0)),
            pl.BlockSpec((GS, M_HEADS, 1, M_DH), lambda b, j: (b, 0, 0, 0)),
            pl.BlockSpec((G, SUBLANES, LANES), lambda b, j: (b, 0, 0)),
            pl.BlockSpec((CONV_W, D_M), const2),
            pl.BlockSpec((1, D_M), const2),
            pl.BlockSpec((M_HEADS, M_DH, 2 * M_DH), const3),
            pl.BlockSpec((M_HEADS, M_DH, M_DH), const3),
            pl.BlockSpec((M_HEADS, M_DH, M_DH), const3),
            pl.BlockSpec((1, LANES), const2),
            pl.BlockSpec((1, D_M), const2),
            pl.BlockSpec((1, D_M), const2),
        ] + rider_specs,
        out_specs=[
            pl.BlockSpec((G, None, R, D_M), lambda b, j: (b, j, 0, 0)),
            pl.BlockSpec((GS, M_HEADS, M_DH, M_DH), lambda b, j: (b, 0, 0, 0)),
            pl.BlockSpec((GS, M_HEADS, 1, M_DH), lambda b, j: (b, 0, 0, 0)),
            pl.BlockSpec((G, None, SUBLANES, LANES), lambda b, j: (b, j, 0, 0)),
            pl.BlockSpec((GS, CONV_W - 1, D_M), lambda b, j: (b, 0, 0)),
        ] + rider_specs,
        out_shape=[
            jax.ShapeDtypeStruct((nblk, nc, R, D_M), f32),
            jax.ShapeDtypeStruct((B, M_HEADS, M_DH, M_DH), f32),
            jax.ShapeDtypeStruct((B, M_HEADS, 1, M_DH), f32),
            jax.ShapeDtypeStruct((nblk, nc, SUBLANES, LANES), f32),
            jax.ShapeDtypeStruct((B, CONV_W - 1, D_M), f32),
        ] + [jax.ShapeDtypeStruct(r.shape, bf16) for r in riders],
        scratch_shapes=[pltpu.VMEM((GS, SUBLANES, D_M), f32), pltpu.VMEM((G, SUBLANES, LANES), f32)],
        compiler_params=_cparams(2),
        name="mlstm",
    )(proj4, proj4, proj4, buf8, c0, n0, m0_3, cw, cb, wqk, wv, wkT, bif, ng, sk, *riders)
    y, c_new, n_new, m_t, b_new = outs[:5]
    m_last = m_t[:, nc - 1, :M_HEADS, :].reshape(nblk, M_HEADS, S, L)[:, :, :, L - 1]
    m_last = m_last.transpose(0, 2, 1).reshape(B, M_HEADS)
    return (y.reshape(B * T, D_M), c_new, n_new, m_last, b_new) + tuple(outs[5:])


def _layer_norm(x, g, b):
    mu = jnp.mean(x, axis=-1, keepdims=True)
    var = jnp.mean(jnp.square(x - mu), axis=-1, keepdims=True)
    return (x - mu) * lax.rsqrt(var + LN_EPS) * g + b


def _first_lane_of_max(vals, lane_f):
    vmax = jnp.max(vals, axis=1, keepdims=True)
    idx = jnp.min(jnp.where(vals == vmax, lane_f, float(LANES)), axis=1, keepdims=True)
    return vmax, idx


def _outproj_kernel(xp_ref, xs_ref, rgp_ref, rgs_ref, mp_ref, ms_ref, wo_ref, g1_ref, b1_ref,
                    wr_ref, br_ref, x1_ref, cnt_ref, *, n_p):
    i = pl.program_id(0)

    @pl.when(i == 0)
    def _():
        cnt_ref[...] = jnp.zeros_like(cnt_ref)

    def route(lg, lane, lane_f):
        neg = -jnp.inf
        gl = jnp.where(lane < N_GROUPS, lg, neg)
        gmax, gidx = _first_lane_of_max(gl, lane_f)
        p_g = 1.0 / jnp.sum(jnp.exp(gl - gmax), axis=1, keepdims=True)
        e_lo = float(N_GROUPS) + float(EXPERTS_PER_GROUP) * gidx
        el = jnp.where((lane_f >= e_lo) & (lane_f < e_lo + float(EXPERTS_PER_GROUP)), lg, neg)
        v1, i1 = _first_lane_of_max(el, lane_f)
        v2, i2 = _first_lane_of_max(jnp.where(lane_f == i1, neg, el), lane_f)
        d = jnp.exp(v2 - v1)
        w1 = p_g / (1.0 + d)
        w2 = p_g * d / (1.0 + d)
        first_low = i1 < i2
        j_lo = jnp.minimum(i1, i2) - e_lo
        j_hi = jnp.maximum(i1, i2) - e_lo
        n_first = float(EXPERTS_PER_GROUP - 1) * j_lo - 0.5 * j_lo * (j_lo - 1.0)
        cls = float(PAIRS_PER_GROUP) * gidx + n_first + (j_hi - j_lo - 1.0)
        return cls, jnp.where(first_low, w1, w2), jnp.where(first_low, w2, w1)

    def run(x_ref, rg_ref, m_ref):
        rp = TM // OUT_PARTS
        parts = [slice(p * rp, (p + 1) * rp) for p in range(OUT_PARTS)]
        lane = lax.broadcasted_iota(jnp.int32, (rp, LANES), 1)
        lane_f = lane.astype(f32)
        heads = [jnp.concatenate([rg_ref[r, :].astype(bf16), m_ref[r, :].astype(bf16)], axis=1) for r in parts]
        mix = [jnp.dot(h, wo_ref[...], preferred_element_type=f32) for h in heads]
        x1 = [_layer_norm(ALPHA * x_ref[r, :] + mx, g1_ref[...], b1_ref[...]) for r, mx in zip(parts, mix)]
        for r, v in zip(parts, x1):
            x1_ref[r, 0:D_MODEL] = v

        hi = [v.astype(bf16) for v in x1]
        lo = [(v - h.astype(f32)).astype(bf16) for v, h in zip(x1, hi)]
        hi_terms = [jnp.dot(h, wr_ref[...], preferred_element_type=f32) for h in hi]
        lo_term = [jnp.dot(v, wr_ref[:, 0:LANES], preferred_element_type=f32) for v in lo]
        lg = [ht[:, 0:LANES] + ht[:, LANES:2 * LANES] + lt + br_ref[...] for ht, lt in zip(hi_terms, lo_term)]
        routed = [route(v, lane, lane_f) for v in lg]

        ti = lax.broadcasted_iota(jnp.int32, (rp, rp), 0)
        si = lax.broadcasted_iota(jnp.int32, (rp, rp), 1)
        tri = (si <= ti).astype(bf16)
        hot = [lane_f == cls for cls, _, _ in routed]
        cum = [jnp.dot(tri, h.astype(bf16), preferred_element_type=f32) for h in hot]
        seen = cnt_ref[...]
        for r, h, cm, (cls, w_lo, w_hi) in zip(parts, hot, cum, routed):
            rank = jnp.sum(jnp.where(h, cm - 1.0 + seen, 0.0), axis=1, keepdims=True)
            seen = seen + cm[rp - 1:rp, :]
            info = jnp.zeros((rp, LANES), f32)
            for c, val in enumerate((cls, rank, w_lo, w_hi)):
                info = jnp.where(lane == c, val, info)
            x1_ref[r, D_MODEL:D_MODEL + LANES] = info
        cnt_ref[...] = seen

    @pl.when(i < n_p)
    def _():
        run(xp_ref, rgp_ref, mp_ref)

    @pl.when(i >= n_p)
    def _():
        run(xs_ref, rgs_ref, ms_ref)


def _outproj(xp, xs, rgp, rgs, mp, ms, wo, g1, b1, wr, br):
    n_p, n_s = xp.shape[0] // TM, xs.shape[0] // TM
    n = n_p + n_s
    pmap = lambda i: (jnp.minimum(i, n_p - 1), 0)
    smap = lambda i: (jnp.maximum(i - n_p, 0), 0)
    const = lambda i: (0, 0)
    return pl.pallas_call(
        functools.partial(_outproj_kernel, n_p=n_p),
        grid=(n,),
        in_specs=[
            pl.BlockSpec((TM, D_MODEL), pmap), pl.BlockSpec((TM, D_MODEL), smap),
            pl.BlockSpec((TM, D_RG), pmap), pl.BlockSpec((TM, D_RG), smap),
            pl.BlockSpec((TM, D_M), pmap), pl.BlockSpec((TM, D_M), smap),
            pl.BlockSpec((D_RG + D_M, D_MODEL), const),
            pl.BlockSpec((1, D_MODEL), const), pl.BlockSpec((1, D_MODEL), const),
            pl.BlockSpec((D_MODEL, 2 * LANES), const),
            pl.BlockSpec((1, LANES), const),
        ],
        out_specs=[
            pl.BlockSpec((TM, X_ROW), lambda i: (i, 0)),
            pl.BlockSpec((1, LANES), const),
        ],
        out_shape=[
            jax.ShapeDtypeStruct((n * TM, X_ROW), f32),
            jax.ShapeDtypeStruct((1, LANES), f32),
        ],
        compiler_params=_cparams(1),
        name="outproj",
    )(xp, xs, rgp, rgs, mp, ms, wo, g1, b1, wr, br)


GRP_MOVE = TM_MOVE // SUBLANES


def _grouped(x):
    return x.reshape(x.shape[0] // SUBLANES, SUBLANES, x.shape[1])


def _tile_rows_wait(hbm_ref, sem):
    rows = hbm_ref.at[pl.ds(0, TM_MOVE)]
    pltpu.make_async_copy(rows, rows, sem).wait()


def _for_each_tile_row(body):
    def group(k, c):
        for j in range(SUBLANES):
            body(k, j, k * SUBLANES + j)
        return c

    lax.fori_loop(0, GRP_MOVE, group, 0)


def _dispatch_kernel(pad_ref, pos_ref, x1_ref, xs_ref, zero_scr, xbuf, sem, zsem, lsem):
    i = pl.program_id(0)
    n = pl.num_programs(0)

    def zero_tile(t):
        return pltpu.make_async_copy(zero_scr, xs_ref.at[pl.ds(pl.multiple_of(t * TM_MOE, TM_MOE), TM_MOE)], zsem)

    def zero_chunk(row0):
        return pltpu.make_async_copy(zero_scr.at[pl.ds(0, SUBLANES)],
                                     xs_ref.at[pl.ds(pl.multiple_of(row0, SUBLANES), SUBLANES)], zsem)

    @pl.when(i == 0)
    def _():
        zero_scr[...] = jnp.zeros_like(zero_scr)
        n_used = pad_ref[2 * N_CLASSES]
        n_tiles = xs_ref.shape[0] // TM_MOE
        for go in (lambda cp: cp.start(), lambda cp: cp.wait()):
            lax.fori_loop(n_used, n_tiles, lambda t, c: (go(zero_tile(t)), c)[1], 0)

            def per_class(cl, c):
                row0 = pad_ref[2 * cl]
                lax.fori_loop(0, pad_ref[2 * cl + 1],
                              lambda k, c2: (go(zero_chunk(row0 + k * SUBLANES)), c2)[1], 0)
                return c

            lax.fori_loop(0, N_CLASSES, per_class, 0)

    def load(t, s):
        return pltpu.make_async_copy(x1_ref.at[pl.ds(t * GRP_MOVE, GRP_MOVE)], xbuf.at[s], lsem.at[s])

    @pl.when(i == 0)
    def _():
        load(0, 0).start()

    @pl.when(i + 1 < n)
    def _():
        load(i + 1, (i + 1) % 3).start()

    slot = i % 3
    load(i, slot).wait()

    def start(k, j, r):
        pltpu.make_async_copy(xbuf.at[slot, k, pl.ds(j, 1)], xs_ref.at[pl.ds(pos_ref[0, 0, r], 1)],
                              sem.at[slot]).start()

    _for_each_tile_row(start)

    @pl.when(i > 0)
    def _():
        _tile_rows_wait(xs_ref, sem.at[(i + 2) % 3])

    @pl.when(i == n - 1)
    def _():
        _tile_rows_wait(xs_ref, sem.at[slot])


def _dispatch(pad_plan, pos, x1, n_rows):
    n = x1.shape[0] // TM_MOVE
    return pl.pallas_call(
        _dispatch_kernel,
        grid_spec=pltpu.PrefetchScalarGridSpec(
            num_scalar_prefetch=1,
            grid=(n,),
            in_specs=[
                pl.BlockSpec((1, 1, TM_MOVE), lambda i, *_: (i, 0, 0), memory_space=pltpu.SMEM),
                pl.BlockSpec(memory_space=pl.ANY),
            ],
            out_specs=pl.BlockSpec(memory_space=pl.ANY),
            scratch_shapes=[pltpu.VMEM((TM_MOE, X_ROW), f32),
                            pltpu.VMEM((3, GRP_MOVE, SUBLANES, X_ROW), f32),
                            pltpu.SemaphoreType.DMA((3,)), pltpu.SemaphoreType.DMA(()),
                            pltpu.SemaphoreType.DMA((3,))],
        ),
        out_shape=jax.ShapeDtypeStruct((n_rows, X_ROW), f32),
        compiler_params=_cparams(1),
        name="dispatch",
    )(pad_plan, pos, _grouped(x1))


def _expert_kernel(ta_ref, tb_ref, ca_ref, cb_ref, sa_ref, sb_ref, nu_ref,
                   x_ref, wg_hbm, wu_hbm, wd_hbm, g2_ref, b2_ref, y_ref,
                   wga, wua, wda, wgb, wub, wdb, sem_a, sem_b):
    i = pl.program_id(0)
    n_used = nu_ref[0]
    first = i * MOE_TILES
    side_a = (ta_ref, ca_ref, sa_ref, (wga, wua, wda), sem_a)
    side_b = (tb_ref, cb_ref, sb_ref, (wgb, wub, wdb), sem_b)

    def copies(side, j):
        t_ref, _, s_ref, bufs, sem = side
        e, s = t_ref[j], s_ref[j]
        return [pltpu.make_async_copy(w.at[e], buf.at[s], sem.at[s])
                for w, buf in zip((wg_hbm, wu_hbm, wd_hbm), bufs)]

    def fetch(j):
        @pl.when(j < n_used)
        def _():
            for side in (side_a, side_b):
                @pl.when(side[1][j] == 1)
                def _():
                    for cp in copies(side, j):
                        cp.start()

    @pl.when(i == 0)
    def _():
        for j in range(W_AHEAD):
            fetch(j)

    for t in range(MOE_TILES):
        fetch(first + W_AHEAD + t)

    @pl.when(first < n_used)
    def _():
        for t in range(MOE_TILES):
            for side in (side_a, side_b):
                @pl.when(side[1][first + t] == 1)
                def _():
                    for cp in copies(side, first + t):
                        cp.wait()
        rows = [slice(t * TM_MOE, (t + 1) * TM_MOE) for t in range(MOE_TILES)]
        x = [x_ref[r, 0:D_MODEL] for r in rows]
        info = [x_ref[r, D_MODEL:X_ROW] for r in rows]
        xb = [v.astype(bf16) for v in x]
        units = [(t, bufs, s_ref[first + t]) for t in range(MOE_TILES)
                 for bufs, s_ref in (((wga, wua, wda), sa_ref), ((wgb, wub, wdb), sb_ref))]
        hg = [jnp.dot(xb[t], bufs[0][s], preferred_element_type=f32) for t, bufs, s in units]
        hu = [jnp.dot(xb[t], bufs[1][s], preferred_element_type=f32) for t, bufs, s in units]
        mid = [(g * jax.nn.sigmoid(g) * u).astype(bf16) for g, u in zip(hg, hu)]
        ys = [jnp.dot(m, bufs[2][s], preferred_element_type=f32)
              for m, (t, bufs, s) in zip(mid, units)]
        for t in range(MOE_TILES):
            ffn = info[t][:, 2:3] * ys[2 * t] + info[t][:, 3:4] * ys[2 * t + 1]
            y_ref[rows[t], :] = _layer_norm(ALPHA * x[t] + ffn, g2_ref[...], b2_ref[...])

    @pl.when(first >= n_used)
    def _():
        y_ref[...] = jnp.zeros_like(y_ref)


def _experts(tile_ea, tile_eb, n_used, xs, wg, wu, wd, g2, b2):
    nt = xs.shape[0] // TM_MOE

    def ring_plan(tile_e):
        opens = jnp.concatenate([jnp.ones((1,), jnp.int32), (tile_e[1:] != tile_e[:-1]).astype(jnp.int32)])
        return opens, (jnp.cumsum(opens) - 1) % W_SLOTS

    open_a, slot_a = ring_plan(tile_ea)
    open_b, slot_b = ring_plan(tile_eb)
    assert nt % MOE_TILES == 0
    n_used = ((n_used + MOE_TILES - 1) // MOE_TILES) * MOE_TILES
    const = lambda i, *_: (0, 0)
    xmap = lambda i, ta, tb, ca, cb, sa, sb, nu: (jnp.minimum(i, nu[0] // MOE_TILES - 1), 0)
    w_in = pltpu.VMEM((W_SLOTS, D_MODEL, D_EXPERT), bf16)
    w_out = pltpu.VMEM((W_SLOTS, D_EXPERT, D_MODEL), bf16)
    return pl.pallas_call(
        _expert_kernel,
        grid_spec=pltpu.PrefetchScalarGridSpec(
            num_scalar_prefetch=7,
            grid=(nt // MOE_TILES,),
            in_specs=[
                pl.BlockSpec((MOE_TILES * TM_MOE, X_ROW), xmap),
                pl.BlockSpec(memory_space=pl.ANY), pl.BlockSpec(memory_space=pl.ANY),
                pl.BlockSpec(memory_space=pl.ANY),
                pl.BlockSpec((1, D_MODEL), const), pl.BlockSpec((1, D_MODEL), const),
            ],
            out_specs=pl.BlockSpec((MOE_TILES * TM_MOE, D_MODEL), lambda i, *_: (i, 0)),
            scratch_shapes=[w_in, w_in, w_out, w_in, w_in, w_out,
                            pltpu.SemaphoreType.DMA((W_SLOTS,)), pltpu.SemaphoreType.DMA((W_SLOTS,))],
        ),
        out_shape=jax.ShapeDtypeStruct((nt * TM_MOE, D_MODEL), f32),
        compiler_params=_cparams(1),
        name="experts",
    )(tile_ea, tile_eb, open_a, open_b, slot_a.astype(jnp.int32), slot_b.astype(jnp.int32), n_used,
      xs, wg, wu, wd, g2, b2)


def _collect_kernel(pos_ref, posn_ref, ys_ref, op_ref, os_ref, ybuf, sem, *, n_p):
    i = pl.program_id(0)
    n = pl.num_programs(0)
    slot = i % 2

    def gather(p_ref, s):
        def start(k, j, r):
            pltpu.make_async_copy(ys_ref.at[pl.ds(p_ref[0, 0, r], 1)], ybuf.at[s, k, pl.ds(j, 1)],
                                  sem.at[s]).start()

        _for_each_tile_row(start)

    @pl.when(i == 0)
    def _():
        gather(pos_ref, 0)

    @pl.when(i + 1 < n)
    def _():
        gather(posn_ref, 1 - slot)

    _tile_rows_wait(ys_ref, sem.at[slot])

    @pl.when(i < n_p)
    def _():
        op_ref[...] = ybuf[slot].reshape(TM_MOVE, D_MODEL)

    @pl.when(i >= n_p)
    def _():
        os_ref[...] = ybuf[slot].reshape(TM_MOVE, D_MODEL)


def _collect(pos, ys, n_p):
    n = pos.shape[0]
    n_s = n - n_p
    return pl.pallas_call(
        functools.partial(_collect_kernel, n_p=n_p),
        grid=(n,),
        in_specs=[
            pl.BlockSpec((1, 1, TM_MOVE), lambda i: (i, 0, 0), memory_space=pltpu.SMEM),
            pl.BlockSpec((1, 1, TM_MOVE), lambda i: (jnp.minimum(i + 1, n - 1), 0, 0),
                         memory_space=pltpu.SMEM),
            pl.BlockSpec(memory_space=pl.ANY),
        ],
        out_specs=[
            pl.BlockSpec((TM_MOVE, D_MODEL), lambda i: (jnp.minimum(i, n_p - 1), 0)),
            pl.BlockSpec((TM_MOVE, D_MODEL), lambda i: (jnp.maximum(i - n_p, 0), 0)),
        ],
        out_shape=[
            jax.ShapeDtypeStruct((n_p * TM_MOVE, D_MODEL), f32),
            jax.ShapeDtypeStruct((n_s * TM_MOVE, D_MODEL), f32),
        ],
        scratch_shapes=[pltpu.VMEM((2, GRP_MOVE, SUBLANES, D_MODEL), f32), pltpu.SemaphoreType.DMA((2,))],
        compiler_params=_cparams(1),
        name="collect",
    )(pos, pos, ys)


def _block_diag(w):
    n, d, _ = w.shape
    eye = jnp.eye(n, dtype=w.dtype)
    return (eye[:, None, :, None] * w[:, :, None, :]).reshape(n * d, n * d)


def _pair_walk(n):
    total = n * (n - 1) // 2

    def extend(path, used):
        if len(path) == total:
            return path
        a, b = path[-1]
        for c in range(n):
            for nxt in ((a, c), (c, b)):
                key = frozenset(nxt)
                if len(key) == 2 and key not in used:
                    out = extend(path + [nxt], used | {key})
                    if out:
                        return out
        return None

    return extend([(0, 1)], {frozenset((0, 1))})


def _pad_history(buf):
    return jnp.pad(buf, ((0, 0), (SUBLANES - (CONV_W - 1), 0), (0, 0)))


def kernel(x_prompt, x_sample, state_rg_h, state_rg_conv, state_m_C, state_m_n, state_m_m, state_m_conv, w_in, rg_conv_w, rg_conv_b, rg_w_a, rg_b_a, rg_w_x, rg_b_x, rg_lambda, m_conv_w, m_conv_b, m_w_q, m_w_k, m_w_v, m_b_i, m_b_f, m_norm_g, m_skip, w_out, ln1_g, ln1_b, ln2_g, ln2_b, moe_w_group, moe_b_group, moe_w_expert, moe_b_expert, moe_w_gate, moe_w_up, moe_w_down):
    BP, TP, _ = x_prompt.shape
    BS, TS, _ = x_sample.shape
    n_prompt, n_sample = BP * TP, BS * TS
    xp = x_prompt.reshape(n_prompt, D_MODEL)
    xs = x_sample.reshape(n_sample, D_MODEL)
    l = 0

    w_in_p = jnp.pad(w_in[l], ((0, 0), (0, N_PROJ - w_in.shape[-1]))).astype(bf16)
    wa = _block_diag(rg_w_a[l]).astype(bf16)
    wx = _block_diag(rg_w_x[l]).astype(bf16)
    row = lambda v: v.reshape(1, -1)
    bif = jnp.pad(jnp.concatenate([m_b_i[l], m_b_f[l]]), (0, LANES - 2 * M_HEADS)).reshape(1, LANES)
    wqk = jnp.concatenate([m_w_q[l], m_w_k[l]], axis=-1).astype(bf16)
    wv = m_w_v[l].astype(bf16)
    wkT = m_w_k[l].transpose(0, 2, 1).astype(bf16)
    w_route = jnp.pad(jnp.concatenate([moe_w_group[l], moe_w_expert[l]], axis=1),
                      ((0, 0), (0, LANES - N_GROUPS - N_EXPERTS)))
    wr_hi = w_route.astype(bf16)
    wr_lo = (w_route - wr_hi.astype(f32)).astype(bf16)
    wr = jnp.concatenate([wr_hi, wr_lo], axis=1)
    b_route = jnp.pad(jnp.concatenate([moe_b_group[l], moe_b_expert[l]]),
                      (0, LANES - N_GROUPS - N_EXPERTS)).reshape(1, LANES)

    proj_p, proj_s = _inproj(xp, xs, w_in_p)

    rg_args = (rg_conv_w[l], row(rg_conv_b[l]), wa, row(rg_b_a[l]), wx, row(rg_b_x[l]), row(rg_lambda[l]))
    zeros = lambda *s: jnp.zeros(s, f32)
    s_blk = M_ROWS // TS
    yrg_p, p_rg_h, p_rg_conv = _rglru(proj_p, 0, BP, TP, 1, RG_ROWS, zeros(BP, SUBLANES, D_RG),
                                      zeros(BP, 1, D_RG), *rg_args)
    yrg_s, s_rg_h, s_rg_conv = _rglru(proj_s, 0, BS, TS, RG_ROWS // TS, TS,
                                      _pad_history(state_rg_conv[l]), state_rg_h[l].reshape(BS, 1, D_RG), *rg_args)

    m_args = (m_conv_w[l], row(m_conv_b[l]), wqk, wv, wkT, bif, row(m_norm_g[l]), row(m_skip[l]))
    m0_p = jnp.full((SUBLANES, BP * LANES), M_INIT, f32)
    ym_p, p_m_C, p_m_n, p_m_m, p_m_conv, wg_b, wu_b, wd_b = _mlstm(
        proj_p, BP, TP, 1, M_ROWS, M_BLOCKS_PROMPT, zeros(BP, SUBLANES, D_M), zeros(BP, M_HEADS, M_DH, M_DH),
        zeros(BP, M_HEADS, 1, M_DH), m0_p, *m_args, riders=(moe_w_gate[l], moe_w_up[l], moe_w_down[l]))
    m0_s = jnp.pad(jnp.repeat(state_m_m[l].T, TS, axis=1), ((0, SUBLANES - M_HEADS), (0, 0)))
    ym_s, s_m_C, s_m_n, s_m_m, s_m_conv = _mlstm(
        proj_s, BS, TS, s_blk, TS, M_BLOCKS_SAMPLE, _pad_history(state_m_conv[l]), state_m_C[l],
        state_m_n[l].reshape(BS, M_HEADS, 1, M_DH), m0_s, *m_args)

    x1, cnt = _outproj(xp, xs, yrg_p, yrg_s, ym_p, ym_s, w_out[l].astype(bf16),
                       row(ln1_g[l]), row(ln1_b[l]), wr, b_route)

    n_tok = n_prompt + n_sample
    n_tiles = n_tok // TM_MOE + N_CLASSES
    lay_cls, lay_a, lay_b = [], [], []
    for g in range(N_GROUPS):
        for ja, jb in _pair_walk(EXPERTS_PER_GROUP):
            assert ja < jb
            n_first = (EXPERTS_PER_GROUP - 1) * ja - ja * (ja - 1) // 2
            lay_cls.append(PAIRS_PER_GROUP * g + n_first + jb - ja - 1)
            lay_a.append(g * EXPERTS_PER_GROUP + ja)
            lay_b.append(g * EXPERTS_PER_GROUP + jb)
    classes = jnp.arange(N_CLASSES, dtype=jnp.int32)
    in_slot = jnp.array(lay_cls, jnp.int32)[:, None] == classes
    counts = jnp.sum(jnp.where(in_slot, cnt[0, :N_CLASSES].astype(jnp.int32), 0), axis=1)
    padded = ((counts + TM_MOE - 1) // TM_MOE) * TM_MOE
    ends = jnp.cumsum(padded)
    offs = ends - padded
    offs_of_cls = jnp.sum(jnp.where(in_slot, offs[:, None], 0), axis=0)
    cls = x1[:, D_MODEL].astype(jnp.int32)
    rank = x1[:, D_MODEL + 1].astype(jnp.int32)
    pos = jnp.sum(jnp.where(cls[:, None] == classes, offs_of_cls, 0), axis=-1) + rank
    pos = pos.reshape(n_tok // TM_MOVE, 1, TM_MOVE)
    tiles = jnp.arange(n_tiles, dtype=jnp.int32)
    tile_slot = jnp.minimum(jnp.sum(tiles[:, None] >= (ends // TM_MOE)[None, :], axis=1), N_CLASSES - 1)
    on_slot = tile_slot[:, None] == classes
    tile_ea = jnp.sum(jnp.where(on_slot, jnp.array(lay_a, jnp.int32), 0), axis=-1).astype(jnp.int32)
    tile_eb = jnp.sum(jnp.where(on_slot, jnp.array(lay_b, jnp.int32), 0), axis=-1).astype(jnp.int32)
    n_used = (ends[-1] // TM_MOE).reshape(1).astype(jnp.int32)
    pad_row0 = ((offs + counts) // SUBLANES) * SUBLANES
    pad_plan = jnp.concatenate([jnp.stack([pad_row0, (ends - pad_row0) // SUBLANES], axis=1).reshape(-1),
                                     n_used]).astype(jnp.int32)

    x_sorted = _dispatch(pad_plan, pos, x1, n_tiles * TM_MOE)
    y_sorted = _experts(tile_ea, tile_eb, n_used, x_sorted, wg_b, wu_b, wd_b,
                        row(ln2_g[l]), row(ln2_b[l]))
    y_p, y_s = _collect(pos, y_sorted, n_prompt // TM_MOVE)

    return (y_p.reshape(BP, TP, D_MODEL), y_s.reshape(BS, TS, D_MODEL),
            p_rg_h.reshape(1, BP, D_RG), p_rg_conv[None], p_m_C[None], p_m_n.reshape(1, BP, M_HEADS, M_DH),
            p_m_m[None], p_m_conv[None],
            s_rg_h.reshape(1, BS, D_RG), s_rg_conv[None], s_m_C[None], s_m_n.reshape(1, BS, M_HEADS, M_DH),
            s_m_m[None], s_m_conv[None])
```

```python
import functools

import jax
import jax.numpy as jnp
from jax import lax
from jax.experimental import pallas as pl
from jax.experimental.pallas import tpu as pltpu

f32 = jnp.float32
bf16 = jnp.bfloat16

D_MODEL = 1024
D_RG = 512
RG_C = 8.0
D_M = 512
M_HEADS = 4
M_DH = 128
CONV_W = 4
N_GROUPS = 4
EXPERTS_PER_GROUP = 8
N_EXPERTS = 32
D_EXPERT = 256
ALPHA = 2.0 ** 0.25
LN_EPS = 1e-5
M_INIT = -1.0e4

LANES = 128
SUBLANES = 8
TM = 512
TM_MOVE = 1024
OUT_PARTS = 4
TM_MOE = 128
MOE_TILES = 4
W_AHEAD = 8
W_SLOTS = W_AHEAD + MOE_TILES
PAIRS_PER_GROUP = EXPERTS_PER_GROUP * (EXPERTS_PER_GROUP - 1) // 2
N_CLASSES = N_GROUPS * PAIRS_PER_GROUP
X_ROW = D_MODEL + LANES
N_PROJ = 2 * D_RG + 2 * D_M + LANES
RG_ROWS = 1024
M_ROWS = 128
M_BLOCKS_PROMPT = 8
M_BLOCKS_SAMPLE = 2
V7X_VMEM_BYTES = 64 * 1024 * 1024
VMEM_LIMIT = V7X_VMEM_BYTES - 8 * 1024 * 1024


def _cparams(n_axes):
    return pltpu.CompilerParams(dimension_semantics=("arbitrary",) * n_axes,
                                vmem_limit_bytes=VMEM_LIMIT)


def _inproj_kernel(xp_ref, xs_ref, w_ref, op_ref, os_ref, *, n_p):
    i = pl.program_id(0)

    def run(x_ref, o_ref):
        o_ref[...] = jnp.dot(x_ref[...].astype(bf16), w_ref[...], preferred_element_type=f32)

    @pl.when(i < n_p)
    def _():
        run(xp_ref, op_ref)

    @pl.when(i >= n_p)
    def _():
        run(xs_ref, os_ref)


def _inproj(xp, xs, w):
    n_p, n_s = xp.shape[0] // TM, xs.shape[0] // TM
    pmap = lambda i: (jnp.minimum(i, n_p - 1), 0)
    smap = lambda i: (jnp.maximum(i - n_p, 0), 0)
    return pl.pallas_call(
        functools.partial(_inproj_kernel, n_p=n_p),
        grid=(n_p + n_s,),
        in_specs=[
            pl.BlockSpec((TM, D_MODEL), pmap),
            pl.BlockSpec((TM, D_MODEL), smap),
            pl.BlockSpec((D_MODEL, N_PROJ), lambda i: (0, 0)),
        ],
        out_specs=[pl.BlockSpec((TM, N_PROJ), pmap), pl.BlockSpec((TM, N_PROJ), smap)],
        out_shape=[jax.ShapeDtypeStruct((n_p * TM, N_PROJ), f32),
                   jax.ShapeDtypeStruct((n_s * TM, N_PROJ), f32)],
        compiler_params=_cparams(1),
        name="inproj",
    )(xp, xs, w)


def _causal_conv(tail_scr, x, cw_ref, cb_ref, S, L, C):
    tail = tail_scr[...]
    sub = lax.broadcasted_iota(jnp.int32, (S, SUBLANES, C), 1)
    acc = cb_ref[...] + cw_ref[CONV_W - 1:CONV_W, :] * x
    for d in range(1, CONV_W):
        back = pltpu.roll(x, d, 0).reshape(S, L, C)
        head = jnp.where(sub < d, pltpu.roll(tail, d, 1), back[:, 0:SUBLANES, :])
        if L > SUBLANES:
            back = jnp.concatenate([head, back[:, SUBLANES:, :]], axis=1)
        else:
            back = head
        acc = acc + cw_ref[CONV_W - 1 - d:CONV_W - d, :] * back.reshape(S * L, C)
    tail_scr[...] = x.reshape(S, L, C)[:, L - SUBLANES:, :]
    return acc


def _softplus(x):
    return jnp.maximum(x, 0.0) + jnp.log1p(jnp.exp(-jnp.abs(x)))


def _rglru_kernel(x_ref, g_ref, buf_ref, h0_ref, cw_ref, cb_ref, wa_ref, ba_ref, wx_ref, bx_ref,
                  lam_ref, y_ref, hN_ref, cN_ref, xs_scr, *, S, L):
    R = S * L
    t = pl.program_id(1)

    @pl.when(t == 0)
    def _():
        xs_scr[...] = buf_ref[...]
        hN_ref[...] = h0_ref[...]

    x = x_ref[...]
    xc = _causal_conv(xs_scr, x, cw_ref, cb_ref, S, L, D_RG)
    xcb = xc.astype(bf16)
    r = jax.nn.sigmoid(jnp.dot(xcb, wa_ref[...], preferred_element_type=f32) + ba_ref[...])
    ig = jax.nn.sigmoid(jnp.dot(xcb, wx_ref[...], preferred_element_type=f32) + bx_ref[...])
    log_a = (-RG_C) * r * _softplus(-lam_ref[...])
    a = jnp.exp(log_a)
    th = jnp.tanh(log_a)
    u = jnp.sqrt(-2.0 * th / (1.0 - th)) * ig * xc

    n_grp, grp_per_seq = R // SUBLANES, L // SUBLANES
    a3 = a.reshape(n_grp, SUBLANES, D_RG)
    u3 = u.reshape(n_grp, SUBLANES, D_RG)
    sub = lax.broadcasted_iota(jnp.int32, (n_grp, SUBLANES, D_RG), 1)
    s = 1
    while s < SUBLANES:
        ok = sub >= s
        a_sh = pltpu.roll(a3, s, 1)
        u_sh = pltpu.roll(u3, s, 1)
        u3 = jnp.where(ok, a3 * u_sh + u3, u3)
        a3 = jnp.where(ok, a3 * a_sh, a3)
        s *= 2
    h0 = hN_ref[...]
    groups = []
    for kg in range(n_grp):
        carry = h0[kg // grp_per_seq] if kg % grp_per_seq == 0 else groups[-1][SUBLANES - 1:SUBLANES, :]
        groups.append(a3[kg] * carry + u3[kg])
    h = jnp.concatenate(groups, axis=0)

    y_ref[...] = h * jax.nn.gelu(g_ref[...], approximate=True)
    hN_ref[...] = h.reshape(S, L, D_RG)[:, L - 1:L, :]
    cN_ref[...] = xs_scr[:, SUBLANES - (CONV_W - 1):SUBLANES, :]


def _rglru(proj, row0, B, T, S, L, buf8, h0, cw, cb, wa, ba, wx, bx, lam):
    R = S * L
    nb, nt = B // S, T // L
    blk0 = row0 // R
    row_map = lambda b, t: (blk0 + b * nt + t, 0)
    const2 = lambda b, t: (0, 0)
    return pl.pallas_call(
        functools.partial(_rglru_kernel, S=S, L=L),
        grid=(nb, nt),
        in_specs=[
            pl.BlockSpec((R, D_RG), row_map),
            pl.BlockSpec((R, D_RG), lambda b, t: (blk0 + b * nt + t, 1)),
            pl.BlockSpec((S, SUBLANES, D_RG), lambda b, t: (b, 0, 0)),
            pl.BlockSpec((S, 1, D_RG), lambda b, t: (b, 0, 0)),
            pl.BlockSpec((CONV_W, D_RG), const2),
            pl.BlockSpec((1, D_RG), const2),
            pl.BlockSpec((D_RG, D_RG), const2),
            pl.BlockSpec((1, D_RG), const2),
            pl.BlockSpec((D_RG, D_RG), const2),
            pl.BlockSpec((1, D_RG), const2),
            pl.BlockSpec((1, D_RG), const2),
        ],
        out_specs=[
            pl.BlockSpec((R, D_RG), lambda b, t: (b * nt + t, 0)),
            pl.BlockSpec((S, 1, D_RG), lambda b, t: (b, 0, 0)),
            pl.BlockSpec((S, CONV_W - 1, D_RG), lambda b, t: (b, 0, 0)),
        ],
        out_shape=[
            jax.ShapeDtypeStruct((B * T, D_RG), f32),
            jax.ShapeDtypeStruct((B, 1, D_RG), f32),
            jax.ShapeDtypeStruct((B, CONV_W - 1, D_RG), f32),
        ],
        scratch_shapes=[pltpu.VMEM((S, SUBLANES, D_RG), f32)],
        compiler_params=_cparams(2),
        name="rglru",
    )(proj, proj, buf8, h0, cw, cb, wa, ba, wx, bx, lam)


def _seg_scan(x, op, fill, tin, L, reverse=False):
    s = 1
    while s < L:
        if reverse:
            sh = pltpu.roll(x, LANES - s, 1)
            ok = tin < L - s
        else:
            sh = pltpu.roll(x, s, 1)
            ok = tin >= s
        x = op(x, jnp.where(ok, sh, fill))
        s *= 2
    return x


def _mlstm_kernel(*refs, S, L, G, n_riders):
    (x_ref, z_ref, gt_ref, buf_ref, c0_ref, n0_ref, m0_ref,
     cw_ref, cb_ref, wqk_ref, wv_ref, wkT_ref, bif_ref, ng_ref, sk_ref) = refs[:15]
    rider_in = refs[15:15 + n_riders]
    y_ref, cN_ref, nN_ref, mN_ref, bN_ref = refs[15 + n_riders:20 + n_riders]
    rider_out = refs[20 + n_riders:20 + 2 * n_riders]
    xs_scr, m_scr = refs[20 + 2 * n_riders:]
    for src, dst in zip(rider_in, rider_out):
        dst[...] = src[...].astype(bf16)

    R = S * L
    GS, GR = G * S, G * R
    j = pl.program_id(1)

    @pl.when(j == 0)
    def _():
        xs_scr[...] = buf_ref[...]
        cN_ref[...] = c0_ref[...]
        nN_ref[...] = n0_ref[...]
        m_scr[...] = m0_ref[...]

    x = x_ref[...].reshape(GR, D_M)
    xc = _causal_conv(xs_scr, x, cw_ref, cb_ref, GS, L, D_M)
    xa = xc * jax.nn.sigmoid(xc)
    xab = xa.astype(bf16)
    xb = x.astype(bf16)

    il_parts, f_parts = [], []
    for g in range(G):
        gT = (gt_ref[g] + bif_ref[...]).T
        il_parts.append(gT[0:SUBLANES, :])
        f_parts.append(pltpu.roll(gT[0:SUBLANES, :], M_HEADS, 0))
    il = jnp.concatenate(il_parts, axis=0)
    fl = -_softplus(-jnp.concatenate(f_parts, axis=0))
    tin = lax.broadcasted_iota(jnp.int32, (G * SUBLANES, LANES), 1) % L
    bcum = _seg_scan(fl, jnp.add, 0.0, tin, L)
    a = il - bcum
    m_prev = m_scr[...].reshape(G * SUBLANES, LANES)
    big_m = jnp.maximum(m_prev, _seg_scan(a, jnp.maximum, -jnp.inf, tin, L))
    m_t = bcum + big_m
    if S == 1:
        m_last = jnp.broadcast_to(big_m[:, LANES - 1:LANES], big_m.shape)
    else:
        m_last = _seg_scan(big_m, jnp.maximum, -jnp.inf, tin, L, reverse=True)
    rows = [big_m, jnp.exp(m_prev - big_m), jnp.exp(-m_t), jnp.exp(a - m_last), jnp.exp(m_prev - m_last)]
    mN_ref[...] = m_t.reshape(G, SUBLANES, LANES)
    m_scr[...] = jnp.broadcast_to(m_t[:, LANES - 1:LANES], m_t.shape).reshape(G, SUBLANES, LANES)
    pad_rows = jnp.zeros((LANES - len(rows) * SUBLANES, LANES), f32)
    cols = []
    for g in range(G):
        gs = slice(g * SUBLANES, (g + 1) * SUBLANES)
        cols.append(jnp.concatenate([r[gs] for r in rows] + [pad_rows], axis=0).T)

    def col(g, q, h):
        return cols[g][:, SUBLANES * q + h:SUBLANES * q + h + 1]

    qs, ks, vs, kTs = [], [], [], []
    for h in range(M_HEADS):
        hs = slice(h * M_DH, (h + 1) * M_DH)
        qk_h = jnp.dot(xab[:, hs], wqk_ref[h], preferred_element_type=f32)
        qs.append(qk_h[:, 0:M_DH])
        ks.append(qk_h[:, M_DH:2 * M_DH] * (M_DH ** -0.5))
        vs.append(jnp.dot(xb[:, hs], wv_ref[h], preferred_element_type=f32))
        if S == 1:
            kT_h = lax.dot_general(wkT_ref[h], xab[:, hs], (((1,), (1,)), ((), ())),
                                   preferred_element_type=f32)
            kTs.append((kT_h * (M_DH ** -0.5)).astype(bf16))

    ti = lax.broadcasted_iota(jnp.int32, (R, R), 0)
    si = lax.broadcasted_iota(jnp.int32, (R, R), 1)
    mask = (si <= ti) & ((ti // L) == (si // L))
    ones_b = jnp.ones((R, M_DH), bf16)
    pairs = [(g, h) for g in range(G) for h in range(M_HEADS)]
    blk = lambda g: slice(g * R, (g + 1) * R)
    seqs = [slice(b * L, (b + 1) * L) for b in range(S)]
    q = {(g, h): qs[h][blk(g)] for g, h in pairs}
    k = {(g, h): ks[h][blk(g)] for g, h in pairs}
    v = {(g, h): vs[h][blk(g)] for g, h in pairs}
    qb = {p: q[p].astype(bf16) for p in pairs}
    kb = {p: k[p].astype(bf16) for p in pairs}
    qk = {p: lax.dot_general(qb[p], kb[p], (((1,), (1,)), ((), ())), preferred_element_type=f32)
          for p in pairs}
    sm = {}
    for g, h in pairs:
        a_row = a[g * SUBLANES + h:g * SUBLANES + h + 1, :]
        decay = jnp.exp(jnp.where(mask, a_row - col(g, 0, h), -jnp.inf))
        sm[g, h] = (qk[g, h] * decay).astype(bf16)
    nd = {p: jnp.dot(sm[p], jnp.concatenate([v[p].astype(bf16), ones_b], axis=1),
                     preferred_element_type=f32) for p in pairs}
    c_old = {(g, h, b): cN_ref[g * S + b, h] for g, h in pairs for b in range(S)}
    n_old = {(g, h, b): nN_ref[g * S + b, h] for g, h in pairs for b in range(S)}
    q_c = {(g, h, b): jnp.dot(qb[g, h][seqs[b]], c_old[g, h, b].astype(bf16), preferred_element_type=f32)
           for g, h in pairs for b in range(S)}
    hh = {}
    for g, h in pairs:
        g_col, e_col = col(g, 1, h), col(g, 2, h)
        pieces = []
        for b, rs in enumerate(seqs):
            q_n = jnp.sum(q[g, h][rs] * n_old[g, h, b], axis=1, keepdims=True)
            num = nd[g, h][rs, 0:M_DH] + g_col[rs] * q_c[g, h, b]
            den = nd[g, h][rs, M_DH:2 * M_DH] + g_col[rs] * q_n
            pieces.append(num / jnp.maximum(jnp.abs(den), e_col[rs]))
        hh[g, h] = pieces[0] if S == 1 else jnp.concatenate(pieces, axis=0)
    mu = {p: jnp.mean(hh[p], axis=1, keepdims=True) for p in pairs}
    var = {p: jnp.mean(jnp.square(hh[p] - mu[p]), axis=1, keepdims=True) for p in pairs}
    hn_blocks = [jnp.concatenate([(hh[g, h] - mu[g, h]) * lax.rsqrt(var[g, h] + LN_EPS)
                                  for h in range(M_HEADS)], axis=1) for g in range(G)]
    new_c, new_n = [], []
    for g, h in pairs:
        w_col = col(g, 3, h)
        wv = (w_col * v[g, h]).astype(bf16)
        wk = w_col * k[g, h]
        for b, rs in enumerate(seqs):
            g_end = col(g, 4, h)[(b + 1) * L - 1:(b + 1) * L, :]
            if S == 1:
                kv = jnp.dot(kTs[h][:, blk(g)], wv, preferred_element_type=f32)
            else:
                kv = lax.dot_general(kb[g, h][rs], wv[rs], (((0,), (0,)), ((), ())),
                                     preferred_element_type=f32)
            new_c.append((g * S + b, h, g_end * c_old[g, h, b] + kv))
            new_n.append((g * S + b, h, g_end * n_old[g, h, b] + jnp.sum(wk[rs], axis=0, keepdims=True)))

    hn = jnp.concatenate(hn_blocks, axis=0) * ng_ref[...]
    y = jax.nn.sigmoid(z_ref[...].reshape(GR, D_M)) * (hn + sk_ref[...] * xa)
    y_ref[...] = y.reshape(G, R, D_M)
    for sq, h, val in new_c:
        cN_ref[sq, h] = val
    for sq, h, val in new_n:
        nN_ref[sq, h] = val
    bN_ref[...] = xs_scr[:, SUBLANES - (CONV_W - 1):SUBLANES, :]


def _mlstm(proj, B, T, S, L, G, buf8, c0, n0, m0_lanes, cw, cb, wqk, wv, wkT, bif, ng, sk, riders=()):
    R = S * L
    assert R == M_ROWS == LANES
    nblk, nc = B // S, T // L
    nb = nblk // G
    assert nblk * S == B and nc * L == T and nb * G == nblk and (S == 1 or nc == 1)
    steps = nb * nc
    assert all(r.shape[0] % steps == 0 for r in riders)
    rider_specs = [pl.BlockSpec((r.shape[0] // steps,) + r.shape[1:], lambda b, j: (b * nc + j, 0, 0))
                   for r in riders]
    proj4 = proj.reshape(nblk, nc, R, N_PROJ)
    m0_3 = m0_lanes.reshape(SUBLANES, nblk, LANES).transpose(1, 0, 2)
    const2 = lambda b, j: (0, 0)
    const3 = lambda b, j: (0, 0, 0)
    gate_blk = (2 * D_RG + 2 * D_M) // LANES
    GS = G * S
    outs = pl.pallas_call(
        functools.partial(_mlstm_kernel, S=S, L=L, G=G, n_riders=len(riders)),
        grid=(nb, nc),
        in_specs=[
            pl.BlockSpec((G, None, R, D_M), lambda b, j: (b, j, 0, 2)),
            pl.BlockSpec((G, None, R, D_M), lambda b, j: (b, j, 0, 3)),
            pl.BlockSpec((G, None, R, LANES), lambda b, j: (b, j, 0, gate_blk)),
            pl.BlockSpec((GS, SUBLANES, D_M), lambda b, j: (b, 0, 0)),
            pl.BlockSpec((GS, M_HEADS, M_DH, M_DH), lambda b, j: (b, 0, 0, 0)),
            pl.BlockSpec((GS, M_HEADS, 1, M_DH), lambda b, j: (b, 0, 0, 0)),
            pl.BlockSpec((G, SUBLANES, LANES), lambda b, j: (b, 0, 0)),
            pl.BlockSpec((CONV_W, D_M), const2),
            pl.BlockSpec((1, D_M), const2),
            pl.BlockSpec((M_HEADS, M_DH, 2 * M_DH), const3),
            pl.BlockSpec((M_HEADS, M_DH, M_DH), const3),
            pl.BlockSpec((M_HEADS, M_DH, M_DH), const3),
            pl.BlockSpec((1, LANES), const2),
            pl.BlockSpec((1, D_M), const2),
            pl.BlockSpec((1, D_M), const2),
        ] + rider_specs,
        out_specs=[
            pl.BlockSpec((G, None, R, D_M), lambda b, j: (b, j, 0, 0)),
            pl.BlockSpec((GS, M_HEADS, M_DH, M_DH), lambda b, j: (b, 0, 0, 0)),
            pl.BlockSpec((GS, M_HEADS, 1, M_DH), lambda b, j: (b, 0, 0, 0)),
            pl.BlockSpec((G, None, SUBLANES, LANES), lambda b, j: (b, j, 0, 0)),
            pl.BlockSpec((GS, CONV_W - 1, D_M), lambda b, j: (b, 0, 0)),
        ] + rider_specs,
        out_shape=[
            jax.ShapeDtypeStruct((nblk, nc, R, D_M), f32),
            jax.ShapeDtypeStruct((B, M_HEADS, M_DH, M_DH), f32),
            jax.ShapeDtypeStruct((B, M_HEADS, 1, M_DH), f32),
            jax.ShapeDtypeStruct((nblk, nc, SUBLANES, LANES), f32),
            jax.ShapeDtypeStruct((B, CONV_W - 1, D_M), f32),
        ] + [jax.ShapeDtypeStruct(r.shape, bf16) for r in riders],
        scratch_shapes=[pltpu.VMEM((GS, SUBLANES, D_M), f32), pltpu.VMEM((G, SUBLANES, LANES), f32)],
        compiler_params=_cparams(2),
        name="mlstm",
    )(proj4, proj4, proj4, buf8, c0, n0, m0_3, cw, cb, wqk, wv, wkT, bif, ng, sk, *riders)
    y, c_new, n_new, m_t, b_new = outs[:5]
    m_last = m_t[:, nc - 1, :M_HEADS, :].reshape(nblk, M_HEADS, S, L)[:, :, :, L - 1]
    m_last = m_last.transpose(0, 2, 1).reshape(B, M_HEADS)
    return (y.reshape(B * T, D_M), c_new, n_new, m_last, b_new) + tuple(outs[5:])


def _layer_norm(x, g, b):
    mu = jnp.mean(x, axis=-1, keepdims=True)
    var = jnp.mean(jnp.square(x - mu), axis=-1, keepdims=True)
    return (x - mu) * lax.rsqrt(var + LN_EPS) * g + b


def _first_lane_of_max(vals, lane_f):
    vmax = jnp.max(vals, axis=1, keepdims=True)
    idx = jnp.min(jnp.where(vals == vmax, lane_f, float(LANES)), axis=1, keepdims=True)
    return vmax, idx


def _outproj_kernel(xp_ref, xs_ref, rgp_ref, rgs_ref, mp_ref, ms_ref, wo_ref, g1_ref, b1_ref,
                    wr_ref, br_ref, x1_ref, cnt_ref, *, n_p):
    i = pl.program_id(0)

    @pl.when(i == 0)
    def _():
        cnt_ref[...] = jnp.zeros_like(cnt_ref)

    def route(lg, lane, lane_f):
        neg = -jnp.inf
        gl = jnp.where(lane < N_GROUPS, lg, neg)
        gmax, gidx = _first_lane_of_max(gl, lane_f)
        p_g = 1.0 / jnp.sum(jnp.exp(gl - gmax), axis=1, keepdims=True)
        e_lo = float(N_GROUPS) + float(EXPERTS_PER_GROUP) * gidx
        el = jnp.where((lane_f >= e_lo) & (lane_f < e_lo + float(EXPERTS_PER_GROUP)), lg, neg)
        v1, i1 = _first_lane_of_max(el, lane_f)
        v2, i2 = _first_lane_of_max(jnp.where(lane_f == i1, neg, el), lane_f)
        d = jnp.exp(v2 - v1)
        w1 = p_g / (1.0 + d)
        w2 = p_g * d / (1.0 + d)
        first_low = i1 < i2
        j_lo = jnp.minimum(i1, i2) - e_lo
        j_hi = jnp.maximum(i1, i2) - e_lo
        n_first = float(EXPERTS_PER_GROUP - 1) * j_lo - 0.5 * j_lo * (j_lo - 1.0)
        cls = float(PAIRS_PER_GROUP) * gidx + n_first + (j_hi - j_lo - 1.0)
        return cls, jnp.where(first_low, w1, w2), jnp.where(first_low, w2, w1)

    def run(x_ref, rg_ref, m_ref):
        rp = TM // OUT_PARTS
        parts = [slice(p * rp, (p + 1) * rp) for p in range(OUT_PARTS)]
        lane = lax.broadcasted_iota(jnp.int32, (rp, LANES), 1)
        lane_f = lane.astype(f32)
        heads = [jnp.concatenate([rg_ref[r, :].astype(bf16), m_ref[r, :].astype(bf16)], axis=1) for r in parts]
        mix = [jnp.dot(h, wo_ref[...], preferred_element_type=f32) for h in heads]
        x1 = [_layer_norm(ALPHA * x_ref[r, :] + mx, g1_ref[...], b1_ref[...]) for r, mx in zip(parts, mix)]
        for r, v in zip(parts, x1):
            x1_ref[r, 0:D_MODEL] = v

        hi = [v.astype(bf16) for v in x1]
        lo = [(v - h.astype(f32)).astype(bf16) for v, h in zip(x1, hi)]
        hi_terms = [jnp.dot(h, wr_ref[...], preferred_element_type=f32) for h in hi]
        lo_term = [jnp.dot(v, wr_ref[:, 0:LANES], preferred_element_type=f32) for v in lo]
        lg = [ht[:, 0:LANES] + ht[:, LANES:2 * LANES] + lt + br_ref[...] for ht, lt in zip(hi_terms, lo_term)]
        routed = [route(v, lane, lane_f) for v in lg]

        ti = lax.broadcasted_iota(jnp.int32, (rp, rp), 0)
        si = lax.broadcasted_iota(jnp.int32, (rp, rp), 1)
        tri = (si <= ti).astype(bf16)
        hot = [lane_f == cls for cls, _, _ in routed]
        cum = [jnp.dot(tri, h.astype(bf16), preferred_element_type=f32) for h in hot]
        seen = cnt_ref[...]
        for r, h, cm, (cls, w_lo, w_hi) in zip(parts, hot, cum, routed):
            rank = jnp.sum(jnp.where(h, cm - 1.0 + seen, 0.0), axis=1, keepdims=True)
            seen = seen + cm[rp - 1:rp, :]
            info = jnp.zeros((rp, LANES), f32)
            for c, val in enumerate((cls, rank, w_lo, w_hi)):
                info = jnp.where(lane == c, val, info)
            x1_ref[r, D_MODEL:D_MODEL + LANES] = info
        cnt_ref[...] = seen

    @pl.when(i < n_p)
    def _():
        run(xp_ref, rgp_ref, mp_ref)

    @pl.when(i >= n_p)
    def _():
        run(xs_ref, rgs_ref, ms_ref)


def _outproj(xp, xs, rgp, rgs, mp, ms, wo, g1, b1, wr, br):
    n_p, n_s = xp.shape[0] // TM, xs.shape[0] // TM
    n = n_p + n_s
    pmap = lambda i: (jnp.minimum(i, n_p - 1), 0)
    smap = lambda i: (jnp.maximum(i - n_p, 0), 0)
    const = lambda i: (0, 0)
    return pl.pallas_call(
        functools.partial(_outproj_kernel, n_p=n_p),
        grid=(n,),
        in_specs=[
            pl.BlockSpec((TM, D_MODEL), pmap), pl.BlockSpec((TM, D_MODEL), smap),
            pl.BlockSpec((TM, D_RG), pmap), pl.BlockSpec((TM, D_RG), smap),
            pl.BlockSpec((TM, D_M), pmap), pl.BlockSpec((TM, D_M), smap),
            pl.BlockSpec((D_RG + D_M, D_MODEL), const),
            pl.BlockSpec((1, D_MODEL), const), pl.BlockSpec((1, D_MODEL), const),
            pl.BlockSpec((D_MODEL, 2 * LANES), const),
            pl.BlockSpec((1, LANES), const),
        ],
        out_specs=[
            pl.BlockSpec((TM, X_ROW), lambda i: (i, 0)),
            pl.BlockSpec((1, LANES), const),
        ],
        out_shape=[
            jax.ShapeDtypeStruct((n * TM, X_ROW), f32),
            jax.ShapeDtypeStruct((1, LANES), f32),
        ],
        compiler_params=_cparams(1),
        name="outproj",
    )(xp, xs, rgp, rgs, mp, ms, wo, g1, b1, wr, br)


GRP_MOVE = TM_MOVE // SUBLANES


def _grouped(x):
    return x.reshape(x.shape[0] // SUBLANES, SUBLANES, x.shape[1])


def _tile_rows_wait(hbm_ref, sem):
    rows = hbm_ref.at[pl.ds(0, TM_MOVE)]
    pltpu.make_async_copy(rows, rows, sem).wait()


def _for_each_tile_row(body):
    def group(k, c):
        for j in range(SUBLANES):
            body(k, j, k * SUBLANES + j)
        return c

    lax.fori_loop(0, GRP_MOVE, group, 0)


def _dispatch_kernel(pad_ref, pos_ref, x1_ref, xs_ref, zero_scr, xbuf, sem, zsem, lsem):
    i = pl.program_id(0)
    n = pl.num_programs(0)

    def zero_tile(t):
        return pltpu.make_async_copy(zero_scr, xs_ref.at[pl.ds(pl.multiple_of(t * TM_MOE, TM_MOE), TM_MOE)], zsem)

    def zero_chunk(row0):
        return pltpu.make_async_copy(zero_scr.at[pl.ds(0, SUBLANES)],
                                     xs_ref.at[pl.ds(pl.multiple_of(row0, SUBLANES), SUBLANES)], zsem)

    @pl.when(i == 0)
    def _():
        zero_scr[...] = jnp.zeros_like(zero_scr)
        n_used = pad_ref[2 * N_CLASSES]
        n_tiles = xs_ref.shape[0] // TM_MOE
        for go in (lambda cp: cp.start(), lambda cp: cp.wait()):
            lax.fori_loop(n_used, n_tiles, lambda t, c: (go(zero_tile(t)), c)[1], 0)

            def per_class(cl, c):
                row0 = pad_ref[2 * cl]
                lax.fori_loop(0, pad_ref[2 * cl + 1],
                              lambda k, c2: (go(zero_chunk(row0 + k * SUBLANES)), c2)[1], 0)
                return c

            lax.fori_loop(0, N_CLASSES, per_class, 0)

    def load(t, s):
        return pltpu.make_async_copy(x1_ref.at[pl.ds(t * GRP_MOVE, GRP_MOVE)], xbuf.at[s], lsem.at[s])

    @pl.when(i == 0)
    def _():
        load(0, 0).start()

    @pl.when(i + 1 < n)
    def _():
        load(i + 1, (i + 1) % 3).start()

    slot = i % 3
    load(i, slot).wait()

    def start(k, j, r):
        pltpu.make_async_copy(xbuf.at[slot, k, pl.ds(j, 1)], xs_ref.at[pl.ds(pos_ref[0, 0, r], 1)],
                              sem.at[slot]).start()

    _for_each_tile_row(start)

    @pl.when(i > 0)
    def _():
        _tile_rows_wait(xs_ref, sem.at[(i + 2) % 3])

    @pl.when(i == n - 1)
    def _():
        _tile_rows_wait(xs_ref, sem.at[slot])


def _dispatch(pad_plan, pos, x1, n_rows):
    n = x1.shape[0] // TM_MOVE
    return pl.pallas_call(
        _dispatch_kernel,
        grid_spec=pltpu.PrefetchScalarGridSpec(
            num_scalar_prefetch=1,
            grid=(n,),
            in_specs=[
                pl.BlockSpec((1, 1, TM_MOVE), lambda i, *_: (i, 0, 0), memory_space=pltpu.SMEM),
                pl.BlockSpec(memory_space=pl.ANY),
            ],
            out_specs=pl.BlockSpec(memory_space=pl.ANY),
            scratch_shapes=[pltpu.VMEM((TM_MOE, X_ROW), f32),
                            pltpu.VMEM((3, GRP_MOVE, SUBLANES, X_ROW), f32),
                            pltpu.SemaphoreType.DMA((3,)), pltpu.SemaphoreType.DMA(()),
                            pltpu.SemaphoreType.DMA((3,))],
        ),
        out_shape=jax.ShapeDtypeStruct((n_rows, X_ROW), f32),
        compiler_params=_cparams(1),
        name="dispatch",
    )(pad_plan, pos, _grouped(x1))


def _expert_kernel(ta_ref, tb_ref, ca_ref, cb_ref, sa_ref, sb_ref, nu_ref,
                   x_ref, wg_hbm, wu_hbm, wd_hbm, g2_ref, b2_ref, y_ref,
                   wga, wua, wda, wgb, wub, wdb, sem_a, sem_b):
    i = pl.program_id(0)
    n_used = nu_ref[0]
    first = i * MOE_TILES
    side_a = (ta_ref, ca_ref, sa_ref, (wga, wua, wda), sem_a)
    side_b = (tb_ref, cb_ref, sb_ref, (wgb, wub, wdb), sem_b)

    def copies(side, j):
        t_ref, _, s_ref, bufs, sem = side
        e, s = t_ref[j], s_ref[j]
        return [pltpu.make_async_copy(w.at[e], buf.at[s], sem.at[s])
                for w, buf in zip((wg_hbm, wu_hbm, wd_hbm), bufs)]

    def fetch(j):
        @pl.when(j < n_used)
        def _():
            for side in (side_a, side_b):
                @pl.when(side[1][j] == 1)
                def _():
                    for cp in copies(side, j):
                        cp.start()

    @pl.when(i == 0)
    def _():
        for j in range(W_AHEAD):
            fetch(j)

    for t in range(MOE_TILES):
        fetch(first + W_AHEAD + t)

    @pl.when(first < n_used)
    def _():
        for t in range(MOE_TILES):
            for side in (side_a, side_b):
                @pl.when(side[1][first + t] == 1)
                def _():
                    for cp in copies(side, first + t):
                        cp.wait()
        rows = [slice(t * TM_MOE, (t + 1) * TM_MOE) for t in range(MOE_TILES)]
        x = [x_ref[r, 0:D_MODEL] for r in rows]
        info = [x_ref[r, D_MODEL:X_ROW] for r in rows]
        xb = [v.astype(bf16) for v in x]
        units = [(t, bufs, s_ref[first + t]) for t in range(MOE_TILES)
                 for bufs, s_ref in (((wga, wua, wda), sa_ref), ((wgb, wub, wdb), sb_ref))]
        hg = [jnp.dot(xb[t], bufs[0][s], preferred_element_type=f32) for t, bufs, s in units]
        hu = [jnp.dot(xb[t], bufs[1][s], preferred_element_type=f32) for t, bufs, s in units]
        mid = [(g * jax.nn.sigmoid(g) * u).astype(bf16) for g, u in zip(hg, hu)]
        ys = [jnp.dot(m, bufs[2][s], preferred_element_type=f32)
              for m, (t, bufs, s) in zip(mid, units)]
        for t in range(MOE_TILES):
            ffn = info[t][:, 2:3] * ys[2 * t] + info[t][:, 3:4] * ys[2 * t + 1]
            y_ref[rows[t], :] = _layer_norm(ALPHA * x[t] + ffn, g2_ref[...], b2_ref[...])

    @pl.when(first >= n_used)
    def _():
        y_ref[...] = jnp.zeros_like(y_ref)


def _experts(tile_ea, tile_eb, n_used, xs, wg, wu, wd, g2, b2):
    nt = xs.shape[0] // TM_MOE

    def ring_plan(tile_e):
        opens = jnp.concatenate([jnp.ones((1,), jnp.int32), (tile_e[1:] != tile_e[:-1]).astype(jnp.int32)])
        return opens, (jnp.cumsum(opens) - 1) % W_SLOTS

    open_a, slot_a = ring_plan(tile_ea)
    open_b, slot_b = ring_plan(tile_eb)
    assert nt % MOE_TILES == 0
    n_used = ((n_used + MOE_TILES - 1) // MOE_TILES) * MOE_TILES
    const = lambda i, *_: (0, 0)
    xmap = lambda i, ta, tb, ca, cb, sa, sb, nu: (jnp.minimum(i, nu[0] // MOE_TILES - 1), 0)
    w_in = pltpu.VMEM((W_SLOTS, D_MODEL, D_EXPERT), bf16)
    w_out = pltpu.VMEM((W_SLOTS, D_EXPERT, D_MODEL), bf16)
    return pl.pallas_call(
        _expert_kernel,
        grid_spec=pltpu.PrefetchScalarGridSpec(
            num_scalar_prefetch=7,
            grid=(nt // MOE_TILES,),
            in_specs=[
                pl.BlockSpec((MOE_TILES * TM_MOE, X_ROW), xmap),
                pl.BlockSpec(memory_space=pl.ANY), pl.BlockSpec(memory_space=pl.ANY),
                pl.BlockSpec(memory_space=pl.ANY),
                pl.BlockSpec((1, D_MODEL), const), pl.BlockSpec((1, D_MODEL), const),
            ],
            out_specs=pl.BlockSpec((MOE_TILES * TM_MOE, D_MODEL), lambda i, *_: (i, 0)),
            scratch_shapes=[w_in, w_in, w_out, w_in, w_in, w_out,
                            pltpu.SemaphoreType.DMA((W_SLOTS,)), pltpu.SemaphoreType.DMA((W_SLOTS,))],
        ),
        out_shape=jax.ShapeDtypeStruct((nt * TM_MOE, D_MODEL), f32),
        compiler_params=_cparams(1),
        name="experts",
    )(tile_ea, tile_eb, open_a, open_b, slot_a.astype(jnp.int32), slot_b.astype(jnp.int32), n_used,
      xs, wg, wu, wd, g2, b2)


def _collect_kernel(pos_ref, posn_ref, ys_ref, op_ref, os_ref, ybuf, sem, *, n_p):
    i = pl.program_id(0)
    n = pl.num_programs(0)
    slot = i % 2

    def gather(p_ref, s):
        def start(k, j, r):
            pltpu.make_async_copy(ys_ref.at[pl.ds(p_ref[0, 0, r], 1)], ybuf.at[s, k, pl.ds(j, 1)],
                                  sem.at[s]).start()

        _for_each_tile_row(start)

    @pl.when(i == 0)
    def _():
        gather(pos_ref, 0)

    @pl.when(i + 1 < n)
    def _():
        gather(posn_ref, 1 - slot)

    _tile_rows_wait(ys_ref, sem.at[slot])

    @pl.when(i < n_p)
    def _():
        op_ref[...] = ybuf[slot].reshape(TM_MOVE, D_MODEL)

    @pl.when(i >= n_p)
    def _():
        os_ref[...] = ybuf[slot].reshape(TM_MOVE, D_MODEL)


def _collect(pos, ys, n_p):
    n = pos.shape[0]
    n_s = n - n_p
    return pl.pallas_call(
        functools.partial(_collect_kernel, n_p=n_p),
        grid=(n,),
        in_specs=[
            pl.BlockSpec((1, 1, TM_MOVE), lambda i: (i, 0, 0), memory_space=pltpu.SMEM),
            pl.BlockSpec((1, 1, TM_MOVE), lambda i: (jnp.minimum(i + 1, n - 1), 0, 0),
                         memory_space=pltpu.SMEM),
            pl.BlockSpec(memory_space=pl.ANY),
        ],
        out_specs=[
            pl.BlockSpec((TM_MOVE, D_MODEL), lambda i: (jnp.minimum(i, n_p - 1), 0)),
            pl.BlockSpec((TM_MOVE, D_MODEL), lambda i: (jnp.maximum(i - n_p, 0), 0)),
        ],
        out_shape=[
            jax.ShapeDtypeStruct((n_p * TM_MOVE, D_MODEL), f32),
            jax.ShapeDtypeStruct((n_s * TM_MOVE, D_MODEL), f32),
        ],
        scratch_shapes=[pltpu.VMEM((2, GRP_MOVE, SUBLANES, D_MODEL), f32), pltpu.SemaphoreType.DMA((2,))],
        compiler_params=_cparams(1),
        name="collect",
    )(pos, pos, ys)


def _block_diag(w):
    n, d, _ = w.shape
    eye = jnp.eye(n, dtype=w.dtype)
    return (eye[:, None, :, None] * w[:, :, None, :]).reshape(n * d, n * d)


def _pair_walk(n):
    total = n * (n - 1) // 2

    def extend(path, used):
        if len(path) == total:
            return path
        a, b = path[-1]
        for c in range(n):
            for nxt in ((a, c), (c, b)):
                key = frozenset(nxt)
                if len(key) == 2 and key not in used:
                    out = extend(path + [nxt], used | {key})
                    if out:
                        return out
        return None

    return extend([(0, 1)], {frozenset((0, 1))})


def _pad_history(buf):
    return jnp.pad(buf, ((0, 0), (SUBLANES - (CONV_W - 1), 0), (0, 0)))


def kernel(x_prompt, x_sample, state_rg_h, state_rg_conv, state_m_C, state_m_n, state_m_m, state_m_conv, w_in, rg_conv_w, rg_conv_b, rg_w_a, rg_b_a, rg_w_x, rg_b_x, rg_lambda, m_conv_w, m_conv_b, m_w_q, m_w_k, m_w_v, m_b_i, m_b_f, m_norm_g, m_skip, w_out, ln1_g, ln1_b, ln2_g, ln2_b, moe_w_group, moe_b_group, moe_w_expert, moe_b_expert, moe_w_gate, moe_w_up, moe_w_down):
    BP, TP, _ = x_prompt.shape
    BS, TS, _ = x_sample.shape
    n_prompt, n_sample = BP * TP, BS * TS
    xp = x_prompt.reshape(n_prompt, D_MODEL)
    xs = x_sample.reshape(n_sample, D_MODEL)
    l = 0

    w_in_p = jnp.pad(w_in[l], ((0, 0), (0, N_PROJ - w_in.shape[-1]))).astype(bf16)
    wa = _block_diag(rg_w_a[l]).astype(bf16)
    wx = _block_diag(rg_w_x[l]).astype(bf16)
    row = lambda v: v.reshape(1, -1)
    bif = jnp.pad(jnp.concatenate([m_b_i[l], m_b_f[l]]), (0, LANES - 2 * M_HEADS)).reshape(1, LANES)
    wqk = jnp.concatenate([m_w_q[l], m_w_k[l]], axis=-1).astype(bf16)
    wv = m_w_v[l].astype(bf16)
    wkT = m_w_k[l].transpose(0, 2, 1).astype(bf16)
    w_route = jnp.pad(jnp.concatenate([moe_w_group[l], moe_w_expert[l]], axis=1),
                      ((0, 0), (0, LANES - N_GROUPS - N_EXPERTS)))
    wr_hi = w_route.astype(bf16)
    wr_lo = (w_route - wr_hi.astype(f32)).astype(bf16)
    wr = jnp.concatenate([wr_hi, wr_lo], axis=1)
    b_route = jnp.pad(jnp.concatenate([moe_b_group[l], moe_b_expert[l]]),
                      (0, LANES - N_GROUPS - N_EXPERTS)).reshape(1, LANES)

    proj_p, proj_s = _inproj(xp, xs, w_in_p)

    rg_args = (rg_conv_w[l], row(rg_conv_b[l]), wa, row(rg_b_a[l]), wx, row(rg_b_x[l]), row(rg_lambda[l]))
    zeros = lambda *s: jnp.zeros(s, f32)
    s_blk = M_ROWS // TS
    yrg_p, p_rg_h, p_rg_conv = _rglru(proj_p, 0, BP, TP, 1, RG_ROWS, zeros(BP, SUBLANES, D_RG),
                                      zeros(BP, 1, D_RG), *rg_args)
    yrg_s, s_rg_h, s_rg_conv = _rglru(proj_s, 0, BS, TS, RG_ROWS // TS, TS,
                                      _pad_history(state_rg_conv[l]), state_rg_h[l].reshape(BS, 1, D_RG), *rg_args)

    m_args = (m_conv_w[l], row(m_conv_b[l]), wqk, wv, wkT, bif, row(m_norm_g[l]), row(m_skip[l]))
    m0_p = jnp.full((SUBLANES, BP * LANES), M_INIT, f32)
    ym_p, p_m_C, p_m_n, p_m_m, p_m_conv, wg_b, wu_b, wd_b = _mlstm(
        proj_p, BP, TP, 1, M_ROWS, M_BLOCKS_PROMPT, zeros(BP, SUBLANES, D_M), zeros(BP, M_HEADS, M_DH, M_DH),
        zeros(BP, M_HEADS, 1, M_DH), m0_p, *m_args, riders=(moe_w_gate[l], moe_w_up[l], moe_w_down[l]))
    m0_s = jnp.pad(jnp.repeat(state_m_m[l].T, TS, axis=1), ((0, SUBLANES - M_HEADS), (0, 0)))
    ym_s, s_m_C, s_m_n, s_m_m, s_m_conv = _mlstm(
        proj_s, BS, TS, s_blk, TS, M_BLOCKS_SAMPLE, _pad_history(state_m_conv[l]), state_m_C[l],
        state_m_n[l].reshape(BS, M_HEADS, 1, M_DH), m0_s, *m_args)

    x1, cnt = _outproj(xp, xs, yrg_p, yrg_s, ym_p, ym_s, w_out[l].astype(bf16),
                       row(ln1_g[l]), row(ln1_b[l]), wr, b_route)

    n_tok = n_prompt + n_sample
    n_tiles = n_tok // TM_MOE + N_CLASSES
    lay_cls, lay_a, lay_b = [], [], []
    for g in range(N_GROUPS):
        for ja, jb in _pair_walk(EXPERTS_PER_GROUP):
            assert ja < jb
            n_first = (EXPERTS_PER_GROUP - 1) * ja - ja * (ja - 1) // 2
            lay_cls.append(PAIRS_PER_GROUP * g + n_first + jb - ja - 1)
            lay_a.append(g * EXPERTS_PER_GROUP + ja)
            lay_b.append(g * EXPERTS_PER_GROUP + jb)
    classes = jnp.arange(N_CLASSES, dtype=jnp.int32)
    in_slot = jnp.array(lay_cls, jnp.int32)[:, None] == classes
    counts = jnp.sum(jnp.where(in_slot, cnt[0, :N_CLASSES].astype(jnp.int32), 0), axis=1)
    padded = ((counts + TM_MOE - 1) // TM_MOE) * TM_MOE
    ends = jnp.cumsum(padded)
    offs = ends - padded
    offs_of_cls = jnp.sum(jnp.where(in_slot, offs[:, None], 0), axis=0)
    cls = x1[:, D_MODEL].astype(jnp.int32)
    rank = x1[:, D_MODEL + 1].astype(jnp.int32)
    pos = jnp.sum(jnp.where(cls[:, None] == classes, offs_of_cls, 0), axis=-1) + rank
    pos = pos.reshape(n_tok // TM_MOVE, 1, TM_MOVE)
    tiles = jnp.arange(n_tiles, dtype=jnp.int32)
    tile_slot = jnp.minimum(jnp.sum(tiles[:, None] >= (ends // TM_MOE)[None, :], axis=1), N_CLASSES - 1)
    on_slot = tile_slot[:, None] == classes
    tile_ea = jnp.sum(jnp.where(on_slot, jnp.array(lay_a, jnp.int32), 0), axis=-1).astype(jnp.int32)
    tile_eb = jnp.sum(jnp.where(on_slot, jnp.array(lay_b, jnp.int32), 0), axis=-1).astype(jnp.int32)
    n_used = (ends[-1] // TM_MOE).reshape(1).astype(jnp.int32)
    pad_row0 = ((offs + counts) // SUBLANES) * SUBLANES
    pad_plan = jnp.concatenate([jnp.stack([pad_row0, (ends - pad_row0) // SUBLANES], axis=1).reshape(-1),
                                     n_used]).astype(jnp.int32)

    x_sorted = _dispatch(pad_plan, pos, x1, n_tiles * TM_MOE)
    y_sorted = _experts(tile_ea, tile_eb, n_used, x_sorted, wg_b, wu_b, wd_b,
                        row(ln2_g[l]), row(ln2_b[l]))
    y_p, y_s = _collect(pos, y_sorted, n_prompt // TM_MOVE)

    return (y_p.reshape(BP, TP, D_MODEL), y_s.reshape(BS, TS, D_MODEL),
            p_rg_h.reshape(1, BP, D_RG), p_rg_conv[None], p_m_C[None], p_m_n.reshape(1, BP, M_HEADS, M_DH),
            p_m_m[None], p_m_conv[None],
            s_rg_h.reshape(1, BS, D_RG), s_rg_conv[None], s_m_C[None], s_m_n.reshape(1, BS, M_HEADS, M_DH),
            s_m_m[None], s_m_conv[None])
```

```python
import functools

import jax
import jax.numpy as jnp
from jax import lax
from jax.experimental import pallas as pl
from jax.experimental.pallas import tpu as pltpu

f32 = jnp.float32
bf16 = jnp.bfloat16

D_MODEL = 1024
D_RG = 512
RG_C = 8.0
D_M = 512
M_HEADS = 4
M_DH = 128
CONV_W = 4
N_GROUPS = 4
EXPERTS_PER_GROUP = 8
N_EXPERTS = 32
D_EXPERT = 256
ALPHA = 2.0 ** 0.25
LN_EPS = 1e-5
M_INIT = -1.0e4

LANES = 128
SUBLANES = 8
TM = 512
TM_MOVE = 1024
TM_OUT = 1024
OUT_PARTS = 8
TM_MOE = 128
MOE_TILES = 4
W_AHEAD = 8
W_SLOTS = W_AHEAD + MOE_TILES
PAIRS_PER_GROUP = EXPERTS_PER_GROUP * (EXPERTS_PER_GROUP - 1) // 2
N_CLASSES = N_GROUPS * PAIRS_PER_GROUP
X_ROW = D_MODEL + LANES
N_PROJ = 2 * D_RG + 2 * D_M + LANES
RG_ROWS = 1024
M_ROWS = 128
M_BLOCKS_PROMPT = 8
M_BLOCKS_SAMPLE = 2
V7X_VMEM_BYTES = 64 * 1024 * 1024
VMEM_LIMIT = V7X_VMEM_BYTES - 8 * 1024 * 1024


def _cparams(n_axes):
    return pltpu.CompilerParams(dimension_semantics=("arbitrary",) * n_axes,
                                vmem_limit_bytes=VMEM_LIMIT)


def _inproj_kernel(xp_ref, xs_ref, w_ref, op_ref, os_ref, *, n_p):
    i = pl.program_id(0)

    def run(x_ref, o_ref):
        o_ref[...] = jnp.dot(x_ref[...].astype(bf16), w_ref[...], preferred_element_type=f32)

    @pl.when(i < n_p)
    def _():
        run(xp_ref, op_ref)

    @pl.when(i >= n_p)
    def _():
        run(xs_ref, os_ref)


def _inproj(xp, xs, w):
    n_p, n_s = xp.shape[0] // TM, xs.shape[0] // TM
    pmap = lambda i: (jnp.minimum(i, n_p - 1), 0)
    smap = lambda i: (jnp.maximum(i - n_p, 0), 0)
    return pl.pallas_call(
        functools.partial(_inproj_kernel, n_p=n_p),
        grid=(n_p + n_s,),
        in_specs=[
            pl.BlockSpec((TM, D_MODEL), pmap),
            pl.BlockSpec((TM, D_MODEL), smap),
            pl.BlockSpec((D_MODEL, N_PROJ), lambda i: (0, 0)),
        ],
        out_specs=[pl.BlockSpec((TM, N_PROJ), pmap), pl.BlockSpec((TM, N_PROJ), smap)],
        out_shape=[jax.ShapeDtypeStruct((n_p * TM, N_PROJ), f32),
                   jax.ShapeDtypeStruct((n_s * TM, N_PROJ), f32)],
        compiler_params=_cparams(1),
        name="inproj",
    )(xp, xs, w)


def _causal_conv(tail_scr, x, cw_ref, cb_ref, S, L, C):
    tail = tail_scr[...]
    sub = lax.broadcasted_iota(jnp.int32, (S, SUBLANES, C), 1)
    acc = cb_ref[...] + cw_ref[CONV_W - 1:CONV_W, :] * x
    for d in range(1, CONV_W):
        back = pltpu.roll(x, d, 0).reshape(S, L, C)
        head = jnp.where(sub < d, pltpu.roll(tail, d, 1), back[:, 0:SUBLANES, :])
        if L > SUBLANES:
            back = jnp.concatenate([head, back[:, SUBLANES:, :]], axis=1)
        else:
            back = head
        acc = acc + cw_ref[CONV_W - 1 - d:CONV_W - d, :] * back.reshape(S * L, C)
    tail_scr[...] = x.reshape(S, L, C)[:, L - SUBLANES:, :]
    return acc


def _softplus(x):
    return jnp.maximum(x, 0.0) + jnp.log1p(jnp.exp(-jnp.abs(x)))


def _rglru_kernel(x_ref, g_ref, buf_ref, h0_ref, cw_ref, cb_ref, wa_ref, ba_ref, wx_ref, bx_ref,
                  lam_ref, y_ref, hN_ref, cN_ref, xs_scr, *, S, L):
    R = S * L
    t = pl.program_id(1)

    @pl.when(t == 0)
    def _():
        xs_scr[...] = buf_ref[...]
        hN_ref[...] = h0_ref[...]

    x = x_ref[...]
    xc = _causal_conv(xs_scr, x, cw_ref, cb_ref, S, L, D_RG)
    xcb = xc.astype(bf16)
    r = jax.nn.sigmoid(jnp.dot(xcb, wa_ref[...], preferred_element_type=f32) + ba_ref[...])
    ig = jax.nn.sigmoid(jnp.dot(xcb, wx_ref[...], preferred_element_type=f32) + bx_ref[...])
    log_a = (-RG_C) * r * _softplus(-lam_ref[...])
    a = jnp.exp(log_a)
    th = jnp.tanh(log_a)
    u = jnp.sqrt(-2.0 * th / (1.0 - th)) * ig * xc

    n_grp, grp_per_seq = R // SUBLANES, L // SUBLANES
    a3 = a.reshape(n_grp, SUBLANES, D_RG)
    u3 = u.reshape(n_grp, SUBLANES, D_RG)
    sub = lax.broadcasted_iota(jnp.int32, (n_grp, SUBLANES, D_RG), 1)
    s = 1
    while s < SUBLANES:
        ok = sub >= s
        a_sh = pltpu.roll(a3, s, 1)
        u_sh = pltpu.roll(u3, s, 1)
        u3 = jnp.where(ok, a3 * u_sh + u3, u3)
        a3 = jnp.where(ok, a3 * a_sh, a3)
        s *= 2
    h0 = hN_ref[...]
    groups = []
    for kg in range(n_grp):
        carry = h0[kg // grp_per_seq] if kg % grp_per_seq == 0 else groups[-1][SUBLANES - 1:SUBLANES, :]
        groups.append(a3[kg] * carry + u3[kg])
    h = jnp.concatenate(groups, axis=0)

    y_ref[...] = h * jax.nn.gelu(g_ref[...], approximate=True)
    hN_ref[...] = h.reshape(S, L, D_RG)[:, L - 1:L, :]
    cN_ref[...] = xs_scr[:, SUBLANES - (CONV_W - 1):SUBLANES, :]


def _rglru(proj, row0, B, T, S, L, buf8, h0, cw, cb, wa, ba, wx, bx, lam):
    R = S * L
    nb, nt = B // S, T // L
    blk0 = row0 // R
    row_map = lambda b, t: (blk0 + b * nt + t, 0)
    const2 = lambda b, t: (0, 0)
    return pl.pallas_call(
        functools.partial(_rglru_kernel, S=S, L=L),
        grid=(nb, nt),
        in_specs=[
            pl.BlockSpec((R, D_RG), row_map),
            pl.BlockSpec((R, D_RG), lambda b, t: (blk0 + b * nt + t, 1)),
            pl.BlockSpec((S, SUBLANES, D_RG), lambda b, t: (b, 0, 0)),
            pl.BlockSpec((S, 1, D_RG), lambda b, t: (b, 0, 0)),
            pl.BlockSpec((CONV_W, D_RG), const2),
            pl.BlockSpec((1, D_RG), const2),
            pl.BlockSpec((D_RG, D_RG), const2),
            pl.BlockSpec((1, D_RG), const2),
            pl.BlockSpec((D_RG, D_RG), const2),
            pl.BlockSpec((1, D_RG), const2),
            pl.BlockSpec((1, D_RG), const2),
        ],
        out_specs=[
            pl.BlockSpec((R, D_RG), lambda b, t: (b * nt + t, 0)),
            pl.BlockSpec((S, 1, D_RG), lambda b, t: (b, 0, 0)),
            pl.BlockSpec((S, CONV_W - 1, D_RG), lambda b, t: (b, 0, 0)),
        ],
        out_shape=[
            jax.ShapeDtypeStruct((B * T, D_RG), f32),
            jax.ShapeDtypeStruct((B, 1, D_RG), f32),
            jax.ShapeDtypeStruct((B, CONV_W - 1, D_RG), f32),
        ],
        scratch_shapes=[pltpu.VMEM((S, SUBLANES, D_RG), f32)],
        compiler_params=_cparams(2),
        name="rglru",
    )(proj, proj, buf8, h0, cw, cb, wa, ba, wx, bx, lam)


def _seg_scan(x, op, fill, tin, L, reverse=False):
    s = 1
    while s < L:
        if reverse:
            sh = pltpu.roll(x, LANES - s, 1)
            ok = tin < L - s
        else:
            sh = pltpu.roll(x, s, 1)
            ok = tin >= s
        x = op(x, jnp.where(ok, sh, fill))
        s *= 2
    return x


def _mlstm_kernel(*refs, S, L, G, n_riders):
    (x_ref, z_ref, gt_ref, buf_ref, c0_ref, n0_ref, m0_ref,
     cw_ref, cb_ref, wqk_ref, wv_ref, wkT_ref, bif_ref, ng_ref, sk_ref) = refs[:15]
    rider_in = refs[15:15 + n_riders]
    y_ref, cN_ref, nN_ref, mN_ref, bN_ref = refs[15 + n_riders:20 + n_riders]
    rider_out = refs[20 + n_riders:20 + 2 * n_riders]
    xs_scr, m_scr = refs[20 + 2 * n_riders:]
    for src, dst in zip(rider_in, rider_out):
        dst[...] = src[...].astype(bf16)

    R = S * L
    GS, GR = G * S, G * R
    j = pl.program_id(1)

    @pl.when(j == 0)
    def _():
        xs_scr[...] = buf_ref[...]
        cN_ref[...] = c0_ref[...]
        nN_ref[...] = n0_ref[...]
        m_scr[...] = m0_ref[...]

    x = x_ref[...].reshape(GR, D_M)
    xc = _causal_conv(xs_scr, x, cw_ref, cb_ref, GS, L, D_M)
    xa = xc * jax.nn.sigmoid(xc)
    xab = xa.astype(bf16)
    xb = x.astype(bf16)

    il_parts, f_parts = [], []
    for g in range(G):
        gT = (gt_ref[g] + bif_ref[...]).T
        il_parts.append(gT[0:SUBLANES, :])
        f_parts.append(pltpu.roll(gT[0:SUBLANES, :], M_HEADS, 0))
    il = jnp.concatenate(il_parts, axis=0)
    fl = -_softplus(-jnp.concatenate(f_parts, axis=0))
    tin = lax.broadcasted_iota(jnp.int32, (G * SUBLANES, LANES), 1) % L
    bcum = _seg_scan(fl, jnp.add, 0.0, tin, L)
    a = il - bcum
    m_prev = m_scr[...].reshape(G * SUBLANES, LANES)
    big_m = jnp.maximum(m_prev, _seg_scan(a, jnp.maximum, -jnp.inf, tin, L))
    m_t = bcum + big_m
    if S == 1:
        m_last = jnp.broadcast_to(big_m[:, LANES - 1:LANES], big_m.shape)
    else:
        m_last = _seg_scan(big_m, jnp.maximum, -jnp.inf, tin, L, reverse=True)
    rows = [big_m, jnp.exp(m_prev - big_m), jnp.exp(-m_t), jnp.exp(a - m_last), jnp.exp(m_prev - m_last)]
    mN_ref[...] = m_t.reshape(G, SUBLANES, LANES)
    m_scr[...] = jnp.broadcast_to(m_t[:, LANES - 1:LANES], m_t.shape).reshape(G, SUBLANES, LANES)
    pad_rows = jnp.zeros((LANES - len(rows) * SUBLANES, LANES), f32)
    cols = []
    for g in range(G):
        gs = slice(g * SUBLANES, (g + 1) * SUBLANES)
        cols.append(jnp.concatenate([r[gs] for r in rows] + [pad_rows], axis=0).T)

    def col(g, q, h):
        return cols[g][:, SUBLANES * q + h:SUBLANES * q + h + 1]

    qs, ks, vs, kTs = [], [], [], []
    for h in range(M_HEADS):
        hs = slice(h * M_DH, (h + 1) * M_DH)
        qk_h = jnp.dot(xab[:, hs], wqk_ref[h], preferred_element_type=f32)
        qs.append(qk_h[:, 0:M_DH])
        ks.append(qk_h[:, M_DH:2 * M_DH] * (M_DH ** -0.5))
        vs.append(jnp.dot(xb[:, hs], wv_ref[h], preferred_element_type=f32))
        if S == 1:
            kT_h = lax.dot_general(wkT_ref[h], xab[:, hs], (((1,), (1,)), ((), ())),
                                   preferred_element_type=f32)
            kTs.append((kT_h * (M_DH ** -0.5)).astype(bf16))

    ti = lax.broadcasted_iota(jnp.int32, (R, R), 0)
    si = lax.broadcasted_iota(jnp.int32, (R, R), 1)
    mask = (si <= ti) & ((ti // L) == (si // L))
    ones_b = jnp.ones((R, M_DH), bf16)
    pairs = [(g, h) for g in range(G) for h in range(M_HEADS)]
    blk = lambda g: slice(g * R, (g + 1) * R)
    seqs = [slice(b * L, (b + 1) * L) for b in range(S)]
    q = {(g, h): qs[h][blk(g)] for g, h in pairs}
    k = {(g, h): ks[h][blk(g)] for g, h in pairs}
    v = {(g, h): vs[h][blk(g)] for g, h in pairs}
    qb = {p: q[p].astype(bf16) for p in pairs}
    kb = {p: k[p].astype(bf16) for p in pairs}
    qk = {p: lax.dot_general(qb[p], kb[p], (((1,), (1,)), ((), ())), preferred_element_type=f32)
          for p in pairs}
    sm = {}
    for g, h in pairs:
        a_row = a[g * SUBLANES + h:g * SUBLANES + h + 1, :]
        decay = jnp.exp(jnp.where(mask, a_row - col(g, 0, h), -jnp.inf))
        sm[g, h] = (qk[g, h] * decay).astype(bf16)
    nd = {p: jnp.dot(sm[p], jnp.concatenate([v[p].astype(bf16), ones_b], axis=1),
                     preferred_element_type=f32) for p in pairs}
    c_old = {(g, h, b): cN_ref[g * S + b, h] for g, h in pairs for b in range(S)}
    n_old = {(g, h, b): nN_ref[g * S + b, h] for g, h in pairs for b in range(S)}
    q_c = {(g, h, b): jnp.dot(qb[g, h][seqs[b]], c_old[g, h, b].astype(bf16), preferred_element_type=f32)
           for g, h in pairs for b in range(S)}
    hh = {}
    for g, h in pairs:
        g_col, e_col = col(g, 1, h), col(g, 2, h)
        pieces = []
        for b, rs in enumerate(seqs):
            q_n = jnp.sum(q[g, h][rs] * n_old[g, h, b], axis=1, keepdims=True)
            num = nd[g, h][rs, 0:M_DH] + g_col[rs] * q_c[g, h, b]
            den = nd[g, h][rs, M_DH:2 * M_DH] + g_col[rs] * q_n
            pieces.append(num / jnp.maximum(jnp.abs(den), e_col[rs]))
        hh[g, h] = pieces[0] if S == 1 else jnp.concatenate(pieces, axis=0)
    mu = {p: jnp.mean(hh[p], axis=1, keepdims=True) for p in pairs}
    var = {p: jnp.mean(jnp.square(hh[p] - mu[p]), axis=1, keepdims=True) for p in pairs}
    hn_blocks = [jnp.concatenate([(hh[g, h] - mu[g, h]) * lax.rsqrt(var[g, h] + LN_EPS)
                                  for h in range(M_HEADS)], axis=1) for g in range(G)]
    new_c, new_n = [], []
    for g, h in pairs:
        w_col = col(g, 3, h)
        wv = (w_col * v[g, h]).astype(bf16)
        wk = w_col * k[g, h]
        for b, rs in enumerate(seqs):
            g_end = col(g, 4, h)[(b + 1) * L - 1:(b + 1) * L, :]
            if S == 1:
                kv = jnp.dot(kTs[h][:, blk(g)], wv, preferred_element_type=f32)
            else:
                kv = lax.dot_general(kb[g, h][rs], wv[rs], (((0,), (0,)), ((), ())),
                                     preferred_element_type=f32)
            new_c.append((g * S + b, h, g_end * c_old[g, h, b] + kv))
            new_n.append((g * S + b, h, g_end * n_old[g, h, b] + jnp.sum(wk[rs], axis=0, keepdims=True)))

    hn = jnp.concatenate(hn_blocks, axis=0) * ng_ref[...]
    y = jax.nn.sigmoid(z_ref[...].reshape(GR, D_M)) * (hn + sk_ref[...] * xa)
    y_ref[...] = y.reshape(G, R, D_M)
    for sq, h, val in new_c:
        cN_ref[sq, h] = val
    for sq, h, val in new_n:
        nN_ref[sq, h] = val
    bN_ref[...] = xs_scr[:, SUBLANES - (CONV_W - 1):SUBLANES, :]


def _mlstm(proj, B, T, S, L, G, buf8, c0, n0, m0_lanes, cw, cb, wqk, wv, wkT, bif, ng, sk, riders=()):
    R = S * L
    assert R == M_ROWS == LANES
    nblk, nc = B // S, T // L
    nb = nblk // G
    assert nblk * S == B and nc * L == T and nb * G == nblk and (S == 1 or nc == 1)
    steps = nb * nc
    assert all(r.shape[0] % steps == 0 for r in riders)
    rider_specs = [pl.BlockSpec((r.shape[0] // steps,) + r.shape[1:], lambda b, j: (b * nc + j, 0, 0))
                   for r in riders]
    proj4 = proj.reshape(nblk, nc, R, N_PROJ)
    m0_3 = m0_lanes.reshape(SUBLANES, nblk, LANES).transpose(1, 0, 2)
    const2 = lambda b, j: (0, 0)
    const3 = lambda b, j: (0, 0, 0)
    gate_blk = (2 * D_RG + 2 * D_M) // LANES
    GS = G * S
    outs = pl.pallas_call(
        functools.partial(_mlstm_kernel, S=S, L=L, G=G, n_riders=len(riders)),
        grid=(nb, nc),
        in_specs=[
            pl.BlockSpec((G, None, R, D_M), lambda b, j: (b, j, 0, 2)),
            pl.BlockSpec((G, None, R, D_M), lambda b, j: (b, j, 0, 3)),
            pl.BlockSpec((G, None, R, LANES), lambda b, j: (b, j, 0, gate_blk)),
            pl.BlockSpec((GS, SUBLANES, D_M), lambda b, j: (b, 0, 0)),
            pl.BlockSpec((GS, M_HEADS, M_DH, M_DH), lambda b, j: (b, 0, 0, 0)),
            pl.BlockSpec((GS, M_HEADS, 1, M_DH), lambda b, j: (b, 0, 0, 0)),
            pl.BlockSpec((G, SUBLANES, LANES), lambda b, j: (b, 0, 0)),
            pl.BlockSpec((CONV_W, D_M), const2),
            pl.BlockSpec((1, D_M), const2),
            pl.BlockSpec((M_HEADS, M_DH, 2 * M_DH), const3),
            pl.BlockSpec((M_HEADS, M_DH, M_DH), const3),
            pl.BlockSpec((M_HEADS, M_DH, M_DH), const3),
            pl.BlockSpec((1, LANES), const2),
            pl.BlockSpec((1, D_M), const2),
            pl.BlockSpec((1, D_M), const2),
        ] + rider_specs,
        out_specs=[
            pl.BlockSpec((G, None, R, D_M), lambda b, j: (b, j, 0, 0)),
            pl.BlockSpec((GS, M_HEADS, M_DH, M_DH), lambda b, j: (b, 0, 0, 0)),
            pl.BlockSpec((GS, M_HEADS, 1, M_DH), lambda b, j: (b, 0, 0, 0)),
            pl.BlockSpec((G, None, SUBLANES, LANES), lambda b, j: (b, j, 0, 0)),
            pl.BlockSpec((GS, CONV_W - 1, D_M), lambda b, j: (b, 0, 0)),
        ] + rider_specs,
        out_shape=[
            jax.ShapeDtypeStruct((nblk, nc, R, D_M), f32),
            jax.ShapeDtypeStruct((B, M_HEADS, M_DH, M_DH), f32),
            jax.ShapeDtypeStruct((B, M_HEADS, 1, M_DH), f32),
            jax.ShapeDtypeStruct((nblk, nc, SUBLANES, LANES), f32),
            jax.ShapeDtypeStruct((B, CONV_W - 1, D_M), f32),
        ] + [jax.ShapeDtypeStruct(r.shape, bf16) for r in riders],
        scratch_shapes=[pltpu.VMEM((GS, SUBLANES, D_M), f32), pltpu.VMEM((G, SUBLANES, LANES), f32)],
        compiler_params=_cparams(2),
        name="mlstm",
    )(proj4, proj4, proj4, buf8, c0, n0, m0_3, cw, cb, wqk, wv, wkT, bif, ng, sk, *riders)
    y, c_new, n_new, m_t, b_new = outs[:5]
    m_last = m_t[:, nc - 1, :M_HEADS, :].reshape(nblk, M_HEADS, S, L)[:, :, :, L - 1]
    m_last = m_last.transpose(0, 2, 1).reshape(B, M_HEADS)
    return (y.reshape(B * T, D_M), c_new, n_new, m_last, b_new) + tuple(outs[5:])


def _layer_norm(x, g, b):
    mu = jnp.mean(x, axis=-1, keepdims=True)
    var = jnp.mean(jnp.square(x - mu), axis=-1, keepdims=True)
    return (x - mu) * lax.rsqrt(var + LN_EPS) * g + b


def _first_lane_of_max(vals, lane_f):
    vmax = jnp.max(vals, axis=1, keepdims=True)
    idx = jnp.min(jnp.where(vals == vmax, lane_f, float(LANES)), axis=1, keepdims=True)
    return vmax, idx


def _outproj_kernel(xp_ref, xs_ref, rgp_ref, rgs_ref, mp_ref, ms_ref, wo_ref, g1_ref, b1_ref,
                    wr_ref, br_ref, x1_ref, cnt_ref, *, n_p):
    i = pl.program_id(0)

    @pl.when(i == 0)
    def _():
        cnt_ref[...] = jnp.zeros_like(cnt_ref)

    def route(lg, lane, lane_f):
        neg = -jnp.inf
        gl = jnp.where(lane < N_GROUPS, lg, neg)
        gmax, gidx = _first_lane_of_max(gl, lane_f)
        p_g = 1.0 / jnp.sum(jnp.exp(gl - gmax), axis=1, keepdims=True)
        e_lo = float(N_GROUPS) + float(EXPERTS_PER_GROUP) * gidx
        el = jnp.where((lane_f >= e_lo) & (lane_f < e_lo + float(EXPERTS_PER_GROUP)), lg, neg)
        v1, i1 = _first_lane_of_max(el, lane_f)
        v2, i2 = _first_lane_of_max(jnp.where(lane_f == i1, neg, el), lane_f)
        d = jnp.exp(v2 - v1)
        w1 = p_g / (1.0 + d)
        w2 = p_g * d / (1.0 + d)
        first_low = i1 < i2
        j_lo = jnp.minimum(i1, i2) - e_lo
        j_hi = jnp.maximum(i1, i2) - e_lo
        n_first = float(EXPERTS_PER_GROUP - 1) * j_lo - 0.5 * j_lo * (j_lo - 1.0)
        cls = float(PAIRS_PER_GROUP) * gidx + n_first + (j_hi - j_lo - 1.0)
        return cls, jnp.where(first_low, w1, w2), jnp.where(first_low, w2, w1)

    def run(x_ref, rg_ref, m_ref):
        rp = TM_OUT // OUT_PARTS
        parts = [slice(p * rp, (p + 1) * rp) for p in range(OUT_PARTS)]
        lane = lax.broadcasted_iota(jnp.int32, (rp, LANES), 1)
        lane_f = lane.astype(f32)
        heads = [jnp.concatenate([rg_ref[r, :].astype(bf16), m_ref[r, :].astype(bf16)], axis=1) for r in parts]
        mix = [jnp.dot(h, wo_ref[...], preferred_element_type=f32) for h in heads]
        x1 = [_layer_norm(ALPHA * x_ref[r, :] + mx, g1_ref[...], b1_ref[...]) for r, mx in zip(parts, mix)]
        for r, v in zip(parts, x1):
            x1_ref[r, 0:D_MODEL] = v

        hi = [v.astype(bf16) for v in x1]
        lo = [(v - h.astype(f32)).astype(bf16) for v, h in zip(x1, hi)]
        hi_terms = [jnp.dot(h, wr_ref[...], preferred_element_type=f32) for h in hi]
        lo_term = [jnp.dot(v, wr_ref[:, 0:LANES], preferred_element_type=f32) for v in lo]
        lg = [ht[:, 0:LANES] + ht[:, LANES:2 * LANES] + lt + br_ref[...] for ht, lt in zip(hi_terms, lo_term)]
        routed = [route(v, lane, lane_f) for v in lg]

        ti = lax.broadcasted_iota(jnp.int32, (rp, rp), 0)
        si = lax.broadcasted_iota(jnp.int32, (rp, rp), 1)
        tri = (si <= ti).astype(bf16)
        hot = [lane_f == cls for cls, _, _ in routed]
        cum = [jnp.dot(tri, h.astype(bf16), preferred_element_type=f32) for h in hot]
        seen = cnt_ref[...]
        for r, h, cm, (cls, w_lo, w_hi) in zip(parts, hot, cum, routed):
            rank = jnp.sum(jnp.where(h, cm - 1.0 + seen, 0.0), axis=1, keepdims=True)
            seen = seen + cm[rp - 1:rp, :]
            info = jnp.zeros((rp, LANES), f32)
            for c, val in enumerate((cls, rank, w_lo, w_hi)):
                info = jnp.where(lane == c, val, info)
            x1_ref[r, D_MODEL:D_MODEL + LANES] = info
        cnt_ref[...] = seen

    @pl.when(i < n_p)
    def _():
        run(xp_ref, rgp_ref, mp_ref)

    @pl.when(i >= n_p)
    def _():
        run(xs_ref, rgs_ref, ms_ref)


def _outproj(xp, xs, rgp, rgs, mp, ms, wo, g1, b1, wr, br):
    n_p, n_s = xp.shape[0] // TM_OUT, xs.shape[0] // TM_OUT
    n = n_p + n_s
    pmap = lambda i: (jnp.minimum(i, n_p - 1), 0)
    smap = lambda i: (jnp.maximum(i - n_p, 0), 0)
    const = lambda i: (0, 0)
    return pl.pallas_call(
        functools.partial(_outproj_kernel, n_p=n_p),
        grid=(n,),
        in_specs=[
            pl.BlockSpec((TM_OUT, D_MODEL), pmap), pl.BlockSpec((TM_OUT, D_MODEL), smap),
            pl.BlockSpec((TM_OUT, D_RG), pmap), pl.BlockSpec((TM_OUT, D_RG), smap),
            pl.BlockSpec((TM_OUT, D_M), pmap), pl.BlockSpec((TM_OUT, D_M), smap),
            pl.BlockSpec((D_RG + D_M, D_MODEL), const),
            pl.BlockSpec((1, D_MODEL), const), pl.BlockSpec((1, D_MODEL), const),
            pl.BlockSpec((D_MODEL, 2 * LANES), const),
            pl.BlockSpec((1, LANES), const),
        ],
        out_specs=[
            pl.BlockSpec((TM_OUT, X_ROW), lambda i: (i, 0)),
            pl.BlockSpec((1, LANES), const),
        ],
        out_shape=[
            jax.ShapeDtypeStruct((n * TM_OUT, X_ROW), f32),
            jax.ShapeDtypeStruct((1, LANES), f32),
        ],
        compiler_params=_cparams(1),
        name="outproj",
    )(xp, xs, rgp, rgs, mp, ms, wo, g1, b1, wr, br)


GRP_MOVE = TM_MOVE // SUBLANES


def _grouped(x):
    return x.reshape(x.shape[0] // SUBLANES, SUBLANES, x.shape[1])


def _tile_rows_wait(hbm_ref, sem):
    rows = hbm_ref.at[pl.ds(0, TM_MOVE)]
    pltpu.make_async_copy(rows, rows, sem).wait()


def _for_each_tile_row(body):
    def group(k, c):
        for j in range(SUBLANES):
            body(k, j, k * SUBLANES + j)
        return c

    lax.fori_loop(0, GRP_MOVE, group, 0)


def _dispatch_kernel(pad_ref, pos_ref, x1_ref, xs_ref, zero_scr, xbuf, sem, zsem, lsem):
    i = pl.program_id(0)
    n = pl.num_programs(0)

    def zero_tile(t):
        return pltpu.make_async_copy(zero_scr, xs_ref.at[pl.ds(pl.multiple_of(t * TM_MOE, TM_MOE), TM_MOE)], zsem)

    def zero_chunk(row0):
        return pltpu.make_async_copy(zero_scr.at[pl.ds(0, SUBLANES)],
                                     xs_ref.at[pl.ds(pl.multiple_of(row0, SUBLANES), SUBLANES)], zsem)

    @pl.when(i == 0)
    def _():
        zero_scr[...] = jnp.zeros_like(zero_scr)
        n_used = pad_ref[2 * N_CLASSES]
        n_tiles = xs_ref.shape[0] // TM_MOE
        for go in (lambda cp: cp.start(), lambda cp: cp.wait()):
            lax.fori_loop(n_used, n_tiles, lambda t, c: (go(zero_tile(t)), c)[1], 0)

            def per_class(cl, c):
                row0 = pad_ref[2 * cl]
                lax.fori_loop(0, pad_ref[2 * cl + 1],
                              lambda k, c2: (go(zero_chunk(row0 + k * SUBLANES)), c2)[1], 0)
                return c

            lax.fori_loop(0, N_CLASSES, per_class, 0)

    def load(t, s):
        return pltpu.make_async_copy(x1_ref.at[pl.ds(t * GRP_MOVE, GRP_MOVE)], xbuf.at[s], lsem.at[s])

    @pl.when(i == 0)
    def _():
        load(0, 0).start()

    @pl.when(i + 1 < n)
    def _():
        load(i + 1, (i + 1) % 3).start()

    slot = i % 3
    load(i, slot).wait()

    def start(k, j, r):
        pltpu.make_async_copy(xbuf.at[slot, k, pl.ds(j, 1)], xs_ref.at[pl.ds(pos_ref[0, 0, r], 1)],
                              sem.at[slot]).start()

    _for_each_tile_row(start)

    @pl.when(i > 0)
    def _():
        _tile_rows_wait(xs_ref, sem.at[(i + 2) % 3])

    @pl.when(i == n - 1)
    def _():
        _tile_rows_wait(xs_ref, sem.at[slot])


def _dispatch(pad_plan, pos, x1, n_rows):
    n = x1.shape[0] // TM_MOVE
    return pl.pallas_call(
        _dispatch_kernel,
        grid_spec=pltpu.PrefetchScalarGridSpec(
            num_scalar_prefetch=1,
            grid=(n,),
            in_specs=[
                pl.BlockSpec((1, 1, TM_MOVE), lambda i, *_: (i, 0, 0), memory_space=pltpu.SMEM),
                pl.BlockSpec(memory_space=pl.ANY),
            ],
            out_specs=pl.BlockSpec(memory_space=pl.ANY),
            scratch_shapes=[pltpu.VMEM((TM_MOE, X_ROW), f32),
                            pltpu.VMEM((3, GRP_MOVE, SUBLANES, X_ROW), f32),
                            pltpu.SemaphoreType.DMA((3,)), pltpu.SemaphoreType.DMA(()),
                            pltpu.SemaphoreType.DMA((3,))],
        ),
        out_shape=jax.ShapeDtypeStruct((n_rows, X_ROW), f32),
        compiler_params=_cparams(1),
        name="dispatch",
    )(pad_plan, pos, _grouped(x1))


def _expert_kernel(ta_ref, tb_ref, ca_ref, cb_ref, sa_ref, sb_ref, nu_ref,
                   x_ref, wg_hbm, wu_hbm, wd_hbm, g2_ref, b2_ref, y_ref,
                   wga, wua, wda, wgb, wub, wdb, sem_a, sem_b):
    i = pl.program_id(0)
    n_used = nu_ref[0]
    first = i * MOE_TILES
    side_a = (ta_ref, ca_ref, sa_ref, (wga, wua, wda), sem_a)
    side_b = (tb_ref, cb_ref, sb_ref, (wgb, wub, wdb), sem_b)

    def copies(side, j):
        t_ref, _, s_ref, bufs, sem = side
        e, s = t_ref[j], s_ref[j]
        return [pltpu.make_async_copy(w.at[e], buf.at[s], sem.at[s])
                for w, buf in zip((wg_hbm, wu_hbm, wd_hbm), bufs)]

    def fetch(j):
        @pl.when(j < n_used)
        def _():
            for side in (side_a, side_b):
                @pl.when(side[1][j] == 1)
                def _():
                    for cp in copies(side, j):
                        cp.start()

    @pl.when(i == 0)
    def _():
        for j in range(W_AHEAD):
            fetch(j)

    for t in range(MOE_TILES):
        fetch(first + W_AHEAD + t)

    @pl.when(first < n_used)
    def _():
        for t in range(MOE_TILES):
            for side in (side_a, side_b):
                @pl.when(side[1][first + t] == 1)
                def _():
                    for cp in copies(side, first + t):
                        cp.wait()
        rows = [slice(t * TM_MOE, (t + 1) * TM_MOE) for t in range(MOE_TILES)]
        x = [x_ref[r, 0:D_MODEL] for r in rows]
        info = [x_ref[r, D_MODEL:X_ROW] for r in rows]
        xb = [v.astype(bf16) for v in x]
        units = [(t, bufs, s_ref[first + t]) for t in range(MOE_TILES)
                 for bufs, s_ref in (((wga, wua, wda), sa_ref), ((wgb, wub, wdb), sb_ref))]
        hg = [jnp.dot(xb[t], bufs[0][s], preferred_element_type=f32) for t, bufs, s in units]
        hu = [jnp.dot(xb[t], bufs[1][s], preferred_element_type=f32) for t, bufs, s in units]
        mid = [(g * jax.nn.sigmoid(g) * u).astype(bf16) for g, u in zip(hg, hu)]
        ys = [jnp.dot(m, bufs[2][s], preferred_element_type=f32)
              for m, (t, bufs, s) in zip(mid, units)]
        for t in range(MOE_TILES):
            ffn = info[t][:, 2:3] * ys[2 * t] + info[t][:, 3:4] * ys[2 * t + 1]
            y_ref[rows[t], :] = _layer_norm(ALPHA * x[t] + ffn, g2_ref[...], b2_ref[...])

    @pl.when(first >= n_used)
    def _():
        y_ref[...] = jnp.zeros_like(y_ref)


def _experts(tile_ea, tile_eb, n_used, xs, wg, wu, wd, g2, b2):
    nt = xs.shape[0] // TM_MOE

    def ring_plan(tile_e):
        opens = jnp.concatenate([jnp.ones((1,), jnp.int32), (tile_e[1:] != tile_e[:-1]).astype(jnp.int32)])
        return opens, (jnp.cumsum(opens) - 1) % W_SLOTS

    open_a, slot_a = ring_plan(tile_ea)
    open_b, slot_b = ring_plan(tile_eb)
    assert nt % MOE_TILES == 0
    n_used = ((n_used + MOE_TILES - 1) // MOE_TILES) * MOE_TILES
    const = lambda i, *_: (0, 0)
    xmap = lambda i, ta, tb, ca, cb, sa, sb, nu: (jnp.minimum(i, nu[0] // MOE_TILES - 1), 0)
    w_in = pltpu.VMEM((W_SLOTS, D_MODEL, D_EXPERT), bf16)
    w_out = pltpu.VMEM((W_SLOTS, D_EXPERT, D_MODEL), bf16)
    return pl.pallas_call(
        _expert_kernel,
        grid_spec=pltpu.PrefetchScalarGridSpec(
            num_scalar_prefetch=7,
            grid=(nt // MOE_TILES,),
            in_specs=[
                pl.BlockSpec((MOE_TILES * TM_MOE, X_ROW), xmap),
                pl.BlockSpec(memory_space=pl.ANY), pl.BlockSpec(memory_space=pl.ANY),
                pl.BlockSpec(memory_space=pl.ANY),
                pl.BlockSpec((1, D_MODEL), const), pl.BlockSpec((1, D_MODEL), const),
            ],
            out_specs=pl.BlockSpec((MOE_TILES * TM_MOE, D_MODEL), lambda i, *_: (i, 0)),
            scratch_shapes=[w_in, w_in, w_out, w_in, w_in, w_out,
                            pltpu.SemaphoreType.DMA((W_SLOTS,)), pltpu.SemaphoreType.DMA((W_SLOTS,))],
        ),
        out_shape=jax.ShapeDtypeStruct((nt * TM_MOE, D_MODEL), f32),
        compiler_params=_cparams(1),
        name="experts",
    )(tile_ea, tile_eb, open_a, open_b, slot_a.astype(jnp.int32), slot_b.astype(jnp.int32), n_used,
      xs, wg, wu, wd, g2, b2)


def _collect_kernel(pos_ref, posn_ref, ys_ref, op_ref, os_ref, ybuf, sem, *, n_p):
    i = pl.program_id(0)
    n = pl.num_programs(0)
    slot = i % 2

    def gather(p_ref, s):
        def start(k, j, r):
            pltpu.make_async_copy(ys_ref.at[pl.ds(p_ref[0, 0, r], 1)], ybuf.at[s, k, pl.ds(j, 1)],
                                  sem.at[s]).start()

        _for_each_tile_row(start)

    @pl.when(i == 0)
    def _():
        gather(pos_ref, 0)

    @pl.when(i + 1 < n)
    def _():
        gather(posn_ref, 1 - slot)

    _tile_rows_wait(ys_ref, sem.at[slot])

    @pl.when(i < n_p)
    def _():
        op_ref[...] = ybuf[slot].reshape(TM_MOVE, D_MODEL)

    @pl.when(i >= n_p)
    def _():
        os_ref[...] = ybuf[slot].reshape(TM_MOVE, D_MODEL)


def _collect(pos, ys, n_p):
    n = pos.shape[0]
    n_s = n - n_p
    return pl.pallas_call(
        functools.partial(_collect_kernel, n_p=n_p),
        grid=(n,),
        in_specs=[
            pl.BlockSpec((1, 1, TM_MOVE), lambda i: (i, 0, 0), memory_space=pltpu.SMEM),
            pl.BlockSpec((1, 1, TM_MOVE), lambda i: (jnp.minimum(i + 1, n - 1), 0, 0),
                         memory_space=pltpu.SMEM),
            pl.BlockSpec(memory_space=pl.ANY),
        ],
        out_specs=[
            pl.BlockSpec((TM_MOVE, D_MODEL), lambda i: (jnp.minimum(i, n_p - 1), 0)),
            pl.BlockSpec((TM_MOVE, D_MODEL), lambda i: (jnp.maximum(i - n_p, 0), 0)),
        ],
        out_shape=[
            jax.ShapeDtypeStruct((n_p * TM_MOVE, D_MODEL), f32),
            jax.ShapeDtypeStruct((n_s * TM_MOVE, D_MODEL), f32),
        ],
        scratch_shapes=[pltpu.VMEM((2, GRP_MOVE, SUBLANES, D_MODEL), f32), pltpu.SemaphoreType.DMA((2,))],
        compiler_params=_cparams(1),
        name="collect",
    )(pos, pos, ys)


def _block_diag(w):
    n, d, _ = w.shape
    eye = jnp.eye(n, dtype=w.dtype)
    return (eye[:, None, :, None] * w[:, :, None, :]).reshape(n * d, n * d)


def _pair_walk(n):
    total = n * (n - 1) // 2

    def extend(path, used):
        if len(path) == total:
            return path
        a, b = path[-1]
        for c in range(n):
            for nxt in ((a, c), (c, b)):
                key = frozenset(nxt)
                if len(key) == 2 and key not in used:
                    out = extend(path + [nxt], used | {key})
                    if out:
                        return out
        return None

    return extend([(0, 1)], {frozenset((0, 1))})


def _pad_history(buf):
    return jnp.pad(buf, ((0, 0), (SUBLANES - (CONV_W - 1), 0), (0, 0)))


def kernel(x_prompt, x_sample, state_rg_h, state_rg_conv, state_m_C, state_m_n, state_m_m, state_m_conv, w_in, rg_conv_w, rg_conv_b, rg_w_a, rg_b_a, rg_w_x, rg_b_x, rg_lambda, m_conv_w, m_conv_b, m_w_q, m_w_k, m_w_v, m_b_i, m_b_f, m_norm_g, m_skip, w_out, ln1_g, ln1_b, ln2_g, ln2_b, moe_w_group, moe_b_group, moe_w_expert, moe_b_expert, moe_w_gate, moe_w_up, moe_w_down):
    BP, TP, _ = x_prompt.shape
    BS, TS, _ = x_sample.shape
    n_prompt, n_sample = BP * TP, BS * TS
    xp = x_prompt.reshape(n_prompt, D_MODEL)
    xs = x_sample.reshape(n_sample, D_MODEL)
    l = 0

    w_in_p = jnp.pad(w_in[l], ((0, 0), (0, N_PROJ - w_in.shape[-1]))).astype(bf16)
    wa = _block_diag(rg_w_a[l]).astype(bf16)
    wx = _block_diag(rg_w_x[l]).astype(bf16)
    row = lambda v: v.reshape(1, -1)
    bif = jnp.pad(jnp.concatenate([m_b_i[l], m_b_f[l]]), (0, LANES - 2 * M_HEADS)).reshape(1, LANES)
    wqk = jnp.concatenate([m_w_q[l], m_w_k[l]], axis=-1).astype(bf16)
    wv = m_w_v[l].astype(bf16)
    wkT = m_w_k[l].transpose(0, 2, 1).astype(bf16)
    w_route = jnp.pad(jnp.concatenate([moe_w_group[l], moe_w_expert[l]], axis=1),
                      ((0, 0), (0, LANES - N_GROUPS - N_EXPERTS)))
    wr_hi = w_route.astype(bf16)
    wr_lo = (w_route - wr_hi.astype(f32)).astype(bf16)
    wr = jnp.concatenate([wr_hi, wr_lo], axis=1)
    b_route = jnp.pad(jnp.concatenate([moe_b_group[l], moe_b_expert[l]]),
                      (0, LANES - N_GROUPS - N_EXPERTS)).reshape(1, LANES)

    proj_p, proj_s = _inproj(xp, xs, w_in_p)

    rg_args = (rg_conv_w[l], row(rg_conv_b[l]), wa, row(rg_b_a[l]), wx, row(rg_b_x[l]), row(rg_lambda[l]))
    zeros = lambda *s: jnp.zeros(s, f32)
    s_blk = M_ROWS // TS
    yrg_p, p_rg_h, p_rg_conv = _rglru(proj_p, 0, BP, TP, 1, RG_ROWS, zeros(BP, SUBLANES, D_RG),
                                      zeros(BP, 1, D_RG), *rg_args)
    yrg_s, s_rg_h, s_rg_conv = _rglru(proj_s, 0, BS, TS, RG_ROWS // TS, TS,
                                      _pad_history(state_rg_conv[l]), state_rg_h[l].reshape(BS, 1, D_RG), *rg_args)

    m_args = (m_conv_w[l], row(m_conv_b[l]), wqk, wv, wkT, bif, row(m_norm_g[l]), row(m_skip[l]))
    m0_p = jnp.full((SUBLANES, BP * LANES), M_INIT, f32)
    ym_p, p_m_C, p_m_n, p_m_m, p_m_conv, wg_b, wu_b, wd_b = _mlstm(
        proj_p, BP, TP, 1, M_ROWS, M_BLOCKS_PROMPT, zeros(BP, SUBLANES, D_M), zeros(BP, M_HEADS, M_DH, M_DH),
        zeros(BP, M_HEADS, 1, M_DH), m0_p, *m_args, riders=(moe_w_gate[l], moe_w_up[l], moe_w_down[l]))
    m0_s = jnp.pad(jnp.repeat(state_m_m[l].T, TS, axis=1), ((0, SUBLANES - M_HEADS), (0, 0)))
    ym_s, s_m_C, s_m_n, s_m_m, s_m_conv = _mlstm(
        proj_s, BS, TS, s_blk, TS, M_BLOCKS_SAMPLE, _pad_history(state_m_conv[l]), state_m_C[l],
        state_m_n[l].reshape(BS, M_HEADS, 1, M_DH), m0_s, *m_args)

    x1, cnt = _outproj(xp, xs, yrg_p, yrg_s, ym_p, ym_s, w_out[l].astype(bf16),
                       row(ln1_g[l]), row(ln1_b[l]), wr, b_route)

    n_tok = n_prompt + n_sample
    n_tiles = n_tok // TM_MOE + N_CLASSES
    lay_cls, lay_a, lay_b = [], [], []
    for g in range(N_GROUPS):
        for ja, jb in _pair_walk(EXPERTS_PER_GROUP):
            assert ja < jb
            n_first = (EXPERTS_PER_GROUP - 1) * ja - ja * (ja - 1) // 2
            lay_cls.append(PAIRS_PER_GROUP * g + n_first + jb - ja - 1)
            lay_a.append(g * EXPERTS_PER_GROUP + ja)
            lay_b.append(g * EXPERTS_PER_GROUP + jb)
    classes = jnp.arange(N_CLASSES, dtype=jnp.int32)
    in_slot = jnp.array(lay_cls, jnp.int32)[:, None] == classes
    counts = jnp.sum(jnp.where(in_slot, cnt[0, :N_CLASSES].astype(jnp.int32), 0), axis=1)
    padded = ((counts + TM_MOE - 1) // TM_MOE) * TM_MOE
    ends = jnp.cumsum(padded)
    offs = ends - padded
    offs_of_cls = jnp.sum(jnp.where(in_slot, offs[:, None], 0), axis=0)
    cls = x1[:, D_MODEL].astype(jnp.int32)
    rank = x1[:, D_MODEL + 1].astype(jnp.int32)
    pos = jnp.sum(jnp.where(cls[:, None] == classes, offs_of_cls, 0), axis=-1) + rank
    pos = pos.reshape(n_tok // TM_MOVE, 1, TM_MOVE)
    tiles = jnp.arange(n_tiles, dtype=jnp.int32)
    tile_slot = jnp.minimum(jnp.sum(tiles[:, None] >= (ends // TM_MOE)[None, :], axis=1), N_CLASSES - 1)
    on_slot = tile_slot[:, None] == classes
    tile_ea = jnp.sum(jnp.where(on_slot, jnp.array(lay_a, jnp.int32), 0), axis=-1).astype(jnp.int32)
    tile_eb = jnp.sum(jnp.where(on_slot, jnp.array(lay_b, jnp.int32), 0), axis=-1).astype(jnp.int32)
    n_used = (ends[-1] // TM_MOE).reshape(1).astype(jnp.int32)
    pad_row0 = ((offs + counts) // SUBLANES) * SUBLANES
    pad_plan = jnp.concatenate([jnp.stack([pad_row0, (ends - pad_row0) // SUBLANES], axis=1).reshape(-1),
                                     n_used]).astype(jnp.int32)

    x_sorted = _dispatch(pad_plan, pos, x1, n_tiles * TM_MOE)
    y_sorted = _experts(tile_ea, tile_eb, n_used, x_sorted, wg_b, wu_b, wd_b,
                        row(ln2_g[l]), row(ln2_b[l]))
    y_p, y_s = _collect(pos, y_sorted, n_prompt // TM_MOVE)

    return (y_p.reshape(BP, TP, D_MODEL), y_s.reshape(BS, TS, D_MODEL),
            p_rg_h.reshape(1, BP, D_RG), p_rg_conv[None], p_m_C[None], p_m_n.reshape(1, BP, M_HEADS, M_DH),
            p_m_m[None], p_m_conv[None],
            s_rg_h.reshape(1, BS, D_RG), s_rg_conv[None], s_m_C[None], s_m_n.reshape(1, BS, M_HEADS, M_DH),
            s_m_m[None], s_m_conv[None])
```

```python
import functools

import jax
import jax.numpy as jnp
from jax import lax
from jax.experimental import pallas as pl
from jax.experimental.pallas import tpu as pltpu

f32 = jnp.float32
bf16 = jnp.bfloat16

D_MODEL = 1024
D_RG = 512
RG_C = 8.0
D_M = 512
M_HEADS = 4
M_DH = 128
CONV_W = 4
N_GROUPS = 4
EXPERTS_PER_GROUP = 8
N_EXPERTS = 32
D_EXPERT = 256
ALPHA = 2.0 ** 0.25
LN_EPS = 1e-5
M_INIT = -1.0e4

LANES = 128
SUBLANES = 8
TM = 512
TM_MOVE = 1024
TM_OUT = 1024
OUT_PARTS = 8
TM_MOE = 128
MOE_TILES = 4
W_AHEAD = 8
W_SLOTS = W_AHEAD + MOE_TILES
PAIRS_PER_GROUP = EXPERTS_PER_GROUP * (EXPERTS_PER_GROUP - 1) // 2
N_CLASSES = N_GROUPS * PAIRS_PER_GROUP
X_ROW = D_MODEL + LANES
N_PROJ = 2 * D_RG + 2 * D_M + LANES
RG_ROWS = 2048
M_ROWS = 128
M_BLOCKS_PROMPT = 8
M_BLOCKS_SAMPLE = 2
V7X_VMEM_BYTES = 64 * 1024 * 1024
VMEM_LIMIT = V7X_VMEM_BYTES - 8 * 1024 * 1024


def _cparams(n_axes):
    return pltpu.CompilerParams(dimension_semantics=("arbitrary",) * n_axes,
                                vmem_limit_bytes=VMEM_LIMIT)


def _inproj_kernel(xp_ref, xs_ref, w_ref, op_ref, os_ref, *, n_p):
    i = pl.program_id(0)

    def run(x_ref, o_ref):
        o_ref[...] = jnp.dot(x_ref[...].astype(bf16), w_ref[...], preferred_element_type=f32)

    @pl.when(i < n_p)
    def _():
        run(xp_ref, op_ref)

    @pl.when(i >= n_p)
    def _():
        run(xs_ref, os_ref)


def _inproj(xp, xs, w):
    n_p, n_s = xp.shape[0] // TM, xs.shape[0] // TM
    pmap = lambda i: (jnp.minimum(i, n_p - 1), 0)
    smap = lambda i: (jnp.maximum(i - n_p, 0), 0)
    return pl.pallas_call(
        functools.partial(_inproj_kernel, n_p=n_p),
        grid=(n_p + n_s,),
        in_specs=[
            pl.BlockSpec((TM, D_MODEL), pmap),
            pl.BlockSpec((TM, D_MODEL), smap),
            pl.BlockSpec((D_MODEL, N_PROJ), lambda i: (0, 0)),
        ],
        out_specs=[pl.BlockSpec((TM, N_PROJ), pmap), pl.BlockSpec((TM, N_PROJ), smap)],
        out_shape=[jax.ShapeDtypeStruct((n_p * TM, N_PROJ), f32),
                   jax.ShapeDtypeStruct((n_s * TM, N_PROJ), f32)],
        compiler_params=_cparams(1),
        name="inproj",
    )(xp, xs, w)


def _causal_conv(tail_scr, x, cw_ref, cb_ref, S, L, C):
    tail = tail_scr[...]
    sub = lax.broadcasted_iota(jnp.int32, (S, SUBLANES, C), 1)
    acc = cb_ref[...] + cw_ref[CONV_W - 1:CONV_W, :] * x
    for d in range(1, CONV_W):
        back = pltpu.roll(x, d, 0).reshape(S, L, C)
        head = jnp.where(sub < d, pltpu.roll(tail, d, 1), back[:, 0:SUBLANES, :])
        if L > SUBLANES:
            back = jnp.concatenate([head, back[:, SUBLANES:, :]], axis=1)
        else:
            back = head
        acc = acc + cw_ref[CONV_W - 1 - d:CONV_W - d, :] * back.reshape(S * L, C)
    tail_scr[...] = x.reshape(S, L, C)[:, L - SUBLANES:, :]
    return acc


def _softplus(x):
    return jnp.maximum(x, 0.0) + jnp.log1p(jnp.exp(-jnp.abs(x)))


def _rglru_kernel(x_ref, g_ref, buf_ref, h0_ref, cw_ref, cb_ref, wa_ref, ba_ref, wx_ref, bx_ref,
                  lam_ref, y_ref, hN_ref, cN_ref, xs_scr, *, S, L):
    R = S * L
    t = pl.program_id(1)

    @pl.when(t == 0)
    def _():
        xs_scr[...] = buf_ref[...]
        hN_ref[...] = h0_ref[...]

    x = x_ref[...]
    xc = _causal_conv(xs_scr, x, cw_ref, cb_ref, S, L, D_RG)
    xcb = xc.astype(bf16)
    r = jax.nn.sigmoid(jnp.dot(xcb, wa_ref[...], preferred_element_type=f32) + ba_ref[...])
    ig = jax.nn.sigmoid(jnp.dot(xcb, wx_ref[...], preferred_element_type=f32) + bx_ref[...])
    log_a = (-RG_C) * r * _softplus(-lam_ref[...])
    a = jnp.exp(log_a)
    th = jnp.tanh(log_a)
    u = jnp.sqrt(-2.0 * th / (1.0 - th)) * ig * xc

    n_grp, grp_per_seq = R // SUBLANES, L // SUBLANES
    a3 = a.reshape(n_grp, SUBLANES, D_RG)
    u3 = u.reshape(n_grp, SUBLANES, D_RG)
    sub = lax.broadcasted_iota(jnp.int32, (n_grp, SUBLANES, D_RG), 1)
    s = 1
    while s < SUBLANES:
        ok = sub >= s
        a_sh = pltpu.roll(a3, s, 1)
        u_sh = pltpu.roll(u3, s, 1)
        u3 = jnp.where(ok, a3 * u_sh + u3, u3)
        a3 = jnp.where(ok, a3 * a_sh, a3)
        s *= 2
    h0 = hN_ref[...]
    groups = []
    for kg in range(n_grp):
        carry = h0[kg // grp_per_seq] if kg % grp_per_seq == 0 else groups[-1][SUBLANES - 1:SUBLANES, :]
        groups.append(a3[kg] * carry + u3[kg])
    h = jnp.concatenate(groups, axis=0)

    y_ref[...] = h * jax.nn.gelu(g_ref[...], approximate=True)
    hN_ref[...] = h.reshape(S, L, D_RG)[:, L - 1:L, :]
    cN_ref[...] = xs_scr[:, SUBLANES - (CONV_W - 1):SUBLANES, :]


def _rglru(proj, row0, B, T, S, L, buf8, h0, cw, cb, wa, ba, wx, bx, lam):
    R = S * L
    nb, nt = B // S, T // L
    blk0 = row0 // R
    row_map = lambda b, t: (blk0 + b * nt + t, 0)
    const2 = lambda b, t: (0, 0)
    return pl.pallas_call(
        functools.partial(_rglru_kernel, S=S, L=L),
        grid=(nb, nt),
        in_specs=[
            pl.BlockSpec((R, D_RG), row_map),
            pl.BlockSpec((R, D_RG), lambda b, t: (blk0 + b * nt + t, 1)),
            pl.BlockSpec((S, SUBLANES, D_RG), lambda b, t: (b, 0, 0)),
            pl.BlockSpec((S, 1, D_RG), lambda b, t: (b, 0, 0)),
            pl.BlockSpec((CONV_W, D_RG), const2),
            pl.BlockSpec((1, D_RG), const2),
            pl.BlockSpec((D_RG, D_RG), const2),
            pl.BlockSpec((1, D_RG), const2),
            pl.BlockSpec((D_RG, D_RG), const2),
            pl.BlockSpec((1, D_RG), const2),
            pl.BlockSpec((1, D_RG), const2),
        ],
        out_specs=[
            pl.BlockSpec((R, D_RG), lambda b, t: (b * nt + t, 0)),
            pl.BlockSpec((S, 1, D_RG), lambda b, t: (b, 0, 0)),
            pl.BlockSpec((S, CONV_W - 1, D_RG), lambda b, t: (b, 0, 0)),
        ],
        out_shape=[
            jax.ShapeDtypeStruct((B * T, D_RG), f32),
            jax.ShapeDtypeStruct((B, 1, D_RG), f32),
            jax.ShapeDtypeStruct((B, CONV_W - 1, D_RG), f32),
        ],
        scratch_shapes=[pltpu.VMEM((S, SUBLANES, D_RG), f32)],
        compiler_params=_cparams(2),
        name="rglru",
    )(proj, proj, buf8, h0, cw, cb, wa, ba, wx, bx, lam)


def _seg_scan(x, op, fill, tin, L, reverse=False):
    s = 1
    while s < L:
        if reverse:
            sh = pltpu.roll(x, LANES - s, 1)
            ok = tin < L - s
        else:
            sh = pltpu.roll(x, s, 1)
            ok = tin >= s
        x = op(x, jnp.where(ok, sh, fill))
        s *= 2
    return x


def _mlstm_kernel(*refs, S, L, G, n_riders):
    (x_ref, z_ref, gt_ref, buf_ref, c0_ref, n0_ref, m0_ref,
     cw_ref, cb_ref, wqk_ref, wv_ref, wkT_ref, bif_ref, ng_ref, sk_ref) = refs[:15]
    rider_in = refs[15:15 + n_riders]
    y_ref, cN_ref, nN_ref, mN_ref, bN_ref = refs[15 + n_riders:20 + n_riders]
    rider_out = refs[20 + n_riders:20 + 2 * n_riders]
    xs_scr, m_scr = refs[20 + 2 * n_riders:]
    for src, dst in zip(rider_in, rider_out):
        dst[...] = src[...].astype(bf16)

    R = S * L
    GS, GR = G * S, G * R
    j = pl.program_id(1)

    @pl.when(j == 0)
    def _():
        xs_scr[...] = buf_ref[...]
        cN_ref[...] = c0_ref[...]
        nN_ref[...] = n0_ref[...]
        m_scr[...] = m0_ref[...]

    x = x_ref[...].reshape(GR, D_M)
    xc = _causal_conv(xs_scr, x, cw_ref, cb_ref, GS, L, D_M)
    xa = xc * jax.nn.sigmoid(xc)
    xab = xa.astype(bf16)
    xb = x.astype(bf16)

    il_parts, f_parts = [], []
    for g in range(G):
        gT = (gt_ref[g] + bif_ref[...]).T
        il_parts.append(gT[0:SUBLANES, :])
        f_parts.append(pltpu.roll(gT[0:SUBLANES, :], M_HEADS, 0))
    il = jnp.concatenate(il_parts, axis=0)
    fl = -_softplus(-jnp.concatenate(f_parts, axis=0))
    tin = lax.broadcasted_iota(jnp.int32, (G * SUBLANES, LANES), 1) % L
    bcum = _seg_scan(fl, jnp.add, 0.0, tin, L)
    a = il - bcum
    m_prev = m_scr[...].reshape(G * SUBLANES, LANES)
    big_m = jnp.maximum(m_prev, _seg_scan(a, jnp.maximum, -jnp.inf, tin, L))
    m_t = bcum + big_m
    if S == 1:
        m_last = jnp.broadcast_to(big_m[:, LANES - 1:LANES], big_m.shape)
    else:
        m_last = _seg_scan(big_m, jnp.maximum, -jnp.inf, tin, L, reverse=True)
    rows = [big_m, jnp.exp(m_prev - big_m), jnp.exp(-m_t), jnp.exp(a - m_last), jnp.exp(m_prev - m_last)]
    mN_ref[...] = m_t.reshape(G, SUBLANES, LANES)
    m_scr[...] = jnp.broadcast_to(m_t[:, LANES - 1:LANES], m_t.shape).reshape(G, SUBLANES, LANES)
    pad_rows = jnp.zeros((LANES - len(rows) * SUBLANES, LANES), f32)
    cols = []
    for g in range(G):
        gs = slice(g * SUBLANES, (g + 1) * SUBLANES)
        cols.append(jnp.concatenate([r[gs] for r in rows] + [pad_rows], axis=0).T)

    def col(g, q, h):
        return cols[g][:, SUBLANES * q + h:SUBLANES * q + h + 1]

    qs, ks, vs, kTs = [], [], [], []
    for h in range(M_HEADS):
        hs = slice(h * M_DH, (h + 1) * M_DH)
        qk_h = jnp.dot(xab[:, hs], wqk_ref[h], preferred_element_type=f32)
        qs.append(qk_h[:, 0:M_DH])
        ks.append(qk_h[:, M_DH:2 * M_DH] * (M_DH ** -0.5))
        vs.append(jnp.dot(xb[:, hs], wv_ref[h], preferred_element_type=f32))
        if S == 1:
            kT_h = lax.dot_general(wkT_ref[h], xab[:, hs], (((1,), (1,)), ((), ())),
                                   preferred_element_type=f32)
            kTs.append((kT_h * (M_DH ** -0.5)).astype(bf16))

    ti = lax.broadcasted_iota(jnp.int32, (R, R), 0)
    si = lax.broadcasted_iota(jnp.int32, (R, R), 1)
    mask = (si <= ti) & ((ti // L) == (si // L))
    ones_b = jnp.ones((R, M_DH), bf16)
    pairs = [(g, h) for g in range(G) for h in range(M_HEADS)]
    blk = lambda g: slice(g * R, (g + 1) * R)
    seqs = [slice(b * L, (b + 1) * L) for b in range(S)]
    q = {(g, h): qs[h][blk(g)] for g, h in pairs}
    k = {(g, h): ks[h][blk(g)] for g, h in pairs}
    v = {(g, h): vs[h][blk(g)] for g, h in pairs}
    qb = {p: q[p].astype(bf16) for p in pairs}
    kb = {p: k[p].astype(bf16) for p in pairs}
    qk = {p: lax.dot_general(qb[p], kb[p], (((1,), (1,)), ((), ())), preferred_element_type=f32)
          for p in pairs}
    sm = {}
    for g, h in pairs:
        a_row = a[g * SUBLANES + h:g * SUBLANES + h + 1, :]
        decay = jnp.exp(jnp.where(mask, a_row - col(g, 0, h), -jnp.inf))
        sm[g, h] = (qk[g, h] * decay).astype(bf16)
    nd = {p: jnp.dot(sm[p], jnp.concatenate([v[p].astype(bf16), ones_b], axis=1),
                     preferred_element_type=f32) for p in pairs}
    c_old = {(g, h, b): cN_ref[g * S + b, h] for g, h in pairs for b in range(S)}
    n_old = {(g, h, b): nN_ref[g * S + b, h] for g, h in pairs for b in range(S)}
    q_c = {(g, h, b): jnp.dot(qb[g, h][seqs[b]], c_old[g, h, b].astype(bf16), preferred_element_type=f32)
           for g, h in pairs for b in range(S)}
    hh = {}
    for g, h in pairs:
        g_col, e_col = col(g, 1, h), col(g, 2, h)
        pieces = []
        for b, rs in enumerate(seqs):
            q_n = jnp.sum(q[g, h][rs] * n_old[g, h, b], axis=1, keepdims=True)
            num = nd[g, h][rs, 0:M_DH] + g_col[rs] * q_c[g, h, b]
            den = nd[g, h][rs, M_DH:2 * M_DH] + g_col[rs] * q_n
            pieces.append(num / jnp.maximum(jnp.abs(den), e_col[rs]))
        hh[g, h] = pieces[0] if S == 1 else jnp.concatenate(pieces, axis=0)
    mu = {p: jnp.mean(hh[p], axis=1, keepdims=True) for p in pairs}
    var = {p: jnp.mean(jnp.square(hh[p] - mu[p]), axis=1, keepdims=True) for p in pairs}
    hn_blocks = [jnp.concatenate([(hh[g, h] - mu[g, h]) * lax.rsqrt(var[g, h] + LN_EPS)
                                  for h in range(M_HEADS)], axis=1) for g in range(G)]
    new_c, new_n = [], []
    for g, h in pairs:
        w_col = col(g, 3, h)
        wv = (w_col * v[g, h]).astype(bf16)
        wk = w_col * k[g, h]
        for b, rs in enumerate(seqs):
            g_end = col(g, 4, h)[(b + 1) * L - 1:(b + 1) * L, :]
            if S == 1:
                kv = jnp.dot(kTs[h][:, blk(g)], wv, preferred_element_type=f32)
            else:
                kv = lax.dot_general(kb[g, h][rs], wv[rs], (((0,), (0,)), ((), ())),
                                     preferred_element_type=f32)
            new_c.append((g * S + b, h, g_end * c_old[g, h, b] + kv))
            new_n.append((g * S + b, h, g_end * n_old[g, h, b] + jnp.sum(wk[rs], axis=0, keepdims=True)))

    hn = jnp.concatenate(hn_blocks, axis=0) * ng_ref[...]
    y = jax.nn.sigmoid(z_ref[...].reshape(GR, D_M)) * (hn + sk_ref[...] * xa)
    y_ref[...] = y.reshape(G, R, D_M)
    for sq, h, val in new_c:
        cN_ref[sq, h] = val
    for sq, h, val in new_n:
        nN_ref[sq, h] = val
    bN_ref[...] = xs_scr[:, SUBLANES - (CONV_W - 1):SUBLANES, :]


def _mlstm(proj, B, T, S, L, G, buf8, c0, n0, m0_lanes, cw, cb, wqk, wv, wkT, bif, ng, sk, riders=()):
    R = S * L
    assert R == M_ROWS == LANES
    nblk, nc = B // S, T // L
    nb = nblk // G
    assert nblk * S == B and nc * L == T and nb * G == nblk and (S == 1 or nc == 1)
    steps = nb * nc
    assert all(r.shape[0] % steps == 0 for r in riders)
    rider_specs = [pl.BlockSpec((r.shape[0] // steps,) + r.shape[1:], lambda b, j: (b * nc + j, 0, 0))
                   for r in riders]
    proj4 = proj.reshape(nblk, nc, R, N_PROJ)
    m0_3 = m0_lanes.reshape(SUBLANES, nblk, LANES).transpose(1, 0, 2)
    const2 = lambda b, j: (0, 0)
    const3 = lambda b, j: (0, 0, 0)
    gate_blk = (2 * D_RG + 2 * D_M) // LANES
    GS = G * S
    outs = pl.pallas_call(
        functools.partial(_mlstm_kernel, S=S, L=L, G=G, n_riders=len(riders)),
        grid=(nb, nc),
        in_specs=[
            pl.BlockSpec((G, None, R, D_M), lambda b, j: (b, j, 0, 2)),
            pl.BlockSpec((G, None, R, D_M), lambda b, j: (b, j, 0, 3)),
            pl.BlockSpec((G, None, R, LANES), lambda b, j: (b, j, 0, gate_blk)),
            pl.BlockSpec((GS, SUBLANES, D_M), lambda b, j: (b, 0, 0)),
            pl.BlockSpec((GS, M_HEADS, M_DH, M_DH), lambda b, j: (b, 0, 0, 0)),
            pl.BlockSpec((GS, M_HEADS, 1, M_DH), lambda b, j: (b, 0, 0, 0)),
            pl.BlockSpec((G, SUBLANES, LANES), lambda b, j: (b, 0, 0)),
            pl.BlockSpec((CONV_W, D_M), const2),
            pl.BlockSpec((1, D_M), const2),
            pl.BlockSpec((M_HEADS, M_DH, 2 * M_DH), const3),
            pl.BlockSpec((M_HEADS, M_DH, M_DH), const3),
            pl.BlockSpec((M_HEADS, M_DH, M_DH), const3),
            pl.BlockSpec((1, LANES), const2),
            pl.BlockSpec((1, D_M), const2),
            pl.BlockSpec((1, D_M), const2),
        ] + rider_specs,
        out_specs=[
            pl.BlockSpec((G, None, R, D_M), lambda b, j: (b, j, 0, 0)),
            pl.BlockSpec((GS, M_HEADS, M_DH, M_DH), lambda b, j: (b, 0, 0, 0)),
            pl.BlockSpec((GS, M_HEADS, 1, M_DH), lambda b, j: (b, 0, 0, 0)),
            pl.BlockSpec((G, None, SUBLANES, LANES), lambda b, j: (b, j, 0, 0)),
            pl.BlockSpec((GS, CONV_W - 1, D_M), lambda b, j: (b, 0, 0)),
        ] + rider_specs,
        out_shape=[
            jax.ShapeDtypeStruct((nblk, nc, R, D_M), f32),
            jax.ShapeDtypeStruct((B, M_HEADS, M_DH, M_DH), f32),
            jax.ShapeDtypeStruct((B, M_HEADS, 1, M_DH), f32),
            jax.ShapeDtypeStruct((nblk, nc, SUBLANES, LANES), f32),
            jax.ShapeDtypeStruct((B, CONV_W - 1, D_M), f32),
        ] + [jax.ShapeDtypeStruct(r.shape, bf16) for r in riders],
        scratch_shapes=[pltpu.VMEM((GS, SUBLANES, D_M), f32), pltpu.VMEM((G, SUBLANES, LANES), f32)],
        compiler_params=_cparams(2),
        name="mlstm",
    )(proj4, proj4, proj4, buf8, c0, n0, m0_3, cw, cb, wqk, wv, wkT, bif, ng, sk, *riders)
    y, c_new, n_new, m_t, b_new = outs[:5]
    m_last = m_t[:, nc - 1, :M_HEADS, :].reshape(nblk, M_HEADS, S, L)[:, :, :, L - 1]
    m_last = m_last.transpose(0, 2, 1).reshape(B, M_HEADS)
    return (y.reshape(B * T, D_M), c_new, n_new, m_last, b_new) + tuple(outs[5:])


def _layer_norm(x, g, b):
    mu = jnp.mean(x, axis=-1, keepdims=True)
    var = jnp.mean(jnp.square(x - mu), axis=-1, keepdims=True)
    return (x - mu) * lax.rsqrt(var + LN_EPS) * g + b


def _first_lane_of_max(vals, lane_f):
    vmax = jnp.max(vals, axis=1, keepdims=True)
    idx = jnp.min(jnp.where(vals == vmax, lane_f, float(LANES)), axis=1, keepdims=True)
    return vmax, idx


def _outproj_kernel(xp_ref, xs_ref, rgp_ref, rgs_ref, mp_ref, ms_ref, wo_ref, g1_ref, b1_ref,
                    wr_ref, br_ref, x1_ref, cnt_ref, *, n_p):
    i = pl.program_id(0)

    @pl.when(i == 0)
    def _():
        cnt_ref[...] = jnp.zeros_like(cnt_ref)

    def route(lg, lane, lane_f):
        neg = -jnp.inf
        gl = jnp.where(lane < N_GROUPS, lg, neg)
        gmax, gidx = _first_lane_of_max(gl, lane_f)
        p_g = 1.0 / jnp.sum(jnp.exp(gl - gmax), axis=1, keepdims=True)
        e_lo = float(N_GROUPS) + float(EXPERTS_PER_GROUP) * gidx
        el = jnp.where((lane_f >= e_lo) & (lane_f < e_lo + float(EXPERTS_PER_GROUP)), lg, neg)
        v1, i1 = _first_lane_of_max(el, lane_f)
        v2, i2 = _first_lane_of_max(jnp.where(lane_f == i1, neg, el), lane_f)
        d = jnp.exp(v2 - v1)
        w1 = p_g / (1.0 + d)
        w2 = p_g * d / (1.0 + d)
        first_low = i1 < i2
        j_lo = jnp.minimum(i1, i2) - e_lo
        j_hi = jnp.maximum(i1, i2) - e_lo
        n_first = float(EXPERTS_PER_GROUP - 1) * j_lo - 0.5 * j_lo * (j_lo - 1.0)
        cls = float(PAIRS_PER_GROUP) * gidx + n_first + (j_hi - j_lo - 1.0)
        return cls, jnp.where(first_low, w1, w2), jnp.where(first_low, w2, w1)

    def run(x_ref, rg_ref, m_ref):
        rp = TM_OUT // OUT_PARTS
        parts = [slice(p * rp, (p + 1) * rp) for p in range(OUT_PARTS)]
        lane = lax.broadcasted_iota(jnp.int32, (rp, LANES), 1)
        lane_f = lane.astype(f32)
        heads = [jnp.concatenate([rg_ref[r, :].astype(bf16), m_ref[r, :].astype(bf16)], axis=1) for r in parts]
        mix = [jnp.dot(h, wo_ref[...], preferred_element_type=f32) for h in heads]
        x1 = [_layer_norm(ALPHA * x_ref[r, :] + mx, g1_ref[...], b1_ref[...]) for r, mx in zip(parts, mix)]
        for r, v in zip(parts, x1):
            x1_ref[r, 0:D_MODEL] = v

        hi = [v.astype(bf16) for v in x1]
        lo = [(v - h.astype(f32)).astype(bf16) for v, h in zip(x1, hi)]
        hi_terms = [jnp.dot(h, wr_ref[...], preferred_element_type=f32) for h in hi]
        lo_term = [jnp.dot(v, wr_ref[:, 0:LANES], preferred_element_type=f32) for v in lo]
        lg = [ht[:, 0:LANES] + ht[:, LANES:2 * LANES] + lt + br_ref[...] for ht, lt in zip(hi_terms, lo_term)]
        routed = [route(v, lane, lane_f) for v in lg]

        ti = lax.broadcasted_iota(jnp.int32, (rp, rp), 0)
        si = lax.broadcasted_iota(jnp.int32, (rp, rp), 1)
        tri = (si <= ti).astype(bf16)
        hot = [lane_f == cls for cls, _, _ in routed]
        cum = [jnp.dot(tri, h.astype(bf16), preferred_element_type=f32) for h in hot]
        seen = cnt_ref[...]
        for r, h, cm, (cls, w_lo, w_hi) in zip(parts, hot, cum, routed):
            rank = jnp.sum(jnp.where(h, cm - 1.0 + seen, 0.0), axis=1, keepdims=True)
            seen = seen + cm[rp - 1:rp, :]
            info = jnp.zeros((rp, LANES), f32)
            for c, val in enumerate((cls, rank, w_lo, w_hi)):
                info = jnp.where(lane == c, val, info)
            x1_ref[r, D_MODEL:D_MODEL + LANES] = info
        cnt_ref[...] = seen

    @pl.when(i < n_p)
    def _():
        run(xp_ref, rgp_ref, mp_ref)

    @pl.when(i >= n_p)
    def _():
        run(xs_ref, rgs_ref, ms_ref)


def _outproj(xp, xs, rgp, rgs, mp, ms, wo, g1, b1, wr, br):
    n_p, n_s = xp.shape[0] // TM_OUT, xs.shape[0] // TM_OUT
    n = n_p + n_s
    pmap = lambda i: (jnp.minimum(i, n_p - 1), 0)
    smap = lambda i: (jnp.maximum(i - n_p, 0), 0)
    const = lambda i: (0, 0)
    return pl.pallas_call(
        functools.partial(_outproj_kernel, n_p=n_p),
        grid=(n,),
        in_specs=[
            pl.BlockSpec((TM_OUT, D_MODEL), pmap), pl.BlockSpec((TM_OUT, D_MODEL), smap),
            pl.BlockSpec((TM_OUT, D_RG), pmap), pl.BlockSpec((TM_OUT, D_RG), smap),
            pl.BlockSpec((TM_OUT, D_M), pmap), pl.BlockSpec((TM_OUT, D_M), smap),
            pl.BlockSpec((D_RG + D_M, D_MODEL), const),
            pl.BlockSpec((1, D_MODEL), const), pl.BlockSpec((1, D_MODEL), const),
            pl.BlockSpec((D_MODEL, 2 * LANES), const),
            pl.BlockSpec((1, LANES), const),
        ],
        out_specs=[
            pl.BlockSpec((TM_OUT, X_ROW), lambda i: (i, 0)),
            pl.BlockSpec((1, LANES), const),
        ],
        out_shape=[
            jax.ShapeDtypeStruct((n * TM_OUT, X_ROW), f32),
            jax.ShapeDtypeStruct((1, LANES), f32),
        ],
        compiler_params=_cparams(1),
        name="outproj",
    )(xp, xs, rgp, rgs, mp, ms, wo, g1, b1, wr, br)


GRP_MOVE = TM_MOVE // SUBLANES


def _grouped(x):
    return x.reshape(x.shape[0] // SUBLANES, SUBLANES, x.shape[1])


def _tile_rows_wait(hbm_ref, sem):
    rows = hbm_ref.at[pl.ds(0, TM_MOVE)]
    pltpu.make_async_copy(rows, rows, sem).wait()


def _for_each_tile_row(body):
    def group(k, c):
        for j in range(SUBLANES):
            body(k, j, k * SUBLANES + j)
        return c

    lax.fori_loop(0, GRP_MOVE, group, 0)


def _dispatch_kernel(pad_ref, pos_ref, x1_ref, xs_ref, zero_scr, xbuf, sem, zsem, lsem):
    i = pl.program_id(0)
    n = pl.num_programs(0)

    def zero_tile(t):
        return pltpu.make_async_copy(zero_scr, xs_ref.at[pl.ds(pl.multiple_of(t * TM_MOE, TM_MOE), TM_MOE)], zsem)

    def zero_chunk(row0):
        return pltpu.make_async_copy(zero_scr.at[pl.ds(0, SUBLANES)],
                                     xs_ref.at[pl.ds(pl.multiple_of(row0, SUBLANES), SUBLANES)], zsem)

    @pl.when(i == 0)
    def _():
        zero_scr[...] = jnp.zeros_like(zero_scr)
        n_used = pad_ref[2 * N_CLASSES]
        n_tiles = xs_ref.shape[0] // TM_MOE
        for go in (lambda cp: cp.start(), lambda cp: cp.wait()):
            lax.fori_loop(n_used, n_tiles, lambda t, c: (go(zero_tile(t)), c)[1], 0)

            def per_class(cl, c):
                row0 = pad_ref[2 * cl]
                lax.fori_loop(0, pad_ref[2 * cl + 1],
                              lambda k, c2: (go(zero_chunk(row0 + k * SUBLANES)), c2)[1], 0)
                return c

            lax.fori_loop(0, N_CLASSES, per_class, 0)

    def load(t, s):
        return pltpu.make_async_copy(x1_ref.at[pl.ds(t * GRP_MOVE, GRP_MOVE)], xbuf.at[s], lsem.at[s])

    @pl.when(i == 0)
    def _():
        load(0, 0).start()

    @pl.when(i + 1 < n)
    def _():
        load(i + 1, (i + 1) % 3).start()

    slot = i % 3
    load(i, slot).wait()

    def start(k, j, r):
        pltpu.make_async_copy(xbuf.at[slot, k, pl.ds(j, 1)], xs_ref.at[pl.ds(pos_ref[0, 0, r], 1)],
                              sem.at[slot]).start()

    _for_each_tile_row(start)

    @pl.when(i > 0)
    def _():
        _tile_rows_wait(xs_ref, sem.at[(i + 2) % 3])

    @pl.when(i == n - 1)
    def _():
        _tile_rows_wait(xs_ref, sem.at[slot])


def _dispatch(pad_plan, pos, x1, n_rows):
    n = x1.shape[0] // TM_MOVE
    return pl.pallas_call(
        _dispatch_kernel,
        grid_spec=pltpu.PrefetchScalarGridSpec(
            num_scalar_prefetch=1,
            grid=(n,),
            in_specs=[
                pl.BlockSpec((1, 1, TM_MOVE), lambda i, *_: (i, 0, 0), memory_space=pltpu.SMEM),
                pl.BlockSpec(memory_space=pl.ANY),
            ],
            out_specs=pl.BlockSpec(memory_space=pl.ANY),
            scratch_shapes=[pltpu.VMEM((TM_MOE, X_ROW), f32),
                            pltpu.VMEM((3, GRP_MOVE, SUBLANES, X_ROW), f32),
                            pltpu.SemaphoreType.DMA((3,)), pltpu.SemaphoreType.DMA(()),
                            pltpu.SemaphoreType.DMA((3,))],
        ),
        out_shape=jax.ShapeDtypeStruct((n_rows, X_ROW), f32),
        compiler_params=_cparams(1),
        name="dispatch",
    )(pad_plan, pos, _grouped(x1))


def _expert_kernel(ta_ref, tb_ref, ca_ref, cb_ref, sa_ref, sb_ref, nu_ref,
                   x_ref, wg_hbm, wu_hbm, wd_hbm, g2_ref, b2_ref, y_ref,
                   wga, wua, wda, wgb, wub, wdb, sem_a, sem_b):
    i = pl.program_id(0)
    n_used = nu_ref[0]
    first = i * MOE_TILES
    side_a = (ta_ref, ca_ref, sa_ref, (wga, wua, wda), sem_a)
    side_b = (tb_ref, cb_ref, sb_ref, (wgb, wub, wdb), sem_b)

    def copies(side, j):
        t_ref, _, s_ref, bufs, sem = side
        e, s = t_ref[j], s_ref[j]
        return [pltpu.make_async_copy(w.at[e], buf.at[s], sem.at[s])
                for w, buf in zip((wg_hbm, wu_hbm, wd_hbm), bufs)]

    def fetch(j):
        @pl.when(j < n_used)
        def _():
            for side in (side_a, side_b):
                @pl.when(side[1][j] == 1)
                def _():
                    for cp in copies(side, j):
                        cp.start()

    @pl.when(i == 0)
    def _():
        for j in range(W_AHEAD):
            fetch(j)

    for t in range(MOE_TILES):
        fetch(first + W_AHEAD + t)

    @pl.when(first < n_used)
    def _():
        for t in range(MOE_TILES):
            for side in (side_a, side_b):
                @pl.when(side[1][first + t] == 1)
                def _():
                    for cp in copies(side, first + t):
                        cp.wait()
        rows = [slice(t * TM_MOE, (t + 1) * TM_MOE) for t in range(MOE_TILES)]
        x = [x_ref[r, 0:D_MODEL] for r in rows]
        info = [x_ref[r, D_MODEL:X_ROW] for r in rows]
        xb = [v.astype(bf16) for v in x]
        units = [(t, bufs, s_ref[first + t]) for t in range(MOE_TILES)
                 for bufs, s_ref in (((wga, wua, wda), sa_ref), ((wgb, wub, wdb), sb_ref))]
        hg = [jnp.dot(xb[t], bufs[0][s], preferred_element_type=f32) for t, bufs, s in units]
        hu = [jnp.dot(xb[t], bufs[1][s], preferred_element_type=f32) for t, bufs, s in units]
        mid = [(g * jax.nn.sigmoid(g) * u).astype(bf16) for g, u in zip(hg, hu)]
        ys = [jnp.dot(m, bufs[2][s], preferred_element_type=f32)
              for m, (t, bufs, s) in zip(mid, units)]
        for t in range(MOE_TILES):
            ffn = info[t][:, 2:3] * ys[2 * t] + info[t][:, 3:4] * ys[2 * t + 1]
            y_ref[rows[t], :] = _layer_norm(ALPHA * x[t] + ffn, g2_ref[...], b2_ref[...])

    @pl.when(first >= n_used)
    def _():
        y_ref[...] = jnp.zeros_like(y_ref)


def _experts(tile_ea, tile_eb, n_used, xs, wg, wu, wd, g2, b2):
    nt = xs.shape[0] // TM_MOE

    def ring_plan(tile_e):
        opens = jnp.concatenate([jnp.ones((1,), jnp.int32), (tile_e[1:] != tile_e[:-1]).astype(jnp.int32)])
        return opens, (jnp.cumsum(opens) - 1) % W_SLOTS

    open_a, slot_a = ring_plan(tile_ea)
    open_b, slot_b = ring_plan(tile_eb)
    assert nt % MOE_TILES == 0
    n_used = ((n_used + MOE_TILES - 1) // MOE_TILES) * MOE_TILES
    const = lambda i, *_: (0, 0)
    xmap = lambda i, ta, tb, ca, cb, sa, sb, nu: (jnp.minimum(i, nu[0] // MOE_TILES - 1), 0)
    w_in = pltpu.VMEM((W_SLOTS, D_MODEL, D_EXPERT), bf16)
    w_out = pltpu.VMEM((W_SLOTS, D_EXPERT, D_MODEL), bf16)
    return pl.pallas_call(
        _expert_kernel,
        grid_spec=pltpu.PrefetchScalarGridSpec(
            num_scalar_prefetch=7,
            grid=(nt // MOE_TILES,),
            in_specs=[
                pl.BlockSpec((MOE_TILES * TM_MOE, X_ROW), xmap),
                pl.BlockSpec(memory_space=pl.ANY), pl.BlockSpec(memory_space=pl.ANY),
                pl.BlockSpec(memory_space=pl.ANY),
                pl.BlockSpec((1, D_MODEL), const), pl.BlockSpec((1, D_MODEL), const),
            ],
            out_specs=pl.BlockSpec((MOE_TILES * TM_MOE, D_MODEL), lambda i, *_: (i, 0)),
            scratch_shapes=[w_in, w_in, w_out, w_in, w_in, w_out,
                            pltpu.SemaphoreType.DMA((W_SLOTS,)), pltpu.SemaphoreType.DMA((W_SLOTS,))],
        ),
        out_shape=jax.ShapeDtypeStruct((nt * TM_MOE, D_MODEL), f32),
        compiler_params=_cparams(1),
        name="experts",
    )(tile_ea, tile_eb, open_a, open_b, slot_a.astype(jnp.int32), slot_b.astype(jnp.int32), n_used,
      xs, wg, wu, wd, g2, b2)


def _collect_kernel(pos_ref, posn_ref, ys_ref, op_ref, os_ref, ybuf, sem, *, n_p):
    i = pl.program_id(0)
    n = pl.num_programs(0)
    slot = i % 2

    def gather(p_ref, s):
        def start(k, j, r):
            pltpu.make_async_copy(ys_ref.at[pl.ds(p_ref[0, 0, r], 1)], ybuf.at[s, k, pl.ds(j, 1)],
                                  sem.at[s]).start()

        _for_each_tile_row(start)

    @pl.when(i == 0)
    def _():
        gather(pos_ref, 0)

    @pl.when(i + 1 < n)
    def _():
        gather(posn_ref, 1 - slot)

    _tile_rows_wait(ys_ref, sem.at[slot])

    @pl.when(i < n_p)
    def _():
        op_ref[...] = ybuf[slot].reshape(TM_MOVE, D_MODEL)

    @pl.when(i >= n_p)
    def _():
        os_ref[...] = ybuf[slot].reshape(TM_MOVE, D_MODEL)


def _collect(pos, ys, n_p):
    n = pos.shape[0]
    n_s = n - n_p
    return pl.pallas_call(
        functools.partial(_collect_kernel, n_p=n_p),
        grid=(n,),
        in_specs=[
            pl.BlockSpec((1, 1, TM_MOVE), lambda i: (i, 0, 0), memory_space=pltpu.SMEM),
            pl.BlockSpec((1, 1, TM_MOVE), lambda i: (jnp.minimum(i + 1, n - 1), 0, 0),
                         memory_space=pltpu.SMEM),
            pl.BlockSpec(memory_space=pl.ANY),
        ],
        out_specs=[
            pl.BlockSpec((TM_MOVE, D_MODEL), lambda i: (jnp.minimum(i, n_p - 1), 0)),
            pl.BlockSpec((TM_MOVE, D_MODEL), lambda i: (jnp.maximum(i - n_p, 0), 0)),
        ],
        out_shape=[
            jax.ShapeDtypeStruct((n_p * TM_MOVE, D_MODEL), f32),
            jax.ShapeDtypeStruct((n_s * TM_MOVE, D_MODEL), f32),
        ],
        scratch_shapes=[pltpu.VMEM((2, GRP_MOVE, SUBLANES, D_MODEL), f32), pltpu.SemaphoreType.DMA((2,))],
        compiler_params=_cparams(1),
        name="collect",
    )(pos, pos, ys)


def _block_diag(w):
    n, d, _ = w.shape
    eye = jnp.eye(n, dtype=w.dtype)
    return (eye[:, None, :, None] * w[:, :, None, :]).reshape(n * d, n * d)


def _pair_walk(n):
    total = n * (n - 1) // 2

    def extend(path, used):
        if len(path) == total:
            return path
        a, b = path[-1]
        for c in range(n):
            for nxt in ((a, c), (c, b)):
                key = frozenset(nxt)
                if len(key) == 2 and key not in used:
                    out = extend(path + [nxt], used | {key})
                    if out:
                        return out
        return None

    return extend([(0, 1)], {frozenset((0, 1))})


def _pad_history(buf):
    return jnp.pad(buf, ((0, 0), (SUBLANES - (CONV_W - 1), 0), (0, 0)))


def kernel(x_prompt, x_sample, state_rg_h, state_rg_conv, state_m_C, state_m_n, state_m_m, state_m_conv, w_in, rg_conv_w, rg_conv_b, rg_w_a, rg_b_a, rg_w_x, rg_b_x, rg_lambda, m_conv_w, m_conv_b, m_w_q, m_w_k, m_w_v, m_b_i, m_b_f, m_norm_g, m_skip, w_out, ln1_g, ln1_b, ln2_g, ln2_b, moe_w_group, moe_b_group, moe_w_expert, moe_b_expert, moe_w_gate, moe_w_up, moe_w_down):
    BP, TP, _ = x_prompt.shape
    BS, TS, _ = x_sample.shape
    n_prompt, n_sample = BP * TP, BS * TS
    xp = x_prompt.reshape(n_prompt, D_MODEL)
    xs = x_sample.reshape(n_sample, D_MODEL)
    l = 0

    w_in_p = jnp.pad(w_in[l], ((0, 0), (0, N_PROJ - w_in.shape[-1]))).astype(bf16)
    wa = _block_diag(rg_w_a[l]).astype(bf16)
    wx = _block_diag(rg_w_x[l]).astype(bf16)
    row = lambda v: v.reshape(1, -1)
    bif = jnp.pad(jnp.concatenate([m_b_i[l], m_b_f[l]]), (0, LANES - 2 * M_HEADS)).reshape(1, LANES)
    wqk = jnp.concatenate([m_w_q[l], m_w_k[l]], axis=-1).astype(bf16)
    wv = m_w_v[l].astype(bf16)
    wkT = m_w_k[l].transpose(0, 2, 1).astype(bf16)
    w_route = jnp.pad(jnp.concatenate([moe_w_group[l], moe_w_expert[l]], axis=1),
                      ((0, 0), (0, LANES - N_GROUPS - N_EXPERTS)))
    wr_hi = w_route.astype(bf16)
    wr_lo = (w_route - wr_hi.astype(f32)).astype(bf16)
    wr = jnp.concatenate([wr_hi, wr_lo], axis=1)
    b_route = jnp.pad(jnp.concatenate([moe_b_group[l], moe_b_expert[l]]),
                      (0, LANES - N_GROUPS - N_EXPERTS)).reshape(1, LANES)

    proj_p, proj_s = _inproj(xp, xs, w_in_p)

    rg_args = (rg_conv_w[l], row(rg_conv_b[l]), wa, row(rg_b_a[l]), wx, row(rg_b_x[l]), row(rg_lambda[l]))
    zeros = lambda *s: jnp.zeros(s, f32)
    s_blk = M_ROWS // TS
    yrg_p, p_rg_h, p_rg_conv = _rglru(proj_p, 0, BP, TP, 1, RG_ROWS, zeros(BP, SUBLANES, D_RG),
                                      zeros(BP, 1, D_RG), *rg_args)
    yrg_s, s_rg_h, s_rg_conv = _rglru(proj_s, 0, BS, TS, min(RG_ROWS // TS, BS), TS,
                                      _pad_history(state_rg_conv[l]), state_rg_h[l].reshape(BS, 1, D_RG), *rg_args)

    m_args = (m_conv_w[l], row(m_conv_b[l]), wqk, wv, wkT, bif, row(m_norm_g[l]), row(m_skip[l]))
    m0_p = jnp.full((SUBLANES, BP * LANES), M_INIT, f32)
    ym_p, p_m_C, p_m_n, p_m_m, p_m_conv, wg_b, wu_b, wd_b = _mlstm(
        proj_p, BP, TP, 1, M_ROWS, M_BLOCKS_PROMPT, zeros(BP, SUBLANES, D_M), zeros(BP, M_HEADS, M_DH, M_DH),
        zeros(BP, M_HEADS, 1, M_DH), m0_p, *m_args, riders=(moe_w_gate[l], moe_w_up[l], moe_w_down[l]))
    m0_s = jnp.pad(jnp.repeat(state_m_m[l].T, TS, axis=1), ((0, SUBLANES - M_HEADS), (0, 0)))
    ym_s, s_m_C, s_m_n, s_m_m, s_m_conv = _mlstm(
        proj_s, BS, TS, s_blk, TS, M_BLOCKS_SAMPLE, _pad_history(state_m_conv[l]), state_m_C[l],
        state_m_n[l].reshape(BS, M_HEADS, 1, M_DH), m0_s, *m_args)

    x1, cnt = _outproj(xp, xs, yrg_p, yrg_s, ym_p, ym_s, w_out[l].astype(bf16),
                       row(ln1_g[l]), row(ln1_b[l]), wr, b_route)

    n_tok = n_prompt + n_sample
    n_tiles = n_tok // TM_MOE + N_CLASSES
    lay_cls, lay_a, lay_b = [], [], []
    for g in range(N_GROUPS):
        for ja, jb in _pair_walk(EXPERTS_PER_GROUP):
            assert ja < jb
            n_first = (EXPERTS_PER_GROUP - 1) * ja - ja * (ja - 1) // 2
            lay_cls.append(PAIRS_PER_GROUP * g + n_first + jb - ja - 1)
            lay_a.append(g * EXPERTS_PER_GROUP + ja)
            lay_b.append(g * EXPERTS_PER_GROUP + jb)
    classes = jnp.arange(N_CLASSES, dtype=jnp.int32)
    in_slot = jnp.array(lay_cls, jnp.int32)[:, None] == classes
    counts = jnp.sum(jnp.where(in_slot, cnt[0, :N_CLASSES].astype(jnp.int32), 0), axis=1)
    padded = ((counts + TM_MOE - 1) // TM_MOE) * TM_MOE
    ends = jnp.cumsum(padded)
    offs = ends - padded
    offs_of_cls = jnp.sum(jnp.where(in_slot, offs[:, None], 0), axis=0)
    cls = x1[:, D_MODEL].astype(jnp.int32)
    rank = x1[:, D_MODEL + 1].astype(jnp.int32)
    pos = jnp.sum(jnp.where(cls[:, None] == classes, offs_of_cls, 0), axis=-1) + rank
    pos = pos.reshape(n_tok // TM_MOVE, 1, TM_MOVE)
    tiles = jnp.arange(n_tiles, dtype=jnp.int32)
    tile_slot = jnp.minimum(jnp.sum(tiles[:, None] >= (ends // TM_MOE)[None, :], axis=1), N_CLASSES - 1)
    on_slot = tile_slot[:, None] == classes
    tile_ea = jnp.sum(jnp.where(on_slot, jnp.array(lay_a, jnp.int32), 0), axis=-1).astype(jnp.int32)
    tile_eb = jnp.sum(jnp.where(on_slot, jnp.array(lay_b, jnp.int32), 0), axis=-1).astype(jnp.int32)
    n_used = (ends[-1] // TM_MOE).reshape(1).astype(jnp.int32)
    pad_row0 = ((offs + counts) // SUBLANES) * SUBLANES
    pad_plan = jnp.concatenate([jnp.stack([pad_row0, (ends - pad_row0) // SUBLANES], axis=1).reshape(-1),
                                     n_used]).astype(jnp.int32)

    x_sorted = _dispatch(pad_plan, pos, x1, n_tiles * TM_MOE)
    y_sorted = _experts(tile_ea, tile_eb, n_used, x_sorted, wg_b, wu_b, wd_b,
                        row(ln2_g[l]), row(ln2_b[l]))
    y_p, y_s = _collect(pos, y_sorted, n_prompt // TM_MOVE)

    return (y_p.reshape(BP, TP, D_MODEL), y_s.reshape(BS, TS, D_MODEL),
            p_rg_h.reshape(1, BP, D_RG), p_rg_conv[None], p_m_C[None], p_m_n.reshape(1, BP, M_HEADS, M_DH),
            p_m_m[None], p_m_conv[None],
            s_rg_h.reshape(1, BS, D_RG), s_rg_conv[None], s_m_C[None], s_m_n.reshape(1, BS, M_HEADS, M_DH),
            s_m_m[None], s_m_conv[None])
```

```python
import functools

import jax
import jax.numpy as jnp
from jax import lax
from jax.experimental import pallas as pl
from jax.experimental.pallas import tpu as pltpu

f32 = jnp.float32
bf16 = jnp.bfloat16

D_MODEL = 1024
D_RG = 512
RG_C = 8.0
D_M = 512
M_HEADS = 4
M_DH = 128
CONV_W = 4
N_GROUPS = 4
EXPERTS_PER_GROUP = 8
N_EXPERTS = 32
D_EXPERT = 256
ALPHA = 2.0 ** 0.25
LN_EPS = 1e-5
M_INIT = -1.0e4

LANES = 128
SUBLANES = 8
TM = 512
TM_MOVE = 1024
TM_OUT = 1024
OUT_PARTS = 8
TM_MOE = 128
MOE_TILES = 4
W_AHEAD = 8
W_SLOTS = W_AHEAD + MOE_TILES
X_AHEAD = 2
X_SLOTS = X_AHEAD + 1
PAIRS_PER_GROUP = EXPERTS_PER_GROUP * (EXPERTS_PER_GROUP - 1) // 2
N_CLASSES = N_GROUPS * PAIRS_PER_GROUP
X_ROW = D_MODEL + LANES
N_PROJ = 2 * D_RG + 2 * D_M + LANES
RG_ROWS = 1024
M_ROWS = 128
M_BLOCKS_PROMPT = 8
M_BLOCKS_SAMPLE = 2
V7X_VMEM_BYTES = 64 * 1024 * 1024
VMEM_LIMIT = V7X_VMEM_BYTES - 8 * 1024 * 1024


def _cparams(n_axes):
    return pltpu.CompilerParams(dimension_semantics=("arbitrary",) * n_axes,
                                vmem_limit_bytes=VMEM_LIMIT)


def _inproj_kernel(xp_ref, xs_ref, w_ref, op_ref, os_ref, *, n_p):
    i = pl.program_id(0)

    def run(x_ref, o_ref):
        o_ref[...] = jnp.dot(x_ref[...].astype(bf16), w_ref[...], preferred_element_type=f32)

    @pl.when(i < n_p)
    def _():
        run(xp_ref, op_ref)

    @pl.when(i >= n_p)
    def _():
        run(xs_ref, os_ref)


def _inproj(xp, xs, w):
    n_p, n_s = xp.shape[0] // TM, xs.shape[0] // TM
    pmap = lambda i: (jnp.minimum(i, n_p - 1), 0)
    smap = lambda i: (jnp.maximum(i - n_p, 0), 0)
    return pl.pallas_call(
        functools.partial(_inproj_kernel, n_p=n_p),
        grid=(n_p + n_s,),
        in_specs=[
            pl.BlockSpec((TM, D_MODEL), pmap),
            pl.BlockSpec((TM, D_MODEL), smap),
            pl.BlockSpec((D_MODEL, N_PROJ), lambda i: (0, 0)),
        ],
        out_specs=[pl.BlockSpec((TM, N_PROJ), pmap), pl.BlockSpec((TM, N_PROJ), smap)],
        out_shape=[jax.ShapeDtypeStruct((n_p * TM, N_PROJ), f32),
                   jax.ShapeDtypeStruct((n_s * TM, N_PROJ), f32)],
        compiler_params=_cparams(1),
        name="inproj",
    )(xp, xs, w)


def _causal_conv(tail_scr, x, cw_ref, cb_ref, S, L, C):
    tail = tail_scr[...]
    sub = lax.broadcasted_iota(jnp.int32, (S, SUBLANES, C), 1)
    acc = cb_ref[...] + cw_ref[CONV_W - 1:CONV_W, :] * x
    for d in range(1, CONV_W):
        back = pltpu.roll(x, d, 0).reshape(S, L, C)
        head = jnp.where(sub < d, pltpu.roll(tail, d, 1), back[:, 0:SUBLANES, :])
        if L > SUBLANES:
            back = jnp.concatenate([head, back[:, SUBLANES:, :]], axis=1)
        else:
            back = head
        acc = acc + cw_ref[CONV_W - 1 - d:CONV_W - d, :] * back.reshape(S * L, C)
    tail_scr[...] = x.reshape(S, L, C)[:, L - SUBLANES:, :]
    return acc


def _softplus(x):
    return jnp.maximum(x, 0.0) + jnp.log1p(jnp.exp(-jnp.abs(x)))


def _rglru_kernel(x_ref, g_ref, buf_ref, h0_ref, cw_ref, cb_ref, wa_ref, ba_ref, wx_ref, bx_ref,
                  lam_ref, y_ref, hN_ref, cN_ref, xs_scr, *, S, L):
    R = S * L
    t = pl.program_id(1)

    @pl.when(t == 0)
    def _():
        xs_scr[...] = buf_ref[...]
        hN_ref[...] = h0_ref[...]

    x = x_ref[...]
    xc = _causal_conv(xs_scr, x, cw_ref, cb_ref, S, L, D_RG)
    xcb = xc.astype(bf16)
    r = jax.nn.sigmoid(jnp.dot(xcb, wa_ref[...], preferred_element_type=f32) + ba_ref[...])
    ig = jax.nn.sigmoid(jnp.dot(xcb, wx_ref[...], preferred_element_type=f32) + bx_ref[...])
    log_a = (-RG_C) * r * _softplus(-lam_ref[...])
    a = jnp.exp(log_a)
    th = jnp.tanh(log_a)
    u = jnp.sqrt(-2.0 * th / (1.0 - th)) * ig * xc

    n_grp, grp_per_seq = R // SUBLANES, L // SUBLANES
    a3 = a.reshape(n_grp, SUBLANES, D_RG)
    u3 = u.reshape(n_grp, SUBLANES, D_RG)
    sub = lax.broadcasted_iota(jnp.int32, (n_grp, SUBLANES, D_RG), 1)
    s = 1
    while s < SUBLANES:
        ok = sub >= s
        a_sh = pltpu.roll(a3, s, 1)
        u_sh = pltpu.roll(u3, s, 1)
        u3 = jnp.where(ok, a3 * u_sh + u3, u3)
        a3 = jnp.where(ok, a3 * a_sh, a3)
        s *= 2
    h0 = hN_ref[...]
    groups = []
    for kg in range(n_grp):
        carry = h0[kg // grp_per_seq] if kg % grp_per_seq == 0 else groups[-1][SUBLANES - 1:SUBLANES, :]
        groups.append(a3[kg] * carry + u3[kg])
    h = jnp.concatenate(groups, axis=0)

    y_ref[...] = h * jax.nn.gelu(g_ref[...], approximate=True)
    hN_ref[...] = h.reshape(S, L, D_RG)[:, L - 1:L, :]
    cN_ref[...] = xs_scr[:, SUBLANES - (CONV_W - 1):SUBLANES, :]


def _rglru(proj, row0, B, T, S, L, buf8, h0, cw, cb, wa, ba, wx, bx, lam):
    R = S * L
    nb, nt = B // S, T // L
    blk0 = row0 // R
    row_map = lambda b, t: (blk0 + b * nt + t, 0)
    const2 = lambda b, t: (0, 0)
    return pl.pallas_call(
        functools.partial(_rglru_kernel, S=S, L=L),
        grid=(nb, nt),
        in_specs=[
            pl.BlockSpec((R, D_RG), row_map),
            pl.BlockSpec((R, D_RG), lambda b, t: (blk0 + b * nt + t, 1)),
            pl.BlockSpec((S, SUBLANES, D_RG), lambda b, t: (b, 0, 0)),
            pl.BlockSpec((S, 1, D_RG), lambda b, t: (b, 0, 0)),
            pl.BlockSpec((CONV_W, D_RG), const2),
            pl.BlockSpec((1, D_RG), const2),
            pl.BlockSpec((D_RG, D_RG), const2),
            pl.BlockSpec((1, D_RG), const2),
            pl.BlockSpec((D_RG, D_RG), const2),
            pl.BlockSpec((1, D_RG), const2),
            pl.BlockSpec((1, D_RG), const2),
        ],
        out_specs=[
            pl.BlockSpec((R, D_RG), lambda b, t: (b * nt + t, 0)),
            pl.BlockSpec((S, 1, D_RG), lambda b, t: (b, 0, 0)),
            pl.BlockSpec((S, CONV_W - 1, D_RG), lambda b, t: (b, 0, 0)),
        ],
        out_shape=[
            jax.ShapeDtypeStruct((B * T, D_RG), f32),
            jax.ShapeDtypeStruct((B, 1, D_RG), f32),
            jax.ShapeDtypeStruct((B, CONV_W - 1, D_RG), f32),
        ],
        scratch_shapes=[pltpu.VMEM((S, SUBLANES, D_RG), f32)],
        compiler_params=_cparams(2),
        name="rglru",
    )(proj, proj, buf8, h0, cw, cb, wa, ba, wx, bx, lam)


def _seg_scan(x, op, fill, tin, L, reverse=False):
    s = 1
    while s < L:
        if reverse:
            sh = pltpu.roll(x, LANES - s, 1)
            ok = tin < L - s
        else:
            sh = pltpu.roll(x, s, 1)
            ok = tin >= s
        x = op(x, jnp.where(ok, sh, fill))
        s *= 2
    return x


def _mlstm_kernel(*refs, S, L, G, n_riders):
    (x_ref, z_ref, gt_ref, buf_ref, c0_ref, n0_ref, m0_ref,
     cw_ref, cb_ref, wqk_ref, wv_ref, wkT_ref, bif_ref, ng_ref, sk_ref) = refs[:15]
    rider_in = refs[15:15 + n_riders]
    y_ref, cN_ref, nN_ref, mN_ref, bN_ref = refs[15 + n_riders:20 + n_riders]
    rider_out = refs[20 + n_riders:20 + 2 * n_riders]
    xs_scr, m_scr = refs[20 + 2 * n_riders:]
    for src, dst in zip(rider_in, rider_out):
        dst[...] = src[...].astype(bf16)

    R = S * L
    GS, GR = G * S, G * R
    j = pl.program_id(1)

    @pl.when(j == 0)
    def _():
        xs_scr[...] = buf_ref[...]
        cN_ref[...] = c0_ref[...]
        nN_ref[...] = n0_ref[...]
        m_scr[...] = m0_ref[...]

    x = x_ref[...].reshape(GR, D_M)
    xc = _causal_conv(xs_scr, x, cw_ref, cb_ref, GS, L, D_M)
    xa = xc * jax.nn.sigmoid(xc)
    xab = xa.astype(bf16)
    xb = x.astype(bf16)

    il_parts, f_parts = [], []
    for g in range(G):
        gT = (gt_ref[g] + bif_ref[...]).T
        il_parts.append(gT[0:SUBLANES, :])
        f_parts.append(pltpu.roll(gT[0:SUBLANES, :], M_HEADS, 0))
    il = jnp.concatenate(il_parts, axis=0)
    fl = -_softplus(-jnp.concatenate(f_parts, axis=0))
    tin = lax.broadcasted_iota(jnp.int32, (G * SUBLANES, LANES), 1) % L
    bcum = _seg_scan(fl, jnp.add, 0.0, tin, L)
    a = il - bcum
    m_prev = m_scr[...].reshape(G * SUBLANES, LANES)
    big_m = jnp.maximum(m_prev, _seg_scan(a, jnp.maximum, -jnp.inf, tin, L))
    m_t = bcum + big_m
    if S == 1:
        m_last = jnp.broadcast_to(big_m[:, LANES - 1:LANES], big_m.shape)
    else:
        m_last = _seg_scan(big_m, jnp.maximum, -jnp.inf, tin, L, reverse=True)
    rows = [big_m, jnp.exp(m_prev - big_m), jnp.exp(-m_t), jnp.exp(a - m_last), jnp.exp(m_prev - m_last)]
    mN_ref[...] = m_t.reshape(G, SUBLANES, LANES)
    m_scr[...] = jnp.broadcast_to(m_t[:, LANES - 1:LANES], m_t.shape).reshape(G, SUBLANES, LANES)
    pad_rows = jnp.zeros((LANES - len(rows) * SUBLANES, LANES), f32)
    cols = []
    for g in range(G):
        gs = slice(g * SUBLANES, (g + 1) * SUBLANES)
        cols.append(jnp.concatenate([r[gs] for r in rows] + [pad_rows], axis=0).T)

    def col(g, q, h):
        return cols[g][:, SUBLANES * q + h:SUBLANES * q + h + 1]

    qs, ks, vs, kTs = [], [], [], []
    for h in range(M_HEADS):
        hs = slice(h * M_DH, (h + 1) * M_DH)
        qk_h = jnp.dot(xab[:, hs], wqk_ref[h], preferred_element_type=f32)
        qs.append(qk_h[:, 0:M_DH])
        ks.append(qk_h[:, M_DH:2 * M_DH] * (M_DH ** -0.5))
        vs.append(jnp.dot(xb[:, hs], wv_ref[h], preferred_element_type=f32))
        if S == 1:
            kT_h = lax.dot_general(wkT_ref[h], xab[:, hs], (((1,), (1,)), ((), ())),
                                   preferred_element_type=f32)
            kTs.append((kT_h * (M_DH ** -0.5)).astype(bf16))

    ti = lax.broadcasted_iota(jnp.int32, (R, R), 0)
    si = lax.broadcasted_iota(jnp.int32, (R, R), 1)
    mask = (si <= ti) & ((ti // L) == (si // L))
    ones_b = jnp.ones((R, M_DH), bf16)
    pairs = [(g, h) for g in range(G) for h in range(M_HEADS)]
    blk = lambda g: slice(g * R, (g + 1) * R)
    seqs = [slice(b * L, (b + 1) * L) for b in range(S)]
    q = {(g, h): qs[h][blk(g)] for g, h in pairs}
    k = {(g, h): ks[h][blk(g)] for g, h in pairs}
    v = {(g, h): vs[h][blk(g)] for g, h in pairs}
    qb = {p: q[p].astype(bf16) for p in pairs}
    kb = {p: k[p].astype(bf16) for p in pairs}
    qk = {p: lax.dot_general(qb[p], kb[p], (((1,), (1,)), ((), ())), preferred_element_type=f32)
          for p in pairs}
    sm = {}
    for g, h in pairs:
        a_row = a[g * SUBLANES + h:g * SUBLANES + h + 1, :]
        decay = jnp.exp(jnp.where(mask, a_row - col(g, 0, h), -jnp.inf))
        sm[g, h] = (qk[g, h] * decay).astype(bf16)
    nd = {p: jnp.dot(sm[p], jnp.concatenate([v[p].astype(bf16), ones_b], axis=1),
                     preferred_element_type=f32) for p in pairs}
    c_old = {(g, h, b): cN_ref[g * S + b, h] for g, h in pairs for b in range(S)}
    n_old = {(g, h, b): nN_ref[g * S + b, h] for g, h in pairs for b in range(S)}
    q_c = {(g, h, b): jnp.dot(qb[g, h][seqs[b]], c_old[g, h, b].astype(bf16), preferred_element_type=f32)
           for g, h in pairs for b in range(S)}
    hh = {}
    for g, h in pairs:
        g_col, e_col = col(g, 1, h), col(g, 2, h)
        pieces = []
        for b, rs in enumerate(seqs):
            q_n = jnp.sum(q[g, h][rs] * n_old[g, h, b], axis=1, keepdims=True)
            num = nd[g, h][rs, 0:M_DH] + g_col[rs] * q_c[g, h, b]
            den = nd[g, h][rs, M_DH:2 * M_DH] + g_col[rs] * q_n
            pieces.append(num / jnp.maximum(jnp.abs(den), e_col[rs]))
        hh[g, h] = pieces[0] if S == 1 else jnp.concatenate(pieces, axis=0)
    mu = {p: jnp.mean(hh[p], axis=1, keepdims=True) for p in pairs}
    var = {p: jnp.mean(jnp.square(hh[p] - mu[p]), axis=1, keepdims=True) for p in pairs}
    hn_blocks = [jnp.concatenate([(hh[g, h] - mu[g, h]) * lax.rsqrt(var[g, h] + LN_EPS)
                                  for h in range(M_HEADS)], axis=1) for g in range(G)]
    new_c, new_n = [], []
    for g, h in pairs:
        w_col = col(g, 3, h)
        wv = (w_col * v[g, h]).astype(bf16)
        wk = w_col * k[g, h]
        for b, rs in enumerate(seqs):
            g_end = col(g, 4, h)[(b + 1) * L - 1:(b + 1) * L, :]
            if S == 1:
                kv = jnp.dot(kTs[h][:, blk(g)], wv, preferred_element_type=f32)
            else:
                kv = lax.dot_general(kb[g, h][rs], wv[rs], (((0,), (0,)), ((), ())),
                                     preferred_element_type=f32)
            new_c.append((g * S + b, h, g_end * c_old[g, h, b] + kv))
            new_n.append((g * S + b, h, g_end * n_old[g, h, b] + jnp.sum(wk[rs], axis=0, keepdims=True)))

    hn = jnp.concatenate(hn_blocks, axis=0) * ng_ref[...]
    y = jax.nn.sigmoid(z_ref[...].reshape(GR, D_M)) * (hn + sk_ref[...] * xa)
    y_ref[...] = y.reshape(G, R, D_M)
    for sq, h, val in new_c:
        cN_ref[sq, h] = val
    for sq, h, val in new_n:
        nN_ref[sq, h] = val
    bN_ref[...] = xs_scr[:, SUBLANES - (CONV_W - 1):SUBLANES, :]


def _mlstm(proj, B, T, S, L, G, buf8, c0, n0, m0_lanes, cw, cb, wqk, wv, wkT, bif, ng, sk, riders=()):
    R = S * L
    assert R == M_ROWS == LANES
    nblk, nc = B // S, T // L
    nb = nblk // G
    assert nblk * S == B and nc * L == T and nb * G == nblk and (S == 1 or nc == 1)
    steps = nb * nc
    assert all(r.shape[0] % steps == 0 for r in riders)
    rider_specs = [pl.BlockSpec((r.shape[0] // steps,) + r.shape[1:], lambda b, j: (b * nc + j, 0, 0))
                   for r in riders]
    proj4 = proj.reshape(nblk, nc, R, N_PROJ)
    m0_3 = m0_lanes.reshape(SUBLANES, nblk, LANES).transpose(1, 0, 2)
    const2 = lambda b, j: (0, 0)
    const3 = lambda b, j: (0, 0, 0)
    gate_blk = (2 * D_RG + 2 * D_M) // LANES
    GS = G * S
    outs = pl.pallas_call(
        functools.partial(_mlstm_kernel, S=S, L=L, G=G, n_riders=len(riders)),
        grid=(nb, nc),
        in_specs=[
            pl.BlockSpec((G, None, R, D_M), lambda b, j: (b, j, 0, 2)),
            pl.BlockSpec((G, None, R, D_M), lambda b, j: (b, j, 0, 3)),
            pl.BlockSpec((G, None, R, LANES), lambda b, j: (b, j, 0, gate_blk)),
            pl.BlockSpec((GS, SUBLANES, D_M), lambda b, j: (b, 0, 0)),
            pl.BlockSpec((GS, M_HEADS, M_DH, M_DH), lambda b, j: (b, 0, 0, 0)),
            pl.BlockSpec((GS, M_HEADS, 1, M_DH), lambda b, j: (b, 0, 0, 0)),
            pl.BlockSpec((G, SUBLANES, LANES), lambda b, j: (b, 0, 0)),
            pl.BlockSpec((CONV_W, D_M), const2),
            pl.BlockSpec((1, D_M), const2),
            pl.BlockSpec((M_HEADS, M_DH, 2 * M_DH), const3),
            pl.BlockSpec((M_HEADS, M_DH, M_DH), const3),
            pl.BlockSpec((M_HEADS, M_DH, M_DH), const3),
            pl.BlockSpec((1, LANES), const2),
            pl.BlockSpec((1, D_M), const2),
            pl.BlockSpec((1, D_M), const2),
        ] + rider_specs,
        out_specs=[
            pl.BlockSpec((G, None, R, D_M), lambda b, j: (b, j, 0, 0)),
            pl.BlockSpec((GS, M_HEADS, M_DH, M_DH), lambda b, j: (b, 0, 0, 0)),
            pl.BlockSpec((GS, M_HEADS, 1, M_DH), lambda b, j: (b, 0, 0, 0)),
            pl.BlockSpec((G, None, SUBLANES, LANES), lambda b, j: (b, j, 0, 0)),
            pl.BlockSpec((GS, CONV_W - 1, D_M), lambda b, j: (b, 0, 0)),
        ] + rider_specs,
        out_shape=[
            jax.ShapeDtypeStruct((nblk, nc, R, D_M), f32),
            jax.ShapeDtypeStruct((B, M_HEADS, M_DH, M_DH), f32),
            jax.ShapeDtypeStruct((B, M_HEADS, 1, M_DH), f32),
            jax.ShapeDtypeStruct((nblk, nc, SUBLANES, LANES), f32),
            jax.ShapeDtypeStruct((B, CONV_W - 1, D_M), f32),
        ] + [jax.ShapeDtypeStruct(r.shape, bf16) for r in riders],
        scratch_shapes=[pltpu.VMEM((GS, SUBLANES, D_M), f32), pltpu.VMEM((G, SUBLANES, LANES), f32)],
        compiler_params=_cparams(2),
        name="mlstm",
    )(proj4, proj4, proj4, buf8, c0, n0, m0_3, cw, cb, wqk, wv, wkT, bif, ng, sk, *riders)
    y, c_new, n_new, m_t, b_new = outs[:5]
    m_last = m_t[:, nc - 1, :M_HEADS, :].reshape(nblk, M_HEADS, S, L)[:, :, :, L - 1]
    m_last = m_last.transpose(0, 2, 1).reshape(B, M_HEADS)
    return (y.reshape(B * T, D_M), c_new, n_new, m_last, b_new) + tuple(outs[5:])


def _layer_norm(x, g, b):
    mu = jnp.mean(x, axis=-1, keepdims=True)
    var = jnp.mean(jnp.square(x - mu), axis=-1, keepdims=True)
    return (x - mu) * lax.rsqrt(var + LN_EPS) * g + b


def _first_lane_of_max(vals, lane_f):
    vmax = jnp.max(vals, axis=1, keepdims=True)
    idx = jnp.min(jnp.where(vals == vmax, lane_f, float(LANES)), axis=1, keepdims=True)
    return vmax, idx


def _outproj_kernel(xp_ref, xs_ref, rgp_ref, rgs_ref, mp_ref, ms_ref, wo_ref, g1_ref, b1_ref,
                    wr_ref, br_ref, x1_ref, cnt_ref, *, n_p):
    i = pl.program_id(0)

    @pl.when(i == 0)
    def _():
        cnt_ref[...] = jnp.zeros_like(cnt_ref)

    def route(lg, lane, lane_f):
        neg = -jnp.inf
        gl = jnp.where(lane < N_GROUPS, lg, neg)
        gmax, gidx = _first_lane_of_max(gl, lane_f)
        p_g = 1.0 / jnp.sum(jnp.exp(gl - gmax), axis=1, keepdims=True)
        e_lo = float(N_GROUPS) + float(EXPERTS_PER_GROUP) * gidx
        el = jnp.where((lane_f >= e_lo) & (lane_f < e_lo + float(EXPERTS_PER_GROUP)), lg, neg)
        v1, i1 = _first_lane_of_max(el, lane_f)
        v2, i2 = _first_lane_of_max(jnp.where(lane_f == i1, neg, el), lane_f)
        d = jnp.exp(v2 - v1)
        w1 = p_g / (1.0 + d)
        w2 = p_g * d / (1.0 + d)
        first_low = i1 < i2
        j_lo = jnp.minimum(i1, i2) - e_lo
        j_hi = jnp.maximum(i1, i2) - e_lo
        n_first = float(EXPERTS_PER_GROUP - 1) * j_lo - 0.5 * j_lo * (j_lo - 1.0)
        cls = float(PAIRS_PER_GROUP) * gidx + n_first + (j_hi - j_lo - 1.0)
        return cls, jnp.where(first_low, w1, w2), jnp.where(first_low, w2, w1)

    def run(x_ref, rg_ref, m_ref):
        rp = TM_OUT // OUT_PARTS
        parts = [slice(p * rp, (p + 1) * rp) for p in range(OUT_PARTS)]
        lane = lax.broadcasted_iota(jnp.int32, (rp, LANES), 1)
        lane_f = lane.astype(f32)
        heads = [jnp.concatenate([rg_ref[r, :].astype(bf16), m_ref[r, :].astype(bf16)], axis=1) for r in parts]
        mix = [jnp.dot(h, wo_ref[...], preferred_element_type=f32) for h in heads]
        x1 = [_layer_norm(ALPHA * x_ref[r, :] + mx, g1_ref[...], b1_ref[...]) for r, mx in zip(parts, mix)]
        for r, v in zip(parts, x1):
            x1_ref[r, 0:D_MODEL] = v

        hi = [v.astype(bf16) for v in x1]
        lo = [(v - h.astype(f32)).astype(bf16) for v, h in zip(x1, hi)]
        hi_terms = [jnp.dot(h, wr_ref[...], preferred_element_type=f32) for h in hi]
        lo_term = [jnp.dot(v, wr_ref[:, 0:LANES], preferred_element_type=f32) for v in lo]
        lg = [ht[:, 0:LANES] + ht[:, LANES:2 * LANES] + lt + br_ref[...] for ht, lt in zip(hi_terms, lo_term)]
        routed = [route(v, lane, lane_f) for v in lg]

        ti = lax.broadcasted_iota(jnp.int32, (rp, rp), 0)
        si = lax.broadcasted_iota(jnp.int32, (rp, rp), 1)
        tri = (si <= ti).astype(bf16)
        hot = [lane_f == cls for cls, _, _ in routed]
        cum = [jnp.dot(tri, h.astype(bf16), preferred_element_type=f32) for h in hot]
        seen = cnt_ref[...]
        for r, h, cm, (cls, w_lo, w_hi) in zip(parts, hot, cum, routed):
            rank = jnp.sum(jnp.where(h, cm - 1.0 + seen, 0.0), axis=1, keepdims=True)
            seen = seen + cm[rp - 1:rp, :]
            info = jnp.zeros((rp, LANES), f32)
            for c, val in enumerate((cls, rank, w_lo, w_hi)):
                info = jnp.where(lane == c, val, info)
            x1_ref[r, D_MODEL:D_MODEL + LANES] = info
        cnt_ref[...] = seen

    @pl.when(i < n_p)
    def _():
        run(xp_ref, rgp_ref, mp_ref)

    @pl.when(i >= n_p)
    def _():
        run(xs_ref, rgs_ref, ms_ref)


def _outproj(xp, xs, rgp, rgs, mp, ms, wo, g1, b1, wr, br):
    n_p, n_s = xp.shape[0] // TM_OUT, xs.shape[0] // TM_OUT
    n = n_p + n_s
    pmap = lambda i: (jnp.minimum(i, n_p - 1), 0)
    smap = lambda i: (jnp.maximum(i - n_p, 0), 0)
    const = lambda i: (0, 0)
    return pl.pallas_call(
        functools.partial(_outproj_kernel, n_p=n_p),
        grid=(n,),
        in_specs=[
            pl.BlockSpec((TM_OUT, D_MODEL), pmap), pl.BlockSpec((TM_OUT, D_MODEL), smap),
            pl.BlockSpec((TM_OUT, D_RG), pmap), pl.BlockSpec((TM_OUT, D_RG), smap),
            pl.BlockSpec((TM_OUT, D_M), pmap), pl.BlockSpec((TM_OUT, D_M), smap),
            pl.BlockSpec((D_RG + D_M, D_MODEL), const),
            pl.BlockSpec((1, D_MODEL), const), pl.BlockSpec((1, D_MODEL), const),
            pl.BlockSpec((D_MODEL, 2 * LANES), const),
            pl.BlockSpec((1, LANES), const),
        ],
        out_specs=[
            pl.BlockSpec((TM_OUT, X_ROW), lambda i: (i, 0)),
            pl.BlockSpec((1, LANES), const),
        ],
        out_shape=[
            jax.ShapeDtypeStruct((n * TM_OUT, X_ROW), f32),
            jax.ShapeDtypeStruct((1, LANES), f32),
        ],
        compiler_params=_cparams(1),
        name="outproj",
    )(xp, xs, rgp, rgs, mp, ms, wo, g1, b1, wr, br)


GRP_MOVE = TM_MOVE // SUBLANES


def _grouped(x):
    return x.reshape(x.shape[0] // SUBLANES, SUBLANES, x.shape[1])


def _tile_rows_wait(hbm_ref, sem):
    rows = hbm_ref.at[pl.ds(0, TM_MOVE)]
    pltpu.make_async_copy(rows, rows, sem).wait()


def _for_each_tile_row(body):
    def group(k, c):
        for j in range(SUBLANES):
            body(k, j, k * SUBLANES + j)
        return c

    lax.fori_loop(0, GRP_MOVE, group, 0)


def _dispatch_kernel(pad_ref, pos_ref, x1_ref, xs_ref, zero_scr, xbuf, sem, zsem, lsem):
    i = pl.program_id(0)
    n = pl.num_programs(0)

    def zero_tile(t):
        return pltpu.make_async_copy(zero_scr, xs_ref.at[pl.ds(pl.multiple_of(t * TM_MOE, TM_MOE), TM_MOE)], zsem)

    def zero_chunk(row0):
        return pltpu.make_async_copy(zero_scr.at[pl.ds(0, SUBLANES)],
                                     xs_ref.at[pl.ds(pl.multiple_of(row0, SUBLANES), SUBLANES)], zsem)

    @pl.when(i == 0)
    def _():
        zero_scr[...] = jnp.zeros_like(zero_scr)
        n_used = pad_ref[2 * N_CLASSES]
        n_tiles = xs_ref.shape[0] // TM_MOE
        for go in (lambda cp: cp.start(), lambda cp: cp.wait()):
            lax.fori_loop(n_used, n_tiles, lambda t, c: (go(zero_tile(t)), c)[1], 0)

            def per_class(cl, c):
                row0 = pad_ref[2 * cl]
                lax.fori_loop(0, pad_ref[2 * cl + 1],
                              lambda k, c2: (go(zero_chunk(row0 + k * SUBLANES)), c2)[1], 0)
                return c

            lax.fori_loop(0, N_CLASSES, per_class, 0)

    def load(t, s):
        return pltpu.make_async_copy(x1_ref.at[pl.ds(t * GRP_MOVE, GRP_MOVE)], xbuf.at[s], lsem.at[s])

    @pl.when(i == 0)
    def _():
        load(0, 0).start()

    @pl.when(i + 1 < n)
    def _():
        load(i + 1, (i + 1) % 3).start()

    slot = i % 3
    load(i, slot).wait()

    def start(k, j, r):
        pltpu.make_async_copy(xbuf.at[slot, k, pl.ds(j, 1)], xs_ref.at[pl.ds(pos_ref[0, 0, r], 1)],
                              sem.at[slot]).start()

    _for_each_tile_row(start)

    @pl.when(i > 0)
    def _():
        _tile_rows_wait(xs_ref, sem.at[(i + 2) % 3])

    @pl.when(i == n - 1)
    def _():
        _tile_rows_wait(xs_ref, sem.at[slot])


def _dispatch(pad_plan, pos, x1, n_rows):
    n = x1.shape[0] // TM_MOVE
    return pl.pallas_call(
        _dispatch_kernel,
        grid_spec=pltpu.PrefetchScalarGridSpec(
            num_scalar_prefetch=1,
            grid=(n,),
            in_specs=[
                pl.BlockSpec((1, 1, TM_MOVE), lambda i, *_: (i, 0, 0), memory_space=pltpu.SMEM),
                pl.BlockSpec(memory_space=pl.ANY),
            ],
            out_specs=pl.BlockSpec(memory_space=pl.ANY),
            scratch_shapes=[pltpu.VMEM((TM_MOE, X_ROW), f32),
                            pltpu.VMEM((3, GRP_MOVE, SUBLANES, X_ROW), f32),
                            pltpu.SemaphoreType.DMA((3,)), pltpu.SemaphoreType.DMA(()),
                            pltpu.SemaphoreType.DMA((3,))],
        ),
        out_shape=jax.ShapeDtypeStruct((n_rows, X_ROW), f32),
        compiler_params=_cparams(1),
        name="dispatch",
    )(pad_plan, pos, _grouped(x1))


def _expert_kernel(ta_ref, tb_ref, ca_ref, cb_ref, sa_ref, sb_ref, nu_ref,
                   x_hbm, wg_hbm, wu_hbm, wd_hbm, g2_ref, b2_ref, y_ref,
                   wga, wua, wda, wgb, wub, wdb, sem_a, sem_b, xring, sem_x):
    i = pl.program_id(0)
    n_used = nu_ref[0]
    first = i * MOE_TILES
    side_a = (ta_ref, ca_ref, sa_ref, (wga, wua, wda), sem_a)
    side_b = (tb_ref, cb_ref, sb_ref, (wgb, wub, wdb), sem_b)

    def copies(side, j):
        t_ref, _, s_ref, bufs, sem = side
        e, s = t_ref[j], s_ref[j]
        return [pltpu.make_async_copy(w.at[e], buf.at[s], sem.at[s])
                for w, buf in zip((wg_hbm, wu_hbm, wd_hbm), bufs)]

    def fetch(j):
        @pl.when(j < n_used)
        def _():
            for side in (side_a, side_b):
                @pl.when(side[1][j] == 1)
                def _():
                    for cp in copies(side, j):
                        cp.start()

    @pl.when(i == 0)
    def _():
        for j in range(W_AHEAD):
            fetch(j)

    for t in range(MOE_TILES):
        fetch(first + W_AHEAD + t)

    step_rows = MOE_TILES * TM_MOE

    def x_copy(step):
        rows = pl.ds(pl.multiple_of(step * step_rows, step_rows), step_rows)
        return pltpu.make_async_copy(x_hbm.at[rows], xring.at[step % X_SLOTS], sem_x.at[step % X_SLOTS])

    def x_fetch(step):
        @pl.when(step * MOE_TILES < n_used)
        def _():
            x_copy(step).start()

    @pl.when(i == 0)
    def _():
        for s in range(X_AHEAD):
            x_fetch(jnp.int32(s))

    x_fetch(i + X_AHEAD)

    @pl.when(first < n_used)
    def _():
        x_copy(i).wait()
        x_ref = xring.at[i % X_SLOTS]
        for t in range(MOE_TILES):
            for side in (side_a, side_b):
                @pl.when(side[1][first + t] == 1)
                def _():
                    for cp in copies(side, first + t):
                        cp.wait()
        rows = [slice(t * TM_MOE, (t + 1) * TM_MOE) for t in range(MOE_TILES)]
        x = [x_ref[r, 0:D_MODEL] for r in rows]
        info = [x_ref[r, D_MODEL:X_ROW] for r in rows]
        xb = [v.astype(bf16) for v in x]
        units = [(t, bufs, s_ref[first + t]) for t in range(MOE_TILES)
                 for bufs, s_ref in (((wga, wua, wda), sa_ref), ((wgb, wub, wdb), sb_ref))]
        hg = [jnp.dot(xb[t], bufs[0][s], preferred_element_type=f32) for t, bufs, s in units]
        hu = [jnp.dot(xb[t], bufs[1][s], preferred_element_type=f32) for t, bufs, s in units]
        mid = [(g * jax.nn.sigmoid(g) * u).astype(bf16) for g, u in zip(hg, hu)]
        ys = [jnp.dot(m, bufs[2][s], preferred_element_type=f32)
              for m, (t, bufs, s) in zip(mid, units)]
        for t in range(MOE_TILES):
            ffn = info[t][:, 2:3] * ys[2 * t] + info[t][:, 3:4] * ys[2 * t + 1]
            y_ref[rows[t], :] = _layer_norm(ALPHA * x[t] + ffn, g2_ref[...], b2_ref[...])

    @pl.when(first >= n_used)
    def _():
        y_ref[...] = jnp.zeros_like(y_ref)


def _experts(tile_ea, tile_eb, n_used, xs, wg, wu, wd, g2, b2):
    nt = xs.shape[0] // TM_MOE

    def ring_plan(tile_e):
        opens = jnp.concatenate([jnp.ones((1,), jnp.int32), (tile_e[1:] != tile_e[:-1]).astype(jnp.int32)])
        return opens, (jnp.cumsum(opens) - 1) % W_SLOTS

    open_a, slot_a = ring_plan(tile_ea)
    open_b, slot_b = ring_plan(tile_eb)
    assert nt % MOE_TILES == 0
    n_used = ((n_used + MOE_TILES - 1) // MOE_TILES) * MOE_TILES
    const = lambda i, *_: (0, 0)
    w_in = pltpu.VMEM((W_SLOTS, D_MODEL, D_EXPERT), bf16)
    w_out = pltpu.VMEM((W_SLOTS, D_EXPERT, D_MODEL), bf16)
    return pl.pallas_call(
        _expert_kernel,
        grid_spec=pltpu.PrefetchScalarGridSpec(
            num_scalar_prefetch=7,
            grid=(nt // MOE_TILES,),
            in_specs=[
                pl.BlockSpec(memory_space=pl.ANY),
                pl.BlockSpec(memory_space=pl.ANY), pl.BlockSpec(memory_space=pl.ANY),
                pl.BlockSpec(memory_space=pl.ANY),
                pl.BlockSpec((1, D_MODEL), const), pl.BlockSpec((1, D_MODEL), const),
            ],
            out_specs=pl.BlockSpec((MOE_TILES * TM_MOE, D_MODEL), lambda i, *_: (i, 0)),
            scratch_shapes=[w_in, w_in, w_out, w_in, w_in, w_out,
                            pltpu.SemaphoreType.DMA((W_SLOTS,)), pltpu.SemaphoreType.DMA((W_SLOTS,)),
                            pltpu.VMEM((X_SLOTS, MOE_TILES * TM_MOE, X_ROW), f32),
                            pltpu.SemaphoreType.DMA((X_SLOTS,))],
        ),
        out_shape=jax.ShapeDtypeStruct((nt * TM_MOE, D_MODEL), f32),
        compiler_params=_cparams(1),
        name="experts",
    )(tile_ea, tile_eb, open_a, open_b, slot_a.astype(jnp.int32), slot_b.astype(jnp.int32), n_used,
      xs, wg, wu, wd, g2, b2)


def _collect_kernel(pos_ref, posn_ref, ys_ref, op_ref, os_ref, ybuf, sem, *, n_p):
    i = pl.program_id(0)
    n = pl.num_programs(0)
    slot = i % 2

    def gather(p_ref, s):
        def start(k, j, r):
            pltpu.make_async_copy(ys_ref.at[pl.ds(p_ref[0, 0, r], 1)], ybuf.at[s, k, pl.ds(j, 1)],
                                  sem.at[s]).start()

        _for_each_tile_row(start)

    @pl.when(i == 0)
    def _():
        gather(pos_ref, 0)

    @pl.when(i + 1 < n)
    def _():
        gather(posn_ref, 1 - slot)

    _tile_rows_wait(ys_ref, sem.at[slot])

    @pl.when(i < n_p)
    def _():
        op_ref[...] = ybuf[slot].reshape(TM_MOVE, D_MODEL)

    @pl.when(i >= n_p)
    def _():
        os_ref[...] = ybuf[slot].reshape(TM_MOVE, D_MODEL)


def _collect(pos, ys, n_p):
    n = pos.shape[0]
    n_s = n - n_p
    return pl.pallas_call(
        functools.partial(_collect_kernel, n_p=n_p),
        grid=(n,),
        in_specs=[
            pl.BlockSpec((1, 1, TM_MOVE), lambda i: (i, 0, 0), memory_space=pltpu.SMEM),
            pl.BlockSpec((1, 1, TM_MOVE), lambda i: (jnp.minimum(i + 1, n - 1), 0, 0),
                         memory_space=pltpu.SMEM),
            pl.BlockSpec(memory_space=pl.ANY),
        ],
        out_specs=[
            pl.BlockSpec((TM_MOVE, D_MODEL), lambda i: (jnp.minimum(i, n_p - 1), 0)),
            pl.BlockSpec((TM_MOVE, D_MODEL), lambda i: (jnp.maximum(i - n_p, 0), 0)),
        ],
        out_shape=[
            jax.ShapeDtypeStruct((n_p * TM_MOVE, D_MODEL), f32),
            jax.ShapeDtypeStruct((n_s * TM_MOVE, D_MODEL), f32),
        ],
        scratch_shapes=[pltpu.VMEM((2, GRP_MOVE, SUBLANES, D_MODEL), f32), pltpu.SemaphoreType.DMA((2,))],
        compiler_params=_cparams(1),
        name="collect",
    )(pos, pos, ys)


def _block_diag(w):
    n, d, _ = w.shape
    eye = jnp.eye(n, dtype=w.dtype)
    return (eye[:, None, :, None] * w[:, :, None, :]).reshape(n * d, n * d)


def _pair_walk(n):
    total = n * (n - 1) // 2

    def extend(path, used):
        if len(path) == total:
            return path
        a, b = path[-1]
        for c in range(n):
            for nxt in ((a, c), (c, b)):
                key = frozenset(nxt)
                if len(key) == 2 and key not in used:
                    out = extend(path + [nxt], used | {key})
                    if out:
                        return out
        return None

    return extend([(0, 1)], {frozenset((0, 1))})


def _pad_history(buf):
    return jnp.pad(buf, ((0, 0), (SUBLANES - (CONV_W - 1), 0), (0, 0)))


def kernel(x_prompt, x_sample, state_rg_h, state_rg_conv, state_m_C, state_m_n, state_m_m, state_m_conv, w_in, rg_conv_w, rg_conv_b, rg_w_a, rg_b_a, rg_w_x, rg_b_x, rg_lambda, m_conv_w, m_conv_b, m_w_q, m_w_k, m_w_v, m_b_i, m_b_f, m_norm_g, m_skip, w_out, ln1_g, ln1_b, ln2_g, ln2_b, moe_w_group, moe_b_group, moe_w_expert, moe_b_expert, moe_w_gate, moe_w_up, moe_w_down):
    BP, TP, _ = x_prompt.shape
    BS, TS, _ = x_sample.shape
    n_prompt, n_sample = BP * TP, BS * TS
    xp = x_prompt.reshape(n_prompt, D_MODEL)
    xs = x_sample.reshape(n_sample, D_MODEL)
    l = 0

    w_in_p = jnp.pad(w_in[l], ((0, 0), (0, N_PROJ - w_in.shape[-1]))).astype(bf16)
    wa = _block_diag(rg_w_a[l]).astype(bf16)
    wx = _block_diag(rg_w_x[l]).astype(bf16)
    row = lambda v: v.reshape(1, -1)
    bif = jnp.pad(jnp.concatenate([m_b_i[l], m_b_f[l]]), (0, LANES - 2 * M_HEADS)).reshape(1, LANES)
    wqk = jnp.concatenate([m_w_q[l], m_w_k[l]], axis=-1).astype(bf16)
    wv = m_w_v[l].astype(bf16)
    wkT = m_w_k[l].transpose(0, 2, 1).astype(bf16)
    w_route = jnp.pad(jnp.concatenate([moe_w_group[l], moe_w_expert[l]], axis=1),
                      ((0, 0), (0, LANES - N_GROUPS - N_EXPERTS)))
    wr_hi = w_route.astype(bf16)
    wr_lo = (w_route - wr_hi.astype(f32)).astype(bf16)
    wr = jnp.concatenate([wr_hi, wr_lo], axis=1)
    b_route = jnp.pad(jnp.concatenate([moe_b_group[l], moe_b_expert[l]]),
                      (0, LANES - N_GROUPS - N_EXPERTS)).reshape(1, LANES)

    proj_p, proj_s = _inproj(xp, xs, w_in_p)

    rg_args = (rg_conv_w[l], row(rg_conv_b[l]), wa, row(rg_b_a[l]), wx, row(rg_b_x[l]), row(rg_lambda[l]))
    zeros = lambda *s: jnp.zeros(s, f32)
    s_blk = M_ROWS // TS
    yrg_p, p_rg_h, p_rg_conv = _rglru(proj_p, 0, BP, TP, 1, RG_ROWS, zeros(BP, SUBLANES, D_RG),
                                      zeros(BP, 1, D_RG), *rg_args)
    yrg_s, s_rg_h, s_rg_conv = _rglru(proj_s, 0, BS, TS, RG_ROWS // TS, TS,
                                      _pad_history(state_rg_conv[l]), state_rg_h[l].reshape(BS, 1, D_RG), *rg_args)

    m_args = (m_conv_w[l], row(m_conv_b[l]), wqk, wv, wkT, bif, row(m_norm_g[l]), row(m_skip[l]))
    m0_p = jnp.full((SUBLANES, BP * LANES), M_INIT, f32)
    ym_p, p_m_C, p_m_n, p_m_m, p_m_conv, wg_b, wu_b, wd_b = _mlstm(
        proj_p, BP, TP, 1, M_ROWS, M_BLOCKS_PROMPT, zeros(BP, SUBLANES, D_M), zeros(BP, M_HEADS, M_DH, M_DH),
        zeros(BP, M_HEADS, 1, M_DH), m0_p, *m_args, riders=(moe_w_gate[l], moe_w_up[l], moe_w_down[l]))
    m0_s = jnp.pad(jnp.repeat(state_m_m[l].T, TS, axis=1), ((0, SUBLANES - M_HEADS), (0, 0)))
    ym_s, s_m_C, s_m_n, s_m_m, s_m_conv = _mlstm(
        proj_s, BS, TS, s_blk, TS, M_BLOCKS_SAMPLE, _pad_history(state_m_conv[l]), state_m_C[l],
        state_m_n[l].reshape(BS, M_HEADS, 1, M_DH), m0_s, *m_args)

    x1, cnt = _outproj(xp, xs, yrg_p, yrg_s, ym_p, ym_s, w_out[l].astype(bf16),
                       row(ln1_g[l]), row(ln1_b[l]), wr, b_route)

    n_tok = n_prompt + n_sample
    n_tiles = n_tok // TM_MOE + N_CLASSES
    lay_cls, lay_a, lay_b = [], [], []
    for g in range(N_GROUPS):
        for ja, jb in _pair_walk(EXPERTS_PER_GROUP):
            assert ja < jb
            n_first = (EXPERTS_PER_GROUP - 1) * ja - ja * (ja - 1) // 2
            lay_cls.append(PAIRS_PER_GROUP * g + n_first + jb - ja - 1)
            lay_a.append(g * EXPERTS_PER_GROUP + ja)
            lay_b.append(g * EXPERTS_PER_GROUP + jb)
    classes = jnp.arange(N_CLASSES, dtype=jnp.int32)
    in_slot = jnp.array(lay_cls, jnp.int32)[:, None] == classes
    counts = jnp.sum(jnp.where(in_slot, cnt[0, :N_CLASSES].astype(jnp.int32), 0), axis=1)
    padded = ((counts + TM_MOE - 1) // TM_MOE) * TM_MOE
    ends = jnp.cumsum(padded)
    offs = ends - padded
    offs_of_cls = jnp.sum(jnp.where(in_slot, offs[:, None], 0), axis=0)
    cls = x1[:, D_MODEL].astype(jnp.int32)
    rank = x1[:, D_MODEL + 1].astype(jnp.int32)
    pos = jnp.sum(jnp.where(cls[:, None] == classes, offs_of_cls, 0), axis=-1) + rank
    pos = pos.reshape(n_tok // TM_MOVE, 1, TM_MOVE)
    tiles = jnp.arange(n_tiles, dtype=jnp.int32)
    tile_slot = jnp.minimum(jnp.sum(tiles[:, None] >= (ends // TM_MOE)[None, :], axis=1), N_CLASSES - 1)
    on_slot = tile_slot[:, None] == classes
    tile_ea = jnp.sum(jnp.where(on_slot, jnp.array(lay_a, jnp.int32), 0), axis=-1).astype(jnp.int32)
    tile_eb = jnp.sum(jnp.where(on_slot, jnp.array(lay_b, jnp.int32), 0), axis=-1).astype(jnp.int32)
    n_used = (ends[-1] // TM_MOE).reshape(1).astype(jnp.int32)
    pad_row0 = ((offs + counts) // SUBLANES) * SUBLANES
    pad_plan = jnp.concatenate([jnp.stack([pad_row0, (ends - pad_row0) // SUBLANES], axis=1).reshape(-1),
                                     n_used]).astype(jnp.int32)

    x_sorted = _dispatch(pad_plan, pos, x1, n_tiles * TM_MOE)
    y_sorted = _experts(tile_ea, tile_eb, n_used, x_sorted, wg_b, wu_b, wd_b,
                        row(ln2_g[l]), row(ln2_b[l]))
    y_p, y_s = _collect(pos, y_sorted, n_prompt // TM_MOVE)

    return (y_p.reshape(BP, TP, D_MODEL), y_s.reshape(BS, TS, D_MODEL),
            p_rg_h.reshape(1, BP, D_RG), p_rg_conv[None], p_m_C[None], p_m_n.reshape(1, BP, M_HEADS, M_DH),
            p_m_m[None], p_m_conv[None],
            s_rg_h.reshape(1, BS, D_RG), s_rg_conv[None], s_m_C[None], s_m_n.reshape(1, BS, M_HEADS, M_DH),
            s_m_m[None], s_m_conv[None])
```

```python
import functools

import jax
import jax.numpy as jnp
from jax import lax
from jax.experimental import pallas as pl
from jax.experimental.pallas import tpu as pltpu

f32 = jnp.float32
bf16 = jnp.bfloat16

D_MODEL = 1024
D_RG = 512
RG_C = 8.0
D_M = 512
M_HEADS = 4
M_DH = 128
CONV_W = 4
N_GROUPS = 4
EXPERTS_PER_GROUP = 8
N_EXPERTS = 32
D_EXPERT = 256
ALPHA = 2.0 ** 0.25
LN_EPS = 1e-5
M_INIT = -1.0e4

LANES = 128
SUBLANES = 8
TM = 512
TM_MOVE = 1024
TM_OUT = 1024
OUT_PARTS = 8
TM_MOE = 128
MOE_TILES = 4
W_AHEAD = 8
W_SLOTS = W_AHEAD + MOE_TILES
PAIRS_PER_GROUP = EXPERTS_PER_GROUP * (EXPERTS_PER_GROUP - 1) // 2
N_CLASSES = N_GROUPS * PAIRS_PER_GROUP
X_ROW = D_MODEL + LANES
N_PROJ = 2 * D_RG + 2 * D_M + LANES
RG_ROWS = 1024
M_ROWS = 128
M_BLOCKS_PROMPT = 8
M_BLOCKS_SAMPLE = 2
V7X_VMEM_BYTES = 64 * 1024 * 1024
VMEM_LIMIT = V7X_VMEM_BYTES - 8 * 1024 * 1024


def _cparams(n_axes):
    return pltpu.CompilerParams(dimension_semantics=("arbitrary",) * n_axes,
                                vmem_limit_bytes=VMEM_LIMIT)


def _inproj_kernel(xp_ref, xs_ref, w_ref, op_ref, os_ref, *, n_p):
    i = pl.program_id(0)

    def run(x_ref, o_ref):
        o_ref[...] = jnp.dot(x_ref[...].astype(bf16), w_ref[...], preferred_element_type=f32)

    @pl.when(i < n_p)
    def _():
        run(xp_ref, op_ref)

    @pl.when(i >= n_p)
    def _():
        run(xs_ref, os_ref)


def _inproj(xp, xs, w):
    n_p, n_s = xp.shape[0] // TM, xs.shape[0] // TM
    pmap = lambda i: (jnp.minimum(i, n_p - 1), 0)
    smap = lambda i: (jnp.maximum(i - n_p, 0), 0)
    return pl.pallas_call(
        functools.partial(_inproj_kernel, n_p=n_p),
        grid=(n_p + n_s,),
        in_specs=[
            pl.BlockSpec((TM, D_MODEL), pmap),
            pl.BlockSpec((TM, D_MODEL), smap),
            pl.BlockSpec((D_MODEL, N_PROJ), lambda i: (0, 0)),
        ],
        out_specs=[pl.BlockSpec((TM, N_PROJ), pmap), pl.BlockSpec((TM, N_PROJ), smap)],
        out_shape=[jax.ShapeDtypeStruct((n_p * TM, N_PROJ), f32),
                   jax.ShapeDtypeStruct((n_s * TM, N_PROJ), f32)],
        compiler_params=_cparams(1),
        name="inproj",
    )(xp, xs, w)


def _causal_conv(tail_scr, x, cw_ref, cb_ref, S, L, C):
    tail = tail_scr[...]
    sub = lax.broadcasted_iota(jnp.int32, (S, SUBLANES, C), 1)
    acc = cb_ref[...] + cw_ref[CONV_W - 1:CONV_W, :] * x
    for d in range(1, CONV_W):
        back = pltpu.roll(x, d, 0).reshape(S, L, C)
        head = jnp.where(sub < d, pltpu.roll(tail, d, 1), back[:, 0:SUBLANES, :])
        if L > SUBLANES:
            back = jnp.concatenate([head, back[:, SUBLANES:, :]], axis=1)
        else:
            back = head
        acc = acc + cw_ref[CONV_W - 1 - d:CONV_W - d, :] * back.reshape(S * L, C)
    tail_scr[...] = x.reshape(S, L, C)[:, L - SUBLANES:, :]
    return acc


def _softplus(x):
    return jnp.maximum(x, 0.0) + jnp.log1p(jnp.exp(-jnp.abs(x)))


def _rglru_kernel(*refs, S, L, fresh):
    x_ref, g_ref = refs[:2]
    buf_ref, h0_ref = (None, None) if fresh else refs[2:4]
    (cw_ref, cb_ref, wa_ref, ba_ref, wx_ref, bx_ref, lam_ref,
     y_ref, hN_ref, cN_ref, xs_scr) = refs[2 if fresh else 4:]
    R = S * L
    t = pl.program_id(1)

    @pl.when(t == 0)
    def _():
        xs_scr[...] = jnp.zeros_like(xs_scr) if fresh else buf_ref[...]
        hN_ref[...] = jnp.zeros_like(hN_ref) if fresh else h0_ref[...]

    x = x_ref[...]
    xc = _causal_conv(xs_scr, x, cw_ref, cb_ref, S, L, D_RG)
    xcb = xc.astype(bf16)
    r = jax.nn.sigmoid(jnp.dot(xcb, wa_ref[...], preferred_element_type=f32) + ba_ref[...])
    ig = jax.nn.sigmoid(jnp.dot(xcb, wx_ref[...], preferred_element_type=f32) + bx_ref[...])
    log_a = (-RG_C) * r * _softplus(-lam_ref[...])
    a = jnp.exp(log_a)
    th = jnp.tanh(log_a)
    u = jnp.sqrt(-2.0 * th / (1.0 - th)) * ig * xc

    n_grp, grp_per_seq = R // SUBLANES, L // SUBLANES
    a3 = a.reshape(n_grp, SUBLANES, D_RG)
    u3 = u.reshape(n_grp, SUBLANES, D_RG)
    sub = lax.broadcasted_iota(jnp.int32, (n_grp, SUBLANES, D_RG), 1)
    s = 1
    while s < SUBLANES:
        ok = sub >= s
        a_sh = pltpu.roll(a3, s, 1)
        u_sh = pltpu.roll(u3, s, 1)
        u3 = jnp.where(ok, a3 * u_sh + u3, u3)
        a3 = jnp.where(ok, a3 * a_sh, a3)
        s *= 2
    h0 = hN_ref[...]
    groups = []
    for kg in range(n_grp):
        carry = h0[kg // grp_per_seq] if kg % grp_per_seq == 0 else groups[-1][SUBLANES - 1:SUBLANES, :]
        groups.append(a3[kg] * carry + u3[kg])
    h = jnp.concatenate(groups, axis=0)

    y_ref[...] = h * jax.nn.gelu(g_ref[...], approximate=True)
    hN_ref[...] = h.reshape(S, L, D_RG)[:, L - 1:L, :]
    cN_ref[...] = xs_scr[:, SUBLANES - (CONV_W - 1):SUBLANES, :]


def _rglru(proj, row0, B, T, S, L, state, cw, cb, wa, ba, wx, bx, lam):
    R = S * L
    nb, nt = B // S, T // L
    blk0 = row0 // R
    row_map = lambda b, t: (blk0 + b * nt + t, 0)
    const2 = lambda b, t: (0, 0)
    state_specs = [] if state is None else [pl.BlockSpec((S, SUBLANES, D_RG), lambda b, t: (b, 0, 0)),
                                            pl.BlockSpec((S, 1, D_RG), lambda b, t: (b, 0, 0))]
    return pl.pallas_call(
        functools.partial(_rglru_kernel, S=S, L=L, fresh=state is None),
        grid=(nb, nt),
        in_specs=[
            pl.BlockSpec((R, D_RG), row_map),
            pl.BlockSpec((R, D_RG), lambda b, t: (blk0 + b * nt + t, 1)),
        ] + state_specs + [
            pl.BlockSpec((CONV_W, D_RG), const2),
            pl.BlockSpec((1, D_RG), const2),
            pl.BlockSpec((D_RG, D_RG), const2),
            pl.BlockSpec((1, D_RG), const2),
            pl.BlockSpec((D_RG, D_RG), const2),
            pl.BlockSpec((1, D_RG), const2),
            pl.BlockSpec((1, D_RG), const2),
        ],
        out_specs=[
            pl.BlockSpec((R, D_RG), lambda b, t: (b * nt + t, 0)),
            pl.BlockSpec((S, 1, D_RG), lambda b, t: (b, 0, 0)),
            pl.BlockSpec((S, CONV_W - 1, D_RG), lambda b, t: (b, 0, 0)),
        ],
        out_shape=[
            jax.ShapeDtypeStruct((B * T, D_RG), f32),
            jax.ShapeDtypeStruct((B, 1, D_RG), f32),
            jax.ShapeDtypeStruct((B, CONV_W - 1, D_RG), f32),
        ],
        scratch_shapes=[pltpu.VMEM((S, SUBLANES, D_RG), f32)],
        compiler_params=_cparams(2),
        name="rglru",
    )(proj, proj, *(state or ()), cw, cb, wa, ba, wx, bx, lam)


def _seg_scan(x, op, fill, tin, L, reverse=False):
    s = 1
    while s < L:
        if reverse:
            sh = pltpu.roll(x, LANES - s, 1)
            ok = tin < L - s
        else:
            sh = pltpu.roll(x, s, 1)
            ok = tin >= s
        x = op(x, jnp.where(ok, sh, fill))
        s *= 2
    return x


def _mlstm_kernel(*refs, S, L, G, n_riders, fresh):
    x_ref, z_ref, gt_ref = refs[:3]
    n_state = 0 if fresh else 4
    state_refs = refs[3:3 + n_state]
    n_in = 11 + n_state
    cw_ref, cb_ref, wqk_ref, wv_ref, wkT_ref, bif_ref, ng_ref, sk_ref = refs[3 + n_state:n_in]
    rider_in = refs[n_in:n_in + n_riders]
    y_ref, cN_ref, nN_ref, mN_ref, bN_ref = refs[n_in + n_riders:n_in + n_riders + 5]
    rider_out = refs[n_in + n_riders + 5:n_in + 2 * n_riders + 5]
    xs_scr, m_scr = refs[n_in + 2 * n_riders + 5:]
    for src, dst in zip(rider_in, rider_out):
        dst[...] = src[...].astype(bf16)

    R = S * L
    GS, GR = G * S, G * R
    j = pl.program_id(1)

    @pl.when(j == 0)
    def _():
        if fresh:
            xs_scr[...] = jnp.zeros_like(xs_scr)
            cN_ref[...] = jnp.zeros_like(cN_ref)
            nN_ref[...] = jnp.zeros_like(nN_ref)
            m_scr[...] = jnp.full(m_scr.shape, M_INIT, f32)
        else:
            buf_ref, c0_ref, n0_ref, m0_ref = state_refs
            xs_scr[...] = buf_ref[...]
            cN_ref[...] = c0_ref[...]
            nN_ref[...] = n0_ref[...]
            m_scr[...] = m0_ref[...]

    x = x_ref[...].reshape(GR, D_M)
    xc = _causal_conv(xs_scr, x, cw_ref, cb_ref, GS, L, D_M)
    xa = xc * jax.nn.sigmoid(xc)
    xab = xa.astype(bf16)
    xb = x.astype(bf16)

    il_parts, f_parts = [], []
    for g in range(G):
        gT = (gt_ref[g] + bif_ref[...]).T
        il_parts.append(gT[0:SUBLANES, :])
        f_parts.append(pltpu.roll(gT[0:SUBLANES, :], M_HEADS, 0))
    il = jnp.concatenate(il_parts, axis=0)
    fl = -_softplus(-jnp.concatenate(f_parts, axis=0))
    tin = lax.broadcasted_iota(jnp.int32, (G * SUBLANES, LANES), 1) % L
    bcum = _seg_scan(fl, jnp.add, 0.0, tin, L)
    a = il - bcum
    m_prev = m_scr[...].reshape(G * SUBLANES, LANES)
    big_m = jnp.maximum(m_prev, _seg_scan(a, jnp.maximum, -jnp.inf, tin, L))
    m_t = bcum + big_m
    if S == 1:
        m_last = jnp.broadcast_to(big_m[:, LANES - 1:LANES], big_m.shape)
    else:
        m_last = _seg_scan(big_m, jnp.maximum, -jnp.inf, tin, L, reverse=True)
    rows = [big_m, jnp.exp(m_prev - big_m), jnp.exp(-m_t), jnp.exp(a - m_last), jnp.exp(m_prev - m_last)]
    mN_ref[...] = m_t.reshape(G, SUBLANES, LANES)
    m_scr[...] = jnp.broadcast_to(m_t[:, LANES - 1:LANES], m_t.shape).reshape(G, SUBLANES, LANES)
    pad_rows = jnp.zeros((LANES - len(rows) * SUBLANES, LANES), f32)
    cols = []
    for g in range(G):
        gs = slice(g * SUBLANES, (g + 1) * SUBLANES)
        cols.append(jnp.concatenate([r[gs] for r in rows] + [pad_rows], axis=0).T)

    def col(g, q, h):
        return cols[g][:, SUBLANES * q + h:SUBLANES * q + h + 1]

    qs, ks, vs, kTs = [], [], [], []
    for h in range(M_HEADS):
        hs = slice(h * M_DH, (h + 1) * M_DH)
        qk_h = jnp.dot(xab[:, hs], wqk_ref[h], preferred_element_type=f32)
        qs.append(qk_h[:, 0:M_DH])
        ks.append(qk_h[:, M_DH:2 * M_DH] * (M_DH ** -0.5))
        vs.append(jnp.dot(xb[:, hs], wv_ref[h], preferred_element_type=f32))
        if S == 1:
            kT_h = lax.dot_general(wkT_ref[h], xab[:, hs], (((1,), (1,)), ((), ())),
                                   preferred_element_type=f32)
            kTs.append((kT_h * (M_DH ** -0.5)).astype(bf16))

    ti = lax.broadcasted_iota(jnp.int32, (R, R), 0)
    si = lax.broadcasted_iota(jnp.int32, (R, R), 1)
    mask = (si <= ti) & ((ti // L) == (si // L))
    ones_b = jnp.ones((R, M_DH), bf16)
    pairs = [(g, h) for g in range(G) for h in range(M_HEADS)]
    blk = lambda g: slice(g * R, (g + 1) * R)
    seqs = [slice(b * L, (b + 1) * L) for b in range(S)]
    q = {(g, h): qs[h][blk(g)] for g, h in pairs}
    k = {(g, h): ks[h][blk(g)] for g, h in pairs}
    v = {(g, h): vs[h][blk(g)] for g, h in pairs}
    qb = {p: q[p].astype(bf16) for p in pairs}
    kb = {p: k[p].astype(bf16) for p in pairs}
    qk = {p: lax.dot_general(qb[p], kb[p], (((1,), (1,)), ((), ())), preferred_element_type=f32)
          for p in pairs}
    sm = {}
    for g, h in pairs:
        a_row = a[g * SUBLANES + h:g * SUBLANES + h + 1, :]
        decay = jnp.exp(jnp.where(mask, a_row - col(g, 0, h), -jnp.inf))
        sm[g, h] = (qk[g, h] * decay).astype(bf16)
    nd = {p: jnp.dot(sm[p], jnp.concatenate([v[p].astype(bf16), ones_b], axis=1),
                     preferred_element_type=f32) for p in pairs}
    c_old = {(g, h, b): cN_ref[g * S + b, h] for g, h in pairs for b in range(S)}
    n_old = {(g, h, b): nN_ref[g * S + b, h] for g, h in pairs for b in range(S)}
    q_c = {(g, h, b): jnp.dot(qb[g, h][seqs[b]], c_old[g, h, b].astype(bf16), preferred_element_type=f32)
           for g, h in pairs for b in range(S)}
    hh = {}
    for g, h in pairs:
        g_col, e_col = col(g, 1, h), col(g, 2, h)
        pieces = []
        for b, rs in enumerate(seqs):
            q_n = jnp.sum(q[g, h][rs] * n_old[g, h, b], axis=1, keepdims=True)
            num = nd[g, h][rs, 0:M_DH] + g_col[rs] * q_c[g, h, b]
            den = nd[g, h][rs, M_DH:2 * M_DH] + g_col[rs] * q_n
            pieces.append(num / jnp.maximum(jnp.abs(den), e_col[rs]))
        hh[g, h] = pieces[0] if S == 1 else jnp.concatenate(pieces, axis=0)
    mu = {p: jnp.mean(hh[p], axis=1, keepdims=True) for p in pairs}
    var = {p: jnp.mean(jnp.square(hh[p] - mu[p]), axis=1, keepdims=True) for p in pairs}
    hn_blocks = [jnp.concatenate([(hh[g, h] - mu[g, h]) * lax.rsqrt(var[g, h] + LN_EPS)
                                  for h in range(M_HEADS)], axis=1) for g in range(G)]
    new_c, new_n = [], []
    for g, h in pairs:
        w_col = col(g, 3, h)
        wv = (w_col * v[g, h]).astype(bf16)
        wk = w_col * k[g, h]
        for b, rs in enumerate(seqs):
            g_end = col(g, 4, h)[(b + 1) * L - 1:(b + 1) * L, :]
            if S == 1:
                kv = jnp.dot(kTs[h][:, blk(g)], wv, preferred_element_type=f32)
            else:
                kv = lax.dot_general(kb[g, h][rs], wv[rs], (((0,), (0,)), ((), ())),
                                     preferred_element_type=f32)
            new_c.append((g * S + b, h, g_end * c_old[g, h, b] + kv))
            new_n.append((g * S + b, h, g_end * n_old[g, h, b] + jnp.sum(wk[rs], axis=0, keepdims=True)))

    hn = jnp.concatenate(hn_blocks, axis=0) * ng_ref[...]
    y = jax.nn.sigmoid(z_ref[...].reshape(GR, D_M)) * (hn + sk_ref[...] * xa)
    y_ref[...] = y.reshape(G, R, D_M)
    for sq, h, val in new_c:
        cN_ref[sq, h] = val
    for sq, h, val in new_n:
        nN_ref[sq, h] = val
    bN_ref[...] = xs_scr[:, SUBLANES - (CONV_W - 1):SUBLANES, :]


def _mlstm(proj, B, T, S, L, G, state, cw, cb, wqk, wv, wkT, bif, ng, sk, riders=()):
    R = S * L
    assert R == M_ROWS == LANES
    nblk, nc = B // S, T // L
    nb = nblk // G
    assert nblk * S == B and nc * L == T and nb * G == nblk and (S == 1 or nc == 1)
    steps = nb * nc
    assert all(r.shape[0] % steps == 0 for r in riders)
    rider_specs = [pl.BlockSpec((r.shape[0] // steps,) + r.shape[1:], lambda b, j: (b * nc + j, 0, 0))
                   for r in riders]
    proj4 = proj.reshape(nblk, nc, R, N_PROJ)
    const2 = lambda b, j: (0, 0)
    const3 = lambda b, j: (0, 0, 0)
    gate_blk = (2 * D_RG + 2 * D_M) // LANES
    GS = G * S
    state_args, state_specs = (), []
    if state is not None:
        buf8, c0, n0, m0_lanes = state
        state_args = (buf8, c0, n0, m0_lanes.reshape(SUBLANES, nblk, LANES).transpose(1, 0, 2))
        state_specs = [
            pl.BlockSpec((GS, SUBLANES, D_M), lambda b, j: (b, 0, 0)),
            pl.BlockSpec((GS, M_HEADS, M_DH, M_DH), lambda b, j: (b, 0, 0, 0)),
            pl.BlockSpec((GS, M_HEADS, 1, M_DH), lambda b, j: (b, 0, 0, 0)),
            pl.BlockSpec((G, SUBLANES, LANES), lambda b, j: (b, 0, 0)),
        ]
    outs = pl.pallas_call(
        functools.partial(_mlstm_kernel, S=S, L=L, G=G, n_riders=len(riders), fresh=state is None),
        grid=(nb, nc),
        in_specs=[
            pl.BlockSpec((G, None, R, D_M), lambda b, j: (b, j, 0, 2)),
            pl.BlockSpec((G, None, R, D_M), lambda b, j: (b, j, 0, 3)),
            pl.BlockSpec((G, None, R, LANES), lambda b, j: (b, j, 0, gate_blk)),
        ] + state_specs + [
            pl.BlockSpec((CONV_W, D_M), const2),
            pl.BlockSpec((1, D_M), const2),
            pl.BlockSpec((M_HEADS, M_DH, 2 * M_DH), const3),
            pl.BlockSpec((M_HEADS, M_DH, M_DH), const3),
            pl.BlockSpec((M_HEADS, M_DH, M_DH), const3),
            pl.BlockSpec((1, LANES), const2),
            pl.BlockSpec((1, D_M), const2),
            pl.BlockSpec((1, D_M), const2),
        ] + rider_specs,
        out_specs=[
            pl.BlockSpec((G, None, R, D_M), lambda b, j: (b, j, 0, 0)),
            pl.BlockSpec((GS, M_HEADS, M_DH, M_DH), lambda b, j: (b, 0, 0, 0)),
            pl.BlockSpec((GS, M_HEADS, 1, M_DH), lambda b, j: (b, 0, 0, 0)),
            pl.BlockSpec((G, None, SUBLANES, LANES), lambda b, j: (b, j, 0, 0)),
            pl.BlockSpec((GS, CONV_W - 1, D_M), lambda b, j: (b, 0, 0)),
        ] + rider_specs,
        out_shape=[
            jax.ShapeDtypeStruct((nblk, nc, R, D_M), f32),
            jax.ShapeDtypeStruct((B, M_HEADS, M_DH, M_DH), f32),
            jax.ShapeDtypeStruct((B, M_HEADS, 1, M_DH), f32),
            jax.ShapeDtypeStruct((nblk, nc, SUBLANES, LANES), f32),
            jax.ShapeDtypeStruct((B, CONV_W - 1, D_M), f32),
        ] + [jax.ShapeDtypeStruct(r.shape, bf16) for r in riders],
        scratch_shapes=[pltpu.VMEM((GS, SUBLANES, D_M), f32), pltpu.VMEM((G, SUBLANES, LANES), f32)],
        compiler_params=_cparams(2),
        name="mlstm",
    )(proj4, proj4, proj4, *state_args, cw, cb, wqk, wv, wkT, bif, ng, sk, *riders)
    y, c_new, n_new, m_t, b_new = outs[:5]
    m_last = m_t[:, nc - 1, :M_HEADS, :].reshape(nblk, M_HEADS, S, L)[:, :, :, L - 1]
    m_last = m_last.transpose(0, 2, 1).reshape(B, M_HEADS)
    return (y.reshape(B * T, D_M), c_new, n_new, m_last, b_new) + tuple(outs[5:])


def _layer_norm(x, g, b):
    mu = jnp.mean(x, axis=-1, keepdims=True)
    var = jnp.mean(jnp.square(x - mu), axis=-1, keepdims=True)
    return (x - mu) * lax.rsqrt(var + LN_EPS) * g + b


def _first_lane_of_max(vals, lane_f):
    vmax = jnp.max(vals, axis=1, keepdims=True)
    idx = jnp.min(jnp.where(vals == vmax, lane_f, float(LANES)), axis=1, keepdims=True)
    return vmax, idx


def _outproj_kernel(xp_ref, xs_ref, rgp_ref, rgs_ref, mp_ref, ms_ref, wo_ref, g1_ref, b1_ref,
                    wr_ref, br_ref, x1_ref, cnt_ref, *, n_p):
    i = pl.program_id(0)

    @pl.when(i == 0)
    def _():
        cnt_ref[...] = jnp.zeros_like(cnt_ref)

    def route(lg, lane, lane_f):
        neg = -jnp.inf
        gl = jnp.where(lane < N_GROUPS, lg, neg)
        gmax, gidx = _first_lane_of_max(gl, lane_f)
        p_g = 1.0 / jnp.sum(jnp.exp(gl - gmax), axis=1, keepdims=True)
        e_lo = float(N_GROUPS) + float(EXPERTS_PER_GROUP) * gidx
        el = jnp.where((lane_f >= e_lo) & (lane_f < e_lo + float(EXPERTS_PER_GROUP)), lg, neg)
        v1, i1 = _first_lane_of_max(el, lane_f)
        v2, i2 = _first_lane_of_max(jnp.where(lane_f == i1, neg, el), lane_f)
        d = jnp.exp(v2 - v1)
        w1 = p_g / (1.0 + d)
        w2 = p_g * d / (1.0 + d)
        first_low = i1 < i2
        j_lo = jnp.minimum(i1, i2) - e_lo
        j_hi = jnp.maximum(i1, i2) - e_lo
        n_first = float(EXPERTS_PER_GROUP - 1) * j_lo - 0.5 * j_lo * (j_lo - 1.0)
        cls = float(PAIRS_PER_GROUP) * gidx + n_first + (j_hi - j_lo - 1.0)
        return cls, jnp.where(first_low, w1, w2), jnp.where(first_low, w2, w1)

    def run(x_ref, rg_ref, m_ref):
        rp = TM_OUT // OUT_PARTS
        parts = [slice(p * rp, (p + 1) * rp) for p in range(OUT_PARTS)]
        lane = lax.broadcasted_iota(jnp.int32, (rp, LANES), 1)
        lane_f = lane.astype(f32)
        heads = [jnp.concatenate([rg_ref[r, :].astype(bf16), m_ref[r, :].astype(bf16)], axis=1) for r in parts]
        mix = [jnp.dot(h, wo_ref[...], preferred_element_type=f32) for h in heads]
        x1 = [_layer_norm(ALPHA * x_ref[r, :] + mx, g1_ref[...], b1_ref[...]) for r, mx in zip(parts, mix)]
        for r, v in zip(parts, x1):
            x1_ref[r, 0:D_MODEL] = v

        hi = [v.astype(bf16) for v in x1]
        lo = [(v - h.astype(f32)).astype(bf16) for v, h in zip(x1, hi)]
        hi_terms = [jnp.dot(h, wr_ref[...], preferred_element_type=f32) for h in hi]
        lo_term = [jnp.dot(v, wr_ref[:, 0:LANES], preferred_element_type=f32) for v in lo]
        lg = [ht[:, 0:LANES] + ht[:, LANES:2 * LANES] + lt + br_ref[...] for ht, lt in zip(hi_terms, lo_term)]
        routed = [route(v, lane, lane_f) for v in lg]

        ti = lax.broadcasted_iota(jnp.int32, (rp, rp), 0)
        si = lax.broadcasted_iota(jnp.int32, (rp, rp), 1)
        tri = (si <= ti).astype(bf16)
        hot = [lane_f == cls for cls, _, _ in routed]
        cum = [jnp.dot(tri, h.astype(bf16), preferred_element_type=f32) for h in hot]
        seen = cnt_ref[...]
        for r, h, cm, (cls, w_lo, w_hi) in zip(parts, hot, cum, routed):
            rank = jnp.sum(jnp.where(h, cm - 1.0 + seen, 0.0), axis=1, keepdims=True)
            seen = seen + cm[rp - 1:rp, :]
            info = jnp.zeros((rp, LANES), f32)
            for c, val in enumerate((cls, rank, w_lo, w_hi)):
                info = jnp.where(lane == c, val, info)
            x1_ref[r, D_MODEL:D_MODEL + LANES] = info
        cnt_ref[...] = seen

    @pl.when(i < n_p)
    def _():
        run(xp_ref, rgp_ref, mp_ref)

    @pl.when(i >= n_p)
    def _():
        run(xs_ref, rgs_ref, ms_ref)


def _outproj(xp, xs, rgp, rgs, mp, ms, wo, g1, b1, wr, br):
    n_p, n_s = xp.shape[0] // TM_OUT, xs.shape[0] // TM_OUT
    n = n_p + n_s
    pmap = lambda i: (jnp.minimum(i, n_p - 1), 0)
    smap = lambda i: (jnp.maximum(i - n_p, 0), 0)
    const = lambda i: (0, 0)
    return pl.pallas_call(
        functools.partial(_outproj_kernel, n_p=n_p),
        grid=(n,),
        in_specs=[
            pl.BlockSpec((TM_OUT, D_MODEL), pmap), pl.BlockSpec((TM_OUT, D_MODEL), smap),
            pl.BlockSpec((TM_OUT, D_RG), pmap), pl.BlockSpec((TM_OUT, D_RG), smap),
            pl.BlockSpec((TM_OUT, D_M), pmap), pl.BlockSpec((TM_OUT, D_M), smap),
            pl.BlockSpec((D_RG + D_M, D_MODEL), const),
            pl.BlockSpec((1, D_MODEL), const), pl.BlockSpec((1, D_MODEL), const),
            pl.BlockSpec((D_MODEL, 2 * LANES), const),
            pl.BlockSpec((1, LANES), const),
        ],
        out_specs=[
            pl.BlockSpec((TM_OUT, X_ROW), lambda i: (i, 0)),
            pl.BlockSpec((1, LANES), const),
        ],
        out_shape=[
            jax.ShapeDtypeStruct((n * TM_OUT, X_ROW), f32),
            jax.ShapeDtypeStruct((1, LANES), f32),
        ],
        compiler_params=_cparams(1),
        name="outproj",
    )(xp, xs, rgp, rgs, mp, ms, wo, g1, b1, wr, br)


GRP_MOVE = TM_MOVE // SUBLANES


def _grouped(x):
    return x.reshape(x.shape[0] // SUBLANES, SUBLANES, x.shape[1])


def _tile_rows_wait(hbm_ref, sem):
    rows = hbm_ref.at[pl.ds(0, TM_MOVE)]
    pltpu.make_async_copy(rows, rows, sem).wait()


def _for_each_tile_row(body):
    def group(k, c):
        for j in range(SUBLANES):
            body(k, j, k * SUBLANES + j)
        return c

    lax.fori_loop(0, GRP_MOVE, group, 0)


def _dispatch_kernel(pad_ref, pos_ref, x1_ref, xs_ref, zero_scr, xbuf, sem, zsem, lsem):
    i = pl.program_id(0)
    n = pl.num_programs(0)

    def zero_tile(t):
        return pltpu.make_async_copy(zero_scr, xs_ref.at[pl.ds(pl.multiple_of(t * TM_MOE, TM_MOE), TM_MOE)], zsem)

    def zero_chunk(row0):
        return pltpu.make_async_copy(zero_scr.at[pl.ds(0, SUBLANES)],
                                     xs_ref.at[pl.ds(pl.multiple_of(row0, SUBLANES), SUBLANES)], zsem)

    @pl.when(i == 0)
    def _():
        zero_scr[...] = jnp.zeros_like(zero_scr)
        n_used = pad_ref[2 * N_CLASSES]
        n_tiles = xs_ref.shape[0] // TM_MOE
        for go in (lambda cp: cp.start(), lambda cp: cp.wait()):
            lax.fori_loop(n_used, n_tiles, lambda t, c: (go(zero_tile(t)), c)[1], 0)

            def per_class(cl, c):
                row0 = pad_ref[2 * cl]
                lax.fori_loop(0, pad_ref[2 * cl + 1],
                              lambda k, c2: (go(zero_chunk(row0 + k * SUBLANES)), c2)[1], 0)
                return c

            lax.fori_loop(0, N_CLASSES, per_class, 0)

    def load(t, s):
        return pltpu.make_async_copy(x1_ref.at[pl.ds(t * GRP_MOVE, GRP_MOVE)], xbuf.at[s], lsem.at[s])

    @pl.when(i == 0)
    def _():
        load(0, 0).start()

    @pl.when(i + 1 < n)
    def _():
        load(i + 1, (i + 1) % 3).start()

    slot = i % 3
    load(i, slot).wait()

    def start(k, j, r):
        pltpu.make_async_copy(xbuf.at[slot, k, pl.ds(j, 1)], xs_ref.at[pl.ds(pos_ref[0, 0, r], 1)],
                              sem.at[slot]).start()

    _for_each_tile_row(start)

    @pl.when(i > 0)
    def _():
        _tile_rows_wait(xs_ref, sem.at[(i + 2) % 3])

    @pl.when(i == n - 1)
    def _():
        _tile_rows_wait(xs_ref, sem.at[slot])


def _dispatch(pad_plan, pos, x1, n_rows):
    n = x1.shape[0] // TM_MOVE
    return pl.pallas_call(
        _dispatch_kernel,
        grid_spec=pltpu.PrefetchScalarGridSpec(
            num_scalar_prefetch=1,
            grid=(n,),
            in_specs=[
                pl.BlockSpec((1, 1, TM_MOVE), lambda i, *_: (i, 0, 0), memory_space=pltpu.SMEM),
                pl.BlockSpec(memory_space=pl.ANY),
            ],
            out_specs=pl.BlockSpec(memory_space=pl.ANY),
            scratch_shapes=[pltpu.VMEM((TM_MOE, X_ROW), f32),
                            pltpu.VMEM((3, GRP_MOVE, SUBLANES, X_ROW), f32),
                            pltpu.SemaphoreType.DMA((3,)), pltpu.SemaphoreType.DMA(()),
                            pltpu.SemaphoreType.DMA((3,))],
        ),
        out_shape=jax.ShapeDtypeStruct((n_rows, X_ROW), f32),
        compiler_params=_cparams(1),
        name="dispatch",
    )(pad_plan, pos, _grouped(x1))


def _expert_kernel(ta_ref, tb_ref, ca_ref, cb_ref, sa_ref, sb_ref, nu_ref,
                   x_ref, wg_hbm, wu_hbm, wd_hbm, g2_ref, b2_ref, y_ref,
                   wga, wua, wda, wgb, wub, wdb, sem_a, sem_b):
    i = pl.program_id(0)
    n_used = nu_ref[0]
    first = i * MOE_TILES
    side_a = (ta_ref, ca_ref, sa_ref, (wga, wua, wda), sem_a)
    side_b = (tb_ref, cb_ref, sb_ref, (wgb, wub, wdb), sem_b)

    def copies(side, j):
        t_ref, _, s_ref, bufs, sem = side
        e, s = t_ref[j], s_ref[j]
        return [pltpu.make_async_copy(w.at[e], buf.at[s], sem.at[s])
                for w, buf in zip((wg_hbm, wu_hbm, wd_hbm), bufs)]

    def fetch(j):
        @pl.when(j < n_used)
        def _():
            for side in (side_a, side_b):
                @pl.when(side[1][j] == 1)
                def _():
                    for cp in copies(side, j):
                        cp.start()

    @pl.when(i == 0)
    def _():
        for j in range(W_AHEAD):
            fetch(j)

    for t in range(MOE_TILES):
        fetch(first + W_AHEAD + t)

    @pl.when(first < n_used)
    def _():
        for t in range(MOE_TILES):
            for side in (side_a, side_b):
                @pl.when(side[1][first + t] == 1)
                def _():
                    for cp in copies(side, first + t):
                        cp.wait()
        rows = [slice(t * TM_MOE, (t + 1) * TM_MOE) for t in range(MOE_TILES)]
        x = [x_ref[r, 0:D_MODEL] for r in rows]
        info = [x_ref[r, D_MODEL:X_ROW] for r in rows]
        xb = [v.astype(bf16) for v in x]
        units = [(t, bufs, s_ref[first + t]) for t in range(MOE_TILES)
                 for bufs, s_ref in (((wga, wua, wda), sa_ref), ((wgb, wub, wdb), sb_ref))]
        hg = [jnp.dot(xb[t], bufs[0][s], preferred_element_type=f32) for t, bufs, s in units]
        hu = [jnp.dot(xb[t], bufs[1][s], preferred_element_type=f32) for t, bufs, s in units]
        mid = [(g * jax.nn.sigmoid(g) * u).astype(bf16) for g, u in zip(hg, hu)]
        ys = [jnp.dot(m, bufs[2][s], preferred_element_type=f32)
              for m, (t, bufs, s) in zip(mid, units)]
        for t in range(MOE_TILES):
            ffn = info[t][:, 2:3] * ys[2 * t] + info[t][:, 3:4] * ys[2 * t + 1]
            y_ref[rows[t], :] = _layer_norm(ALPHA * x[t] + ffn, g2_ref[...], b2_ref[...])

    @pl.when(first >= n_used)
    def _():
        y_ref[...] = jnp.zeros_like(y_ref)


def _experts(tile_ea, tile_eb, n_used, xs, wg, wu, wd, g2, b2):
    nt = xs.shape[0] // TM_MOE

    def ring_plan(tile_e):
        opens = jnp.concatenate([jnp.ones((1,), jnp.int32), (tile_e[1:] != tile_e[:-1]).astype(jnp.int32)])
        return opens, (jnp.cumsum(opens) - 1) % W_SLOTS

    open_a, slot_a = ring_plan(tile_ea)
    open_b, slot_b = ring_plan(tile_eb)
    assert nt % MOE_TILES == 0
    n_used = ((n_used + MOE_TILES - 1) // MOE_TILES) * MOE_TILES
    const = lambda i, *_: (0, 0)
    xmap = lambda i, ta, tb, ca, cb, sa, sb, nu: (jnp.minimum(i, nu[0] // MOE_TILES - 1), 0)
    w_in = pltpu.VMEM((W_SLOTS, D_MODEL, D_EXPERT), bf16)
    w_out = pltpu.VMEM((W_SLOTS, D_EXPERT, D_MODEL), bf16)
    return pl.pallas_call(
        _expert_kernel,
        grid_spec=pltpu.PrefetchScalarGridSpec(
            num_scalar_prefetch=7,
            grid=(nt // MOE_TILES,),
            in_specs=[
                pl.BlockSpec((MOE_TILES * TM_MOE, X_ROW), xmap),
                pl.BlockSpec(memory_space=pl.ANY), pl.BlockSpec(memory_space=pl.ANY),
                pl.BlockSpec(memory_space=pl.ANY),
                pl.BlockSpec((1, D_MODEL), const), pl.BlockSpec((1, D_MODEL), const),
            ],
            out_specs=pl.BlockSpec((MOE_TILES * TM_MOE, D_MODEL), lambda i, *_: (i, 0)),
            scratch_shapes=[w_in, w_in, w_out, w_in, w_in, w_out,
                            pltpu.SemaphoreType.DMA((W_SLOTS,)), pltpu.SemaphoreType.DMA((W_SLOTS,))],
        ),
        out_shape=jax.ShapeDtypeStruct((nt * TM_MOE, D_MODEL), f32),
        compiler_params=_cparams(1),
        name="experts",
    )(tile_ea, tile_eb, open_a, open_b, slot_a.astype(jnp.int32), slot_b.astype(jnp.int32), n_used,
      xs, wg, wu, wd, g2, b2)


def _collect_kernel(pos_ref, posn_ref, ys_ref, op_ref, os_ref, ybuf, sem, *, n_p):
    i = pl.program_id(0)
    n = pl.num_programs(0)
    slot = i % 2

    def gather(p_ref, s):
        def start(k, j, r):
            pltpu.make_async_copy(ys_ref.at[pl.ds(p_ref[0, 0, r], 1)], ybuf.at[s, k, pl.ds(j, 1)],
                                  sem.at[s]).start()

        _for_each_tile_row(start)

    @pl.when(i == 0)
    def _():
        gather(pos_ref, 0)

    @pl.when(i + 1 < n)
    def _():
        gather(posn_ref, 1 - slot)

    _tile_rows_wait(ys_ref, sem.at[slot])

    @pl.when(i < n_p)
    def _():
        op_ref[...] = ybuf[slot].reshape(TM_MOVE, D_MODEL)

    @pl.when(i >= n_p)
    def _():
        os_ref[...] = ybuf[slot].reshape(TM_MOVE, D_MODEL)


def _collect(pos, ys, n_p):
    n = pos.shape[0]
    n_s = n - n_p
    return pl.pallas_call(
        functools.partial(_collect_kernel, n_p=n_p),
        grid=(n,),
        in_specs=[
            pl.BlockSpec((1, 1, TM_MOVE), lambda i: (i, 0, 0), memory_space=pltpu.SMEM),
            pl.BlockSpec((1, 1, TM_MOVE), lambda i: (jnp.minimum(i + 1, n - 1), 0, 0),
                         memory_space=pltpu.SMEM),
            pl.BlockSpec(memory_space=pl.ANY),
        ],
        out_specs=[
            pl.BlockSpec((TM_MOVE, D_MODEL), lambda i: (jnp.minimum(i, n_p - 1), 0)),
            pl.BlockSpec((TM_MOVE, D_MODEL), lambda i: (jnp.maximum(i - n_p, 0), 0)),
        ],
        out_shape=[
            jax.ShapeDtypeStruct((n_p * TM_MOVE, D_MODEL), f32),
            jax.ShapeDtypeStruct((n_s * TM_MOVE, D_MODEL), f32),
        ],
        scratch_shapes=[pltpu.VMEM((2, GRP_MOVE, SUBLANES, D_MODEL), f32), pltpu.SemaphoreType.DMA((2,))],
        compiler_params=_cparams(1),
        name="collect",
    )(pos, pos, ys)


def _block_diag(w):
    n, d, _ = w.shape
    eye = jnp.eye(n, dtype=w.dtype)
    return (eye[:, None, :, None] * w[:, :, None, :]).reshape(n * d, n * d)


def _pair_walk(n):
    total = n * (n - 1) // 2

    def extend(path, used):
        if len(path) == total:
            return path
        a, b = path[-1]
        for c in range(n):
            for nxt in ((a, c), (c, b)):
                key = frozenset(nxt)
                if len(key) == 2 and key not in used:
                    out = extend(path + [nxt], used | {key})
                    if out:
                        return out
        return None

    return extend([(0, 1)], {frozenset((0, 1))})


def _pad_history(buf):
    return jnp.pad(buf, ((0, 0), (SUBLANES - (CONV_W - 1), 0), (0, 0)))


def kernel(x_prompt, x_sample, state_rg_h, state_rg_conv, state_m_C, state_m_n, state_m_m, state_m_conv, w_in, rg_conv_w, rg_conv_b, rg_w_a, rg_b_a, rg_w_x, rg_b_x, rg_lambda, m_conv_w, m_conv_b, m_w_q, m_w_k, m_w_v, m_b_i, m_b_f, m_norm_g, m_skip, w_out, ln1_g, ln1_b, ln2_g, ln2_b, moe_w_group, moe_b_group, moe_w_expert, moe_b_expert, moe_w_gate, moe_w_up, moe_w_down):
    BP, TP, _ = x_prompt.shape
    BS, TS, _ = x_sample.shape
    n_prompt, n_sample = BP * TP, BS * TS
    xp = x_prompt.reshape(n_prompt, D_MODEL)
    xs = x_sample.reshape(n_sample, D_MODEL)
    l = 0

    w_in_p = jnp.pad(w_in[l], ((0, 0), (0, N_PROJ - w_in.shape[-1]))).astype(bf16)
    wa = _block_diag(rg_w_a[l]).astype(bf16)
    wx = _block_diag(rg_w_x[l]).astype(bf16)
    row = lambda v: v.reshape(1, -1)
    bif = jnp.pad(jnp.concatenate([m_b_i[l], m_b_f[l]]), (0, LANES - 2 * M_HEADS)).reshape(1, LANES)
    wqk = jnp.concatenate([m_w_q[l], m_w_k[l]], axis=-1).astype(bf16)
    wv = m_w_v[l].astype(bf16)
    wkT = m_w_k[l].transpose(0, 2, 1).astype(bf16)
    w_route = jnp.pad(jnp.concatenate([moe_w_group[l], moe_w_expert[l]], axis=1),
                      ((0, 0), (0, LANES - N_GROUPS - N_EXPERTS)))
    wr_hi = w_route.astype(bf16)
    wr_lo = (w_route - wr_hi.astype(f32)).astype(bf16)
    wr = jnp.concatenate([wr_hi, wr_lo], axis=1)
    b_route = jnp.pad(jnp.concatenate([moe_b_group[l], moe_b_expert[l]]),
                      (0, LANES - N_GROUPS - N_EXPERTS)).reshape(1, LANES)

    proj_p, proj_s = _inproj(xp, xs, w_in_p)

    rg_args = (rg_conv_w[l], row(rg_conv_b[l]), wa, row(rg_b_a[l]), wx, row(rg_b_x[l]), row(rg_lambda[l]))
    s_blk = M_ROWS // TS
    yrg_p, p_rg_h, p_rg_conv = _rglru(proj_p, 0, BP, TP, 1, RG_ROWS, None, *rg_args)
    rg_state = (_pad_history(state_rg_conv[l]), state_rg_h[l].reshape(BS, 1, D_RG))
    yrg_s, s_rg_h, s_rg_conv = _rglru(proj_s, 0, BS, TS, RG_ROWS // TS, TS, rg_state, *rg_args)

    m_args = (m_conv_w[l], row(m_conv_b[l]), wqk, wv, wkT, bif, row(m_norm_g[l]), row(m_skip[l]))
    ym_p, p_m_C, p_m_n, p_m_m, p_m_conv, wg_b, wu_b, wd_b = _mlstm(
        proj_p, BP, TP, 1, M_ROWS, M_BLOCKS_PROMPT, None, *m_args,
        riders=(moe_w_gate[l], moe_w_up[l], moe_w_down[l]))
    m0_s = jnp.pad(jnp.repeat(state_m_m[l].T, TS, axis=1), ((0, SUBLANES - M_HEADS), (0, 0)))
    m_state = (_pad_history(state_m_conv[l]), state_m_C[l], state_m_n[l].reshape(BS, M_HEADS, 1, M_DH), m0_s)
    ym_s, s_m_C, s_m_n, s_m_m, s_m_conv = _mlstm(proj_s, BS, TS, s_blk, TS, M_BLOCKS_SAMPLE, m_state, *m_args)

    x1, cnt = _outproj(xp, xs, yrg_p, yrg_s, ym_p, ym_s, w_out[l].astype(bf16),
                       row(ln1_g[l]), row(ln1_b[l]), wr, b_route)

    n_tok = n_prompt + n_sample
    n_tiles = n_tok // TM_MOE + N_CLASSES
    lay_cls, lay_a, lay_b = [], [], []
    for g in range(N_GROUPS):
        for ja, jb in _pair_walk(EXPERTS_PER_GROUP):
            assert ja < jb
            n_first = (EXPERTS_PER_GROUP - 1) * ja - ja * (ja - 1) // 2
            lay_cls.append(PAIRS_PER_GROUP * g + n_first + jb - ja - 1)
            lay_a.append(g * EXPERTS_PER_GROUP + ja)
            lay_b.append(g * EXPERTS_PER_GROUP + jb)
    classes = jnp.arange(N_CLASSES, dtype=jnp.int32)
    in_slot = jnp.array(lay_cls, jnp.int32)[:, None] == classes
    counts = jnp.sum(jnp.where(in_slot, cnt[0, :N_CLASSES].astype(jnp.int32), 0), axis=1)
    padded = ((counts + TM_MOE - 1) // TM_MOE) * TM_MOE
    ends = jnp.cumsum(padded)
    offs = ends - padded
    offs_of_cls = jnp.sum(jnp.where(in_slot, offs[:, None], 0), axis=0)
    cls = x1[:, D_MODEL].astype(jnp.int32)
    rank = x1[:, D_MODEL + 1].astype(jnp.int32)
    pos = jnp.sum(jnp.where(cls[:, None] == classes, offs_of_cls, 0), axis=-1) + rank
    pos = pos.reshape(n_tok // TM_MOVE, 1, TM_MOVE)
    tiles = jnp.arange(n_tiles, dtype=jnp.int32)
    tile_slot = jnp.minimum(jnp.sum(tiles[:, None] >= (ends // TM_MOE)[None, :], axis=1), N_CLASSES - 1)
    on_slot = tile_slot[:, None] == classes
    tile_ea = jnp.sum(jnp.where(on_slot, jnp.array(lay_a, jnp.int32), 0), axis=-1).astype(jnp.int32)
    tile_eb = jnp.sum(jnp.where(on_slot, jnp.array(lay_b, jnp.int32), 0), axis=-1).astype(jnp.int32)
    n_used = (ends[-1] // TM_MOE).reshape(1).astype(jnp.int32)
    pad_row0 = ((offs + counts) // SUBLANES) * SUBLANES
    pad_plan = jnp.concatenate([jnp.stack([pad_row0, (ends - pad_row0) // SUBLANES], axis=1).reshape(-1),
                                     n_used]).astype(jnp.int32)

    x_sorted = _dispatch(pad_plan, pos, x1, n_tiles * TM_MOE)
    y_sorted = _experts(tile_ea, tile_eb, n_used, x_sorted, wg_b, wu_b, wd_b,
                        row(ln2_g[l]), row(ln2_b[l]))
    y_p, y_s = _collect(pos, y_sorted, n_prompt // TM_MOVE)

    return (y_p.reshape(BP, TP, D_MODEL), y_s.reshape(BS, TS, D_MODEL),
            p_rg_h.reshape(1, BP, D_RG), p_rg_conv[None], p_m_C[None], p_m_n.reshape(1, BP, M_HEADS, M_DH),
            p_m_m[None], p_m_conv[None],
            s_rg_h.reshape(1, BS, D_RG), s_rg_conv[None], s_m_C[None], s_m_n.reshape(1, BS, M_HEADS, M_DH),
            s_m_m[None], s_m_conv[None])
```

```python
import functools

import jax
import jax.numpy as jnp
from jax import lax
from jax.experimental import pallas as pl
from jax.experimental.pallas import tpu as pltpu

f32 = jnp.float32
bf16 = jnp.bfloat16

D_MODEL = 1024
D_RG = 512
RG_C = 8.0
D_M = 512
M_HEADS = 4
M_DH = 128
CONV_W = 4
N_GROUPS = 4
EXPERTS_PER_GROUP = 8
N_EXPERTS = 32
D_EXPERT = 256
ALPHA = 2.0 ** 0.25
LN_EPS = 1e-5
M_INIT = -1.0e4

LANES = 128
SUBLANES = 8
TM = 512
TM_MOVE = 1024
TM_OUT = 1024
OUT_PARTS = 8
TM_MOE = 128
MOE_TILES = 4
W_AHEAD = 8
W_SLOTS = W_AHEAD + MOE_TILES
PAIRS_PER_GROUP = EXPERTS_PER_GROUP * (EXPERTS_PER_GROUP - 1) // 2
N_CLASSES = N_GROUPS * PAIRS_PER_GROUP
X_ROW = D_MODEL + LANES
N_PROJ = 2 * D_RG + 2 * D_M + LANES
RG_ROWS = 1024
M_ROWS = 128
M_BLOCKS_PROMPT = 8
M_BLOCKS_SAMPLE = 2
V7X_VMEM_BYTES = 64 * 1024 * 1024
VMEM_LIMIT = V7X_VMEM_BYTES - 8 * 1024 * 1024


def _cparams(n_axes):
    return pltpu.CompilerParams(dimension_semantics=("arbitrary",) * n_axes,
                                vmem_limit_bytes=VMEM_LIMIT)


def _inproj_kernel(xp_ref, xs_ref, w_ref, op_ref, os_ref, *, n_p):
    i = pl.program_id(0)

    def run(x_ref, o_ref):
        o_ref[...] = jnp.dot(x_ref[...].astype(bf16), w_ref[...], preferred_element_type=f32)

    @pl.when(i < n_p)
    def _():
        run(xp_ref, op_ref)

    @pl.when(i >= n_p)
    def _():
        run(xs_ref, os_ref)


def _inproj(xp, xs, w):
    n_p, n_s = xp.shape[0] // TM, xs.shape[0] // TM
    pmap = lambda i: (jnp.minimum(i, n_p - 1), 0)
    smap = lambda i: (jnp.maximum(i - n_p, 0), 0)
    return pl.pallas_call(
        functools.partial(_inproj_kernel, n_p=n_p),
        grid=(n_p + n_s,),
        in_specs=[
            pl.BlockSpec((TM, D_MODEL), pmap),
            pl.BlockSpec((TM, D_MODEL), smap),
            pl.BlockSpec((D_MODEL, N_PROJ), lambda i: (0, 0)),
        ],
        out_specs=[pl.BlockSpec((TM, N_PROJ), pmap), pl.BlockSpec((TM, N_PROJ), smap)],
        out_shape=[jax.ShapeDtypeStruct((n_p * TM, N_PROJ), f32),
                   jax.ShapeDtypeStruct((n_s * TM, N_PROJ), f32)],
        compiler_params=_cparams(1),
        name="inproj",
    )(xp, xs, w)


def _causal_conv(tail_scr, x, cw_ref, cb_ref, S, L, C):
    tail = tail_scr[...]
    sub = lax.broadcasted_iota(jnp.int32, (S, SUBLANES, C), 1)
    acc = cb_ref[...] + cw_ref[CONV_W - 1:CONV_W, :] * x
    for d in range(1, CONV_W):
        back = pltpu.roll(x, d, 0).reshape(S, L, C)
        head = jnp.where(sub < d, pltpu.roll(tail, d, 1), back[:, 0:SUBLANES, :])
        if L > SUBLANES:
            back = jnp.concatenate([head, back[:, SUBLANES:, :]], axis=1)
        else:
            back = head
        acc = acc + cw_ref[CONV_W - 1 - d:CONV_W - d, :] * back.reshape(S * L, C)
    tail_scr[...] = x.reshape(S, L, C)[:, L - SUBLANES:, :]
    return acc


def _softplus(x):
    return jnp.maximum(x, 0.0) + jnp.log1p(jnp.exp(-jnp.abs(x)))


def _rglru_kernel(*refs, S, L, fresh):
    x_ref, g_ref = refs[:2]
    buf_ref, h0_ref = (None, None) if fresh else refs[2:4]
    (cw_ref, cb_ref, wa_ref, ba_ref, wx_ref, bx_ref, lam_ref,
     y_ref, hN_ref, cN_ref, xs_scr) = refs[2 if fresh else 4:]
    R = S * L
    t = pl.program_id(1)

    @pl.when(t == 0)
    def _():
        xs_scr[...] = jnp.zeros_like(xs_scr)
        if not fresh:
            xs_scr[:, SUBLANES - (CONV_W - 1):SUBLANES, :] = buf_ref[...]
        hN_ref[...] = jnp.zeros_like(hN_ref) if fresh else h0_ref[...]

    x = x_ref[...]
    xc = _causal_conv(xs_scr, x, cw_ref, cb_ref, S, L, D_RG)
    xcb = xc.astype(bf16)
    r = jax.nn.sigmoid(jnp.dot(xcb, wa_ref[...], preferred_element_type=f32) + ba_ref[...])
    ig = jax.nn.sigmoid(jnp.dot(xcb, wx_ref[...], preferred_element_type=f32) + bx_ref[...])
    log_a = (-RG_C) * r * _softplus(-lam_ref[...])
    a = jnp.exp(log_a)
    th = jnp.tanh(log_a)
    u = jnp.sqrt(-2.0 * th / (1.0 - th)) * ig * xc

    n_grp, grp_per_seq = R // SUBLANES, L // SUBLANES
    a3 = a.reshape(n_grp, SUBLANES, D_RG)
    u3 = u.reshape(n_grp, SUBLANES, D_RG)
    sub = lax.broadcasted_iota(jnp.int32, (n_grp, SUBLANES, D_RG), 1)
    s = 1
    while s < SUBLANES:
        ok = sub >= s
        a_sh = pltpu.roll(a3, s, 1)
        u_sh = pltpu.roll(u3, s, 1)
        u3 = jnp.where(ok, a3 * u_sh + u3, u3)
        a3 = jnp.where(ok, a3 * a_sh, a3)
        s *= 2
    h0 = hN_ref[...]
    groups = []
    for kg in range(n_grp):
        carry = h0[kg // grp_per_seq] if kg % grp_per_seq == 0 else groups[-1][SUBLANES - 1:SUBLANES, :]
        groups.append(a3[kg] * carry + u3[kg])
    h = jnp.concatenate(groups, axis=0)

    y_ref[...] = h * jax.nn.gelu(g_ref[...], approximate=True)
    hN_ref[...] = h.reshape(S, L, D_RG)[:, L - 1:L, :]
    cN_ref[...] = xs_scr[:, SUBLANES - (CONV_W - 1):SUBLANES, :]


def _rglru(proj, row0, B, T, S, L, state, cw, cb, wa, ba, wx, bx, lam):
    R = S * L
    nb, nt = B // S, T // L
    blk0 = row0 // R
    row_map = lambda b, t: (blk0 + b * nt + t, 0)
    const2 = lambda b, t: (0, 0)
    state_specs = [] if state is None else [pl.BlockSpec((S, CONV_W - 1, D_RG), lambda b, t: (b, 0, 0)),
                                            pl.BlockSpec((S, 1, D_RG), lambda b, t: (b, 0, 0))]
    return pl.pallas_call(
        functools.partial(_rglru_kernel, S=S, L=L, fresh=state is None),
        grid=(nb, nt),
        in_specs=[
            pl.BlockSpec((R, D_RG), row_map),
            pl.BlockSpec((R, D_RG), lambda b, t: (blk0 + b * nt + t, 1)),
        ] + state_specs + [
            pl.BlockSpec((CONV_W, D_RG), const2),
            pl.BlockSpec((1, D_RG), const2),
            pl.BlockSpec((D_RG, D_RG), const2),
            pl.BlockSpec((1, D_RG), const2),
            pl.BlockSpec((D_RG, D_RG), const2),
            pl.BlockSpec((1, D_RG), const2),
            pl.BlockSpec((1, D_RG), const2),
        ],
        out_specs=[
            pl.BlockSpec((R, D_RG), lambda b, t: (b * nt + t, 0)),
            pl.BlockSpec((S, 1, D_RG), lambda b, t: (b, 0, 0)),
            pl.BlockSpec((S, CONV_W - 1, D_RG), lambda b, t: (b, 0, 0)),
        ],
        out_shape=[
            jax.ShapeDtypeStruct((B * T, D_RG), f32),
            jax.ShapeDtypeStruct((B, 1, D_RG), f32),
            jax.ShapeDtypeStruct((B, CONV_W - 1, D_RG), f32),
        ],
        scratch_shapes=[pltpu.VMEM((S, SUBLANES, D_RG), f32)],
        compiler_params=_cparams(2),
        name="rglru",
    )(proj, proj, *(state or ()), cw, cb, wa, ba, wx, bx, lam)


def _seg_scan(x, op, fill, tin, L, reverse=False):
    s = 1
    while s < L:
        if reverse:
            sh = pltpu.roll(x, LANES - s, 1)
            ok = tin < L - s
        else:
            sh = pltpu.roll(x, s, 1)
            ok = tin >= s
        x = op(x, jnp.where(ok, sh, fill))
        s *= 2
    return x


def _mlstm_kernel(*refs, S, L, G, n_riders, fresh):
    x_ref, z_ref, gt_ref = refs[:3]
    n_state = 0 if fresh else 4
    state_refs = refs[3:3 + n_state]
    n_in = 11 + n_state
    cw_ref, cb_ref, wqk_ref, wv_ref, wkT_ref, bif_ref, ng_ref, sk_ref = refs[3 + n_state:n_in]
    rider_in = refs[n_in:n_in + n_riders]
    y_ref, cN_ref, nN_ref, mN_ref, bN_ref = refs[n_in + n_riders:n_in + n_riders + 5]
    rider_out = refs[n_in + n_riders + 5:n_in + 2 * n_riders + 5]
    xs_scr, m_scr = refs[n_in + 2 * n_riders + 5:]
    for src, dst in zip(rider_in, rider_out):
        dst[...] = src[...].astype(bf16)

    R = S * L
    GS, GR = G * S, G * R
    j = pl.program_id(1)

    @pl.when(j == 0)
    def _():
        if fresh:
            xs_scr[...] = jnp.zeros_like(xs_scr)
            cN_ref[...] = jnp.zeros_like(cN_ref)
            nN_ref[...] = jnp.zeros_like(nN_ref)
            m_scr[...] = jnp.full(m_scr.shape, M_INIT, f32)
        else:
            buf_ref, c0_ref, n0_ref, m0_ref = state_refs
            xs_scr[...] = jnp.zeros_like(xs_scr)
            xs_scr[:, SUBLANES - (CONV_W - 1):SUBLANES, :] = buf_ref[...]
            cN_ref[...] = c0_ref[...]
            nN_ref[...] = n0_ref[...]
            m_scr[...] = m0_ref[...]

    x = x_ref[...].reshape(GR, D_M)
    xc = _causal_conv(xs_scr, x, cw_ref, cb_ref, GS, L, D_M)
    xa = xc * jax.nn.sigmoid(xc)
    xab = xa.astype(bf16)
    xb = x.astype(bf16)

    il_parts, f_parts = [], []
    for g in range(G):
        gT = (gt_ref[g] + bif_ref[...]).T
        il_parts.append(gT[0:SUBLANES, :])
        f_parts.append(pltpu.roll(gT[0:SUBLANES, :], M_HEADS, 0))
    il = jnp.concatenate(il_parts, axis=0)
    fl = -_softplus(-jnp.concatenate(f_parts, axis=0))
    tin = lax.broadcasted_iota(jnp.int32, (G * SUBLANES, LANES), 1) % L
    bcum = _seg_scan(fl, jnp.add, 0.0, tin, L)
    a = il - bcum
    m_prev = m_scr[...].reshape(G * SUBLANES, LANES)
    big_m = jnp.maximum(m_prev, _seg_scan(a, jnp.maximum, -jnp.inf, tin, L))
    m_t = bcum + big_m
    if S == 1:
        m_last = jnp.broadcast_to(big_m[:, LANES - 1:LANES], big_m.shape)
    else:
        m_last = _seg_scan(big_m, jnp.maximum, -jnp.inf, tin, L, reverse=True)
    rows = [big_m, jnp.exp(m_prev - big_m), jnp.exp(-m_t), jnp.exp(a - m_last), jnp.exp(m_prev - m_last)]
    mN_ref[...] = m_t.reshape(G, SUBLANES, LANES)
    m_scr[...] = jnp.broadcast_to(m_t[:, LANES - 1:LANES], m_t.shape).reshape(G, SUBLANES, LANES)
    pad_rows = jnp.zeros((LANES - len(rows) * SUBLANES, LANES), f32)
    cols = []
    for g in range(G):
        gs = slice(g * SUBLANES, (g + 1) * SUBLANES)
        cols.append(jnp.concatenate([r[gs] for r in rows] + [pad_rows], axis=0).T)

    def col(g, q, h):
        return cols[g][:, SUBLANES * q + h:SUBLANES * q + h + 1]

    qs, ks, vs, kTs = [], [], [], []
    for h in range(M_HEADS):
        hs = slice(h * M_DH, (h + 1) * M_DH)
        qk_h = jnp.dot(xab[:, hs], wqk_ref[h], preferred_element_type=f32)
        qs.append(qk_h[:, 0:M_DH])
        ks.append(qk_h[:, M_DH:2 * M_DH] * (M_DH ** -0.5))
        vs.append(jnp.dot(xb[:, hs], wv_ref[h], preferred_element_type=f32))
        if S == 1:
            kT_h = lax.dot_general(wkT_ref[h], xab[:, hs], (((1,), (1,)), ((), ())),
                                   preferred_element_type=f32)
            kTs.append((kT_h * (M_DH ** -0.5)).astype(bf16))

    ti = lax.broadcasted_iota(jnp.int32, (R, R), 0)
    si = lax.broadcasted_iota(jnp.int32, (R, R), 1)
    mask = (si <= ti) & ((ti // L) == (si // L))
    ones_b = jnp.ones((R, M_DH), bf16)
    pairs = [(g, h) for g in range(G) for h in range(M_HEADS)]
    blk = lambda g: slice(g * R, (g + 1) * R)
    seqs = [slice(b * L, (b + 1) * L) for b in range(S)]
    q = {(g, h): qs[h][blk(g)] for g, h in pairs}
    k = {(g, h): ks[h][blk(g)] for g, h in pairs}
    v = {(g, h): vs[h][blk(g)] for g, h in pairs}
    qb = {p: q[p].astype(bf16) for p in pairs}
    kb = {p: k[p].astype(bf16) for p in pairs}
    qk = {p: lax.dot_general(qb[p], kb[p], (((1,), (1,)), ((), ())), preferred_element_type=f32)
          for p in pairs}
    sm = {}
    for g, h in pairs:
        a_row = a[g * SUBLANES + h:g * SUBLANES + h + 1, :]
        decay = jnp.exp(jnp.where(mask, a_row - col(g, 0, h), -jnp.inf))
        sm[g, h] = (qk[g, h] * decay).astype(bf16)
    nd = {p: jnp.dot(sm[p], jnp.concatenate([v[p].astype(bf16), ones_b], axis=1),
                     preferred_element_type=f32) for p in pairs}
    c_old = {(g, h, b): cN_ref[g * S + b, h] for g, h in pairs for b in range(S)}
    n_old = {(g, h, b): nN_ref[g * S + b, h] for g, h in pairs for b in range(S)}
    q_c = {(g, h, b): jnp.dot(qb[g, h][seqs[b]], c_old[g, h, b].astype(bf16), preferred_element_type=f32)
           for g, h in pairs for b in range(S)}
    hh = {}
    for g, h in pairs:
        g_col, e_col = col(g, 1, h), col(g, 2, h)
        pieces = []
        for b, rs in enumerate(seqs):
            q_n = jnp.sum(q[g, h][rs] * n_old[g, h, b], axis=1, keepdims=True)
            num = nd[g, h][rs, 0:M_DH] + g_col[rs] * q_c[g, h, b]
            den = nd[g, h][rs, M_DH:2 * M_DH] + g_col[rs] * q_n
            pieces.append(num / jnp.maximum(jnp.abs(den), e_col[rs]))
        hh[g, h] = pieces[0] if S == 1 else jnp.concatenate(pieces, axis=0)
    mu = {p: jnp.mean(hh[p], axis=1, keepdims=True) for p in pairs}
    var = {p: jnp.mean(jnp.square(hh[p] - mu[p]), axis=1, keepdims=True) for p in pairs}
    hn_blocks = [jnp.concatenate([(hh[g, h] - mu[g, h]) * lax.rsqrt(var[g, h] + LN_EPS)
                                  for h in range(M_HEADS)], axis=1) for g in range(G)]
    new_c, new_n = [], []
    for g, h in pairs:
        w_col = col(g, 3, h)
        wv = (w_col * v[g, h]).astype(bf16)
        wk = w_col * k[g, h]
        for b, rs in enumerate(seqs):
            g_end = col(g, 4, h)[(b + 1) * L - 1:(b + 1) * L, :]
            if S == 1:
                kv = jnp.dot(kTs[h][:, blk(g)], wv, preferred_element_type=f32)
            else:
                kv = lax.dot_general(kb[g, h][rs], wv[rs], (((0,), (0,)), ((), ())),
                                     preferred_element_type=f32)
            new_c.append((g * S + b, h, g_end * c_old[g, h, b] + kv))
            new_n.append((g * S + b, h, g_end * n_old[g, h, b] + jnp.sum(wk[rs], axis=0, keepdims=True)))

    hn = jnp.concatenate(hn_blocks, axis=0) * ng_ref[...]
    y = jax.nn.sigmoid(z_ref[...].reshape(GR, D_M)) * (hn + sk_ref[...] * xa)
    y_ref[...] = y.reshape(G, R, D_M)
    for sq, h, val in new_c:
        cN_ref[sq, h] = val
    for sq, h, val in new_n:
        nN_ref[sq, h] = val
    bN_ref[...] = xs_scr[:, SUBLANES - (CONV_W - 1):SUBLANES, :]


def _mlstm(proj, B, T, S, L, G, state, cw, cb, wqk, wv, wkT, bif, ng, sk, riders=()):
    R = S * L
    assert R == M_ROWS == LANES
    nblk, nc = B // S, T // L
    nb = nblk // G
    assert nblk * S == B and nc * L == T and nb * G == nblk and (S == 1 or nc == 1)
    steps = nb * nc
    assert all(r.shape[0] % steps == 0 for r in riders)
    rider_specs = [pl.BlockSpec((r.shape[0] // steps,) + r.shape[1:], lambda b, j: (b * nc + j, 0, 0))
                   for r in riders]
    proj4 = proj.reshape(nblk, nc, R, N_PROJ)
    const2 = lambda b, j: (0, 0)
    const3 = lambda b, j: (0, 0, 0)
    gate_blk = (2 * D_RG + 2 * D_M) // LANES
    GS = G * S
    state_args, state_specs = (), []
    if state is not None:
        buf8, c0, n0, m0_lanes = state
        state_args = (buf8, c0, n0, m0_lanes.reshape(SUBLANES, nblk, LANES).transpose(1, 0, 2))
        state_specs = [
            pl.BlockSpec((GS, CONV_W - 1, D_M), lambda b, j: (b, 0, 0)),
            pl.BlockSpec((GS, M_HEADS, M_DH, M_DH), lambda b, j: (b, 0, 0, 0)),
            pl.BlockSpec((GS, M_HEADS, 1, M_DH), lambda b, j: (b, 0, 0, 0)),
            pl.BlockSpec((G, SUBLANES, LANES), lambda b, j: (b, 0, 0)),
        ]
    outs = pl.pallas_call(
        functools.partial(_mlstm_kernel, S=S, L=L, G=G, n_riders=len(riders), fresh=state is None),
        grid=(nb, nc),
        in_specs=[
            pl.BlockSpec((G, None, R, D_M), lambda b, j: (b, j, 0, 2)),
            pl.BlockSpec((G, None, R, D_M), lambda b, j: (b, j, 0, 3)),
            pl.BlockSpec((G, None, R, LANES), lambda b, j: (b, j, 0, gate_blk)),
        ] + state_specs + [
            pl.BlockSpec((CONV_W, D_M), const2),
            pl.BlockSpec((1, D_M), const2),
            pl.BlockSpec((M_HEADS, M_DH, 2 * M_DH), const3),
            pl.BlockSpec((M_HEADS, M_DH, M_DH), const3),
            pl.BlockSpec((M_HEADS, M_DH, M_DH), const3),
            pl.BlockSpec((1, LANES), const2),
            pl.BlockSpec((1, D_M), const2),
            pl.BlockSpec((1, D_M), const2),
        ] + rider_specs,
        out_specs=[
            pl.BlockSpec((G, None, R, D_M), lambda b, j: (b, j, 0, 0)),
            pl.BlockSpec((GS, M_HEADS, M_DH, M_DH), lambda b, j: (b, 0, 0, 0)),
            pl.BlockSpec((GS, M_HEADS, 1, M_DH), lambda b, j: (b, 0, 0, 0)),
            pl.BlockSpec((G, None, SUBLANES, LANES), lambda b, j: (b, j, 0, 0)),
            pl.BlockSpec((GS, CONV_W - 1, D_M), lambda b, j: (b, 0, 0)),
        ] + rider_specs,
        out_shape=[
            jax.ShapeDtypeStruct((nblk, nc, R, D_M), f32),
            jax.ShapeDtypeStruct((B, M_HEADS, M_DH, M_DH), f32),
            jax.ShapeDtypeStruct((B, M_HEADS, 1, M_DH), f32),
            jax.ShapeDtypeStruct((nblk, nc, SUBLANES, LANES), f32),
            jax.ShapeDtypeStruct((B, CONV_W - 1, D_M), f32),
        ] + [jax.ShapeDtypeStruct(r.shape, bf16) for r in riders],
        scratch_shapes=[pltpu.VMEM((GS, SUBLANES, D_M), f32), pltpu.VMEM((G, SUBLANES, LANES), f32)],
        compiler_params=_cparams(2),
        name="mlstm",
    )(proj4, proj4, proj4, *state_args, cw, cb, wqk, wv, wkT, bif, ng, sk, *riders)
    y, c_new, n_new, m_t, b_new = outs[:5]
    m_last = m_t[:, nc - 1, :M_HEADS, :].reshape(nblk, M_HEADS, S, L)[:, :, :, L - 1]
    m_last = m_last.transpose(0, 2, 1).reshape(B, M_HEADS)
    return (y.reshape(B * T, D_M), c_new, n_new, m_last, b_new) + tuple(outs[5:])


def _layer_norm(x, g, b):
    mu = jnp.mean(x, axis=-1, keepdims=True)
    var = jnp.mean(jnp.square(x - mu), axis=-1, keepdims=True)
    return (x - mu) * lax.rsqrt(var + LN_EPS) * g + b


def _first_lane_of_max(vals, lane_f):
    vmax = jnp.max(vals, axis=1, keepdims=True)
    idx = jnp.min(jnp.where(vals == vmax, lane_f, float(LANES)), axis=1, keepdims=True)
    return vmax, idx


def _outproj_kernel(xp_ref, xs_ref, rgp_ref, rgs_ref, mp_ref, ms_ref, wo_ref, g1_ref, b1_ref,
                    wr_ref, br_ref, x1_ref, cnt_ref, *, n_p):
    i = pl.program_id(0)

    @pl.when(i == 0)
    def _():
        cnt_ref[...] = jnp.zeros_like(cnt_ref)

    def route(lg, lane, lane_f):
        neg = -jnp.inf
        gl = jnp.where(lane < N_GROUPS, lg, neg)
        gmax, gidx = _first_lane_of_max(gl, lane_f)
        p_g = 1.0 / jnp.sum(jnp.exp(gl - gmax), axis=1, keepdims=True)
        e_lo = float(N_GROUPS) + float(EXPERTS_PER_GROUP) * gidx
        el = jnp.where((lane_f >= e_lo) & (lane_f < e_lo + float(EXPERTS_PER_GROUP)), lg, neg)
        v1, i1 = _first_lane_of_max(el, lane_f)
        v2, i2 = _first_lane_of_max(jnp.where(lane_f == i1, neg, el), lane_f)
        d = jnp.exp(v2 - v1)
        w1 = p_g / (1.0 + d)
        w2 = p_g * d / (1.0 + d)
        first_low = i1 < i2
        j_lo = jnp.minimum(i1, i2) - e_lo
        j_hi = jnp.maximum(i1, i2) - e_lo
        n_first = float(EXPERTS_PER_GROUP - 1) * j_lo - 0.5 * j_lo * (j_lo - 1.0)
        cls = float(PAIRS_PER_GROUP) * gidx + n_first + (j_hi - j_lo - 1.0)
        return cls, jnp.where(first_low, w1, w2), jnp.where(first_low, w2, w1)

    def run(x_ref, rg_ref, m_ref):
        rp = TM_OUT // OUT_PARTS
        parts = [slice(p * rp, (p + 1) * rp) for p in range(OUT_PARTS)]
        lane = lax.broadcasted_iota(jnp.int32, (rp, LANES), 1)
        lane_f = lane.astype(f32)
        heads = [jnp.concatenate([rg_ref[r, :].astype(bf16), m_ref[r, :].astype(bf16)], axis=1) for r in parts]
        mix = [jnp.dot(h, wo_ref[...], preferred_element_type=f32) for h in heads]
        x1 = [_layer_norm(ALPHA * x_ref[r, :] + mx, g1_ref[...], b1_ref[...]) for r, mx in zip(parts, mix)]
        for r, v in zip(parts, x1):
            x1_ref[r, 0:D_MODEL] = v

        hi = [v.astype(bf16) for v in x1]
        lo = [(v - h.astype(f32)).astype(bf16) for v, h in zip(x1, hi)]
        hi_terms = [jnp.dot(h, wr_ref[...], preferred_element_type=f32) for h in hi]
        lo_term = [jnp.dot(v, wr_ref[:, 0:LANES], preferred_element_type=f32) for v in lo]
        lg = [ht[:, 0:LANES] + ht[:, LANES:2 * LANES] + lt + br_ref[...] for ht, lt in zip(hi_terms, lo_term)]
        routed = [route(v, lane, lane_f) for v in lg]

        ti = lax.broadcasted_iota(jnp.int32, (rp, rp), 0)
        si = lax.broadcasted_iota(jnp.int32, (rp, rp), 1)
        tri = (si <= ti).astype(bf16)
        hot = [lane_f == cls for cls, _, _ in routed]
        cum = [jnp.dot(tri, h.astype(bf16), preferred_element_type=f32) for h in hot]
        seen = cnt_ref[...]
        for r, h, cm, (cls, w_lo, w_hi) in zip(parts, hot, cum, routed):
            rank = jnp.sum(jnp.where(h, cm - 1.0 + seen, 0.0), axis=1, keepdims=True)
            seen = seen + cm[rp - 1:rp, :]
            info = jnp.zeros((rp, LANES), f32)
            for c, val in enumerate((cls, rank, w_lo, w_hi)):
                info = jnp.where(lane == c, val, info)
            x1_ref[r, D_MODEL:D_MODEL + LANES] = info
        cnt_ref[...] = seen

    @pl.when(i < n_p)
    def _():
        run(xp_ref, rgp_ref, mp_ref)

    @pl.when(i >= n_p)
    def _():
        run(xs_ref, rgs_ref, ms_ref)


def _outproj(xp, xs, rgp, rgs, mp, ms, wo, g1, b1, wr, br):
    n_p, n_s = xp.shape[0] // TM_OUT, xs.shape[0] // TM_OUT
    n = n_p + n_s
    pmap = lambda i: (jnp.minimum(i, n_p - 1), 0)
    smap = lambda i: (jnp.maximum(i - n_p, 0), 0)
    const = lambda i: (0, 0)
    return pl.pallas_call(
        functools.partial(_outproj_kernel, n_p=n_p),
        grid=(n,),
        in_specs=[
            pl.BlockSpec((TM_OUT, D_MODEL), pmap), pl.BlockSpec((TM_OUT, D_MODEL), smap),
            pl.BlockSpec((TM_OUT, D_RG), pmap), pl.BlockSpec((TM_OUT, D_RG), smap),
            pl.BlockSpec((TM_OUT, D_M), pmap), pl.BlockSpec((TM_OUT, D_M), smap),
            pl.BlockSpec((D_RG + D_M, D_MODEL), const),
            pl.BlockSpec((1, D_MODEL), const), pl.BlockSpec((1, D_MODEL), const),
            pl.BlockSpec((D_MODEL, 2 * LANES), const),
            pl.BlockSpec((1, LANES), const),
        ],
        out_specs=[
            pl.BlockSpec((TM_OUT, X_ROW), lambda i: (i, 0)),
            pl.BlockSpec((1, LANES), const),
        ],
        out_shape=[
            jax.ShapeDtypeStruct((n * TM_OUT, X_ROW), f32),
            jax.ShapeDtypeStruct((1, LANES), f32),
        ],
        compiler_params=_cparams(1),
        name="outproj",
    )(xp, xs, rgp, rgs, mp, ms, wo, g1, b1, wr, br)


GRP_MOVE = TM_MOVE // SUBLANES


def _grouped(x):
    return x.reshape(x.shape[0] // SUBLANES, SUBLANES, x.shape[1])


def _tile_rows_wait(hbm_ref, sem):
    rows = hbm_ref.at[pl.ds(0, TM_MOVE)]
    pltpu.make_async_copy(rows, rows, sem).wait()


def _for_each_tile_row(body):
    def group(k, c):
        for j in range(SUBLANES):
            body(k, j, k * SUBLANES + j)
        return c

    lax.fori_loop(0, GRP_MOVE, group, 0)


def _dispatch_kernel(pad_ref, pos_ref, x1_ref, xs_ref, zero_scr, xbuf, sem, zsem, lsem):
    i = pl.program_id(0)
    n = pl.num_programs(0)

    def zero_tile(t):
        return pltpu.make_async_copy(zero_scr, xs_ref.at[pl.ds(pl.multiple_of(t * TM_MOE, TM_MOE), TM_MOE)], zsem)

    def zero_chunk(row0):
        return pltpu.make_async_copy(zero_scr.at[pl.ds(0, SUBLANES)],
                                     xs_ref.at[pl.ds(pl.multiple_of(row0, SUBLANES), SUBLANES)], zsem)

    @pl.when(i == 0)
    def _():
        zero_scr[...] = jnp.zeros_like(zero_scr)
        n_used = pad_ref[2 * N_CLASSES]
        n_tiles = xs_ref.shape[0] // TM_MOE
        for go in (lambda cp: cp.start(), lambda cp: cp.wait()):
            lax.fori_loop(n_used, n_tiles, lambda t, c: (go(zero_tile(t)), c)[1], 0)

            def per_class(cl, c):
                row0 = pad_ref[2 * cl]
                lax.fori_loop(0, pad_ref[2 * cl + 1],
                              lambda k, c2: (go(zero_chunk(row0 + k * SUBLANES)), c2)[1], 0)
                return c

            lax.fori_loop(0, N_CLASSES, per_class, 0)

    def load(t, s):
        return pltpu.make_async_copy(x1_ref.at[pl.ds(t * GRP_MOVE, GRP_MOVE)], xbuf.at[s], lsem.at[s])

    @pl.when(i == 0)
    def _():
        load(0, 0).start()

    @pl.when(i + 1 < n)
    def _():
        load(i + 1, (i + 1) % 3).start()

    slot = i % 3
    load(i, slot).wait()

    def start(k, j, r):
        pltpu.make_async_copy(xbuf.at[slot, k, pl.ds(j, 1)], xs_ref.at[pl.ds(pos_ref[0, 0, r], 1)],
                              sem.at[slot]).start()

    _for_each_tile_row(start)

    @pl.when(i > 0)
    def _():
        _tile_rows_wait(xs_ref, sem.at[(i + 2) % 3])

    @pl.when(i == n - 1)
    def _():
        _tile_rows_wait(xs_ref, sem.at[slot])


def _dispatch(pad_plan, pos, x1, n_rows):
    n = x1.shape[0] // TM_MOVE
    return pl.pallas_call(
        _dispatch_kernel,
        grid_spec=pltpu.PrefetchScalarGridSpec(
            num_scalar_prefetch=1,
            grid=(n,),
            in_specs=[
                pl.BlockSpec((1, 1, TM_MOVE), lambda i, *_: (i, 0, 0), memory_space=pltpu.SMEM),
                pl.BlockSpec(memory_space=pl.ANY),
            ],
            out_specs=pl.BlockSpec(memory_space=pl.ANY),
            scratch_shapes=[pltpu.VMEM((TM_MOE, X_ROW), f32),
                            pltpu.VMEM((3, GRP_MOVE, SUBLANES, X_ROW), f32),
                            pltpu.SemaphoreType.DMA((3,)), pltpu.SemaphoreType.DMA(()),
                            pltpu.SemaphoreType.DMA((3,))],
        ),
        out_shape=jax.ShapeDtypeStruct((n_rows, X_ROW), f32),
        compiler_params=_cparams(1),
        name="dispatch",
    )(pad_plan, pos, _grouped(x1))


def _expert_kernel(ta_ref, tb_ref, ca_ref, cb_ref, sa_ref, sb_ref, nu_ref,
                   x_ref, wg_hbm, wu_hbm, wd_hbm, g2_ref, b2_ref, y_ref,
                   wga, wua, wda, wgb, wub, wdb, sem_a, sem_b):
    i = pl.program_id(0)
    n_used = nu_ref[0]
    first = i * MOE_TILES
    side_a = (ta_ref, ca_ref, sa_ref, (wga, wua, wda), sem_a)
    side_b = (tb_ref, cb_ref, sb_ref, (wgb, wub, wdb), sem_b)

    def copies(side, j):
        t_ref, _, s_ref, bufs, sem = side
        e, s = t_ref[j], s_ref[j]
        return [pltpu.make_async_copy(w.at[e], buf.at[s], sem.at[s])
                for w, buf in zip((wg_hbm, wu_hbm, wd_hbm), bufs)]

    def fetch(j):
        @pl.when(j < n_used)
        def _():
            for side in (side_a, side_b):
                @pl.when(side[1][j] == 1)
                def _():
                    for cp in copies(side, j):
                        cp.start()

    @pl.when(i == 0)
    def _():
        for j in range(W_AHEAD):
            fetch(j)

    for t in range(MOE_TILES):
        fetch(first + W_AHEAD + t)

    @pl.when(first < n_used)
    def _():
        for t in range(MOE_TILES):
            for side in (side_a, side_b):
                @pl.when(side[1][first + t] == 1)
                def _():
                    for cp in copies(side, first + t):
                        cp.wait()
        rows = [slice(t * TM_MOE, (t + 1) * TM_MOE) for t in range(MOE_TILES)]
        x = [x_ref[r, 0:D_MODEL] for r in rows]
        info = [x_ref[r, D_MODEL:X_ROW] for r in rows]
        xb = [v.astype(bf16) for v in x]
        units = [(t, bufs, s_ref[first + t]) for t in range(MOE_TILES)
                 for bufs, s_ref in (((wga, wua, wda), sa_ref), ((wgb, wub, wdb), sb_ref))]
        hg = [jnp.dot(xb[t], bufs[0][s], preferred_element_type=f32) for t, bufs, s in units]
        hu = [jnp.dot(xb[t], bufs[1][s], preferred_element_type=f32) for t, bufs, s in units]
        mid = [(g * jax.nn.sigmoid(g) * u).astype(bf16) for g, u in zip(hg, hu)]
        ys = [jnp.dot(m, bufs[2][s], preferred_element_type=f32)
              for m, (t, bufs, s) in zip(mid, units)]
        for t in range(MOE_TILES):
            ffn = info[t][:, 2:3] * ys[2 * t] + info[t][:, 3:4] * ys[2 * t + 1]
            y_ref[rows[t], :] = _layer_norm(ALPHA * x[t] + ffn, g2_ref[...], b2_ref[...])

    @pl.when(first >= n_used)
    def _():
        y_ref[...] = jnp.zeros_like(y_ref)


def _experts(tile_ea, tile_eb, n_used, xs, wg, wu, wd, g2, b2):
    nt = xs.shape[0] // TM_MOE

    def ring_plan(tile_e):
        opens = jnp.concatenate([jnp.ones((1,), jnp.int32), (tile_e[1:] != tile_e[:-1]).astype(jnp.int32)])
        return opens, (jnp.cumsum(opens) - 1) % W_SLOTS

    open_a, slot_a = ring_plan(tile_ea)
    open_b, slot_b = ring_plan(tile_eb)
    assert nt % MOE_TILES == 0
    n_used = ((n_used + MOE_TILES - 1) // MOE_TILES) * MOE_TILES
    const = lambda i, *_: (0, 0)
    xmap = lambda i, ta, tb, ca, cb, sa, sb, nu: (jnp.minimum(i, nu[0] // MOE_TILES - 1), 0)
    w_in = pltpu.VMEM((W_SLOTS, D_MODEL, D_EXPERT), bf16)
    w_out = pltpu.VMEM((W_SLOTS, D_EXPERT, D_MODEL), bf16)
    return pl.pallas_call(
        _expert_kernel,
        grid_spec=pltpu.PrefetchScalarGridSpec(
            num_scalar_prefetch=7,
            grid=(nt // MOE_TILES,),
            in_specs=[
                pl.BlockSpec((MOE_TILES * TM_MOE, X_ROW), xmap),
                pl.BlockSpec(memory_space=pl.ANY), pl.BlockSpec(memory_space=pl.ANY),
                pl.BlockSpec(memory_space=pl.ANY),
                pl.BlockSpec((1, D_MODEL), const), pl.BlockSpec((1, D_MODEL), const),
            ],
            out_specs=pl.BlockSpec((MOE_TILES * TM_MOE, D_MODEL), lambda i, *_: (i, 0)),
            scratch_shapes=[w_in, w_in, w_out, w_in, w_in, w_out,
                            pltpu.SemaphoreType.DMA((W_SLOTS,)), pltpu.SemaphoreType.DMA((W_SLOTS,))],
        ),
        out_shape=jax.ShapeDtypeStruct((nt * TM_MOE, D_MODEL), f32),
        compiler_params=_cparams(1),
        name="experts",
    )(tile_ea, tile_eb, open_a, open_b, slot_a.astype(jnp.int32), slot_b.astype(jnp.int32), n_used,
      xs, wg, wu, wd, g2, b2)


def _collect_kernel(pos_ref, posn_ref, ys_ref, op_ref, os_ref, ybuf, sem, *, n_p):
    i = pl.program_id(0)
    n = pl.num_programs(0)
    slot = i % 2

    def gather(p_ref, s):
        def start(k, j, r):
            pltpu.make_async_copy(ys_ref.at[pl.ds(p_ref[0, 0, r], 1)], ybuf.at[s, k, pl.ds(j, 1)],
                                  sem.at[s]).start()

        _for_each_tile_row(start)

    @pl.when(i == 0)
    def _():
        gather(pos_ref, 0)

    @pl.when(i + 1 < n)
    def _():
        gather(posn_ref, 1 - slot)

    _tile_rows_wait(ys_ref, sem.at[slot])

    @pl.when(i < n_p)
    def _():
        op_ref[...] = ybuf[slot].reshape(TM_MOVE, D_MODEL)

    @pl.when(i >= n_p)
    def _():
        os_ref[...] = ybuf[slot].reshape(TM_MOVE, D_MODEL)


def _collect(pos, ys, n_p):
    n = pos.shape[0]
    n_s = n - n_p
    return pl.pallas_call(
        functools.partial(_collect_kernel, n_p=n_p),
        grid=(n,),
        in_specs=[
            pl.BlockSpec((1, 1, TM_MOVE), lambda i: (i, 0, 0), memory_space=pltpu.SMEM),
            pl.BlockSpec((1, 1, TM_MOVE), lambda i: (jnp.minimum(i + 1, n - 1), 0, 0),
                         memory_space=pltpu.SMEM),
            pl.BlockSpec(memory_space=pl.ANY),
        ],
        out_specs=[
            pl.BlockSpec((TM_MOVE, D_MODEL), lambda i: (jnp.minimum(i, n_p - 1), 0)),
            pl.BlockSpec((TM_MOVE, D_MODEL), lambda i: (jnp.maximum(i - n_p, 0), 0)),
        ],
        out_shape=[
            jax.ShapeDtypeStruct((n_p * TM_MOVE, D_MODEL), f32),
            jax.ShapeDtypeStruct((n_s * TM_MOVE, D_MODEL), f32),
        ],
        scratch_shapes=[pltpu.VMEM((2, GRP_MOVE, SUBLANES, D_MODEL), f32), pltpu.SemaphoreType.DMA((2,))],
        compiler_params=_cparams(1),
        name="collect",
    )(pos, pos, ys)


def _block_diag(w):
    n, d, _ = w.shape
    eye = jnp.eye(n, dtype=w.dtype)
    return (eye[:, None, :, None] * w[:, :, None, :]).reshape(n * d, n * d)


def _pair_walk(n):
    total = n * (n - 1) // 2

    def extend(path, used):
        if len(path) == total:
            return path
        a, b = path[-1]
        for c in range(n):
            for nxt in ((a, c), (c, b)):
                key = frozenset(nxt)
                if len(key) == 2 and key not in used:
                    out = extend(path + [nxt], used | {key})
                    if out:
                        return out
        return None

    return extend([(0, 1)], {frozenset((0, 1))})


def kernel(x_prompt, x_sample, state_rg_h, state_rg_conv, state_m_C, state_m_n, state_m_m, state_m_conv, w_in, rg_conv_w, rg_conv_b, rg_w_a, rg_b_a, rg_w_x, rg_b_x, rg_lambda, m_conv_w, m_conv_b, m_w_q, m_w_k, m_w_v, m_b_i, m_b_f, m_norm_g, m_skip, w_out, ln1_g, ln1_b, ln2_g, ln2_b, moe_w_group, moe_b_group, moe_w_expert, moe_b_expert, moe_w_gate, moe_w_up, moe_w_down):
    BP, TP, _ = x_prompt.shape
    BS, TS, _ = x_sample.shape
    n_prompt, n_sample = BP * TP, BS * TS
    xp = x_prompt.reshape(n_prompt, D_MODEL)
    xs = x_sample.reshape(n_sample, D_MODEL)
    l = 0

    w_in_p = jnp.pad(w_in[l], ((0, 0), (0, N_PROJ - w_in.shape[-1]))).astype(bf16)
    wa = _block_diag(rg_w_a[l]).astype(bf16)
    wx = _block_diag(rg_w_x[l]).astype(bf16)
    row = lambda v: v.reshape(1, -1)
    bif = jnp.pad(jnp.concatenate([m_b_i[l], m_b_f[l]]), (0, LANES - 2 * M_HEADS)).reshape(1, LANES)
    wqk = jnp.concatenate([m_w_q[l], m_w_k[l]], axis=-1).astype(bf16)
    wv = m_w_v[l].astype(bf16)
    wkT = m_w_k[l].transpose(0, 2, 1).astype(bf16)
    w_route = jnp.pad(jnp.concatenate([moe_w_group[l], moe_w_expert[l]], axis=1),
                      ((0, 0), (0, LANES - N_GROUPS - N_EXPERTS)))
    wr_hi = w_route.astype(bf16)
    wr_lo = (w_route - wr_hi.astype(f32)).astype(bf16)
    wr = jnp.concatenate([wr_hi, wr_lo], axis=1)
    b_route = jnp.pad(jnp.concatenate([moe_b_group[l], moe_b_expert[l]]),
                      (0, LANES - N_GROUPS - N_EXPERTS)).reshape(1, LANES)

    proj_p, proj_s = _inproj(xp, xs, w_in_p)

    rg_args = (rg_conv_w[l], row(rg_conv_b[l]), wa, row(rg_b_a[l]), wx, row(rg_b_x[l]), row(rg_lambda[l]))
    s_blk = M_ROWS // TS
    yrg_p, p_rg_h, p_rg_conv = _rglru(proj_p, 0, BP, TP, 1, RG_ROWS, None, *rg_args)
    rg_state = (state_rg_conv[l], state_rg_h[l].reshape(BS, 1, D_RG))
    yrg_s, s_rg_h, s_rg_conv = _rglru(proj_s, 0, BS, TS, RG_ROWS // TS, TS, rg_state, *rg_args)

    m_args = (m_conv_w[l], row(m_conv_b[l]), wqk, wv, wkT, bif, row(m_norm_g[l]), row(m_skip[l]))
    ym_p, p_m_C, p_m_n, p_m_m, p_m_conv, wg_b, wu_b, wd_b = _mlstm(
        proj_p, BP, TP, 1, M_ROWS, M_BLOCKS_PROMPT, None, *m_args,
        riders=(moe_w_gate[l], moe_w_up[l], moe_w_down[l]))
    m0_s = jnp.pad(jnp.repeat(state_m_m[l].T, TS, axis=1), ((0, SUBLANES - M_HEADS), (0, 0)))
    m_state = (state_m_conv[l], state_m_C[l], state_m_n[l].reshape(BS, M_HEADS, 1, M_DH), m0_s)
    ym_s, s_m_C, s_m_n, s_m_m, s_m_conv = _mlstm(proj_s, BS, TS, s_blk, TS, M_BLOCKS_SAMPLE, m_state, *m_args)

    x1, cnt = _outproj(xp, xs, yrg_p, yrg_s, ym_p, ym_s, w_out[l].astype(bf16),
                       row(ln1_g[l]), row(ln1_b[l]), wr, b_route)

    n_tok = n_prompt + n_sample
    n_tiles = n_tok // TM_MOE + N_CLASSES
    lay_cls, lay_a, lay_b = [], [], []
    for g in range(N_GROUPS):
        for ja, jb in _pair_walk(EXPERTS_PER_GROUP):
            assert ja < jb
            n_first = (EXPERTS_PER_GROUP - 1) * ja - ja * (ja - 1) // 2
            lay_cls.append(PAIRS_PER_GROUP * g + n_first + jb - ja - 1)
            lay_a.append(g * EXPERTS_PER_GROUP + ja)
            lay_b.append(g * EXPERTS_PER_GROUP + jb)
    classes = jnp.arange(N_CLASSES, dtype=jnp.int32)
    in_slot = jnp.array(lay_cls, jnp.int32)[:, None] == classes
    counts = jnp.sum(jnp.where(in_slot, cnt[0, :N_CLASSES].astype(jnp.int32), 0), axis=1)
    padded = ((counts + TM_MOE - 1) // TM_MOE) * TM_MOE
    ends = jnp.cumsum(padded)
    offs = ends - padded
    offs_of_cls = jnp.sum(jnp.where(in_slot, offs[:, None], 0), axis=0)
    cls = x1[:, D_MODEL].astype(jnp.int32)
    rank = x1[:, D_MODEL + 1].astype(jnp.int32)
    pos = jnp.sum(jnp.where(cls[:, None] == classes, offs_of_cls, 0), axis=-1) + rank
    pos = pos.reshape(n_tok // TM_MOVE, 1, TM_MOVE)
    tiles = jnp.arange(n_tiles, dtype=jnp.int32)
    tile_slot = jnp.minimum(jnp.sum(tiles[:, None] >= (ends // TM_MOE)[None, :], axis=1), N_CLASSES - 1)
    on_slot = tile_slot[:, None] == classes
    tile_ea = jnp.sum(jnp.where(on_slot, jnp.array(lay_a, jnp.int32), 0), axis=-1).astype(jnp.int32)
    tile_eb = jnp.sum(jnp.where(on_slot, jnp.array(lay_b, jnp.int32), 0), axis=-1).astype(jnp.int32)
    n_used = (ends[-1] // TM_MOE).reshape(1).astype(jnp.int32)
    pad_row0 = ((offs + counts) // SUBLANES) * SUBLANES
    pad_plan = jnp.concatenate([jnp.stack([pad_row0, (ends - pad_row0) // SUBLANES], axis=1).reshape(-1),
                                     n_used]).astype(jnp.int32)

    x_sorted = _dispatch(pad_plan, pos, x1, n_tiles * TM_MOE)
    y_sorted = _experts(tile_ea, tile_eb, n_used, x_sorted, wg_b, wu_b, wd_b,
                        row(ln2_g[l]), row(ln2_b[l]))
    y_p, y_s = _collect(pos, y_sorted, n_prompt // TM_MOVE)

    return (y_p.reshape(BP, TP, D_MODEL), y_s.reshape(BS, TS, D_MODEL),
            p_rg_h.reshape(1, BP, D_RG), p_rg_conv[None], p_m_C[None], p_m_n.reshape(1, BP, M_HEADS, M_DH),
            p_m_m[None], p_m_conv[None],
            s_rg_h.reshape(1, BS, D_RG), s_rg_conv[None], s_m_C[None], s_m_n.reshape(1, BS, M_HEADS, M_DH),
            s_m_m[None], s_m_conv[None])
```

```python
import functools

import jax
import jax.numpy as jnp
from jax import lax
from jax.experimental import pallas as pl
from jax.experimental.pallas import tpu as pltpu

f32 = jnp.float32
bf16 = jnp.bfloat16

D_MODEL = 1024
D_RG = 512
RG_C = 8.0
D_M = 512
M_HEADS = 4
M_DH = 128
CONV_W = 4
N_GROUPS = 4
EXPERTS_PER_GROUP = 8
N_EXPERTS = 32
D_EXPERT = 256
ALPHA = 2.0 ** 0.25
LN_EPS = 1e-5
M_INIT = -1.0e4

LANES = 128
SUBLANES = 8
TM = 512
TM_MOVE = 1024
TM_OUT = 1024
OUT_PARTS = 8
TM_MOE = 128
MOE_TILES = 4
W_AHEAD = 8
W_SLOTS = W_AHEAD + MOE_TILES
PAIRS_PER_GROUP = EXPERTS_PER_GROUP * (EXPERTS_PER_GROUP - 1) // 2
N_CLASSES = N_GROUPS * PAIRS_PER_GROUP
X_ROW = D_MODEL + LANES
N_PROJ = 2 * D_RG + 2 * D_M + LANES
RG_ROWS = 1024
M_ROWS = 128
M_BLOCKS_PROMPT = 8
M_BLOCKS_SAMPLE = 2
V7X_VMEM_BYTES = 64 * 1024 * 1024
VMEM_LIMIT = V7X_VMEM_BYTES - 8 * 1024 * 1024


def _cparams(n_axes):
    return pltpu.CompilerParams(dimension_semantics=("arbitrary",) * n_axes,
                                vmem_limit_bytes=VMEM_LIMIT)


def _inproj_kernel(xp_ref, xs_ref, w_ref, op_ref, os_ref, *, n_p):
    i = pl.program_id(0)

    def run(x_ref, o_ref):
        o_ref[...] = jnp.dot(x_ref[...].astype(bf16), w_ref[...], preferred_element_type=f32)

    @pl.when(i < n_p)
    def _():
        run(xp_ref, op_ref)

    @pl.when(i >= n_p)
    def _():
        run(xs_ref, os_ref)


def _inproj(xp, xs, w):
    n_p, n_s = xp.shape[0] // TM, xs.shape[0] // TM
    pmap = lambda i: (jnp.minimum(i, n_p - 1), 0)
    smap = lambda i: (jnp.maximum(i - n_p, 0), 0)
    return pl.pallas_call(
        functools.partial(_inproj_kernel, n_p=n_p),
        grid=(n_p + n_s,),
        in_specs=[
            pl.BlockSpec((TM, D_MODEL), pmap),
            pl.BlockSpec((TM, D_MODEL), smap),
            pl.BlockSpec((D_MODEL, N_PROJ), lambda i: (0, 0)),
        ],
        out_specs=[pl.BlockSpec((TM, N_PROJ), pmap), pl.BlockSpec((TM, N_PROJ), smap)],
        out_shape=[jax.ShapeDtypeStruct((n_p * TM, N_PROJ), f32),
                   jax.ShapeDtypeStruct((n_s * TM, N_PROJ), f32)],
        compiler_params=_cparams(1),
        name="inproj",
    )(xp, xs, w)


def _causal_conv(tail_scr, x, cw_ref, cb_ref, S, L, C):
    tail = tail_scr[...]
    sub = lax.broadcasted_iota(jnp.int32, (S, SUBLANES, C), 1)
    acc = cb_ref[...] + cw_ref[CONV_W - 1:CONV_W, :] * x
    for d in range(1, CONV_W):
        back = pltpu.roll(x, d, 0).reshape(S, L, C)
        head = jnp.where(sub < d, pltpu.roll(tail, d, 1), back[:, 0:SUBLANES, :])
        if L > SUBLANES:
            back = jnp.concatenate([head, back[:, SUBLANES:, :]], axis=1)
        else:
            back = head
        acc = acc + cw_ref[CONV_W - 1 - d:CONV_W - d, :] * back.reshape(S * L, C)
    tail_scr[...] = x.reshape(S, L, C)[:, L - SUBLANES:, :]
    return acc


def _softplus(x):
    return jnp.maximum(x, 0.0) + jnp.log1p(jnp.exp(-jnp.abs(x)))


def _rglru_kernel(*refs, S, L, fresh):
    x_ref, g_ref = refs[:2]
    buf_ref, h0_ref = (None, None) if fresh else refs[2:4]
    (cw_ref, cb_ref, wa_ref, ba_ref, wx_ref, bx_ref, lam_ref,
     y_ref, hN_ref, cN_ref, xs_scr) = refs[2 if fresh else 4:]
    R = S * L
    t = pl.program_id(1)

    @pl.when(t == 0)
    def _():
        xs_scr[...] = jnp.zeros_like(xs_scr)
        if not fresh:
            xs_scr[:, SUBLANES - (CONV_W - 1):SUBLANES, :] = buf_ref[...]
        hN_ref[...] = jnp.zeros_like(hN_ref) if fresh else h0_ref[...]

    x = x_ref[...]
    xc = _causal_conv(xs_scr, x, cw_ref, cb_ref, S, L, D_RG)
    xcb = xc.astype(bf16)
    r = jax.nn.sigmoid(jnp.dot(xcb, wa_ref[...], preferred_element_type=f32) + ba_ref[...])
    ig = jax.nn.sigmoid(jnp.dot(xcb, wx_ref[...], preferred_element_type=f32) + bx_ref[...])
    log_a = (-RG_C) * r * _softplus(-lam_ref[...])
    a = jnp.exp(log_a)
    th = jnp.tanh(log_a)
    u = jnp.sqrt(-2.0 * th / (1.0 - th)) * ig * xc

    n_grp, grp_per_seq = R // SUBLANES, L // SUBLANES
    a3 = a.reshape(n_grp, SUBLANES, D_RG)
    u3 = u.reshape(n_grp, SUBLANES, D_RG)
    sub = lax.broadcasted_iota(jnp.int32, (n_grp, SUBLANES, D_RG), 1)
    s = 1
    while s < SUBLANES:
        ok = sub >= s
        a_sh = pltpu.roll(a3, s, 1)
        u_sh = pltpu.roll(u3, s, 1)
        u3 = jnp.where(ok, a3 * u_sh + u3, u3)
        a3 = jnp.where(ok, a3 * a_sh, a3)
        s *= 2
    h0 = hN_ref[...]
    groups = []
    for kg in range(n_grp):
        carry = h0[kg // grp_per_seq] if kg % grp_per_seq == 0 else groups[-1][SUBLANES - 1:SUBLANES, :]
        groups.append(a3[kg] * carry + u3[kg])
    h = jnp.concatenate(groups, axis=0)

    y_ref[...] = h * jax.nn.gelu(g_ref[...], approximate=True)
    hN_ref[...] = h.reshape(S, L, D_RG)[:, L - 1:L, :]
    cN_ref[...] = xs_scr[:, SUBLANES - (CONV_W - 1):SUBLANES, :]


def _rglru(proj, row0, B, T, S, L, state, cw, cb, wa, ba, wx, bx, lam):
    R = S * L
    nb, nt = B // S, T // L
    blk0 = row0 // R
    row_map = lambda b, t: (blk0 + b * nt + t, 0)
    const2 = lambda b, t: (0, 0)
    state_specs = [] if state is None else [pl.BlockSpec((S, CONV_W - 1, D_RG), lambda b, t: (b, 0, 0)),
                                            pl.BlockSpec((S, 1, D_RG), lambda b, t: (b, 0, 0))]
    return pl.pallas_call(
        functools.partial(_rglru_kernel, S=S, L=L, fresh=state is None),
        grid=(nb, nt),
        in_specs=[
            pl.BlockSpec((R, D_RG), row_map),
            pl.BlockSpec((R, D_RG), lambda b, t: (blk0 + b * nt + t, 1)),
        ] + state_specs + [
            pl.BlockSpec((CONV_W, D_RG), const2),
            pl.BlockSpec((1, D_RG), const2),
            pl.BlockSpec((D_RG, D_RG), const2),
            pl.BlockSpec((1, D_RG), const2),
            pl.BlockSpec((D_RG, D_RG), const2),
            pl.BlockSpec((1, D_RG), const2),
            pl.BlockSpec((1, D_RG), const2),
        ],
        out_specs=[
            pl.BlockSpec((R, D_RG), lambda b, t: (b * nt + t, 0)),
            pl.BlockSpec((S, 1, D_RG), lambda b, t: (b, 0, 0)),
            pl.BlockSpec((S, CONV_W - 1, D_RG), lambda b, t: (b, 0, 0)),
        ],
        out_shape=[
            jax.ShapeDtypeStruct((B * T, D_RG), f32),
            jax.ShapeDtypeStruct((B, 1, D_RG), f32),
            jax.ShapeDtypeStruct((B, CONV_W - 1, D_RG), f32),
        ],
        scratch_shapes=[pltpu.VMEM((S, SUBLANES, D_RG), f32)],
        compiler_params=_cparams(2),
        name="rglru",
    )(proj, proj, *(state or ()), cw, cb, wa, ba, wx, bx, lam)


def _seg_scan(x, op, fill, tin, L, reverse=False):
    s = 1
    while s < L:
        if reverse:
            sh = pltpu.roll(x, LANES - s, 1)
            ok = tin < L - s
        else:
            sh = pltpu.roll(x, s, 1)
            ok = tin >= s
        x = op(x, jnp.where(ok, sh, fill))
        s *= 2
    return x


def _mlstm_kernel(*refs, S, L, G, n_riders, fresh):
    x_ref, z_ref, gt_ref = refs[:3]
    n_state = 0 if fresh else 4
    state_refs = refs[3:3 + n_state]
    n_in = 11 + n_state
    cw_ref, cb_ref, wqk_ref, wv_ref, wkT_ref, bif_ref, ng_ref, sk_ref = refs[3 + n_state:n_in]
    rider_in = refs[n_in:n_in + n_riders]
    y_ref, cN_ref, nN_ref, mN_ref, bN_ref = refs[n_in + n_riders:n_in + n_riders + 5]
    rider_out = refs[n_in + n_riders + 5:n_in + 2 * n_riders + 5]
    xs_scr, m_scr = refs[n_in + 2 * n_riders + 5:]
    for src, dst in zip(rider_in, rider_out):
        dst[...] = src[...].astype(bf16)

    R = S * L
    GS, GR = G * S, G * R
    j = pl.program_id(1)

    @pl.when(j == 0)
    def _():
        if fresh:
            xs_scr[...] = jnp.zeros_like(xs_scr)
            cN_ref[...] = jnp.zeros_like(cN_ref)
            nN_ref[...] = jnp.zeros_like(nN_ref)
            m_scr[...] = jnp.full(m_scr.shape, M_INIT, f32)
        else:
            buf_ref, c0_ref, n0_ref, m0_ref = state_refs
            xs_scr[...] = jnp.zeros_like(xs_scr)
            xs_scr[:, SUBLANES - (CONV_W - 1):SUBLANES, :] = buf_ref[...]
            cN_ref[...] = c0_ref[...]
            nN_ref[...] = n0_ref[...]
            m_scr[...] = m0_ref[...]

    x = x_ref[...].reshape(GR, D_M)
    xc = _causal_conv(xs_scr, x, cw_ref, cb_ref, GS, L, D_M)
    xa = xc * jax.nn.sigmoid(xc)
    xab = xa.astype(bf16)
    xb = x.astype(bf16)

    il_parts, f_parts = [], []
    for g in range(G):
        gT = (gt_ref[g] + bif_ref[...]).T
        il_parts.append(gT[0:SUBLANES, :])
        f_parts.append(pltpu.roll(gT[0:SUBLANES, :], M_HEADS, 0))
    il = jnp.concatenate(il_parts, axis=0)
    fl = -_softplus(-jnp.concatenate(f_parts, axis=0))
    tin = lax.broadcasted_iota(jnp.int32, (G * SUBLANES, LANES), 1) % L
    bcum = _seg_scan(fl, jnp.add, 0.0, tin, L)
    a = il - bcum
    m_prev = m_scr[...].reshape(G * SUBLANES, LANES)
    big_m = jnp.maximum(m_prev, _seg_scan(a, jnp.maximum, -jnp.inf, tin, L))
    m_t = bcum + big_m
    if S == 1:
        m_last = jnp.broadcast_to(big_m[:, LANES - 1:LANES], big_m.shape)
    else:
        m_last = _seg_scan(big_m, jnp.maximum, -jnp.inf, tin, L, reverse=True)
    rows = [big_m, jnp.exp(m_prev - big_m), jnp.exp(-m_t), jnp.exp(a - m_last), jnp.exp(m_prev - m_last)]
    mN_ref[...] = m_t.reshape(G, SUBLANES, LANES)
    m_scr[...] = jnp.broadcast_to(m_t[:, LANES - 1:LANES], m_t.shape).reshape(G, SUBLANES, LANES)
    pad_rows = jnp.zeros((LANES - len(rows) * SUBLANES, LANES), f32)
    cols = []
    for g in range(G):
        gs = slice(g * SUBLANES, (g + 1) * SUBLANES)
        cols.append(jnp.concatenate([r[gs] for r in rows] + [pad_rows], axis=0).T)

    def col(g, q, h):
        return cols[g][:, SUBLANES * q + h:SUBLANES * q + h + 1]

    qs, ks, vs, kTs = [], [], [], []
    for h in range(M_HEADS):
        hs = slice(h * M_DH, (h + 1) * M_DH)
        qk_h = jnp.dot(xab[:, hs], wqk_ref[h], preferred_element_type=f32)
        qs.append(qk_h[:, 0:M_DH])
        ks.append(qk_h[:, M_DH:2 * M_DH] * (M_DH ** -0.5))
        vs.append(jnp.dot(xb[:, hs], wv_ref[h], preferred_element_type=f32))
        if S == 1:
            kT_h = lax.dot_general(wkT_ref[h], xab[:, hs], (((1,), (1,)), ((), ())),
                                   preferred_element_type=f32)
            kTs.append((kT_h * (M_DH ** -0.5)).astype(bf16))

    ti = lax.broadcasted_iota(jnp.int32, (R, R), 0)
    si = lax.broadcasted_iota(jnp.int32, (R, R), 1)
    mask = (si <= ti) & ((ti // L) == (si // L))
    ones_b = jnp.ones((R, M_DH), bf16)
    pairs = [(g, h) for g in range(G) for h in range(M_HEADS)]
    blk = lambda g: slice(g * R, (g + 1) * R)
    seqs = [slice(b * L, (b + 1) * L) for b in range(S)]
    q = {(g, h): qs[h][blk(g)] for g, h in pairs}
    k = {(g, h): ks[h][blk(g)] for g, h in pairs}
    v = {(g, h): vs[h][blk(g)] for g, h in pairs}
    qb = {p: q[p].astype(bf16) for p in pairs}
    kb = {p: k[p].astype(bf16) for p in pairs}
    qk = {p: lax.dot_general(qb[p], kb[p], (((1,), (1,)), ((), ())), preferred_element_type=f32)
          for p in pairs}
    sm = {}
    for g, h in pairs:
        a_row = a[g * SUBLANES + h:g * SUBLANES + h + 1, :]
        decay = jnp.exp(jnp.where(mask, a_row - col(g, 0, h), -jnp.inf))
        sm[g, h] = (qk[g, h] * decay).astype(bf16)
    nd = {p: jnp.dot(sm[p], jnp.concatenate([v[p].astype(bf16), ones_b], axis=1),
                     preferred_element_type=f32) for p in pairs}
    c_old = {(g, h, b): cN_ref[g * S + b, h] for g, h in pairs for b in range(S)}
    n_old = {(g, h, b): nN_ref[g * S + b, h] for g, h in pairs for b in range(S)}
    q_c = {(g, h, b): jnp.dot(qb[g, h][seqs[b]], c_old[g, h, b].astype(bf16), preferred_element_type=f32)
           for g, h in pairs for b in range(S)}
    hh = {}
    for g, h in pairs:
        g_col, e_col = col(g, 1, h), col(g, 2, h)
        pieces = []
        for b, rs in enumerate(seqs):
            q_n = jnp.sum(q[g, h][rs] * n_old[g, h, b], axis=1, keepdims=True)
            num = nd[g, h][rs, 0:M_DH] + g_col[rs] * q_c[g, h, b]
            den = nd[g, h][rs, M_DH:2 * M_DH] + g_col[rs] * q_n
            pieces.append(num / jnp.maximum(jnp.abs(den), e_col[rs]))
        hh[g, h] = pieces[0] if S == 1 else jnp.concatenate(pieces, axis=0)
    mu = {p: jnp.mean(hh[p], axis=1, keepdims=True) for p in pairs}
    var = {p: jnp.mean(jnp.square(hh[p] - mu[p]), axis=1, keepdims=True) for p in pairs}
    hn_blocks = [jnp.concatenate([(hh[g, h] - mu[g, h]) * lax.rsqrt(var[g, h] + LN_EPS)
                                  for h in range(M_HEADS)], axis=1) for g in range(G)]
    new_c, new_n = [], []
    for g, h in pairs:
        w_col = col(g, 3, h)
        wv = (w_col * v[g, h]).astype(bf16)
        wk = w_col * k[g, h]
        for b, rs in enumerate(seqs):
            g_end = col(g, 4, h)[(b + 1) * L - 1:(b + 1) * L, :]
            if S == 1:
                kv = jnp.dot(kTs[h][:, blk(g)], wv, preferred_element_type=f32)
            else:
                kv = lax.dot_general(kb[g, h][rs], wv[rs], (((0,), (0,)), ((), ())),
                                     preferred_element_type=f32)
            new_c.append((g * S + b, h, g_end * c_old[g, h, b] + kv))
            new_n.append((g * S + b, h, g_end * n_old[g, h, b] + jnp.sum(wk[rs], axis=0, keepdims=True)))

    hn = jnp.concatenate(hn_blocks, axis=0) * ng_ref[...]
    y = jax.nn.sigmoid(z_ref[...].reshape(GR, D_M)) * (hn + sk_ref[...] * xa)
    y_ref[...] = y.reshape(G, R, D_M)
    for sq, h, val in new_c:
        cN_ref[sq, h] = val
    for sq, h, val in new_n:
        nN_ref[sq, h] = val
    bN_ref[...] = xs_scr[:, SUBLANES - (CONV_W - 1):SUBLANES, :]


def _mlstm(proj, B, T, S, L, G, state, cw, cb, wqk, wv, wkT, bif, ng, sk, riders=()):
    R = S * L
    assert R == M_ROWS == LANES
    nblk, nc = B // S, T // L
    nb = nblk // G
    assert nblk * S == B and nc * L == T and nb * G == nblk and (S == 1 or nc == 1)
    steps = nb * nc
    assert all(r.shape[0] % steps == 0 for r in riders)
    rider_specs = [pl.BlockSpec((r.shape[0] // steps,) + r.shape[1:], lambda b, j: (b * nc + j, 0, 0))
                   for r in riders]
    proj4 = proj.reshape(nblk, nc, R, N_PROJ)
    const2 = lambda b, j: (0, 0)
    const3 = lambda b, j: (0, 0, 0)
    gate_blk = (2 * D_RG + 2 * D_M) // LANES
    GS = G * S
    state_args, state_specs = (), []
    if state is not None:
        buf8, c0, n0, m0_lanes = state
        state_args = (buf8, c0, n0, m0_lanes.reshape(SUBLANES, nblk, LANES).transpose(1, 0, 2))
        state_specs = [
            pl.BlockSpec((GS, CONV_W - 1, D_M), lambda b, j: (b, 0, 0)),
            pl.BlockSpec((GS, M_HEADS, M_DH, M_DH), lambda b, j: (b, 0, 0, 0)),
            pl.BlockSpec((GS, M_HEADS, 1, M_DH), lambda b, j: (b, 0, 0, 0)),
            pl.BlockSpec((G, SUBLANES, LANES), lambda b, j: (b, 0, 0)),
        ]
    outs = pl.pallas_call(
        functools.partial(_mlstm_kernel, S=S, L=L, G=G, n_riders=len(riders), fresh=state is None),
        grid=(nb, nc),
        in_specs=[
            pl.BlockSpec((G, None, R, D_M), lambda b, j: (b, j, 0, 2)),
            pl.BlockSpec((G, None, R, D_M), lambda b, j: (b, j, 0, 3)),
            pl.BlockSpec((G, None, R, LANES), lambda b, j: (b, j, 0, gate_blk)),
        ] + state_specs + [
            pl.BlockSpec((CONV_W, D_M), const2),
            pl.BlockSpec((1, D_M), const2),
            pl.BlockSpec((M_HEADS, M_DH, 2 * M_DH), const3),
            pl.BlockSpec((M_HEADS, M_DH, M_DH), const3),
            pl.BlockSpec((M_HEADS, M_DH, M_DH), const3),
            pl.BlockSpec((1, LANES), const2),
            pl.BlockSpec((1, D_M), const2),
            pl.BlockSpec((1, D_M), const2),
        ] + rider_specs,
        out_specs=[
            pl.BlockSpec((G, None, R, D_M), lambda b, j: (b, j, 0, 0)),
            pl.BlockSpec((GS, M_HEADS, M_DH, M_DH), lambda b, j: (b, 0, 0, 0)),
            pl.BlockSpec((GS, M_HEADS, 1, M_DH), lambda b, j: (b, 0, 0, 0)),
            pl.BlockSpec((G, None, SUBLANES, LANES), lambda b, j: (b, j, 0, 0)),
            pl.BlockSpec((GS, CONV_W - 1, D_M), lambda b, j: (b, 0, 0)),
        ] + rider_specs,
        out_shape=[
            jax.ShapeDtypeStruct((nblk, nc, R, D_M), f32),
            jax.ShapeDtypeStruct((B, M_HEADS, M_DH, M_DH), f32),
            jax.ShapeDtypeStruct((B, M_HEADS, 1, M_DH), f32),
            jax.ShapeDtypeStruct((nblk, nc, SUBLANES, LANES), f32),
            jax.ShapeDtypeStruct((B, CONV_W - 1, D_M), f32),
        ] + [jax.ShapeDtypeStruct(r.shape, bf16) for r in riders],
        scratch_shapes=[pltpu.VMEM((GS, SUBLANES, D_M), f32), pltpu.VMEM((G, SUBLANES, LANES), f32)],
        compiler_params=_cparams(2),
        name="mlstm",
    )(proj4, proj4, proj4, *state_args, cw, cb, wqk, wv, wkT, bif, ng, sk, *riders)
    y, c_new, n_new, m_t, b_new = outs[:5]
    m_last = m_t[:, nc - 1, :M_HEADS, :].reshape(nblk, M_HEADS, S, L)[:, :, :, L - 1]
    m_last = m_last.transpose(0, 2, 1).reshape(B, M_HEADS)
    return (y.reshape(B * T, D_M), c_new, n_new, m_last, b_new) + tuple(outs[5:])


def _layer_norm(x, g, b):
    mu = jnp.mean(x, axis=-1, keepdims=True)
    var = jnp.mean(jnp.square(x - mu), axis=-1, keepdims=True)
    return (x - mu) * lax.rsqrt(var + LN_EPS) * g + b


def _first_lane_of_max(vals, lane_f):
    vmax = jnp.max(vals, axis=1, keepdims=True)
    idx = jnp.min(jnp.where(vals == vmax, lane_f, float(LANES)), axis=1, keepdims=True)
    return vmax, idx


def _outproj_kernel(xp_ref, xs_ref, rgp_ref, rgs_ref, mp_ref, ms_ref, wo_ref, g1_ref, b1_ref,
                    wr_ref, br_ref, x1_ref, cnt_ref, *, n_p):
    i = pl.program_id(0)

    @pl.when(i == 0)
    def _():
        cnt_ref[...] = jnp.zeros_like(cnt_ref)

    def route(lg, lane, lane_f):
        neg = -jnp.inf
        gl = jnp.where(lane < N_GROUPS, lg, neg)
        gmax, gidx = _first_lane_of_max(gl, lane_f)
        p_g = 1.0 / jnp.sum(jnp.exp(gl - gmax), axis=1, keepdims=True)
        e_lo = float(N_GROUPS) + float(EXPERTS_PER_GROUP) * gidx
        el = jnp.where((lane_f >= e_lo) & (lane_f < e_lo + float(EXPERTS_PER_GROUP)), lg, neg)
        v1, i1 = _first_lane_of_max(el, lane_f)
        v2, i2 = _first_lane_of_max(jnp.where(lane_f == i1, neg, el), lane_f)
        d = jnp.exp(v2 - v1)
        w1 = p_g / (1.0 + d)
        w2 = p_g * d / (1.0 + d)
        first_low = i1 < i2
        j_lo = jnp.minimum(i1, i2) - e_lo
        j_hi = jnp.maximum(i1, i2) - e_lo
        n_first = float(EXPERTS_PER_GROUP - 1) * j_lo - 0.5 * j_lo * (j_lo - 1.0)
        cls = float(PAIRS_PER_GROUP) * gidx + n_first + (j_hi - j_lo - 1.0)
        return cls, jnp.where(first_low, w1, w2), jnp.where(first_low, w2, w1)

    def run(x_ref, rg_ref, m_ref):
        rp = TM_OUT // OUT_PARTS
        parts = [slice(p * rp, (p + 1) * rp) for p in range(OUT_PARTS)]
        lane = lax.broadcasted_iota(jnp.int32, (rp, LANES), 1)
        lane_f = lane.astype(f32)
        heads = [jnp.concatenate([rg_ref[r, :].astype(bf16), m_ref[r, :].astype(bf16)], axis=1) for r in parts]
        mix = [jnp.dot(h, wo_ref[...], preferred_element_type=f32) for h in heads]
        x1 = [_layer_norm(ALPHA * x_ref[r, :] + mx, g1_ref[...], b1_ref[...]) for r, mx in zip(parts, mix)]
        for r, v in zip(parts, x1):
            x1_ref[r, 0:D_MODEL] = v

        hi = [v.astype(bf16) for v in x1]
        lo = [(v - h.astype(f32)).astype(bf16) for v, h in zip(x1, hi)]
        hi_terms = [jnp.dot(h, wr_ref[...], preferred_element_type=f32) for h in hi]
        lo_term = [jnp.dot(v, wr_ref[:, 0:LANES], preferred_element_type=f32) for v in lo]
        lg = [ht[:, 0:LANES] + ht[:, LANES:2 * LANES] + lt + br_ref[...] for ht, lt in zip(hi_terms, lo_term)]
        routed = [route(v, lane, lane_f) for v in lg]

        ti = lax.broadcasted_iota(jnp.int32, (rp, rp), 0)
        si = lax.broadcasted_iota(jnp.int32, (rp, rp), 1)
        tri = (si <= ti).astype(bf16)
        hot = [lane_f == cls for cls, _, _ in routed]
        cum = [jnp.dot(tri, h.astype(bf16), preferred_element_type=f32) for h in hot]
        seen = cnt_ref[...]
        for r, h, cm, (cls, w_lo, w_hi) in zip(parts, hot, cum, routed):
            rank = jnp.sum(jnp.where(h, cm - 1.0 + seen, 0.0), axis=1, keepdims=True)
            seen = seen + cm[rp - 1:rp, :]
            info = jnp.zeros((rp, LANES), f32)
            for c, val in enumerate((cls, rank, w_lo, w_hi)):
                info = jnp.where(lane == c, val, info)
            x1_ref[r, D_MODEL:D_MODEL + LANES] = info
        cnt_ref[...] = seen

    @pl.when(i < n_p)
    def _():
        run(xp_ref, rgp_ref, mp_ref)

    @pl.when(i >= n_p)
    def _():
        run(xs_ref, rgs_ref, ms_ref)


def _outproj(xp, xs, rgp, rgs, mp, ms, wo, g1, b1, wr, br):
    n_p, n_s = xp.shape[0] // TM_OUT, xs.shape[0] // TM_OUT
    n = n_p + n_s
    pmap = lambda i: (jnp.minimum(i, n_p - 1), 0)
    smap = lambda i: (jnp.maximum(i - n_p, 0), 0)
    const = lambda i: (0, 0)
    return pl.pallas_call(
        functools.partial(_outproj_kernel, n_p=n_p),
        grid=(n,),
        in_specs=[
            pl.BlockSpec((TM_OUT, D_MODEL), pmap), pl.BlockSpec((TM_OUT, D_MODEL), smap),
            pl.BlockSpec((TM_OUT, D_RG), pmap), pl.BlockSpec((TM_OUT, D_RG), smap),
            pl.BlockSpec((TM_OUT, D_M), pmap), pl.BlockSpec((TM_OUT, D_M), smap),
            pl.BlockSpec((D_RG + D_M, D_MODEL), const),
            pl.BlockSpec((1, D_MODEL), const), pl.BlockSpec((1, D_MODEL), const),
            pl.BlockSpec((D_MODEL, 2 * LANES), const),
            pl.BlockSpec((1, LANES), const),
        ],
        out_specs=[
            pl.BlockSpec((TM_OUT, X_ROW), lambda i: (i, 0)),
            pl.BlockSpec((1, LANES), const),
        ],
        out_shape=[
            jax.ShapeDtypeStruct((n * TM_OUT, X_ROW), f32),
            jax.ShapeDtypeStruct((1, LANES), f32),
        ],
        compiler_params=_cparams(1),
        name="outproj",
    )(xp, xs, rgp, rgs, mp, ms, wo, g1, b1, wr, br)


GRP_MOVE = TM_MOVE // SUBLANES


def _grouped(x):
    return x.reshape(x.shape[0] // SUBLANES, SUBLANES, x.shape[1])


def _tile_rows_wait(hbm_ref, sem):
    rows = hbm_ref.at[pl.ds(0, TM_MOVE)]
    pltpu.make_async_copy(rows, rows, sem).wait()


def _for_each_tile_row(body):
    def group(k, c):
        for j in range(SUBLANES):
            body(k, j, k * SUBLANES + j)
        return c

    lax.fori_loop(0, GRP_MOVE, group, 0)


def _dispatch_kernel(pad_ref, pos_ref, x1_ref, xs_ref, zero_scr, xbuf, sem, zsem, tsem, lsem):
    i = pl.program_id(0)
    n = pl.num_programs(0)
    start, wait = (lambda cp: cp.start()), (lambda cp: cp.wait())

    def load(t, s):
        return pltpu.make_async_copy(x1_ref.at[pl.ds(t * GRP_MOVE, GRP_MOVE)], xbuf.at[s], lsem.at[s])

    def zero_chunk(row0):
        return pltpu.make_async_copy(zero_scr.at[pl.ds(0, SUBLANES)],
                                     xs_ref.at[pl.ds(pl.multiple_of(row0, SUBLANES), SUBLANES)], zsem)

    def tail_tiles(go):
        def zero_tile(t, c):
            rows = pl.ds(pl.multiple_of(t * TM_MOE, TM_MOE), TM_MOE)
            go(pltpu.make_async_copy(zero_scr, xs_ref.at[rows], tsem))
            return c

        lax.fori_loop(pad_ref[2 * N_CLASSES], xs_ref.shape[0] // TM_MOE, zero_tile, 0)

    @pl.when(i == 0)
    def _():
        load(0, 0).start()
        zero_scr[...] = jnp.zeros_like(zero_scr)
        tail_tiles(start)
        for go in (start, wait):
            def per_class(cl, c):
                row0 = pad_ref[2 * cl]
                lax.fori_loop(0, pad_ref[2 * cl + 1],
                              lambda k, c2: (go(zero_chunk(row0 + k * SUBLANES)), c2)[1], 0)
                return c

            lax.fori_loop(0, N_CLASSES, per_class, 0)

    @pl.when(i == n - 1)
    def _():
        tail_tiles(wait)

    @pl.when(i + 1 < n)
    def _():
        load(i + 1, (i + 1) % 3).start()

    slot = i % 3
    load(i, slot).wait()

    def start(k, j, r):
        pltpu.make_async_copy(xbuf.at[slot, k, pl.ds(j, 1)], xs_ref.at[pl.ds(pos_ref[0, 0, r], 1)],
                              sem.at[slot]).start()

    _for_each_tile_row(start)

    @pl.when(i > 0)
    def _():
        _tile_rows_wait(xs_ref, sem.at[(i + 2) % 3])

    @pl.when(i == n - 1)
    def _():
        _tile_rows_wait(xs_ref, sem.at[slot])


def _dispatch(pad_plan, pos, x1, n_rows):
    n = x1.shape[0] // TM_MOVE
    return pl.pallas_call(
        _dispatch_kernel,
        grid_spec=pltpu.PrefetchScalarGridSpec(
            num_scalar_prefetch=1,
            grid=(n,),
            in_specs=[
                pl.BlockSpec((1, 1, TM_MOVE), lambda i, *_: (i, 0, 0), memory_space=pltpu.SMEM),
                pl.BlockSpec(memory_space=pl.ANY),
            ],
            out_specs=pl.BlockSpec(memory_space=pl.ANY),
            scratch_shapes=[pltpu.VMEM((TM_MOE, X_ROW), f32),
                            pltpu.VMEM((3, GRP_MOVE, SUBLANES, X_ROW), f32),
                            pltpu.SemaphoreType.DMA((3,)), pltpu.SemaphoreType.DMA(()),
                            pltpu.SemaphoreType.DMA(()), pltpu.SemaphoreType.DMA((3,))],
        ),
        out_shape=jax.ShapeDtypeStruct((n_rows, X_ROW), f32),
        compiler_params=_cparams(1),
        name="dispatch",
    )(pad_plan, pos, _grouped(x1))


def _expert_kernel(ta_ref, tb_ref, ca_ref, cb_ref, sa_ref, sb_ref, nu_ref,
                   x_ref, wg_hbm, wu_hbm, wd_hbm, g2_ref, b2_ref, y_ref,
                   wga, wua, wda, wgb, wub, wdb, sem_a, sem_b):
    i = pl.program_id(0)
    n_used = nu_ref[0]
    first = i * MOE_TILES
    side_a = (ta_ref, ca_ref, sa_ref, (wga, wua, wda), sem_a)
    side_b = (tb_ref, cb_ref, sb_ref, (wgb, wub, wdb), sem_b)

    def copies(side, j):
        t_ref, _, s_ref, bufs, sem = side
        e, s = t_ref[j], s_ref[j]
        return [pltpu.make_async_copy(w.at[e], buf.at[s], sem.at[s])
                for w, buf in zip((wg_hbm, wu_hbm, wd_hbm), bufs)]

    def fetch(j):
        @pl.when(j < n_used)
        def _():
            for side in (side_a, side_b):
                @pl.when(side[1][j] == 1)
                def _():
                    for cp in copies(side, j):
                        cp.start()

    @pl.when(i == 0)
    def _():
        for j in range(W_AHEAD):
            fetch(j)

    for t in range(MOE_TILES):
        fetch(first + W_AHEAD + t)

    @pl.when(first < n_used)
    def _():
        for t in range(MOE_TILES):
            for side in (side_a, side_b):
                @pl.when(side[1][first + t] == 1)
                def _():
                    for cp in copies(side, first + t):
                        cp.wait()
        rows = [slice(t * TM_MOE, (t + 1) * TM_MOE) for t in range(MOE_TILES)]
        x = [x_ref[r, 0:D_MODEL] for r in rows]
        info = [x_ref[r, D_MODEL:X_ROW] for r in rows]
        xb = [v.astype(bf16) for v in x]
        units = [(t, bufs, s_ref[first + t]) for t in range(MOE_TILES)
                 for bufs, s_ref in (((wga, wua, wda), sa_ref), ((wgb, wub, wdb), sb_ref))]
        hg = [jnp.dot(xb[t], bufs[0][s], preferred_element_type=f32) for t, bufs, s in units]
        hu = [jnp.dot(xb[t], bufs[1][s], preferred_element_type=f32) for t, bufs, s in units]
        mid = [(g * jax.nn.sigmoid(g) * u).astype(bf16) for g, u in zip(hg, hu)]
        ys = [jnp.dot(m, bufs[2][s], preferred_element_type=f32)
              for m, (t, bufs, s) in zip(mid, units)]
        for t in range(MOE_TILES):
            ffn = info[t][:, 2:3] * ys[2 * t] + info[t][:, 3:4] * ys[2 * t + 1]
            y_ref[rows[t], :] = _layer_norm(ALPHA * x[t] + ffn, g2_ref[...], b2_ref[...])

    @pl.when(first >= n_used)
    def _():
        y_ref[...] = jnp.zeros_like(y_ref)


def _experts(tile_ea, tile_eb, n_used, xs, wg, wu, wd, g2, b2):
    nt = xs.shape[0] // TM_MOE

    def ring_plan(tile_e):
        opens = jnp.concatenate([jnp.ones((1,), jnp.int32), (tile_e[1:] != tile_e[:-1]).astype(jnp.int32)])
        return opens, (jnp.cumsum(opens) - 1) % W_SLOTS

    open_a, slot_a = ring_plan(tile_ea)
    open_b, slot_b = ring_plan(tile_eb)
    assert nt % MOE_TILES == 0
    n_used = ((n_used + MOE_TILES - 1) // MOE_TILES) * MOE_TILES
    const = lambda i, *_: (0, 0)
    xmap = lambda i, ta, tb, ca, cb, sa, sb, nu: (jnp.minimum(i, nu[0] // MOE_TILES - 1), 0)
    w_in = pltpu.VMEM((W_SLOTS, D_MODEL, D_EXPERT), bf16)
    w_out = pltpu.VMEM((W_SLOTS, D_EXPERT, D_MODEL), bf16)
    return pl.pallas_call(
        _expert_kernel,
        grid_spec=pltpu.PrefetchScalarGridSpec(
            num_scalar_prefetch=7,
            grid=(nt // MOE_TILES,),
            in_specs=[
                pl.BlockSpec((MOE_TILES * TM_MOE, X_ROW), xmap),
                pl.BlockSpec(memory_space=pl.ANY), pl.BlockSpec(memory_space=pl.ANY),
                pl.BlockSpec(memory_space=pl.ANY),
                pl.BlockSpec((1, D_MODEL), const), pl.BlockSpec((1, D_MODEL), const),
            ],
            out_specs=pl.BlockSpec((MOE_TILES * TM_MOE, D_MODEL), lambda i, *_: (i, 0)),
            scratch_shapes=[w_in, w_in, w_out, w_in, w_in, w_out,
                            pltpu.SemaphoreType.DMA((W_SLOTS,)), pltpu.SemaphoreType.DMA((W_SLOTS,))],
        ),
        out_shape=jax.ShapeDtypeStruct((nt * TM_MOE, D_MODEL), f32),
        compiler_params=_cparams(1),
        name="experts",
    )(tile_ea, tile_eb, open_a, open_b, slot_a.astype(jnp.int32), slot_b.astype(jnp.int32), n_used,
      xs, wg, wu, wd, g2, b2)


def _collect_kernel(pos_ref, posn_ref, ys_ref, op_ref, os_ref, ybuf, sem, *, n_p):
    i = pl.program_id(0)
    n = pl.num_programs(0)
    slot = i % 2

    def gather(p_ref, s):
        def start(k, j, r):
            pltpu.make_async_copy(ys_ref.at[pl.ds(p_ref[0, 0, r], 1)], ybuf.at[s, k, pl.ds(j, 1)],
                                  sem.at[s]).start()

        _for_each_tile_row(start)

    @pl.when(i == 0)
    def _():
        gather(pos_ref, 0)

    @pl.when(i + 1 < n)
    def _():
        gather(posn_ref, 1 - slot)

    _tile_rows_wait(ys_ref, sem.at[slot])

    @pl.when(i < n_p)
    def _():
        op_ref[...] = ybuf[slot].reshape(TM_MOVE, D_MODEL)

    @pl.when(i >= n_p)
    def _():
        os_ref[...] = ybuf[slot].reshape(TM_MOVE, D_MODEL)


def _collect(pos, ys, n_p):
    n = pos.shape[0]
    n_s = n - n_p
    return pl.pallas_call(
        functools.partial(_collect_kernel, n_p=n_p),
        grid=(n,),
        in_specs=[
            pl.BlockSpec((1, 1, TM_MOVE), lambda i: (i, 0, 0), memory_space=pltpu.SMEM),
            pl.BlockSpec((1, 1, TM_MOVE), lambda i: (jnp.minimum(i + 1, n - 1), 0, 0),
                         memory_space=pltpu.SMEM),
            pl.BlockSpec(memory_space=pl.ANY),
        ],
        out_specs=[
            pl.BlockSpec((TM_MOVE, D_MODEL), lambda i: (jnp.minimum(i, n_p - 1), 0)),
            pl.BlockSpec((TM_MOVE, D_MODEL), lambda i: (jnp.maximum(i - n_p, 0), 0)),
        ],
        out_shape=[
            jax.ShapeDtypeStruct((n_p * TM_MOVE, D_MODEL), f32),
            jax.ShapeDtypeStruct((n_s * TM_MOVE, D_MODEL), f32),
        ],
        scratch_shapes=[pltpu.VMEM((2, GRP_MOVE, SUBLANES, D_MODEL), f32), pltpu.SemaphoreType.DMA((2,))],
        compiler_params=_cparams(1),
        name="collect",
    )(pos, pos, ys)


def _block_diag(w):
    n, d, _ = w.shape
    eye = jnp.eye(n, dtype=w.dtype)
    return (eye[:, None, :, None] * w[:, :, None, :]).reshape(n * d, n * d)


def _pair_walk(n):
    total = n * (n - 1) // 2

    def extend(path, used):
        if len(path) == total:
            return path
        a, b = path[-1]
        for c in range(n):
            for nxt in ((a, c), (c, b)):
                key = frozenset(nxt)
                if len(key) == 2 and key not in used:
                    out = extend(path + [nxt], used | {key})
                    if out:
                        return out
        return None

    return extend([(0, 1)], {frozenset((0, 1))})


def kernel(x_prompt, x_sample, state_rg_h, state_rg_conv, state_m_C, state_m_n, state_m_m, state_m_conv, w_in, rg_conv_w, rg_conv_b, rg_w_a, rg_b_a, rg_w_x, rg_b_x, rg_lambda, m_conv_w, m_conv_b, m_w_q, m_w_k, m_w_v, m_b_i, m_b_f, m_norm_g, m_skip, w_out, ln1_g, ln1_b, ln2_g, ln2_b, moe_w_group, moe_b_group, moe_w_expert, moe_b_expert, moe_w_gate, moe_w_up, moe_w_down):
    BP, TP, _ = x_prompt.shape
    BS, TS, _ = x_sample.shape
    n_prompt, n_sample = BP * TP, BS * TS
    xp = x_prompt.reshape(n_prompt, D_MODEL)
    xs = x_sample.reshape(n_sample, D_MODEL)
    l = 0

    w_in_p = jnp.pad(w_in[l], ((0, 0), (0, N_PROJ - w_in.shape[-1]))).astype(bf16)
    wa = _block_diag(rg_w_a[l]).astype(bf16)
    wx = _block_diag(rg_w_x[l]).astype(bf16)
    row = lambda v: v.reshape(1, -1)
    bif = jnp.pad(jnp.concatenate([m_b_i[l], m_b_f[l]]), (0, LANES - 2 * M_HEADS)).reshape(1, LANES)
    wqk = jnp.concatenate([m_w_q[l], m_w_k[l]], axis=-1).astype(bf16)
    wv = m_w_v[l].astype(bf16)
    wkT = m_w_k[l].transpose(0, 2, 1).astype(bf16)
    w_route = jnp.pad(jnp.concatenate([moe_w_group[l], moe_w_expert[l]], axis=1),
                      ((0, 0), (0, LANES - N_GROUPS - N_EXPERTS)))
    wr_hi = w_route.astype(bf16)
    wr_lo = (w_route - wr_hi.astype(f32)).astype(bf16)
    wr = jnp.concatenate([wr_hi, wr_lo], axis=1)
    b_route = jnp.pad(jnp.concatenate([moe_b_group[l], moe_b_expert[l]]),
                      (0, LANES - N_GROUPS - N_EXPERTS)).reshape(1, LANES)

    proj_p, proj_s = _inproj(xp, xs, w_in_p)

    rg_args = (rg_conv_w[l], row(rg_conv_b[l]), wa, row(rg_b_a[l]), wx, row(rg_b_x[l]), row(rg_lambda[l]))
    s_blk = M_ROWS // TS
    yrg_p, p_rg_h, p_rg_conv = _rglru(proj_p, 0, BP, TP, 1, RG_ROWS, None, *rg_args)
    rg_state = (state_rg_conv[l], state_rg_h[l].reshape(BS, 1, D_RG))
    yrg_s, s_rg_h, s_rg_conv = _rglru(proj_s, 0, BS, TS, RG_ROWS // TS, TS, rg_state, *rg_args)

    m_args = (m_conv_w[l], row(m_conv_b[l]), wqk, wv, wkT, bif, row(m_norm_g[l]), row(m_skip[l]))
    ym_p, p_m_C, p_m_n, p_m_m, p_m_conv, wg_b, wu_b, wd_b = _mlstm(
        proj_p, BP, TP, 1, M_ROWS, M_BLOCKS_PROMPT, None, *m_args,
        riders=(moe_w_gate[l], moe_w_up[l], moe_w_down[l]))
    m0_s = jnp.pad(jnp.repeat(state_m_m[l].T, TS, axis=1), ((0, SUBLANES - M_HEADS), (0, 0)))
    m_state = (state_m_conv[l], state_m_C[l], state_m_n[l].reshape(BS, M_HEADS, 1, M_DH), m0_s)
    ym_s, s_m_C, s_m_n, s_m_m, s_m_conv = _mlstm(proj_s, BS, TS, s_blk, TS, M_BLOCKS_SAMPLE, m_state, *m_args)

    x1, cnt = _outproj(xp, xs, yrg_p, yrg_s, ym_p, ym_s, w_out[l].astype(bf16),
                       row(ln1_g[l]), row(ln1_b[l]), wr, b_route)

    n_tok = n_prompt + n_sample
    n_tiles = n_tok // TM_MOE + N_CLASSES
    lay_cls, lay_a, lay_b = [], [], []
    for g in range(N_GROUPS):
        for ja, jb in _pair_walk(EXPERTS_PER_GROUP):
            assert ja < jb
            n_first = (EXPERTS_PER_GROUP - 1) * ja - ja * (ja - 1) // 2
            lay_cls.append(PAIRS_PER_GROUP * g + n_first + jb - ja - 1)
            lay_a.append(g * EXPERTS_PER_GROUP + ja)
            lay_b.append(g * EXPERTS_PER_GROUP + jb)
    classes = jnp.arange(N_CLASSES, dtype=jnp.int32)
    in_slot = jnp.array(lay_cls, jnp.int32)[:, None] == classes
    counts = jnp.sum(jnp.where(in_slot, cnt[0, :N_CLASSES].astype(jnp.int32), 0), axis=1)
    padded = ((counts + TM_MOE - 1) // TM_MOE) * TM_MOE
    ends = jnp.cumsum(padded)
    offs = ends - padded
    offs_of_cls = jnp.sum(jnp.where(in_slot, offs[:, None], 0), axis=0)
    cls = x1[:, D_MODEL].astype(jnp.int32)
    rank = x1[:, D_MODEL + 1].astype(jnp.int32)
    pos = jnp.sum(jnp.where(cls[:, None] == classes, offs_of_cls, 0), axis=-1) + rank
    pos = pos.reshape(n_tok // TM_MOVE, 1, TM_MOVE)
    tiles = jnp.arange(n_tiles, dtype=jnp.int32)
    tile_slot = jnp.minimum(jnp.sum(tiles[:, None] >= (ends // TM_MOE)[None, :], axis=1), N_CLASSES - 1)
    on_slot = tile_slot[:, None] == classes
    tile_ea = jnp.sum(jnp.where(on_slot, jnp.array(lay_a, jnp.int32), 0), axis=-1).astype(jnp.int32)
    tile_eb = jnp.sum(jnp.where(on_slot, jnp.array(lay_b, jnp.int32), 0), axis=-1).astype(jnp.int32)
    n_used = (ends[-1] // TM_MOE).reshape(1).astype(jnp.int32)
    pad_row0 = ((offs + counts) // SUBLANES) * SUBLANES
    pad_plan = jnp.concatenate([jnp.stack([pad_row0, (ends - pad_row0) // SUBLANES], axis=1).reshape(-1),
                                     n_used]).astype(jnp.int32)

    x_sorted = _dispatch(pad_plan, pos, x1, n_tiles * TM_MOE)
    y_sorted = _experts(tile_ea, tile_eb, n_used, x_sorted, wg_b, wu_b, wd_b,
                        row(ln2_g[l]), row(ln2_b[l]))
    y_p, y_s = _collect(pos, y_sorted, n_prompt // TM_MOVE)

    return (y_p.reshape(BP, TP, D_MODEL), y_s.reshape(BS, TS, D_MODEL),
            p_rg_h.reshape(1, BP, D_RG), p_rg_conv[None], p_m_C[None], p_m_n.reshape(1, BP, M_HEADS, M_DH),
            p_m_m[None], p_m_conv[None],
            s_rg_h.reshape(1, BS, D_RG), s_rg_conv[None], s_m_C[None], s_m_n.reshape(1, BS, M_HEADS, M_DH),
            s_m_m[None], s_m_conv[None])
```

```python
import functools

import jax
import jax.numpy as jnp
from jax import lax
from jax.experimental import pallas as pl
from jax.experimental.pallas import tpu as pltpu

f32 = jnp.float32
bf16 = jnp.bfloat16

D_MODEL = 1024
D_RG = 512
RG_C = 8.0
D_M = 512
M_HEADS = 4
M_DH = 128
CONV_W = 4
N_GROUPS = 4
EXPERTS_PER_GROUP = 8
N_EXPERTS = 32
D_EXPERT = 256
ALPHA = 2.0 ** 0.25
LN_EPS = 1e-5
M_INIT = -1.0e4

LANES = 128
SUBLANES = 8
TM = 512
TM_MOVE = 1024
TM_OUT = 1024
OUT_PARTS = 8
TM_MOE = 128
MOE_TILES = 4
W_AHEAD = 8
W_SLOTS = W_AHEAD + MOE_TILES
PAIRS_PER_GROUP = EXPERTS_PER_GROUP * (EXPERTS_PER_GROUP - 1) // 2
N_CLASSES = N_GROUPS * PAIRS_PER_GROUP
X_ROW = D_MODEL + LANES
N_PROJ = 2 * D_RG + 2 * D_M + LANES
RG_ROWS = 1024
M_ROWS = 128
M_BLOCKS_PROMPT = 8
M_BLOCKS_SAMPLE = 2
V7X_VMEM_BYTES = 64 * 1024 * 1024
VMEM_LIMIT = V7X_VMEM_BYTES - 8 * 1024 * 1024


def _cparams(n_axes):
    return pltpu.CompilerParams(dimension_semantics=("arbitrary",) * n_axes,
                                vmem_limit_bytes=VMEM_LIMIT)


def _inproj_kernel(xp_ref, xs_ref, w_ref, op_ref, os_ref, *, n_p):
    i = pl.program_id(0)

    def run(x_ref, o_ref):
        o_ref[...] = jnp.dot(x_ref[...].astype(bf16), w_ref[...], preferred_element_type=f32)

    @pl.when(i < n_p)
    def _():
        run(xp_ref, op_ref)

    @pl.when(i >= n_p)
    def _():
        run(xs_ref, os_ref)


def _inproj(xp, xs, w):
    n_p, n_s = xp.shape[0] // TM, xs.shape[0] // TM
    pmap = lambda i: (jnp.minimum(i, n_p - 1), 0)
    smap = lambda i: (jnp.maximum(i - n_p, 0), 0)
    return pl.pallas_call(
        functools.partial(_inproj_kernel, n_p=n_p),
        grid=(n_p + n_s,),
        in_specs=[
            pl.BlockSpec((TM, D_MODEL), pmap),
            pl.BlockSpec((TM, D_MODEL), smap),
            pl.BlockSpec((D_MODEL, N_PROJ), lambda i: (0, 0)),
        ],
        out_specs=[pl.BlockSpec((TM, N_PROJ), pmap), pl.BlockSpec((TM, N_PROJ), smap)],
        out_shape=[jax.ShapeDtypeStruct((n_p * TM, N_PROJ), f32),
                   jax.ShapeDtypeStruct((n_s * TM, N_PROJ), f32)],
        compiler_params=_cparams(1),
        name="inproj",
    )(xp, xs, w)


def _causal_conv(tail_scr, x, cw_ref, cb_ref, S, L, C):
    tail = tail_scr[...]
    sub = lax.broadcasted_iota(jnp.int32, (S, SUBLANES, C), 1)
    acc = cb_ref[...] + cw_ref[CONV_W - 1:CONV_W, :] * x
    for d in range(1, CONV_W):
        back = pltpu.roll(x, d, 0).reshape(S, L, C)
        head = jnp.where(sub < d, pltpu.roll(tail, d, 1), back[:, 0:SUBLANES, :])
        if L > SUBLANES:
            back = jnp.concatenate([head, back[:, SUBLANES:, :]], axis=1)
        else:
            back = head
        acc = acc + cw_ref[CONV_W - 1 - d:CONV_W - d, :] * back.reshape(S * L, C)
    tail_scr[...] = x.reshape(S, L, C)[:, L - SUBLANES:, :]
    return acc


def _softplus(x):
    return jnp.maximum(x, 0.0) + jnp.log1p(jnp.exp(-jnp.abs(x)))


def _rglru_kernel(*refs, S, L, fresh):
    x_ref, g_ref = refs[:2]
    buf_ref, h0_ref = (None, None) if fresh else refs[2:4]
    (cw_ref, cb_ref, wa_ref, ba_ref, wx_ref, bx_ref, lam_ref,
     y_ref, hN_ref, cN_ref, xs_scr) = refs[2 if fresh else 4:]
    R = S * L
    t = pl.program_id(1)

    @pl.when(t == 0)
    def _():
        xs_scr[...] = jnp.zeros_like(xs_scr)
        if not fresh:
            xs_scr[:, SUBLANES - (CONV_W - 1):SUBLANES, :] = buf_ref[...]
        hN_ref[...] = jnp.zeros_like(hN_ref) if fresh else h0_ref[...]

    x = x_ref[...]
    xc = _causal_conv(xs_scr, x, cw_ref, cb_ref, S, L, D_RG)
    xcb = xc.astype(bf16)
    r = jax.nn.sigmoid(jnp.dot(xcb, wa_ref[...], preferred_element_type=f32) + ba_ref[...])
    ig = jax.nn.sigmoid(jnp.dot(xcb, wx_ref[...], preferred_element_type=f32) + bx_ref[...])
    log_a = (-RG_C) * r * _softplus(-lam_ref[...])
    a = jnp.exp(log_a)
    th = jnp.tanh(log_a)
    u = jnp.sqrt(-2.0 * th / (1.0 - th)) * ig * xc

    n_grp, grp_per_seq = R // SUBLANES, L // SUBLANES
    a3 = a.reshape(n_grp, SUBLANES, D_RG)
    u3 = u.reshape(n_grp, SUBLANES, D_RG)
    sub = lax.broadcasted_iota(jnp.int32, (n_grp, SUBLANES, D_RG), 1)
    s = 1
    while s < SUBLANES:
        ok = sub >= s
        a_sh = pltpu.roll(a3, s, 1)
        u_sh = pltpu.roll(u3, s, 1)
        u3 = jnp.where(ok, a3 * u_sh + u3, u3)
        a3 = jnp.where(ok, a3 * a_sh, a3)
        s *= 2
    h0 = hN_ref[...]
    groups = []
    for kg in range(n_grp):
        carry = h0[kg // grp_per_seq] if kg % grp_per_seq == 0 else groups[-1][SUBLANES - 1:SUBLANES, :]
        groups.append(a3[kg] * carry + u3[kg])
    h = jnp.concatenate(groups, axis=0)

    y_ref[...] = h * jax.nn.gelu(g_ref[...], approximate=True)
    hN_ref[...] = h.reshape(S, L, D_RG)[:, L - 1:L, :]
    cN_ref[...] = xs_scr[:, SUBLANES - (CONV_W - 1):SUBLANES, :]


def _rglru(proj, row0, B, T, S, L, state, cw, cb, wa, ba, wx, bx, lam):
    R = S * L
    nb, nt = B // S, T // L
    blk0 = row0 // R
    row_map = lambda b, t: (blk0 + b * nt + t, 0)
    const2 = lambda b, t: (0, 0)
    state_specs = [] if state is None else [pl.BlockSpec((S, CONV_W - 1, D_RG), lambda b, t: (b, 0, 0)),
                                            pl.BlockSpec((S, 1, D_RG), lambda b, t: (b, 0, 0))]
    return pl.pallas_call(
        functools.partial(_rglru_kernel, S=S, L=L, fresh=state is None),
        grid=(nb, nt),
        in_specs=[
            pl.BlockSpec((R, D_RG), row_map),
            pl.BlockSpec((R, D_RG), lambda b, t: (blk0 + b * nt + t, 1)),
        ] + state_specs + [
            pl.BlockSpec((CONV_W, D_RG), const2),
            pl.BlockSpec((1, D_RG), const2),
            pl.BlockSpec((D_RG, D_RG), const2),
            pl.BlockSpec((1, D_RG), const2),
            pl.BlockSpec((D_RG, D_RG), const2),
            pl.BlockSpec((1, D_RG), const2),
            pl.BlockSpec((1, D_RG), const2),
        ],
        out_specs=[
            pl.BlockSpec((R, D_RG), lambda b, t: (b * nt + t, 0)),
            pl.BlockSpec((S, 1, D_RG), lambda b, t: (b, 0, 0)),
            pl.BlockSpec((S, CONV_W - 1, D_RG), lambda b, t: (b, 0, 0)),
        ],
        out_shape=[
            jax.ShapeDtypeStruct((B * T, D_RG), f32),
            jax.ShapeDtypeStruct((B, 1, D_RG), f32),
            jax.ShapeDtypeStruct((B, CONV_W - 1, D_RG), f32),
        ],
        scratch_shapes=[pltpu.VMEM((S, SUBLANES, D_RG), f32)],
        compiler_params=_cparams(2),
        name="rglru",
    )(proj, proj, *(state or ()), cw, cb, wa, ba, wx, bx, lam)


def _seg_scan(x, op, fill, tin, L, reverse=False):
    s = 1
    while s < L:
        if reverse:
            sh = pltpu.roll(x, LANES - s, 1)
            ok = tin < L - s
        else:
            sh = pltpu.roll(x, s, 1)
            ok = tin >= s
        x = op(x, jnp.where(ok, sh, fill))
        s *= 2
    return x


def _mlstm_kernel(*refs, S, L, G, n_riders, fresh):
    x_ref, z_ref, gt_ref = refs[:3]
    n_state = 0 if fresh else 4
    state_refs = refs[3:3 + n_state]
    n_in = 11 + n_state
    cw_ref, cb_ref, wqk_ref, wv_ref, wkT_ref, bif_ref, ng_ref, sk_ref = refs[3 + n_state:n_in]
    rider_in = refs[n_in:n_in + n_riders]
    y_ref, cN_ref, nN_ref, mN_ref, bN_ref = refs[n_in + n_riders:n_in + n_riders + 5]
    rider_out = refs[n_in + n_riders + 5:n_in + 2 * n_riders + 5]
    xs_scr, m_scr = refs[n_in + 2 * n_riders + 5:]
    for src, dst in zip(rider_in, rider_out):
        dst[...] = src[...].astype(bf16)

    R = S * L
    GS, GR = G * S, G * R
    j = pl.program_id(1)

    @pl.when(j == 0)
    def _():
        if fresh:
            xs_scr[...] = jnp.zeros_like(xs_scr)
            cN_ref[...] = jnp.zeros_like(cN_ref)
            nN_ref[...] = jnp.zeros_like(nN_ref)
            m_scr[...] = jnp.full(m_scr.shape, M_INIT, f32)
        else:
            buf_ref, c0_ref, n0_ref, m0_ref = state_refs
            xs_scr[...] = jnp.zeros_like(xs_scr)
            xs_scr[:, SUBLANES - (CONV_W - 1):SUBLANES, :] = buf_ref[...]
            cN_ref[...] = c0_ref[...]
            nN_ref[...] = n0_ref[...]
            m_scr[...] = m0_ref[...]

    x = x_ref[...].reshape(GR, D_M)
    xc = _causal_conv(xs_scr, x, cw_ref, cb_ref, GS, L, D_M)
    xa = xc * jax.nn.sigmoid(xc)
    xab = xa.astype(bf16)
    xb = x.astype(bf16)

    il_parts, f_parts = [], []
    for g in range(G):
        gT = (gt_ref[g] + bif_ref[...]).T
        il_parts.append(gT[0:SUBLANES, :])
        f_parts.append(pltpu.roll(gT[0:SUBLANES, :], M_HEADS, 0))
    il = jnp.concatenate(il_parts, axis=0)
    fl = -_softplus(-jnp.concatenate(f_parts, axis=0))
    tin = lax.broadcasted_iota(jnp.int32, (G * SUBLANES, LANES), 1) % L
    bcum = _seg_scan(fl, jnp.add, 0.0, tin, L)
    a = il - bcum
    m_prev = m_scr[...].reshape(G * SUBLANES, LANES)
    big_m = jnp.maximum(m_prev, _seg_scan(a, jnp.maximum, -jnp.inf, tin, L))
    m_t = bcum + big_m
    if S == 1:
        m_last = jnp.broadcast_to(big_m[:, LANES - 1:LANES], big_m.shape)
    else:
        m_last = _seg_scan(big_m, jnp.maximum, -jnp.inf, tin, L, reverse=True)
    rows = [big_m, jnp.exp(m_prev - big_m), jnp.exp(-m_t), jnp.exp(a - m_last), jnp.exp(m_prev - m_last)]
    mN_ref[...] = m_t.reshape(G, SUBLANES, LANES)
    m_scr[...] = jnp.broadcast_to(m_t[:, LANES - 1:LANES], m_t.shape).reshape(G, SUBLANES, LANES)
    pad_rows = jnp.zeros((LANES - len(rows) * SUBLANES, LANES), f32)
    cols = []
    for g in range(G):
        gs = slice(g * SUBLANES, (g + 1) * SUBLANES)
        cols.append(jnp.concatenate([r[gs] for r in rows] + [pad_rows], axis=0).T)

    def col(g, q, h):
        return cols[g][:, SUBLANES * q + h:SUBLANES * q + h + 1]

    qs, ks, vs, kTs = [], [], [], []
    for h in range(M_HEADS):
        hs = slice(h * M_DH, (h + 1) * M_DH)
        qk_h = jnp.dot(xab[:, hs], wqk_ref[h], preferred_element_type=f32)
        qs.append(qk_h[:, 0:M_DH])
        ks.append(qk_h[:, M_DH:2 * M_DH] * (M_DH ** -0.5))
        vs.append(jnp.dot(xb[:, hs], wv_ref[h], preferred_element_type=f32))
        if S == 1:
            kT_h = lax.dot_general(wkT_ref[h], xab[:, hs], (((1,), (1,)), ((), ())),
                                   preferred_element_type=f32)
            kTs.append((kT_h * (M_DH ** -0.5)).astype(bf16))

    ti = lax.broadcasted_iota(jnp.int32, (R, R), 0)
    si = lax.broadcasted_iota(jnp.int32, (R, R), 1)
    mask = (si <= ti) & ((ti // L) == (si // L))
    ones_b = jnp.ones((R, M_DH), bf16)
    pairs = [(g, h) for g in range(G) for h in range(M_HEADS)]
    blk = lambda g: slice(g * R, (g + 1) * R)
    seqs = [slice(b * L, (b + 1) * L) for b in range(S)]
    q = {(g, h): qs[h][blk(g)] for g, h in pairs}
    k = {(g, h): ks[h][blk(g)] for g, h in pairs}
    v = {(g, h): vs[h][blk(g)] for g, h in pairs}
    qb = {p: q[p].astype(bf16) for p in pairs}
    kb = {p: k[p].astype(bf16) for p in pairs}
    qk = {p: lax.dot_general(qb[p], kb[p], (((1,), (1,)), ((), ())), preferred_element_type=f32)
          for p in pairs}
    sm = {}
    for g, h in pairs:
        a_row = a[g * SUBLANES + h:g * SUBLANES + h + 1, :]
        decay = jnp.exp(jnp.where(mask, a_row - col(g, 0, h), -jnp.inf))
        sm[g, h] = (qk[g, h] * decay).astype(bf16)
    nd = {p: jnp.dot(sm[p], jnp.concatenate([v[p].astype(bf16), ones_b], axis=1),
                     preferred_element_type=f32) for p in pairs}
    c_old = {(g, h, b): cN_ref[g * S + b, h] for g, h in pairs for b in range(S)}
    n_old = {(g, h, b): nN_ref[g * S + b, h] for g, h in pairs for b in range(S)}
    q_c = {(g, h, b): jnp.dot(qb[g, h][seqs[b]], c_old[g, h, b].astype(bf16), preferred_element_type=f32)
           for g, h in pairs for b in range(S)}
    hh = {}
    for g, h in pairs:
        g_col, e_col = col(g, 1, h), col(g, 2, h)
        pieces = []
        for b, rs in enumerate(seqs):
            q_n = jnp.sum(q[g, h][rs] * n_old[g, h, b], axis=1, keepdims=True)
            num = nd[g, h][rs, 0:M_DH] + g_col[rs] * q_c[g, h, b]
            den = nd[g, h][rs, M_DH:2 * M_DH] + g_col[rs] * q_n
            pieces.append(num / jnp.maximum(jnp.abs(den), e_col[rs]))
        hh[g, h] = pieces[0] if S == 1 else jnp.concatenate(pieces, axis=0)
    mu = {p: jnp.mean(hh[p], axis=1, keepdims=True) for p in pairs}
    var = {p: jnp.mean(jnp.square(hh[p] - mu[p]), axis=1, keepdims=True) for p in pairs}
    hn_blocks = [jnp.concatenate([(hh[g, h] - mu[g, h]) * lax.rsqrt(var[g, h] + LN_EPS)
                                  for h in range(M_HEADS)], axis=1) for g in range(G)]
    new_c, new_n = [], []
    for g, h in pairs:
        w_col = col(g, 3, h)
        wv = (w_col * v[g, h]).astype(bf16)
        wk = w_col * k[g, h]
        for b, rs in enumerate(seqs):
            g_end = col(g, 4, h)[(b + 1) * L - 1:(b + 1) * L, :]
            if S == 1:
                kv = jnp.dot(kTs[h][:, blk(g)], wv, preferred_element_type=f32)
            else:
                kv = lax.dot_general(kb[g, h][rs], wv[rs], (((0,), (0,)), ((), ())),
                                     preferred_element_type=f32)
            new_c.append((g * S + b, h, g_end * c_old[g, h, b] + kv))
            new_n.append((g * S + b, h, g_end * n_old[g, h, b] + jnp.sum(wk[rs], axis=0, keepdims=True)))

    hn = jnp.concatenate(hn_blocks, axis=0) * ng_ref[...]
    y = jax.nn.sigmoid(z_ref[...].reshape(GR, D_M)) * (hn + sk_ref[...] * xa)
    y_ref[...] = y.reshape(G, R, D_M)
    for sq, h, val in new_c:
        cN_ref[sq, h] = val
    for sq, h, val in new_n:
        nN_ref[sq, h] = val
    bN_ref[...] = xs_scr[:, SUBLANES - (CONV_W - 1):SUBLANES, :]


def _mlstm(proj, B, T, S, L, G, state, cw, cb, wqk, wv, wkT, bif, ng, sk, riders=()):
    R = S * L
    assert R == M_ROWS == LANES
    nblk, nc = B // S, T // L
    nb = nblk // G
    assert nblk * S == B and nc * L == T and nb * G == nblk and (S == 1 or nc == 1)
    steps = nb * nc
    assert all(r.shape[0] % steps == 0 for r in riders)
    rider_specs = [pl.BlockSpec((r.shape[0] // steps,) + r.shape[1:], lambda b, j: (b * nc + j, 0, 0))
                   for r in riders]
    proj4 = proj.reshape(nblk, nc, R, N_PROJ)
    const2 = lambda b, j: (0, 0)
    const3 = lambda b, j: (0, 0, 0)
    gate_blk = (2 * D_RG + 2 * D_M) // LANES
    GS = G * S
    state_args, state_specs = (), []
    if state is not None:
        buf8, c0, n0, m0_lanes = state
        state_args = (buf8, c0, n0, m0_lanes.reshape(SUBLANES, nblk, LANES).transpose(1, 0, 2))
        state_specs = [
            pl.BlockSpec((GS, CONV_W - 1, D_M), lambda b, j: (b, 0, 0)),
            pl.BlockSpec((GS, M_HEADS, M_DH, M_DH), lambda b, j: (b, 0, 0, 0)),
            pl.BlockSpec((GS, M_HEADS, 1, M_DH), lambda b, j: (b, 0, 0, 0)),
            pl.BlockSpec((G, SUBLANES, LANES), lambda b, j: (b, 0, 0)),
        ]
    outs = pl.pallas_call(
        functools.partial(_mlstm_kernel, S=S, L=L, G=G, n_riders=len(riders), fresh=state is None),
        grid=(nb, nc),
        in_specs=[
            pl.BlockSpec((G, None, R, D_M), lambda b, j: (b, j, 0, 2)),
            pl.BlockSpec((G, None, R, D_M), lambda b, j: (b, j, 0, 3)),
            pl.BlockSpec((G, None, R, LANES), lambda b, j: (b, j, 0, gate_blk)),
        ] + state_specs + [
            pl.BlockSpec((CONV_W, D_M), const2),
            pl.BlockSpec((1, D_M), const2),
            pl.BlockSpec((M_HEADS, M_DH, 2 * M_DH), const3),
            pl.BlockSpec((M_HEADS, M_DH, M_DH), const3),
            pl.BlockSpec((M_HEADS, M_DH, M_DH), const3),
            pl.BlockSpec((1, LANES), const2),
            pl.BlockSpec((1, D_M), const2),
            pl.BlockSpec((1, D_M), const2),
        ] + rider_specs,
        out_specs=[
            pl.BlockSpec((G, None, R, D_M), lambda b, j: (b, j, 0, 0)),
            pl.BlockSpec((GS, M_HEADS, M_DH, M_DH), lambda b, j: (b, 0, 0, 0)),
            pl.BlockSpec((GS, M_HEADS, 1, M_DH), lambda b, j: (b, 0, 0, 0)),
            pl.BlockSpec((G, None, SUBLANES, LANES), lambda b, j: (b, j, 0, 0)),
            pl.BlockSpec((GS, CONV_W - 1, D_M), lambda b, j: (b, 0, 0)),
        ] + rider_specs,
        out_shape=[
            jax.ShapeDtypeStruct((nblk, nc, R, D_M), f32),
            jax.ShapeDtypeStruct((B, M_HEADS, M_DH, M_DH), f32),
            jax.ShapeDtypeStruct((B, M_HEADS, 1, M_DH), f32),
            jax.ShapeDtypeStruct((nblk, nc, SUBLANES, LANES), f32),
            jax.ShapeDtypeStruct((B, CONV_W - 1, D_M), f32),
        ] + [jax.ShapeDtypeStruct(r.shape, bf16) for r in riders],
        scratch_shapes=[pltpu.VMEM((GS, SUBLANES, D_M), f32), pltpu.VMEM((G, SUBLANES, LANES), f32)],
        compiler_params=_cparams(2),
        name="mlstm",
    )(proj4, proj4, proj4, *state_args, cw, cb, wqk, wv, wkT, bif, ng, sk, *riders)
    y, c_new, n_new, m_t, b_new = outs[:5]
    m_last = m_t[:, nc - 1, :M_HEADS, :].reshape(nblk, M_HEADS, S, L)[:, :, :, L - 1]
    m_last = m_last.transpose(0, 2, 1).reshape(B, M_HEADS)
    return (y.reshape(B * T, D_M), c_new, n_new, m_last, b_new) + tuple(outs[5:])


def _layer_norm(x, g, b):
    mu = jnp.mean(x, axis=-1, keepdims=True)
    var = jnp.mean(jnp.square(x - mu), axis=-1, keepdims=True)
    return (x - mu) * lax.rsqrt(var + LN_EPS) * g + b


def _first_lane_of_max(vals, lane_f):
    vmax = jnp.max(vals, axis=1, keepdims=True)
    idx = jnp.min(jnp.where(vals == vmax, lane_f, float(LANES)), axis=1, keepdims=True)
    return vmax, idx


def _outproj_kernel(xp_ref, xs_ref, rgp_ref, rgs_ref, mp_ref, ms_ref, wo_ref, g1_ref, b1_ref,
                    wr_ref, br_ref, x1_ref, cnt_ref, *, n_p):
    i = pl.program_id(0)

    @pl.when(i == 0)
    def _():
        cnt_ref[...] = jnp.zeros_like(cnt_ref)

    def route(lg, lane, lane_f):
        neg = -jnp.inf
        gl = jnp.where(lane < N_GROUPS, lg, neg)
        gmax, gidx = _first_lane_of_max(gl, lane_f)
        p_g = 1.0 / jnp.sum(jnp.exp(gl - gmax), axis=1, keepdims=True)
        e_lo = float(N_GROUPS) + float(EXPERTS_PER_GROUP) * gidx
        el = jnp.where((lane_f >= e_lo) & (lane_f < e_lo + float(EXPERTS_PER_GROUP)), lg, neg)
        v1, i1 = _first_lane_of_max(el, lane_f)
        v2, i2 = _first_lane_of_max(jnp.where(lane_f == i1, neg, el), lane_f)
        d = jnp.exp(v2 - v1)
        w1 = p_g / (1.0 + d)
        w2 = p_g * d / (1.0 + d)
        first_low = i1 < i2
        j_lo = jnp.minimum(i1, i2) - e_lo
        j_hi = jnp.maximum(i1, i2) - e_lo
        n_first = float(EXPERTS_PER_GROUP - 1) * j_lo - 0.5 * j_lo * (j_lo - 1.0)
        cls = float(PAIRS_PER_GROUP) * gidx + n_first + (j_hi - j_lo - 1.0)
        return cls, jnp.where(first_low, w1, w2), jnp.where(first_low, w2, w1)

    def run(x_ref, rg_ref, m_ref):
        rp = TM_OUT // OUT_PARTS
        parts = [slice(p * rp, (p + 1) * rp) for p in range(OUT_PARTS)]
        lane = lax.broadcasted_iota(jnp.int32, (rp, LANES), 1)
        lane_f = lane.astype(f32)
        heads = [jnp.concatenate([rg_ref[r, :].astype(bf16), m_ref[r, :].astype(bf16)], axis=1) for r in parts]
        mix = [jnp.dot(h, wo_ref[...], preferred_element_type=f32) for h in heads]
        x1 = [_layer_norm(ALPHA * x_ref[r, :] + mx, g1_ref[...], b1_ref[...]) for r, mx in zip(parts, mix)]
        for r, v in zip(parts, x1):
            x1_ref[r, 0:D_MODEL] = v

        hi = [v.astype(bf16) for v in x1]
        lo = [(v - h.astype(f32)).astype(bf16) for v, h in zip(x1, hi)]
        hi_terms = [jnp.dot(h, wr_ref[...], preferred_element_type=f32) for h in hi]
        lo_term = [jnp.dot(v, wr_ref[:, 0:LANES], preferred_element_type=f32) for v in lo]
        lg = [ht[:, 0:LANES] + ht[:, LANES:2 * LANES] + lt + br_ref[...] for ht, lt in zip(hi_terms, lo_term)]
        routed = [route(v, lane, lane_f) for v in lg]

        ti = lax.broadcasted_iota(jnp.int32, (rp, rp), 0)
        si = lax.broadcasted_iota(jnp.int32, (rp, rp), 1)
        tri = (si <= ti).astype(bf16)
        hot = [lane_f == cls for cls, _, _ in routed]
        cum = [jnp.dot(tri, h.astype(bf16), preferred_element_type=f32) for h in hot]
        seen = cnt_ref[...]
        for r, h, cm, (cls, w_lo, w_hi) in zip(parts, hot, cum, routed):
            rank = jnp.sum(jnp.where(h, cm - 1.0 + seen, 0.0), axis=1, keepdims=True)
            seen = seen + cm[rp - 1:rp, :]
            info = jnp.zeros((rp, LANES), f32)
            for c, val in enumerate((cls, rank, w_lo, w_hi)):
                info = jnp.where(lane == c, val, info)
            x1_ref[r, D_MODEL:D_MODEL + LANES] = info
        cnt_ref[...] = seen

    @pl.when(i < n_p)
    def _():
        run(xp_ref, rgp_ref, mp_ref)

    @pl.when(i >= n_p)
    def _():
        run(xs_ref, rgs_ref, ms_ref)


def _outproj(xp, xs, rgp, rgs, mp, ms, wo, g1, b1, wr, br):
    n_p, n_s = xp.shape[0] // TM_OUT, xs.shape[0] // TM_OUT
    n = n_p + n_s
    pmap = lambda i: (jnp.minimum(i, n_p - 1), 0)
    smap = lambda i: (jnp.maximum(i - n_p, 0), 0)
    const = lambda i: (0, 0)
    return pl.pallas_call(
        functools.partial(_outproj_kernel, n_p=n_p),
        grid=(n,),
        in_specs=[
            pl.BlockSpec((TM_OUT, D_MODEL), pmap), pl.BlockSpec((TM_OUT, D_MODEL), smap),
            pl.BlockSpec((TM_OUT, D_RG), pmap), pl.BlockSpec((TM_OUT, D_RG), smap),
            pl.BlockSpec((TM_OUT, D_M), pmap), pl.BlockSpec((TM_OUT, D_M), smap),
            pl.BlockSpec((D_RG + D_M, D_MODEL), const),
            pl.BlockSpec((1, D_MODEL), const), pl.BlockSpec((1, D_MODEL), const),
            pl.BlockSpec((D_MODEL, 2 * LANES), const),
            pl.BlockSpec((1, LANES), const),
        ],
        out_specs=[
            pl.BlockSpec((TM_OUT, X_ROW), lambda i: (i, 0)),
            pl.BlockSpec((1, LANES), const),
        ],
        out_shape=[
            jax.ShapeDtypeStruct((n * TM_OUT, X_ROW), f32),
            jax.ShapeDtypeStruct((1, LANES), f32),
        ],
        compiler_params=_cparams(1),
        name="outproj",
    )(xp, xs, rgp, rgs, mp, ms, wo, g1, b1, wr, br)


GRP_MOVE = TM_MOVE // SUBLANES


def _grouped(x):
    return x.reshape(x.shape[0] // SUBLANES, SUBLANES, x.shape[1])


def _tile_rows_wait(hbm_ref, sem):
    rows = hbm_ref.at[pl.ds(0, TM_MOVE)]
    pltpu.make_async_copy(rows, rows, sem).wait()


def _for_each_tile_row(body):
    def group(k, c):
        for j in range(SUBLANES):
            body(k, j, k * SUBLANES + j)
        return c

    lax.fori_loop(0, GRP_MOVE, group, 0)


def _dispatch_kernel(pad_ref, pos_ref, x1_ref, xs_ref, zero_scr, xbuf, sem, zsem, tsem, lsem):
    i = pl.program_id(0)
    n = pl.num_programs(0)
    start, wait = (lambda cp: cp.start()), (lambda cp: cp.wait())

    def load(t, s):
        return pltpu.make_async_copy(x1_ref.at[pl.ds(t * GRP_MOVE, GRP_MOVE)], xbuf.at[s], lsem.at[s])

    def zero_chunk(row0):
        return pltpu.make_async_copy(zero_scr.at[pl.ds(0, SUBLANES)],
                                     xs_ref.at[pl.ds(pl.multiple_of(row0, SUBLANES), SUBLANES)], zsem)

    def tail_tiles(go):
        def zero_tile(t, c):
            rows = pl.ds(pl.multiple_of(t * TM_MOE, TM_MOE), TM_MOE)
            go(pltpu.make_async_copy(zero_scr, xs_ref.at[rows], tsem))
            return c

        lax.fori_loop(pad_ref[2 * N_CLASSES], xs_ref.shape[0] // TM_MOE, zero_tile, 0)

    @pl.when(i == 0)
    def _():
        load(0, 0).start()
        zero_scr[...] = jnp.zeros_like(zero_scr)
        tail_tiles(start)
        for go in (start, wait):
            def per_class(cl, c):
                row0 = pad_ref[2 * cl]
                lax.fori_loop(0, pad_ref[2 * cl + 1],
                              lambda k, c2: (go(zero_chunk(row0 + k * SUBLANES)), c2)[1], 0)
                return c

            lax.fori_loop(0, N_CLASSES, per_class, 0)

    @pl.when(i == n - 1)
    def _():
        tail_tiles(wait)

    @pl.when(i + 1 < n)
    def _():
        load(i + 1, (i + 1) % 3).start()

    slot = i % 3
    load(i, slot).wait()

    def start(k, j, r):
        pltpu.make_async_copy(xbuf.at[slot, k, pl.ds(j, 1)], xs_ref.at[pl.ds(pos_ref[0, 0, r], 1)],
                              sem.at[slot]).start()

    _for_each_tile_row(start)

    @pl.when(i > 0)
    def _():
        _tile_rows_wait(xs_ref, sem.at[(i + 2) % 3])

    @pl.when(i == n - 1)
    def _():
        _tile_rows_wait(xs_ref, sem.at[slot])


def _dispatch(pad_plan, pos, x1, n_rows):
    n = x1.shape[0] // TM_MOVE
    return pl.pallas_call(
        _dispatch_kernel,
        grid_spec=pltpu.PrefetchScalarGridSpec(
            num_scalar_prefetch=1,
            grid=(n,),
            in_specs=[
                pl.BlockSpec((1, 1, TM_MOVE), lambda i, *_: (i, 0, 0), memory_space=pltpu.SMEM),
                pl.BlockSpec(memory_space=pl.ANY),
            ],
            out_specs=pl.BlockSpec(memory_space=pl.ANY),
            scratch_shapes=[pltpu.VMEM((TM_MOE, X_ROW), f32),
                            pltpu.VMEM((3, GRP_MOVE, SUBLANES, X_ROW), f32),
                            pltpu.SemaphoreType.DMA((3,)), pltpu.SemaphoreType.DMA(()),
                            pltpu.SemaphoreType.DMA(()), pltpu.SemaphoreType.DMA((3,))],
        ),
        out_shape=jax.ShapeDtypeStruct((n_rows, X_ROW), f32),
        compiler_params=_cparams(1),
        name="dispatch",
    )(pad_plan, pos, _grouped(x1))


def _expert_kernel(ta_ref, tb_ref, ca_ref, cb_ref, sa_ref, sb_ref, nu_ref,
                   x_ref, wg_hbm, wu_hbm, wd_hbm, g2_ref, b2_ref, y_ref,
                   wga, wua, wda, wgb, wub, wdb, sem_a, sem_b):
    i = pl.program_id(0)
    n_used = nu_ref[0]
    first = i * MOE_TILES
    side_a = (ta_ref, ca_ref, sa_ref, (wga, wua, wda), sem_a)
    side_b = (tb_ref, cb_ref, sb_ref, (wgb, wub, wdb), sem_b)

    def copies(side, j):
        t_ref, _, s_ref, bufs, sem = side
        e, s = t_ref[j], s_ref[j]
        return [pltpu.make_async_copy(w.at[e], buf.at[s], sem.at[s])
                for w, buf in zip((wg_hbm, wu_hbm, wd_hbm), bufs)]

    def fetch(j):
        @pl.when(j < n_used)
        def _():
            for side in (side_a, side_b):
                @pl.when(side[1][j] == 1)
                def _():
                    for cp in copies(side, j):
                        cp.start()

    @pl.when(i == 0)
    def _():
        for j in range(W_AHEAD):
            fetch(j)

    for t in range(MOE_TILES):
        fetch(first + W_AHEAD + t)

    @pl.when(first < n_used)
    def _():
        for t in range(MOE_TILES):
            for side in (side_a, side_b):
                @pl.when(side[1][first + t] == 1)
                def _():
                    for cp in copies(side, first + t):
                        cp.wait()
        rows = [slice(t * TM_MOE, (t + 1) * TM_MOE) for t in range(MOE_TILES)]
        x = [x_ref[r, 0:D_MODEL] for r in rows]
        info = [x_ref[r, D_MODEL:X_ROW] for r in rows]
        xb = [v.astype(bf16) for v in x]
        units = [(t, bufs, s_ref[first + t]) for t in range(MOE_TILES)
                 for bufs, s_ref in (((wga, wua, wda), sa_ref), ((wgb, wub, wdb), sb_ref))]
        hg = [jnp.dot(xb[t], bufs[0][s], preferred_element_type=f32) for t, bufs, s in units]
        hu = [jnp.dot(xb[t], bufs[1][s], preferred_element_type=f32) for t, bufs, s in units]
        mid = [(g * jax.nn.sigmoid(g) * u).astype(bf16) for g, u in zip(hg, hu)]
        ys = [jnp.dot(m, bufs[2][s], preferred_element_type=f32)
              for m, (t, bufs, s) in zip(mid, units)]
        for t in range(MOE_TILES):
            ffn = info[t][:, 2:3] * ys[2 * t] + info[t][:, 3:4] * ys[2 * t + 1]
            y_ref[rows[t], :] = _layer_norm(ALPHA * x[t] + ffn, g2_ref[...], b2_ref[...])

    @pl.when(first >= n_used)
    def _():
        y_ref[...] = jnp.zeros_like(y_ref)


def _experts(tile_ea, tile_eb, n_used, xs, wg, wu, wd, g2, b2):
    nt = xs.shape[0] // TM_MOE

    def ring_plan(tile_e):
        opens = jnp.concatenate([jnp.ones((1,), jnp.int32), (tile_e[1:] != tile_e[:-1]).astype(jnp.int32)])
        return opens, (jnp.cumsum(opens) - 1) % W_SLOTS

    open_a, slot_a = ring_plan(tile_ea)
    open_b, slot_b = ring_plan(tile_eb)
    assert nt % MOE_TILES == 0
    n_used = ((n_used + MOE_TILES - 1) // MOE_TILES) * MOE_TILES
    const = lambda i, *_: (0, 0)
    xmap = lambda i, ta, tb, ca, cb, sa, sb, nu: (jnp.minimum(i, nu[0] // MOE_TILES - 1), 0)
    w_in = pltpu.VMEM((W_SLOTS, D_MODEL, D_EXPERT), bf16)
    w_out = pltpu.VMEM((W_SLOTS, D_EXPERT, D_MODEL), bf16)
    return pl.pallas_call(
        _expert_kernel,
        grid_spec=pltpu.PrefetchScalarGridSpec(
            num_scalar_prefetch=7,
            grid=(nt // MOE_TILES,),
            in_specs=[
                pl.BlockSpec((MOE_TILES * TM_MOE, X_ROW), xmap),
                pl.BlockSpec(memory_space=pl.ANY), pl.BlockSpec(memory_space=pl.ANY),
                pl.BlockSpec(memory_space=pl.ANY),
                pl.BlockSpec((1, D_MODEL), const), pl.BlockSpec((1, D_MODEL), const),
            ],
            out_specs=pl.BlockSpec((MOE_TILES * TM_MOE, D_MODEL), lambda i, *_: (i, 0)),
            scratch_shapes=[w_in, w_in, w_out, w_in, w_in, w_out,
                            pltpu.SemaphoreType.DMA((W_SLOTS,)), pltpu.SemaphoreType.DMA((W_SLOTS,))],
        ),
        out_shape=jax.ShapeDtypeStruct((nt * TM_MOE, D_MODEL), f32),
        compiler_params=_cparams(1),
        name="experts",
    )(tile_ea, tile_eb, open_a, open_b, slot_a.astype(jnp.int32), slot_b.astype(jnp.int32), n_used,
      xs, wg, wu, wd, g2, b2)


def _collect_kernel(pos_ref, posn_ref, ys_ref, op_ref, os_ref, ybuf, gsem, wsem, *, n_p):
    i = pl.program_id(0)
    n = pl.num_programs(0)
    slot = i % 3

    def gather(p_ref, s):
        def start(k, j, r):
            pltpu.make_async_copy(ys_ref.at[pl.ds(p_ref[0, 0, r], 1)], ybuf.at[s, k, pl.ds(j, 1)],
                                  gsem.at[s]).start()

        _for_each_tile_row(start)

    def write_out(s, out_ref, tile):
        return pltpu.make_async_copy(ybuf.at[s], out_ref.at[pl.ds(tile * GRP_MOVE, GRP_MOVE)], wsem.at[s])

    @pl.when(i == 0)
    def _():
        gather(pos_ref, 0)

    @pl.when(i >= 2)
    def _():
        write_out((i + 1) % 3, op_ref, 0).wait()

    @pl.when(i + 1 < n)
    def _():
        gather(posn_ref, (i + 1) % 3)

    _tile_rows_wait(ys_ref, gsem.at[slot])

    @pl.when(i < n_p)
    def _():
        write_out(slot, op_ref, i).start()

    @pl.when(i >= n_p)
    def _():
        write_out(slot, os_ref, i - n_p).start()

    @pl.when(i == n - 1)
    def _():
        write_out(slot, op_ref, 0).wait()

        @pl.when(n >= 2)
        def _():
            write_out((i + 2) % 3, op_ref, 0).wait()


def _collect(pos, ys, n_p):
    n = pos.shape[0]
    n_s = n - n_p
    outs = pl.pallas_call(
        functools.partial(_collect_kernel, n_p=n_p),
        grid=(n,),
        in_specs=[
            pl.BlockSpec((1, 1, TM_MOVE), lambda i: (i, 0, 0), memory_space=pltpu.SMEM),
            pl.BlockSpec((1, 1, TM_MOVE), lambda i: (jnp.minimum(i + 1, n - 1), 0, 0),
                         memory_space=pltpu.SMEM),
            pl.BlockSpec(memory_space=pl.ANY),
        ],
        out_specs=[pl.BlockSpec(memory_space=pl.ANY), pl.BlockSpec(memory_space=pl.ANY)],
        out_shape=[
            jax.ShapeDtypeStruct((n_p * GRP_MOVE, SUBLANES, D_MODEL), f32),
            jax.ShapeDtypeStruct((n_s * GRP_MOVE, SUBLANES, D_MODEL), f32),
        ],
        scratch_shapes=[pltpu.VMEM((3, GRP_MOVE, SUBLANES, D_MODEL), f32),
                        pltpu.SemaphoreType.DMA((3,)), pltpu.SemaphoreType.DMA((3,))],
        compiler_params=_cparams(1),
        name="collect",
    )(pos, pos, ys)
    return outs[0].reshape(n_p * TM_MOVE, D_MODEL), outs[1].reshape(n_s * TM_MOVE, D_MODEL)


def _block_diag(w):
    n, d, _ = w.shape
    eye = jnp.eye(n, dtype=w.dtype)
    return (eye[:, None, :, None] * w[:, :, None, :]).reshape(n * d, n * d)


def _pair_walk(n):
    total = n * (n - 1) // 2

    def extend(path, used):
        if len(path) == total:
            return path
        a, b = path[-1]
        for c in range(n):
            for nxt in ((a, c), (c, b)):
                key = frozenset(nxt)
                if len(key) == 2 and key not in used:
                    out = extend(path + [nxt], used | {key})
                    if out:
                        return out
        return None

    return extend([(0, 1)], {frozenset((0, 1))})


def kernel(x_prompt, x_sample, state_rg_h, state_rg_conv, state_m_C, state_m_n, state_m_m, state_m_conv, w_in, rg_conv_w, rg_conv_b, rg_w_a, rg_b_a, rg_w_x, rg_b_x, rg_lambda, m_conv_w, m_conv_b, m_w_q, m_w_k, m_w_v, m_b_i, m_b_f, m_norm_g, m_skip, w_out, ln1_g, ln1_b, ln2_g, ln2_b, moe_w_group, moe_b_group, moe_w_expert, moe_b_expert, moe_w_gate, moe_w_up, moe_w_down):
    BP, TP, _ = x_prompt.shape
    BS, TS, _ = x_sample.shape
    n_prompt, n_sample = BP * TP, BS * TS
    xp = x_prompt.reshape(n_prompt, D_MODEL)
    xs = x_sample.reshape(n_sample, D_MODEL)
    l = 0

    w_in_p = jnp.pad(w_in[l], ((0, 0), (0, N_PROJ - w_in.shape[-1]))).astype(bf16)
    wa = _block_diag(rg_w_a[l]).astype(bf16)
    wx = _block_diag(rg_w_x[l]).astype(bf16)
    row = lambda v: v.reshape(1, -1)
    bif = jnp.pad(jnp.concatenate([m_b_i[l], m_b_f[l]]), (0, LANES - 2 * M_HEADS)).reshape(1, LANES)
    wqk = jnp.concatenate([m_w_q[l], m_w_k[l]], axis=-1).astype(bf16)
    wv = m_w_v[l].astype(bf16)
    wkT = m_w_k[l].transpose(0, 2, 1).astype(bf16)
    w_route = jnp.pad(jnp.concatenate([moe_w_group[l], moe_w_expert[l]], axis=1),
                      ((0, 0), (0, LANES - N_GROUPS - N_EXPERTS)))
    wr_hi = w_route.astype(bf16)
    wr_lo = (w_route - wr_hi.astype(f32)).astype(bf16)
    wr = jnp.concatenate([wr_hi, wr_lo], axis=1)
    b_route = jnp.pad(jnp.concatenate([moe_b_group[l], moe_b_expert[l]]),
                      (0, LANES - N_GROUPS - N_EXPERTS)).reshape(1, LANES)

    proj_p, proj_s = _inproj(xp, xs, w_in_p)

    rg_args = (rg_conv_w[l], row(rg_conv_b[l]), wa, row(rg_b_a[l]), wx, row(rg_b_x[l]), row(rg_lambda[l]))
    s_blk = M_ROWS // TS
    yrg_p, p_rg_h, p_rg_conv = _rglru(proj_p, 0, BP, TP, 1, RG_ROWS, None, *rg_args)
    rg_state = (state_rg_conv[l], state_rg_h[l].reshape(BS, 1, D_RG))
    yrg_s, s_rg_h, s_rg_conv = _rglru(proj_s, 0, BS, TS, RG_ROWS // TS, TS, rg_state, *rg_args)

    m_args = (m_conv_w[l], row(m_conv_b[l]), wqk, wv, wkT, bif, row(m_norm_g[l]), row(m_skip[l]))
    ym_p, p_m_C, p_m_n, p_m_m, p_m_conv, wg_b, wu_b, wd_b = _mlstm(
        proj_p, BP, TP, 1, M_ROWS, M_BLOCKS_PROMPT, None, *m_args,
        riders=(moe_w_gate[l], moe_w_up[l], moe_w_down[l]))
    m0_s = jnp.pad(jnp.repeat(state_m_m[l].T, TS, axis=1), ((0, SUBLANES - M_HEADS), (0, 0)))
    m_state = (state_m_conv[l], state_m_C[l], state_m_n[l].reshape(BS, M_HEADS, 1, M_DH), m0_s)
    ym_s, s_m_C, s_m_n, s_m_m, s_m_conv = _mlstm(proj_s, BS, TS, s_blk, TS, M_BLOCKS_SAMPLE, m_state, *m_args)

    x1, cnt = _outproj(xp, xs, yrg_p, yrg_s, ym_p, ym_s, w_out[l].astype(bf16),
                       row(ln1_g[l]), row(ln1_b[l]), wr, b_route)

    n_tok = n_prompt + n_sample
    n_tiles = n_tok // TM_MOE + N_CLASSES
    lay_cls, lay_a, lay_b = [], [], []
    for g in range(N_GROUPS):
        for ja, jb in _pair_walk(EXPERTS_PER_GROUP):
            assert ja < jb
            n_first = (EXPERTS_PER_GROUP - 1) * ja - ja * (ja - 1) // 2
            lay_cls.append(PAIRS_PER_GROUP * g + n_first + jb - ja - 1)
            lay_a.append(g * EXPERTS_PER_GROUP + ja)
            lay_b.append(g * EXPERTS_PER_GROUP + jb)
    classes = jnp.arange(N_CLASSES, dtype=jnp.int32)
    in_slot = jnp.array(lay_cls, jnp.int32)[:, None] == classes
    counts = jnp.sum(jnp.where(in_slot, cnt[0, :N_CLASSES].astype(jnp.int32), 0), axis=1)
    padded = ((counts + TM_MOE - 1) // TM_MOE) * TM_MOE
    ends = jnp.cumsum(padded)
    offs = ends - padded
    offs_of_cls = jnp.sum(jnp.where(in_slot, offs[:, None], 0), axis=0)
    cls = x1[:, D_MODEL].astype(jnp.int32)
    rank = x1[:, D_MODEL + 1].astype(jnp.int32)
    pos = jnp.sum(jnp.where(cls[:, None] == classes, offs_of_cls, 0), axis=-1) + rank
    pos = pos.reshape(n_tok // TM_MOVE, 1, TM_MOVE)
    tiles = jnp.arange(n_tiles, dtype=jnp.int32)
    tile_slot = jnp.minimum(jnp.sum(tiles[:, None] >= (ends // TM_MOE)[None, :], axis=1), N_CLASSES - 1)
    on_slot = tile_slot[:, None] == classes
    tile_ea = jnp.sum(jnp.where(on_slot, jnp.array(lay_a, jnp.int32), 0), axis=-1).astype(jnp.int32)
    tile_eb = jnp.sum(jnp.where(on_slot, jnp.array(lay_b, jnp.int32), 0), axis=-1).astype(jnp.int32)
    n_used = (ends[-1] // TM_MOE).reshape(1).astype(jnp.int32)
    pad_row0 = ((offs + counts) // SUBLANES) * SUBLANES
    pad_plan = jnp.concatenate([jnp.stack([pad_row0, (ends - pad_row0) // SUBLANES], axis=1).reshape(-1),
                                     n_used]).astype(jnp.int32)

    x_sorted = _dispatch(pad_plan, pos, x1, n_tiles * TM_MOE)
    y_sorted = _experts(tile_ea, tile_eb, n_used, x_sorted, wg_b, wu_b, wd_b,
                        row(ln2_g[l]), row(ln2_b[l]))
    y_p, y_s = _collect(pos, y_sorted, n_prompt // TM_MOVE)

    return (y_p.reshape(BP, TP, D_MODEL), y_s.reshape(BS, TS, D_MODEL),
            p_rg_h.reshape(1, BP, D_RG), p_rg_conv[None], p_m_C[None], p_m_n.reshape(1, BP, M_HEADS, M_DH),
            p_m_m[None], p_m_conv[None],
            s_rg_h.reshape(1, BS, D_RG), s_rg_conv[None], s_m_C[None], s_m_n.reshape(1, BS, M_HEADS, M_DH),
            s_m_m[None], s_m_conv[None])
```

```python
import functools

import jax
import jax.numpy as jnp
from jax import lax
from jax.experimental import pallas as pl
from jax.experimental.pallas import tpu as pltpu

f32 = jnp.float32
bf16 = jnp.bfloat16

D_MODEL = 1024
D_RG = 512
RG_C = 8.0
D_M = 512
M_HEADS = 4
M_DH = 128
CONV_W = 4
N_GROUPS = 4
EXPERTS_PER_GROUP = 8
N_EXPERTS = 32
D_EXPERT = 256
ALPHA = 2.0 ** 0.25
LN_EPS = 1e-5
M_INIT = -1.0e4

LANES = 128
SUBLANES = 8
TM = 512
TM_MOVE = 1024
TM_OUT = 1024
OUT_PARTS = 8
TM_MOE = 128
MOE_TILES = 4
W_AHEAD = 10
W_SLOTS = W_AHEAD + MOE_TILES
PAIRS_PER_GROUP = EXPERTS_PER_GROUP * (EXPERTS_PER_GROUP - 1) // 2
N_CLASSES = N_GROUPS * PAIRS_PER_GROUP
X_ROW = D_MODEL + LANES
N_PROJ = 2 * D_RG + 2 * D_M + LANES
RG_ROWS = 1024
M_ROWS = 128
M_BLOCKS_PROMPT = 8
M_BLOCKS_SAMPLE = 2
V7X_VMEM_BYTES = 64 * 1024 * 1024
VMEM_LIMIT = V7X_VMEM_BYTES - 8 * 1024 * 1024


def _cparams(n_axes):
    return pltpu.CompilerParams(dimension_semantics=("arbitrary",) * n_axes,
                                vmem_limit_bytes=VMEM_LIMIT)


def _inproj_kernel(xp_ref, xs_ref, w_ref, op_ref, os_ref, *, n_p):
    i = pl.program_id(0)

    def run(x_ref, o_ref):
        o_ref[...] = jnp.dot(x_ref[...].astype(bf16), w_ref[...], preferred_element_type=f32)

    @pl.when(i < n_p)
    def _():
        run(xp_ref, op_ref)

    @pl.when(i >= n_p)
    def _():
        run(xs_ref, os_ref)


def _inproj(xp, xs, w):
    n_p, n_s = xp.shape[0] // TM, xs.shape[0] // TM
    pmap = lambda i: (jnp.minimum(i, n_p - 1), 0)
    smap = lambda i: (jnp.maximum(i - n_p, 0), 0)
    return pl.pallas_call(
        functools.partial(_inproj_kernel, n_p=n_p),
        grid=(n_p + n_s,),
        in_specs=[
            pl.BlockSpec((TM, D_MODEL), pmap),
            pl.BlockSpec((TM, D_MODEL), smap),
            pl.BlockSpec((D_MODEL, N_PROJ), lambda i: (0, 0)),
        ],
        out_specs=[pl.BlockSpec((TM, N_PROJ), pmap), pl.BlockSpec((TM, N_PROJ), smap)],
        out_shape=[jax.ShapeDtypeStruct((n_p * TM, N_PROJ), f32),
                   jax.ShapeDtypeStruct((n_s * TM, N_PROJ), f32)],
        compiler_params=_cparams(1),
        name="inproj",
    )(xp, xs, w)


def _causal_conv(tail_scr, x, cw_ref, cb_ref, S, L, C):
    tail = tail_scr[...]
    sub = lax.broadcasted_iota(jnp.int32, (S, SUBLANES, C), 1)
    acc = cb_ref[...] + cw_ref[CONV_W - 1:CONV_W, :] * x
    for d in range(1, CONV_W):
        back = pltpu.roll(x, d, 0).reshape(S, L, C)
        head = jnp.where(sub < d, pltpu.roll(tail, d, 1), back[:, 0:SUBLANES, :])
        if L > SUBLANES:
            back = jnp.concatenate([head, back[:, SUBLANES:, :]], axis=1)
        else:
            back = head
        acc = acc + cw_ref[CONV_W - 1 - d:CONV_W - d, :] * back.reshape(S * L, C)
    tail_scr[...] = x.reshape(S, L, C)[:, L - SUBLANES:, :]
    return acc


def _softplus(x):
    return jnp.maximum(x, 0.0) + jnp.log1p(jnp.exp(-jnp.abs(x)))


def _rglru_kernel(*refs, S, L, fresh):
    x_ref, g_ref = refs[:2]
    buf_ref, h0_ref = (None, None) if fresh else refs[2:4]
    (cw_ref, cb_ref, wa_ref, ba_ref, wx_ref, bx_ref, lam_ref,
     y_ref, hN_ref, cN_ref, xs_scr) = refs[2 if fresh else 4:]
    R = S * L
    t = pl.program_id(1)

    @pl.when(t == 0)
    def _():
        xs_scr[...] = jnp.zeros_like(xs_scr)
        if not fresh:
            xs_scr[:, SUBLANES - (CONV_W - 1):SUBLANES, :] = buf_ref[...]
        hN_ref[...] = jnp.zeros_like(hN_ref) if fresh else h0_ref[...]

    x = x_ref[...]
    xc = _causal_conv(xs_scr, x, cw_ref, cb_ref, S, L, D_RG)
    xcb = xc.astype(bf16)
    r = jax.nn.sigmoid(jnp.dot(xcb, wa_ref[...], preferred_element_type=f32) + ba_ref[...])
    ig = jax.nn.sigmoid(jnp.dot(xcb, wx_ref[...], preferred_element_type=f32) + bx_ref[...])
    log_a = (-RG_C) * r * _softplus(-lam_ref[...])
    a = jnp.exp(log_a)
    th = jnp.tanh(log_a)
    u = jnp.sqrt(-2.0 * th / (1.0 - th)) * ig * xc

    n_grp, grp_per_seq = R // SUBLANES, L // SUBLANES
    a3 = a.reshape(n_grp, SUBLANES, D_RG)
    u3 = u.reshape(n_grp, SUBLANES, D_RG)
    sub = lax.broadcasted_iota(jnp.int32, (n_grp, SUBLANES, D_RG), 1)
    s = 1
    while s < SUBLANES:
        ok = sub >= s
        a_sh = pltpu.roll(a3, s, 1)
        u_sh = pltpu.roll(u3, s, 1)
        u3 = jnp.where(ok, a3 * u_sh + u3, u3)
        a3 = jnp.where(ok, a3 * a_sh, a3)
        s *= 2
    h0 = hN_ref[...]
    groups = []
    for kg in range(n_grp):
        carry = h0[kg // grp_per_seq] if kg % grp_per_seq == 0 else groups[-1][SUBLANES - 1:SUBLANES, :]
        groups.append(a3[kg] * carry + u3[kg])
    h = jnp.concatenate(groups, axis=0)

    y_ref[...] = h * jax.nn.gelu(g_ref[...], approximate=True)
    hN_ref[...] = h.reshape(S, L, D_RG)[:, L - 1:L, :]
    cN_ref[...] = xs_scr[:, SUBLANES - (CONV_W - 1):SUBLANES, :]


def _rglru(proj, row0, B, T, S, L, state, cw, cb, wa, ba, wx, bx, lam):
    R = S * L
    nb, nt = B // S, T // L
    blk0 = row0 // R
    row_map = lambda b, t: (blk0 + b * nt + t, 0)
    const2 = lambda b, t: (0, 0)
    state_specs = [] if state is None else [pl.BlockSpec((S, CONV_W - 1, D_RG), lambda b, t: (b, 0, 0)),
                                            pl.BlockSpec((S, 1, D_RG), lambda b, t: (b, 0, 0))]
    return pl.pallas_call(
        functools.partial(_rglru_kernel, S=S, L=L, fresh=state is None),
        grid=(nb, nt),
        in_specs=[
            pl.BlockSpec((R, D_RG), row_map),
            pl.BlockSpec((R, D_RG), lambda b, t: (blk0 + b * nt + t, 1)),
        ] + state_specs + [
            pl.BlockSpec((CONV_W, D_RG), const2),
            pl.BlockSpec((1, D_RG), const2),
            pl.BlockSpec((D_RG, D_RG), const2),
            pl.BlockSpec((1, D_RG), const2),
            pl.BlockSpec((D_RG, D_RG), const2),
            pl.BlockSpec((1, D_RG), const2),
            pl.BlockSpec((1, D_RG), const2),
        ],
        out_specs=[
            pl.BlockSpec((R, D_RG), lambda b, t: (b * nt + t, 0)),
            pl.BlockSpec((S, 1, D_RG), lambda b, t: (b, 0, 0)),
            pl.BlockSpec((S, CONV_W - 1, D_RG), lambda b, t: (b, 0, 0)),
        ],
        out_shape=[
            jax.ShapeDtypeStruct((B * T, D_RG), f32),
            jax.ShapeDtypeStruct((B, 1, D_RG), f32),
            jax.ShapeDtypeStruct((B, CONV_W - 1, D_RG), f32),
        ],
        scratch_shapes=[pltpu.VMEM((S, SUBLANES, D_RG), f32)],
        compiler_params=_cparams(2),
        name="rglru",
    )(proj, proj, *(state or ()), cw, cb, wa, ba, wx, bx, lam)


def _seg_scan(x, op, fill, tin, L, reverse=False):
    s = 1
    while s < L:
        if reverse:
            sh = pltpu.roll(x, LANES - s, 1)
            ok = tin < L - s
        else:
            sh = pltpu.roll(x, s, 1)
            ok = tin >= s
        x = op(x, jnp.where(ok, sh, fill))
        s *= 2
    return x


def _mlstm_kernel(*refs, S, L, G, n_riders, fresh):
    x_ref, z_ref, gt_ref = refs[:3]
    n_state = 0 if fresh else 4
    state_refs = refs[3:3 + n_state]
    n_in = 11 + n_state
    cw_ref, cb_ref, wqk_ref, wv_ref, wkT_ref, bif_ref, ng_ref, sk_ref = refs[3 + n_state:n_in]
    rider_in = refs[n_in:n_in + n_riders]
    y_ref, cN_ref, nN_ref, mN_ref, bN_ref = refs[n_in + n_riders:n_in + n_riders + 5]
    rider_out = refs[n_in + n_riders + 5:n_in + 2 * n_riders + 5]
    xs_scr, m_scr = refs[n_in + 2 * n_riders + 5:]
    for src, dst in zip(rider_in, rider_out):
        dst[...] = src[...].astype(bf16)

    R = S * L
    GS, GR = G * S, G * R
    j = pl.program_id(1)

    @pl.when(j == 0)
    def _():
        if fresh:
            xs_scr[...] = jnp.zeros_like(xs_scr)
            cN_ref[...] = jnp.zeros_like(cN_ref)
            nN_ref[...] = jnp.zeros_like(nN_ref)
            m_scr[...] = jnp.full(m_scr.shape, M_INIT, f32)
        else:
            buf_ref, c0_ref, n0_ref, m0_ref = state_refs
            xs_scr[...] = jnp.zeros_like(xs_scr)
            xs_scr[:, SUBLANES - (CONV_W - 1):SUBLANES, :] = buf_ref[...]
            cN_ref[...] = c0_ref[...]
            nN_ref[...] = n0_ref[...]
            m_scr[...] = m0_ref[...]

    x = x_ref[...].reshape(GR, D_M)
    xc = _causal_conv(xs_scr, x, cw_ref, cb_ref, GS, L, D_M)
    xa = xc * jax.nn.sigmoid(xc)
    xab = xa.astype(bf16)
    xb = x.astype(bf16)

    il_parts, f_parts = [], []
    for g in range(G):
        gT = (gt_ref[g] + bif_ref[...]).T
        il_parts.append(gT[0:SUBLANES, :])
        f_parts.append(pltpu.roll(gT[0:SUBLANES, :], M_HEADS, 0))
    il = jnp.concatenate(il_parts, axis=0)
    fl = -_softplus(-jnp.concatenate(f_parts, axis=0))
    tin = lax.broadcasted_iota(jnp.int32, (G * SUBLANES, LANES), 1) % L
    bcum = _seg_scan(fl, jnp.add, 0.0, tin, L)
    a = il - bcum
    m_prev = m_scr[...].reshape(G * SUBLANES, LANES)
    big_m = jnp.maximum(m_prev, _seg_scan(a, jnp.maximum, -jnp.inf, tin, L))
    m_t = bcum + big_m
    if S == 1:
        m_last = jnp.broadcast_to(big_m[:, LANES - 1:LANES], big_m.shape)
    else:
        m_last = _seg_scan(big_m, jnp.maximum, -jnp.inf, tin, L, reverse=True)
    rows = [big_m, jnp.exp(m_prev - big_m), jnp.exp(-m_t), jnp.exp(a - m_last), jnp.exp(m_prev - m_last)]
    mN_ref[...] = m_t.reshape(G, SUBLANES, LANES)
    m_scr[...] = jnp.broadcast_to(m_t[:, LANES - 1:LANES], m_t.shape).reshape(G, SUBLANES, LANES)
    pad_rows = jnp.zeros((LANES - len(rows) * SUBLANES, LANES), f32)
    cols = []
    for g in range(G):
        gs = slice(g * SUBLANES, (g + 1) * SUBLANES)
        cols.append(jnp.concatenate([r[gs] for r in rows] + [pad_rows], axis=0).T)

    def col(g, q, h):
        return cols[g][:, SUBLANES * q + h:SUBLANES * q + h + 1]

    qs, ks, vs, kTs = [], [], [], []
    for h in range(M_HEADS):
        hs = slice(h * M_DH, (h + 1) * M_DH)
        qk_h = jnp.dot(xab[:, hs], wqk_ref[h], preferred_element_type=f32)
        qs.append(qk_h[:, 0:M_DH])
        ks.append(qk_h[:, M_DH:2 * M_DH] * (M_DH ** -0.5))
        vs.append(jnp.dot(xb[:, hs], wv_ref[h], preferred_element_type=f32))
        if S == 1:
            kT_h = lax.dot_general(wkT_ref[h], xab[:, hs], (((1,), (1,)), ((), ())),
                                   preferred_element_type=f32)
            kTs.append((kT_h * (M_DH ** -0.5)).astype(bf16))

    ti = lax.broadcasted_iota(jnp.int32, (R, R), 0)
    si = lax.broadcasted_iota(jnp.int32, (R, R), 1)
    mask = (si <= ti) & ((ti // L) == (si // L))
    ones_b = jnp.ones((R, M_DH), bf16)
    pairs = [(g, h) for g in range(G) for h in range(M_HEADS)]
    blk = lambda g: slice(g * R, (g + 1) * R)
    seqs = [slice(b * L, (b + 1) * L) for b in range(S)]
    q = {(g, h): qs[h][blk(g)] for g, h in pairs}
    k = {(g, h): ks[h][blk(g)] for g, h in pairs}
    v = {(g, h): vs[h][blk(g)] for g, h in pairs}
    qb = {p: q[p].astype(bf16) for p in pairs}
    kb = {p: k[p].astype(bf16) for p in pairs}
    qk = {p: lax.dot_general(qb[p], kb[p], (((1,), (1,)), ((), ())), preferred_element_type=f32)
          for p in pairs}
    sm = {}
    for g, h in pairs:
        a_row = a[g * SUBLANES + h:g * SUBLANES + h + 1, :]
        decay = jnp.exp(jnp.where(mask, a_row - col(g, 0, h), -jnp.inf))
        sm[g, h] = (qk[g, h] * decay).astype(bf16)
    nd = {p: jnp.dot(sm[p], jnp.concatenate([v[p].astype(bf16), ones_b], axis=1),
                     preferred_element_type=f32) for p in pairs}
    c_old = {(g, h, b): cN_ref[g * S + b, h] for g, h in pairs for b in range(S)}
    n_old = {(g, h, b): nN_ref[g * S + b, h] for g, h in pairs for b in range(S)}
    q_c = {(g, h, b): jnp.dot(qb[g, h][seqs[b]], c_old[g, h, b].astype(bf16), preferred_element_type=f32)
           for g, h in pairs for b in range(S)}
    hh = {}
    for g, h in pairs:
        g_col, e_col = col(g, 1, h), col(g, 2, h)
        pieces = []
        for b, rs in enumerate(seqs):
            q_n = jnp.sum(q[g, h][rs] * n_old[g, h, b], axis=1, keepdims=True)
            num = nd[g, h][rs, 0:M_DH] + g_col[rs] * q_c[g, h, b]
            den = nd[g, h][rs, M_DH:2 * M_DH] + g_col[rs] * q_n
            pieces.append(num / jnp.maximum(jnp.abs(den), e_col[rs]))
        hh[g, h] = pieces[0] if S == 1 else jnp.concatenate(pieces, axis=0)
    mu = {p: jnp.mean(hh[p], axis=1, keepdims=True) for p in pairs}
    var = {p: jnp.mean(jnp.square(hh[p] - mu[p]), axis=1, keepdims=True) for p in pairs}
    hn_blocks = [jnp.concatenate([(hh[g, h] - mu[g, h]) * lax.rsqrt(var[g, h] + LN_EPS)
                                  for h in range(M_HEADS)], axis=1) for g in range(G)]
    new_c, new_n = [], []
    for g, h in pairs:
        w_col = col(g, 3, h)
        wv = (w_col * v[g, h]).astype(bf16)
        wk = w_col * k[g, h]
        for b, rs in enumerate(seqs):
            g_end = col(g, 4, h)[(b + 1) * L - 1:(b + 1) * L, :]
            if S == 1:
                kv = jnp.dot(kTs[h][:, blk(g)], wv, preferred_element_type=f32)
            else:
                kv = lax.dot_general(kb[g, h][rs], wv[rs], (((0,), (0,)), ((), ())),
                                     preferred_element_type=f32)
            new_c.append((g * S + b, h, g_end * c_old[g, h, b] + kv))
            new_n.append((g * S + b, h, g_end * n_old[g, h, b] + jnp.sum(wk[rs], axis=0, keepdims=True)))

    hn = jnp.concatenate(hn_blocks, axis=0) * ng_ref[...]
    y = jax.nn.sigmoid(z_ref[...].reshape(GR, D_M)) * (hn + sk_ref[...] * xa)
    y_ref[...] = y.reshape(G, R, D_M)
    for sq, h, val in new_c:
        cN_ref[sq, h] = val
    for sq, h, val in new_n:
        nN_ref[sq, h] = val
    bN_ref[...] = xs_scr[:, SUBLANES - (CONV_W - 1):SUBLANES, :]


def _mlstm(proj, B, T, S, L, G, state, cw, cb, wqk, wv, wkT, bif, ng, sk, riders=()):
    R = S * L
    assert R == M_ROWS == LANES
    nblk, nc = B // S, T // L
    nb = nblk // G
    assert nblk * S == B and nc * L == T and nb * G == nblk and (S == 1 or nc == 1)
    steps = nb * nc
    assert all(r.shape[0] % steps == 0 for r in riders)
    rider_specs = [pl.BlockSpec((r.shape[0] // steps,) + r.shape[1:], lambda b, j: (b * nc + j, 0, 0))
                   for r in riders]
    proj4 = proj.reshape(nblk, nc, R, N_PROJ)
    const2 = lambda b, j: (0, 0)
    const3 = lambda b, j: (0, 0, 0)
    gate_blk = (2 * D_RG + 2 * D_M) // LANES
    GS = G * S
    state_args, state_specs = (), []
    if state is not None:
        buf8, c0, n0, m0_lanes = state
        state_args = (buf8, c0, n0, m0_lanes.reshape(SUBLANES, nblk, LANES).transpose(1, 0, 2))
        state_specs = [
            pl.BlockSpec((GS, CONV_W - 1, D_M), lambda b, j: (b, 0, 0)),
            pl.BlockSpec((GS, M_HEADS, M_DH, M_DH), lambda b, j: (b, 0, 0, 0)),
            pl.BlockSpec((GS, M_HEADS, 1, M_DH), lambda b, j: (b, 0, 0, 0)),
            pl.BlockSpec((G, SUBLANES, LANES), lambda b, j: (b, 0, 0)),
        ]
    outs = pl.pallas_call(
        functools.partial(_mlstm_kernel, S=S, L=L, G=G, n_riders=len(riders), fresh=state is None),
        grid=(nb, nc),
        in_specs=[
            pl.BlockSpec((G, None, R, D_M), lambda b, j: (b, j, 0, 2)),
            pl.BlockSpec((G, None, R, D_M), lambda b, j: (b, j, 0, 3)),
            pl.BlockSpec((G, None, R, LANES), lambda b, j: (b, j, 0, gate_blk)),
        ] + state_specs + [
            pl.BlockSpec((CONV_W, D_M), const2),
            pl.BlockSpec((1, D_M), const2),
            pl.BlockSpec((M_HEADS, M_DH, 2 * M_DH), const3),
            pl.BlockSpec((M_HEADS, M_DH, M_DH), const3),
            pl.BlockSpec((M_HEADS, M_DH, M_DH), const3),
            pl.BlockSpec((1, LANES), const2),
            pl.BlockSpec((1, D_M), const2),
            pl.BlockSpec((1, D_M), const2),
        ] + rider_specs,
        out_specs=[
            pl.BlockSpec((G, None, R, D_M), lambda b, j: (b, j, 0, 0)),
            pl.BlockSpec((GS, M_HEADS, M_DH, M_DH), lambda b, j: (b, 0, 0, 0)),
            pl.BlockSpec((GS, M_HEADS, 1, M_DH), lambda b, j: (b, 0, 0, 0)),
            pl.BlockSpec((G, None, SUBLANES, LANES), lambda b, j: (b, j, 0, 0)),
            pl.BlockSpec((GS, CONV_W - 1, D_M), lambda b, j: (b, 0, 0)),
        ] + rider_specs,
        out_shape=[
            jax.ShapeDtypeStruct((nblk, nc, R, D_M), f32),
            jax.ShapeDtypeStruct((B, M_HEADS, M_DH, M_DH), f32),
            jax.ShapeDtypeStruct((B, M_HEADS, 1, M_DH), f32),
            jax.ShapeDtypeStruct((nblk, nc, SUBLANES, LANES), f32),
            jax.ShapeDtypeStruct((B, CONV_W - 1, D_M), f32),
        ] + [jax.ShapeDtypeStruct(r.shape, bf16) for r in riders],
        scratch_shapes=[pltpu.VMEM((GS, SUBLANES, D_M), f32), pltpu.VMEM((G, SUBLANES, LANES), f32)],
        compiler_params=_cparams(2),
        name="mlstm",
    )(proj4, proj4, proj4, *state_args, cw, cb, wqk, wv, wkT, bif, ng, sk, *riders)
    y, c_new, n_new, m_t, b_new = outs[:5]
    m_last = m_t[:, nc - 1, :M_HEADS, :].reshape(nblk, M_HEADS, S, L)[:, :, :, L - 1]
    m_last = m_last.transpose(0, 2, 1).reshape(B, M_HEADS)
    return (y.reshape(B * T, D_M), c_new, n_new, m_last, b_new) + tuple(outs[5:])


def _layer_norm(x, g, b):
    mu = jnp.mean(x, axis=-1, keepdims=True)
    var = jnp.mean(jnp.square(x - mu), axis=-1, keepdims=True)
    return (x - mu) * lax.rsqrt(var + LN_EPS) * g + b


def _first_lane_of_max(vals, lane_f):
    vmax = jnp.max(vals, axis=1, keepdims=True)
    idx = jnp.min(jnp.where(vals == vmax, lane_f, float(LANES)), axis=1, keepdims=True)
    return vmax, idx


def _outproj_kernel(xp_ref, xs_ref, rgp_ref, rgs_ref, mp_ref, ms_ref, wo_ref, g1_ref, b1_ref,
                    wr_ref, br_ref, x1_ref, cnt_ref, *, n_p):
    i = pl.program_id(0)

    @pl.when(i == 0)
    def _():
        cnt_ref[...] = jnp.zeros_like(cnt_ref)

    def route(lg, lane, lane_f):
        neg = -jnp.inf
        gl = jnp.where(lane < N_GROUPS, lg, neg)
        gmax, gidx = _first_lane_of_max(gl, lane_f)
        p_g = 1.0 / jnp.sum(jnp.exp(gl - gmax), axis=1, keepdims=True)
        e_lo = float(N_GROUPS) + float(EXPERTS_PER_GROUP) * gidx
        el = jnp.where((lane_f >= e_lo) & (lane_f < e_lo + float(EXPERTS_PER_GROUP)), lg, neg)
        v1, i1 = _first_lane_of_max(el, lane_f)
        v2, i2 = _first_lane_of_max(jnp.where(lane_f == i1, neg, el), lane_f)
        d = jnp.exp(v2 - v1)
        w1 = p_g / (1.0 + d)
        w2 = p_g * d / (1.0 + d)
        first_low = i1 < i2
        j_lo = jnp.minimum(i1, i2) - e_lo
        j_hi = jnp.maximum(i1, i2) - e_lo
        n_first = float(EXPERTS_PER_GROUP - 1) * j_lo - 0.5 * j_lo * (j_lo - 1.0)
        cls = float(PAIRS_PER_GROUP) * gidx + n_first + (j_hi - j_lo - 1.0)
        return cls, jnp.where(first_low, w1, w2), jnp.where(first_low, w2, w1)

    def run(x_ref, rg_ref, m_ref):
        rp = TM_OUT // OUT_PARTS
        parts = [slice(p * rp, (p + 1) * rp) for p in range(OUT_PARTS)]
        lane = lax.broadcasted_iota(jnp.int32, (rp, LANES), 1)
        lane_f = lane.astype(f32)
        heads = [jnp.concatenate([rg_ref[r, :].astype(bf16), m_ref[r, :].astype(bf16)], axis=1) for r in parts]
        mix = [jnp.dot(h, wo_ref[...], preferred_element_type=f32) for h in heads]
        x1 = [_layer_norm(ALPHA * x_ref[r, :] + mx, g1_ref[...], b1_ref[...]) for r, mx in zip(parts, mix)]
        for r, v in zip(parts, x1):
            x1_ref[r, 0:D_MODEL] = v

        hi = [v.astype(bf16) for v in x1]
        lo = [(v - h.astype(f32)).astype(bf16) for v, h in zip(x1, hi)]
        hi_terms = [jnp.dot(h, wr_ref[...], preferred_element_type=f32) for h in hi]
        lo_term = [jnp.dot(v, wr_ref[:, 0:LANES], preferred_element_type=f32) for v in lo]
        lg = [ht[:, 0:LANES] + ht[:, LANES:2 * LANES] + lt + br_ref[...] for ht, lt in zip(hi_terms, lo_term)]
        routed = [route(v, lane, lane_f) for v in lg]

        ti = lax.broadcasted_iota(jnp.int32, (rp, rp), 0)
        si = lax.broadcasted_iota(jnp.int32, (rp, rp), 1)
        tri = (si <= ti).astype(bf16)
        hot = [lane_f == cls for cls, _, _ in routed]
        cum = [jnp.dot(tri, h.astype(bf16), preferred_element_type=f32) for h in hot]
        seen = cnt_ref[...]
        for r, h, cm, (cls, w_lo, w_hi) in zip(parts, hot, cum, routed):
            rank = jnp.sum(jnp.where(h, cm - 1.0 + seen, 0.0), axis=1, keepdims=True)
            seen = seen + cm[rp - 1:rp, :]
            info = jnp.zeros((rp, LANES), f32)
            for c, val in enumerate((cls, rank, w_lo, w_hi)):
                info = jnp.where(lane == c, val, info)
            x1_ref[r, D_MODEL:D_MODEL + LANES] = info
        cnt_ref[...] = seen

    @pl.when(i < n_p)
    def _():
        run(xp_ref, rgp_ref, mp_ref)

    @pl.when(i >= n_p)
    def _():
        run(xs_ref, rgs_ref, ms_ref)


def _outproj(xp, xs, rgp, rgs, mp, ms, wo, g1, b1, wr, br):
    n_p, n_s = xp.shape[0] // TM_OUT, xs.shape[0] // TM_OUT
    n = n_p + n_s
    pmap = lambda i: (jnp.minimum(i, n_p - 1), 0)
    smap = lambda i: (jnp.maximum(i - n_p, 0), 0)
    const = lambda i: (0, 0)
    return pl.pallas_call(
        functools.partial(_outproj_kernel, n_p=n_p),
        grid=(n,),
        in_specs=[
            pl.BlockSpec((TM_OUT, D_MODEL), pmap), pl.BlockSpec((TM_OUT, D_MODEL), smap),
            pl.BlockSpec((TM_OUT, D_RG), pmap), pl.BlockSpec((TM_OUT, D_RG), smap),
            pl.BlockSpec((TM_OUT, D_M), pmap), pl.BlockSpec((TM_OUT, D_M), smap),
            pl.BlockSpec((D_RG + D_M, D_MODEL), const),
            pl.BlockSpec((1, D_MODEL), const), pl.BlockSpec((1, D_MODEL), const),
            pl.BlockSpec((D_MODEL, 2 * LANES), const),
            pl.BlockSpec((1, LANES), const),
        ],
        out_specs=[
            pl.BlockSpec((TM_OUT, X_ROW), lambda i: (i, 0)),
            pl.BlockSpec((1, LANES), const),
        ],
        out_shape=[
            jax.ShapeDtypeStruct((n * TM_OUT, X_ROW), f32),
            jax.ShapeDtypeStruct((1, LANES), f32),
        ],
        compiler_params=_cparams(1),
        name="outproj",
    )(xp, xs, rgp, rgs, mp, ms, wo, g1, b1, wr, br)


GRP_MOVE = TM_MOVE // SUBLANES


def _grouped(x):
    return x.reshape(x.shape[0] // SUBLANES, SUBLANES, x.shape[1])


def _tile_rows_wait(hbm_ref, sem):
    rows = hbm_ref.at[pl.ds(0, TM_MOVE)]
    pltpu.make_async_copy(rows, rows, sem).wait()


def _for_each_tile_row(body):
    def group(k, c):
        for j in range(SUBLANES):
            body(k, j, k * SUBLANES + j)
        return c

    lax.fori_loop(0, GRP_MOVE, group, 0)


def _dispatch_kernel(pad_ref, pos_ref, x1_ref, xs_ref, zero_scr, xbuf, sem, zsem, tsem, lsem):
    i = pl.program_id(0)
    n = pl.num_programs(0)
    start, wait = (lambda cp: cp.start()), (lambda cp: cp.wait())

    def load(t, s):
        return pltpu.make_async_copy(x1_ref.at[pl.ds(t * GRP_MOVE, GRP_MOVE)], xbuf.at[s], lsem.at[s])

    def zero_chunk(row0):
        return pltpu.make_async_copy(zero_scr.at[pl.ds(0, SUBLANES)],
                                     xs_ref.at[pl.ds(pl.multiple_of(row0, SUBLANES), SUBLANES)], zsem)

    def tail_tiles(go):
        def zero_tile(t, c):
            rows = pl.ds(pl.multiple_of(t * TM_MOE, TM_MOE), TM_MOE)
            go(pltpu.make_async_copy(zero_scr, xs_ref.at[rows], tsem))
            return c

        lax.fori_loop(pad_ref[2 * N_CLASSES], xs_ref.shape[0] // TM_MOE, zero_tile, 0)

    @pl.when(i == 0)
    def _():
        load(0, 0).start()
        zero_scr[...] = jnp.zeros_like(zero_scr)
        tail_tiles(start)
        for go in (start, wait):
            def per_class(cl, c):
                row0 = pad_ref[2 * cl]
                lax.fori_loop(0, pad_ref[2 * cl + 1],
                              lambda k, c2: (go(zero_chunk(row0 + k * SUBLANES)), c2)[1], 0)
                return c

            lax.fori_loop(0, N_CLASSES, per_class, 0)

    @pl.when(i == n - 1)
    def _():
        tail_tiles(wait)

    @pl.when(i + 1 < n)
    def _():
        load(i + 1, (i + 1) % 3).start()

    slot = i % 3
    load(i, slot).wait()

    def start(k, j, r):
        pltpu.make_async_copy(xbuf.at[slot, k, pl.ds(j, 1)], xs_ref.at[pl.ds(pos_ref[0, 0, r], 1)],
                              sem.at[slot]).start()

    _for_each_tile_row(start)

    @pl.when(i > 0)
    def _():
        _tile_rows_wait(xs_ref, sem.at[(i + 2) % 3])

    @pl.when(i == n - 1)
    def _():
        _tile_rows_wait(xs_ref, sem.at[slot])


def _dispatch(pad_plan, pos, x1, n_rows):
    n = x1.shape[0] // TM_MOVE
    return pl.pallas_call(
        _dispatch_kernel,
        grid_spec=pltpu.PrefetchScalarGridSpec(
            num_scalar_prefetch=1,
            grid=(n,),
            in_specs=[
                pl.BlockSpec((1, 1, TM_MOVE), lambda i, *_: (i, 0, 0), memory_space=pltpu.SMEM),
                pl.BlockSpec(memory_space=pl.ANY),
            ],
            out_specs=pl.BlockSpec(memory_space=pl.ANY),
            scratch_shapes=[pltpu.VMEM((TM_MOE, X_ROW), f32),
                            pltpu.VMEM((3, GRP_MOVE, SUBLANES, X_ROW), f32),
                            pltpu.SemaphoreType.DMA((3,)), pltpu.SemaphoreType.DMA(()),
                            pltpu.SemaphoreType.DMA(()), pltpu.SemaphoreType.DMA((3,))],
        ),
        out_shape=jax.ShapeDtypeStruct((n_rows, X_ROW), f32),
        compiler_params=_cparams(1),
        name="dispatch",
    )(pad_plan, pos, _grouped(x1))


def _expert_kernel(ta_ref, tb_ref, ca_ref, cb_ref, sa_ref, sb_ref, nu_ref,
                   x_ref, wg_hbm, wu_hbm, wd_hbm, g2_ref, b2_ref, y_ref,
                   wga, wua, wda, wgb, wub, wdb, sem_a, sem_b):
    i = pl.program_id(0)
    n_used = nu_ref[0]
    first = i * MOE_TILES
    side_a = (ta_ref, ca_ref, sa_ref, (wga, wua, wda), sem_a)
    side_b = (tb_ref, cb_ref, sb_ref, (wgb, wub, wdb), sem_b)

    def copies(side, j):
        t_ref, _, s_ref, bufs, sem = side
        e, s = t_ref[j], s_ref[j]
        return [pltpu.make_async_copy(w.at[e], buf.at[s], sem.at[s])
                for w, buf in zip((wg_hbm, wu_hbm, wd_hbm), bufs)]

    def fetch(j):
        @pl.when(j < n_used)
        def _():
            for side in (side_a, side_b):
                @pl.when(side[1][j] == 1)
                def _():
                    for cp in copies(side, j):
                        cp.start()

    @pl.when(i == 0)
    def _():
        for j in range(W_AHEAD):
            fetch(j)

    for t in range(MOE_TILES):
        fetch(first + W_AHEAD + t)

    @pl.when(first < n_used)
    def _():
        for t in range(MOE_TILES):
            for side in (side_a, side_b):
                @pl.when(side[1][first + t] == 1)
                def _():
                    for cp in copies(side, first + t):
                        cp.wait()
        rows = [slice(t * TM_MOE, (t + 1) * TM_MOE) for t in range(MOE_TILES)]
        x = [x_ref[r, 0:D_MODEL] for r in rows]
        info = [x_ref[r, D_MODEL:X_ROW] for r in rows]
        xb = [v.astype(bf16) for v in x]
        units = [(t, bufs, s_ref[first + t]) for t in range(MOE_TILES)
                 for bufs, s_ref in (((wga, wua, wda), sa_ref), ((wgb, wub, wdb), sb_ref))]
        hg = [jnp.dot(xb[t], bufs[0][s], preferred_element_type=f32) for t, bufs, s in units]
        hu = [jnp.dot(xb[t], bufs[1][s], preferred_element_type=f32) for t, bufs, s in units]
        mid = [(g * jax.nn.sigmoid(g) * u).astype(bf16) for g, u in zip(hg, hu)]
        ys = [jnp.dot(m, bufs[2][s], preferred_element_type=f32)
              for m, (t, bufs, s) in zip(mid, units)]
        for t in range(MOE_TILES):
            ffn = info[t][:, 2:3] * ys[2 * t] + info[t][:, 3:4] * ys[2 * t + 1]
            y_ref[rows[t], :] = _layer_norm(ALPHA * x[t] + ffn, g2_ref[...], b2_ref[...])

    @pl.when(first >= n_used)
    def _():
        y_ref[...] = jnp.zeros_like(y_ref)


def _experts(tile_ea, tile_eb, n_used, xs, wg, wu, wd, g2, b2):
    nt = xs.shape[0] // TM_MOE

    def ring_plan(tile_e):
        opens = jnp.concatenate([jnp.ones((1,), jnp.int32), (tile_e[1:] != tile_e[:-1]).astype(jnp.int32)])
        return opens, (jnp.cumsum(opens) - 1) % W_SLOTS

    open_a, slot_a = ring_plan(tile_ea)
    open_b, slot_b = ring_plan(tile_eb)
    assert nt % MOE_TILES == 0
    n_used = ((n_used + MOE_TILES - 1) // MOE_TILES) * MOE_TILES
    const = lambda i, *_: (0, 0)
    xmap = lambda i, ta, tb, ca, cb, sa, sb, nu: (jnp.minimum(i, nu[0] // MOE_TILES - 1), 0)
    w_in = pltpu.VMEM((W_SLOTS, D_MODEL, D_EXPERT), bf16)
    w_out = pltpu.VMEM((W_SLOTS, D_EXPERT, D_MODEL), bf16)
    return pl.pallas_call(
        _expert_kernel,
        grid_spec=pltpu.PrefetchScalarGridSpec(
            num_scalar_prefetch=7,
            grid=(nt // MOE_TILES,),
            in_specs=[
                pl.BlockSpec((MOE_TILES * TM_MOE, X_ROW), xmap),
                pl.BlockSpec(memory_space=pl.ANY), pl.BlockSpec(memory_space=pl.ANY),
                pl.BlockSpec(memory_space=pl.ANY),
                pl.BlockSpec((1, D_MODEL), const), pl.BlockSpec((1, D_MODEL), const),
            ],
            out_specs=pl.BlockSpec((MOE_TILES * TM_MOE, D_MODEL), lambda i, *_: (i, 0)),
            scratch_shapes=[w_in, w_in, w_out, w_in, w_in, w_out,
                            pltpu.SemaphoreType.DMA((W_SLOTS,)), pltpu.SemaphoreType.DMA((W_SLOTS,))],
        ),
        out_shape=jax.ShapeDtypeStruct((nt * TM_MOE, D_MODEL), f32),
        compiler_params=_cparams(1),
        name="experts",
    )(tile_ea, tile_eb, open_a, open_b, slot_a.astype(jnp.int32), slot_b.astype(jnp.int32), n_used,
      xs, wg, wu, wd, g2, b2)


def _collect_kernel(pos_ref, posn_ref, ys_ref, op_ref, os_ref, ybuf, gsem, wsem, *, n_p):
    i = pl.program_id(0)
    n = pl.num_programs(0)
    slot = i % 3

    def gather(p_ref, s):
        def start(k, j, r):
            pltpu.make_async_copy(ys_ref.at[pl.ds(p_ref[0, 0, r], 1)], ybuf.at[s, k, pl.ds(j, 1)],
                                  gsem.at[s]).start()

        _for_each_tile_row(start)

    def write_out(s, out_ref, tile):
        return pltpu.make_async_copy(ybuf.at[s], out_ref.at[pl.ds(tile * GRP_MOVE, GRP_MOVE)], wsem.at[s])

    @pl.when(i == 0)
    def _():
        gather(pos_ref, 0)

    @pl.when(i >= 2)
    def _():
        write_out((i + 1) % 3, op_ref, 0).wait()

    @pl.when(i + 1 < n)
    def _():
        gather(posn_ref, (i + 1) % 3)

    _tile_rows_wait(ys_ref, gsem.at[slot])

    @pl.when(i < n_p)
    def _():
        write_out(slot, op_ref, i).start()

    @pl.when(i >= n_p)
    def _():
        write_out(slot, os_ref, i - n_p).start()

    @pl.when(i == n - 1)
    def _():
        write_out(slot, op_ref, 0).wait()

        @pl.when(n >= 2)
        def _():
            write_out((i + 2) % 3, op_ref, 0).wait()


def _collect(pos, ys, n_p):
    n = pos.shape[0]
    n_s = n - n_p
    outs = pl.pallas_call(
        functools.partial(_collect_kernel, n_p=n_p),
        grid=(n,),
        in_specs=[
            pl.BlockSpec((1, 1, TM_MOVE), lambda i: (i, 0, 0), memory_space=pltpu.SMEM),
            pl.BlockSpec((1, 1, TM_MOVE), lambda i: (jnp.minimum(i + 1, n - 1), 0, 0),
                         memory_space=pltpu.SMEM),
            pl.BlockSpec(memory_space=pl.ANY),
        ],
        out_specs=[pl.BlockSpec(memory_space=pl.ANY), pl.BlockSpec(memory_space=pl.ANY)],
        out_shape=[
            jax.ShapeDtypeStruct((n_p * GRP_MOVE, SUBLANES, D_MODEL), f32),
            jax.ShapeDtypeStruct((n_s * GRP_MOVE, SUBLANES, D_MODEL), f32),
        ],
        scratch_shapes=[pltpu.VMEM((3, GRP_MOVE, SUBLANES, D_MODEL), f32),
                        pltpu.SemaphoreType.DMA((3,)), pltpu.SemaphoreType.DMA((3,))],
        compiler_params=_cparams(1),
        name="collect",
    )(pos, pos, ys)
    return outs[0].reshape(n_p * TM_MOVE, D_MODEL), outs[1].reshape(n_s * TM_MOVE, D_MODEL)


def _block_diag(w):
    n, d, _ = w.shape
    eye = jnp.eye(n, dtype=w.dtype)
    return (eye[:, None, :, None] * w[:, :, None, :]).reshape(n * d, n * d)


def _pair_walk(n):
    total = n * (n - 1) // 2

    def extend(path, used):
        if len(path) == total:
            return path
        a, b = path[-1]
        for c in range(n):
            for nxt in ((a, c), (c, b)):
                key = frozenset(nxt)
                if len(key) == 2 and key not in used:
                    out = extend(path + [nxt], used | {key})
                    if out:
                        return out
        return None

    return extend([(0, 1)], {frozenset((0, 1))})


def kernel(x_prompt, x_sample, state_rg_h, state_rg_conv, state_m_C, state_m_n, state_m_m, state_m_conv, w_in, rg_conv_w, rg_conv_b, rg_w_a, rg_b_a, rg_w_x, rg_b_x, rg_lambda, m_conv_w, m_conv_b, m_w_q, m_w_k, m_w_v, m_b_i, m_b_f, m_norm_g, m_skip, w_out, ln1_g, ln1_b, ln2_g, ln2_b, moe_w_group, moe_b_group, moe_w_expert, moe_b_expert, moe_w_gate, moe_w_up, moe_w_down):
    BP, TP, _ = x_prompt.shape
    BS, TS, _ = x_sample.shape
    n_prompt, n_sample = BP * TP, BS * TS
    xp = x_prompt.reshape(n_prompt, D_MODEL)
    xs = x_sample.reshape(n_sample, D_MODEL)
    l = 0

    w_in_p = jnp.pad(w_in[l], ((0, 0), (0, N_PROJ - w_in.shape[-1]))).astype(bf16)
    wa = _block_diag(rg_w_a[l]).astype(bf16)
    wx = _block_diag(rg_w_x[l]).astype(bf16)
    row = lambda v: v.reshape(1, -1)
    bif = jnp.pad(jnp.concatenate([m_b_i[l], m_b_f[l]]), (0, LANES - 2 * M_HEADS)).reshape(1, LANES)
    wqk = jnp.concatenate([m_w_q[l], m_w_k[l]], axis=-1).astype(bf16)
    wv = m_w_v[l].astype(bf16)
    wkT = m_w_k[l].transpose(0, 2, 1).astype(bf16)
    w_route = jnp.pad(jnp.concatenate([moe_w_group[l], moe_w_expert[l]], axis=1),
                      ((0, 0), (0, LANES - N_GROUPS - N_EXPERTS)))
    wr_hi = w_route.astype(bf16)
    wr_lo = (w_route - wr_hi.astype(f32)).astype(bf16)
    wr = jnp.concatenate([wr_hi, wr_lo], axis=1)
    b_route = jnp.pad(jnp.concatenate([moe_b_group[l], moe_b_expert[l]]),
                      (0, LANES - N_GROUPS - N_EXPERTS)).reshape(1, LANES)

    proj_p, proj_s = _inproj(xp, xs, w_in_p)

    rg_args = (rg_conv_w[l], row(rg_conv_b[l]), wa, row(rg_b_a[l]), wx, row(rg_b_x[l]), row(rg_lambda[l]))
    s_blk = M_ROWS // TS
    yrg_p, p_rg_h, p_rg_conv = _rglru(proj_p, 0, BP, TP, 1, RG_ROWS, None, *rg_args)
    rg_state = (state_rg_conv[l], state_rg_h[l].reshape(BS, 1, D_RG))
    yrg_s, s_rg_h, s_rg_conv = _rglru(proj_s, 0, BS, TS, RG_ROWS // TS, TS, rg_state, *rg_args)

    m_args = (m_conv_w[l], row(m_conv_b[l]), wqk, wv, wkT, bif, row(m_norm_g[l]), row(m_skip[l]))
    ym_p, p_m_C, p_m_n, p_m_m, p_m_conv, wg_b, wu_b, wd_b = _mlstm(
        proj_p, BP, TP, 1, M_ROWS, M_BLOCKS_PROMPT, None, *m_args,
        riders=(moe_w_gate[l], moe_w_up[l], moe_w_down[l]))
    m0_s = jnp.pad(jnp.repeat(state_m_m[l].T, TS, axis=1), ((0, SUBLANES - M_HEADS), (0, 0)))
    m_state = (state_m_conv[l], state_m_C[l], state_m_n[l].reshape(BS, M_HEADS, 1, M_DH), m0_s)
    ym_s, s_m_C, s_m_n, s_m_m, s_m_conv = _mlstm(proj_s, BS, TS, s_blk, TS, M_BLOCKS_SAMPLE, m_state, *m_args)

    x1, cnt = _outproj(xp, xs, yrg_p, yrg_s, ym_p, ym_s, w_out[l].astype(bf16),
                       row(ln1_g[l]), row(ln1_b[l]), wr, b_route)

    n_tok = n_prompt + n_sample
    n_tiles = n_tok // TM_MOE + N_CLASSES
    lay_cls, lay_a, lay_b = [], [], []
    for g in range(N_GROUPS):
        for ja, jb in _pair_walk(EXPERTS_PER_GROUP):
            assert ja < jb
            n_first = (EXPERTS_PER_GROUP - 1) * ja - ja * (ja - 1) // 2
            lay_cls.append(PAIRS_PER_GROUP * g + n_first + jb - ja - 1)
            lay_a.append(g * EXPERTS_PER_GROUP + ja)
            lay_b.append(g * EXPERTS_PER_GROUP + jb)
    classes = jnp.arange(N_CLASSES, dtype=jnp.int32)
    in_slot = jnp.array(lay_cls, jnp.int32)[:, None] == classes
    counts = jnp.sum(jnp.where(in_slot, cnt[0, :N_CLASSES].astype(jnp.int32), 0), axis=1)
    padded = ((counts + TM_MOE - 1) // TM_MOE) * TM_MOE
    ends = jnp.cumsum(padded)
    offs = ends - padded
    offs_of_cls = jnp.sum(jnp.where(in_slot, offs[:, None], 0), axis=0)
    cls = x1[:, D_MODEL].astype(jnp.int32)
    rank = x1[:, D_MODEL + 1].astype(jnp.int32)
    pos = jnp.sum(jnp.where(cls[:, None] == classes, offs_of_cls, 0), axis=-1) + rank
    pos = pos.reshape(n_tok // TM_MOVE, 1, TM_MOVE)
    tiles = jnp.arange(n_tiles, dtype=jnp.int32)
    tile_slot = jnp.minimum(jnp.sum(tiles[:, None] >= (ends // TM_MOE)[None, :], axis=1), N_CLASSES - 1)
    on_slot = tile_slot[:, None] == classes
    tile_ea = jnp.sum(jnp.where(on_slot, jnp.array(lay_a, jnp.int32), 0), axis=-1).astype(jnp.int32)
    tile_eb = jnp.sum(jnp.where(on_slot, jnp.array(lay_b, jnp.int32), 0), axis=-1).astype(jnp.int32)
    n_used = (ends[-1] // TM_MOE).reshape(1).astype(jnp.int32)
    pad_row0 = ((offs + counts) // SUBLANES) * SUBLANES
    pad_plan = jnp.concatenate([jnp.stack([pad_row0, (ends - pad_row0) // SUBLANES], axis=1).reshape(-1),
                                     n_used]).astype(jnp.int32)

    x_sorted = _dispatch(pad_plan, pos, x1, n_tiles * TM_MOE)
    y_sorted = _experts(tile_ea, tile_eb, n_used, x_sorted, wg_b, wu_b, wd_b,
                        row(ln2_g[l]), row(ln2_b[l]))
    y_p, y_s = _collect(pos, y_sorted, n_prompt // TM_MOVE)

    return (y_p.reshape(BP, TP, D_MODEL), y_s.reshape(BS, TS, D_MODEL),
            p_rg_h.reshape(1, BP, D_RG), p_rg_conv[None], p_m_C[None], p_m_n.reshape(1, BP, M_HEADS, M_DH),
            p_m_m[None], p_m_conv[None],
            s_rg_h.reshape(1, BS, D_RG), s_rg_conv[None], s_m_C[None], s_m_n.reshape(1, BS, M_HEADS, M_DH),
            s_m_m[None], s_m_conv[None])
```

```python
import functools

import jax
import jax.numpy as jnp
from jax import lax
from jax.experimental import pallas as pl
from jax.experimental.pallas import tpu as pltpu

f32 = jnp.float32
bf16 = jnp.bfloat16

D_MODEL = 1024
D_RG = 512
RG_C = 8.0
D_M = 512
M_HEADS = 4
M_DH = 128
CONV_W = 4
N_GROUPS = 4
EXPERTS_PER_GROUP = 8
N_EXPERTS = 32
D_EXPERT = 256
ALPHA = 2.0 ** 0.25
LN_EPS = 1e-5
M_INIT = -1.0e4

LANES = 128
SUBLANES = 8
TM = 512
TM_MOVE = 1024
TM_OUT = 1024
OUT_PARTS = 8
TM_MOE = 128
MOE_TILES = 4
W_AHEAD = 10
W_SLOTS = W_AHEAD + MOE_TILES
PAIRS_PER_GROUP = EXPERTS_PER_GROUP * (EXPERTS_PER_GROUP - 1) // 2
N_CLASSES = N_GROUPS * PAIRS_PER_GROUP
X_ROW = D_MODEL + LANES
N_PROJ = 2 * D_RG + 2 * D_M + LANES
RG_ROWS = 1024
M_ROWS = 128
M_BLOCKS_PROMPT = 8
M_BLOCKS_SAMPLE = 2
V7X_VMEM_BYTES = 64 * 1024 * 1024
VMEM_LIMIT = V7X_VMEM_BYTES - 8 * 1024 * 1024


def _cparams(n_axes):
    return pltpu.CompilerParams(dimension_semantics=("arbitrary",) * n_axes,
                                vmem_limit_bytes=VMEM_LIMIT)


def _inproj_kernel(xp_ref, xs_ref, w_ref, op_ref, os_ref, *, n_p):
    i = pl.program_id(0)

    def run(x_ref, o_ref):
        o_ref[...] = jnp.dot(x_ref[...].astype(bf16), w_ref[...], preferred_element_type=f32)

    @pl.when(i < n_p)
    def _():
        run(xp_ref, op_ref)

    @pl.when(i >= n_p)
    def _():
        run(xs_ref, os_ref)


def _inproj(xp, xs, w):
    n_p, n_s = xp.shape[0] // TM, xs.shape[0] // TM
    pmap = lambda i: (jnp.minimum(i, n_p - 1), 0)
    smap = lambda i: (jnp.maximum(i - n_p, 0), 0)
    return pl.pallas_call(
        functools.partial(_inproj_kernel, n_p=n_p),
        grid=(n_p + n_s,),
        in_specs=[
            pl.BlockSpec((TM, D_MODEL), pmap),
            pl.BlockSpec((TM, D_MODEL), smap),
            pl.BlockSpec((D_MODEL, N_PROJ), lambda i: (0, 0)),
        ],
        out_specs=[pl.BlockSpec((TM, N_PROJ), pmap), pl.BlockSpec((TM, N_PROJ), smap)],
        out_shape=[jax.ShapeDtypeStruct((n_p * TM, N_PROJ), f32),
                   jax.ShapeDtypeStruct((n_s * TM, N_PROJ), f32)],
        compiler_params=_cparams(1),
        name="inproj",
    )(xp, xs, w)


def _causal_conv(tail_scr, x, cw_ref, cb_ref, S, L, C):
    tail = tail_scr[...]
    sub = lax.broadcasted_iota(jnp.int32, (S, SUBLANES, C), 1)
    acc = cb_ref[...] + cw_ref[CONV_W - 1:CONV_W, :] * x
    for d in range(1, CONV_W):
        back = pltpu.roll(x, d, 0).reshape(S, L, C)
        head = jnp.where(sub < d, pltpu.roll(tail, d, 1), back[:, 0:SUBLANES, :])
        if L > SUBLANES:
            back = jnp.concatenate([head, back[:, SUBLANES:, :]], axis=1)
        else:
            back = head
        acc = acc + cw_ref[CONV_W - 1 - d:CONV_W - d, :] * back.reshape(S * L, C)
    tail_scr[...] = x.reshape(S, L, C)[:, L - SUBLANES:, :]
    return acc


def _softplus(x):
    return jnp.maximum(x, 0.0) + jnp.log1p(jnp.exp(-jnp.abs(x)))


def _rglru_kernel(*refs, S, L, fresh):
    x_ref, g_ref = refs[:2]
    buf_ref, h0_ref = (None, None) if fresh else refs[2:4]
    (cw_ref, cb_ref, wa_ref, ba_ref, wx_ref, bx_ref, lam_ref,
     y_ref, hN_ref, cN_ref, xs_scr) = refs[2 if fresh else 4:]
    R = S * L
    t = pl.program_id(1)

    @pl.when(t == 0)
    def _():
        xs_scr[...] = jnp.zeros_like(xs_scr)
        if not fresh:
            xs_scr[:, SUBLANES - (CONV_W - 1):SUBLANES, :] = buf_ref[...]
        hN_ref[...] = jnp.zeros_like(hN_ref) if fresh else h0_ref[...]

    x = x_ref[...]
    xc = _causal_conv(xs_scr, x, cw_ref, cb_ref, S, L, D_RG)
    xcb = xc.astype(bf16)
    r = jax.nn.sigmoid(jnp.dot(xcb, wa_ref[...], preferred_element_type=f32) + ba_ref[...])
    ig = jax.nn.sigmoid(jnp.dot(xcb, wx_ref[...], preferred_element_type=f32) + bx_ref[...])
    log_a = (-RG_C) * r * _softplus(-lam_ref[...])
    a = jnp.exp(log_a)
    th = jnp.tanh(log_a)
    u = jnp.sqrt(-2.0 * th / (1.0 - th)) * ig * xc

    n_grp, grp_per_seq = R // SUBLANES, L // SUBLANES
    a3 = a.reshape(n_grp, SUBLANES, D_RG)
    u3 = u.reshape(n_grp, SUBLANES, D_RG)
    sub = lax.broadcasted_iota(jnp.int32, (n_grp, SUBLANES, D_RG), 1)
    s = 1
    while s < SUBLANES:
        ok = sub >= s
        a_sh = pltpu.roll(a3, s, 1)
        u_sh = pltpu.roll(u3, s, 1)
        u3 = jnp.where(ok, a3 * u_sh + u3, u3)
        a3 = jnp.where(ok, a3 * a_sh, a3)
        s *= 2
    h0 = hN_ref[...]
    groups = []
    for kg in range(n_grp):
        carry = h0[kg // grp_per_seq] if kg % grp_per_seq == 0 else groups[-1][SUBLANES - 1:SUBLANES, :]
        groups.append(a3[kg] * carry + u3[kg])
    h = jnp.concatenate(groups, axis=0)

    y_ref[...] = h * jax.nn.gelu(g_ref[...], approximate=True)
    hN_ref[...] = h.reshape(S, L, D_RG)[:, L - 1:L, :]
    cN_ref[...] = xs_scr[:, SUBLANES - (CONV_W - 1):SUBLANES, :]


def _rglru(proj, row0, B, T, S, L, state, cw, cb, wa, ba, wx, bx, lam):
    R = S * L
    nb, nt = B // S, T // L
    blk0 = row0 // R
    row_map = lambda b, t: (blk0 + b * nt + t, 0)
    const2 = lambda b, t: (0, 0)
    state_specs = [] if state is None else [pl.BlockSpec((S, CONV_W - 1, D_RG), lambda b, t: (b, 0, 0)),
                                            pl.BlockSpec((S, 1, D_RG), lambda b, t: (b, 0, 0))]
    return pl.pallas_call(
        functools.partial(_rglru_kernel, S=S, L=L, fresh=state is None),
        grid=(nb, nt),
        in_specs=[
            pl.BlockSpec((R, D_RG), row_map),
            pl.BlockSpec((R, D_RG), lambda b, t: (blk0 + b * nt + t, 1)),
        ] + state_specs + [
            pl.BlockSpec((CONV_W, D_RG), const2),
            pl.BlockSpec((1, D_RG), const2),
            pl.BlockSpec((D_RG, D_RG), const2),
            pl.BlockSpec((1, D_RG), const2),
            pl.BlockSpec((D_RG, D_RG), const2),
            pl.BlockSpec((1, D_RG), const2),
            pl.BlockSpec((1, D_RG), const2),
        ],
        out_specs=[
            pl.BlockSpec((R, D_RG), lambda b, t: (b * nt + t, 0)),
            pl.BlockSpec((S, 1, D_RG), lambda b, t: (b, 0, 0)),
            pl.BlockSpec((S, CONV_W - 1, D_RG), lambda b, t: (b, 0, 0)),
        ],
        out_shape=[
            jax.ShapeDtypeStruct((B * T, D_RG), f32),
            jax.ShapeDtypeStruct((B, 1, D_RG), f32),
            jax.ShapeDtypeStruct((B, CONV_W - 1, D_RG), f32),
        ],
        scratch_shapes=[pltpu.VMEM((S, SUBLANES, D_RG), f32)],
        compiler_params=_cparams(2),
        name="rglru",
    )(proj, proj, *(state or ()), cw, cb, wa, ba, wx, bx, lam)


def _seg_scan(x, op, fill, tin, L, reverse=False):
    s = 1
    while s < L:
        if reverse:
            sh = pltpu.roll(x, LANES - s, 1)
            ok = tin < L - s
        else:
            sh = pltpu.roll(x, s, 1)
            ok = tin >= s
        x = op(x, jnp.where(ok, sh, fill))
        s *= 2
    return x


def _mlstm_kernel(*refs, S, L, G, n_riders, fresh, single_chunk):
    x_ref, z_ref, gt_ref = refs[:3]
    n_state = 0 if fresh else 4
    state_refs = refs[3:3 + n_state]
    n_in = 11 + n_state
    cw_ref, cb_ref, wqk_ref, wv_ref, wkT_ref, bif_ref, ng_ref, sk_ref = refs[3 + n_state:n_in]
    rider_in = refs[n_in:n_in + n_riders]
    y_ref, cN_ref, nN_ref, mN_ref, bN_ref = refs[n_in + n_riders:n_in + n_riders + 5]
    rider_out = refs[n_in + n_riders + 5:n_in + 2 * n_riders + 5]
    xs_scr, m_scr = refs[n_in + 2 * n_riders + 5:]
    for src, dst in zip(rider_in, rider_out):
        dst[...] = src[...].astype(bf16)

    R = S * L
    GS, GR = G * S, G * R
    j = pl.program_id(1)

    @pl.when(j == 0)
    def _():
        if fresh:
            xs_scr[...] = jnp.zeros_like(xs_scr)
            cN_ref[...] = jnp.zeros_like(cN_ref)
            nN_ref[...] = jnp.zeros_like(nN_ref)
            m_scr[...] = jnp.full(m_scr.shape, M_INIT, f32)
        else:
            buf_ref, c0_ref, n0_ref, m0_ref = state_refs
            xs_scr[...] = jnp.zeros_like(xs_scr)
            xs_scr[:, SUBLANES - (CONV_W - 1):SUBLANES, :] = buf_ref[...]
            if not single_chunk:
                cN_ref[...] = c0_ref[...]
                nN_ref[...] = n0_ref[...]
            m_scr[...] = m0_ref[...]

    x = x_ref[...].reshape(GR, D_M)
    xc = _causal_conv(xs_scr, x, cw_ref, cb_ref, GS, L, D_M)
    xa = xc * jax.nn.sigmoid(xc)
    xab = xa.astype(bf16)
    xb = x.astype(bf16)

    il_parts, f_parts = [], []
    for g in range(G):
        gT = (gt_ref[g] + bif_ref[...]).T
        il_parts.append(gT[0:SUBLANES, :])
        f_parts.append(pltpu.roll(gT[0:SUBLANES, :], M_HEADS, 0))
    il = jnp.concatenate(il_parts, axis=0)
    fl = -_softplus(-jnp.concatenate(f_parts, axis=0))
    tin = lax.broadcasted_iota(jnp.int32, (G * SUBLANES, LANES), 1) % L
    bcum = _seg_scan(fl, jnp.add, 0.0, tin, L)
    a = il - bcum
    m_prev = m_scr[...].reshape(G * SUBLANES, LANES)
    big_m = jnp.maximum(m_prev, _seg_scan(a, jnp.maximum, -jnp.inf, tin, L))
    m_t = bcum + big_m
    if S == 1:
        m_last = jnp.broadcast_to(big_m[:, LANES - 1:LANES], big_m.shape)
    else:
        m_last = _seg_scan(big_m, jnp.maximum, -jnp.inf, tin, L, reverse=True)
    rows = [big_m, jnp.exp(m_prev - big_m), jnp.exp(-m_t), jnp.exp(a - m_last), jnp.exp(m_prev - m_last)]
    mN_ref[...] = m_t.reshape(G, SUBLANES, LANES)
    m_scr[...] = jnp.broadcast_to(m_t[:, LANES - 1:LANES], m_t.shape).reshape(G, SUBLANES, LANES)
    pad_rows = jnp.zeros((LANES - len(rows) * SUBLANES, LANES), f32)
    cols = []
    for g in range(G):
        gs = slice(g * SUBLANES, (g + 1) * SUBLANES)
        cols.append(jnp.concatenate([r[gs] for r in rows] + [pad_rows], axis=0).T)

    def col(g, q, h):
        return cols[g][:, SUBLANES * q + h:SUBLANES * q + h + 1]

    qs, ks, vs, kTs = [], [], [], []
    for h in range(M_HEADS):
        hs = slice(h * M_DH, (h + 1) * M_DH)
        qk_h = jnp.dot(xab[:, hs], wqk_ref[h], preferred_element_type=f32)
        qs.append(qk_h[:, 0:M_DH])
        ks.append(qk_h[:, M_DH:2 * M_DH] * (M_DH ** -0.5))
        vs.append(jnp.dot(xb[:, hs], wv_ref[h], preferred_element_type=f32))
        if S == 1:
            kT_h = lax.dot_general(wkT_ref[h], xab[:, hs], (((1,), (1,)), ((), ())),
                                   preferred_element_type=f32)
            kTs.append((kT_h * (M_DH ** -0.5)).astype(bf16))

    ti = lax.broadcasted_iota(jnp.int32, (R, R), 0)
    si = lax.broadcasted_iota(jnp.int32, (R, R), 1)
    mask = (si <= ti) & ((ti // L) == (si // L))
    ones_b = jnp.ones((R, M_DH), bf16)
    pairs = [(g, h) for g in range(G) for h in range(M_HEADS)]
    blk = lambda g: slice(g * R, (g + 1) * R)
    seqs = [slice(b * L, (b + 1) * L) for b in range(S)]
    q = {(g, h): qs[h][blk(g)] for g, h in pairs}
    k = {(g, h): ks[h][blk(g)] for g, h in pairs}
    v = {(g, h): vs[h][blk(g)] for g, h in pairs}
    qb = {p: q[p].astype(bf16) for p in pairs}
    kb = {p: k[p].astype(bf16) for p in pairs}
    qk = {p: lax.dot_general(qb[p], kb[p], (((1,), (1,)), ((), ())), preferred_element_type=f32)
          for p in pairs}
    sm = {}
    for g, h in pairs:
        a_row = a[g * SUBLANES + h:g * SUBLANES + h + 1, :]
        decay = jnp.exp(jnp.where(mask, a_row - col(g, 0, h), -jnp.inf))
        sm[g, h] = (qk[g, h] * decay).astype(bf16)
    nd = {p: jnp.dot(sm[p], jnp.concatenate([v[p].astype(bf16), ones_b], axis=1),
                     preferred_element_type=f32) for p in pairs}
    c_src, n_src = (state_refs[1], state_refs[2]) if single_chunk and not fresh else (cN_ref, nN_ref)
    c_old = {(g, h, b): c_src[g * S + b, h] for g, h in pairs for b in range(S)}
    n_old = {(g, h, b): n_src[g * S + b, h] for g, h in pairs for b in range(S)}
    q_c = {(g, h, b): jnp.dot(qb[g, h][seqs[b]], c_old[g, h, b].astype(bf16), preferred_element_type=f32)
           for g, h in pairs for b in range(S)}
    hh = {}
    for g, h in pairs:
        g_col, e_col = col(g, 1, h), col(g, 2, h)
        pieces = []
        for b, rs in enumerate(seqs):
            q_n = jnp.sum(q[g, h][rs] * n_old[g, h, b], axis=1, keepdims=True)
            num = nd[g, h][rs, 0:M_DH] + g_col[rs] * q_c[g, h, b]
            den = nd[g, h][rs, M_DH:2 * M_DH] + g_col[rs] * q_n
            pieces.append(num / jnp.maximum(jnp.abs(den), e_col[rs]))
        hh[g, h] = pieces[0] if S == 1 else jnp.concatenate(pieces, axis=0)
    mu = {p: jnp.mean(hh[p], axis=1, keepdims=True) for p in pairs}
    var = {p: jnp.mean(jnp.square(hh[p] - mu[p]), axis=1, keepdims=True) for p in pairs}
    hn_blocks = [jnp.concatenate([(hh[g, h] - mu[g, h]) * lax.rsqrt(var[g, h] + LN_EPS)
                                  for h in range(M_HEADS)], axis=1) for g in range(G)]
    new_c, new_n = [], []
    for g, h in pairs:
        w_col = col(g, 3, h)
        wv = (w_col * v[g, h]).astype(bf16)
        wk = w_col * k[g, h]
        for b, rs in enumerate(seqs):
            g_end = col(g, 4, h)[(b + 1) * L - 1:(b + 1) * L, :]
            if S == 1:
                kv = jnp.dot(kTs[h][:, blk(g)], wv, preferred_element_type=f32)
            else:
                kv = lax.dot_general(kb[g, h][rs], wv[rs], (((0,), (0,)), ((), ())),
                                     preferred_element_type=f32)
            new_c.append((g * S + b, h, g_end * c_old[g, h, b] + kv))
            new_n.append((g * S + b, h, g_end * n_old[g, h, b] + jnp.sum(wk[rs], axis=0, keepdims=True)))

    hn = jnp.concatenate(hn_blocks, axis=0) * ng_ref[...]
    y = jax.nn.sigmoid(z_ref[...].reshape(GR, D_M)) * (hn + sk_ref[...] * xa)
    y_ref[...] = y.reshape(G, R, D_M)
    for sq, h, val in new_c:
        cN_ref[sq, h] = val
    for sq, h, val in new_n:
        nN_ref[sq, h] = val
    bN_ref[...] = xs_scr[:, SUBLANES - (CONV_W - 1):SUBLANES, :]


def _mlstm(proj, B, T, S, L, G, state, cw, cb, wqk, wv, wkT, bif, ng, sk, riders=()):
    R = S * L
    assert R == M_ROWS == LANES
    nblk, nc = B // S, T // L
    nb = nblk // G
    assert nblk * S == B and nc * L == T and nb * G == nblk and (S == 1 or nc == 1)
    steps = nb * nc
    assert all(r.shape[0] % steps == 0 for r in riders)
    rider_specs = [pl.BlockSpec((r.shape[0] // steps,) + r.shape[1:], lambda b, j: (b * nc + j, 0, 0))
                   for r in riders]
    proj4 = proj.reshape(nblk, nc, R, N_PROJ)
    const2 = lambda b, j: (0, 0)
    const3 = lambda b, j: (0, 0, 0)
    gate_blk = (2 * D_RG + 2 * D_M) // LANES
    GS = G * S
    state_args, state_specs = (), []
    if state is not None:
        buf8, c0, n0, m0_lanes = state
        state_args = (buf8, c0, n0, m0_lanes.reshape(SUBLANES, nblk, LANES).transpose(1, 0, 2))
        state_specs = [
            pl.BlockSpec((GS, CONV_W - 1, D_M), lambda b, j: (b, 0, 0)),
            pl.BlockSpec((GS, M_HEADS, M_DH, M_DH), lambda b, j: (b, 0, 0, 0)),
            pl.BlockSpec((GS, M_HEADS, 1, M_DH), lambda b, j: (b, 0, 0, 0)),
            pl.BlockSpec((G, SUBLANES, LANES), lambda b, j: (b, 0, 0)),
        ]
    outs = pl.pallas_call(
        functools.partial(_mlstm_kernel, S=S, L=L, G=G, n_riders=len(riders), fresh=state is None,
                          single_chunk=nc == 1),
        grid=(nb, nc),
        in_specs=[
            pl.BlockSpec((G, None, R, D_M), lambda b, j: (b, j, 0, 2)),
            pl.BlockSpec((G, None, R, D_M), lambda b, j: (b, j, 0, 3)),
            pl.BlockSpec((G, None, R, LANES), lambda b, j: (b, j, 0, gate_blk)),
        ] + state_specs + [
            pl.BlockSpec((CONV_W, D_M), const2),
            pl.BlockSpec((1, D_M), const2),
            pl.BlockSpec((M_HEADS, M_DH, 2 * M_DH), const3),
            pl.BlockSpec((M_HEADS, M_DH, M_DH), const3),
            pl.BlockSpec((M_HEADS, M_DH, M_DH), const3),
            pl.BlockSpec((1, LANES), const2),
            pl.BlockSpec((1, D_M), const2),
            pl.BlockSpec((1, D_M), const2),
        ] + rider_specs,
        out_specs=[
            pl.BlockSpec((G, None, R, D_M), lambda b, j: (b, j, 0, 0)),
            pl.BlockSpec((GS, M_HEADS, M_DH, M_DH), lambda b, j: (b, 0, 0, 0)),
            pl.BlockSpec((GS, M_HEADS, 1, M_DH), lambda b, j: (b, 0, 0, 0)),
            pl.BlockSpec((G, None, SUBLANES, LANES), lambda b, j: (b, j, 0, 0)),
            pl.BlockSpec((GS, CONV_W - 1, D_M), lambda b, j: (b, 0, 0)),
        ] + rider_specs,
        out_shape=[
            jax.ShapeDtypeStruct((nblk, nc, R, D_M), f32),
            jax.ShapeDtypeStruct((B, M_HEADS, M_DH, M_DH), f32),
            jax.ShapeDtypeStruct((B, M_HEADS, 1, M_DH), f32),
            jax.ShapeDtypeStruct((nblk, nc, SUBLANES, LANES), f32),
            jax.ShapeDtypeStruct((B, CONV_W - 1, D_M), f32),
        ] + [jax.ShapeDtypeStruct(r.shape, bf16) for r in riders],
        scratch_shapes=[pltpu.VMEM((GS, SUBLANES, D_M), f32), pltpu.VMEM((G, SUBLANES, LANES), f32)],
        compiler_params=_cparams(2),
        name="mlstm",
    )(proj4, proj4, proj4, *state_args, cw, cb, wqk, wv, wkT, bif, ng, sk, *riders)
    y, c_new, n_new, m_t, b_new = outs[:5]
    m_last = m_t[:, nc - 1, :M_HEADS, :].reshape(nblk, M_HEADS, S, L)[:, :, :, L - 1]
    m_last = m_last.transpose(0, 2, 1).reshape(B, M_HEADS)
    return (y.reshape(B * T, D_M), c_new, n_new, m_last, b_new) + tuple(outs[5:])


def _layer_norm(x, g, b):
    mu = jnp.mean(x, axis=-1, keepdims=True)
    var = jnp.mean(jnp.square(x - mu), axis=-1, keepdims=True)
    return (x - mu) * lax.rsqrt(var + LN_EPS) * g + b


def _first_lane_of_max(vals, lane_f):
    vmax = jnp.max(vals, axis=1, keepdims=True)
    idx = jnp.min(jnp.where(vals == vmax, lane_f, float(LANES)), axis=1, keepdims=True)
    return vmax, idx


def _outproj_kernel(xp_ref, xs_ref, rgp_ref, rgs_ref, mp_ref, ms_ref, wo_ref, g1_ref, b1_ref,
                    wr_ref, br_ref, x1_ref, cnt_ref, *, n_p):
    i = pl.program_id(0)

    @pl.when(i == 0)
    def _():
        cnt_ref[...] = jnp.zeros_like(cnt_ref)

    def route(lg, lane, lane_f):
        neg = -jnp.inf
        gl = jnp.where(lane < N_GROUPS, lg, neg)
        gmax, gidx = _first_lane_of_max(gl, lane_f)
        p_g = 1.0 / jnp.sum(jnp.exp(gl - gmax), axis=1, keepdims=True)
        e_lo = float(N_GROUPS) + float(EXPERTS_PER_GROUP) * gidx
        el = jnp.where((lane_f >= e_lo) & (lane_f < e_lo + float(EXPERTS_PER_GROUP)), lg, neg)
        v1, i1 = _first_lane_of_max(el, lane_f)
        v2, i2 = _first_lane_of_max(jnp.where(lane_f == i1, neg, el), lane_f)
        d = jnp.exp(v2 - v1)
        w1 = p_g / (1.0 + d)
        w2 = p_g * d / (1.0 + d)
        first_low = i1 < i2
        j_lo = jnp.minimum(i1, i2) - e_lo
        j_hi = jnp.maximum(i1, i2) - e_lo
        n_first = float(EXPERTS_PER_GROUP - 1) * j_lo - 0.5 * j_lo * (j_lo - 1.0)
        cls = float(PAIRS_PER_GROUP) * gidx + n_first + (j_hi - j_lo - 1.0)
        return cls, jnp.where(first_low, w1, w2), jnp.where(first_low, w2, w1)

    def run(x_ref, rg_ref, m_ref):
        rp = TM_OUT // OUT_PARTS
        parts = [slice(p * rp, (p + 1) * rp) for p in range(OUT_PARTS)]
        lane = lax.broadcasted_iota(jnp.int32, (rp, LANES), 1)
        lane_f = lane.astype(f32)
        heads = [jnp.concatenate([rg_ref[r, :].astype(bf16), m_ref[r, :].astype(bf16)], axis=1) for r in parts]
        mix = [jnp.dot(h, wo_ref[...], preferred_element_type=f32) for h in heads]
        x1 = [_layer_norm(ALPHA * x_ref[r, :] + mx, g1_ref[...], b1_ref[...]) for r, mx in zip(parts, mix)]
        for r, v in zip(parts, x1):
            x1_ref[r, 0:D_MODEL] = v

        hi = [v.astype(bf16) for v in x1]
        lo = [(v - h.astype(f32)).astype(bf16) for v, h in zip(x1, hi)]
        hi_terms = [jnp.dot(h, wr_ref[...], preferred_element_type=f32) for h in hi]
        lo_term = [jnp.dot(v, wr_ref[:, 0:LANES], preferred_element_type=f32) for v in lo]
        lg = [ht[:, 0:LANES] + ht[:, LANES:2 * LANES] + lt + br_ref[...] for ht, lt in zip(hi_terms, lo_term)]
        routed = [route(v, lane, lane_f) for v in lg]

        ti = lax.broadcasted_iota(jnp.int32, (rp, rp), 0)
        si = lax.broadcasted_iota(jnp.int32, (rp, rp), 1)
        tri = (si <= ti).astype(bf16)
        hot = [lane_f == cls for cls, _, _ in routed]
        cum = [jnp.dot(tri, h.astype(bf16), preferred_element_type=f32) for h in hot]
        seen = cnt_ref[...]
        for r, h, cm, (cls, w_lo, w_hi) in zip(parts, hot, cum, routed):
            rank = jnp.sum(jnp.where(h, cm - 1.0 + seen, 0.0), axis=1, keepdims=True)
            seen = seen + cm[rp - 1:rp, :]
            info = jnp.zeros((rp, LANES), f32)
            for c, val in enumerate((cls, rank, w_lo, w_hi)):
                info = jnp.where(lane == c, val, info)
            x1_ref[r, D_MODEL:D_MODEL + LANES] = info
        cnt_ref[...] = seen

    @pl.when(i < n_p)
    def _():
        run(xp_ref, rgp_ref, mp_ref)

    @pl.when(i >= n_p)
    def _():
        run(xs_ref, rgs_ref, ms_ref)


def _outproj(xp, xs, rgp, rgs, mp, ms, wo, g1, b1, wr, br):
    n_p, n_s = xp.shape[0] // TM_OUT, xs.shape[0] // TM_OUT
    n = n_p + n_s
    pmap = lambda i: (jnp.minimum(i, n_p - 1), 0)
    smap = lambda i: (jnp.maximum(i - n_p, 0), 0)
    const = lambda i: (0, 0)
    return pl.pallas_call(
        functools.partial(_outproj_kernel, n_p=n_p),
        grid=(n,),
        in_specs=[
            pl.BlockSpec((TM_OUT, D_MODEL), pmap), pl.BlockSpec((TM_OUT, D_MODEL), smap),
            pl.BlockSpec((TM_OUT, D_RG), pmap), pl.BlockSpec((TM_OUT, D_RG), smap),
            pl.BlockSpec((TM_OUT, D_M), pmap), pl.BlockSpec((TM_OUT, D_M), smap),
            pl.BlockSpec((D_RG + D_M, D_MODEL), const),
            pl.BlockSpec((1, D_MODEL), const), pl.BlockSpec((1, D_MODEL), const),
            pl.BlockSpec((D_MODEL, 2 * LANES), const),
            pl.BlockSpec((1, LANES), const),
        ],
        out_specs=[
            pl.BlockSpec((TM_OUT, X_ROW), lambda i: (i, 0)),
            pl.BlockSpec((1, LANES), const),
        ],
        out_shape=[
            jax.ShapeDtypeStruct((n * TM_OUT, X_ROW), f32),
            jax.ShapeDtypeStruct((1, LANES), f32),
        ],
        compiler_params=_cparams(1),
        name="outproj",
    )(xp, xs, rgp, rgs, mp, ms, wo, g1, b1, wr, br)


GRP_MOVE = TM_MOVE // SUBLANES


def _grouped(x):
    return x.reshape(x.shape[0] // SUBLANES, SUBLANES, x.shape[1])


def _tile_rows_wait(hbm_ref, sem):
    rows = hbm_ref.at[pl.ds(0, TM_MOVE)]
    pltpu.make_async_copy(rows, rows, sem).wait()


def _for_each_tile_row(body):
    def group(k, c):
        for j in range(SUBLANES):
            body(k, j, k * SUBLANES + j)
        return c

    lax.fori_loop(0, GRP_MOVE, group, 0)


def _dispatch_kernel(pad_ref, pos_ref, x1_ref, xs_ref, zero_scr, xbuf, sem, zsem, tsem, lsem):
    i = pl.program_id(0)
    n = pl.num_programs(0)
    start, wait = (lambda cp: cp.start()), (lambda cp: cp.wait())

    def load(t, s):
        return pltpu.make_async_copy(x1_ref.at[pl.ds(t * GRP_MOVE, GRP_MOVE)], xbuf.at[s], lsem.at[s])

    def zero_chunk(row0):
        return pltpu.make_async_copy(zero_scr.at[pl.ds(0, SUBLANES)],
                                     xs_ref.at[pl.ds(pl.multiple_of(row0, SUBLANES), SUBLANES)], zsem)

    def tail_tiles(go):
        def zero_tile(t, c):
            rows = pl.ds(pl.multiple_of(t * TM_MOE, TM_MOE), TM_MOE)
            go(pltpu.make_async_copy(zero_scr, xs_ref.at[rows], tsem))
            return c

        lax.fori_loop(pad_ref[2 * N_CLASSES], xs_ref.shape[0] // TM_MOE, zero_tile, 0)

    @pl.when(i == 0)
    def _():
        load(0, 0).start()
        zero_scr[...] = jnp.zeros_like(zero_scr)
        tail_tiles(start)
        for go in (start, wait):
            def per_class(cl, c):
                row0 = pad_ref[2 * cl]
                lax.fori_loop(0, pad_ref[2 * cl + 1],
                              lambda k, c2: (go(zero_chunk(row0 + k * SUBLANES)), c2)[1], 0)
                return c

            lax.fori_loop(0, N_CLASSES, per_class, 0)

    @pl.when(i == n - 1)
    def _():
        tail_tiles(wait)

    @pl.when(i + 1 < n)
    def _():
        load(i + 1, (i + 1) % 3).start()

    slot = i % 3
    load(i, slot).wait()

    def start(k, j, r):
        pltpu.make_async_copy(xbuf.at[slot, k, pl.ds(j, 1)], xs_ref.at[pl.ds(pos_ref[0, 0, r], 1)],
                              sem.at[slot]).start()

    _for_each_tile_row(start)

    @pl.when(i > 0)
    def _():
        _tile_rows_wait(xs_ref, sem.at[(i + 2) % 3])

    @pl.when(i == n - 1)
    def _():
        _tile_rows_wait(xs_ref, sem.at[slot])


def _dispatch(pad_plan, pos, x1, n_rows):
    n = x1.shape[0] // TM_MOVE
    return pl.pallas_call(
        _dispatch_kernel,
        grid_spec=pltpu.PrefetchScalarGridSpec(
            num_scalar_prefetch=1,
            grid=(n,),
            in_specs=[
                pl.BlockSpec((1, 1, TM_MOVE), lambda i, *_: (i, 0, 0), memory_space=pltpu.SMEM),
                pl.BlockSpec(memory_space=pl.ANY),
            ],
            out_specs=pl.BlockSpec(memory_space=pl.ANY),
            scratch_shapes=[pltpu.VMEM((TM_MOE, X_ROW), f32),
                            pltpu.VMEM((3, GRP_MOVE, SUBLANES, X_ROW), f32),
                            pltpu.SemaphoreType.DMA((3,)), pltpu.SemaphoreType.DMA(()),
                            pltpu.SemaphoreType.DMA(()), pltpu.SemaphoreType.DMA((3,))],
        ),
        out_shape=jax.ShapeDtypeStruct((n_rows, X_ROW), f32),
        compiler_params=_cparams(1),
        name="dispatch",
    )(pad_plan, pos, _grouped(x1))


def _expert_kernel(ta_ref, tb_ref, ca_ref, cb_ref, sa_ref, sb_ref, nu_ref,
                   x_ref, wg_hbm, wu_hbm, wd_hbm, g2_ref, b2_ref, y_ref,
                   wga, wua, wda, wgb, wub, wdb, sem_a, sem_b):
    i = pl.program_id(0)
    n_used = nu_ref[0]
    first = i * MOE_TILES
    side_a = (ta_ref, ca_ref, sa_ref, (wga, wua, wda), sem_a)
    side_b = (tb_ref, cb_ref, sb_ref, (wgb, wub, wdb), sem_b)

    def copies(side, j):
        t_ref, _, s_ref, bufs, sem = side
        e, s = t_ref[j], s_ref[j]
        return [pltpu.make_async_copy(w.at[e], buf.at[s], sem.at[s])
                for w, buf in zip((wg_hbm, wu_hbm, wd_hbm), bufs)]

    def fetch(j):
        @pl.when(j < n_used)
        def _():
            for side in (side_a, side_b):
                @pl.when(side[1][j] == 1)
                def _():
                    for cp in copies(side, j):
                        cp.start()

    @pl.when(i == 0)
    def _():
        for j in range(W_AHEAD):
            fetch(j)

    for t in range(MOE_TILES):
        fetch(first + W_AHEAD + t)

    @pl.when(first < n_used)
    def _():
        for t in range(MOE_TILES):
            for side in (side_a, side_b):
                @pl.when(side[1][first + t] == 1)
                def _():
                    for cp in copies(side, first + t):
                        cp.wait()
        rows = [slice(t * TM_MOE, (t + 1) * TM_MOE) for t in range(MOE_TILES)]
        x = [x_ref[r, 0:D_MODEL] for r in rows]
        info = [x_ref[r, D_MODEL:X_ROW] for r in rows]
        xb = [v.astype(bf16) for v in x]
        units = [(t, bufs, s_ref[first + t]) for t in range(MOE_TILES)
                 for bufs, s_ref in (((wga, wua, wda), sa_ref), ((wgb, wub, wdb), sb_ref))]
        hg = [jnp.dot(xb[t], bufs[0][s], preferred_element_type=f32) for t, bufs, s in units]
        hu = [jnp.dot(xb[t], bufs[1][s], preferred_element_type=f32) for t, bufs, s in units]
        mid = [(g * jax.nn.sigmoid(g) * u).astype(bf16) for g, u in zip(hg, hu)]
        ys = [jnp.dot(m, bufs[2][s], preferred_element_type=f32)
              for m, (t, bufs, s) in zip(mid, units)]
        for t in range(MOE_TILES):
            ffn = info[t][:, 2:3] * ys[2 * t] + info[t][:, 3:4] * ys[2 * t + 1]
            y_ref[rows[t], :] = _layer_norm(ALPHA * x[t] + ffn, g2_ref[...], b2_ref[...])

    @pl.when(first >= n_used)
    def _():
        y_ref[...] = jnp.zeros_like(y_ref)


def _experts(tile_ea, tile_eb, n_used, xs, wg, wu, wd, g2, b2):
    nt = xs.shape[0] // TM_MOE

    def ring_plan(tile_e):
        opens = jnp.concatenate([jnp.ones((1,), jnp.int32), (tile_e[1:] != tile_e[:-1]).astype(jnp.int32)])
        return opens, (jnp.cumsum(opens) - 1) % W_SLOTS

    open_a, slot_a = ring_plan(tile_ea)
    open_b, slot_b = ring_plan(tile_eb)
    assert nt % MOE_TILES == 0
    n_used = ((n_used + MOE_TILES - 1) // MOE_TILES) * MOE_TILES
    const = lambda i, *_: (0, 0)
    xmap = lambda i, ta, tb, ca, cb, sa, sb, nu: (jnp.minimum(i, nu[0] // MOE_TILES - 1), 0)
    w_in = pltpu.VMEM((W_SLOTS, D_MODEL, D_EXPERT), bf16)
    w_out = pltpu.VMEM((W_SLOTS, D_EXPERT, D_MODEL), bf16)
    return pl.pallas_call(
        _expert_kernel,
        grid_spec=pltpu.PrefetchScalarGridSpec(
            num_scalar_prefetch=7,
            grid=(nt // MOE_TILES,),
            in_specs=[
                pl.BlockSpec((MOE_TILES * TM_MOE, X_ROW), xmap),
                pl.BlockSpec(memory_space=pl.ANY), pl.BlockSpec(memory_space=pl.ANY),
                pl.BlockSpec(memory_space=pl.ANY),
                pl.BlockSpec((1, D_MODEL), const), pl.BlockSpec((1, D_MODEL), const),
            ],
            out_specs=pl.BlockSpec((MOE_TILES * TM_MOE, D_MODEL), lambda i, *_: (i, 0)),
            scratch_shapes=[w_in, w_in, w_out, w_in, w_in, w_out,
                            pltpu.SemaphoreType.DMA((W_SLOTS,)), pltpu.SemaphoreType.DMA((W_SLOTS,))],
        ),
        out_shape=jax.ShapeDtypeStruct((nt * TM_MOE, D_MODEL), f32),
        compiler_params=_cparams(1),
        name="experts",
    )(tile_ea, tile_eb, open_a, open_b, slot_a.astype(jnp.int32), slot_b.astype(jnp.int32), n_used,
      xs, wg, wu, wd, g2, b2)


def _collect_kernel(pos_ref, posn_ref, ys_ref, op_ref, os_ref, ybuf, gsem, wsem, *, n_p):
    i = pl.program_id(0)
    n = pl.num_programs(0)
    slot = i % 3

    def gather(p_ref, s):
        def start(k, j, r):
            pltpu.make_async_copy(ys_ref.at[pl.ds(p_ref[0, 0, r], 1)], ybuf.at[s, k, pl.ds(j, 1)],
                                  gsem.at[s]).start()

        _for_each_tile_row(start)

    def write_out(s, out_ref, tile):
        return pltpu.make_async_copy(ybuf.at[s], out_ref.at[pl.ds(tile * GRP_MOVE, GRP_MOVE)], wsem.at[s])

    @pl.when(i == 0)
    def _():
        gather(pos_ref, 0)

    @pl.when(i >= 2)
    def _():
        write_out((i + 1) % 3, op_ref, 0).wait()

    @pl.when(i + 1 < n)
    def _():
        gather(posn_ref, (i + 1) % 3)

    _tile_rows_wait(ys_ref, gsem.at[slot])

    @pl.when(i < n_p)
    def _():
        write_out(slot, op_ref, i).start()

    @pl.when(i >= n_p)
    def _():
        write_out(slot, os_ref, i - n_p).start()

    @pl.when(i == n - 1)
    def _():
        write_out(slot, op_ref, 0).wait()

        @pl.when(n >= 2)
        def _():
            write_out((i + 2) % 3, op_ref, 0).wait()


def _collect(pos, ys, n_p):
    n = pos.shape[0]
    n_s = n - n_p
    outs = pl.pallas_call(
        functools.partial(_collect_kernel, n_p=n_p),
        grid=(n,),
        in_specs=[
            pl.BlockSpec((1, 1, TM_MOVE), lambda i: (i, 0, 0), memory_space=pltpu.SMEM),
            pl.BlockSpec((1, 1, TM_MOVE), lambda i: (jnp.minimum(i + 1, n - 1), 0, 0),
                         memory_space=pltpu.SMEM),
            pl.BlockSpec(memory_space=pl.ANY),
        ],
        out_specs=[pl.BlockSpec(memory_space=pl.ANY), pl.BlockSpec(memory_space=pl.ANY)],
        out_shape=[
            jax.ShapeDtypeStruct((n_p * GRP_MOVE, SUBLANES, D_MODEL), f32),
            jax.ShapeDtypeStruct((n_s * GRP_MOVE, SUBLANES, D_MODEL), f32),
        ],
        scratch_shapes=[pltpu.VMEM((3, GRP_MOVE, SUBLANES, D_MODEL), f32),
                        pltpu.SemaphoreType.DMA((3,)), pltpu.SemaphoreType.DMA((3,))],
        compiler_params=_cparams(1),
        name="collect",
    )(pos, pos, ys)
    return outs[0].reshape(n_p * TM_MOVE, D_MODEL), outs[1].reshape(n_s * TM_MOVE, D_MODEL)


def _block_diag(w):
    n, d, _ = w.shape
    eye = jnp.eye(n, dtype=w.dtype)
    return (eye[:, None, :, None] * w[:, :, None, :]).reshape(n * d, n * d)


def _pair_walk(n):
    total = n * (n - 1) // 2

    def extend(path, used):
        if len(path) == total:
            return path
        a, b = path[-1]
        for c in range(n):
            for nxt in ((a, c), (c, b)):
                key = frozenset(nxt)
                if len(key) == 2 and key not in used:
                    out = extend(path + [nxt], used | {key})
                    if out:
                        return out
        return None

    return extend([(0, 1)], {frozenset((0, 1))})


def kernel(x_prompt, x_sample, state_rg_h, state_rg_conv, state_m_C, state_m_n, state_m_m, state_m_conv, w_in, rg_conv_w, rg_conv_b, rg_w_a, rg_b_a, rg_w_x, rg_b_x, rg_lambda, m_conv_w, m_conv_b, m_w_q, m_w_k, m_w_v, m_b_i, m_b_f, m_norm_g, m_skip, w_out, ln1_g, ln1_b, ln2_g, ln2_b, moe_w_group, moe_b_group, moe_w_expert, moe_b_expert, moe_w_gate, moe_w_up, moe_w_down):
    BP, TP, _ = x_prompt.shape
    BS, TS, _ = x_sample.shape
    n_prompt, n_sample = BP * TP, BS * TS
    xp = x_prompt.reshape(n_prompt, D_MODEL)
    xs = x_sample.reshape(n_sample, D_MODEL)
    l = 0

    w_in_p = jnp.pad(w_in[l], ((0, 0), (0, N_PROJ - w_in.shape[-1]))).astype(bf16)
    wa = _block_diag(rg_w_a[l]).astype(bf16)
    wx = _block_diag(rg_w_x[l]).astype(bf16)
    row = lambda v: v.reshape(1, -1)
    bif = jnp.pad(jnp.concatenate([m_b_i[l], m_b_f[l]]), (0, LANES - 2 * M_HEADS)).reshape(1, LANES)
    wqk = jnp.concatenate([m_w_q[l], m_w_k[l]], axis=-1).astype(bf16)
    wv = m_w_v[l].astype(bf16)
    wkT = m_w_k[l].transpose(0, 2, 1).astype(bf16)
    w_route = jnp.pad(jnp.concatenate([moe_w_group[l], moe_w_expert[l]], axis=1),
                      ((0, 0), (0, LANES - N_GROUPS - N_EXPERTS)))
    wr_hi = w_route.astype(bf16)
    wr_lo = (w_route - wr_hi.astype(f32)).astype(bf16)
    wr = jnp.concatenate([wr_hi, wr_lo], axis=1)
    b_route = jnp.pad(jnp.concatenate([moe_b_group[l], moe_b_expert[l]]),
                      (0, LANES - N_GROUPS - N_EXPERTS)).reshape(1, LANES)

    proj_p, proj_s = _inproj(xp, xs, w_in_p)

    rg_args = (rg_conv_w[l], row(rg_conv_b[l]), wa, row(rg_b_a[l]), wx, row(rg_b_x[l]), row(rg_lambda[l]))
    s_blk = M_ROWS // TS
    yrg_p, p_rg_h, p_rg_conv = _rglru(proj_p, 0, BP, TP, 1, RG_ROWS, None, *rg_args)
    rg_state = (state_rg_conv[l], state_rg_h[l].reshape(BS, 1, D_RG))
    yrg_s, s_rg_h, s_rg_conv = _rglru(proj_s, 0, BS, TS, RG_ROWS // TS, TS, rg_state, *rg_args)

    m_args = (m_conv_w[l], row(m_conv_b[l]), wqk, wv, wkT, bif, row(m_norm_g[l]), row(m_skip[l]))
    ym_p, p_m_C, p_m_n, p_m_m, p_m_conv, wg_b, wu_b, wd_b = _mlstm(
        proj_p, BP, TP, 1, M_ROWS, M_BLOCKS_PROMPT, None, *m_args,
        riders=(moe_w_gate[l], moe_w_up[l], moe_w_down[l]))
    m0_s = jnp.pad(jnp.repeat(state_m_m[l].T, TS, axis=1), ((0, SUBLANES - M_HEADS), (0, 0)))
    m_state = (state_m_conv[l], state_m_C[l], state_m_n[l].reshape(BS, M_HEADS, 1, M_DH), m0_s)
    ym_s, s_m_C, s_m_n, s_m_m, s_m_conv = _mlstm(proj_s, BS, TS, s_blk, TS, M_BLOCKS_SAMPLE, m_state, *m_args)

    x1, cnt = _outproj(xp, xs, yrg_p, yrg_s, ym_p, ym_s, w_out[l].astype(bf16),
                       row(ln1_g[l]), row(ln1_b[l]), wr, b_route)

    n_tok = n_prompt + n_sample
    n_tiles = n_tok // TM_MOE + N_CLASSES
    lay_cls, lay_a, lay_b = [], [], []
    for g in range(N_GROUPS):
        for ja, jb in _pair_walk(EXPERTS_PER_GROUP):
            assert ja < jb
            n_first = (EXPERTS_PER_GROUP - 1) * ja - ja * (ja - 1) // 2
            lay_cls.append(PAIRS_PER_GROUP * g + n_first + jb - ja - 1)
            lay_a.append(g * EXPERTS_PER_GROUP + ja)
            lay_b.append(g * EXPERTS_PER_GROUP + jb)
    classes = jnp.arange(N_CLASSES, dtype=jnp.int32)
    in_slot = jnp.array(lay_cls, jnp.int32)[:, None] == classes
    counts = jnp.sum(jnp.where(in_slot, cnt[0, :N_CLASSES].astype(jnp.int32), 0), axis=1)
    padded = ((counts + TM_MOE - 1) // TM_MOE) * TM_MOE
    ends = jnp.cumsum(padded)
    offs = ends - padded
    offs_of_cls = jnp.sum(jnp.where(in_slot, offs[:, None], 0), axis=0)
    cls = x1[:, D_MODEL].astype(jnp.int32)
    rank = x1[:, D_MODEL + 1].astype(jnp.int32)
    pos = jnp.sum(jnp.where(cls[:, None] == classes, offs_of_cls, 0), axis=-1) + rank
    pos = pos.reshape(n_tok // TM_MOVE, 1, TM_MOVE)
    tiles = jnp.arange(n_tiles, dtype=jnp.int32)
    tile_slot = jnp.minimum(jnp.sum(tiles[:, None] >= (ends // TM_MOE)[None, :], axis=1), N_CLASSES - 1)
    on_slot = tile_slot[:, None] == classes
    tile_ea = jnp.sum(jnp.where(on_slot, jnp.array(lay_a, jnp.int32), 0), axis=-1).astype(jnp.int32)
    tile_eb = jnp.sum(jnp.where(on_slot, jnp.array(lay_b, jnp.int32), 0), axis=-1).astype(jnp.int32)
    n_used = (ends[-1] // TM_MOE).reshape(1).astype(jnp.int32)
    pad_row0 = ((offs + counts) // SUBLANES) * SUBLANES
    pad_plan = jnp.concatenate([jnp.stack([pad_row0, (ends - pad_row0) // SUBLANES], axis=1).reshape(-1),
                                     n_used]).astype(jnp.int32)

    x_sorted = _dispatch(pad_plan, pos, x1, n_tiles * TM_MOE)
    y_sorted = _experts(tile_ea, tile_eb, n_used, x_sorted, wg_b, wu_b, wd_b,
                        row(ln2_g[l]), row(ln2_b[l]))
    y_p, y_s = _collect(pos, y_sorted, n_prompt // TM_MOVE)

    return (y_p.reshape(BP, TP, D_MODEL), y_s.reshape(BS, TS, D_MODEL),
            p_rg_h.reshape(1, BP, D_RG), p_rg_conv[None], p_m_C[None], p_m_n.reshape(1, BP, M_HEADS, M_DH),
            p_m_m[None], p_m_conv[None],
            s_rg_h.reshape(1, BS, D_RG), s_rg_conv[None], s_m_C[None], s_m_n.reshape(1, BS, M_HEADS, M_DH),
            s_m_m[None], s_m_conv[None])
```
